```python
import math
import jax, jax.numpy as jnp
from jax import lax
import numpy as np

D_MODEL = 1024
BATCH = 8
SEQ = 8192
DEPTH = 1

D_MIX = D_MODEL
D_POOL = D_MIX // 2
POOL_WINDOWS = (2, 4, 8, 16)
N_POOL_GROUPS = len(POOL_WINDOWS)
POOL_GROUP = D_POOL // N_POOL_GROUPS
D_GMLP = D_MIX - D_POOL
N_GMLP_HEADS = 8
GMLP_HEAD = D_GMLP // N_GMLP_HEADS
CHUNK = 128
D_FF = 2816
N_SUB = 3
N_MOD = 3
EPS = 1e-6
HALF_STEP = 0.5

kernel_name = "hybrid_pool_gmlp_macaron_adaln"


def rms_norm(x, g):
    xf = x.astype(jnp.float32)
    r = lax.rsqrt(jnp.mean(xf * xf, axis=-1, keepdims=True) + EPS)
    return (xf * r).astype(x.dtype) * g


def layer_norm(x, g, b):
    xf = x.astype(jnp.float32)
    mu = jnp.mean(xf, axis=-1, keepdims=True)
    var = jnp.mean(jnp.square(xf - mu), axis=-1, keepdims=True)
    return ((xf - mu) * lax.rsqrt(var + EPS)).astype(x.dtype) * g + b


def modulate(h, shift, scale):
    return h * (1.0 + scale[:, None, :]) + shift[:, None, :]


def swiglu(h, w_in, w_out):
    gu = h @ w_in
    g, u = jnp.split(gu, 2, axis=-1)
    return (jax.nn.silu(g) * u) @ w_out


def causal_multiscale_pool(xp, w_pool, pool_scale):
    B, S, _ = xp.shape
    xg = xp.reshape(B, S, N_POOL_GROUPS, POOL_GROUP)
    xf = xg.astype(jnp.float32)
    cs = jnp.cumsum(xf, axis=1)
    pos = jnp.arange(S, dtype=jnp.int32)
    outs = []
    for i, w in enumerate(POOL_WINDOWS):
        c_i = cs[:, :, i]
        c_prev = jnp.pad(c_i[:, :-w], ((0, 0), (w, 0), (0, 0)))
        cnt = jnp.minimum(pos + 1, w).astype(jnp.float32)[None, :, None]
        outs.append((c_i - c_prev) / cnt - xf[:, :, i])
    pooled = jnp.stack(outs, axis=2).astype(xp.dtype)
    mixed = jnp.einsum('bsgc,gcd->bsgd', pooled, w_pool)
    return mixed.reshape(B, S, D_POOL) * pool_scale


def chunked_spatial_gating(zb, ln_g, ln_b, w_spatial, b_spatial):
    B, S, _ = zb.shape
    z = jax.nn.gelu(zb)
    u, v = jnp.split(z, 2, axis=-1)
    v = layer_norm(v, ln_g, ln_b)
    vc = v.reshape(B, S // CHUNK, CHUNK, N_GMLP_HEADS, GMLP_HEAD)
    mask = jnp.tril(jnp.ones((CHUNK, CHUNK), dtype=bool))
    ws = jnp.where(mask[None], w_spatial, jnp.zeros((), w_spatial.dtype))
    sv = jnp.einsum('hts,bnshc->bnthc', ws, vc)
    sv = sv + jnp.transpose(b_spatial)[None, None, :, :, None]
    return u * sv.reshape(B, S, D_GMLP)


def _fwd_setup_inputs(seed: int = 0) -> dict:
    key = jax.random.key(seed)
    ks = jax.random.split(key, 24)
    f32 = jnp.float32
    L, D = DEPTH, D_MODEL

    def nrm(k, shape, fan_in):
        return jax.random.normal(k, shape, f32) * (fan_in ** -0.5)

    def gain(k, shape):
        return 1.0 + 0.05 * jax.random.normal(k, shape, f32)

    return {
        "x": jax.random.normal(ks[0], (BATCH, SEQ, D), f32),
        "c": jax.random.normal(ks[1], (BATCH, D), f32),
        "w_ada": nrm(ks[2], (L, D, N_SUB * N_MOD * D), D),
        "b_ada": 0.02 * jax.random.normal(ks[3], (L, N_SUB * N_MOD * D), f32),
        "norm_ffn1_g": gain(ks[4], (L, D)),
        "ffn1_w_in": nrm(ks[5], (L, D, 2 * D_FF), D),
        "ffn1_w_out": nrm(ks[6], (L, D_FF, D), D_FF),
        "norm_mix_g": gain(ks[7], (L, D)),
        "w_mix_in": nrm(ks[8], (L, D, D_POOL + 2 * D_GMLP), D),
        "w_pool": nrm(ks[9], (L, N_POOL_GROUPS, POOL_GROUP, POOL_GROUP), POOL_GROUP),
        "pool_scale": gain(ks[10], (L, D_POOL)),
        "gmlp_ln_g": gain(ks[11], (L, D_GMLP)),
        "gmlp_ln_b": 0.02 * jax.random.normal(ks[12], (L, D_GMLP), f32),
        "w_spatial": nrm(ks[13], (L, N_GMLP_HEADS, CHUNK, CHUNK), CHUNK),
        "b_spatial": gain(ks[14], (L, N_GMLP_HEADS, CHUNK)),
        "w_mix_out": nrm(ks[15], (L, D_MIX, D), D_MIX),
        "norm_ffn2_g": gain(ks[16], (L, D)),
        "ffn2_w_in": nrm(ks[17], (L, D, 2 * D_FF), D),
        "ffn2_w_out": nrm(ks[18], (L, D_FF, D), D_FF),
        "norm_final_g": gain(ks[19], (D,)),
    }


def _fwd_reference(x, c, w_ada, b_ada, norm_ffn1_g, ffn1_w_in, ffn1_w_out,
              norm_mix_g, w_mix_in, w_pool, pool_scale, gmlp_ln_g, gmlp_ln_b,
              w_spatial, b_spatial, w_mix_out, norm_ffn2_g, ffn2_w_in, ffn2_w_out,
              norm_final_g):
    B = x.shape[0]
    c_act = jax.nn.silu(c)
    for l in range(DEPTH):
        mod = (c_act @ w_ada[l] + b_ada[l]).reshape(B, N_SUB, N_MOD, D_MODEL)

        h = modulate(rms_norm(x, norm_ffn1_g[l]), mod[:, 0, 0], mod[:, 0, 1])
        x = x + HALF_STEP * mod[:, 0, 2][:, None, :] * swiglu(h, ffn1_w_in[l], ffn1_w_out[l])

        h = modulate(rms_norm(x, norm_mix_g[l]), mod[:, 1, 0], mod[:, 1, 1])
        proj = h @ w_mix_in[l]
        xa = proj[..., :D_POOL]
        zb = proj[..., D_POOL:]
        ya = causal_multiscale_pool(xa, w_pool[l], pool_scale[l])
        yb = chunked_spatial_gating(zb, gmlp_ln_g[l], gmlp_ln_b[l], w_spatial[l], b_spatial[l])
        mix = jnp.concatenate([ya, yb], axis=-1) @ w_mix_out[l]
        x = x + mod[:, 1, 2][:, None, :] * mix

        h = modulate(rms_norm(x, norm_ffn2_g[l]), mod[:, 2, 0], mod[:, 2, 1])
        x = x + HALF_STEP * mod[:, 2, 2][:, None, :] * swiglu(h, ffn2_w_in[l], ffn2_w_out[l])
    return rms_norm(x, norm_final_g)


import jax as _jax
import jax.numpy as _jnp

TWIN_FORMAT = 'train_step'
FWD_PARAMS = ['x', 'c', 'w_ada', 'b_ada', 'norm_ffn1_g', 'ffn1_w_in', 'ffn1_w_out', 'norm_mix_g', 'w_mix_in', 'w_pool', 'pool_scale', 'gmlp_ln_g', 'gmlp_ln_b', 'w_spatial', 'b_spatial', 'w_mix_out', 'norm_ffn2_g', 'ffn2_w_in', 'ffn2_w_out', 'norm_final_g']
TWIN_WEIGHTS = ['w_ada', 'b_ada', 'norm_ffn1_g', 'ffn1_w_in', 'ffn1_w_out', 'norm_mix_g', 'w_mix_in', 'w_pool', 'pool_scale', 'gmlp_ln_g', 'gmlp_ln_b', 'w_spatial', 'b_spatial', 'w_mix_out', 'norm_ffn2_g', 'ffn2_w_in', 'ffn2_w_out', 'norm_final_g']
TWIN_DIFF_INPUT = 'x'
TWIN_INPUTS = ['x', 'c', 'w_ada', 'b_ada', 'norm_ffn1_g', 'ffn1_w_in', 'ffn1_w_out', 'norm_mix_g', 'w_mix_in', 'w_pool', 'pool_scale', 'gmlp_ln_g', 'gmlp_ln_b', 'w_spatial', 'b_spatial', 'w_mix_out', 'norm_ffn2_g', 'ffn2_w_in', 'ffn2_w_out', 'norm_final_g', 'loss_target', 'm_w_ada', 'm_b_ada', 'm_norm_ffn1_g', 'm_ffn1_w_in', 'm_ffn1_w_out', 'm_norm_mix_g', 'm_w_mix_in', 'm_w_pool', 'm_pool_scale', 'm_gmlp_ln_g', 'm_gmlp_ln_b', 'm_w_spatial', 'm_b_spatial', 'm_w_mix_out', 'm_norm_ffn2_g', 'm_ffn2_w_in', 'm_ffn2_w_out', 'm_norm_final_g', 'v_w_ada', 'v_b_ada', 'v_norm_ffn1_g', 'v_ffn1_w_in', 'v_ffn1_w_out', 'v_norm_mix_g', 'v_w_mix_in', 'v_w_pool', 'v_pool_scale', 'v_gmlp_ln_g', 'v_gmlp_ln_b', 'v_w_spatial', 'v_b_spatial', 'v_w_mix_out', 'v_norm_ffn2_g', 'v_ffn2_w_in', 'v_ffn2_w_out', 'v_norm_final_g']
TWIN_OUTPUTS = ['loss', 'grad_x', 'grad_w_ada', 'grad_b_ada', 'grad_norm_ffn1_g', 'grad_ffn1_w_in', 'grad_ffn1_w_out', 'grad_norm_mix_g', 'grad_w_mix_in', 'grad_w_pool', 'grad_pool_scale', 'grad_gmlp_ln_g', 'grad_gmlp_ln_b', 'grad_w_spatial', 'grad_b_spatial', 'grad_w_mix_out', 'grad_norm_ffn2_g', 'grad_ffn2_w_in', 'grad_ffn2_w_out', 'grad_norm_final_g', 'delta_w_ada', 'delta_b_ada', 'delta_norm_ffn1_g', 'delta_ffn1_w_in', 'delta_ffn1_w_out', 'delta_norm_mix_g', 'delta_w_mix_in', 'delta_w_pool', 'delta_pool_scale', 'delta_gmlp_ln_g', 'delta_gmlp_ln_b', 'delta_w_spatial', 'delta_b_spatial', 'delta_w_mix_out', 'delta_norm_ffn2_g', 'delta_ffn2_w_in', 'delta_ffn2_w_out', 'delta_norm_final_g', 'new_m_w_ada', 'new_m_b_ada', 'new_m_norm_ffn1_g', 'new_m_ffn1_w_in', 'new_m_ffn1_w_out', 'new_m_norm_mix_g', 'new_m_w_mix_in', 'new_m_w_pool', 'new_m_pool_scale', 'new_m_gmlp_ln_g', 'new_m_gmlp_ln_b', 'new_m_w_spatial', 'new_m_b_spatial', 'new_m_w_mix_out', 'new_m_norm_ffn2_g', 'new_m_ffn2_w_in', 'new_m_ffn2_w_out', 'new_m_norm_final_g', 'new_v_w_ada', 'new_v_b_ada', 'new_v_norm_ffn1_g', 'new_v_ffn1_w_in', 'new_v_ffn1_w_out', 'new_v_norm_mix_g', 'new_v_w_mix_in', 'new_v_w_pool', 'new_v_pool_scale', 'new_v_gmlp_ln_g', 'new_v_gmlp_ln_b', 'new_v_w_spatial', 'new_v_b_spatial', 'new_v_w_mix_out', 'new_v_norm_ffn2_g', 'new_v_ffn2_w_in', 'new_v_ffn2_w_out', 'new_v_norm_final_g']
TWIN_LEAF_KINDS = {'loss': 'loss', 'grad_x': 'grad_x', 'grad_w_ada': 'grad_w', 'grad_b_ada': 'grad_w', 'grad_norm_ffn1_g': 'grad_w', 'grad_ffn1_w_in': 'grad_w', 'grad_ffn1_w_out': 'grad_w', 'grad_norm_mix_g': 'grad_w', 'grad_w_mix_in': 'grad_w', 'grad_w_pool': 'grad_w', 'grad_pool_scale': 'grad_w', 'grad_gmlp_ln_g': 'grad_w', 'grad_gmlp_ln_b': 'grad_w', 'grad_w_spatial': 'grad_w', 'grad_b_spatial': 'grad_w', 'grad_w_mix_out': 'grad_w', 'grad_norm_ffn2_g': 'grad_w', 'grad_ffn2_w_in': 'grad_w', 'grad_ffn2_w_out': 'grad_w', 'grad_norm_final_g': 'grad_w', 'delta_w_ada': 'delta_w', 'delta_b_ada': 'delta_w', 'delta_norm_ffn1_g': 'delta_w', 'delta_ffn1_w_in': 'delta_w', 'delta_ffn1_w_out': 'delta_w', 'delta_norm_mix_g': 'delta_w', 'delta_w_mix_in': 'delta_w', 'delta_w_pool': 'delta_w', 'delta_pool_scale': 'delta_w', 'delta_gmlp_ln_g': 'delta_w', 'delta_gmlp_ln_b': 'delta_w', 'delta_w_spatial': 'delta_w', 'delta_b_spatial': 'delta_w', 'delta_w_mix_out': 'delta_w', 'delta_norm_ffn2_g': 'delta_w', 'delta_ffn2_w_in': 'delta_w', 'delta_ffn2_w_out': 'delta_w', 'delta_norm_final_g': 'delta_w', 'new_m_w_ada': 'new_m', 'new_m_b_ada': 'new_m', 'new_m_norm_ffn1_g': 'new_m', 'new_m_ffn1_w_in': 'new_m', 'new_m_ffn1_w_out': 'new_m', 'new_m_norm_mix_g': 'new_m', 'new_m_w_mix_in': 'new_m', 'new_m_w_pool': 'new_m', 'new_m_pool_scale': 'new_m', 'new_m_gmlp_ln_g': 'new_m', 'new_m_gmlp_ln_b': 'new_m', 'new_m_w_spatial': 'new_m', 'new_m_b_spatial': 'new_m', 'new_m_w_mix_out': 'new_m', 'new_m_norm_ffn2_g': 'new_m', 'new_m_ffn2_w_in': 'new_m', 'new_m_ffn2_w_out': 'new_m', 'new_m_norm_final_g': 'new_m', 'new_v_w_ada': 'new_v', 'new_v_b_ada': 'new_v', 'new_v_norm_ffn1_g': 'new_v', 'new_v_ffn1_w_in': 'new_v', 'new_v_ffn1_w_out': 'new_v', 'new_v_norm_mix_g': 'new_v', 'new_v_w_mix_in': 'new_v', 'new_v_w_pool': 'new_v', 'new_v_pool_scale': 'new_v', 'new_v_gmlp_ln_g': 'new_v', 'new_v_gmlp_ln_b': 'new_v', 'new_v_w_spatial': 'new_v', 'new_v_b_spatial': 'new_v', 'new_v_w_mix_out': 'new_v', 'new_v_norm_ffn2_g': 'new_v', 'new_v_ffn2_w_in': 'new_v', 'new_v_ffn2_w_out': 'new_v', 'new_v_norm_final_g': 'new_v'}


def _forward(args):
    return _fwd_reference(*[args[k] for k in FWD_PARAMS])


def _output_shape():
    def fwd():
        inp = _fwd_setup_inputs(0)
        return _fwd_reference(*[inp[k] for k in FWD_PARAMS])
    out = _jax.eval_shape(fwd)
    return out.shape, out.dtype

N_MICROBATCH = 1
ADAM_LR = 0.001
ADAM_B1 = 0.9
ADAM_B2 = 0.999
ADAM_EPS = 1e-08
ADAM_WD = 0.01
ADAM_STEP = 10
PER_EXAMPLE_BATCH_AXIS = {'x': 0, 'c': 0, 'loss_target': 0}
SHARED_INPUTS = []
_WEIGHT_DTYPES = {'w_ada': _jnp.float32, 'b_ada': _jnp.float32, 'norm_ffn1_g': _jnp.float32, 'ffn1_w_in': _jnp.float32, 'ffn1_w_out': _jnp.float32, 'norm_mix_g': _jnp.float32, 'w_mix_in': _jnp.float32, 'w_pool': _jnp.float32, 'pool_scale': _jnp.float32, 'gmlp_ln_g': _jnp.float32, 'gmlp_ln_b': _jnp.float32, 'w_spatial': _jnp.float32, 'b_spatial': _jnp.float32, 'w_mix_out': _jnp.float32, 'norm_ffn2_g': _jnp.float32, 'ffn2_w_in': _jnp.float32, 'ffn2_w_out': _jnp.float32, 'norm_final_g': _jnp.float32}
MOMENT_SCALE = {'w_ada': 1.551393e-01, 'b_ada': 3.428792e-01, 'norm_ffn1_g': 1.083142e-01, 'ffn1_w_in': 4.945744e-02, 'ffn1_w_out': 8.063170e-02, 'norm_mix_g': 1.502260e-01, 'w_mix_in': 1.308661e-01, 'w_pool': 1.370074e-01, 'pool_scale': 1.342625e-01, 'gmlp_ln_g': 8.159200e-02, 'gmlp_ln_b': 7.986779e-02, 'w_spatial': 5.646030e-02, 'b_spatial': 9.225559e-02, 'w_mix_out': 1.795330e-01, 'norm_ffn2_g': 8.588930e-02, 'ffn2_w_in': 4.148517e-02, 'ffn2_w_out': 6.870152e-02, 'norm_final_g': 6.467535e+01}


def _to_microbatches(a, axis):
    t = _jnp.moveaxis(a, axis, 0)
    t = t.reshape((N_MICROBATCH, t.shape[0] // N_MICROBATCH) + t.shape[1:])
    return _jnp.moveaxis(t, 1, axis + 1)


def setup_inputs(seed: int = 0) -> dict:
    inp = _fwd_setup_inputs(seed)
    key = _jax.random.fold_in(_jax.random.key(seed), 7919)
    shape, _ = _output_shape()
    out = dict(inp)
    out["loss_target"] = _jax.random.normal(_jax.random.fold_in(key, 0), shape, _jnp.float32)
    for i, name in enumerate(TWIN_WEIGHTS):
        w = inp[name].astype(_jnp.float32)
        if MOMENT_SCALE is None:
            s = _jnp.sqrt(_jnp.mean(_jnp.square(w)) + 1e-30)
        else:
            s = MOMENT_SCALE[name]
        km, kv = _jax.random.split(_jax.random.fold_in(key, i + 1))
        out[name] = w
        out["m_" + name] = s * _jax.random.normal(km, w.shape, _jnp.float32)
        out["v_" + name] = (s * s) * _jax.random.uniform(kv, w.shape, _jnp.float32, 0.5, 1.5)
    if N_MICROBATCH > 1:
        for name, axis in PER_EXAMPLE_BATCH_AXIS.items():
            out[name] = _to_microbatches(out[name], axis)
    return {'x': out['x'], 'c': out['c'], 'w_ada': out['w_ada'], 'b_ada': out['b_ada'], 'norm_ffn1_g': out['norm_ffn1_g'], 'ffn1_w_in': out['ffn1_w_in'], 'ffn1_w_out': out['ffn1_w_out'], 'norm_mix_g': out['norm_mix_g'], 'w_mix_in': out['w_mix_in'], 'w_pool': out['w_pool'], 'pool_scale': out['pool_scale'], 'gmlp_ln_g': out['gmlp_ln_g'], 'gmlp_ln_b': out['gmlp_ln_b'], 'w_spatial': out['w_spatial'], 'b_spatial': out['b_spatial'], 'w_mix_out': out['w_mix_out'], 'norm_ffn2_g': out['norm_ffn2_g'], 'ffn2_w_in': out['ffn2_w_in'], 'ffn2_w_out': out['ffn2_w_out'], 'norm_final_g': out['norm_final_g'], 'loss_target': out['loss_target'], 'm_w_ada': out['m_w_ada'], 'm_b_ada': out['m_b_ada'], 'm_norm_ffn1_g': out['m_norm_ffn1_g'], 'm_ffn1_w_in': out['m_ffn1_w_in'], 'm_ffn1_w_out': out['m_ffn1_w_out'], 'm_norm_mix_g': out['m_norm_mix_g'], 'm_w_mix_in': out['m_w_mix_in'], 'm_w_pool': out['m_w_pool'], 'm_pool_scale': out['m_pool_scale'], 'm_gmlp_ln_g': out['m_gmlp_ln_g'], 'm_gmlp_ln_b': out['m_gmlp_ln_b'], 'm_w_spatial': out['m_w_spatial'], 'm_b_spatial': out['m_b_spatial'], 'm_w_mix_out': out['m_w_mix_out'], 'm_norm_ffn2_g': out['m_norm_ffn2_g'], 'm_ffn2_w_in': out['m_ffn2_w_in'], 'm_ffn2_w_out': out['m_ffn2_w_out'], 'm_norm_final_g': out['m_norm_final_g'], 'v_w_ada': out['v_w_ada'], 'v_b_ada': out['v_b_ada'], 'v_norm_ffn1_g': out['v_norm_ffn1_g'], 'v_ffn1_w_in': out['v_ffn1_w_in'], 'v_ffn1_w_out': out['v_ffn1_w_out'], 'v_norm_mix_g': out['v_norm_mix_g'], 'v_w_mix_in': out['v_w_mix_in'], 'v_w_pool': out['v_w_pool'], 'v_pool_scale': out['v_pool_scale'], 'v_gmlp_ln_g': out['v_gmlp_ln_g'], 'v_gmlp_ln_b': out['v_gmlp_ln_b'], 'v_w_spatial': out['v_w_spatial'], 'v_b_spatial': out['v_b_spatial'], 'v_w_mix_out': out['v_w_mix_out'], 'v_norm_ffn2_g': out['v_norm_ffn2_g'], 'v_ffn2_w_in': out['v_ffn2_w_in'], 'v_ffn2_w_out': out['v_ffn2_w_out'], 'v_norm_final_g': out['v_norm_final_g']}


def _loss(weights, diff, rest, loss_target):
    with _jax.named_scope("forward"):
        args = {**rest, TWIN_DIFF_INPUT: diff, **{k: w.astype(_WEIGHT_DTYPES[k]) for k, w in weights.items()}}
        y = _forward(args)
    with _jax.named_scope("loss_head"):
        err = _jnp.square(y.astype(_jnp.float32) - loss_target)
        return 0.5 * _jnp.sum(_jnp.mean(err, axis=-1)) if err.ndim else 0.5 * err


def _adamw(w, g, m, v):
    m = ADAM_B1 * m + (1.0 - ADAM_B1) * g
    v = ADAM_B2 * v + (1.0 - ADAM_B2) * _jnp.square(g)
    m_hat = m / (1.0 - ADAM_B1 ** ADAM_STEP)
    v_hat = v / (1.0 - ADAM_B2 ** ADAM_STEP)
    delta = -ADAM_LR * (m_hat / (_jnp.sqrt(v_hat) + ADAM_EPS) + ADAM_WD * w)
    return delta, m, v


def reference(x, c, w_ada, b_ada, norm_ffn1_g, ffn1_w_in, ffn1_w_out, norm_mix_g, w_mix_in, w_pool, pool_scale, gmlp_ln_g, gmlp_ln_b, w_spatial, b_spatial, w_mix_out, norm_ffn2_g, ffn2_w_in, ffn2_w_out, norm_final_g, loss_target, m_w_ada, m_b_ada, m_norm_ffn1_g, m_ffn1_w_in, m_ffn1_w_out, m_norm_mix_g, m_w_mix_in, m_w_pool, m_pool_scale, m_gmlp_ln_g, m_gmlp_ln_b, m_w_spatial, m_b_spatial, m_w_mix_out, m_norm_ffn2_g, m_ffn2_w_in, m_ffn2_w_out, m_norm_final_g, v_w_ada, v_b_ada, v_norm_ffn1_g, v_ffn1_w_in, v_ffn1_w_out, v_norm_mix_g, v_w_mix_in, v_w_pool, v_pool_scale, v_gmlp_ln_g, v_gmlp_ln_b, v_w_spatial, v_b_spatial, v_w_mix_out, v_norm_ffn2_g, v_ffn2_w_in, v_ffn2_w_out, v_norm_final_g):
    given = dict(x=x, c=c, w_ada=w_ada, b_ada=b_ada, norm_ffn1_g=norm_ffn1_g, ffn1_w_in=ffn1_w_in, ffn1_w_out=ffn1_w_out, norm_mix_g=norm_mix_g, w_mix_in=w_mix_in, w_pool=w_pool, pool_scale=pool_scale, gmlp_ln_g=gmlp_ln_g, gmlp_ln_b=gmlp_ln_b, w_spatial=w_spatial, b_spatial=b_spatial, w_mix_out=w_mix_out, norm_ffn2_g=norm_ffn2_g, ffn2_w_in=ffn2_w_in, ffn2_w_out=ffn2_w_out, norm_final_g=norm_final_g, loss_target=loss_target, m_w_ada=m_w_ada, m_b_ada=m_b_ada, m_norm_ffn1_g=m_norm_ffn1_g, m_ffn1_w_in=m_ffn1_w_in, m_ffn1_w_out=m_ffn1_w_out, m_norm_mix_g=m_norm_mix_g, m_w_mix_in=m_w_mix_in, m_w_pool=m_w_pool, m_pool_scale=m_pool_scale, m_gmlp_ln_g=m_gmlp_ln_g, m_gmlp_ln_b=m_gmlp_ln_b, m_w_spatial=m_w_spatial, m_b_spatial=m_b_spatial, m_w_mix_out=m_w_mix_out, m_norm_ffn2_g=m_norm_ffn2_g, m_ffn2_w_in=m_ffn2_w_in, m_ffn2_w_out=m_ffn2_w_out, m_norm_final_g=m_norm_final_g, v_w_ada=v_w_ada, v_b_ada=v_b_ada, v_norm_ffn1_g=v_norm_ffn1_g, v_ffn1_w_in=v_ffn1_w_in, v_ffn1_w_out=v_ffn1_w_out, v_norm_mix_g=v_norm_mix_g, v_w_mix_in=v_w_mix_in, v_w_pool=v_w_pool, v_pool_scale=v_pool_scale, v_gmlp_ln_g=v_gmlp_ln_g, v_gmlp_ln_b=v_gmlp_ln_b, v_w_spatial=v_w_spatial, v_b_spatial=v_b_spatial, v_w_mix_out=v_w_mix_out, v_norm_ffn2_g=v_norm_ffn2_g, v_ffn2_w_in=v_ffn2_w_in, v_ffn2_w_out=v_ffn2_w_out, v_norm_final_g=v_norm_final_g)
    weights = {n: given[n] for n in TWIN_WEIGHTS}
    shared = {n: given[n] for n in SHARED_INPUTS}
    per_example = {n: given[n] for n in ['x', 'c']}
    grad_fn = _jax.value_and_grad(_loss, argnums=(0, 1))

    def one_microbatch(ex, loss_target):
        ex = dict(ex)
        diff = ex.pop(TWIN_DIFF_INPUT)
        return grad_fn(weights, diff, {**shared, **ex}, loss_target)

    if N_MICROBATCH == 1:
        loss, (grad_w, grad_x) = one_microbatch(per_example, given["loss_target"])
    else:
        def body(carry, xs):
            loss_sum, grad_sum = carry
            l_k, (gw_k, gx_k) = one_microbatch(xs[0], xs[1])
            with _jax.named_scope("update"):
                return (loss_sum + l_k, _jax.tree.map(_jnp.add, grad_sum, gw_k)), gx_k

        init = (_jnp.zeros((), _jnp.float32), _jax.tree.map(_jnp.zeros_like, weights))
        (loss, grad_w), grad_x = _jax.lax.scan(body, init, (per_example, given["loss_target"]))
    with _jax.named_scope("update"):
        delta_w, new_m, new_v = {}, {}, {}
        for n in TWIN_WEIGHTS:
            delta_w[n], new_m[n], new_v[n] = _adamw(weights[n], grad_w[n], given["m_" + n], given["v_" + n])
    return (loss, grad_x, *[grad_w[n] for n in TWIN_WEIGHTS], *[delta_w[n] for n in TWIN_WEIGHTS],
            *[new_m[n] for n in TWIN_WEIGHTS], *[new_v[n] for n in TWIN_WEIGHTS])
```

```python
import functools
import math

import jax
import jax.numpy as jnp
from jax import lax
from jax.experimental import pallas as pl
from jax.experimental.pallas import tpu as pltpu

F32 = jnp.float32
BF16 = jnp.bfloat16
MESH = pl.DeviceIdType.MESH
HIGHEST = lax.Precision.HIGHEST

EPS = 1e-6
D = 1024
DFF = 2816
CH = DFF // 2
NQ = 4
DP = 512
DG = 512
DPROJ = DP + 2 * DG
POOL_WINDOWS = (2, 4, 8, 16)
HALO = 16
CHUNK = 128
LANE = 128
N_DEV = 8

ADAM_LR = 0.001
ADAM_B1 = 0.9
ADAM_B2 = 0.999
ADAM_EPS = 1e-08
ADAM_WD = 0.01
ADAM_STEP = 10

VMEM_LIMIT = 60 * 1024 * 1024

TM_FFN_FWD = 512
TM_FFN_BWD = 256
TM_MIX = 256
TM_HEAD = 512


def _call(body, **kw):
    return pl.pallas_call(body, interpret=False, **kw)


def _params(sem=None, vmem=None):
    return pltpu.CompilerParams(dimension_semantics=sem, vmem_limit_bytes=vmem)


def _sds(shape, dtype):
    return jax.ShapeDtypeStruct(shape, dtype)


ANY = pl.BlockSpec(memory_space=pl.ANY)
VMEM = pl.BlockSpec(memory_space=pltpu.VMEM)
SMEM = pl.BlockSpec(memory_space=pltpu.SMEM)


def _norm_mod(x, gn, sc, sh):
    r = lax.rsqrt(jnp.mean(x * x, axis=-1, keepdims=True) + EPS)
    xn = x * r
    hp = xn * gn
    return r, xn, hp, hp * (1.0 + sc) + sh


def _norm_mod_bwd(dh, r, xn, hp, gn, sc):
    one_sc = 1.0 + sc
    dsh = jnp.sum(dh, axis=0, keepdims=True)
    dsc = jnp.sum(dh * hp, axis=0, keepdims=True)
    dgn = jnp.sum(dh * one_sc * xn, axis=0, keepdims=True)
    dxn = dh * (gn * one_sc)
    dx = r * (dxn - xn * jnp.mean(dxn * xn, axis=-1, keepdims=True))
    return dsh, dsc, dgn, dx


def _dot(a, b):
    return jnp.dot(a, b, preferred_element_type=F32)


def _dot_nt(a, b):
    return lax.dot_general(a, b, (((1,), (1,)), ((), ())), preferred_element_type=F32)


def _dot_tn(a, b):
    return lax.dot_general(a, b, (((0,), (0,)), ((), ())), preferred_element_type=F32)


_GELU_C = math.sqrt(2.0 / math.pi)
_GELU_A = 0.044715


def _gelu_fwd_bwd(x):
    x2 = x * x
    t = jnp.tanh(_GELU_C * (x + _GELU_A * x * x2))
    g = 0.5 * x * (1.0 + t)
    dg = 0.5 * (1.0 + t) + 0.5 * x * (1.0 - t * t) * (_GELU_C * (1.0 + 3.0 * _GELU_A * x2))
    return g, dg


def _adamw(w, g, m, v):
    m = ADAM_B1 * m + (1.0 - ADAM_B1) * g
    v = ADAM_B2 * v + (1.0 - ADAM_B2) * (g * g)
    m_hat = m / (1.0 - ADAM_B1 ** ADAM_STEP)
    v_hat = v / (1.0 - ADAM_B2 ** ADAM_STEP)
    delta = -ADAM_LR * (m_hat / (jnp.sqrt(v_hat) + ADAM_EPS) + ADAM_WD * w)
    return delta, m, v


def _row_block(rows, cap=256, mult=16):
    best = None
    for t in range(mult, min(rows, cap) + 1, mult):
        if rows % t == 0:
            best = t
    assert best is not None, rows
    return best


def _ffn_fwd(x, modv, win, wout):
    S = x.shape[0]
    tm = TM_FFN_FWD
    nt = S // tm

    def body(x_ref, mod_ref, wg_ref, wu_ref, wo_ref, xo_ref, gs_ref, us_ref, h_scr, acc_scr):
        j = pl.program_id(1)

        @pl.when(j == 0)
        def _():
            _, _, _, h = _norm_mod(x_ref[...], mod_ref[3:4, :], mod_ref[1:2, :], mod_ref[0:1, :])
            h_scr[...] = h.astype(BF16)
            acc_scr[...] = jnp.zeros_like(acc_scr)

        h = h_scr[...]
        g = _dot(h, wg_ref[...]).astype(BF16)
        u = _dot(h, wu_ref[...]).astype(BF16)
        gs_ref[...] = g
        us_ref[...] = u
        gf = g.astype(F32)
        a = (gf * jax.nn.sigmoid(gf) * u.astype(F32)).astype(BF16)
        acc_scr[...] += _dot(a, wo_ref[...])

        @pl.when(j == 1)
        def _():
            xo_ref[...] = x_ref[...] + (0.5 * mod_ref[2:3, :]) * acc_scr[...]

    return _call(
        body, name="ffn_fwd",
        grid=(nt, 2),
        in_specs=[
            pl.BlockSpec((tm, D), lambda i, j: (i, 0)),
            pl.BlockSpec((8, D), lambda i, j: (0, 0)),
            pl.BlockSpec((None, D, CH), lambda i, j: (j, 0, 0)),
            pl.BlockSpec((None, D, CH), lambda i, j: (2 + j, 0, 0)),
            pl.BlockSpec((None, CH, D), lambda i, j: (j, 0, 0)),
        ],
        out_specs=[
            pl.BlockSpec((tm, D), lambda i, j: (i, 0)),
            pl.BlockSpec((tm, CH), lambda i, j: (i, j)),
            pl.BlockSpec((tm, CH), lambda i, j: (i, j)),
        ],
        out_shape=[_sds((S, D), F32), _sds((S, DFF), BF16), _sds((S, DFF), BF16)],
        scratch_shapes=[pltpu.VMEM((tm, D), BF16), pltpu.VMEM((tm, D), F32)],
        compiler_params=_params(("arbitrary", "arbitrary"), VMEM_LIMIT),
    )(x, modv, win, win, wout)


def _ffn_bwd(x, dxo, gs, us, modv, win, wout):
    S = x.shape[0]
    tm = TM_FFN_BWD
    nt = S // tm
    assert nt >= 2

    def body(x_ref, dxo_ref, gs_ref, us_ref, mod_ref, wg_ref, wu_ref, wo_ref,
             dx_ref, dwin_ref, dwout_ref, vec_ref, dhbuf_ref, accg, accu, accw, dh_st, dh_ld, sems):
        j = pl.program_id(0)
        i = pl.program_id(1)
        store = lambda t: pltpu.make_async_copy(dh_st, dhbuf_ref.at[t], sems.at[3])
        load = lambda t: pltpu.make_async_copy(dhbuf_ref.at[t], dh_ld, sems.at[4])

        @pl.when(i == 0)
        def _():
            accg[...] = jnp.zeros_like(accg)
            accu[...] = jnp.zeros_like(accu)
            accw[...] = jnp.zeros_like(accw)

        @pl.when((i == 0) & (j == 0))
        def _():
            vec_ref[...] = jnp.zeros_like(vec_ref)

        @pl.when((j == 0) & (i > 0))
        def _():
            store(i - 1).wait()

        @pl.when((j == 1) & (i == 0))
        def _():
            store(nt - 1).wait()

        @pl.when(j == 1)
        def _():
            load(i).start()

        gn, sc, sh, gate = mod_ref[3:4, :], mod_ref[1:2, :], mod_ref[0:1, :], mod_ref[2:3, :]
        x = x_ref[...]
        r, xn, hp, h = _norm_mod(x, gn, sc, sh)
        hb = h.astype(BF16)
        dxo = dxo_ref[...]
        dy = (dxo * (0.5 * gate)).astype(BF16)
        g = gs_ref[...].astype(F32)
        u = us_ref[...].astype(F32)
        sig = jax.nn.sigmoid(g)
        sl = g * sig
        a = (sl * u).astype(BF16)
        accw[...] += _dot_tn(a, dxo.astype(BF16))
        da = _dot_nt(dy, wo_ref[...])
        dg = (da * u * (sig * (1.0 + g * (1.0 - sig)))).astype(BF16)
        du = (da * sl).astype(BF16)
        accg[...] += _dot_tn(hb, dg)
        accu[...] += _dot_tn(hb, du)
        dhp = _dot_nt(dg, wg_ref[...]) + _dot_nt(du, wu_ref[...])

        @pl.when(j == 0)
        def _():
            dh_st[...] = dhp
            store(i).start()

        @pl.when(j == 1)
        def _():
            load(i).wait()
            dh = dh_ld[...] + dhp
            dsh, dsc, dgn, dxin = _norm_mod_bwd(dh, r, xn, hp, gn, sc)
            vec_ref[0:1, :] += dsh
            vec_ref[1:2, :] += dsc
            vec_ref[3:4, :] += dgn
            dx_ref[...] = dxo + dxin

        @pl.when(i == nt - 1)
        def _():
            gw = accw[...]
            vec_ref[2:3, :] += 0.5 * jnp.sum(wo_ref[...].astype(F32) * gw, axis=0, keepdims=True)
            accw[...] = gw * (0.5 * gate)
            cps = [pltpu.make_async_copy(accg, dwin_ref.at[j], sems.at[0]),
                   pltpu.make_async_copy(accu, dwin_ref.at[2 + j], sems.at[1]),
                   pltpu.make_async_copy(accw, dwout_ref.at[j], sems.at[2])]
            for cp in cps:
                cp.start()
            for cp in cps:
                cp.wait()

    dx, dwin, dwout, vec, _ = _call(
        body, name="ffn_bwd",
        grid=(2, nt),
        in_specs=[
            pl.BlockSpec((tm, D), lambda j, i: (i, 0)),
            pl.BlockSpec((tm, D), lambda j, i: (i, 0)),
            pl.BlockSpec((tm, CH), lambda j, i: (i, j)),
            pl.BlockSpec((tm, CH), lambda j, i: (i, j)),
            pl.BlockSpec((8, D), lambda j, i: (0, 0)),
            pl.BlockSpec((None, D, CH), lambda j, i: (j, 0, 0)),
            pl.BlockSpec((None, D, CH), lambda j, i: (2 + j, 0, 0)),
            pl.BlockSpec((None, CH, D), lambda j, i: (j, 0, 0)),
        ],
        out_specs=[
            pl.BlockSpec((tm, D), lambda j, i: (i * j, 0)),
            ANY, ANY,
            pl.BlockSpec((8, D), lambda j, i: (0, 0)),
            ANY,
        ],
        out_shape=[_sds((S, D), F32), _sds((NQ, D, CH), F32), _sds((2, CH, D), F32), _sds((8, D), F32),
                   _sds((nt, tm, D), F32)],
        scratch_shapes=[pltpu.VMEM((D, CH), F32), pltpu.VMEM((D, CH), F32), pltpu.VMEM((CH, D), F32),
                        pltpu.VMEM((tm, D), F32), pltpu.VMEM((tm, D), F32), pltpu.SemaphoreType.DMA((5,))],
        compiler_params=_params(("arbitrary", "arbitrary"), VMEM_LIMIT),
    )(x, dxo, gs, us, modv, win, win, wout)
    return dx, dwin, dwout, vec


def _head(x, target, gf):
    S = x.shape[0]
    tm = TM_HEAD
    nt = S // tm

    def body(x_ref, t_ref, g_ref, dx_ref, loss_ref, dg_ref):
        i = pl.program_id(0)

        @pl.when(i == 0)
        def _():
            loss_ref[...] = jnp.zeros_like(loss_ref)
            dg_ref[...] = jnp.zeros_like(dg_ref)

        x = x_ref[...]
        gf_ = g_ref[0:1, :]
        r = lax.rsqrt(jnp.mean(x * x, axis=-1, keepdims=True) + EPS)
        xn = x * r
        err = xn * gf_ - t_ref[...]
        loss_ref[...] += (0.5 / D) * jnp.sum(err * err)
        dy = err * (1.0 / D)
        dg_ref[0:1, :] += jnp.sum(dy * xn, axis=0, keepdims=True)
        dxn = dy * gf_
        dx_ref[...] = r * (dxn - xn * jnp.mean(dxn * xn, axis=-1, keepdims=True))

    return _call(
        body, name="head",
        grid=(nt,),
        in_specs=[pl.BlockSpec((tm, D), lambda i: (i, 0)), pl.BlockSpec((tm, D), lambda i: (i, 0)),
                  pl.BlockSpec((8, D), lambda i: (0, 0))],
        out_specs=[pl.BlockSpec((tm, D), lambda i: (i, 0)), pl.BlockSpec((8, LANE), lambda i: (0, 0)),
                   pl.BlockSpec((8, D), lambda i: (0, 0))],
        out_shape=[_sds((S, D), F32), _sds((8, LANE), F32), _sds((8, D), F32)],
        compiler_params=_params(("arbitrary",), VMEM_LIMIT),
    )(x, target, gf)


def _prep_spatial(w_spatial, b_spatial_t):
    def body(w_ref, b_ref, wcat_ref, wtcat_ref, bias_ref):
        row = lax.broadcasted_iota(jnp.int32, (CHUNK, CHUNK), 0)
        col = lax.broadcasted_iota(jnp.int32, (CHUNK, CHUNK), 1)
        tril = col <= row
        for p in range(4):
            wa = jnp.where(tril, w_ref[2 * p], 0.0)
            wb = jnp.where(tril, w_ref[2 * p + 1], 0.0)
            wcat_ref[p] = jnp.concatenate([wa, wb], axis=1).astype(BF16)
            wtcat_ref[p] = jnp.concatenate([wa.T, wb.T], axis=1).astype(BF16)
        head = lax.broadcasted_iota(jnp.int32, (8, DG), 0)
        ch = lax.broadcasted_iota(jnp.int32, (8, DG), 1)
        spread = jnp.where(ch // 64 == head, 1.0, 0.0).astype(F32)
        bias_ref[...] = jnp.dot(b_ref[...], spread, precision=HIGHEST, preferred_element_type=F32)

    return _call(
        body, name="prep_spatial",
        in_specs=[VMEM, VMEM], out_specs=[VMEM, VMEM, VMEM],
        out_shape=[_sds((4, CHUNK, 2 * CHUNK), BF16), _sds((4, CHUNK, 2 * CHUNK), BF16), _sds((CHUNK, DG), F32)],
    )(w_spatial, b_spatial_t)


def _pair_rhs(blocks):
    lane = lax.broadcasted_iota(jnp.int32, (CHUNK, LANE), 1)
    lo = lane < 64
    top = jnp.concatenate([jnp.where(lo, b, 0.0) for b in blocks], axis=1)
    bot = jnp.concatenate([jnp.where(lo, 0.0, b) for b in blocks], axis=1)
    return top, bot


def _gmlp_branch(zb, vecs, wcat_ref, bias_ref, nchunks):
    z, dz = _gelu_fwd_bwd(zb)
    u = z[:, :DG]
    v = z[:, DG:]
    ln_g, ln_b = vecs[1:2, :], vecs[2:3, :]
    mu = jnp.mean(v, axis=-1, keepdims=True)
    vc = v - mu
    rstd = lax.rsqrt(jnp.mean(vc * vc, axis=-1, keepdims=True) + EPS)
    vhat = vc * rstd
    vl = vhat * ln_g + ln_b
    sv_cols = []
    for p in range(4):
        blocks = [vl[k * CHUNK:(k + 1) * CHUNK, p * LANE:(p + 1) * LANE] for k in range(nchunks)]
        top, bot = _pair_rhs(blocks)
        rhs = jnp.concatenate([top, bot], axis=0).astype(BF16)
        out = _dot(wcat_ref[p], rhs)
        bias = bias_ref[:, p * LANE:(p + 1) * LANE]
        sv_cols.append(jnp.concatenate([out[:, k * LANE:(k + 1) * LANE] + bias for k in range(nchunks)], axis=0))
    sv = jnp.concatenate(sv_cols, axis=1)
    return dict(u=u, dz=dz, rstd=rstd, vhat=vhat, vl=vl, sv=sv, yb=u * sv)


def _mix_fwd(x, modv, win, wpool, vecs, wcat, bias, wout):
    S = x.shape[0]
    tm = TM_MIX
    nt = S // tm
    nchunks = tm // CHUNK

    def body(x_ref, mod_ref, win_ref, wpool_ref, vec_ref, wcat_ref, bias_ref, wout_ref,
             xo_ref, pooled_ref, zb_ref, ext):
        i = pl.program_id(0)
        x = x_ref[...]
        _, _, _, h = _norm_mod(x, mod_ref[3:4, :], mod_ref[1:2, :], mod_ref[0:1, :])
        proj = _dot(h.astype(BF16), win_ref[...])
        xa = proj[:, :DP]
        zb = proj[:, DP:]
        zb_ref[...] = zb

        @pl.when(i == 0)
        def _():
            ext[0:HALO, :] = jnp.zeros((HALO, DP), F32)

        ext[HALO:HALO + tm, :] = xa
        pos = i * tm + lax.broadcasted_iota(jnp.int32, (tm, 1), 0)
        vecs = vec_ref[...]
        ya_cols = []
        pooled_cols = []
        for gi, w in enumerate(POOL_WINDOWS):
            cols = slice(gi * LANE, (gi + 1) * LANE)
            s = xa[:, cols]
            for k in range(1, w):
                s = s + ext[HALO - k:HALO - k + tm, cols]
            cnt = jnp.minimum(pos + 1, w).astype(F32)
            pooled = (s / cnt - xa[:, cols]).astype(BF16)
            pooled_cols.append(pooled)
            ya_cols.append(_dot(pooled, wpool_ref[gi]) * vecs[0:1, cols])
        pooled_ref[...] = jnp.concatenate(pooled_cols, axis=1)
        ext[0:HALO, :] = ext[tm:tm + HALO, :]

        gm = _gmlp_branch(zb, vecs, wcat_ref, bias_ref, nchunks)
        cat = jnp.concatenate(ya_cols + [gm["yb"]], axis=1).astype(BF16)
        xo_ref[...] = x + mod_ref[2:3, :] * _dot(cat, wout_ref[...])

    full = lambda shape: pl.BlockSpec(shape, lambda i: (0,) * len(shape))
    return _call(
        body, name="mix_fwd",
        grid=(nt,),
        in_specs=[pl.BlockSpec((tm, D), lambda i: (i, 0)), full((8, D)), full((D, DPROJ)),
                  full((4, LANE, LANE)), full((8, DP)), full((4, CHUNK, 2 * CHUNK)), full((CHUNK, DG)),
                  full((DP + DG, D))],
        out_specs=[pl.BlockSpec((tm, D), lambda i: (i, 0)), pl.BlockSpec((tm, DP), lambda i: (i, 0)),
                   pl.BlockSpec((tm, 2 * DG), lambda i: (i, 0))],
        out_shape=[_sds((S, D), F32), _sds((S, DP), BF16), _sds((S, 2 * DG), F32)],
        scratch_shapes=[pltpu.VMEM((tm + HALO, DP), F32)],
        compiler_params=_params(("arbitrary",), VMEM_LIMIT),
    )(x, modv, win, wpool, vecs, wcat, bias, wout)


def _mix_bwd(x, dxo, pooled, zb, modv, win, wpool, vecs, wcat, wtcat, bias, wout):
    S = x.shape[0]
    tm = TM_MIX
    nt = S // tm
    nchunks = tm // CHUNK

    def body(x_ref, dxo_ref, pooled_ref, zb_ref, mod_ref, win_ref, wpool_ref, vec_ref, wcat_ref, wtcat_ref,
             bias_ref, wout_ref,
             dx_ref, dwin_ref, dwout_ref, dwpool_ref, dwsp_ref, dbsp_ref, v512_ref, vd_ref, qext, dsv_acc):
        step = pl.program_id(0)
        tile = nt - 1 - step

        @pl.when(step == 0)
        def _():
            dwin_ref[...] = jnp.zeros_like(dwin_ref)
            dwout_ref[...] = jnp.zeros_like(dwout_ref)
            dwpool_ref[...] = jnp.zeros_like(dwpool_ref)
            dwsp_ref[...] = jnp.zeros_like(dwsp_ref)
            v512_ref[...] = jnp.zeros_like(v512_ref)
            vd_ref[...] = jnp.zeros_like(vd_ref)
            dsv_acc[...] = jnp.zeros_like(dsv_acc)
            qext[tm:tm + HALO, :] = jnp.zeros((HALO, DP), F32)

        gn, sc, sh, gate = mod_ref[3:4, :], mod_ref[1:2, :], mod_ref[0:1, :], mod_ref[2:3, :]
        vecs = vec_ref[...]
        x = x_ref[...]
        r, xn, hp, h = _norm_mod(x, gn, sc, sh)
        hb = h.astype(BF16)
        dxo = dxo_ref[...]

        pooled = pooled_ref[...]
        mixed_cols = [_dot(pooled[:, gi * LANE:(gi + 1) * LANE], wpool_ref[gi]) for gi in range(4)]
        mixed = jnp.concatenate(mixed_cols, axis=1)
        scale = vecs[0:1, :]
        gm = _gmlp_branch(zb_ref[...], vecs, wcat_ref, bias_ref, nchunks)
        cat = jnp.concatenate([mixed * scale, gm["yb"]], axis=1).astype(BF16)

        dwout_ref[...] += _dot_tn(cat, dxo.astype(BF16))
        dcat = _dot_nt((dxo * gate).astype(BF16), wout_ref[...])
        dya = dcat[:, :DP]
        dyb = dcat[:, DP:]

        v512_ref[0:1, :] += jnp.sum(dya * mixed, axis=0, keepdims=True)
        dmixed = (dya * scale).astype(BF16)
        pos = tile * tm + lax.broadcasted_iota(jnp.int32, (tm, 1), 0)
        dpooled_cols = []
        for gi, w in enumerate(POOL_WINDOWS):
            cols = slice(gi * LANE, (gi + 1) * LANE)
            dp = _dot_nt(dmixed[:, cols], wpool_ref[gi])
            dwpool_ref[gi] += _dot_tn(pooled[:, cols], dmixed[:, cols])
            cnt = jnp.minimum(pos + 1, w).astype(F32)
            qext[0:tm, cols] = dp / cnt
            dpooled_cols.append(dp)
        dxa_cols = []
        for gi, w in enumerate(POOL_WINDOWS):
            cols = slice(gi * LANE, (gi + 1) * LANE)
            s = qext[0:tm, cols]
            for k in range(1, w):
                s = s + qext[k:k + tm, cols]
            dxa_cols.append(s - dpooled_cols[gi])
        qext[tm:tm + HALO, :] = qext[0:HALO, :]

        u, sv, vl = gm["u"], gm["sv"], gm["vl"]
        du = dyb * sv
        dsv = dyb * u
        dvl_cols = []
        for p in range(4):
            cols = slice(p * LANE, (p + 1) * LANE)
            dblocks = [dsv[k * CHUNK:(k + 1) * CHUNK, cols] for k in range(nchunks)]
            vblocks = [vl[k * CHUNK:(k + 1) * CHUNK, cols] for k in range(nchunks)]
            tot = dblocks[0]
            for b in dblocks[1:]:
                tot = tot + b
            dsv_acc[:, cols] += tot
            top, bot = _pair_rhs(dblocks)
            out = _dot(wtcat_ref[p], jnp.concatenate([top, bot], axis=0).astype(BF16))
            dvl_cols.append(jnp.concatenate([out[:, k * LANE:(k + 1) * LANE] for k in range(nchunks)], axis=0))
            vcat = jnp.concatenate(vblocks, axis=1).astype(BF16)
            dwsp_ref[2 * p] += _dot_nt(top.astype(BF16), vcat)
            dwsp_ref[2 * p + 1] += _dot_nt(bot.astype(BF16), vcat)
        dvl = jnp.concatenate(dvl_cols, axis=1)
        vhat, rstd = gm["vhat"], gm["rstd"]
        v512_ref[1:2, :] += jnp.sum(dvl * vhat, axis=0, keepdims=True)
        v512_ref[2:3, :] += jnp.sum(dvl, axis=0, keepdims=True)
        dvh = dvl * vecs[1:2, :]
        dv = rstd * (dvh - jnp.mean(dvh, axis=-1, keepdims=True)
                     - vhat * jnp.mean(dvh * vhat, axis=-1, keepdims=True))
        dzb = jnp.concatenate([du, dv], axis=1) * gm["dz"]

        dproj = jnp.concatenate(dxa_cols + [dzb], axis=1).astype(BF16)
        dwin_ref[...] += _dot_tn(hb, dproj)
        dh = _dot_nt(dproj, win_ref[...])
        dsh, dsc, dgn, dxin = _norm_mod_bwd(dh, r, xn, hp, gn, sc)
        vd_ref[0:1, :] += dsh
        vd_ref[1:2, :] += dsc
        vd_ref[3:4, :] += dgn
        dx_ref[...] = dxo + dxin

        @pl.when(step == nt - 1)
        def _():
            gw = dwout_ref[...]
            vd_ref[2:3, :] += jnp.sum(wout_ref[...].astype(F32) * gw, axis=0, keepdims=True)
            dwout_ref[...] = gw * gate
            row = lax.broadcasted_iota(jnp.int32, (CHUNK, CHUNK), 0)
            col = lax.broadcasted_iota(jnp.int32, (CHUNK, CHUNK), 1)
            for hh in range(8):
                dwsp_ref[hh] = jnp.where(col <= row, dwsp_ref[hh], 0.0)
            head = lax.broadcasted_iota(jnp.int32, (8, DG), 0)
            ch = lax.broadcasted_iota(jnp.int32, (8, DG), 1)
            spread = jnp.where(ch // 64 == head, 1.0, 0.0).astype(F32)
            dbsp_ref[...] = lax.dot_general(spread, dsv_acc[...], (((1,), (1,)), ((), ())),
                                            precision=HIGHEST, preferred_element_type=F32)

    full = lambda shape: pl.BlockSpec(shape, lambda s: (0,) * len(shape))
    rev = lambda cols: pl.BlockSpec((tm, cols), lambda s: (nt - 1 - s, 0))
    return _call(
        body, name="mix_bwd",
        grid=(nt,),
        in_specs=[rev(D), rev(D), rev(DP), rev(2 * DG), full((8, D)), full((D, DPROJ)), full((4, LANE, LANE)),
                  full((8, DP)), full((4, CHUNK, 2 * CHUNK)), full((4, CHUNK, 2 * CHUNK)), full((CHUNK, DG)),
                  full((DP + DG, D))],
        out_specs=[rev(D), full((D, DPROJ)), full((DP + DG, D)), full((4, LANE, LANE)), full((8, CHUNK, CHUNK)),
                   full((8, CHUNK)), full((8, DP)), full((8, D))],
        out_shape=[_sds((S, D), F32), _sds((D, DPROJ), F32), _sds((DP + DG, D), F32), _sds((4, LANE, LANE), F32),
                   _sds((8, CHUNK, CHUNK), F32), _sds((8, CHUNK), F32), _sds((8, DP), F32), _sds((8, D), F32)],
        scratch_shapes=[pltpu.VMEM((tm + HALO, DP), F32), pltpu.VMEM((CHUNK, DG), F32)],
        compiler_params=_params(("arbitrary",), VMEM_LIMIT),
    )(x, dxo, pooled, zb, modv, win, wpool, vecs, wcat, wtcat, bias, wout)


def _chip_sum(g, rbuf, core):
    _, _, hr, cols = g.shape
    tr = _row_block(hr)

    def body(c_ref, g_ref, r_ref, o_ref):
        o_ref[...] = (g_ref[...] + r_ref[...]).astype(BF16)

    return pl.pallas_call(
        body, name="chip_sum", interpret=False,
        grid_spec=pltpu.PrefetchScalarGridSpec(
            num_scalar_prefetch=1, grid=(NQ, hr // tr),
            in_specs=[pl.BlockSpec((None, None, tr, cols), lambda q, i, c: (q, c[0], i, 0)),
                      pl.BlockSpec((None, tr, cols), lambda q, i, c: (q, i, 0))],
            out_specs=pl.BlockSpec((None, tr, cols), lambda q, i, c: (q, i, 0))),
        out_shape=_sds((NQ, hr, cols), BF16),
        compiler_params=_params(("arbitrary", "arbitrary"), None),
    )(core, g, rbuf)


def _sum4(rbuf):
    _, hr, cols = rbuf.shape
    tr = _row_block(hr)

    def body(r_ref, o_ref):
        acc = r_ref[0].astype(F32)
        for s in range(1, NQ):
            acc = acc + r_ref[s].astype(F32)
        o_ref[...] = acc

    return _call(
        body, name="sum4",
        grid=(hr // tr,),
        in_specs=[pl.BlockSpec((NQ, tr, cols), lambda i: (0, i, 0))],
        out_specs=pl.BlockSpec((tr, cols), lambda i: (i, 0)),
        out_shape=_sds((hr, cols), F32),
        compiler_params=_params(("arbitrary",), None),
    )(rbuf)


def _adamw_call(w, g, m, v):
    rows, cols = w.shape
    tr = _row_block(rows, mult=8)

    def body(w_ref, g_ref, m_ref, v_ref, d_ref, mo_ref, vo_ref):
        d, mn, vn = _adamw(w_ref[...], g_ref[...], m_ref[...], v_ref[...])
        d_ref[...] = d
        mo_ref[...] = mn
        vo_ref[...] = vn

    spec = pl.BlockSpec((tr, cols), lambda i: (i, 0))
    return _call(
        body, name="adamw",
        grid=(rows // tr,),
        in_specs=[spec] * 4, out_specs=[spec] * 3,
        out_shape=[_sds((rows, cols), F32)] * 3,
        compiler_params=_params(("arbitrary",), None),
    )(w, g, m, v)


def _ada_grad_adamw(cact_t, dmod_q, w, m, v):
    rows, cols = w.shape
    tc = 256
    assert cols % tc == 0

    def body(c_ref, d_ref, w_ref, m_ref, v_ref, g_ref, dl_ref, mo_ref, vo_ref):
        g = jnp.dot(c_ref[...], d_ref[...], precision=HIGHEST, preferred_element_type=F32)
        d, mn, vn = _adamw(w_ref[...], g, m_ref[...], v_ref[...])
        g_ref[...] = g
        dl_ref[...] = d
        mo_ref[...] = mn
        vo_ref[...] = vn

    spec = pl.BlockSpec((rows, tc), lambda i: (0, i))
    return _call(
        body, name="ada_grad_adamw",
        grid=(cols // tc,),
        in_specs=[pl.BlockSpec((rows, 8), lambda i: (0, 0)), pl.BlockSpec((8, tc), lambda i: (0, i)),
                  spec, spec, spec],
        out_specs=[spec] * 4,
        out_shape=[_sds((rows, cols), F32)] * 4,
        compiler_params=_params(("arbitrary",), None),
    )(cact_t, dmod_q, w, m, v)


def _me():
    x, y, c = lax.axis_index("x"), lax.axis_index("y"), lax.axis_index("c")
    return x, y, c


_OFFSETS7 = [(dx, dy, dc) for dx in (0, 1) for dy in (0, 1) for dc in (0, 1) if (dx, dy, dc) != (0, 0, 0)]
_CHIP_OFFSETS = [(1, 0), (0, 1), (1, 1)]


def _ada_fwd(c, w_ada_q, b_ada_q):
    ncol = w_ada_q.shape[1]

    def body(c_ref, w_ref, b_ref, cact_ref, modsel_ref, blk, gath, res, parts, send_sems, recv_sems):
        x, y, cc = _me()
        me = 4 * x + 2 * y + cc
        q = 2 * x + y
        cv = c_ref[...]
        ca = cv * jax.nn.sigmoid(cv)
        row = lax.broadcasted_iota(jnp.int32, (8, D), 0)
        blk[...] = jnp.where(row == me, jnp.broadcast_to(ca, (8, D)), 0.0)
        gath[me] = blk[...]
        sends = []
        for k, (dx, dy, dc) in enumerate(_OFFSETS7):
            cp = pltpu.make_async_remote_copy(blk, gath.at[me], send_sems.at[k], recv_sems.at[k],
                                              device_id=(x ^ dx, y ^ dy, cc ^ dc), device_id_type=MESH)
            cp.start()
            sends.append(cp)
        for cp in sends:
            cp.wait_recv()
        cact = gath[0]
        for d in range(1, N_DEV):
            cact = cact + gath[d]
        cact_ref[...] = cact
        res[...] = jnp.dot(cact, w_ref[...], precision=HIGHEST, preferred_element_type=F32) + b_ref[...]
        parts[q] = res[...]
        sends2 = []
        for k, (dx, dy) in enumerate(_CHIP_OFFSETS):
            cp = pltpu.make_async_remote_copy(res, parts.at[q], send_sems.at[7 + k], recv_sems.at[7 + k],
                                              device_id=(x ^ dx, y ^ dy, cc), device_id_type=MESH)
            cp.start()
            sends2.append(cp)
        for cp in sends2:
            cp.wait_recv()
        row2 = lax.broadcasted_iota(jnp.int32, (8, ncol), 0)
        out = jnp.zeros((8, ncol), F32)
        for s in range(NQ):
            mine = jnp.sum(jnp.where(row2 == me, parts[s], 0.0), axis=0, keepdims=True)
            out = out + jnp.where(row2 == s, jnp.broadcast_to(mine, (8, ncol)), 0.0)
        modsel_ref[...] = out
        for cp in sends + sends2:
            cp.wait_send()

    return _call(
        body, name="ada_fwd",
        in_specs=[VMEM, VMEM, VMEM], out_specs=[VMEM, VMEM],
        out_shape=[_sds((8, D), F32), _sds((8, ncol), F32)],
        scratch_shapes=[pltpu.VMEM((8, D), F32), pltpu.VMEM((N_DEV, 8, D), F32), pltpu.VMEM((8, ncol), F32),
                        pltpu.VMEM((NQ, 8, ncol), F32), pltpu.SemaphoreType.DMA((10,)), pltpu.SemaphoreType.DMA((10,))],
        compiler_params=_params(None, VMEM_LIMIT),
    )(c, w_ada_q, b_ada_q)


def _gather_weights(ws):
    n = len(ws)

    def body(*refs):
        ins, outs = refs[:n], refs[n:2 * n]
        send_sems, recv_sems, loc_sems = refs[2 * n:]
        x, y, cc = _me()
        q = 2 * x + y
        pending, locs = [], []
        for w in range(n):
            hr = ins[w].shape[0] // 2
            mine = pl.ds(cc * hr, hr)
            loc = pltpu.make_async_copy(ins[w], outs[w].at[q], loc_sems.at[w])
            loc.start()
            locs.append(loc)
            for j, (dx, dy) in enumerate(_CHIP_OFFSETS):
                cp = pltpu.make_async_remote_copy(ins[w].at[mine], outs[w].at[q, mine],
                                                  send_sems.at[6 * w + j], recv_sems.at[6 * w + j],
                                                  device_id=(x ^ dx, y ^ dy, cc), device_id_type=MESH)
                cp.start()
                pending.append(cp)
        forwards = []
        for w in range(n):
            hr = ins[w].shape[0] // 2
            mine = pl.ds(cc * hr, hr)
            for j, (dx, dy) in enumerate(_CHIP_OFFSETS):
                pq = 2 * (x ^ dx) + (y ^ dy)
                land = outs[w].at[pq, mine]
                pltpu.make_async_remote_copy(land, land, send_sems.at[6 * w + j], recv_sems.at[6 * w + j],
                                             device_id=(x, y, cc), device_id_type=MESH).wait_recv()
                fw = pltpu.make_async_remote_copy(land, land, send_sems.at[6 * w + 3 + j], recv_sems.at[6 * w + 3 + j],
                                                  device_id=(x, y, 1 - cc), device_id_type=MESH)
                fw.start()
                forwards.append(fw)
        for w in range(n):
            hr = ins[w].shape[0] // 2
            other = pl.ds((1 - cc) * hr, hr)
            for j, (dx, dy) in enumerate(_CHIP_OFFSETS):
                pq = 2 * (x ^ dx) + (y ^ dy)
                land = outs[w].at[pq, other]
                pltpu.make_async_remote_copy(land, land, send_sems.at[6 * w + 3 + j], recv_sems.at[6 * w + 3 + j],
                                             device_id=(x, y, 1 - cc), device_id_type=MESH).wait_recv()
        for loc in locs:
            loc.wait()
        for cp in pending + forwards:
            cp.wait_send()

    return _call(
        body, name="gather_weights",
        in_specs=[ANY] * n, out_specs=[ANY] * n,
        out_shape=[_sds((NQ,) + tuple(w.shape), w.dtype) for w in ws],
        scratch_shapes=[pltpu.SemaphoreType.DMA((6 * n,)), pltpu.SemaphoreType.DMA((6 * n,)),
                        pltpu.SemaphoreType.DMA((n,))],
    )(*ws)


def _sibling_halves(gs):
    n = len(gs)

    def body(*refs):
        ins, outs = refs[:n], refs[n:2 * n]
        send_sems, recv_sems = refs[2 * n:]
        x, y, cc = _me()
        cps = []
        for w in range(n):
            cp = pltpu.make_async_remote_copy(ins[w].at[:, 1 - cc], outs[w], send_sems.at[w], recv_sems.at[w],
                                              device_id=(x, y, 1 - cc), device_id_type=MESH)
            cp.start()
            cps.append(cp)
        for cp in cps:
            cp.wait()

    return _call(
        body, name="sibling_halves",
        in_specs=[ANY] * n, out_specs=[ANY] * n,
        out_shape=[_sds((NQ,) + tuple(g.shape[2:]), F32) for g in gs],
        scratch_shapes=[pltpu.SemaphoreType.DMA((n,)), pltpu.SemaphoreType.DMA((n,))],
    )(*gs)


def _chip_exchange(cs):
    n = len(cs)

    def body(*refs):
        ins, outs = refs[:n], refs[n:2 * n]
        send_sems, recv_sems, loc_sems = refs[2 * n:]
        x, y, cc = _me()
        q = 2 * x + y
        cps = []
        for w in range(n):
            loc = pltpu.make_async_copy(ins[w].at[q], outs[w].at[q], loc_sems.at[w])
            loc.start()
            cps.append(loc)
            for j, (dx, dy) in enumerate(_CHIP_OFFSETS):
                pq = 2 * (x ^ dx) + (y ^ dy)
                cp = pltpu.make_async_remote_copy(ins[w].at[pq], outs[w].at[q], send_sems.at[3 * w + j],
                                                  recv_sems.at[3 * w + j],
                                                  device_id=(x ^ dx, y ^ dy, cc), device_id_type=MESH)
                cp.start()
                cps.append(cp)
        for cp in cps:
            cp.wait()

    return _call(
        body, name="chip_exchange",
        in_specs=[ANY] * n, out_specs=[ANY] * n,
        out_shape=[_sds(tuple(c.shape), c.dtype) for c in cs],
        scratch_shapes=[pltpu.SemaphoreType.DMA((3 * n,)), pltpu.SemaphoreType.DMA((3 * n,)),
                        pltpu.SemaphoreType.DMA((n,))],
    )(*cs)


def _sibling_share(fs):
    n = len(fs)

    def body(*refs):
        ins, outs = refs[:n], refs[n:2 * n]
        send_sems, recv_sems, loc_sems = refs[2 * n:]
        x, y, cc = _me()
        cps = []
        for w in range(n):
            loc = pltpu.make_async_copy(ins[w], outs[w].at[cc], loc_sems.at[w])
            loc.start()
            cps.append(loc)
            cp = pltpu.make_async_remote_copy(ins[w], outs[w].at[cc], send_sems.at[w], recv_sems.at[w],
                                              device_id=(x, y, 1 - cc), device_id_type=MESH)
            cp.start()
            cps.append(cp)
        for cp in cps:
            cp.wait()

    return _call(
        body, name="sibling_share",
        in_specs=[ANY] * n, out_specs=[ANY] * n,
        out_shape=[_sds((2,) + tuple(f.shape), F32) for f in fs],
        scratch_shapes=[pltpu.SemaphoreType.DMA((n,)), pltpu.SemaphoreType.DMA((n,)), pltpu.SemaphoreType.DMA((n,))],
    )(*fs)


def _small_allreduce_adamw(g, w, m, v):
    rows = g.shape[0]

    def body(g_ref, w_ref, m_ref, v_ref, gs_ref, d_ref, mo_ref, vo_ref, gath, send_sems, recv_sems, loc_sem):
        x, y, cc = _me()
        me = 4 * x + 2 * y + cc
        loc = pltpu.make_async_copy(g_ref, gath.at[me], loc_sem)
        loc.start()
        cps = []
        for k, (dx, dy, dc) in enumerate(_OFFSETS7):
            cp = pltpu.make_async_remote_copy(g_ref, gath.at[me], send_sems.at[k], recv_sems.at[k],
                                              device_id=(x ^ dx, y ^ dy, cc ^ dc), device_id_type=MESH)
            cp.start()
            cps.append(cp)
        loc.wait()
        for cp in cps:
            cp.wait()
        tot = gath[0]
        for dev in range(1, N_DEV):
            tot = tot + gath[dev]
        gs_ref[...] = tot
        d, mn, vn = _adamw(w_ref[...], tot, m_ref[...], v_ref[...])
        d_ref[...] = d
        mo_ref[...] = mn
        vo_ref[...] = vn

    return _call(
        body, name="small_allreduce_adamw",
        in_specs=[VMEM] * 4, out_specs=[VMEM] * 5,
        out_shape=[_sds((rows, LANE), F32)] * 4 + [_sds((N_DEV, rows, LANE), F32)],
        scratch_shapes=[pltpu.SemaphoreType.DMA((7,)), pltpu.SemaphoreType.DMA((7,)), pltpu.SemaphoreType.DMA],
        compiler_params=_params(None, VMEM_LIMIT),
    )(g, w, m, v)


_SMALL = ["b_ada", "norm_ffn1_g", "norm_mix_g", "pool_scale", "gmlp_ln_g", "gmlp_ln_b", "b_spatial",
          "norm_ffn2_g", "norm_final_g", "w_pool", "w_spatial"]


def _pack(parts):
    blocks, layout, r0 = [], {}, 0
    for name in _SMALL:
        a = parts[name]
        n = a.size
        rows = -(-n // LANE)
        rows8 = -(-rows // 8) * 8
        flat = a.reshape(-1).astype(F32)
        if rows8 * LANE != n:
            flat = jnp.concatenate([flat, jnp.zeros((rows8 * LANE - n,), F32)])
        blocks.append(flat.reshape(rows8, LANE))
        layout[name] = (r0, n, a.shape)
        r0 += rows8
    return jnp.concatenate(blocks, axis=0), layout


def _unpack(packed, layout):
    out = {}
    for name, (r0, n, shape) in layout.items():
        rows = -(-n // LANE)
        out[name] = packed[r0:r0 + rows].reshape(-1)[:n].reshape(shape)
    return out


def _modv(mod9, sub, gain):
    rows = jnp.concatenate([mod9[3 * sub:3 * sub + 3], gain.reshape(1, D), jnp.zeros((4, D), F32)], axis=0)
    return rows


def _local_step(x, target, mod9, gains, wfull, small):
    mv1 = _modv(mod9, 0, gains["norm_ffn1_g"])
    mv2 = _modv(mod9, 1, gains["norm_mix_g"])
    mv3 = _modv(mod9, 2, gains["norm_ffn2_g"])
    wcat, wtcat, bias = _prep_spatial(small["w_spatial"], small["b_spatial"].T)
    wpool = small["w_pool"].astype(BF16)
    vecs = jnp.concatenate([small["pool_scale"].reshape(1, DP), small["gmlp_ln_g"].reshape(1, DG),
                            small["gmlp_ln_b"].reshape(1, DG), jnp.zeros((5, DP), F32)], axis=0)
    gf = jnp.concatenate([small["norm_final_g"].reshape(1, D), jnp.zeros((7, D), F32)], axis=0)

    x1, g1s, u1s = _ffn_fwd(x, mv1, wfull["ffn1_w_in"], wfull["ffn1_w_out"])
    x2, pooled, zb = _mix_fwd(x1, mv2, wfull["w_mix_in"], wpool, vecs, wcat, bias, wfull["w_mix_out"])
    x3, g3s, u3s = _ffn_fwd(x2, mv3, wfull["ffn2_w_in"], wfull["ffn2_w_out"])
    dx3, loss_blk, dgf = _head(x3, target, gf)
    dx2, dwin2, dwout2, vec3 = _ffn_bwd(x2, dx3, g3s, u3s, mv3, wfull["ffn2_w_in"], wfull["ffn2_w_out"])
    (dx1, dwmi, dwmo, dwpool, dwsp, dbsp, v512, vec2) = _mix_bwd(
        x1, dx2, pooled, zb, mv2, wfull["w_mix_in"], wpool, vecs, wcat, wtcat, bias, wfull["w_mix_out"])
    dx0, dwin1, dwout1, vec1 = _ffn_bwd(x, dx1, g1s, u1s, mv1, wfull["ffn1_w_in"], wfull["ffn1_w_out"])

    dmod = jnp.concatenate([vec1[0:3], vec2[0:3], vec3[0:3]], axis=0)
    grads = dict(
        ffn1_w_in=dwin1, ffn1_w_out=dwout1, w_mix_in=dwmi, w_mix_out=dwmo, ffn2_w_in=dwin2, ffn2_w_out=dwout2,
        b_ada=dmod.reshape(1, 9 * D), norm_ffn1_g=vec1[3:4], norm_mix_g=vec2[3:4], norm_ffn2_g=vec3[3:4],
        pool_scale=v512[0:1], gmlp_ln_g=v512[1:2], gmlp_ln_b=v512[2:3], b_spatial=dbsp[None],
        norm_final_g=dgf[0], w_pool=dwpool[None], w_spatial=dwsp[None])
    return loss_blk, dx0, grads


_BIG = ["ffn1_w_in", "ffn1_w_out", "w_mix_in", "w_mix_out", "ffn2_w_in", "ffn2_w_out"]


def kernel(x, c, w_ada, b_ada, norm_ffn1_g, ffn1_w_in, ffn1_w_out, norm_mix_g, w_mix_in, w_pool, pool_scale, gmlp_ln_g, gmlp_ln_b, w_spatial, b_spatial, w_mix_out, norm_ffn2_g, ffn2_w_in, ffn2_w_out, norm_final_g, loss_target, m_w_ada, m_b_ada, m_norm_ffn1_g, m_ffn1_w_in, m_ffn1_w_out, m_norm_mix_g, m_w_mix_in, m_w_pool, m_pool_scale, m_gmlp_ln_g, m_gmlp_ln_b, m_w_spatial, m_b_spatial, m_w_mix_out, m_norm_ffn2_g, m_ffn2_w_in, m_ffn2_w_out, m_norm_final_g, v_w_ada, v_b_ada, v_norm_ffn1_g, v_ffn1_w_in, v_ffn1_w_out, v_norm_mix_g, v_w_mix_in, v_w_pool, v_pool_scale, v_gmlp_ln_g, v_gmlp_ln_b, v_w_spatial, v_b_spatial, v_w_mix_out, v_norm_ffn2_g, v_ffn2_w_in, v_ffn2_w_out, v_norm_final_g):
    names = ["w_ada", "b_ada", "norm_ffn1_g", "ffn1_w_in", "ffn1_w_out", "norm_mix_g", "w_mix_in", "w_pool",
             "pool_scale", "gmlp_ln_g", "gmlp_ln_b", "w_spatial", "b_spatial", "w_mix_out", "norm_ffn2_g",
             "ffn2_w_in", "ffn2_w_out", "norm_final_g"]
    W = dict(zip(names, [w_ada, b_ada, norm_ffn1_g, ffn1_w_in, ffn1_w_out, norm_mix_g, w_mix_in, w_pool, pool_scale,
                         gmlp_ln_g, gmlp_ln_b, w_spatial, b_spatial, w_mix_out, norm_ffn2_g, ffn2_w_in, ffn2_w_out,
                         norm_final_g]))
    M = dict(zip(names, [m_w_ada, m_b_ada, m_norm_ffn1_g, m_ffn1_w_in, m_ffn1_w_out, m_norm_mix_g, m_w_mix_in, m_w_pool,
                         m_pool_scale, m_gmlp_ln_g, m_gmlp_ln_b, m_w_spatial, m_b_spatial, m_w_mix_out, m_norm_ffn2_g,
                         m_ffn2_w_in, m_ffn2_w_out, m_norm_final_g]))
    V = dict(zip(names, [v_w_ada, v_b_ada, v_norm_ffn1_g, v_ffn1_w_in, v_ffn1_w_out, v_norm_mix_g, v_w_mix_in, v_w_pool,
                         v_pool_scale, v_gmlp_ln_g, v_gmlp_ln_b, v_w_spatial, v_b_spatial, v_w_mix_out, v_norm_ffn2_g,
                         v_ffn2_w_in, v_ffn2_w_out, v_norm_final_g]))

    xi, yi, ci = _me()
    q = 2 * xi + yi
    core = ci.astype(jnp.int32).reshape(1)

    ncol = w_ada.shape[2]
    b_q = lax.dynamic_slice_in_dim(b_ada, q * ncol, ncol, axis=1)
    cact_all, modsel = _ada_fwd(c, w_ada[0], b_q)
    mod9 = modsel[:NQ].reshape(9, D)

    quarters = [W[n][0].astype(BF16) for n in _BIG]
    gathered = dict(zip(_BIG, _gather_weights(quarters)))
    wfull = dict(
        ffn1_w_in=gathered["ffn1_w_in"], ffn2_w_in=gathered["ffn2_w_in"],
        ffn1_w_out=gathered["ffn1_w_out"].reshape(2, CH, D), ffn2_w_out=gathered["ffn2_w_out"].reshape(2, CH, D),
        w_mix_in=jnp.transpose(gathered["w_mix_in"], (1, 0, 2)).reshape(D, DPROJ),
        w_mix_out=gathered["w_mix_out"].reshape(DP + DG, D))
    small = dict(w_spatial=w_spatial[0], b_spatial=b_spatial[0], w_pool=w_pool[0], pool_scale=pool_scale[0],
                 gmlp_ln_g=gmlp_ln_g[0], gmlp_ln_b=gmlp_ln_b[0], norm_final_g=norm_final_g)
    gains = dict(norm_ffn1_g=norm_ffn1_g[0], norm_mix_g=norm_mix_g[0], norm_ffn2_g=norm_ffn2_g[0])

    loss_blk, grad_x, grads = _local_step(x[0], loss_target[0], mod9, gains, wfull, small)
    loss = lax.psum(loss_blk[0, 0], ("x", "y", "c"))

    qcols = w_mix_in.shape[2]
    views = dict(
        ffn1_w_in=grads["ffn1_w_in"].reshape(NQ, 2, D // 2, CH),
        ffn2_w_in=grads["ffn2_w_in"].reshape(NQ, 2, D // 2, CH),
        ffn1_w_out=grads["ffn1_w_out"].reshape(NQ, 2, DFF // 8, D),
        ffn2_w_out=grads["ffn2_w_out"].reshape(NQ, 2, DFF // 8, D),
        w_mix_in=jnp.transpose(grads["w_mix_in"].reshape(D, NQ, qcols), (1, 0, 2)).reshape(NQ, 2, D // 2, qcols),
        w_mix_out=grads["w_mix_out"].reshape(NQ, 2, (DP + DG) // 8, D))
    gviews = [views[n] for n in _BIG]
    sib = _sibling_halves(gviews)
    csums = [_chip_sum(g, r, core) for g, r in zip(gviews, sib)]
    exch = _chip_exchange(csums)
    halves = [_sum4(e) for e in exch]
    fulls = _sibling_share(halves)
    out_g, out_d, out_m, out_v = {}, {}, {}, {}
    for n, f in zip(_BIG, fulls):
        wq = W[n][0]
        g2 = f.reshape(wq.shape)
        d, mn, vn = _adamw_call(wq, g2, M[n][0], V[n][0])
        out_g[n], out_d[n], out_m[n], out_v[n] = g2[None], d[None], mn[None], vn[None]

    gp, layout = _pack({n: grads[n] for n in _SMALL})
    wp, _ = _pack({n: W[n] for n in _SMALL})
    mp, _ = _pack({n: M[n] for n in _SMALL})
    vp, _ = _pack({n: V[n] for n in _SMALL})
    gs, dl, mo, vo, gath = _small_allreduce_adamw(gp, wp, mp, vp)
    for packed, dst in ((gs, out_g), (dl, out_d), (mo, out_m), (vo, out_v)):
        for n, a in _unpack(packed, layout).items():
            dst[n] = a.reshape(W[n].shape)

    r0, nb, _ = layout["b_ada"]
    dmod_all = gath[:, r0:r0 + nb // LANE, :].reshape(N_DEV, nb)
    dmod_q = lax.dynamic_slice_in_dim(dmod_all, q * ncol, ncol, axis=1)
    ga, da, ma, va = _ada_grad_adamw(cact_all.T, dmod_q, w_ada[0], m_w_ada[0], v_w_ada[0])
    out_g["w_ada"], out_d["w_ada"], out_m["w_ada"], out_v["w_ada"] = ga[None], da[None], ma[None], va[None]

    return (loss, grad_x[None], *[out_g[n] for n in names], *[out_d[n] for n in names],
            *[out_m[n] for n in names], *[out_v[n] for n in names])
```

```python
import functools
import math

import jax
import jax.numpy as jnp
from jax import lax
from jax.experimental import pallas as pl
from jax.experimental.pallas import tpu as pltpu

F32 = jnp.float32
BF16 = jnp.bfloat16
MESH = pl.DeviceIdType.MESH
HIGHEST = lax.Precision.HIGHEST

EPS = 1e-6
D = 1024
DFF = 2816
CH = DFF // 2
NQ = 4
DP = 512
DG = 512
DPROJ = DP + 2 * DG
POOL_WINDOWS = (2, 4, 8, 16)
HALO = 16
CHUNK = 128
LANE = 128
N_DEV = 8

ADAM_LR = 0.001
ADAM_B1 = 0.9
ADAM_B2 = 0.999
ADAM_EPS = 1e-08
ADAM_WD = 0.01
ADAM_STEP = 10

VMEM_LIMIT = 60 * 1024 * 1024

TM_FFN_FWD = 512
TM_FFN_BWD = 256
TM_MIX = 256
TM_HEAD = 512


def _call(body, **kw):
    return pl.pallas_call(body, interpret=False, **kw)


def _params(sem=None, vmem=None):
    return pltpu.CompilerParams(dimension_semantics=sem, vmem_limit_bytes=vmem)


def _sds(shape, dtype):
    return jax.ShapeDtypeStruct(shape, dtype)


ANY = pl.BlockSpec(memory_space=pl.ANY)
VMEM = pl.BlockSpec(memory_space=pltpu.VMEM)
SMEM = pl.BlockSpec(memory_space=pltpu.SMEM)


def _norm_mod(x, gn, sc, sh):
    r = lax.rsqrt(jnp.mean(x * x, axis=-1, keepdims=True) + EPS)
    xn = x * r
    hp = xn * gn
    return r, xn, hp, hp * (1.0 + sc) + sh


def _norm_mod_bwd(dh, r, xn, hp, gn, sc):
    one_sc = 1.0 + sc
    dsh = jnp.sum(dh, axis=0, keepdims=True)
    dsc = jnp.sum(dh * hp, axis=0, keepdims=True)
    dgn = jnp.sum(dh * one_sc * xn, axis=0, keepdims=True)
    dxn = dh * (gn * one_sc)
    dx = r * (dxn - xn * jnp.mean(dxn * xn, axis=-1, keepdims=True))
    return dsh, dsc, dgn, dx


def _dot(a, b):
    return jnp.dot(a, b, preferred_element_type=F32)


def _dot_nt(a, b):
    return lax.dot_general(a, b, (((1,), (1,)), ((), ())), preferred_element_type=F32)


def _dot_tn(a, b):
    return lax.dot_general(a, b, (((0,), (0,)), ((), ())), preferred_element_type=F32)


_GELU_C = math.sqrt(2.0 / math.pi)
_GELU_A = 0.044715


def _gelu_fwd_bwd(x):
    x2 = x * x
    t = jnp.tanh(_GELU_C * (x + _GELU_A * x * x2))
    g = 0.5 * x * (1.0 + t)
    dg = 0.5 * (1.0 + t) + 0.5 * x * (1.0 - t * t) * (_GELU_C * (1.0 + 3.0 * _GELU_A * x2))
    return g, dg


def _adamw(w, g, m, v):
    m = ADAM_B1 * m + (1.0 - ADAM_B1) * g
    v = ADAM_B2 * v + (1.0 - ADAM_B2) * (g * g)
    m_hat = m / (1.0 - ADAM_B1 ** ADAM_STEP)
    v_hat = v / (1.0 - ADAM_B2 ** ADAM_STEP)
    delta = -ADAM_LR * (m_hat / (jnp.sqrt(v_hat) + ADAM_EPS) + ADAM_WD * w)
    return delta, m, v


def _row_block(rows, cap=256, mult=16):
    best = None
    for t in range(mult, min(rows, cap) + 1, mult):
        if rows % t == 0:
            best = t
    assert best is not None, rows
    return best


def _ffn_fwd(x, modv, win, wout):
    S = x.shape[0]
    tm = TM_FFN_FWD
    nt = S // tm

    def body(x_ref, mod_ref, wg_ref, wu_ref, wo_ref, xo_ref, gs_ref, us_ref, h_scr, acc_scr):
        j = pl.program_id(1)

        @pl.when(j == 0)
        def _():
            _, _, _, h = _norm_mod(x_ref[...], mod_ref[3:4, :], mod_ref[1:2, :], mod_ref[0:1, :])
            h_scr[...] = h.astype(BF16)
            acc_scr[...] = jnp.zeros_like(acc_scr)

        h = h_scr[...]
        g = _dot(h, wg_ref[...]).astype(BF16)
        u = _dot(h, wu_ref[...]).astype(BF16)
        gs_ref[...] = g
        us_ref[...] = u
        gf = g.astype(F32)
        a = (gf * jax.nn.sigmoid(gf) * u.astype(F32)).astype(BF16)
        acc_scr[...] += _dot(a, wo_ref[...])

        @pl.when(j == 1)
        def _():
            xo_ref[...] = x_ref[...] + (0.5 * mod_ref[2:3, :]) * acc_scr[...]

    return _call(
        body, name="ffn_fwd",
        grid=(nt, 2),
        in_specs=[
            pl.BlockSpec((tm, D), lambda i, j: (i, 0)),
            pl.BlockSpec((8, D), lambda i, j: (0, 0)),
            pl.BlockSpec((None, D, CH), lambda i, j: (j, 0, 0)),
            pl.BlockSpec((None, D, CH), lambda i, j: (2 + j, 0, 0)),
            pl.BlockSpec((None, CH, D), lambda i, j: (j, 0, 0)),
        ],
        out_specs=[
            pl.BlockSpec((tm, D), lambda i, j: (i, 0)),
            pl.BlockSpec((tm, CH), lambda i, j: (i, j)),
            pl.BlockSpec((tm, CH), lambda i, j: (i, j)),
        ],
        out_shape=[_sds((S, D), F32), _sds((S, DFF), BF16), _sds((S, DFF), BF16)],
        scratch_shapes=[pltpu.VMEM((tm, D), BF16), pltpu.VMEM((tm, D), F32)],
        compiler_params=_params(("arbitrary", "arbitrary"), VMEM_LIMIT),
    )(x, modv, win, win, wout)


def _ffn_bwd(x, dxo, gs, us, modv, win, wout):
    S = x.shape[0]
    tm = TM_FFN_BWD
    nt = S // tm
    assert nt >= 2

    def body(x_ref, dxo_ref, gs_ref, us_ref, mod_ref, wg_ref, wu_ref, wo_ref,
             dx_ref, dwin_ref, dwout_ref, vec_ref, dhbuf_ref, accg, accu, accw, dh_st, dh_ld, sems):
        j = pl.program_id(0)
        i = pl.program_id(1)
        store = lambda t: pltpu.make_async_copy(dh_st, dhbuf_ref.at[t], sems.at[3])
        load = lambda t: pltpu.make_async_copy(dhbuf_ref.at[t], dh_ld, sems.at[4])

        @pl.when(i == 0)
        def _():
            accg[...] = jnp.zeros_like(accg)
            accu[...] = jnp.zeros_like(accu)
            accw[...] = jnp.zeros_like(accw)

        @pl.when((i == 0) & (j == 0))
        def _():
            vec_ref[...] = jnp.zeros_like(vec_ref)

        @pl.when((j == 0) & (i > 0))
        def _():
            store(i - 1).wait()

        @pl.when((j == 1) & (i == 0))
        def _():
            store(nt - 1).wait()

        @pl.when(j == 1)
        def _():
            load(i).start()

        gn, sc, sh, gate = mod_ref[3:4, :], mod_ref[1:2, :], mod_ref[0:1, :], mod_ref[2:3, :]
        x = x_ref[...]
        r, xn, hp, h = _norm_mod(x, gn, sc, sh)
        hb = h.astype(BF16)
        dxo = dxo_ref[...]
        dy = (dxo * (0.5 * gate)).astype(BF16)
        g = gs_ref[...].astype(F32)
        u = us_ref[...].astype(F32)
        sig = jax.nn.sigmoid(g)
        sl = g * sig
        a = (sl * u).astype(BF16)
        accw[...] += _dot_tn(a, dxo.astype(BF16))
        da = _dot_nt(dy, wo_ref[...])
        dg = (da * u * (sig * (1.0 + g * (1.0 - sig)))).astype(BF16)
        du = (da * sl).astype(BF16)
        accg[...] += _dot_tn(hb, dg)
        accu[...] += _dot_tn(hb, du)
        dhp = _dot_nt(dg, wg_ref[...]) + _dot_nt(du, wu_ref[...])

        @pl.when(j == 0)
        def _():
            dh_st[...] = dhp
            store(i).start()

        @pl.when(j == 1)
        def _():
            load(i).wait()
            dh = dh_ld[...] + dhp
            dsh, dsc, dgn, dxin = _norm_mod_bwd(dh, r, xn, hp, gn, sc)
            vec_ref[0:1, :] += dsh
            vec_ref[1:2, :] += dsc
            vec_ref[3:4, :] += dgn
            dx_ref[...] = dxo + dxin

        @pl.when(i == nt - 1)
        def _():
            gw = accw[...]
            vec_ref[2:3, :] += 0.5 * jnp.sum(wo_ref[...].astype(F32) * gw, axis=0, keepdims=True)
            accw[...] = gw * (0.5 * gate)
            cps = [pltpu.make_async_copy(accg, dwin_ref.at[j], sems.at[0]),
                   pltpu.make_async_copy(accu, dwin_ref.at[2 + j], sems.at[1]),
                   pltpu.make_async_copy(accw, dwout_ref.at[j], sems.at[2])]
            for cp in cps:
                cp.start()
            for cp in cps:
                cp.wait()

    dx, dwin, dwout, vec, _ = _call(
        body, name="ffn_bwd",
        grid=(2, nt),
        in_specs=[
            pl.BlockSpec((tm, D), lambda j, i: (i, 0)),
            pl.BlockSpec((tm, D), lambda j, i: (i, 0)),
            pl.BlockSpec((tm, CH), lambda j, i: (i, j)),
            pl.BlockSpec((tm, CH), lambda j, i: (i, j)),
            pl.BlockSpec((8, D), lambda j, i: (0, 0)),
            pl.BlockSpec((None, D, CH), lambda j, i: (j, 0, 0)),
            pl.BlockSpec((None, D, CH), lambda j, i: (2 + j, 0, 0)),
            pl.BlockSpec((None, CH, D), lambda j, i: (j, 0, 0)),
        ],
        out_specs=[
            pl.BlockSpec((tm, D), lambda j, i: (i * j, 0)),
            ANY, ANY,
            pl.BlockSpec((8, D), lambda j, i: (0, 0)),
            ANY,
        ],
        out_shape=[_sds((S, D), F32), _sds((NQ, D, CH), F32), _sds((2, CH, D), F32), _sds((8, D), F32),
                   _sds((nt, tm, D), F32)],
        scratch_shapes=[pltpu.VMEM((D, CH), F32), pltpu.VMEM((D, CH), F32), pltpu.VMEM((CH, D), F32),
                        pltpu.VMEM((tm, D), F32), pltpu.VMEM((tm, D), F32), pltpu.SemaphoreType.DMA((5,))],
        compiler_params=_params(("arbitrary", "arbitrary"), VMEM_LIMIT),
    )(x, dxo, gs, us, modv, win, win, wout)
    return dx, dwin, dwout, vec


def _head(x, target, gf):
    S = x.shape[0]
    tm = TM_HEAD
    nt = S // tm

    def body(x_ref, t_ref, g_ref, dx_ref, loss_ref, dg_ref):
        i = pl.program_id(0)

        @pl.when(i == 0)
        def _():
            loss_ref[...] = jnp.zeros_like(loss_ref)
            dg_ref[...] = jnp.zeros_like(dg_ref)

        x = x_ref[...]
        gf_ = g_ref[0:1, :]
        r = lax.rsqrt(jnp.mean(x * x, axis=-1, keepdims=True) + EPS)
        xn = x * r
        err = xn * gf_ - t_ref[...]
        loss_ref[...] += (0.5 / D) * jnp.sum(err * err)
        dy = err * (1.0 / D)
        dg_ref[0:1, :] += jnp.sum(dy * xn, axis=0, keepdims=True)
        dxn = dy * gf_
        dx_ref[...] = r * (dxn - xn * jnp.mean(dxn * xn, axis=-1, keepdims=True))

    return _call(
        body, name="head",
        grid=(nt,),
        in_specs=[pl.BlockSpec((tm, D), lambda i: (i, 0)), pl.BlockSpec((tm, D), lambda i: (i, 0)),
                  pl.BlockSpec((8, D), lambda i: (0, 0))],
        out_specs=[pl.BlockSpec((tm, D), lambda i: (i, 0)), pl.BlockSpec((8, LANE), lambda i: (0, 0)),
                   pl.BlockSpec((8, D), lambda i: (0, 0))],
        out_shape=[_sds((S, D), F32), _sds((8, LANE), F32), _sds((8, D), F32)],
        compiler_params=_params(("arbitrary",), VMEM_LIMIT),
    )(x, target, gf)


def _prep_spatial(w_spatial, b_spatial_t):
    def body(w_ref, b_ref, wcat_ref, wtcat_ref, bias_ref):
        row = lax.broadcasted_iota(jnp.int32, (CHUNK, CHUNK), 0)
        col = lax.broadcasted_iota(jnp.int32, (CHUNK, CHUNK), 1)
        tril = col <= row
        for p in range(4):
            wa = jnp.where(tril, w_ref[2 * p], 0.0)
            wb = jnp.where(tril, w_ref[2 * p + 1], 0.0)
            wcat_ref[p] = jnp.concatenate([wa, wb], axis=1).astype(BF16)
            wtcat_ref[p] = jnp.concatenate([wa.T, wb.T], axis=1).astype(BF16)
        head = lax.broadcasted_iota(jnp.int32, (8, DG), 0)
        ch = lax.broadcasted_iota(jnp.int32, (8, DG), 1)
        spread = jnp.where(ch // 64 == head, 1.0, 0.0).astype(F32)
        bias_ref[...] = jnp.dot(b_ref[...], spread, precision=HIGHEST, preferred_element_type=F32)

    return _call(
        body, name="prep_spatial",
        in_specs=[VMEM, VMEM], out_specs=[VMEM, VMEM, VMEM],
        out_shape=[_sds((4, CHUNK, 2 * CHUNK), BF16), _sds((4, CHUNK, 2 * CHUNK), BF16), _sds((CHUNK, DG), F32)],
    )(w_spatial, b_spatial_t)


def _pair_rhs(blocks):
    lane = lax.broadcasted_iota(jnp.int32, (CHUNK, LANE), 1)
    lo = lane < 64
    top = jnp.concatenate([jnp.where(lo, b, 0.0) for b in blocks], axis=1)
    bot = jnp.concatenate([jnp.where(lo, 0.0, b) for b in blocks], axis=1)
    return top, bot


def _gmlp_branch(zb, vecs, wcat_ref, bias_ref, nchunks):
    z, dz = _gelu_fwd_bwd(zb)
    u = z[:, :DG]
    v = z[:, DG:]
    ln_g, ln_b = vecs[1:2, :], vecs[2:3, :]
    mu = jnp.mean(v, axis=-1, keepdims=True)
    vc = v - mu
    rstd = lax.rsqrt(jnp.mean(vc * vc, axis=-1, keepdims=True) + EPS)
    vhat = vc * rstd
    vl = vhat * ln_g + ln_b
    sv_cols = []
    for p in range(4):
        blocks = [vl[k * CHUNK:(k + 1) * CHUNK, p * LANE:(p + 1) * LANE] for k in range(nchunks)]
        top, bot = _pair_rhs(blocks)
        rhs = jnp.concatenate([top, bot], axis=0).astype(BF16)
        out = _dot(wcat_ref[p], rhs)
        bias = bias_ref[:, p * LANE:(p + 1) * LANE]
        sv_cols.append(jnp.concatenate([out[:, k * LANE:(k + 1) * LANE] + bias for k in range(nchunks)], axis=0))
    sv = jnp.concatenate(sv_cols, axis=1)
    return dict(u=u, dz=dz, rstd=rstd, vhat=vhat, vl=vl, sv=sv, yb=u * sv)


def _mix_fwd(x, modv, win, wpool, vecs, wcat, bias, wout):
    S = x.shape[0]
    tm = TM_MIX
    nt = S // tm
    nchunks = tm // CHUNK

    def body(x_ref, mod_ref, win_ref, wpool_ref, vec_ref, wcat_ref, bias_ref, wout_ref,
             xo_ref, pooled_ref, zb_ref, ext):
        i = pl.program_id(0)
        x = x_ref[...]
        _, _, _, h = _norm_mod(x, mod_ref[3:4, :], mod_ref[1:2, :], mod_ref[0:1, :])
        proj = _dot(h.astype(BF16), win_ref[...])
        xa = proj[:, :DP]
        zb = proj[:, DP:]
        zb_ref[...] = zb

        @pl.when(i == 0)
        def _():
            ext[0:HALO, :] = jnp.zeros((HALO, DP), F32)

        ext[HALO:HALO + tm, :] = xa
        pos = i * tm + lax.broadcasted_iota(jnp.int32, (tm, 1), 0)
        vecs = vec_ref[...]
        ya_cols = []
        pooled_cols = []
        for gi, w in enumerate(POOL_WINDOWS):
            cols = slice(gi * LANE, (gi + 1) * LANE)
            s = xa[:, cols]
            for k in range(1, w):
                s = s + ext[HALO - k:HALO - k + tm, cols]
            cnt = jnp.minimum(pos + 1, w).astype(F32)
            pooled = (s / cnt - xa[:, cols]).astype(BF16)
            pooled_cols.append(pooled)
            ya_cols.append(_dot(pooled, wpool_ref[gi]) * vecs[0:1, cols])
        pooled_ref[...] = jnp.concatenate(pooled_cols, axis=1)
        ext[0:HALO, :] = ext[tm:tm + HALO, :]

        gm = _gmlp_branch(zb, vecs, wcat_ref, bias_ref, nchunks)
        cat = jnp.concatenate(ya_cols + [gm["yb"]], axis=1).astype(BF16)
        xo_ref[...] = x + mod_ref[2:3, :] * _dot(cat, wout_ref[...])

    full = lambda shape: pl.BlockSpec(shape, lambda i: (0,) * len(shape))
    return _call(
        body, name="mix_fwd",
        grid=(nt,),
        in_specs=[pl.BlockSpec((tm, D), lambda i: (i, 0)), full((8, D)), full((D, DPROJ)),
                  full((4, LANE, LANE)), full((8, DP)), full((4, CHUNK, 2 * CHUNK)), full((CHUNK, DG)),
                  full((DP + DG, D))],
        out_specs=[pl.BlockSpec((tm, D), lambda i: (i, 0)), pl.BlockSpec((tm, DP), lambda i: (i, 0)),
                   pl.BlockSpec((tm, 2 * DG), lambda i: (i, 0))],
        out_shape=[_sds((S, D), F32), _sds((S, DP), BF16), _sds((S, 2 * DG), F32)],
        scratch_shapes=[pltpu.VMEM((tm + HALO, DP), F32)],
        compiler_params=_params(("arbitrary",), VMEM_LIMIT),
    )(x, modv, win, wpool, vecs, wcat, bias, wout)


def _mix_bwd(x, dxo, pooled, zb, modv, win, wpool, vecs, wcat, wtcat, bias, wout):
    S = x.shape[0]
    tm = TM_MIX
    nt = S // tm
    nchunks = tm // CHUNK

    def body(x_ref, dxo_ref, pooled_ref, zb_ref, mod_ref, win_ref, wpool_ref, vec_ref, wcat_ref, wtcat_ref,
             bias_ref, wout_ref,
             dx_ref, dwin_ref, dwout_ref, dwpool_ref, dwsp_ref, dbsp_ref, v512_ref, vd_ref, qext, dsv_acc):
        step = pl.program_id(0)
        tile = nt - 1 - step

        @pl.when(step == 0)
        def _():
            dwin_ref[...] = jnp.zeros_like(dwin_ref)
            dwout_ref[...] = jnp.zeros_like(dwout_ref)
            dwpool_ref[...] = jnp.zeros_like(dwpool_ref)
            dwsp_ref[...] = jnp.zeros_like(dwsp_ref)
            v512_ref[...] = jnp.zeros_like(v512_ref)
            vd_ref[...] = jnp.zeros_like(vd_ref)
            dsv_acc[...] = jnp.zeros_like(dsv_acc)
            qext[tm:tm + HALO, :] = jnp.zeros((HALO, DP), F32)

        gn, sc, sh, gate = mod_ref[3:4, :], mod_ref[1:2, :], mod_ref[0:1, :], mod_ref[2:3, :]
        vecs = vec_ref[...]
        x = x_ref[...]
        r, xn, hp, h = _norm_mod(x, gn, sc, sh)
        hb = h.astype(BF16)
        dxo = dxo_ref[...]

        pooled = pooled_ref[...]
        mixed_cols = [_dot(pooled[:, gi * LANE:(gi + 1) * LANE], wpool_ref[gi]) for gi in range(4)]
        mixed = jnp.concatenate(mixed_cols, axis=1)
        scale = vecs[0:1, :]
        gm = _gmlp_branch(zb_ref[...], vecs, wcat_ref, bias_ref, nchunks)
        cat = jnp.concatenate([mixed * scale, gm["yb"]], axis=1).astype(BF16)

        dwout_ref[...] += _dot_tn(cat, dxo.astype(BF16))
        dcat = _dot_nt((dxo * gate).astype(BF16), wout_ref[...])
        dya = dcat[:, :DP]
        dyb = dcat[:, DP:]

        v512_ref[0:1, :] += jnp.sum(dya * mixed, axis=0, keepdims=True)
        dmixed = (dya * scale).astype(BF16)
        pos = tile * tm + lax.broadcasted_iota(jnp.int32, (tm, 1), 0)
        dpooled_cols = []
        for gi, w in enumerate(POOL_WINDOWS):
            cols = slice(gi * LANE, (gi + 1) * LANE)
            dp = _dot_nt(dmixed[:, cols], wpool_ref[gi])
            dwpool_ref[gi] += _dot_tn(pooled[:, cols], dmixed[:, cols])
            cnt = jnp.minimum(pos + 1, w).astype(F32)
            qext[0:tm, cols] = dp / cnt
            dpooled_cols.append(dp)
        dxa_cols = []
        for gi, w in enumerate(POOL_WINDOWS):
            cols = slice(gi * LANE, (gi + 1) * LANE)
            s = qext[0:tm, cols]
            for k in range(1, w):
                s = s + qext[k:k + tm, cols]
            dxa_cols.append(s - dpooled_cols[gi])
        qext[tm:tm + HALO, :] = qext[0:HALO, :]

        u, sv, vl = gm["u"], gm["sv"], gm["vl"]
        du = dyb * sv
        dsv = dyb * u
        dvl_cols = []
        for p in range(4):
            cols = slice(p * LANE, (p + 1) * LANE)
            dblocks = [dsv[k * CHUNK:(k + 1) * CHUNK, cols] for k in range(nchunks)]
            vblocks = [vl[k * CHUNK:(k + 1) * CHUNK, cols] for k in range(nchunks)]
            tot = dblocks[0]
            for b in dblocks[1:]:
                tot = tot + b
            dsv_acc[:, cols] += tot
            top, bot = _pair_rhs(dblocks)
            out = _dot(wtcat_ref[p], jnp.concatenate([top, bot], axis=0).astype(BF16))
            dvl_cols.append(jnp.concatenate([out[:, k * LANE:(k + 1) * LANE] for k in range(nchunks)], axis=0))
            vcat = jnp.concatenate(vblocks, axis=1).astype(BF16)
            dwsp_ref[2 * p] += _dot_nt(top.astype(BF16), vcat)
            dwsp_ref[2 * p + 1] += _dot_nt(bot.astype(BF16), vcat)
        dvl = jnp.concatenate(dvl_cols, axis=1)
        vhat, rstd = gm["vhat"], gm["rstd"]
        v512_ref[1:2, :] += jnp.sum(dvl * vhat, axis=0, keepdims=True)
        v512_ref[2:3, :] += jnp.sum(dvl, axis=0, keepdims=True)
        dvh = dvl * vecs[1:2, :]
        dv = rstd * (dvh - jnp.mean(dvh, axis=-1, keepdims=True)
                     - vhat * jnp.mean(dvh * vhat, axis=-1, keepdims=True))
        dzb = jnp.concatenate([du, dv], axis=1) * gm["dz"]

        dproj = jnp.concatenate(dxa_cols + [dzb], axis=1).astype(BF16)
        dwin_ref[...] += _dot_tn(hb, dproj)
        dh = _dot_nt(dproj, win_ref[...])
        dsh, dsc, dgn, dxin = _norm_mod_bwd(dh, r, xn, hp, gn, sc)
        vd_ref[0:1, :] += dsh
        vd_ref[1:2, :] += dsc
        vd_ref[3:4, :] += dgn
        dx_ref[...] = dxo + dxin

        @pl.when(step == nt - 1)
        def _():
            gw = dwout_ref[...]
            vd_ref[2:3, :] += jnp.sum(wout_ref[...].astype(F32) * gw, axis=0, keepdims=True)
            dwout_ref[...] = gw * gate
            row = lax.broadcasted_iota(jnp.int32, (CHUNK, CHUNK), 0)
            col = lax.broadcasted_iota(jnp.int32, (CHUNK, CHUNK), 1)
            for hh in range(8):
                dwsp_ref[hh] = jnp.where(col <= row, dwsp_ref[hh], 0.0)
            head = lax.broadcasted_iota(jnp.int32, (8, DG), 0)
            ch = lax.broadcasted_iota(jnp.int32, (8, DG), 1)
            spread = jnp.where(ch // 64 == head, 1.0, 0.0).astype(F32)
            dbsp_ref[...] = lax.dot_general(spread, dsv_acc[...], (((1,), (1,)), ((), ())),
                                            precision=HIGHEST, preferred_element_type=F32)

    full = lambda shape: pl.BlockSpec(shape, lambda s: (0,) * len(shape))
    rev = lambda cols: pl.BlockSpec((tm, cols), lambda s: (nt - 1 - s, 0))
    return _call(
        body, name="mix_bwd",
        grid=(nt,),
        in_specs=[rev(D), rev(D), rev(DP), rev(2 * DG), full((8, D)), full((D, DPROJ)), full((4, LANE, LANE)),
                  full((8, DP)), full((4, CHUNK, 2 * CHUNK)), full((4, CHUNK, 2 * CHUNK)), full((CHUNK, DG)),
                  full((DP + DG, D))],
        out_specs=[rev(D), full((D, DPROJ)), full((DP + DG, D)), full((4, LANE, LANE)), full((8, CHUNK, CHUNK)),
                   full((8, CHUNK)), full((8, DP)), full((8, D))],
        out_shape=[_sds((S, D), F32), _sds((D, DPROJ), F32), _sds((DP + DG, D), F32), _sds((4, LANE, LANE), F32),
                   _sds((8, CHUNK, CHUNK), F32), _sds((8, CHUNK), F32), _sds((8, DP), F32), _sds((8, D), F32)],
        scratch_shapes=[pltpu.VMEM((tm + HALO, DP), F32), pltpu.VMEM((CHUNK, DG), F32)],
        compiler_params=_params(("arbitrary",), VMEM_LIMIT),
    )(x, dxo, pooled, zb, modv, win, wpool, vecs, wcat, wtcat, bias, wout)


def _chip_sum(g, rbuf, core):
    _, _, hr, cols = g.shape
    tr = _row_block(hr)

    def body(c_ref, g_ref, r_ref, o_ref):
        o_ref[...] = (g_ref[...] + r_ref[...]).astype(BF16)

    return pl.pallas_call(
        body, name="chip_sum", interpret=False,
        grid_spec=pltpu.PrefetchScalarGridSpec(
            num_scalar_prefetch=1, grid=(NQ, hr // tr),
            in_specs=[pl.BlockSpec((None, None, tr, cols), lambda q, i, c: (q, c[0], i, 0)),
                      pl.BlockSpec((None, tr, cols), lambda q, i, c: (q, i, 0))],
            out_specs=pl.BlockSpec((None, tr, cols), lambda q, i, c: (q, i, 0))),
        out_shape=_sds((NQ, hr, cols), BF16),
        compiler_params=_params(("arbitrary", "arbitrary"), None),
    )(core, g, rbuf)


def _sum4(cs, rbuf, chip):
    _, hr, cols = rbuf.shape
    tr = _row_block(hr)

    def body(q_ref, c_ref, r1_ref, r2_ref, r3_ref, o_ref):
        acc = c_ref[...].astype(F32)
        for r in (r1_ref, r2_ref, r3_ref):
            acc = acc + r[...].astype(F32)
        o_ref[...] = acc

    slot = lambda k: pl.BlockSpec((None, tr, cols), lambda i, q: ((q[0] + k) % NQ, i, 0))
    return pl.pallas_call(
        body, name="sum4", interpret=False,
        grid_spec=pltpu.PrefetchScalarGridSpec(
            num_scalar_prefetch=1, grid=(hr // tr,),
            in_specs=[slot(0), slot(1), slot(2), slot(3)],
            out_specs=pl.BlockSpec((tr, cols), lambda i, q: (i, 0))),
        out_shape=_sds((hr, cols), F32),
        compiler_params=_params(("arbitrary",), None),
    )(chip, cs, rbuf, rbuf, rbuf)


def _adamw_halves(w, own, recv, m, v, core):
    rows, cols = w.shape
    hr = rows // 2
    tr = _row_block(hr, mult=8)
    nb = hr // tr

    def body(c_ref, w_ref, own_ref, recv_ref, m_ref, v_ref, g_ref, d_ref, mo_ref, vo_ref):
        g = jnp.where(pl.program_id(0) == c_ref[0], own_ref[...], recv_ref[...])
        d, mn, vn = _adamw(w_ref[...], g, m_ref[...], v_ref[...])
        g_ref[...] = g
        d_ref[...] = d
        mo_ref[...] = mn
        vo_ref[...] = vn

    full = pl.BlockSpec((tr, cols), lambda h, i, c: (h * nb + i, 0))
    half = pl.BlockSpec((tr, cols), lambda h, i, c: (i, 0))
    return pl.pallas_call(
        body, name="adamw_halves", interpret=False,
        grid_spec=pltpu.PrefetchScalarGridSpec(
            num_scalar_prefetch=1, grid=(2, nb),
            in_specs=[full, half, half, full, full],
            out_specs=[full] * 4),
        out_shape=[_sds((rows, cols), F32)] * 4,
        compiler_params=_params(("arbitrary", "arbitrary"), None),
    )(core, w, own, recv, m, v)


def _cast_place(w, chip):
    rows, cols = w.shape
    tr = _row_block(rows)

    def body(q_ref, w_ref, o_ref):
        o_ref[...] = w_ref[...].astype(BF16)

    return pl.pallas_call(
        body, name="cast_place", interpret=False,
        grid_spec=pltpu.PrefetchScalarGridSpec(
            num_scalar_prefetch=1, grid=(rows // tr,),
            in_specs=[pl.BlockSpec((tr, cols), lambda i, q: (i, 0))],
            out_specs=pl.BlockSpec((None, tr, cols), lambda i, q: (q[0], i, 0))),
        out_shape=_sds((NQ, rows, cols), BF16),
        compiler_params=_params(("arbitrary",), None),
    )(chip, w)


def _ada_grad_adamw(cact_t, dmod_q, w, m, v):
    rows, cols = w.shape
    tc = 256
    assert cols % tc == 0

    def body(c_ref, d_ref, w_ref, m_ref, v_ref, g_ref, dl_ref, mo_ref, vo_ref):
        g = jnp.dot(c_ref[...], d_ref[...], precision=HIGHEST, preferred_element_type=F32)
        d, mn, vn = _adamw(w_ref[...], g, m_ref[...], v_ref[...])
        g_ref[...] = g
        dl_ref[...] = d
        mo_ref[...] = mn
        vo_ref[...] = vn

    spec = pl.BlockSpec((rows, tc), lambda i: (0, i))
    return _call(
        body, name="ada_grad_adamw",
        grid=(cols // tc,),
        in_specs=[pl.BlockSpec((rows, 8), lambda i: (0, 0)), pl.BlockSpec((8, tc), lambda i: (0, i)),
                  spec, spec, spec],
        out_specs=[spec] * 4,
        out_shape=[_sds((rows, cols), F32)] * 4,
        compiler_params=_params(("arbitrary",), None),
    )(cact_t, dmod_q, w, m, v)


def _me():
    x, y, c = lax.axis_index("x"), lax.axis_index("y"), lax.axis_index("c")
    return x, y, c


_OFFSETS7 = [(dx, dy, dc) for dx in (0, 1) for dy in (0, 1) for dc in (0, 1) if (dx, dy, dc) != (0, 0, 0)]
_CHIP_OFFSETS = [(1, 0), (0, 1), (1, 1)]


def _ada_fwd(c, w_ada_q, b_ada_q):
    ncol = w_ada_q.shape[1]

    def body(c_ref, w_ref, b_ref, cact_ref, modsel_ref, blk, gath, res, parts, send_sems, recv_sems):
        x, y, cc = _me()
        me = 4 * x + 2 * y + cc
        q = 2 * x + y
        cv = c_ref[...]
        ca = cv * jax.nn.sigmoid(cv)
        row = lax.broadcasted_iota(jnp.int32, (8, D), 0)
        blk[...] = jnp.where(row == me, jnp.broadcast_to(ca, (8, D)), 0.0)
        gath[me] = blk[...]
        sends = []
        for k, (dx, dy, dc) in enumerate(_OFFSETS7):
            cp = pltpu.make_async_remote_copy(blk, gath.at[me], send_sems.at[k], recv_sems.at[k],
                                              device_id=(x ^ dx, y ^ dy, cc ^ dc), device_id_type=MESH)
            cp.start()
            sends.append(cp)
        for cp in sends:
            cp.wait_recv()
        cact = gath[0]
        for d in range(1, N_DEV):
            cact = cact + gath[d]
        cact_ref[...] = cact
        res[...] = jnp.dot(cact, w_ref[...], precision=HIGHEST, preferred_element_type=F32) + b_ref[...]
        parts[q] = res[...]
        sends2 = []
        for k, (dx, dy) in enumerate(_CHIP_OFFSETS):
            cp = pltpu.make_async_remote_copy(res, parts.at[q], send_sems.at[7 + k], recv_sems.at[7 + k],
                                              device_id=(x ^ dx, y ^ dy, cc), device_id_type=MESH)
            cp.start()
            sends2.append(cp)
        for cp in sends2:
            cp.wait_recv()
        row2 = lax.broadcasted_iota(jnp.int32, (8, ncol), 0)
        out = jnp.zeros((8, ncol), F32)
        for s in range(NQ):
            mine = jnp.sum(jnp.where(row2 == me, parts[s], 0.0), axis=0, keepdims=True)
            out = out + jnp.where(row2 == s, jnp.broadcast_to(mine, (8, ncol)), 0.0)
        modsel_ref[...] = out
        for cp in sends + sends2:
            cp.wait_send()

    return _call(
        body, name="ada_fwd",
        in_specs=[VMEM, VMEM, VMEM], out_specs=[VMEM, VMEM],
        out_shape=[_sds((8, D), F32), _sds((8, ncol), F32)],
        scratch_shapes=[pltpu.VMEM((8, D), F32), pltpu.VMEM((N_DEV, 8, D), F32), pltpu.VMEM((8, ncol), F32),
                        pltpu.VMEM((NQ, 8, ncol), F32), pltpu.SemaphoreType.DMA((10,)), pltpu.SemaphoreType.DMA((10,))],
        compiler_params=_params(None, VMEM_LIMIT),
    )(c, w_ada_q, b_ada_q)


def _gather_weights(ws):
    n = len(ws)

    def body(*refs):
        outs = refs[n:2 * n]
        send_sems, recv_sems = refs[2 * n:]
        x, y, cc = _me()
        q = 2 * x + y
        pending = []
        for w in range(n):
            hr = outs[w].shape[1] // 2
            mine = pl.ds(cc * hr, hr)
            for j, (dx, dy) in enumerate(_CHIP_OFFSETS):
                cp = pltpu.make_async_remote_copy(outs[w].at[q, mine], outs[w].at[q, mine],
                                                  send_sems.at[6 * w + j], recv_sems.at[6 * w + j],
                                                  device_id=(x ^ dx, y ^ dy, cc), device_id_type=MESH)
                cp.start()
                pending.append(cp)
        forwards = []
        for w in range(n):
            hr = outs[w].shape[1] // 2
            mine = pl.ds(cc * hr, hr)
            for j, (dx, dy) in enumerate(_CHIP_OFFSETS):
                pq = 2 * (x ^ dx) + (y ^ dy)
                land = outs[w].at[pq, mine]
                pltpu.make_async_remote_copy(land, land, send_sems.at[6 * w + j], recv_sems.at[6 * w + j],
                                             device_id=(x, y, cc), device_id_type=MESH).wait_recv()
                fw = pltpu.make_async_remote_copy(land, land, send_sems.at[6 * w + 3 + j], recv_sems.at[6 * w + 3 + j],
                                                  device_id=(x, y, 1 - cc), device_id_type=MESH)
                fw.start()
                forwards.append(fw)
        for w in range(n):
            hr = outs[w].shape[1] // 2
            other = pl.ds((1 - cc) * hr, hr)
            for j, (dx, dy) in enumerate(_CHIP_OFFSETS):
                pq = 2 * (x ^ dx) + (y ^ dy)
                land = outs[w].at[pq, other]
                pltpu.make_async_remote_copy(land, land, send_sems.at[6 * w + 3 + j], recv_sems.at[6 * w + 3 + j],
                                             device_id=(x, y, 1 - cc), device_id_type=MESH).wait_recv()
        for cp in pending + forwards:
            cp.wait_send()

    return _call(
        body, name="gather_weights",
        in_specs=[ANY] * n, out_specs=[ANY] * n,
        out_shape=[_sds(tuple(w.shape), w.dtype) for w in ws],
        scratch_shapes=[pltpu.SemaphoreType.DMA((6 * n,)), pltpu.SemaphoreType.DMA((6 * n,))],
        input_output_aliases={i: i for i in range(n)},
    )(*ws)


def _sibling_halves(gs):
    n = len(gs)

    def body(*refs):
        ins, outs = refs[:n], refs[n:2 * n]
        send_sems, recv_sems = refs[2 * n:]
        x, y, cc = _me()
        cps = []
        for w in range(n):
            cp = pltpu.make_async_remote_copy(ins[w].at[:, 1 - cc], outs[w], send_sems.at[w], recv_sems.at[w],
                                              device_id=(x, y, 1 - cc), device_id_type=MESH)
            cp.start()
            cps.append(cp)
        for cp in cps:
            cp.wait()

    return _call(
        body, name="sibling_halves",
        in_specs=[ANY] * n, out_specs=[ANY] * n,
        out_shape=[_sds((NQ,) + tuple(g.shape[2:]), F32) for g in gs],
        scratch_shapes=[pltpu.SemaphoreType.DMA((n,)), pltpu.SemaphoreType.DMA((n,))],
    )(*gs)


def _chip_exchange(cs):
    n = len(cs)

    def body(*refs):
        ins, outs = refs[:n], refs[n:2 * n]
        send_sems, recv_sems = refs[2 * n:]
        x, y, cc = _me()
        q = 2 * x + y
        cps = []
        for w in range(n):
            for j, (dx, dy) in enumerate(_CHIP_OFFSETS):
                pq = 2 * (x ^ dx) + (y ^ dy)
                cp = pltpu.make_async_remote_copy(ins[w].at[pq], outs[w].at[q], send_sems.at[3 * w + j],
                                                  recv_sems.at[3 * w + j],
                                                  device_id=(x ^ dx, y ^ dy, cc), device_id_type=MESH)
                cp.start()
                cps.append(cp)
        for cp in cps:
            cp.wait()

    return _call(
        body, name="chip_exchange",
        in_specs=[ANY] * n, out_specs=[ANY] * n,
        out_shape=[_sds(tuple(c.shape), c.dtype) for c in cs],
        scratch_shapes=[pltpu.SemaphoreType.DMA((3 * n,)), pltpu.SemaphoreType.DMA((3 * n,))],
    )(*cs)


def _sibling_share(fs):
    n = len(fs)

    def body(*refs):
        ins, outs = refs[:n], refs[n:2 * n]
        send_sems, recv_sems = refs[2 * n:]
        x, y, cc = _me()
        cps = []
        for w in range(n):
            cp = pltpu.make_async_remote_copy(ins[w], outs[w], send_sems.at[w], recv_sems.at[w],
                                              device_id=(x, y, 1 - cc), device_id_type=MESH)
            cp.start()
            cps.append(cp)
        for cp in cps:
            cp.wait()

    return _call(
        body, name="sibling_share",
        in_specs=[ANY] * n, out_specs=[ANY] * n,
        out_shape=[_sds(tuple(f.shape), F32) for f in fs],
        scratch_shapes=[pltpu.SemaphoreType.DMA((n,)), pltpu.SemaphoreType.DMA((n,))],
    )(*fs)


def _small_allreduce_adamw(g, w, m, v):
    rows = g.shape[0]

    def body(g_ref, w_ref, m_ref, v_ref, gs_ref, d_ref, mo_ref, vo_ref, gath, send_sems, recv_sems, loc_sem):
        x, y, cc = _me()
        me = 4 * x + 2 * y + cc
        loc = pltpu.make_async_copy(g_ref, gath.at[me], loc_sem)
        loc.start()
        cps = []
        for k, (dx, dy, dc) in enumerate(_OFFSETS7):
            cp = pltpu.make_async_remote_copy(g_ref, gath.at[me], send_sems.at[k], recv_sems.at[k],
                                              device_id=(x ^ dx, y ^ dy, cc ^ dc), device_id_type=MESH)
            cp.start()
            cps.append(cp)
        loc.wait()
        for cp in cps:
            cp.wait()
        tot = gath[0]
        for dev in range(1, N_DEV):
            tot = tot + gath[dev]
        gs_ref[...] = tot
        d, mn, vn = _adamw(w_ref[...], tot, m_ref[...], v_ref[...])
        d_ref[...] = d
        mo_ref[...] = mn
        vo_ref[...] = vn

    return _call(
        body, name="small_allreduce_adamw",
        in_specs=[VMEM] * 4, out_specs=[VMEM] * 5,
        out_shape=[_sds((rows, LANE), F32)] * 4 + [_sds((N_DEV, rows, LANE), F32)],
        scratch_shapes=[pltpu.SemaphoreType.DMA((7,)), pltpu.SemaphoreType.DMA((7,)), pltpu.SemaphoreType.DMA],
        compiler_params=_params(None, VMEM_LIMIT),
    )(g, w, m, v)


_SMALL = ["b_ada", "norm_ffn1_g", "norm_mix_g", "pool_scale", "gmlp_ln_g", "gmlp_ln_b", "b_spatial",
          "norm_ffn2_g", "norm_final_g", "w_pool", "w_spatial"]


def _pack(parts):
    blocks, layout, r0 = [], {}, 0
    for name in _SMALL:
        a = parts[name]
        n = a.size
        rows = -(-n // LANE)
        rows8 = -(-rows // 8) * 8
        flat = a.reshape(-1).astype(F32)
        if rows8 * LANE != n:
            flat = jnp.concatenate([flat, jnp.zeros((rows8 * LANE - n,), F32)])
        blocks.append(flat.reshape(rows8, LANE))
        layout[name] = (r0, n, a.shape)
        r0 += rows8
    return jnp.concatenate(blocks, axis=0), layout


def _unpack(packed, layout):
    out = {}
    for name, (r0, n, shape) in layout.items():
        rows = -(-n // LANE)
        out[name] = packed[r0:r0 + rows].reshape(-1)[:n].reshape(shape)
    return out


def _modv(mod9, sub, gain):
    rows = jnp.concatenate([mod9[3 * sub:3 * sub + 3], gain.reshape(1, D), jnp.zeros((4, D), F32)], axis=0)
    return rows


def _local_step(x, target, mod9, gains, wfull, small):
    mv1 = _modv(mod9, 0, gains["norm_ffn1_g"])
    mv2 = _modv(mod9, 1, gains["norm_mix_g"])
    mv3 = _modv(mod9, 2, gains["norm_ffn2_g"])
    wcat, wtcat, bias = _prep_spatial(small["w_spatial"], small["b_spatial"].T)
    wpool = small["w_pool"].astype(BF16)
    vecs = jnp.concatenate([small["pool_scale"].reshape(1, DP), small["gmlp_ln_g"].reshape(1, DG),
                            small["gmlp_ln_b"].reshape(1, DG), jnp.zeros((5, DP), F32)], axis=0)
    gf = jnp.concatenate([small["norm_final_g"].reshape(1, D), jnp.zeros((7, D), F32)], axis=0)

    x1, g1s, u1s = _ffn_fwd(x, mv1, wfull["ffn1_w_in"], wfull["ffn1_w_out"])
    x2, pooled, zb = _mix_fwd(x1, mv2, wfull["w_mix_in"], wpool, vecs, wcat, bias, wfull["w_mix_out"])
    x3, g3s, u3s = _ffn_fwd(x2, mv3, wfull["ffn2_w_in"], wfull["ffn2_w_out"])
    dx3, loss_blk, dgf = _head(x3, target, gf)
    dx2, dwin2, dwout2, vec3 = _ffn_bwd(x2, dx3, g3s, u3s, mv3, wfull["ffn2_w_in"], wfull["ffn2_w_out"])
    (dx1, dwmi, dwmo, dwpool, dwsp, dbsp, v512, vec2) = _mix_bwd(
        x1, dx2, pooled, zb, mv2, wfull["w_mix_in"], wpool, vecs, wcat, wtcat, bias, wfull["w_mix_out"])
    dx0, dwin1, dwout1, vec1 = _ffn_bwd(x, dx1, g1s, u1s, mv1, wfull["ffn1_w_in"], wfull["ffn1_w_out"])

    dmod = jnp.concatenate([vec1[0:3], vec2[0:3], vec3[0:3]], axis=0)
    grads = dict(
        ffn1_w_in=dwin1, ffn1_w_out=dwout1, w_mix_in=dwmi, w_mix_out=dwmo, ffn2_w_in=dwin2, ffn2_w_out=dwout2,
        b_ada=dmod.reshape(1, 9 * D), norm_ffn1_g=vec1[3:4], norm_mix_g=vec2[3:4], norm_ffn2_g=vec3[3:4],
        pool_scale=v512[0:1], gmlp_ln_g=v512[1:2], gmlp_ln_b=v512[2:3], b_spatial=dbsp[None],
        norm_final_g=dgf[0], w_pool=dwpool[None], w_spatial=dwsp[None])
    return loss_blk, dx0, grads


_BIG = ["ffn1_w_in", "ffn1_w_out", "w_mix_in", "w_mix_out", "ffn2_w_in", "ffn2_w_out"]


def kernel(x, c, w_ada, b_ada, norm_ffn1_g, ffn1_w_in, ffn1_w_out, norm_mix_g, w_mix_in, w_pool, pool_scale, gmlp_ln_g, gmlp_ln_b, w_spatial, b_spatial, w_mix_out, norm_ffn2_g, ffn2_w_in, ffn2_w_out, norm_final_g, loss_target, m_w_ada, m_b_ada, m_norm_ffn1_g, m_ffn1_w_in, m_ffn1_w_out, m_norm_mix_g, m_w_mix_in, m_w_pool, m_pool_scale, m_gmlp_ln_g, m_gmlp_ln_b, m_w_spatial, m_b_spatial, m_w_mix_out, m_norm_ffn2_g, m_ffn2_w_in, m_ffn2_w_out, m_norm_final_g, v_w_ada, v_b_ada, v_norm_ffn1_g, v_ffn1_w_in, v_ffn1_w_out, v_norm_mix_g, v_w_mix_in, v_w_pool, v_pool_scale, v_gmlp_ln_g, v_gmlp_ln_b, v_w_spatial, v_b_spatial, v_w_mix_out, v_norm_ffn2_g, v_ffn2_w_in, v_ffn2_w_out, v_norm_final_g):
    names = ["w_ada", "b_ada", "norm_ffn1_g", "ffn1_w_in", "ffn1_w_out", "norm_mix_g", "w_mix_in", "w_pool",
             "pool_scale", "gmlp_ln_g", "gmlp_ln_b", "w_spatial", "b_spatial", "w_mix_out", "norm_ffn2_g",
             "ffn2_w_in", "ffn2_w_out", "norm_final_g"]
    W = dict(zip(names, [w_ada, b_ada, norm_ffn1_g, ffn1_w_in, ffn1_w_out, norm_mix_g, w_mix_in, w_pool, pool_scale,
                         gmlp_ln_g, gmlp_ln_b, w_spatial, b_spatial, w_mix_out, norm_ffn2_g, ffn2_w_in, ffn2_w_out,
                         norm_final_g]))
    M = dict(zip(names, [m_w_ada, m_b_ada, m_norm_ffn1_g, m_ffn1_w_in, m_ffn1_w_out, m_norm_mix_g, m_w_mix_in, m_w_pool,
                         m_pool_scale, m_gmlp_ln_g, m_gmlp_ln_b, m_w_spatial, m_b_spatial, m_w_mix_out, m_norm_ffn2_g,
                         m_ffn2_w_in, m_ffn2_w_out, m_norm_final_g]))
    V = dict(zip(names, [v_w_ada, v_b_ada, v_norm_ffn1_g, v_ffn1_w_in, v_ffn1_w_out, v_norm_mix_g, v_w_mix_in, v_w_pool,
                         v_pool_scale, v_gmlp_ln_g, v_gmlp_ln_b, v_w_spatial, v_b_spatial, v_w_mix_out, v_norm_ffn2_g,
                         v_ffn2_w_in, v_ffn2_w_out, v_norm_final_g]))

    xi, yi, ci = _me()
    q = 2 * xi + yi
    core = ci.astype(jnp.int32).reshape(1)

    ncol = w_ada.shape[2]
    b_q = lax.dynamic_slice_in_dim(b_ada, q * ncol, ncol, axis=1)
    cact_all, modsel = _ada_fwd(c, w_ada[0], b_q)
    mod9 = modsel[:NQ].reshape(9, D)

    chip = q.astype(jnp.int32).reshape(1)
    gathered = dict(zip(_BIG, _gather_weights([_cast_place(W[n][0], chip) for n in _BIG])))
    wfull = dict(
        ffn1_w_in=gathered["ffn1_w_in"], ffn2_w_in=gathered["ffn2_w_in"],
        ffn1_w_out=gathered["ffn1_w_out"].reshape(2, CH, D), ffn2_w_out=gathered["ffn2_w_out"].reshape(2, CH, D),
        w_mix_in=jnp.transpose(gathered["w_mix_in"], (1, 0, 2)).reshape(D, DPROJ),
        w_mix_out=gathered["w_mix_out"].reshape(DP + DG, D))
    small = dict(w_spatial=w_spatial[0], b_spatial=b_spatial[0], w_pool=w_pool[0], pool_scale=pool_scale[0],
                 gmlp_ln_g=gmlp_ln_g[0], gmlp_ln_b=gmlp_ln_b[0], norm_final_g=norm_final_g)
    gains = dict(norm_ffn1_g=norm_ffn1_g[0], norm_mix_g=norm_mix_g[0], norm_ffn2_g=norm_ffn2_g[0])

    loss_blk, grad_x, grads = _local_step(x[0], loss_target[0], mod9, gains, wfull, small)
    loss = lax.psum(loss_blk[0, 0], ("x", "y", "c"))

    qcols = w_mix_in.shape[2]
    views = dict(
        ffn1_w_in=grads["ffn1_w_in"].reshape(NQ, 2, D // 2, CH),
        ffn2_w_in=grads["ffn2_w_in"].reshape(NQ, 2, D // 2, CH),
        ffn1_w_out=grads["ffn1_w_out"].reshape(NQ, 2, DFF // 8, D),
        ffn2_w_out=grads["ffn2_w_out"].reshape(NQ, 2, DFF // 8, D),
        w_mix_in=jnp.transpose(grads["w_mix_in"].reshape(D, NQ, qcols), (1, 0, 2)).reshape(NQ, 2, D // 2, qcols),
        w_mix_out=grads["w_mix_out"].reshape(NQ, 2, (DP + DG) // 8, D))
    gviews = [views[n] for n in _BIG]
    sib = _sibling_halves(gviews)
    csums = [_chip_sum(g, r, core) for g, r in zip(gviews, sib)]
    exch = _chip_exchange(csums)
    halves = [_sum4(cs, e, chip) for cs, e in zip(csums, exch)]
    others = _sibling_share(halves)
    out_g, out_d, out_m, out_v = {}, {}, {}, {}
    for n, own, recv in zip(_BIG, halves, others):
        g2, d, mn, vn = _adamw_halves(W[n][0], own, recv, M[n][0], V[n][0], core)
        out_g[n], out_d[n], out_m[n], out_v[n] = g2[None], d[None], mn[None], vn[None]

    gp, layout = _pack({n: grads[n] for n in _SMALL})
    wp, _ = _pack({n: W[n] for n in _SMALL})
    mp, _ = _pack({n: M[n] for n in _SMALL})
    vp, _ = _pack({n: V[n] for n in _SMALL})
    gs, dl, mo, vo, gath = _small_allreduce_adamw(gp, wp, mp, vp)
    for packed, dst in ((gs, out_g), (dl, out_d), (mo, out_m), (vo, out_v)):
        for n, a in _unpack(packed, layout).items():
            dst[n] = a.reshape(W[n].shape)

    r0, nb, _ = layout["b_ada"]
    dmod_all = gath[:, r0:r0 + nb // LANE, :].reshape(N_DEV, nb)
    dmod_q = lax.dynamic_slice_in_dim(dmod_all, q * ncol, ncol, axis=1)
    ga, da, ma, va = _ada_grad_adamw(cact_all.T, dmod_q, w_ada[0], m_w_ada[0], v_w_ada[0])
    out_g["w_ada"], out_d["w_ada"], out_m["w_ada"], out_v["w_ada"] = ga[None], da[None], ma[None], va[None]

    return (loss, grad_x[None], *[out_g[n] for n in names], *[out_d[n] for n in names],
            *[out_m[n] for n in names], *[out_v[n] for n in names])
```

```python
import functools
import math

import jax
import jax.numpy as jnp
from jax import lax
from jax.experimental import pallas as pl
from jax.experimental.pallas import tpu as pltpu

F32 = jnp.float32
BF16 = jnp.bfloat16
MESH = pl.DeviceIdType.MESH
HIGHEST = lax.Precision.HIGHEST

EPS = 1e-6
D = 1024
DFF = 2816
CH = DFF // 2
NQ = 4
DP = 512
DG = 512
DPROJ = DP + 2 * DG
POOL_WINDOWS = (2, 4, 8, 16)
HALO = 16
CHUNK = 128
LANE = 128
N_DEV = 8

ADAM_LR = 0.001
ADAM_B1 = 0.9
ADAM_B2 = 0.999
ADAM_EPS = 1e-08
ADAM_WD = 0.01
ADAM_STEP = 10

VMEM_LIMIT = 60 * 1024 * 1024

TM_FFN_FWD = 512
TM_FFN_BWD = 256
TM_MIX = 256
TM_HEAD = 512


def _call(body, **kw):
    return pl.pallas_call(body, interpret=False, **kw)


def _params(sem=None, vmem=None):
    return pltpu.CompilerParams(dimension_semantics=sem, vmem_limit_bytes=vmem)


def _sds(shape, dtype):
    return jax.ShapeDtypeStruct(shape, dtype)


ANY = pl.BlockSpec(memory_space=pl.ANY)
VMEM = pl.BlockSpec(memory_space=pltpu.VMEM)
SMEM = pl.BlockSpec(memory_space=pltpu.SMEM)


def _norm_mod(x, gn, sc, sh):
    r = lax.rsqrt(jnp.mean(x * x, axis=-1, keepdims=True) + EPS)
    xn = x * r
    hp = xn * gn
    return r, xn, hp, hp * (1.0 + sc) + sh


def _norm_mod_bwd(dh, r, xn, hp, gn, sc):
    one_sc = 1.0 + sc
    dsh = jnp.sum(dh, axis=0, keepdims=True)
    dsc = jnp.sum(dh * hp, axis=0, keepdims=True)
    dgn = jnp.sum(dh * one_sc * xn, axis=0, keepdims=True)
    dxn = dh * (gn * one_sc)
    dx = r * (dxn - xn * jnp.mean(dxn * xn, axis=-1, keepdims=True))
    return dsh, dsc, dgn, dx


def _dot(a, b):
    return jnp.dot(a, b, preferred_element_type=F32)


def _dot_nt(a, b):
    return lax.dot_general(a, b, (((1,), (1,)), ((), ())), preferred_element_type=F32)


def _dot_tn(a, b):
    return lax.dot_general(a, b, (((0,), (0,)), ((), ())), preferred_element_type=F32)


_GELU_C = math.sqrt(2.0 / math.pi)
_GELU_A = 0.044715


def _gelu_fwd_bwd(x):
    x2 = x * x
    t = jnp.tanh(_GELU_C * (x + _GELU_A * x * x2))
    g = 0.5 * x * (1.0 + t)
    dg = 0.5 * (1.0 + t) + 0.5 * x * (1.0 - t * t) * (_GELU_C * (1.0 + 3.0 * _GELU_A * x2))
    return g, dg


def _adamw(w, g, m, v):
    m = ADAM_B1 * m + (1.0 - ADAM_B1) * g
    v = ADAM_B2 * v + (1.0 - ADAM_B2) * (g * g)
    m_hat = m / (1.0 - ADAM_B1 ** ADAM_STEP)
    v_hat = v / (1.0 - ADAM_B2 ** ADAM_STEP)
    delta = -ADAM_LR * (m_hat / (jnp.sqrt(v_hat) + ADAM_EPS) + ADAM_WD * w)
    return delta, m, v


def _row_block(rows, cap=256, mult=16):
    best = None
    for t in range(mult, min(rows, cap) + 1, mult):
        if rows % t == 0:
            best = t
    assert best is not None, rows
    return best


def _ffn_fwd(x, modv, win, wout, side=None):
    S = x.shape[0]
    tm = TM_FFN_FWD
    nt = S // tm

    def body(x_ref, mod_ref, wg_ref, wu_ref, wo_ref, xo_ref, gs_ref, us_ref, h_scr, acc_scr):
        j = pl.program_id(1)

        @pl.when(j == 0)
        def _():
            _, _, _, h = _norm_mod(x_ref[...], mod_ref[3:4, :], mod_ref[1:2, :], mod_ref[0:1, :])
            h_scr[...] = h.astype(BF16)
            acc_scr[...] = jnp.zeros_like(acc_scr)

        h = h_scr[...]
        g = _dot(h, wg_ref[...]).astype(BF16)
        u = _dot(h, wu_ref[...]).astype(BF16)
        gs_ref[...] = g
        us_ref[...] = u
        gf = g.astype(F32)
        a = (gf * jax.nn.sigmoid(gf) * u.astype(F32)).astype(BF16)
        acc_scr[...] += _dot(a, wo_ref[...])

        @pl.when(j == 1)
        def _():
            xo_ref[...] = x_ref[...] + (0.5 * mod_ref[2:3, :]) * acc_scr[...]

    step = lambda i, j: lambda: (pl.program_id(0) == i) & (pl.program_id(1) == j)
    return _side_call(
        body, side, (step(0, 0), step((7 * nt) // 10, 0), step(nt - 1, 1)), name="ffn_fwd",
        grid=(nt, 2),
        in_specs=[
            pl.BlockSpec((tm, D), lambda i, j: (i, 0)),
            pl.BlockSpec((8, D), lambda i, j: (0, 0)),
            pl.BlockSpec((None, D, CH), lambda i, j: (j, 0, 0)),
            pl.BlockSpec((None, D, CH), lambda i, j: (2 + j, 0, 0)),
            pl.BlockSpec((None, CH, D), lambda i, j: (j, 0, 0)),
        ],
        out_specs=[
            pl.BlockSpec((tm, D), lambda i, j: (i, 0)),
            pl.BlockSpec((tm, CH), lambda i, j: (i, j)),
            pl.BlockSpec((tm, CH), lambda i, j: (i, j)),
        ],
        out_shape=[_sds((S, D), F32), _sds((S, DFF), BF16), _sds((S, DFF), BF16)],
        scratch_shapes=[pltpu.VMEM((tm, D), BF16), pltpu.VMEM((tm, D), F32)],
        compiler_params=_params(("arbitrary", "arbitrary"), VMEM_LIMIT),
        args=(x, modv, win, win, wout))


def _ffn_bwd(x, dxo, gs, us, modv, win, wout, side=None):
    S = x.shape[0]
    tm = TM_FFN_BWD
    nt = S // tm
    assert nt >= 2
    hi, ho = D // 2, CH // 4

    def body(x_ref, dxo_ref, gs_ref, us_ref, mod_ref, wg_ref, wu_ref, wo_ref,
             dx_ref, dwin_ref, dwout_ref, rwin_ref, rwout_ref, vec_ref, dhbuf_ref,
             accg, accu, accw, dh_st, dh_ld, sems, fsend, frecv):
        j = pl.program_id(0)
        i = pl.program_id(1)
        store = lambda t: pltpu.make_async_copy(dh_st, dhbuf_ref.at[t], sems.at[4])
        load = lambda t: pltpu.make_async_copy(dhbuf_ref.at[t], dh_ld, sems.at[5])

        @pl.when(i == 0)
        def _():
            accg[...] = jnp.zeros_like(accg)
            accu[...] = jnp.zeros_like(accu)
            accw[...] = jnp.zeros_like(accw)

        @pl.when((i == 0) & (j == 0))
        def _():
            vec_ref[...] = jnp.zeros_like(vec_ref)

        @pl.when((j == 0) & (i > 0))
        def _():
            store(i - 1).wait()

        @pl.when((j == 1) & (i == 0))
        def _():
            store(nt - 1).wait()

        @pl.when(j == 1)
        def _():
            load(i).start()

        gn, sc, sh, gate = mod_ref[3:4, :], mod_ref[1:2, :], mod_ref[0:1, :], mod_ref[2:3, :]
        x = x_ref[...]
        r, xn, hp, h = _norm_mod(x, gn, sc, sh)
        hb = h.astype(BF16)
        dxo = dxo_ref[...]
        dy = (dxo * (0.5 * gate)).astype(BF16)
        g = gs_ref[...].astype(F32)
        u = us_ref[...].astype(F32)
        sig = jax.nn.sigmoid(g)
        sl = g * sig
        a = (sl * u).astype(BF16)
        accw[...] += _dot_tn(a, dxo.astype(BF16))
        da = _dot_nt(dy, wo_ref[...])
        dg = (da * u * (sig * (1.0 + g * (1.0 - sig)))).astype(BF16)
        du = (da * sl).astype(BF16)
        accg[...] += _dot_tn(hb, dg)
        accu[...] += _dot_tn(hb, du)
        dhp = _dot_nt(dg, wg_ref[...]) + _dot_nt(du, wu_ref[...])

        @pl.when(j == 0)
        def _():
            dh_st[...] = dhp
            store(i).start()

        @pl.when(j == 1)
        def _():
            load(i).wait()
            dh = dh_ld[...] + dhp
            dsh, dsc, dgn, dxin = _norm_mod_bwd(dh, r, xn, hp, gn, sc)
            vec_ref[0:1, :] += dsh
            vec_ref[1:2, :] += dsc
            vec_ref[3:4, :] += dgn
            dx_ref[...] = dxo + dxin

        def flush(jj):
            mx, my, cc = _me()
            rows = lambda base, n, c: pl.ds(base + c * n, n)
            pieces = [(accg, 0, hi, dwin_ref, rwin_ref, jj), (accu, 0, hi, dwin_ref, rwin_ref, 2 + jj),
                      (accw, 0, ho, dwout_ref, rwout_ref, 2 * jj), (accw, 2 * ho, ho, dwout_ref, rwout_ref, 2 * jj + 1)]
            loc = [pltpu.make_async_copy(acc.at[rows(base, n, cc)], own.at[slot], sems.at[k])
                   for k, (acc, base, n, own, _, slot) in enumerate(pieces)]
            rem = [pltpu.make_async_remote_copy(acc.at[rows(base, n, 1 - cc)], sib.at[slot], fsend.at[4 * jj + k],
                                                frecv.at[4 * jj + k], device_id=(mx, my, 1 - cc), device_id_type=MESH)
                   for k, (acc, base, n, _, sib, slot) in enumerate(pieces)]
            return loc, rem

        for jj in range(2):
            @pl.when((i == nt - 1) & (j == jj))
            def _(jj=jj):
                gw = accw[...]
                vec_ref[2:3, :] += 0.5 * jnp.sum(wo_ref[...].astype(F32) * gw, axis=0, keepdims=True)
                accw[...] = gw * (0.5 * gate)
                loc, rem = flush(jj)
                for cp in loc + rem:
                    cp.start()
                for cp in loc:
                    cp.wait()
                for cp in rem:
                    cp.wait_send()

        @pl.when((i == nt - 1) & (j == 1))
        def _():
            for jj in range(2):
                for cp in flush(jj)[1]:
                    cp.wait_recv()

    step = lambda jj, ii: lambda: (pl.program_id(0) == jj) & (pl.program_id(1) == ii)
    (dx, dwin, dwout, rwin, rwout, vec, _), extra = _side_call(
        body, side, (step(0, 0), None, step(1, nt - 1)), name="ffn_bwd",
        grid=(2, nt),
        in_specs=[
            pl.BlockSpec((tm, D), lambda j, i: (i, 0)),
            pl.BlockSpec((tm, D), lambda j, i: (i, 0)),
            pl.BlockSpec((tm, CH), lambda j, i: (i, j)),
            pl.BlockSpec((tm, CH), lambda j, i: (i, j)),
            pl.BlockSpec((8, D), lambda j, i: (0, 0)),
            pl.BlockSpec((None, D, CH), lambda j, i: (j, 0, 0)),
            pl.BlockSpec((None, D, CH), lambda j, i: (2 + j, 0, 0)),
            pl.BlockSpec((None, CH, D), lambda j, i: (j, 0, 0)),
        ],
        out_specs=[
            pl.BlockSpec((tm, D), lambda j, i: (i * j, 0)),
            ANY, ANY, ANY, ANY,
            pl.BlockSpec((8, D), lambda j, i: (0, 0)),
            ANY,
        ],
        out_shape=[_sds((S, D), F32), _sds((NQ, hi, CH), F32), _sds((NQ, ho, D), F32), _sds((NQ, hi, CH), F32),
                   _sds((NQ, ho, D), F32), _sds((8, D), F32), _sds((nt, tm, D), F32)],
        scratch_shapes=[pltpu.VMEM((D, CH), F32), pltpu.VMEM((D, CH), F32), pltpu.VMEM((CH, D), F32),
                        pltpu.VMEM((tm, D), F32), pltpu.VMEM((tm, D), F32), pltpu.SemaphoreType.DMA((6,)),
                        pltpu.SemaphoreType.DMA((8,)), pltpu.SemaphoreType.DMA((8,))],
        compiler_params=_params(("arbitrary", "arbitrary"), VMEM_LIMIT),
        args=(x, dxo, gs, us, modv, win, win, wout))
    return (dx, dwin, dwout, rwin, rwout, vec), extra


def _head(x, target, gf):
    S = x.shape[0]
    tm = TM_HEAD
    nt = S // tm

    def body(x_ref, t_ref, g_ref, dx_ref, loss_ref, dg_ref):
        i = pl.program_id(0)

        @pl.when(i == 0)
        def _():
            loss_ref[...] = jnp.zeros_like(loss_ref)
            dg_ref[...] = jnp.zeros_like(dg_ref)

        x = x_ref[...]
        gf_ = g_ref[0:1, :]
        r = lax.rsqrt(jnp.mean(x * x, axis=-1, keepdims=True) + EPS)
        xn = x * r
        err = xn * gf_ - t_ref[...]
        loss_ref[...] += (0.5 / D) * jnp.sum(err * err)
        dy = err * (1.0 / D)
        dg_ref[0:1, :] += jnp.sum(dy * xn, axis=0, keepdims=True)
        dxn = dy * gf_
        dx_ref[...] = r * (dxn - xn * jnp.mean(dxn * xn, axis=-1, keepdims=True))

    return _call(
        body, name="head",
        grid=(nt,),
        in_specs=[pl.BlockSpec((tm, D), lambda i: (i, 0)), pl.BlockSpec((tm, D), lambda i: (i, 0)),
                  pl.BlockSpec((8, D), lambda i: (0, 0))],
        out_specs=[pl.BlockSpec((tm, D), lambda i: (i, 0)), pl.BlockSpec((8, LANE), lambda i: (0, 0)),
                   pl.BlockSpec((8, D), lambda i: (0, 0))],
        out_shape=[_sds((S, D), F32), _sds((8, LANE), F32), _sds((8, D), F32)],
        compiler_params=_params(("arbitrary",), VMEM_LIMIT),
    )(x, target, gf)


def _prep_spatial(w_spatial, b_spatial_t):
    def body(w_ref, b_ref, wcat_ref, wtcat_ref, bias_ref):
        row = lax.broadcasted_iota(jnp.int32, (CHUNK, CHUNK), 0)
        col = lax.broadcasted_iota(jnp.int32, (CHUNK, CHUNK), 1)
        tril = col <= row
        for p in range(4):
            wa = jnp.where(tril, w_ref[2 * p], 0.0)
            wb = jnp.where(tril, w_ref[2 * p + 1], 0.0)
            wcat_ref[p] = jnp.concatenate([wa, wb], axis=1).astype(BF16)
            wtcat_ref[p] = jnp.concatenate([wa.T, wb.T], axis=1).astype(BF16)
        head = lax.broadcasted_iota(jnp.int32, (8, DG), 0)
        ch = lax.broadcasted_iota(jnp.int32, (8, DG), 1)
        spread = jnp.where(ch // 64 == head, 1.0, 0.0).astype(F32)
        bias_ref[...] = jnp.dot(b_ref[...], spread, precision=HIGHEST, preferred_element_type=F32)

    return _call(
        body, name="prep_spatial",
        in_specs=[VMEM, VMEM], out_specs=[VMEM, VMEM, VMEM],
        out_shape=[_sds((4, CHUNK, 2 * CHUNK), BF16), _sds((4, CHUNK, 2 * CHUNK), BF16), _sds((CHUNK, DG), F32)],
    )(w_spatial, b_spatial_t)


def _pair_rhs(blocks):
    lane = lax.broadcasted_iota(jnp.int32, (CHUNK, LANE), 1)
    lo = lane < 64
    top = jnp.concatenate([jnp.where(lo, b, 0.0) for b in blocks], axis=1)
    bot = jnp.concatenate([jnp.where(lo, 0.0, b) for b in blocks], axis=1)
    return top, bot


def _gmlp_branch(zb, vecs, wcat_ref, bias_ref, nchunks):
    z, dz = _gelu_fwd_bwd(zb)
    u = z[:, :DG]
    v = z[:, DG:]
    ln_g, ln_b = vecs[1:2, :], vecs[2:3, :]
    mu = jnp.mean(v, axis=-1, keepdims=True)
    vc = v - mu
    rstd = lax.rsqrt(jnp.mean(vc * vc, axis=-1, keepdims=True) + EPS)
    vhat = vc * rstd
    vl = vhat * ln_g + ln_b
    sv_cols = []
    for p in range(4):
        blocks = [vl[k * CHUNK:(k + 1) * CHUNK, p * LANE:(p + 1) * LANE] for k in range(nchunks)]
        top, bot = _pair_rhs(blocks)
        rhs = jnp.concatenate([top, bot], axis=0).astype(BF16)
        out = _dot(wcat_ref[p], rhs)
        bias = bias_ref[:, p * LANE:(p + 1) * LANE]
        sv_cols.append(jnp.concatenate([out[:, k * LANE:(k + 1) * LANE] + bias for k in range(nchunks)], axis=0))
    sv = jnp.concatenate(sv_cols, axis=1)
    return dict(u=u, dz=dz, rstd=rstd, vhat=vhat, vl=vl, sv=sv, yb=u * sv)


def _mix_fwd(x, modv, win, wpool, vecs, wcat, bias, wout):
    S = x.shape[0]
    tm = TM_MIX
    nt = S // tm
    nchunks = tm // CHUNK

    def body(x_ref, mod_ref, win_ref, wpool_ref, vec_ref, wcat_ref, bias_ref, wout_ref,
             xo_ref, pooled_ref, zb_ref, ext):
        i = pl.program_id(0)
        x = x_ref[...]
        _, _, _, h = _norm_mod(x, mod_ref[3:4, :], mod_ref[1:2, :], mod_ref[0:1, :])
        proj = _dot(h.astype(BF16), win_ref[...])
        xa = proj[:, :DP]
        zb = proj[:, DP:]
        zb_ref[...] = zb

        @pl.when(i == 0)
        def _():
            ext[0:HALO, :] = jnp.zeros((HALO, DP), F32)

        ext[HALO:HALO + tm, :] = xa
        pos = i * tm + lax.broadcasted_iota(jnp.int32, (tm, 1), 0)
        vecs = vec_ref[...]
        ya_cols = []
        pooled_cols = []
        for gi, w in enumerate(POOL_WINDOWS):
            cols = slice(gi * LANE, (gi + 1) * LANE)
            s = xa[:, cols]
            for k in range(1, w):
                s = s + ext[HALO - k:HALO - k + tm, cols]
            cnt = jnp.minimum(pos + 1, w).astype(F32)
            pooled = (s / cnt - xa[:, cols]).astype(BF16)
            pooled_cols.append(pooled)
            ya_cols.append(_dot(pooled, wpool_ref[gi]) * vecs[0:1, cols])
        pooled_ref[...] = jnp.concatenate(pooled_cols, axis=1)
        ext[0:HALO, :] = ext[tm:tm + HALO, :]

        gm = _gmlp_branch(zb, vecs, wcat_ref, bias_ref, nchunks)
        cat = jnp.concatenate(ya_cols + [gm["yb"]], axis=1).astype(BF16)
        xo_ref[...] = x + mod_ref[2:3, :] * _dot(cat, wout_ref[...])

    full = lambda shape: pl.BlockSpec(shape, lambda i: (0,) * len(shape))
    return _call(
        body, name="mix_fwd",
        grid=(nt,),
        in_specs=[pl.BlockSpec((tm, D), lambda i: (i, 0)), full((8, D)), full((D, DPROJ)),
                  full((4, LANE, LANE)), full((8, DP)), full((4, CHUNK, 2 * CHUNK)), full((CHUNK, DG)),
                  full((DP + DG, D))],
        out_specs=[pl.BlockSpec((tm, D), lambda i: (i, 0)), pl.BlockSpec((tm, DP), lambda i: (i, 0)),
                   pl.BlockSpec((tm, 2 * DG), lambda i: (i, 0))],
        out_shape=[_sds((S, D), F32), _sds((S, DP), BF16), _sds((S, 2 * DG), F32)],
        scratch_shapes=[pltpu.VMEM((tm + HALO, DP), F32)],
        compiler_params=_params(("arbitrary",), VMEM_LIMIT),
    )(x, modv, win, wpool, vecs, wcat, bias, wout)


def _mix_bwd(x, dxo, pooled, zb, modv, win, wpool, vecs, wcat, wtcat, bias, wout, side=None):
    S = x.shape[0]
    tm = TM_MIX
    nt = S // tm
    nchunks = tm // CHUNK

    def body(x_ref, dxo_ref, pooled_ref, zb_ref, mod_ref, win_ref, wpool_ref, vec_ref, wcat_ref, wtcat_ref,
             bias_ref, wout_ref,
             dx_ref, dwin_ref, dwout_ref, dwpool_ref, dwsp_ref, dbsp_ref, v512_ref, vd_ref, qext, dsv_acc):
        step = pl.program_id(0)
        tile = nt - 1 - step

        @pl.when(step == 0)
        def _():
            dwin_ref[...] = jnp.zeros_like(dwin_ref)
            dwout_ref[...] = jnp.zeros_like(dwout_ref)
            dwpool_ref[...] = jnp.zeros_like(dwpool_ref)
            dwsp_ref[...] = jnp.zeros_like(dwsp_ref)
            v512_ref[...] = jnp.zeros_like(v512_ref)
            vd_ref[...] = jnp.zeros_like(vd_ref)
            dsv_acc[...] = jnp.zeros_like(dsv_acc)
            qext[tm:tm + HALO, :] = jnp.zeros((HALO, DP), F32)

        gn, sc, sh, gate = mod_ref[3:4, :], mod_ref[1:2, :], mod_ref[0:1, :], mod_ref[2:3, :]
        vecs = vec_ref[...]
        x = x_ref[...]
        r, xn, hp, h = _norm_mod(x, gn, sc, sh)
        hb = h.astype(BF16)
        dxo = dxo_ref[...]

        pooled = pooled_ref[...]
        mixed_cols = [_dot(pooled[:, gi * LANE:(gi + 1) * LANE], wpool_ref[gi]) for gi in range(4)]
        mixed = jnp.concatenate(mixed_cols, axis=1)
        scale = vecs[0:1, :]
        gm = _gmlp_branch(zb_ref[...], vecs, wcat_ref, bias_ref, nchunks)
        cat = jnp.concatenate([mixed * scale, gm["yb"]], axis=1).astype(BF16)

        dwout_ref[...] += _dot_tn(cat, dxo.astype(BF16))
        dcat = _dot_nt((dxo * gate).astype(BF16), wout_ref[...])
        dya = dcat[:, :DP]
        dyb = dcat[:, DP:]

        v512_ref[0:1, :] += jnp.sum(dya * mixed, axis=0, keepdims=True)
        dmixed = (dya * scale).astype(BF16)
        pos = tile * tm + lax.broadcasted_iota(jnp.int32, (tm, 1), 0)
        dpooled_cols = []
        for gi, w in enumerate(POOL_WINDOWS):
            cols = slice(gi * LANE, (gi + 1) * LANE)
            dp = _dot_nt(dmixed[:, cols], wpool_ref[gi])
            dwpool_ref[gi] += _dot_tn(pooled[:, cols], dmixed[:, cols])
            cnt = jnp.minimum(pos + 1, w).astype(F32)
            qext[0:tm, cols] = dp / cnt
            dpooled_cols.append(dp)
        dxa_cols = []
        for gi, w in enumerate(POOL_WINDOWS):
            cols = slice(gi * LANE, (gi + 1) * LANE)
            s = qext[0:tm, cols]
            for k in range(1, w):
                s = s + qext[k:k + tm, cols]
            dxa_cols.append(s - dpooled_cols[gi])
        qext[tm:tm + HALO, :] = qext[0:HALO, :]

        u, sv, vl = gm["u"], gm["sv"], gm["vl"]
        du = dyb * sv
        dsv = dyb * u
        dvl_cols = []
        for p in range(4):
            cols = slice(p * LANE, (p + 1) * LANE)
            dblocks = [dsv[k * CHUNK:(k + 1) * CHUNK, cols] for k in range(nchunks)]
            vblocks = [vl[k * CHUNK:(k + 1) * CHUNK, cols] for k in range(nchunks)]
            tot = dblocks[0]
            for b in dblocks[1:]:
                tot = tot + b
            dsv_acc[:, cols] += tot
            top, bot = _pair_rhs(dblocks)
            out = _dot(wtcat_ref[p], jnp.concatenate([top, bot], axis=0).astype(BF16))
            dvl_cols.append(jnp.concatenate([out[:, k * LANE:(k + 1) * LANE] for k in range(nchunks)], axis=0))
            vcat = jnp.concatenate(vblocks, axis=1).astype(BF16)
            dwsp_ref[2 * p] += _dot_nt(top.astype(BF16), vcat)
            dwsp_ref[2 * p + 1] += _dot_nt(bot.astype(BF16), vcat)
        dvl = jnp.concatenate(dvl_cols, axis=1)
        vhat, rstd = gm["vhat"], gm["rstd"]
        v512_ref[1:2, :] += jnp.sum(dvl * vhat, axis=0, keepdims=True)
        v512_ref[2:3, :] += jnp.sum(dvl, axis=0, keepdims=True)
        dvh = dvl * vecs[1:2, :]
        dv = rstd * (dvh - jnp.mean(dvh, axis=-1, keepdims=True)
                     - vhat * jnp.mean(dvh * vhat, axis=-1, keepdims=True))
        dzb = jnp.concatenate([du, dv], axis=1) * gm["dz"]

        dproj = jnp.concatenate(dxa_cols + [dzb], axis=1).astype(BF16)
        dwin_ref[...] += _dot_tn(hb, dproj)
        dh = _dot_nt(dproj, win_ref[...])
        dsh, dsc, dgn, dxin = _norm_mod_bwd(dh, r, xn, hp, gn, sc)
        vd_ref[0:1, :] += dsh
        vd_ref[1:2, :] += dsc
        vd_ref[3:4, :] += dgn
        dx_ref[...] = dxo + dxin

        @pl.when(step == nt - 1)
        def _():
            gw = dwout_ref[...]
            vd_ref[2:3, :] += jnp.sum(wout_ref[...].astype(F32) * gw, axis=0, keepdims=True)
            dwout_ref[...] = gw * gate
            row = lax.broadcasted_iota(jnp.int32, (CHUNK, CHUNK), 0)
            col = lax.broadcasted_iota(jnp.int32, (CHUNK, CHUNK), 1)
            for hh in range(8):
                dwsp_ref[hh] = jnp.where(col <= row, dwsp_ref[hh], 0.0)
            head = lax.broadcasted_iota(jnp.int32, (8, DG), 0)
            ch = lax.broadcasted_iota(jnp.int32, (8, DG), 1)
            spread = jnp.where(ch // 64 == head, 1.0, 0.0).astype(F32)
            dbsp_ref[...] = lax.dot_general(spread, dsv_acc[...], (((1,), (1,)), ((), ())),
                                            precision=HIGHEST, preferred_element_type=F32)

    full = lambda shape: pl.BlockSpec(shape, lambda s: (0,) * len(shape))
    rev = lambda cols: pl.BlockSpec((tm, cols), lambda s: (nt - 1 - s, 0))
    step = lambda s: lambda: pl.program_id(0) == s
    return _side_call(
        body, side, (step(0), None, step(nt - 1)), name="mix_bwd",
        grid=(nt,),
        in_specs=[rev(D), rev(D), rev(DP), rev(2 * DG), full((8, D)), full((D, DPROJ)), full((4, LANE, LANE)),
                  full((8, DP)), full((4, CHUNK, 2 * CHUNK)), full((4, CHUNK, 2 * CHUNK)), full((CHUNK, DG)),
                  full((DP + DG, D))],
        out_specs=[rev(D), full((D, DPROJ)), full((DP + DG, D)), full((4, LANE, LANE)), full((8, CHUNK, CHUNK)),
                   full((8, CHUNK)), full((8, DP)), full((8, D))],
        out_shape=[_sds((S, D), F32), _sds((D, DPROJ), F32), _sds((DP + DG, D), F32), _sds((4, LANE, LANE), F32),
                   _sds((8, CHUNK, CHUNK), F32), _sds((8, CHUNK), F32), _sds((8, DP), F32), _sds((8, D), F32)],
        scratch_shapes=[pltpu.VMEM((tm + HALO, DP), F32), pltpu.VMEM((CHUNK, DG), F32)],
        compiler_params=_params(("arbitrary",), VMEM_LIMIT),
        args=(x, dxo, pooled, zb, modv, win, wpool, vecs, wcat, wtcat, bias, wout))


def _chip_sum(g, rbuf, core):
    _, _, hr, cols = g.shape
    tr = _row_block(hr)

    def body(c_ref, g_ref, r_ref, o_ref):
        o_ref[...] = (g_ref[...] + r_ref[...]).astype(BF16)

    return pl.pallas_call(
        body, name="chip_sum", interpret=False,
        grid_spec=pltpu.PrefetchScalarGridSpec(
            num_scalar_prefetch=1, grid=(NQ, hr // tr),
            in_specs=[pl.BlockSpec((None, None, tr, cols), lambda q, i, c: (q, c[0], i, 0)),
                      pl.BlockSpec((None, tr, cols), lambda q, i, c: (q, i, 0))],
            out_specs=pl.BlockSpec((None, tr, cols), lambda q, i, c: (q, i, 0))),
        out_shape=_sds((NQ, hr, cols), BF16),
        compiler_params=_params(("arbitrary", "arbitrary"), None),
    )(core, g, rbuf)


def _chip_sum_pair(own, rbuf):
    _, hr, cols = own.shape
    tr = _row_block(hr)

    def body(a_ref, b_ref, o_ref):
        o_ref[...] = (a_ref[...] + b_ref[...]).astype(BF16)

    spec = pl.BlockSpec((None, tr, cols), lambda q, i: (q, i, 0))
    return _call(
        body, name="chip_sum_pair",
        grid=(NQ, hr // tr),
        in_specs=[spec, spec], out_specs=spec,
        out_shape=_sds((NQ, hr, cols), BF16),
        compiler_params=_params(("arbitrary", "arbitrary"), None),
    )(own, rbuf)


def _sum4(cs, rbuf, chip):
    _, hr, cols = rbuf.shape
    tr = _row_block(hr)

    def body(q_ref, c_ref, r1_ref, r2_ref, r3_ref, o_ref):
        acc = c_ref[...].astype(F32)
        for r in (r1_ref, r2_ref, r3_ref):
            acc = acc + r[...].astype(F32)
        o_ref[...] = acc

    slot = lambda k: pl.BlockSpec((None, tr, cols), lambda i, q: ((q[0] + k) % NQ, i, 0))
    return pl.pallas_call(
        body, name="sum4", interpret=False,
        grid_spec=pltpu.PrefetchScalarGridSpec(
            num_scalar_prefetch=1, grid=(hr // tr,),
            in_specs=[slot(0), slot(1), slot(2), slot(3)],
            out_specs=pl.BlockSpec((tr, cols), lambda i, q: (i, 0))),
        out_shape=_sds((hr, cols), F32),
        compiler_params=_params(("arbitrary",), None),
    )(chip, cs, rbuf, rbuf, rbuf)


def _adamw_halves(w, own, recv, m, v, core):
    rows, cols = w.shape
    hr = rows // 2
    tr = _row_block(hr, mult=8)
    nb = hr // tr

    def body(c_ref, w_ref, own_ref, recv_ref, m_ref, v_ref, g_ref, d_ref, mo_ref, vo_ref):
        g = jnp.where(pl.program_id(0) == c_ref[0], own_ref[...], recv_ref[...])
        d, mn, vn = _adamw(w_ref[...], g, m_ref[...], v_ref[...])
        g_ref[...] = g
        d_ref[...] = d
        mo_ref[...] = mn
        vo_ref[...] = vn

    full = pl.BlockSpec((tr, cols), lambda h, i, c: (h * nb + i, 0))
    half = pl.BlockSpec((tr, cols), lambda h, i, c: (i, 0))
    return pl.pallas_call(
        body, name="adamw_halves", interpret=False,
        grid_spec=pltpu.PrefetchScalarGridSpec(
            num_scalar_prefetch=1, grid=(2, nb),
            in_specs=[full, half, half, full, full],
            out_specs=[full] * 4),
        out_shape=[_sds((rows, cols), F32)] * 4,
        compiler_params=_params(("arbitrary", "arbitrary"), None),
    )(core, w, own, recv, m, v)


def _cast_place(w, chip):
    rows, cols = w.shape
    tr = _row_block(rows)

    def body(q_ref, w_ref, o_ref):
        o_ref[...] = w_ref[...].astype(BF16)

    return pl.pallas_call(
        body, name="cast_place", interpret=False,
        grid_spec=pltpu.PrefetchScalarGridSpec(
            num_scalar_prefetch=1, grid=(rows // tr,),
            in_specs=[pl.BlockSpec((tr, cols), lambda i, q: (i, 0))],
            out_specs=pl.BlockSpec((None, tr, cols), lambda i, q: (q[0], i, 0))),
        out_shape=_sds((NQ, rows, cols), BF16),
        compiler_params=_params(("arbitrary",), None),
    )(chip, w)


def _ada_grad_adamw(cact_t, dmod_q, w, m, v):
    rows, cols = w.shape
    tc = 256
    assert cols % tc == 0

    def body(c_ref, d_ref, w_ref, m_ref, v_ref, g_ref, dl_ref, mo_ref, vo_ref):
        g = jnp.dot(c_ref[...], d_ref[...], precision=HIGHEST, preferred_element_type=F32)
        d, mn, vn = _adamw(w_ref[...], g, m_ref[...], v_ref[...])
        g_ref[...] = g
        dl_ref[...] = d
        mo_ref[...] = mn
        vo_ref[...] = vn

    spec = pl.BlockSpec((rows, tc), lambda i: (0, i))
    return _call(
        body, name="ada_grad_adamw",
        grid=(cols // tc,),
        in_specs=[pl.BlockSpec((rows, 8), lambda i: (0, 0)), pl.BlockSpec((8, tc), lambda i: (0, i)),
                  spec, spec, spec],
        out_specs=[spec] * 4,
        out_shape=[_sds((rows, cols), F32)] * 4,
        compiler_params=_params(("arbitrary",), None),
    )(cact_t, dmod_q, w, m, v)


def _me():
    x, y, c = lax.axis_index("x"), lax.axis_index("y"), lax.axis_index("c")
    return x, y, c


_OFFSETS7 = [(dx, dy, dc) for dx in (0, 1) for dy in (0, 1) for dc in (0, 1) if (dx, dy, dc) != (0, 0, 0)]
_CHIP_OFFSETS = [(1, 0), (0, 1), (1, 1)]


def _ada_fwd(c, w_ada_q, b_ada_q):
    ncol = w_ada_q.shape[1]

    def body(c_ref, w_ref, b_ref, cact_ref, modsel_ref, blk, gath, res, parts, send_sems, recv_sems):
        x, y, cc = _me()
        me = 4 * x + 2 * y + cc
        q = 2 * x + y
        cv = c_ref[...]
        ca = cv * jax.nn.sigmoid(cv)
        row = lax.broadcasted_iota(jnp.int32, (8, D), 0)
        blk[...] = jnp.where(row == me, jnp.broadcast_to(ca, (8, D)), 0.0)
        gath[me] = blk[...]
        sends = []
        for k, (dx, dy, dc) in enumerate(_OFFSETS7):
            cp = pltpu.make_async_remote_copy(blk, gath.at[me], send_sems.at[k], recv_sems.at[k],
                                              device_id=(x ^ dx, y ^ dy, cc ^ dc), device_id_type=MESH)
            cp.start()
            sends.append(cp)
        for cp in sends:
            cp.wait_recv()
        cact = gath[0]
        for d in range(1, N_DEV):
            cact = cact + gath[d]
        cact_ref[...] = cact
        res[...] = jnp.dot(cact, w_ref[...], precision=HIGHEST, preferred_element_type=F32) + b_ref[...]
        parts[q] = res[...]
        sends2 = []
        for k, (dx, dy) in enumerate(_CHIP_OFFSETS):
            cp = pltpu.make_async_remote_copy(res, parts.at[q], send_sems.at[7 + k], recv_sems.at[7 + k],
                                              device_id=(x ^ dx, y ^ dy, cc), device_id_type=MESH)
            cp.start()
            sends2.append(cp)
        for cp in sends2:
            cp.wait_recv()
        row2 = lax.broadcasted_iota(jnp.int32, (8, ncol), 0)
        out = jnp.zeros((8, ncol), F32)
        for s in range(NQ):
            mine = jnp.sum(jnp.where(row2 == me, parts[s], 0.0), axis=0, keepdims=True)
            out = out + jnp.where(row2 == s, jnp.broadcast_to(mine, (8, ncol)), 0.0)
        modsel_ref[...] = out
        for cp in sends + sends2:
            cp.wait_send()

    return _call(
        body, name="ada_fwd",
        in_specs=[VMEM, VMEM, VMEM], out_specs=[VMEM, VMEM],
        out_shape=[_sds((8, D), F32), _sds((8, ncol), F32)],
        scratch_shapes=[pltpu.VMEM((8, D), F32), pltpu.VMEM((N_DEV, 8, D), F32), pltpu.VMEM((8, ncol), F32),
                        pltpu.VMEM((NQ, 8, ncol), F32), pltpu.SemaphoreType.DMA((10,)), pltpu.SemaphoreType.DMA((10,))],
        compiler_params=_params(None, VMEM_LIMIT),
    )(c, w_ada_q, b_ada_q)


class _Side:
    def __init__(self, ins, out_shapes, aliases, nsem, start, mid=None, finish=None):
        self.ins, self.out_shapes, self.aliases, self.nsem = list(ins), list(out_shapes), dict(aliases), nsem
        self.start, self.mid, self.finish = start, mid, finish


def _join(*sides):
    ins, outs, aliases, offs, nsem = [], [], {}, [], 0
    for s in sides:
        offs.append((len(ins), len(outs), nsem))
        aliases.update({len(ins) + a: len(outs) + b for a, b in s.aliases.items()})
        ins += s.ins
        outs += s.out_shapes
        nsem += s.nsem

    def hook(name):
        def run(i, o, ss, rs, base):
            for s, (io, oo, so) in zip(sides, offs):
                fn = getattr(s, name)
                if fn is not None:
                    fn(i[io:io + len(s.ins)], o[oo:oo + len(s.out_shapes)], ss, rs, base + so)
        return run

    return _Side(ins, outs, aliases, nsem, hook("start"), hook("mid"), hook("finish"))


def _side_call(body, side, when, *, name, in_specs, out_specs, out_shape, scratch_shapes, args, **kw):
    n_in, n_out = len(in_specs), len(out_specs)
    if side is None:
        return _call(body, name=name, in_specs=in_specs, out_specs=out_specs, out_shape=out_shape,
                     scratch_shapes=scratch_shapes, **kw)(*args), []
    ns_in, ns_out = len(side.ins), len(side.out_shapes)

    def hook(fn, k, operands):
        if fn is None:
            return
        if when is None:
            fn(*operands, 0)
        elif when[k] is not None:
            pl.when(when[k]())(functools.partial(fn, *operands, 0))

    def wrapped(*refs):
        ins, s_ins = refs[:n_in], refs[n_in:n_in + ns_in]
        o0 = n_in + ns_in
        outs, s_outs = refs[o0:o0 + n_out], refs[o0 + n_out:o0 + n_out + ns_out]
        rest = refs[o0 + n_out + ns_out:]
        scratch, operands = rest[:-2], (s_ins, s_outs, rest[-2], rest[-1])
        hook(side.start, 0, operands)
        body(*ins, *outs, *scratch)
        hook(side.mid, 1, operands)
        hook(side.finish, 2, operands)

    res = _call(
        wrapped, name=name,
        in_specs=list(in_specs) + [ANY] * ns_in, out_specs=list(out_specs) + [ANY] * ns_out,
        out_shape=list(out_shape) + side.out_shapes,
        scratch_shapes=list(scratch_shapes) + [pltpu.SemaphoreType.DMA((side.nsem,)),
                                               pltpu.SemaphoreType.DMA((side.nsem,))],
        input_output_aliases={n_in + a: n_out + b for a, b in side.aliases.items()},
        **kw)(*args, *side.ins)
    return res[:n_out], res[n_out:]


def _run_side(side, name):
    return _side_call(lambda: None, side, None, name=name, in_specs=[], out_specs=[], out_shape=[],
                      scratch_shapes=[], args=[])[1]


def _remote(src, dst, ss, rs, k, dev):
    return pltpu.make_async_remote_copy(src, dst, ss.at[k], rs.at[k], device_id=dev, device_id_type=MESH)


def _gather_side(bufs):
    n = len(bufs)

    def walk(outs, half):
        x, y, cc = _me()
        for w in range(n):
            hr = outs[w].shape[1] // 2
            rows = pl.ds((cc if half == "mine" else 1 - cc) * hr, hr)
            for j, (dx, dy) in enumerate(_CHIP_OFFSETS):
                yield w, j, (x ^ dx, y ^ dy, cc), outs[w].at[2 * (x ^ dx) + (y ^ dy), rows], outs[w].at[2 * x + y, rows]

    def start(ins, outs, ss, rs, b):
        for w, j, peer, _, own in walk(outs, "mine"):
            _remote(own, own, ss, rs, b + 6 * w + j, peer).start()

    def mid(ins, outs, ss, rs, b):
        x, y, cc = _me()
        for w, j, peer, land, _ in walk(outs, "mine"):
            _remote(land, land, ss, rs, b + 6 * w + j, peer).wait_recv()
            _remote(land, land, ss, rs, b + 6 * w + 3 + j, (x, y, 1 - cc)).start()

    def finish(ins, outs, ss, rs, b):
        x, y, cc = _me()
        for w, j, _, land, _ in walk(outs, "other"):
            _remote(land, land, ss, rs, b + 6 * w + 3 + j, (x, y, 1 - cc)).wait_recv()
        for w, j, peer, land, own in walk(outs, "mine"):
            _remote(own, own, ss, rs, b + 6 * w + j, peer).wait_send()
            _remote(land, land, ss, rs, b + 6 * w + 3 + j, (x, y, 1 - cc)).wait_send()

    return _Side(bufs, [_sds(tuple(w.shape), w.dtype) for w in bufs], {i: i for i in range(n)}, 6 * n,
                 start, mid, finish)


def _copies_side(ins, out_shapes, nsem, copies):
    def start(*a):
        for cp in copies(*a):
            cp.start()

    def finish(*a):
        for cp in copies(*a):
            cp.wait()

    return _Side(ins, out_shapes, {}, nsem, start, None, finish)


def _swap_side(gs):
    def copies(ins, outs, ss, rs, b):
        x, y, cc = _me()
        return [_remote(ins[w].at[:, 1 - cc], outs[w], ss, rs, b + w, (x, y, 1 - cc)) for w in range(len(gs))]

    return _copies_side(gs, [_sds((NQ,) + tuple(g.shape[2:]), F32) for g in gs], len(gs), copies)


def _exchange_side(cs):
    def copies(ins, outs, ss, rs, b):
        x, y, cc = _me()
        return [_remote(ins[w].at[2 * (x ^ dx) + (y ^ dy)], outs[w].at[2 * x + y], ss, rs, b + 3 * w + j,
                        (x ^ dx, y ^ dy, cc))
                for w in range(len(cs)) for j, (dx, dy) in enumerate(_CHIP_OFFSETS)]

    return _copies_side(cs, [_sds(tuple(c.shape), c.dtype) for c in cs], 3 * len(cs), copies)


def _share_side(fs):
    def copies(ins, outs, ss, rs, b):
        x, y, cc = _me()
        return [_remote(ins[w], outs[w], ss, rs, b + w, (x, y, 1 - cc)) for w in range(len(fs))]

    return _copies_side(fs, [_sds(tuple(f.shape), F32) for f in fs], len(fs), copies)


def _small_allreduce_adamw(g, w, m, v):
    rows = g.shape[0]

    def body(g_ref, w_ref, m_ref, v_ref, gs_ref, d_ref, mo_ref, vo_ref, gath, send_sems, recv_sems, loc_sem):
        x, y, cc = _me()
        me = 4 * x + 2 * y + cc
        loc = pltpu.make_async_copy(g_ref, gath.at[me], loc_sem)
        loc.start()
        cps = []
        for k, (dx, dy, dc) in enumerate(_OFFSETS7):
            cp = pltpu.make_async_remote_copy(g_ref, gath.at[me], send_sems.at[k], recv_sems.at[k],
                                              device_id=(x ^ dx, y ^ dy, cc ^ dc), device_id_type=MESH)
            cp.start()
            cps.append(cp)
        loc.wait()
        for cp in cps:
            cp.wait()
        tot = gath[0]
        for dev in range(1, N_DEV):
            tot = tot + gath[dev]
        gs_ref[...] = tot
        d, mn, vn = _adamw(w_ref[...], tot, m_ref[...], v_ref[...])
        d_ref[...] = d
        mo_ref[...] = mn
        vo_ref[...] = vn

    return _call(
        body, name="small_allreduce_adamw",
        in_specs=[VMEM] * 4, out_specs=[VMEM] * 5,
        out_shape=[_sds((rows, LANE), F32)] * 4 + [_sds((N_DEV, rows, LANE), F32)],
        scratch_shapes=[pltpu.SemaphoreType.DMA((7,)), pltpu.SemaphoreType.DMA((7,)), pltpu.SemaphoreType.DMA],
        compiler_params=_params(None, VMEM_LIMIT),
    )(g, w, m, v)


_SMALL = ["b_ada", "norm_ffn1_g", "norm_mix_g", "pool_scale", "gmlp_ln_g", "gmlp_ln_b", "b_spatial",
          "norm_ffn2_g", "norm_final_g", "w_pool", "w_spatial"]


def _pack(parts):
    blocks, layout, r0 = [], {}, 0
    for name in _SMALL:
        a = parts[name]
        n = a.size
        rows = -(-n // LANE)
        rows8 = -(-rows // 8) * 8
        flat = a.reshape(-1).astype(F32)
        if rows8 * LANE != n:
            flat = jnp.concatenate([flat, jnp.zeros((rows8 * LANE - n,), F32)])
        blocks.append(flat.reshape(rows8, LANE))
        layout[name] = (r0, n, a.shape)
        r0 += rows8
    return jnp.concatenate(blocks, axis=0), layout


def _unpack(packed, layout):
    out = {}
    for name, (r0, n, shape) in layout.items():
        rows = -(-n // LANE)
        out[name] = packed[r0:r0 + rows].reshape(-1)[:n].reshape(shape)
    return out


def _modv(mod9, sub, gain):
    rows = jnp.concatenate([mod9[3 * sub:3 * sub + 3], gain.reshape(1, D), jnp.zeros((4, D), F32)], axis=0)
    return rows


_BIG = ["ffn1_w_in", "ffn1_w_out", "w_mix_in", "w_mix_out", "ffn2_w_in", "ffn2_w_out"]


def kernel(x, c, w_ada, b_ada, norm_ffn1_g, ffn1_w_in, ffn1_w_out, norm_mix_g, w_mix_in, w_pool, pool_scale, gmlp_ln_g, gmlp_ln_b, w_spatial, b_spatial, w_mix_out, norm_ffn2_g, ffn2_w_in, ffn2_w_out, norm_final_g, loss_target, m_w_ada, m_b_ada, m_norm_ffn1_g, m_ffn1_w_in, m_ffn1_w_out, m_norm_mix_g, m_w_mix_in, m_w_pool, m_pool_scale, m_gmlp_ln_g, m_gmlp_ln_b, m_w_spatial, m_b_spatial, m_w_mix_out, m_norm_ffn2_g, m_ffn2_w_in, m_ffn2_w_out, m_norm_final_g, v_w_ada, v_b_ada, v_norm_ffn1_g, v_ffn1_w_in, v_ffn1_w_out, v_norm_mix_g, v_w_mix_in, v_w_pool, v_pool_scale, v_gmlp_ln_g, v_gmlp_ln_b, v_w_spatial, v_b_spatial, v_w_mix_out, v_norm_ffn2_g, v_ffn2_w_in, v_ffn2_w_out, v_norm_final_g):
    names = ["w_ada", "b_ada", "norm_ffn1_g", "ffn1_w_in", "ffn1_w_out", "norm_mix_g", "w_mix_in", "w_pool",
             "pool_scale", "gmlp_ln_g", "gmlp_ln_b", "w_spatial", "b_spatial", "w_mix_out", "norm_ffn2_g",
             "ffn2_w_in", "ffn2_w_out", "norm_final_g"]
    W = dict(zip(names, [w_ada, b_ada, norm_ffn1_g, ffn1_w_in, ffn1_w_out, norm_mix_g, w_mix_in, w_pool, pool_scale,
                         gmlp_ln_g, gmlp_ln_b, w_spatial, b_spatial, w_mix_out, norm_ffn2_g, ffn2_w_in, ffn2_w_out,
                         norm_final_g]))
    M = dict(zip(names, [m_w_ada, m_b_ada, m_norm_ffn1_g, m_ffn1_w_in, m_ffn1_w_out, m_norm_mix_g, m_w_mix_in, m_w_pool,
                         m_pool_scale, m_gmlp_ln_g, m_gmlp_ln_b, m_w_spatial, m_b_spatial, m_w_mix_out, m_norm_ffn2_g,
                         m_ffn2_w_in, m_ffn2_w_out, m_norm_final_g]))
    V = dict(zip(names, [v_w_ada, v_b_ada, v_norm_ffn1_g, v_ffn1_w_in, v_ffn1_w_out, v_norm_mix_g, v_w_mix_in, v_w_pool,
                         v_pool_scale, v_gmlp_ln_g, v_gmlp_ln_b, v_w_spatial, v_b_spatial, v_w_mix_out, v_norm_ffn2_g,
                         v_ffn2_w_in, v_ffn2_w_out, v_norm_final_g]))

    xi, yi, ci = _me()
    q = 2 * xi + yi
    core = ci.astype(jnp.int32).reshape(1)

    ncol = w_ada.shape[2]
    b_q = lax.dynamic_slice_in_dim(b_ada, q * ncol, ncol, axis=1)
    cact_all, modsel = _ada_fwd(c, w_ada[0], b_q)
    mod9 = modsel[:NQ].reshape(9, D)

    chip = q.astype(jnp.int32).reshape(1)
    place = lambda n: _cast_place(W[n][0], chip)
    xs, target = x[0], loss_target[0]
    mv1 = _modv(mod9, 0, norm_ffn1_g[0])
    mv2 = _modv(mod9, 1, norm_mix_g[0])
    mv3 = _modv(mod9, 2, norm_ffn2_g[0])
    wcat, wtcat, bias = _prep_spatial(w_spatial[0], b_spatial[0].T)
    wpool = w_pool[0].astype(BF16)
    vecs = jnp.concatenate([pool_scale, gmlp_ln_g, gmlp_ln_b, jnp.zeros((5, DP), F32)], axis=0)
    gf = jnp.concatenate([norm_final_g.reshape(1, D), jnp.zeros((7, D), F32)], axis=0)

    win1, wout1 = _run_side(_gather_side([place("ffn1_w_in"), place("ffn1_w_out")]), "gather_ffn1")
    later = ["w_mix_in", "w_mix_out", "ffn2_w_in", "ffn2_w_out"]
    (x1, g1s, u1s), got = _ffn_fwd(xs, mv1, win1, wout1.reshape(2, CH, D), side=_gather_side([place(n) for n in later]))
    wmi, wmo, win2, wout2 = got
    wmi = jnp.transpose(wmi, (1, 0, 2)).reshape(D, DPROJ)
    wmo = wmo.reshape(DP + DG, D)
    x2, pooled, zb = _mix_fwd(x1, mv2, wmi, wpool, vecs, wcat, bias, wmo)
    (x3, g3s, u3s), _ = _ffn_fwd(x2, mv3, win2, wout2.reshape(2, CH, D))
    dx3, loss_blk, dgf = _head(x3, target, gf)
    loss = lax.psum(loss_blk[0, 0], ("x", "y", "c"))

    (dx2, oin2, oout2, rin2, rout2, vec3), _ = _ffn_bwd(x2, dx3, g3s, u3s, mv3, win2, wout2.reshape(2, CH, D))
    cs2 = [_chip_sum_pair(oin2, rin2), _chip_sum_pair(oout2, rout2)]
    (dx1, dwmi, dwmo, dwpool, dwsp, dbsp, v512, vec2), ex2 = _mix_bwd(
        x1, dx2, pooled, zb, mv2, wmi, wpool, vecs, wcat, wtcat, bias, wmo, side=_exchange_side(cs2))
    half2 = [_sum4(cs, e, chip) for cs, e in zip(cs2, ex2)]
    qcols = w_mix_in.shape[2]
    vmix = [jnp.transpose(dwmi.reshape(D, NQ, qcols), (1, 0, 2)).reshape(NQ, 2, D // 2, qcols),
            dwmo.reshape(NQ, 2, (DP + DG) // 8, D)]
    (grad_x, oin1, oout1, rin1, rout1, vec1), got = _ffn_bwd(
        xs, dx1, g1s, u1s, mv1, win1, wout1.reshape(2, CH, D), side=_join(_swap_side(vmix), _share_side(half2)))
    sibmix, other2 = got[:2], got[2:]
    cs1 = [_chip_sum(g, r, core) for g, r in zip(vmix, sibmix)] + [_chip_sum_pair(oin1, rin1),
                                                                  _chip_sum_pair(oout1, rout1)]
    ex1 = _run_side(_exchange_side(cs1), "chip_exchange")
    half1 = [_sum4(cs, e, chip) for cs, e in zip(cs1, ex1)]
    other1 = _run_side(_share_side(half1), "sibling_share")
    reduced = dict(zip(["w_mix_in", "w_mix_out", "ffn1_w_in", "ffn1_w_out", "ffn2_w_in", "ffn2_w_out"],
                       zip(half1 + half2, list(other1) + list(other2))))
    out_g, out_d, out_m, out_v = {}, {}, {}, {}
    for n in _BIG:
        own, recv = reduced[n]
        g2, d, mn, vn = _adamw_halves(W[n][0], own, recv, M[n][0], V[n][0], core)
        out_g[n], out_d[n], out_m[n], out_v[n] = g2[None], d[None], mn[None], vn[None]

    dmod = jnp.concatenate([vec1[0:3], vec2[0:3], vec3[0:3]], axis=0)
    grads = dict(
        b_ada=dmod.reshape(1, 9 * D), norm_ffn1_g=vec1[3:4], norm_mix_g=vec2[3:4], norm_ffn2_g=vec3[3:4],
        pool_scale=v512[0:1], gmlp_ln_g=v512[1:2], gmlp_ln_b=v512[2:3], b_spatial=dbsp[None],
        norm_final_g=dgf[0], w_pool=dwpool[None], w_spatial=dwsp[None])

    gp, layout = _pack({n: grads[n] for n in _SMALL})
    wp, _ = _pack({n: W[n] for n in _SMALL})
    mp, _ = _pack({n: M[n] for n in _SMALL})
    vp, _ = _pack({n: V[n] for n in _SMALL})
    gs, dl, mo, vo, gath = _small_allreduce_adamw(gp, wp, mp, vp)
    for packed, dst in ((gs, out_g), (dl, out_d), (mo, out_m), (vo, out_v)):
        for n, a in _unpack(packed, layout).items():
            dst[n] = a.reshape(W[n].shape)

    r0, nb, _ = layout["b_ada"]
    dmod_all = gath[:, r0:r0 + nb // LANE, :].reshape(N_DEV, nb)
    dmod_q = lax.dynamic_slice_in_dim(dmod_all, q * ncol, ncol, axis=1)
    ga, da, ma, va = _ada_grad_adamw(cact_all.T, dmod_q, w_ada[0], m_w_ada[0], v_w_ada[0])
    out_g["w_ada"], out_d["w_ada"], out_m["w_ada"], out_v["w_ada"] = ga[None], da[None], ma[None], va[None]

    return (loss, grad_x[None], *[out_g[n] for n in names], *[out_d[n] for n in names],
            *[out_m[n] for n in names], *[out_v[n] for n in names])
```

```python
import functools
import math

import jax
import jax.numpy as jnp
from jax import lax
from jax.experimental import pallas as pl
from jax.experimental.pallas import tpu as pltpu

F32 = jnp.float32
BF16 = jnp.bfloat16
MESH = pl.DeviceIdType.MESH
HIGHEST = lax.Precision.HIGHEST

EPS = 1e-6
D = 1024
DFF = 2816
CH = DFF // 2
NQ = 4
DP = 512
DG = 512
DPROJ = DP + 2 * DG
POOL_WINDOWS = (2, 4, 8, 16)
HALO = 16
CHUNK = 128
LANE = 128
N_DEV = 8

ADAM_LR = 0.001
ADAM_B1 = 0.9
ADAM_B2 = 0.999
ADAM_EPS = 1e-08
ADAM_WD = 0.01
ADAM_STEP = 10

VMEM_LIMIT = 60 * 1024 * 1024

TM_FFN_FWD = 512
TM_FFN_BWD = 256
TM_MIX = 256
TM_HEAD = 512


def _call(body, **kw):
    return pl.pallas_call(body, interpret=False, **kw)


def _params(sem=None, vmem=None):
    return pltpu.CompilerParams(dimension_semantics=sem, vmem_limit_bytes=vmem)


def _sds(shape, dtype):
    return jax.ShapeDtypeStruct(shape, dtype)


ANY = pl.BlockSpec(memory_space=pl.ANY)
VMEM = pl.BlockSpec(memory_space=pltpu.VMEM)
SMEM = pl.BlockSpec(memory_space=pltpu.SMEM)


def _norm_mod(x, gn, sc, sh):
    r = lax.rsqrt(jnp.mean(x * x, axis=-1, keepdims=True) + EPS)
    xn = x * r
    hp = xn * gn
    return r, xn, hp, hp * (1.0 + sc) + sh


def _norm_mod_bwd(dh, r, xn, hp, gn, sc):
    one_sc = 1.0 + sc
    dsh = jnp.sum(dh, axis=0, keepdims=True)
    dsc = jnp.sum(dh * hp, axis=0, keepdims=True)
    dgn = jnp.sum(dh * one_sc * xn, axis=0, keepdims=True)
    dxn = dh * (gn * one_sc)
    dx = r * (dxn - xn * jnp.mean(dxn * xn, axis=-1, keepdims=True))
    return dsh, dsc, dgn, dx


def _dot(a, b):
    return jnp.dot(a, b, preferred_element_type=F32)


def _dot_nt(a, b):
    return lax.dot_general(a, b, (((1,), (1,)), ((), ())), preferred_element_type=F32)


def _dot_tn(a, b):
    return lax.dot_general(a, b, (((0,), (0,)), ((), ())), preferred_element_type=F32)


_GELU_C = math.sqrt(2.0 / math.pi)
_GELU_A = 0.044715


def _gelu_fwd_bwd(x):
    x2 = x * x
    t = jnp.tanh(_GELU_C * (x + _GELU_A * x * x2))
    g = 0.5 * x * (1.0 + t)
    dg = 0.5 * (1.0 + t) + 0.5 * x * (1.0 - t * t) * (_GELU_C * (1.0 + 3.0 * _GELU_A * x2))
    return g, dg


def _adamw(w, g, m, v):
    m = ADAM_B1 * m + (1.0 - ADAM_B1) * g
    v = ADAM_B2 * v + (1.0 - ADAM_B2) * (g * g)
    m_hat = m / (1.0 - ADAM_B1 ** ADAM_STEP)
    v_hat = v / (1.0 - ADAM_B2 ** ADAM_STEP)
    delta = -ADAM_LR * (m_hat / (jnp.sqrt(v_hat) + ADAM_EPS) + ADAM_WD * w)
    return delta, m, v


def _row_block(rows, cap=256, mult=16):
    best = None
    for t in range(mult, min(rows, cap) + 1, mult):
        if rows % t == 0:
            best = t
    assert best is not None, rows
    return best


def _ffn_fwd(x, modv, win, wout, side=None):
    S = x.shape[0]
    tm = TM_FFN_FWD
    nt = S // tm

    def body(x_ref, mod_ref, wg_ref, wu_ref, wo_ref, xo_ref, gs_ref, us_ref, h_scr, acc_scr):
        j = pl.program_id(1)

        @pl.when(j == 0)
        def _():
            _, _, _, h = _norm_mod(x_ref[...], mod_ref[3:4, :], mod_ref[1:2, :], mod_ref[0:1, :])
            h_scr[...] = h.astype(BF16)
            acc_scr[...] = jnp.zeros_like(acc_scr)

        h = h_scr[...]
        g = _dot(h, wg_ref[...]).astype(BF16)
        u = _dot(h, wu_ref[...]).astype(BF16)
        gs_ref[...] = g
        us_ref[...] = u
        gf = g.astype(F32)
        a = (gf * jax.nn.sigmoid(gf) * u.astype(F32)).astype(BF16)
        acc_scr[...] += _dot(a, wo_ref[...])

        @pl.when(j == 1)
        def _():
            xo_ref[...] = x_ref[...] + (0.5 * mod_ref[2:3, :]) * acc_scr[...]

    step = lambda i, j: lambda: (pl.program_id(0) == i) & (pl.program_id(1) == j)
    return _side_call(
        body, side, (step(0, 0), step((7 * nt) // 10, 0), step(nt - 1, 1)), name="ffn_fwd",
        grid=(nt, 2),
        in_specs=[
            pl.BlockSpec((tm, D), lambda i, j: (i, 0)),
            pl.BlockSpec((8, D), lambda i, j: (0, 0)),
            pl.BlockSpec((None, D, CH), lambda i, j: (j, 0, 0)),
            pl.BlockSpec((None, D, CH), lambda i, j: (2 + j, 0, 0)),
            pl.BlockSpec((None, CH, D), lambda i, j: (j, 0, 0)),
        ],
        out_specs=[
            pl.BlockSpec((tm, D), lambda i, j: (i, 0)),
            pl.BlockSpec((tm, CH), lambda i, j: (i, j)),
            pl.BlockSpec((tm, CH), lambda i, j: (i, j)),
        ],
        out_shape=[_sds((S, D), F32), _sds((S, DFF), BF16), _sds((S, DFF), BF16)],
        scratch_shapes=[pltpu.VMEM((tm, D), BF16), pltpu.VMEM((tm, D), F32)],
        compiler_params=_params(("arbitrary", "arbitrary"), VMEM_LIMIT),
        args=(x, modv, win, win, wout))


def _ffn_bwd(x, dxo, gs, us, modv, wint, woutt, wout, side=None):
    S = x.shape[0]
    tm = TM_FFN_BWD
    nt = S // tm
    assert nt >= 2
    hi, ho = D // 2, CH // 4
    wrows = CH // 8

    def body(x_ref, dxo_ref, gs_ref, us_ref, mod_ref, wgt_ref, wut_ref, wot_ref, wo_hbm,
             dx_ref, dwin_ref, dwout_ref, rwin_ref, rwout_ref, vec_ref, dhbuf_ref,
             accg, accu, accw, dh_st, dh_ld, wbuf, sems, fsend, frecv):
        j = pl.program_id(0)
        i = pl.program_id(1)
        store = lambda t: pltpu.make_async_copy(dh_st, dhbuf_ref.at[t], sems.at[4])
        load = lambda t: pltpu.make_async_copy(dhbuf_ref.at[t], dh_ld, sems.at[5])

        @pl.when(i == 0)
        def _():
            accg[...] = jnp.zeros_like(accg)
            accu[...] = jnp.zeros_like(accu)
            accw[...] = jnp.zeros_like(accw)

        @pl.when((i == 0) & (j == 0))
        def _():
            vec_ref[...] = jnp.zeros_like(vec_ref)

        @pl.when((j == 0) & (i > 0))
        def _():
            store(i - 1).wait()

        @pl.when((j == 1) & (i == 0))
        def _():
            store(nt - 1).wait()

        @pl.when(j == 1)
        def _():
            load(i).start()

        gn, sc, sh, gate = mod_ref[3:4, :], mod_ref[1:2, :], mod_ref[0:1, :], mod_ref[2:3, :]
        x = x_ref[...]
        r, xn, hp, h = _norm_mod(x, gn, sc, sh)
        hb = h.astype(BF16)
        dxo = dxo_ref[...]
        dy = (dxo * (0.5 * gate)).astype(BF16)
        g = gs_ref[...].astype(F32)
        u = us_ref[...].astype(F32)
        sig = jax.nn.sigmoid(g)
        sl = g * sig
        a = (sl * u).astype(BF16)
        accw[...] += _dot_tn(a, dxo.astype(BF16))
        da = _dot(dy, wot_ref[...])
        dg = (da * u * (sig * (1.0 + g * (1.0 - sig)))).astype(BF16)
        du = (da * sl).astype(BF16)
        accg[...] += _dot_tn(hb, dg)
        accu[...] += _dot_tn(hb, du)
        dhp = _dot(dg, wgt_ref[...]) + _dot(du, wut_ref[...])

        @pl.when(j == 0)
        def _():
            dh_st[...] = dhp
            store(i).start()

        @pl.when(j == 1)
        def _():
            load(i).wait()
            dh = dh_ld[...] + dhp
            dsh, dsc, dgn, dxin = _norm_mod_bwd(dh, r, xn, hp, gn, sc)
            vec_ref[0:1, :] += dsh
            vec_ref[1:2, :] += dsc
            vec_ref[3:4, :] += dgn
            dx_ref[...] = dxo + dxin

        def flush(jj):
            mx, my, cc = _me()
            rows = lambda base, n, c: pl.ds(base + c * n, n)
            pieces = [(accg, 0, hi, dwin_ref, rwin_ref, jj), (accu, 0, hi, dwin_ref, rwin_ref, 2 + jj),
                      (accw, 0, ho, dwout_ref, rwout_ref, 2 * jj), (accw, 2 * ho, ho, dwout_ref, rwout_ref, 2 * jj + 1)]
            loc = [pltpu.make_async_copy(acc.at[rows(base, n, cc)], own.at[slot], sems.at[k])
                   for k, (acc, base, n, own, _, slot) in enumerate(pieces)]
            rem = [pltpu.make_async_remote_copy(acc.at[rows(base, n, 1 - cc)], sib.at[slot], fsend.at[4 * jj + k],
                                                frecv.at[4 * jj + k], device_id=(mx, my, 1 - cc), device_id_type=MESH)
                   for k, (acc, base, n, _, sib, slot) in enumerate(pieces)]
            return loc, rem

        for jj in range(2):
            @pl.when((i == nt - 1) & (j == jj))
            def _(jj=jj):
                dgate = jnp.zeros((1, D), F32)
                for p in range(CH // wrows):
                    pltpu.sync_copy(wo_hbm.at[jj, pl.ds(p * wrows, wrows)], wbuf)
                    dgate += jnp.sum(wbuf[...].astype(F32) * accw[p * wrows:(p + 1) * wrows, :], axis=0, keepdims=True)
                vec_ref[2:3, :] += 0.5 * dgate
                accw[...] = accw[...] * (0.5 * gate)
                loc, rem = flush(jj)
                for cp in loc + rem:
                    cp.start()
                for cp in loc:
                    cp.wait()
                for cp in rem:
                    cp.wait_send()

        @pl.when((i == nt - 1) & (j == 1))
        def _():
            for jj in range(2):
                for cp in flush(jj)[1]:
                    cp.wait_recv()

    step = lambda jj, ii: lambda: (pl.program_id(0) == jj) & (pl.program_id(1) == ii)
    (dx, dwin, dwout, rwin, rwout, vec, _), extra = _side_call(
        body, side, (step(0, 0), None, step(1, nt - 1)), name="ffn_bwd",
        grid=(2, nt),
        in_specs=[
            pl.BlockSpec((tm, D), lambda j, i: (i, 0)),
            pl.BlockSpec((tm, D), lambda j, i: (i, 0)),
            pl.BlockSpec((tm, CH), lambda j, i: (i, j)),
            pl.BlockSpec((tm, CH), lambda j, i: (i, j)),
            pl.BlockSpec((8, D), lambda j, i: (0, 0)),
            pl.BlockSpec((None, CH, D), lambda j, i: (j, 0, 0)),
            pl.BlockSpec((None, CH, D), lambda j, i: (2 + j, 0, 0)),
            pl.BlockSpec((None, D, CH), lambda j, i: (j, 0, 0)),
            ANY,
        ],
        out_specs=[
            pl.BlockSpec((tm, D), lambda j, i: (i * j, 0)),
            ANY, ANY, ANY, ANY,
            pl.BlockSpec((8, D), lambda j, i: (0, 0)),
            ANY,
        ],
        out_shape=[_sds((S, D), F32), _sds((NQ, hi, CH), F32), _sds((NQ, ho, D), F32), _sds((NQ, hi, CH), F32),
                   _sds((NQ, ho, D), F32), _sds((8, D), F32), _sds((nt, tm, D), F32)],
        scratch_shapes=[pltpu.VMEM((D, CH), F32), pltpu.VMEM((D, CH), F32), pltpu.VMEM((CH, D), F32),
                        pltpu.VMEM((tm, D), F32), pltpu.VMEM((tm, D), F32), pltpu.VMEM((wrows, D), BF16),
                        pltpu.SemaphoreType.DMA((6,)), pltpu.SemaphoreType.DMA((8,)), pltpu.SemaphoreType.DMA((8,))],
        compiler_params=_params(("arbitrary", "arbitrary"), VMEM_LIMIT),
        args=(x, dxo, gs, us, modv, wint, wint, woutt, wout))
    return (dx, dwin, dwout, rwin, rwout, vec), extra


def _head(x, target, gf):
    S = x.shape[0]
    tm = TM_HEAD
    nt = S // tm

    def body(x_ref, t_ref, g_ref, dx_ref, loss_ref, dg_ref):
        i = pl.program_id(0)

        @pl.when(i == 0)
        def _():
            loss_ref[...] = jnp.zeros_like(loss_ref)
            dg_ref[...] = jnp.zeros_like(dg_ref)

        x = x_ref[...]
        gf_ = g_ref[0:1, :]
        r = lax.rsqrt(jnp.mean(x * x, axis=-1, keepdims=True) + EPS)
        xn = x * r
        err = xn * gf_ - t_ref[...]
        loss_ref[...] += (0.5 / D) * jnp.sum(err * err)
        dy = err * (1.0 / D)
        dg_ref[0:1, :] += jnp.sum(dy * xn, axis=0, keepdims=True)
        dxn = dy * gf_
        dx_ref[...] = r * (dxn - xn * jnp.mean(dxn * xn, axis=-1, keepdims=True))

    return _call(
        body, name="head",
        grid=(nt,),
        in_specs=[pl.BlockSpec((tm, D), lambda i: (i, 0)), pl.BlockSpec((tm, D), lambda i: (i, 0)),
                  pl.BlockSpec((8, D), lambda i: (0, 0))],
        out_specs=[pl.BlockSpec((tm, D), lambda i: (i, 0)), pl.BlockSpec((8, LANE), lambda i: (0, 0)),
                   pl.BlockSpec((8, D), lambda i: (0, 0))],
        out_shape=[_sds((S, D), F32), _sds((8, LANE), F32), _sds((8, D), F32)],
        compiler_params=_params(("arbitrary",), VMEM_LIMIT),
    )(x, target, gf)


def _prep_spatial(w_spatial, b_spatial_t):
    def body(w_ref, b_ref, wcat_ref, wtcat_ref, bias_ref):
        row = lax.broadcasted_iota(jnp.int32, (CHUNK, CHUNK), 0)
        col = lax.broadcasted_iota(jnp.int32, (CHUNK, CHUNK), 1)
        tril = col <= row
        for p in range(4):
            wa = jnp.where(tril, w_ref[2 * p], 0.0)
            wb = jnp.where(tril, w_ref[2 * p + 1], 0.0)
            wcat_ref[p] = jnp.concatenate([wa, wb], axis=1).astype(BF16)
            wtcat_ref[p] = jnp.concatenate([wa.T, wb.T], axis=1).astype(BF16)
        head = lax.broadcasted_iota(jnp.int32, (8, DG), 0)
        ch = lax.broadcasted_iota(jnp.int32, (8, DG), 1)
        spread = jnp.where(ch // 64 == head, 1.0, 0.0).astype(F32)
        bias_ref[...] = jnp.dot(b_ref[...], spread, precision=HIGHEST, preferred_element_type=F32)

    return _call(
        body, name="prep_spatial",
        in_specs=[VMEM, VMEM], out_specs=[VMEM, VMEM, VMEM],
        out_shape=[_sds((4, CHUNK, 2 * CHUNK), BF16), _sds((4, CHUNK, 2 * CHUNK), BF16), _sds((CHUNK, DG), F32)],
    )(w_spatial, b_spatial_t)


def _pair_rhs(blocks):
    lane = lax.broadcasted_iota(jnp.int32, (CHUNK, LANE), 1)
    lo = lane < 64
    top = jnp.concatenate([jnp.where(lo, b, 0.0) for b in blocks], axis=1)
    bot = jnp.concatenate([jnp.where(lo, 0.0, b) for b in blocks], axis=1)
    return top, bot


def _gmlp_branch(zb, vecs, wcat_ref, bias_ref, nchunks):
    z, dz = _gelu_fwd_bwd(zb)
    u = z[:, :DG]
    v = z[:, DG:]
    ln_g, ln_b = vecs[1:2, :], vecs[2:3, :]
    mu = jnp.mean(v, axis=-1, keepdims=True)
    vc = v - mu
    rstd = lax.rsqrt(jnp.mean(vc * vc, axis=-1, keepdims=True) + EPS)
    vhat = vc * rstd
    vl = vhat * ln_g + ln_b
    sv_cols = []
    for p in range(4):
        blocks = [vl[k * CHUNK:(k + 1) * CHUNK, p * LANE:(p + 1) * LANE] for k in range(nchunks)]
        top, bot = _pair_rhs(blocks)
        rhs = jnp.concatenate([top, bot], axis=0).astype(BF16)
        out = _dot(wcat_ref[p], rhs)
        bias = bias_ref[:, p * LANE:(p + 1) * LANE]
        sv_cols.append(jnp.concatenate([out[:, k * LANE:(k + 1) * LANE] + bias for k in range(nchunks)], axis=0))
    sv = jnp.concatenate(sv_cols, axis=1)
    return dict(u=u, dz=dz, rstd=rstd, vhat=vhat, vl=vl, sv=sv, yb=u * sv)


def _mix_fwd(x, modv, win, wpool, vecs, wcat, bias, wout):
    S = x.shape[0]
    tm = TM_MIX
    nt = S // tm
    nchunks = tm // CHUNK

    def body(x_ref, mod_ref, win_ref, wpool_ref, vec_ref, wcat_ref, bias_ref, wout_ref,
             xo_ref, pooled_ref, zb_ref, ext):
        i = pl.program_id(0)
        x = x_ref[...]
        _, _, _, h = _norm_mod(x, mod_ref[3:4, :], mod_ref[1:2, :], mod_ref[0:1, :])
        proj = _dot(h.astype(BF16), win_ref[...])
        xa = proj[:, :DP]
        zb = proj[:, DP:]
        zb_ref[...] = zb

        @pl.when(i == 0)
        def _():
            ext[0:HALO, :] = jnp.zeros((HALO, DP), F32)

        ext[HALO:HALO + tm, :] = xa
        pos = i * tm + lax.broadcasted_iota(jnp.int32, (tm, 1), 0)
        vecs = vec_ref[...]
        ya_cols = []
        pooled_cols = []
        for gi, w in enumerate(POOL_WINDOWS):
            cols = slice(gi * LANE, (gi + 1) * LANE)
            s = xa[:, cols]
            for k in range(1, w):
                s = s + ext[HALO - k:HALO - k + tm, cols]
            cnt = jnp.minimum(pos + 1, w).astype(F32)
            pooled = (s / cnt - xa[:, cols]).astype(BF16)
            pooled_cols.append(pooled)
            ya_cols.append(_dot(pooled, wpool_ref[gi]) * vecs[0:1, cols])
        pooled_ref[...] = jnp.concatenate(pooled_cols, axis=1)
        ext[0:HALO, :] = ext[tm:tm + HALO, :]

        gm = _gmlp_branch(zb, vecs, wcat_ref, bias_ref, nchunks)
        cat = jnp.concatenate(ya_cols + [gm["yb"]], axis=1).astype(BF16)
        xo_ref[...] = x + mod_ref[2:3, :] * _dot(cat, wout_ref[...])

    full = lambda shape: pl.BlockSpec(shape, lambda i: (0,) * len(shape))
    return _call(
        body, name="mix_fwd",
        grid=(nt,),
        in_specs=[pl.BlockSpec((tm, D), lambda i: (i, 0)), full((8, D)), full((D, DPROJ)),
                  full((4, LANE, LANE)), full((8, DP)), full((4, CHUNK, 2 * CHUNK)), full((CHUNK, DG)),
                  full((DP + DG, D))],
        out_specs=[pl.BlockSpec((tm, D), lambda i: (i, 0)), pl.BlockSpec((tm, DP), lambda i: (i, 0)),
                   pl.BlockSpec((tm, 2 * DG), lambda i: (i, 0))],
        out_shape=[_sds((S, D), F32), _sds((S, DP), BF16), _sds((S, 2 * DG), F32)],
        scratch_shapes=[pltpu.VMEM((tm + HALO, DP), F32)],
        compiler_params=_params(("arbitrary",), VMEM_LIMIT),
    )(x, modv, win, wpool, vecs, wcat, bias, wout)


def _mix_bwd(x, dxo, pooled, zb, modv, win, wpool, vecs, wcat, wtcat, bias, wout, side=None):
    S = x.shape[0]
    tm = TM_MIX
    nt = S // tm
    nchunks = tm // CHUNK

    def body(x_ref, dxo_ref, pooled_ref, zb_ref, mod_ref, win_ref, wpool_ref, vec_ref, wcat_ref, wtcat_ref,
             bias_ref, wout_ref,
             dx_ref, dwin_ref, dwout_ref, dwpool_ref, dwsp_ref, dbsp_ref, v512_ref, vd_ref, qext, dsv_acc):
        step = pl.program_id(0)
        tile = nt - 1 - step

        @pl.when(step == 0)
        def _():
            dwin_ref[...] = jnp.zeros_like(dwin_ref)
            dwout_ref[...] = jnp.zeros_like(dwout_ref)
            dwpool_ref[...] = jnp.zeros_like(dwpool_ref)
            dwsp_ref[...] = jnp.zeros_like(dwsp_ref)
            v512_ref[...] = jnp.zeros_like(v512_ref)
            vd_ref[...] = jnp.zeros_like(vd_ref)
            dsv_acc[...] = jnp.zeros_like(dsv_acc)
            qext[tm:tm + HALO, :] = jnp.zeros((HALO, DP), F32)

        gn, sc, sh, gate = mod_ref[3:4, :], mod_ref[1:2, :], mod_ref[0:1, :], mod_ref[2:3, :]
        vecs = vec_ref[...]
        x = x_ref[...]
        r, xn, hp, h = _norm_mod(x, gn, sc, sh)
        hb = h.astype(BF16)
        dxo = dxo_ref[...]

        pooled = pooled_ref[...]
        mixed_cols = [_dot(pooled[:, gi * LANE:(gi + 1) * LANE], wpool_ref[gi]) for gi in range(4)]
        mixed = jnp.concatenate(mixed_cols, axis=1)
        scale = vecs[0:1, :]
        gm = _gmlp_branch(zb_ref[...], vecs, wcat_ref, bias_ref, nchunks)
        cat = jnp.concatenate([mixed * scale, gm["yb"]], axis=1).astype(BF16)

        dwout_ref[...] += _dot_tn(cat, dxo.astype(BF16))
        dcat = _dot_nt((dxo * gate).astype(BF16), wout_ref[...])
        dya = dcat[:, :DP]
        dyb = dcat[:, DP:]

        v512_ref[0:1, :] += jnp.sum(dya * mixed, axis=0, keepdims=True)
        dmixed = (dya * scale).astype(BF16)
        pos = tile * tm + lax.broadcasted_iota(jnp.int32, (tm, 1), 0)
        dpooled_cols = []
        for gi, w in enumerate(POOL_WINDOWS):
            cols = slice(gi * LANE, (gi + 1) * LANE)
            dp = _dot_nt(dmixed[:, cols], wpool_ref[gi])
            dwpool_ref[gi] += _dot_tn(pooled[:, cols], dmixed[:, cols])
            cnt = jnp.minimum(pos + 1, w).astype(F32)
            qext[0:tm, cols] = dp / cnt
            dpooled_cols.append(dp)
        dxa_cols = []
        for gi, w in enumerate(POOL_WINDOWS):
            cols = slice(gi * LANE, (gi + 1) * LANE)
            s = qext[0:tm, cols]
            for k in range(1, w):
                s = s + qext[k:k + tm, cols]
            dxa_cols.append(s - dpooled_cols[gi])
        qext[tm:tm + HALO, :] = qext[0:HALO, :]

        u, sv, vl = gm["u"], gm["sv"], gm["vl"]
        du = dyb * sv
        dsv = dyb * u
        dvl_cols = []
        for p in range(4):
            cols = slice(p * LANE, (p + 1) * LANE)
            dblocks = [dsv[k * CHUNK:(k + 1) * CHUNK, cols] for k in range(nchunks)]
            vblocks = [vl[k * CHUNK:(k + 1) * CHUNK, cols] for k in range(nchunks)]
            tot = dblocks[0]
            for b in dblocks[1:]:
                tot = tot + b
            dsv_acc[:, cols] += tot
            top, bot = _pair_rhs(dblocks)
            out = _dot(wtcat_ref[p], jnp.concatenate([top, bot], axis=0).astype(BF16))
            dvl_cols.append(jnp.concatenate([out[:, k * LANE:(k + 1) * LANE] for k in range(nchunks)], axis=0))
            vcat = jnp.concatenate(vblocks, axis=1).astype(BF16)
            dwsp_ref[2 * p] += _dot_nt(top.astype(BF16), vcat)
            dwsp_ref[2 * p + 1] += _dot_nt(bot.astype(BF16), vcat)
        dvl = jnp.concatenate(dvl_cols, axis=1)
        vhat, rstd = gm["vhat"], gm["rstd"]
        v512_ref[1:2, :] += jnp.sum(dvl * vhat, axis=0, keepdims=True)
        v512_ref[2:3, :] += jnp.sum(dvl, axis=0, keepdims=True)
        dvh = dvl * vecs[1:2, :]
        dv = rstd * (dvh - jnp.mean(dvh, axis=-1, keepdims=True)
                     - vhat * jnp.mean(dvh * vhat, axis=-1, keepdims=True))
        dzb = jnp.concatenate([du, dv], axis=1) * gm["dz"]

        dproj = jnp.concatenate(dxa_cols + [dzb], axis=1).astype(BF16)
        dwin_ref[...] += _dot_tn(hb, dproj)
        dh = _dot_nt(dproj, win_ref[...])
        dsh, dsc, dgn, dxin = _norm_mod_bwd(dh, r, xn, hp, gn, sc)
        vd_ref[0:1, :] += dsh
        vd_ref[1:2, :] += dsc
        vd_ref[3:4, :] += dgn
        dx_ref[...] = dxo + dxin

        @pl.when(step == nt - 1)
        def _():
            gw = dwout_ref[...]
            vd_ref[2:3, :] += jnp.sum(wout_ref[...].astype(F32) * gw, axis=0, keepdims=True)
            dwout_ref[...] = gw * gate
            row = lax.broadcasted_iota(jnp.int32, (CHUNK, CHUNK), 0)
            col = lax.broadcasted_iota(jnp.int32, (CHUNK, CHUNK), 1)
            for hh in range(8):
                dwsp_ref[hh] = jnp.where(col <= row, dwsp_ref[hh], 0.0)
            head = lax.broadcasted_iota(jnp.int32, (8, DG), 0)
            ch = lax.broadcasted_iota(jnp.int32, (8, DG), 1)
            spread = jnp.where(ch // 64 == head, 1.0, 0.0).astype(F32)
            dbsp_ref[...] = lax.dot_general(spread, dsv_acc[...], (((1,), (1,)), ((), ())),
                                            precision=HIGHEST, preferred_element_type=F32)

    full = lambda shape: pl.BlockSpec(shape, lambda s: (0,) * len(shape))
    rev = lambda cols: pl.BlockSpec((tm, cols), lambda s: (nt - 1 - s, 0))
    step = lambda s: lambda: pl.program_id(0) == s
    return _side_call(
        body, side, (step(0), None, step(nt - 1)), name="mix_bwd",
        grid=(nt,),
        in_specs=[rev(D), rev(D), rev(DP), rev(2 * DG), full((8, D)), full((D, DPROJ)), full((4, LANE, LANE)),
                  full((8, DP)), full((4, CHUNK, 2 * CHUNK)), full((4, CHUNK, 2 * CHUNK)), full((CHUNK, DG)),
                  full((DP + DG, D))],
        out_specs=[rev(D), full((D, DPROJ)), full((DP + DG, D)), full((4, LANE, LANE)), full((8, CHUNK, CHUNK)),
                   full((8, CHUNK)), full((8, DP)), full((8, D))],
        out_shape=[_sds((S, D), F32), _sds((D, DPROJ), F32), _sds((DP + DG, D), F32), _sds((4, LANE, LANE), F32),
                   _sds((8, CHUNK, CHUNK), F32), _sds((8, CHUNK), F32), _sds((8, DP), F32), _sds((8, D), F32)],
        scratch_shapes=[pltpu.VMEM((tm + HALO, DP), F32), pltpu.VMEM((CHUNK, DG), F32)],
        compiler_params=_params(("arbitrary",), VMEM_LIMIT),
        args=(x, dxo, pooled, zb, modv, win, wpool, vecs, wcat, wtcat, bias, wout))


def _chip_sum(g, rbuf, core):
    _, _, hr, cols = g.shape
    tr = _row_block(hr)

    def body(c_ref, g_ref, r_ref, o_ref):
        o_ref[...] = (g_ref[...] + r_ref[...]).astype(BF16)

    return pl.pallas_call(
        body, name="chip_sum", interpret=False,
        grid_spec=pltpu.PrefetchScalarGridSpec(
            num_scalar_prefetch=1, grid=(NQ, hr // tr),
            in_specs=[pl.BlockSpec((None, None, tr, cols), lambda q, i, c: (q, c[0], i, 0)),
                      pl.BlockSpec((None, tr, cols), lambda q, i, c: (q, i, 0))],
            out_specs=pl.BlockSpec((None, tr, cols), lambda q, i, c: (q, i, 0))),
        out_shape=_sds((NQ, hr, cols), BF16),
        compiler_params=_params(("arbitrary", "arbitrary"), None),
    )(core, g, rbuf)


def _chip_sum_pair(own, rbuf):
    _, hr, cols = own.shape
    tr = _row_block(hr)

    def body(a_ref, b_ref, o_ref):
        o_ref[...] = (a_ref[...] + b_ref[...]).astype(BF16)

    spec = pl.BlockSpec((None, tr, cols), lambda q, i: (q, i, 0))
    return _call(
        body, name="chip_sum_pair",
        grid=(NQ, hr // tr),
        in_specs=[spec, spec], out_specs=spec,
        out_shape=_sds((NQ, hr, cols), BF16),
        compiler_params=_params(("arbitrary", "arbitrary"), None),
    )(own, rbuf)


def _sum4(cs, rbuf, chip):
    _, hr, cols = rbuf.shape
    tr = _row_block(hr)

    def body(q_ref, c_ref, r1_ref, r2_ref, r3_ref, o_ref):
        acc = c_ref[...].astype(F32)
        for r in (r1_ref, r2_ref, r3_ref):
            acc = acc + r[...].astype(F32)
        o_ref[...] = acc

    slot = lambda k: pl.BlockSpec((None, tr, cols), lambda i, q: ((q[0] + k) % NQ, i, 0))
    return pl.pallas_call(
        body, name="sum4", interpret=False,
        grid_spec=pltpu.PrefetchScalarGridSpec(
            num_scalar_prefetch=1, grid=(hr // tr,),
            in_specs=[slot(0), slot(1), slot(2), slot(3)],
            out_specs=pl.BlockSpec((tr, cols), lambda i, q: (i, 0))),
        out_shape=_sds((hr, cols), F32),
        compiler_params=_params(("arbitrary",), None),
    )(chip, cs, rbuf, rbuf, rbuf)


def _adamw_halves(w, own, recv, m, v, core):
    rows, cols = w.shape
    hr = rows // 2
    tr = _row_block(hr, mult=8)
    nb = hr // tr

    def body(c_ref, w_ref, own_ref, recv_ref, m_ref, v_ref, g_ref, d_ref, mo_ref, vo_ref):
        g = jnp.where(pl.program_id(0) == c_ref[0], own_ref[...], recv_ref[...])
        d, mn, vn = _adamw(w_ref[...], g, m_ref[...], v_ref[...])
        g_ref[...] = g
        d_ref[...] = d
        mo_ref[...] = mn
        vo_ref[...] = vn

    full = pl.BlockSpec((tr, cols), lambda h, i, c: (h * nb + i, 0))
    half = pl.BlockSpec((tr, cols), lambda h, i, c: (i, 0))
    return pl.pallas_call(
        body, name="adamw_halves", interpret=False,
        grid_spec=pltpu.PrefetchScalarGridSpec(
            num_scalar_prefetch=1, grid=(2, nb),
            in_specs=[full, half, half, full, full],
            out_specs=[full] * 4),
        out_shape=[_sds((rows, cols), F32)] * 4,
        compiler_params=_params(("arbitrary", "arbitrary"), None),
    )(core, w, own, recv, m, v)


def _cast_place(w, chip):
    rows, cols = w.shape
    tr = _row_block(rows)

    def body(q_ref, w_ref, o_ref):
        o_ref[...] = w_ref[...].astype(BF16)

    return pl.pallas_call(
        body, name="cast_place", interpret=False,
        grid_spec=pltpu.PrefetchScalarGridSpec(
            num_scalar_prefetch=1, grid=(rows // tr,),
            in_specs=[pl.BlockSpec((tr, cols), lambda i, q: (i, 0))],
            out_specs=pl.BlockSpec((None, tr, cols), lambda i, q: (q[0], i, 0))),
        out_shape=_sds((NQ, rows, cols), BF16),
        compiler_params=_params(("arbitrary",), None),
    )(chip, w)


def _ada_grad_adamw(cact_t, dmod_q, w, m, v):
    rows, cols = w.shape
    tc = 256
    assert cols % tc == 0

    def body(c_ref, d_ref, w_ref, m_ref, v_ref, g_ref, dl_ref, mo_ref, vo_ref):
        g = jnp.dot(c_ref[...], d_ref[...], precision=HIGHEST, preferred_element_type=F32)
        d, mn, vn = _adamw(w_ref[...], g, m_ref[...], v_ref[...])
        g_ref[...] = g
        dl_ref[...] = d
        mo_ref[...] = mn
        vo_ref[...] = vn

    spec = pl.BlockSpec((rows, tc), lambda i: (0, i))
    return _call(
        body, name="ada_grad_adamw",
        grid=(cols // tc,),
        in_specs=[pl.BlockSpec((rows, 8), lambda i: (0, 0)), pl.BlockSpec((8, tc), lambda i: (0, i)),
                  spec, spec, spec],
        out_specs=[spec] * 4,
        out_shape=[_sds((rows, cols), F32)] * 4,
        compiler_params=_params(("arbitrary",), None),
    )(cact_t, dmod_q, w, m, v)


def _me():
    x, y, c = lax.axis_index("x"), lax.axis_index("y"), lax.axis_index("c")
    return x, y, c


_OFFSETS7 = [(dx, dy, dc) for dx in (0, 1) for dy in (0, 1) for dc in (0, 1) if (dx, dy, dc) != (0, 0, 0)]
_CHIP_OFFSETS = [(1, 0), (0, 1), (1, 1)]


def _ada_fwd(c, w_ada_q, b_ada_q):
    ncol = w_ada_q.shape[1]

    def body(c_ref, w_ref, b_ref, cact_ref, modsel_ref, blk, gath, res, parts, send_sems, recv_sems):
        x, y, cc = _me()
        me = 4 * x + 2 * y + cc
        q = 2 * x + y
        cv = c_ref[...]
        ca = cv * jax.nn.sigmoid(cv)
        row = lax.broadcasted_iota(jnp.int32, (8, D), 0)
        blk[...] = jnp.where(row == me, jnp.broadcast_to(ca, (8, D)), 0.0)
        gath[me] = blk[...]
        sends = []
        for k, (dx, dy, dc) in enumerate(_OFFSETS7):
            cp = pltpu.make_async_remote_copy(blk, gath.at[me], send_sems.at[k], recv_sems.at[k],
                                              device_id=(x ^ dx, y ^ dy, cc ^ dc), device_id_type=MESH)
            cp.start()
            sends.append(cp)
        for cp in sends:
            cp.wait_recv()
        cact = gath[0]
        for d in range(1, N_DEV):
            cact = cact + gath[d]
        cact_ref[...] = cact
        res[...] = jnp.dot(cact, w_ref[...], precision=HIGHEST, preferred_element_type=F32) + b_ref[...]
        parts[q] = res[...]
        sends2 = []
        for k, (dx, dy) in enumerate(_CHIP_OFFSETS):
            cp = pltpu.make_async_remote_copy(res, parts.at[q], send_sems.at[7 + k], recv_sems.at[7 + k],
                                              device_id=(x ^ dx, y ^ dy, cc), device_id_type=MESH)
            cp.start()
            sends2.append(cp)
        for cp in sends2:
            cp.wait_recv()
        row2 = lax.broadcasted_iota(jnp.int32, (8, ncol), 0)
        out = jnp.zeros((8, ncol), F32)
        for s in range(NQ):
            mine = jnp.sum(jnp.where(row2 == me, parts[s], 0.0), axis=0, keepdims=True)
            out = out + jnp.where(row2 == s, jnp.broadcast_to(mine, (8, ncol)), 0.0)
        modsel_ref[...] = out
        for cp in sends + sends2:
            cp.wait_send()

    return _call(
        body, name="ada_fwd",
        in_specs=[VMEM, VMEM, VMEM], out_specs=[VMEM, VMEM],
        out_shape=[_sds((8, D), F32), _sds((8, ncol), F32)],
        scratch_shapes=[pltpu.VMEM((8, D), F32), pltpu.VMEM((N_DEV, 8, D), F32), pltpu.VMEM((8, ncol), F32),
                        pltpu.VMEM((NQ, 8, ncol), F32), pltpu.SemaphoreType.DMA((10,)), pltpu.SemaphoreType.DMA((10,))],
        compiler_params=_params(None, VMEM_LIMIT),
    )(c, w_ada_q, b_ada_q)


class _Side:
    def __init__(self, ins, out_shapes, aliases, nsem, start, mid=None, finish=None):
        self.ins, self.out_shapes, self.aliases, self.nsem = list(ins), list(out_shapes), dict(aliases), nsem
        self.start, self.mid, self.finish = start, mid, finish


def _join(*sides):
    ins, outs, aliases, offs, nsem = [], [], {}, [], 0
    for s in sides:
        offs.append((len(ins), len(outs), nsem))
        aliases.update({len(ins) + a: len(outs) + b for a, b in s.aliases.items()})
        ins += s.ins
        outs += s.out_shapes
        nsem += s.nsem

    def hook(name):
        def run(i, o, ss, rs, base):
            for s, (io, oo, so) in zip(sides, offs):
                fn = getattr(s, name)
                if fn is not None:
                    fn(i[io:io + len(s.ins)], o[oo:oo + len(s.out_shapes)], ss, rs, base + so)
        return run

    return _Side(ins, outs, aliases, nsem, hook("start"), hook("mid"), hook("finish"))


def _side_call(body, side, when, *, name, in_specs, out_specs, out_shape, scratch_shapes, args, **kw):
    n_in, n_out = len(in_specs), len(out_specs)
    if side is None:
        return _call(body, name=name, in_specs=in_specs, out_specs=out_specs, out_shape=out_shape,
                     scratch_shapes=scratch_shapes, **kw)(*args), []
    ns_in, ns_out = len(side.ins), len(side.out_shapes)

    def hook(fn, k, operands):
        if fn is None:
            return
        if when is None:
            fn(*operands, 0)
        elif when[k] is not None:
            pl.when(when[k]())(functools.partial(fn, *operands, 0))

    def wrapped(*refs):
        ins, s_ins = refs[:n_in], refs[n_in:n_in + ns_in]
        o0 = n_in + ns_in
        outs, s_outs = refs[o0:o0 + n_out], refs[o0 + n_out:o0 + n_out + ns_out]
        rest = refs[o0 + n_out + ns_out:]
        scratch, operands = rest[:-2], (s_ins, s_outs, rest[-2], rest[-1])
        hook(side.start, 0, operands)
        body(*ins, *outs, *scratch)
        hook(side.mid, 1, operands)
        hook(side.finish, 2, operands)

    res = _call(
        wrapped, name=name,
        in_specs=list(in_specs) + [ANY] * ns_in, out_specs=list(out_specs) + [ANY] * ns_out,
        out_shape=list(out_shape) + side.out_shapes,
        scratch_shapes=list(scratch_shapes) + [pltpu.SemaphoreType.DMA((side.nsem,)),
                                               pltpu.SemaphoreType.DMA((side.nsem,))],
        input_output_aliases={n_in + a: n_out + b for a, b in side.aliases.items()},
        **kw)(*args, *side.ins)
    return res[:n_out], res[n_out:]


def _run_side(side, name):
    return _side_call(lambda: None, side, None, name=name, in_specs=[], out_specs=[], out_shape=[],
                      scratch_shapes=[], args=[])[1]


def _remote(src, dst, ss, rs, k, dev):
    return pltpu.make_async_remote_copy(src, dst, ss.at[k], rs.at[k], device_id=dev, device_id_type=MESH)


def _gather_side(bufs):
    n = len(bufs)

    def walk(outs, half):
        x, y, cc = _me()
        for w in range(n):
            hr = outs[w].shape[1] // 2
            rows = pl.ds((cc if half == "mine" else 1 - cc) * hr, hr)
            for j, (dx, dy) in enumerate(_CHIP_OFFSETS):
                yield w, j, (x ^ dx, y ^ dy, cc), outs[w].at[2 * (x ^ dx) + (y ^ dy), rows], outs[w].at[2 * x + y, rows]

    def start(ins, outs, ss, rs, b):
        for w, j, peer, _, own in walk(outs, "mine"):
            _remote(own, own, ss, rs, b + 6 * w + j, peer).start()

    def mid(ins, outs, ss, rs, b):
        x, y, cc = _me()
        for w, j, peer, land, _ in walk(outs, "mine"):
            _remote(land, land, ss, rs, b + 6 * w + j, peer).wait_recv()
            _remote(land, land, ss, rs, b + 6 * w + 3 + j, (x, y, 1 - cc)).start()

    def finish(ins, outs, ss, rs, b):
        x, y, cc = _me()
        for w, j, _, land, _ in walk(outs, "other"):
            _remote(land, land, ss, rs, b + 6 * w + 3 + j, (x, y, 1 - cc)).wait_recv()
        for w, j, peer, land, own in walk(outs, "mine"):
            _remote(own, own, ss, rs, b + 6 * w + j, peer).wait_send()
            _remote(land, land, ss, rs, b + 6 * w + 3 + j, (x, y, 1 - cc)).wait_send()

    return _Side(bufs, [_sds(tuple(w.shape), w.dtype) for w in bufs], {i: i for i in range(n)}, 6 * n,
                 start, mid, finish)


def _copies_side(ins, out_shapes, nsem, copies):
    def start(*a):
        for cp in copies(*a):
            cp.start()

    def finish(*a):
        for cp in copies(*a):
            cp.wait()

    return _Side(ins, out_shapes, {}, nsem, start, None, finish)


def _swap_side(gs):
    def copies(ins, outs, ss, rs, b):
        x, y, cc = _me()
        return [_remote(ins[w].at[:, 1 - cc], outs[w], ss, rs, b + w, (x, y, 1 - cc)) for w in range(len(gs))]

    return _copies_side(gs, [_sds((NQ,) + tuple(g.shape[2:]), F32) for g in gs], len(gs), copies)


def _exchange_side(cs):
    def copies(ins, outs, ss, rs, b):
        x, y, cc = _me()
        return [_remote(ins[w].at[2 * (x ^ dx) + (y ^ dy)], outs[w].at[2 * x + y], ss, rs, b + 3 * w + j,
                        (x ^ dx, y ^ dy, cc))
                for w in range(len(cs)) for j, (dx, dy) in enumerate(_CHIP_OFFSETS)]

    return _copies_side(cs, [_sds(tuple(c.shape), c.dtype) for c in cs], 3 * len(cs), copies)


def _share_side(fs):
    def copies(ins, outs, ss, rs, b):
        x, y, cc = _me()
        return [_remote(ins[w], outs[w], ss, rs, b + w, (x, y, 1 - cc)) for w in range(len(fs))]

    return _copies_side(fs, [_sds(tuple(f.shape), F32) for f in fs], len(fs), copies)


def _small_allreduce_adamw(g, w, m, v):
    rows = g.shape[0]

    def body(g_ref, w_ref, m_ref, v_ref, gs_ref, d_ref, mo_ref, vo_ref, gath, send_sems, recv_sems, loc_sem):
        x, y, cc = _me()
        me = 4 * x + 2 * y + cc
        loc = pltpu.make_async_copy(g_ref, gath.at[me], loc_sem)
        loc.start()
        cps = []
        for k, (dx, dy, dc) in enumerate(_OFFSETS7):
            cp = pltpu.make_async_remote_copy(g_ref, gath.at[me], send_sems.at[k], recv_sems.at[k],
                                              device_id=(x ^ dx, y ^ dy, cc ^ dc), device_id_type=MESH)
            cp.start()
            cps.append(cp)
        loc.wait()
        for cp in cps:
            cp.wait()
        tot = gath[0]
        for dev in range(1, N_DEV):
            tot = tot + gath[dev]
        gs_ref[...] = tot
        d, mn, vn = _adamw(w_ref[...], tot, m_ref[...], v_ref[...])
        d_ref[...] = d
        mo_ref[...] = mn
        vo_ref[...] = vn

    return _call(
        body, name="small_allreduce_adamw",
        in_specs=[VMEM] * 4, out_specs=[VMEM] * 5,
        out_shape=[_sds((rows, LANE), F32)] * 4 + [_sds((N_DEV, rows, LANE), F32)],
        scratch_shapes=[pltpu.SemaphoreType.DMA((7,)), pltpu.SemaphoreType.DMA((7,)), pltpu.SemaphoreType.DMA],
        compiler_params=_params(None, VMEM_LIMIT),
    )(g, w, m, v)


_SMALL = ["b_ada", "norm_ffn1_g", "norm_mix_g", "pool_scale", "gmlp_ln_g", "gmlp_ln_b", "b_spatial",
          "norm_ffn2_g", "norm_final_g", "w_pool", "w_spatial"]


def _pack(parts):
    blocks, layout, r0 = [], {}, 0
    for name in _SMALL:
        a = parts[name]
        n = a.size
        rows = -(-n // LANE)
        rows8 = -(-rows // 8) * 8
        flat = a.reshape(-1).astype(F32)
        if rows8 * LANE != n:
            flat = jnp.concatenate([flat, jnp.zeros((rows8 * LANE - n,), F32)])
        blocks.append(flat.reshape(rows8, LANE))
        layout[name] = (r0, n, a.shape)
        r0 += rows8
    return jnp.concatenate(blocks, axis=0), layout


def _unpack(packed, layout):
    out = {}
    for name, (r0, n, shape) in layout.items():
        rows = -(-n // LANE)
        out[name] = packed[r0:r0 + rows].reshape(-1)[:n].reshape(shape)
    return out


def _modv(mod9, sub, gain):
    rows = jnp.concatenate([mod9[3 * sub:3 * sub + 3], gain.reshape(1, D), jnp.zeros((4, D), F32)], axis=0)
    return rows


_BIG = ["ffn1_w_in", "ffn1_w_out", "w_mix_in", "w_mix_out", "ffn2_w_in", "ffn2_w_out"]


def kernel(x, c, w_ada, b_ada, norm_ffn1_g, ffn1_w_in, ffn1_w_out, norm_mix_g, w_mix_in, w_pool, pool_scale, gmlp_ln_g, gmlp_ln_b, w_spatial, b_spatial, w_mix_out, norm_ffn2_g, ffn2_w_in, ffn2_w_out, norm_final_g, loss_target, m_w_ada, m_b_ada, m_norm_ffn1_g, m_ffn1_w_in, m_ffn1_w_out, m_norm_mix_g, m_w_mix_in, m_w_pool, m_pool_scale, m_gmlp_ln_g, m_gmlp_ln_b, m_w_spatial, m_b_spatial, m_w_mix_out, m_norm_ffn2_g, m_ffn2_w_in, m_ffn2_w_out, m_norm_final_g, v_w_ada, v_b_ada, v_norm_ffn1_g, v_ffn1_w_in, v_ffn1_w_out, v_norm_mix_g, v_w_mix_in, v_w_pool, v_pool_scale, v_gmlp_ln_g, v_gmlp_ln_b, v_w_spatial, v_b_spatial, v_w_mix_out, v_norm_ffn2_g, v_ffn2_w_in, v_ffn2_w_out, v_norm_final_g):
    names = ["w_ada", "b_ada", "norm_ffn1_g", "ffn1_w_in", "ffn1_w_out", "norm_mix_g", "w_mix_in", "w_pool",
             "pool_scale", "gmlp_ln_g", "gmlp_ln_b", "w_spatial", "b_spatial", "w_mix_out", "norm_ffn2_g",
             "ffn2_w_in", "ffn2_w_out", "norm_final_g"]
    W = dict(zip(names, [w_ada, b_ada, norm_ffn1_g, ffn1_w_in, ffn1_w_out, norm_mix_g, w_mix_in, w_pool, pool_scale,
                         gmlp_ln_g, gmlp_ln_b, w_spatial, b_spatial, w_mix_out, norm_ffn2_g, ffn2_w_in, ffn2_w_out,
                         norm_final_g]))
    M = dict(zip(names, [m_w_ada, m_b_ada, m_norm_ffn1_g, m_ffn1_w_in, m_ffn1_w_out, m_norm_mix_g, m_w_mix_in, m_w_pool,
                         m_pool_scale, m_gmlp_ln_g, m_gmlp_ln_b, m_w_spatial, m_b_spatial, m_w_mix_out, m_norm_ffn2_g,
                         m_ffn2_w_in, m_ffn2_w_out, m_norm_final_g]))
    V = dict(zip(names, [v_w_ada, v_b_ada, v_norm_ffn1_g, v_ffn1_w_in, v_ffn1_w_out, v_norm_mix_g, v_w_mix_in, v_w_pool,
                         v_pool_scale, v_gmlp_ln_g, v_gmlp_ln_b, v_w_spatial, v_b_spatial, v_w_mix_out, v_norm_ffn2_g,
                         v_ffn2_w_in, v_ffn2_w_out, v_norm_final_g]))

    xi, yi, ci = _me()
    q = 2 * xi + yi
    core = ci.astype(jnp.int32).reshape(1)

    ncol = w_ada.shape[2]
    b_q = lax.dynamic_slice_in_dim(b_ada, q * ncol, ncol, axis=1)
    cact_all, modsel = _ada_fwd(c, w_ada[0], b_q)
    mod9 = modsel[:NQ].reshape(9, D)

    chip = q.astype(jnp.int32).reshape(1)
    place = lambda n: _cast_place(W[n][0], chip)
    xs, target = x[0], loss_target[0]
    mv1 = _modv(mod9, 0, norm_ffn1_g[0])
    mv2 = _modv(mod9, 1, norm_mix_g[0])
    mv3 = _modv(mod9, 2, norm_ffn2_g[0])
    wcat, wtcat, bias = _prep_spatial(w_spatial[0], b_spatial[0].T)
    wpool = w_pool[0].astype(BF16)
    vecs = jnp.concatenate([pool_scale, gmlp_ln_g, gmlp_ln_b, jnp.zeros((5, DP), F32)], axis=0)
    gf = jnp.concatenate([norm_final_g.reshape(1, D), jnp.zeros((7, D), F32)], axis=0)

    win1, wout1 = _run_side(_gather_side([place("ffn1_w_in"), place("ffn1_w_out")]), "gather_ffn1")
    later = ["w_mix_in", "w_mix_out", "ffn2_w_in", "ffn2_w_out"]
    (x1, g1s, u1s), got = _ffn_fwd(xs, mv1, win1, wout1.reshape(2, CH, D), side=_gather_side([place(n) for n in later]))
    wmi, wmo, win2, wout2 = got
    wmi = jnp.transpose(wmi, (1, 0, 2)).reshape(D, DPROJ)
    wmo = wmo.reshape(DP + DG, D)
    x2, pooled, zb = _mix_fwd(x1, mv2, wmi, wpool, vecs, wcat, bias, wmo)
    (x3, g3s, u3s), _ = _ffn_fwd(x2, mv3, win2, wout2.reshape(2, CH, D))
    dx3, loss_blk, dgf = _head(x3, target, gf)
    loss = lax.psum(loss_blk[0, 0], ("x", "y", "c"))

    wo1, wo2 = wout1.reshape(2, CH, D), wout2.reshape(2, CH, D)
    tr = lambda w: jnp.swapaxes(w, 1, 2)
    (dx2, oin2, oout2, rin2, rout2, vec3), _ = _ffn_bwd(x2, dx3, g3s, u3s, mv3, tr(win2), tr(wo2), wo2)
    cs2 = [_chip_sum_pair(oin2, rin2), _chip_sum_pair(oout2, rout2)]
    (dx1, dwmi, dwmo, dwpool, dwsp, dbsp, v512, vec2), ex2 = _mix_bwd(
        x1, dx2, pooled, zb, mv2, wmi, wpool, vecs, wcat, wtcat, bias, wmo, side=_exchange_side(cs2))
    half2 = [_sum4(cs, e, chip) for cs, e in zip(cs2, ex2)]
    qcols = w_mix_in.shape[2]
    vmix = [jnp.transpose(dwmi.reshape(D, NQ, qcols), (1, 0, 2)).reshape(NQ, 2, D // 2, qcols),
            dwmo.reshape(NQ, 2, (DP + DG) // 8, D)]
    (grad_x, oin1, oout1, rin1, rout1, vec1), got = _ffn_bwd(
        xs, dx1, g1s, u1s, mv1, tr(win1), tr(wo1), wo1, side=_join(_swap_side(vmix), _share_side(half2)))
    sibmix, other2 = got[:2], got[2:]
    cs1 = [_chip_sum(g, r, core) for g, r in zip(vmix, sibmix)] + [_chip_sum_pair(oin1, rin1),
                                                                  _chip_sum_pair(oout1, rout1)]
    ex1 = _run_side(_exchange_side(cs1), "chip_exchange")
    half1 = [_sum4(cs, e, chip) for cs, e in zip(cs1, ex1)]
    other1 = _run_side(_share_side(half1), "sibling_share")
    reduced = dict(zip(["w_mix_in", "w_mix_out", "ffn1_w_in", "ffn1_w_out", "ffn2_w_in", "ffn2_w_out"],
                       zip(half1 + half2, list(other1) + list(other2))))
    out_g, out_d, out_m, out_v = {}, {}, {}, {}
    for n in _BIG:
        own, recv = reduced[n]
        g2, d, mn, vn = _adamw_halves(W[n][0], own, recv, M[n][0], V[n][0], core)
        out_g[n], out_d[n], out_m[n], out_v[n] = g2[None], d[None], mn[None], vn[None]

    dmod = jnp.concatenate([vec1[0:3], vec2[0:3], vec3[0:3]], axis=0)
    grads = dict(
        b_ada=dmod.reshape(1, 9 * D), norm_ffn1_g=vec1[3:4], norm_mix_g=vec2[3:4], norm_ffn2_g=vec3[3:4],
        pool_scale=v512[0:1], gmlp_ln_g=v512[1:2], gmlp_ln_b=v512[2:3], b_spatial=dbsp[None],
        norm_final_g=dgf[0], w_pool=dwpool[None], w_spatial=dwsp[None])

    gp, layout = _pack({n: grads[n] for n in _SMALL})
    wp, _ = _pack({n: W[n] for n in _SMALL})
    mp, _ = _pack({n: M[n] for n in _SMALL})
    vp, _ = _pack({n: V[n] for n in _SMALL})
    gs, dl, mo, vo, gath = _small_allreduce_adamw(gp, wp, mp, vp)
    for packed, dst in ((gs, out_g), (dl, out_d), (mo, out_m), (vo, out_v)):
        for n, a in _unpack(packed, layout).items():
            dst[n] = a.reshape(W[n].shape)

    r0, nb, _ = layout["b_ada"]
    dmod_all = gath[:, r0:r0 + nb // LANE, :].reshape(N_DEV, nb)
    dmod_q = lax.dynamic_slice_in_dim(dmod_all, q * ncol, ncol, axis=1)
    ga, da, ma, va = _ada_grad_adamw(cact_all.T, dmod_q, w_ada[0], m_w_ada[0], v_w_ada[0])
    out_g["w_ada"], out_d["w_ada"], out_m["w_ada"], out_v["w_ada"] = ga[None], da[None], ma[None], va[None]

    return (loss, grad_x[None], *[out_g[n] for n in names], *[out_d[n] for n in names],
            *[out_m[n] for n in names], *[out_v[n] for n in names])
```

```python
import functools
import math

import jax
import jax.numpy as jnp
from jax import lax
from jax.experimental import pallas as pl
from jax.experimental.pallas import tpu as pltpu

F32 = jnp.float32
BF16 = jnp.bfloat16
MESH = pl.DeviceIdType.MESH
HIGHEST = lax.Precision.HIGHEST

EPS = 1e-6
D = 1024
DFF = 2816
CH = DFF // 2
NQ = 4
DP = 512
DG = 512
DPROJ = DP + 2 * DG
POOL_WINDOWS = (2, 4, 8, 16)
HALO = 16
CHUNK = 128
LANE = 128
N_DEV = 8

ADAM_LR = 0.001
ADAM_B1 = 0.9
ADAM_B2 = 0.999
ADAM_EPS = 1e-08
ADAM_WD = 0.01
ADAM_STEP = 10

VMEM_LIMIT = 60 * 1024 * 1024

TM_FFN_FWD = 512
TM_FFN_BWD = 256
TM_MIX = 256


def _call(body, **kw):
    return pl.pallas_call(body, interpret=False, **kw)


def _params(sem=None, vmem=None):
    return pltpu.CompilerParams(dimension_semantics=sem, vmem_limit_bytes=vmem)


def _sds(shape, dtype):
    return jax.ShapeDtypeStruct(shape, dtype)


ANY = pl.BlockSpec(memory_space=pl.ANY)
VMEM = pl.BlockSpec(memory_space=pltpu.VMEM)
SMEM = pl.BlockSpec(memory_space=pltpu.SMEM)


def _norm_mod(x, gn, sc, sh):
    r = lax.rsqrt(jnp.mean(x * x, axis=-1, keepdims=True) + EPS)
    xn = x * r
    hp = xn * gn
    return r, xn, hp, hp * (1.0 + sc) + sh


def _norm_mod_bwd(dh, r, xn, hp, gn, sc):
    one_sc = 1.0 + sc
    dsh = jnp.sum(dh, axis=0, keepdims=True)
    dsc = jnp.sum(dh * hp, axis=0, keepdims=True)
    dgn = jnp.sum(dh * one_sc * xn, axis=0, keepdims=True)
    dxn = dh * (gn * one_sc)
    dx = r * (dxn - xn * jnp.mean(dxn * xn, axis=-1, keepdims=True))
    return dsh, dsc, dgn, dx


def _dot(a, b):
    return jnp.dot(a, b, preferred_element_type=F32)


def _dot_nt(a, b):
    return lax.dot_general(a, b, (((1,), (1,)), ((), ())), preferred_element_type=F32)


def _dot_tn(a, b):
    return lax.dot_general(a, b, (((0,), (0,)), ((), ())), preferred_element_type=F32)


_GELU_C = math.sqrt(2.0 / math.pi)
_GELU_A = 0.044715


def _gelu_fwd_bwd(x):
    x2 = x * x
    t = jnp.tanh(_GELU_C * (x + _GELU_A * x * x2))
    g = 0.5 * x * (1.0 + t)
    dg = 0.5 * (1.0 + t) + 0.5 * x * (1.0 - t * t) * (_GELU_C * (1.0 + 3.0 * _GELU_A * x2))
    return g, dg


def _adamw(w, g, m, v):
    m = ADAM_B1 * m + (1.0 - ADAM_B1) * g
    v = ADAM_B2 * v + (1.0 - ADAM_B2) * (g * g)
    m_hat = m / (1.0 - ADAM_B1 ** ADAM_STEP)
    v_hat = v / (1.0 - ADAM_B2 ** ADAM_STEP)
    delta = -ADAM_LR * (m_hat / (jnp.sqrt(v_hat) + ADAM_EPS) + ADAM_WD * w)
    return delta, m, v


def _row_block(rows, cap=256, mult=16):
    best = None
    for t in range(mult, min(rows, cap) + 1, mult):
        if rows % t == 0:
            best = t
    assert best is not None, rows
    return best


def _head_math(x, target, gf):
    r = lax.rsqrt(jnp.mean(x * x, axis=-1, keepdims=True) + EPS)
    xn = x * r
    err = xn * gf - target
    dy = err * (1.0 / D)
    dxn = dy * gf
    dx = r * (dxn - xn * jnp.mean(dxn * xn, axis=-1, keepdims=True))
    return (0.5 / D) * jnp.sum(err * err), jnp.sum(dy * xn, axis=0, keepdims=True), dx


def _ffn_fwd(x, modv, win, wout, side=None, head=None):
    S = x.shape[0]
    tm = TM_FFN_FWD
    nt = S // tm

    def body(*refs):
        if head is None:
            x_ref, mod_ref, wg_ref, wu_ref, wo_ref, xo_ref, gs_ref, us_ref, h_scr, acc_scr = refs
        else:
            (x_ref, mod_ref, wg_ref, wu_ref, wo_ref, t_ref, gf_ref,
             xo_ref, gs_ref, us_ref, loss_ref, dgf_ref, h_scr, acc_scr) = refs

            @pl.when((pl.program_id(0) == 0) & (pl.program_id(1) == 0))
            def _():
                loss_ref[...] = jnp.zeros_like(loss_ref)
                dgf_ref[...] = jnp.zeros_like(dgf_ref)

        j = pl.program_id(1)

        @pl.when(j == 0)
        def _():
            _, _, _, h = _norm_mod(x_ref[...], mod_ref[3:4, :], mod_ref[1:2, :], mod_ref[0:1, :])
            h_scr[...] = h.astype(BF16)
            acc_scr[...] = jnp.zeros_like(acc_scr)

        h = h_scr[...]
        g = _dot(h, wg_ref[...]).astype(BF16)
        u = _dot(h, wu_ref[...]).astype(BF16)
        gs_ref[...] = g
        us_ref[...] = u
        gf = g.astype(F32)
        a = (gf * jax.nn.sigmoid(gf) * u.astype(F32)).astype(BF16)
        acc_scr[...] += _dot(a, wo_ref[...])

        @pl.when(j == 1)
        def _():
            xo = x_ref[...] + (0.5 * mod_ref[2:3, :]) * acc_scr[...]
            if head is None:
                xo_ref[...] = xo
            else:
                loss, dgf, dx = _head_math(xo, t_ref[...], gf_ref[0:1, :])
                loss_ref[...] += loss
                dgf_ref[0:1, :] += dgf
                xo_ref[...] = dx

    step = lambda i, j: lambda: (pl.program_id(0) == i) & (pl.program_id(1) == j)
    tile = pl.BlockSpec((tm, D), lambda i, j: (i, 0))
    const = lambda shape: pl.BlockSpec(shape, lambda i, j: (0, 0))
    chunk = pl.BlockSpec((tm, CH), lambda i, j: (i, j))
    in_specs = [tile, const((8, D)), pl.BlockSpec((None, D, CH), lambda i, j: (j, 0, 0)),
                pl.BlockSpec((None, D, CH), lambda i, j: (2 + j, 0, 0)), pl.BlockSpec((None, CH, D), lambda i, j: (j, 0, 0))]
    out_specs = [tile, chunk, chunk]
    out_shape = [_sds((S, D), F32), _sds((S, DFF), BF16), _sds((S, DFF), BF16)]
    args = (x, modv, win, win, wout)
    if head is not None:
        in_specs += [tile, const((8, D))]
        out_specs += [const((8, LANE)), const((8, D))]
        out_shape += [_sds((8, LANE), F32), _sds((8, D), F32)]
        args += tuple(head)
    return _side_call(
        body, side, (step(0, 0), step((7 * nt) // 10, 0), step(nt - 1, 1)), name="ffn_fwd",
        grid=(nt, 2), in_specs=in_specs, out_specs=out_specs, out_shape=out_shape,
        scratch_shapes=[pltpu.VMEM((tm, D), BF16), pltpu.VMEM((tm, D), F32)],
        compiler_params=_params(("arbitrary", "arbitrary"), VMEM_LIMIT),
        args=args)


def _ffn_bwd(x, dxo, gs, us, modv, win, wout, side=None):
    S = x.shape[0]
    tm = TM_FFN_BWD
    nt = S // tm
    assert nt >= 2
    hi, ho = D // 2, CH // 4

    def body(x_ref, dxo_ref, gs_ref, us_ref, mod_ref, wg_ref, wu_ref, wo_ref,
             dx_ref, dwin_ref, dwout_ref, rwin_ref, rwout_ref, vec_ref, dhbuf_ref,
             accg, accu, accw, dh_st, dh_ld, sems, fsend, frecv):
        j = pl.program_id(0)
        i = pl.program_id(1)
        store = lambda t: pltpu.make_async_copy(dh_st, dhbuf_ref.at[t], sems.at[4])
        load = lambda t: pltpu.make_async_copy(dhbuf_ref.at[t], dh_ld, sems.at[5])

        @pl.when(i == 0)
        def _():
            accg[...] = jnp.zeros_like(accg)
            accu[...] = jnp.zeros_like(accu)
            accw[...] = jnp.zeros_like(accw)

        @pl.when((i == 0) & (j == 0))
        def _():
            vec_ref[...] = jnp.zeros_like(vec_ref)

        @pl.when((j == 1) & (i == 0))
        def _():
            store(nt - 1).wait()

        @pl.when(j == 1)
        def _():
            load(i).start()

        gn, sc, sh, gate = mod_ref[3:4, :], mod_ref[1:2, :], mod_ref[0:1, :], mod_ref[2:3, :]
        hb = _norm_mod(x_ref[...], gn, sc, sh)[3].astype(BF16)
        dxo = dxo_ref[...]
        dy = (dxo * (0.5 * gate)).astype(BF16)
        g = gs_ref[...].astype(F32)
        u = us_ref[...].astype(F32)
        sig = jax.nn.sigmoid(g)
        sl = g * sig
        a = (sl * u).astype(BF16)
        accw[...] += _dot_tn(a, dxo.astype(BF16))
        da = _dot_nt(dy, wo_ref[...])
        dg = (da * u * (sig * (1.0 + g * (1.0 - sig)))).astype(BF16)
        du = (da * sl).astype(BF16)
        accg[...] += _dot_tn(hb, dg)
        accu[...] += _dot_tn(hb, du)
        dhp = _dot_nt(dg, wg_ref[...]) + _dot_nt(du, wu_ref[...])

        @pl.when(j == 0)
        def _():
            @pl.when(i > 0)
            def _():
                store(i - 1).wait()

            dh_st[...] = dhp
            store(i).start()

        @pl.when(j == 1)
        def _():
            load(i).wait()
            dh = dh_ld[...] + dhp
            r, xn, hp, _ = _norm_mod(x_ref[...], gn, sc, sh)
            dsh, dsc, dgn, dxin = _norm_mod_bwd(dh, r, xn, hp, gn, sc)
            vec_ref[0:1, :] += dsh
            vec_ref[1:2, :] += dsc
            vec_ref[3:4, :] += dgn
            dx_ref[...] = dxo + dxin

        def flush(jj):
            mx, my, cc = _me()
            rows = lambda base, n, c: pl.ds(base + c * n, n)
            pieces = [(accg, 0, hi, dwin_ref, rwin_ref, jj), (accu, 0, hi, dwin_ref, rwin_ref, 2 + jj),
                      (accw, 0, ho, dwout_ref, rwout_ref, 2 * jj), (accw, 2 * ho, ho, dwout_ref, rwout_ref, 2 * jj + 1)]
            loc = [pltpu.make_async_copy(acc.at[rows(base, n, cc)], own.at[slot], sems.at[k])
                   for k, (acc, base, n, own, _, slot) in enumerate(pieces)]
            rem = [pltpu.make_async_remote_copy(acc.at[rows(base, n, 1 - cc)], sib.at[slot], fsend.at[4 * jj + k],
                                                frecv.at[4 * jj + k], device_id=(mx, my, 1 - cc), device_id_type=MESH)
                   for k, (acc, base, n, _, sib, slot) in enumerate(pieces)]
            return loc, rem

        for jj in range(2):
            @pl.when((i == nt - 1) & (j == jj))
            def _(jj=jj):
                gw = accw[...]
                vec_ref[2:3, :] += 0.5 * jnp.sum(wo_ref[...].astype(F32) * gw, axis=0, keepdims=True)
                accw[...] = gw * (0.5 * gate)
                loc, rem = flush(jj)
                for cp in loc + rem:
                    cp.start()
                for cp in loc:
                    cp.wait()
                for cp in rem:
                    cp.wait_send()

        @pl.when((i == nt - 1) & (j == 1))
        def _():
            for jj in range(2):
                for cp in flush(jj)[1]:
                    cp.wait_recv()

    step = lambda jj, ii: lambda: (pl.program_id(0) == jj) & (pl.program_id(1) == ii)
    (dx, dwin, dwout, rwin, rwout, vec, _), extra = _side_call(
        body, side, (step(0, 0), None, step(1, nt - 1)), name="ffn_bwd",
        grid=(2, nt),
        in_specs=[
            pl.BlockSpec((tm, D), lambda j, i: (i, 0)),
            pl.BlockSpec((tm, D), lambda j, i: (i, 0)),
            pl.BlockSpec((tm, CH), lambda j, i: (i, j)),
            pl.BlockSpec((tm, CH), lambda j, i: (i, j)),
            pl.BlockSpec((8, D), lambda j, i: (0, 0)),
            pl.BlockSpec((None, D, CH), lambda j, i: (j, 0, 0)),
            pl.BlockSpec((None, D, CH), lambda j, i: (2 + j, 0, 0)),
            pl.BlockSpec((None, CH, D), lambda j, i: (j, 0, 0)),
        ],
        out_specs=[
            pl.BlockSpec((tm, D), lambda j, i: (i * j, 0)),
            ANY, ANY, ANY, ANY,
            pl.BlockSpec((8, D), lambda j, i: (0, 0)),
            ANY,
        ],
        out_shape=[_sds((S, D), F32), _sds((NQ, hi, CH), F32), _sds((NQ, ho, D), F32), _sds((NQ, hi, CH), F32),
                   _sds((NQ, ho, D), F32), _sds((8, D), F32), _sds((nt, tm, D), F32)],
        scratch_shapes=[pltpu.VMEM((D, CH), F32), pltpu.VMEM((D, CH), F32), pltpu.VMEM((CH, D), F32),
                        pltpu.VMEM((tm, D), F32), pltpu.VMEM((tm, D), F32), pltpu.SemaphoreType.DMA((6,)),
                        pltpu.SemaphoreType.DMA((8,)), pltpu.SemaphoreType.DMA((8,))],
        compiler_params=_params(("arbitrary", "arbitrary"), VMEM_LIMIT),
        args=(x, dxo, gs, us, modv, win, win, wout))
    return (dx, dwin, dwout, rwin, rwout, vec), extra


def _prep_spatial(w_spatial, b_spatial_t):
    def body(w_ref, b_ref, wcat_ref, wtcat_ref, bias_ref):
        row = lax.broadcasted_iota(jnp.int32, (CHUNK, CHUNK), 0)
        col = lax.broadcasted_iota(jnp.int32, (CHUNK, CHUNK), 1)
        tril = col <= row
        for p in range(4):
            wa = jnp.where(tril, w_ref[2 * p], 0.0)
            wb = jnp.where(tril, w_ref[2 * p + 1], 0.0)
            wcat_ref[p] = jnp.concatenate([wa, wb], axis=1).astype(BF16)
            wtcat_ref[p] = jnp.concatenate([wa.T, wb.T], axis=1).astype(BF16)
        head = lax.broadcasted_iota(jnp.int32, (8, DG), 0)
        ch = lax.broadcasted_iota(jnp.int32, (8, DG), 1)
        spread = jnp.where(ch // 64 == head, 1.0, 0.0).astype(F32)
        bias_ref[...] = jnp.dot(b_ref[...], spread, precision=HIGHEST, preferred_element_type=F32)

    return _call(
        body, name="prep_spatial",
        in_specs=[VMEM, VMEM], out_specs=[VMEM, VMEM, VMEM],
        out_shape=[_sds((4, CHUNK, 2 * CHUNK), BF16), _sds((4, CHUNK, 2 * CHUNK), BF16), _sds((CHUNK, DG), F32)],
    )(w_spatial, b_spatial_t)


def _pair_rhs(blocks):
    lane = lax.broadcasted_iota(jnp.int32, (CHUNK, LANE), 1)
    lo = lane < 64
    top = jnp.concatenate([jnp.where(lo, b, 0.0) for b in blocks], axis=1)
    bot = jnp.concatenate([jnp.where(lo, 0.0, b) for b in blocks], axis=1)
    return top, bot


def _gmlp_branch(zb, vecs, wcat_ref, bias_ref, nchunks):
    z, dz = _gelu_fwd_bwd(zb)
    u = z[:, :DG]
    v = z[:, DG:]
    ln_g, ln_b = vecs[1:2, :], vecs[2:3, :]
    mu = jnp.mean(v, axis=-1, keepdims=True)
    vc = v - mu
    rstd = lax.rsqrt(jnp.mean(vc * vc, axis=-1, keepdims=True) + EPS)
    vhat = vc * rstd
    vl = vhat * ln_g + ln_b
    sv_cols = []
    for p in range(4):
        blocks = [vl[k * CHUNK:(k + 1) * CHUNK, p * LANE:(p + 1) * LANE] for k in range(nchunks)]
        top, bot = _pair_rhs(blocks)
        rhs = jnp.concatenate([top, bot], axis=0).astype(BF16)
        out = _dot(wcat_ref[p], rhs)
        bias = bias_ref[:, p * LANE:(p + 1) * LANE]
        sv_cols.append(jnp.concatenate([out[:, k * LANE:(k + 1) * LANE] + bias for k in range(nchunks)], axis=0))
    sv = jnp.concatenate(sv_cols, axis=1)
    return dict(u=u, dz=dz, rstd=rstd, vhat=vhat, vl=vl, sv=sv, yb=u * sv)


def _mix_fwd(x, modv, win, wpool, vecs, wcat, bias, wout):
    S = x.shape[0]
    tm = TM_MIX
    nt = S // tm
    nchunks = tm // CHUNK

    def body(x_ref, mod_ref, win_ref, wpool_ref, vec_ref, wcat_ref, bias_ref, wout_ref,
             xo_ref, pooled_ref, zb_ref, ext):
        i = pl.program_id(0)
        x = x_ref[...]
        _, _, _, h = _norm_mod(x, mod_ref[3:4, :], mod_ref[1:2, :], mod_ref[0:1, :])
        proj = _dot(h.astype(BF16), win_ref[...])
        xa = proj[:, :DP]
        zb = proj[:, DP:]
        zb_ref[...] = zb

        @pl.when(i == 0)
        def _():
            ext[0:HALO, :] = jnp.zeros((HALO, DP), F32)

        ext[HALO:HALO + tm, :] = xa
        pos = i * tm + lax.broadcasted_iota(jnp.int32, (tm, 1), 0)
        vecs = vec_ref[...]
        ya_cols = []
        pooled_cols = []
        for gi, w in enumerate(POOL_WINDOWS):
            cols = slice(gi * LANE, (gi + 1) * LANE)
            s = xa[:, cols]
            for k in range(1, w):
                s = s + ext[HALO - k:HALO - k + tm, cols]
            cnt = jnp.minimum(pos + 1, w).astype(F32)
            pooled = (s / cnt - xa[:, cols]).astype(BF16)
            pooled_cols.append(pooled)
            ya_cols.append(_dot(pooled, wpool_ref[gi]) * vecs[0:1, cols])
        pooled_ref[...] = jnp.concatenate(pooled_cols, axis=1)
        ext[0:HALO, :] = ext[tm:tm + HALO, :]

        gm = _gmlp_branch(zb, vecs, wcat_ref, bias_ref, nchunks)
        cat = jnp.concatenate(ya_cols + [gm["yb"]], axis=1).astype(BF16)
        xo_ref[...] = x + mod_ref[2:3, :] * _dot(cat, wout_ref[...])

    full = lambda shape: pl.BlockSpec(shape, lambda i: (0,) * len(shape))
    return _call(
        body, name="mix_fwd",
        grid=(nt,),
        in_specs=[pl.BlockSpec((tm, D), lambda i: (i, 0)), full((8, D)), full((D, DPROJ)),
                  full((4, LANE, LANE)), full((8, DP)), full((4, CHUNK, 2 * CHUNK)), full((CHUNK, DG)),
                  full((DP + DG, D))],
        out_specs=[pl.BlockSpec((tm, D), lambda i: (i, 0)), pl.BlockSpec((tm, DP), lambda i: (i, 0)),
                   pl.BlockSpec((tm, 2 * DG), lambda i: (i, 0))],
        out_shape=[_sds((S, D), F32), _sds((S, DP), BF16), _sds((S, 2 * DG), F32)],
        scratch_shapes=[pltpu.VMEM((tm + HALO, DP), F32)],
        compiler_params=_params(("arbitrary",), VMEM_LIMIT),
    )(x, modv, win, wpool, vecs, wcat, bias, wout)


def _mix_bwd(x, dxo, pooled, zb, modv, win, wpool, vecs, wcat, wtcat, bias, wout, side=None):
    S = x.shape[0]
    tm = TM_MIX
    nt = S // tm
    nchunks = tm // CHUNK

    def body(x_ref, dxo_ref, pooled_ref, zb_ref, mod_ref, win_ref, wpool_ref, vec_ref, wcat_ref, wtcat_ref,
             bias_ref, wout_ref,
             dx_ref, dwin_ref, dwout_ref, dwpool_ref, dwsp_ref, dbsp_ref, v512_ref, vd_ref, qext, dsv_acc):
        step = pl.program_id(0)
        tile = nt - 1 - step

        @pl.when(step == 0)
        def _():
            dwin_ref[...] = jnp.zeros_like(dwin_ref)
            dwout_ref[...] = jnp.zeros_like(dwout_ref)
            dwpool_ref[...] = jnp.zeros_like(dwpool_ref)
            dwsp_ref[...] = jnp.zeros_like(dwsp_ref)
            v512_ref[...] = jnp.zeros_like(v512_ref)
            vd_ref[...] = jnp.zeros_like(vd_ref)
            dsv_acc[...] = jnp.zeros_like(dsv_acc)
            qext[tm:tm + HALO, :] = jnp.zeros((HALO, DP), F32)

        gn, sc, sh, gate = mod_ref[3:4, :], mod_ref[1:2, :], mod_ref[0:1, :], mod_ref[2:3, :]
        vecs = vec_ref[...]
        x = x_ref[...]
        r, xn, hp, h = _norm_mod(x, gn, sc, sh)
        hb = h.astype(BF16)
        dxo = dxo_ref[...]

        pooled = pooled_ref[...]
        mixed_cols = [_dot(pooled[:, gi * LANE:(gi + 1) * LANE], wpool_ref[gi]) for gi in range(4)]
        mixed = jnp.concatenate(mixed_cols, axis=1)
        scale = vecs[0:1, :]
        gm = _gmlp_branch(zb_ref[...], vecs, wcat_ref, bias_ref, nchunks)
        cat = jnp.concatenate([mixed * scale, gm["yb"]], axis=1).astype(BF16)

        dwout_ref[...] += _dot_tn(cat, dxo.astype(BF16))
        dcat = _dot_nt((dxo * gate).astype(BF16), wout_ref[...])
        dya = dcat[:, :DP]
        dyb = dcat[:, DP:]

        v512_ref[0:1, :] += jnp.sum(dya * mixed, axis=0, keepdims=True)
        dmixed = (dya * scale).astype(BF16)
        pos = tile * tm + lax.broadcasted_iota(jnp.int32, (tm, 1), 0)
        dpooled_cols = []
        for gi, w in enumerate(POOL_WINDOWS):
            cols = slice(gi * LANE, (gi + 1) * LANE)
            dp = _dot_nt(dmixed[:, cols], wpool_ref[gi])
            dwpool_ref[gi] += _dot_tn(pooled[:, cols], dmixed[:, cols])
            cnt = jnp.minimum(pos + 1, w).astype(F32)
            qext[0:tm, cols] = dp / cnt
            dpooled_cols.append(dp)
        dxa_cols = []
        for gi, w in enumerate(POOL_WINDOWS):
            cols = slice(gi * LANE, (gi + 1) * LANE)
            s = qext[0:tm, cols]
            for k in range(1, w):
                s = s + qext[k:k + tm, cols]
            dxa_cols.append(s - dpooled_cols[gi])
        qext[tm:tm + HALO, :] = qext[0:HALO, :]

        u, sv, vl = gm["u"], gm["sv"], gm["vl"]
        du = dyb * sv
        dsv = dyb * u
        dvl_cols = []
        for p in range(4):
            cols = slice(p * LANE, (p + 1) * LANE)
            dblocks = [dsv[k * CHUNK:(k + 1) * CHUNK, cols] for k in range(nchunks)]
            vblocks = [vl[k * CHUNK:(k + 1) * CHUNK, cols] for k in range(nchunks)]
            tot = dblocks[0]
            for b in dblocks[1:]:
                tot = tot + b
            dsv_acc[:, cols] += tot
            top, bot = _pair_rhs(dblocks)
            out = _dot(wtcat_ref[p], jnp.concatenate([top, bot], axis=0).astype(BF16))
            dvl_cols.append(jnp.concatenate([out[:, k * LANE:(k + 1) * LANE] for k in range(nchunks)], axis=0))
            vcat = jnp.concatenate(vblocks, axis=1).astype(BF16)
            dwsp_ref[2 * p] += _dot_nt(top.astype(BF16), vcat)
            dwsp_ref[2 * p + 1] += _dot_nt(bot.astype(BF16), vcat)
        dvl = jnp.concatenate(dvl_cols, axis=1)
        vhat, rstd = gm["vhat"], gm["rstd"]
        v512_ref[1:2, :] += jnp.sum(dvl * vhat, axis=0, keepdims=True)
        v512_ref[2:3, :] += jnp.sum(dvl, axis=0, keepdims=True)
        dvh = dvl * vecs[1:2, :]
        dv = rstd * (dvh - jnp.mean(dvh, axis=-1, keepdims=True)
                     - vhat * jnp.mean(dvh * vhat, axis=-1, keepdims=True))
        dzb = jnp.concatenate([du, dv], axis=1) * gm["dz"]

        dproj = jnp.concatenate(dxa_cols + [dzb], axis=1).astype(BF16)
        dwin_ref[...] += _dot_tn(hb, dproj)
        dh = _dot_nt(dproj, win_ref[...])
        dsh, dsc, dgn, dxin = _norm_mod_bwd(dh, r, xn, hp, gn, sc)
        vd_ref[0:1, :] += dsh
        vd_ref[1:2, :] += dsc
        vd_ref[3:4, :] += dgn
        dx_ref[...] = dxo + dxin

        @pl.when(step == nt - 1)
        def _():
            gw = dwout_ref[...]
            vd_ref[2:3, :] += jnp.sum(wout_ref[...].astype(F32) * gw, axis=0, keepdims=True)
            dwout_ref[...] = gw * gate
            row = lax.broadcasted_iota(jnp.int32, (CHUNK, CHUNK), 0)
            col = lax.broadcasted_iota(jnp.int32, (CHUNK, CHUNK), 1)
            for hh in range(8):
                dwsp_ref[hh] = jnp.where(col <= row, dwsp_ref[hh], 0.0)
            head = lax.broadcasted_iota(jnp.int32, (8, DG), 0)
            ch = lax.broadcasted_iota(jnp.int32, (8, DG), 1)
            spread = jnp.where(ch // 64 == head, 1.0, 0.0).astype(F32)
            dbsp_ref[...] = lax.dot_general(spread, dsv_acc[...], (((1,), (1,)), ((), ())),
                                            precision=HIGHEST, preferred_element_type=F32)

    full = lambda shape: pl.BlockSpec(shape, lambda s: (0,) * len(shape))
    rev = lambda cols: pl.BlockSpec((tm, cols), lambda s: (nt - 1 - s, 0))
    step = lambda s: lambda: pl.program_id(0) == s
    return _side_call(
        body, side, (step(0), None, step(nt - 1)), name="mix_bwd",
        grid=(nt,),
        in_specs=[rev(D), rev(D), rev(DP), rev(2 * DG), full((8, D)), full((D, DPROJ)), full((4, LANE, LANE)),
                  full((8, DP)), full((4, CHUNK, 2 * CHUNK)), full((4, CHUNK, 2 * CHUNK)), full((CHUNK, DG)),
                  full((DP + DG, D))],
        out_specs=[rev(D), full((D, DPROJ)), full((DP + DG, D)), full((4, LANE, LANE)), full((8, CHUNK, CHUNK)),
                   full((8, CHUNK)), full((8, DP)), full((8, D))],
        out_shape=[_sds((S, D), F32), _sds((D, DPROJ), F32), _sds((DP + DG, D), F32), _sds((4, LANE, LANE), F32),
                   _sds((8, CHUNK, CHUNK), F32), _sds((8, CHUNK), F32), _sds((8, DP), F32), _sds((8, D), F32)],
        scratch_shapes=[pltpu.VMEM((tm + HALO, DP), F32), pltpu.VMEM((CHUNK, DG), F32)],
        compiler_params=_params(("arbitrary",), VMEM_LIMIT),
        args=(x, dxo, pooled, zb, modv, win, wpool, vecs, wcat, wtcat, bias, wout))


def _chip_sum(g, rbuf, core):
    _, _, hr, cols = g.shape
    tr = _row_block(hr)

    def body(c_ref, g_ref, r_ref, o_ref):
        o_ref[...] = (g_ref[...] + r_ref[...]).astype(BF16)

    return pl.pallas_call(
        body, name="chip_sum", interpret=False,
        grid_spec=pltpu.PrefetchScalarGridSpec(
            num_scalar_prefetch=1, grid=(NQ, hr // tr),
            in_specs=[pl.BlockSpec((None, None, tr, cols), lambda q, i, c: (q, c[0], i, 0)),
                      pl.BlockSpec((None, tr, cols), lambda q, i, c: (q, i, 0))],
            out_specs=pl.BlockSpec((None, tr, cols), lambda q, i, c: (q, i, 0))),
        out_shape=_sds((NQ, hr, cols), BF16),
        compiler_params=_params(("arbitrary", "arbitrary"), None),
    )(core, g, rbuf)


def _chip_sum_pair(own, rbuf):
    _, hr, cols = own.shape
    tr = _row_block(hr)

    def body(a_ref, b_ref, o_ref):
        o_ref[...] = (a_ref[...] + b_ref[...]).astype(BF16)

    spec = pl.BlockSpec((None, tr, cols), lambda q, i: (q, i, 0))
    return _call(
        body, name="chip_sum_pair",
        grid=(NQ, hr // tr),
        in_specs=[spec, spec], out_specs=spec,
        out_shape=_sds((NQ, hr, cols), BF16),
        compiler_params=_params(("arbitrary", "arbitrary"), None),
    )(own, rbuf)


def _sum4(cs, rbuf, chip):
    _, hr, cols = rbuf.shape
    tr = _row_block(hr)

    def body(q_ref, c_ref, r1_ref, r2_ref, r3_ref, o_ref):
        acc = c_ref[...].astype(F32)
        for r in (r1_ref, r2_ref, r3_ref):
            acc = acc + r[...].astype(F32)
        o_ref[...] = acc

    slot = lambda k: pl.BlockSpec((None, tr, cols), lambda i, q: ((q[0] + k) % NQ, i, 0))
    return pl.pallas_call(
        body, name="sum4", interpret=False,
        grid_spec=pltpu.PrefetchScalarGridSpec(
            num_scalar_prefetch=1, grid=(hr // tr,),
            in_specs=[slot(0), slot(1), slot(2), slot(3)],
            out_specs=pl.BlockSpec((tr, cols), lambda i, q: (i, 0))),
        out_shape=_sds((hr, cols), F32),
        compiler_params=_params(("arbitrary",), None),
    )(chip, cs, rbuf, rbuf, rbuf)


def _adamw_halves(w, own, recv, m, v, core):
    rows, cols = w.shape
    hr = rows // 2
    tr = _row_block(hr, mult=8)
    nb = hr // tr

    def body(c_ref, w_ref, own_ref, recv_ref, m_ref, v_ref, g_ref, d_ref, mo_ref, vo_ref):
        g = jnp.where(pl.program_id(0) == c_ref[0], own_ref[...], recv_ref[...])
        d, mn, vn = _adamw(w_ref[...], g, m_ref[...], v_ref[...])
        g_ref[...] = g
        d_ref[...] = d
        mo_ref[...] = mn
        vo_ref[...] = vn

    full = pl.BlockSpec((tr, cols), lambda h, i, c: (h * nb + i, 0))
    half = pl.BlockSpec((tr, cols), lambda h, i, c: (i, 0))
    return pl.pallas_call(
        body, name="adamw_halves", interpret=False,
        grid_spec=pltpu.PrefetchScalarGridSpec(
            num_scalar_prefetch=1, grid=(2, nb),
            in_specs=[full, half, half, full, full],
            out_specs=[full] * 4),
        out_shape=[_sds((rows, cols), F32)] * 4,
        compiler_params=_params(("arbitrary", "arbitrary"), None),
    )(core, w, own, recv, m, v)


def _cast_place(w, chip):
    rows, cols = w.shape
    tr = _row_block(rows)

    def body(q_ref, w_ref, o_ref):
        o_ref[...] = w_ref[...].astype(BF16)

    return pl.pallas_call(
        body, name="cast_place", interpret=False,
        grid_spec=pltpu.PrefetchScalarGridSpec(
            num_scalar_prefetch=1, grid=(rows // tr,),
            in_specs=[pl.BlockSpec((tr, cols), lambda i, q: (i, 0))],
            out_specs=pl.BlockSpec((None, tr, cols), lambda i, q: (q[0], i, 0))),
        out_shape=_sds((NQ, rows, cols), BF16),
        compiler_params=_params(("arbitrary",), None),
    )(chip, w)


def _ada_grad_adamw(cact_t, dmod_q, w, m, v, side=None):
    rows, cols = w.shape
    tc = 256
    assert cols % tc == 0

    def body(c_ref, d_ref, w_ref, m_ref, v_ref, g_ref, dl_ref, mo_ref, vo_ref):
        g = jnp.dot(c_ref[...], d_ref[...], precision=HIGHEST, preferred_element_type=F32)
        d, mn, vn = _adamw(w_ref[...], g, m_ref[...], v_ref[...])
        g_ref[...] = g
        dl_ref[...] = d
        mo_ref[...] = mn
        vo_ref[...] = vn

    spec = pl.BlockSpec((rows, tc), lambda i: (0, i))
    step = lambda s: lambda: pl.program_id(0) == s
    return _side_call(
        body, side, (step(0), None, step(cols // tc - 1)), name="ada_grad_adamw",
        grid=(cols // tc,),
        in_specs=[pl.BlockSpec((rows, 8), lambda i: (0, 0)), pl.BlockSpec((8, tc), lambda i: (0, i)),
                  spec, spec, spec],
        out_specs=[spec] * 4,
        out_shape=[_sds((rows, cols), F32)] * 4,
        scratch_shapes=[],
        compiler_params=_params(("arbitrary",), None),
        args=(cact_t, dmod_q, w, m, v))


def _me():
    x, y, c = lax.axis_index("x"), lax.axis_index("y"), lax.axis_index("c")
    return x, y, c


_OFFSETS7 = [(dx, dy, dc) for dx in (0, 1) for dy in (0, 1) for dc in (0, 1) if (dx, dy, dc) != (0, 0, 0)]
_CHIP_OFFSETS = [(1, 0), (0, 1), (1, 1)]


def _ada_fwd(c, w_ada_q, b_ada_q):
    ncol = w_ada_q.shape[1]

    def body(c_ref, w_ref, b_ref, cact_ref, modsel_ref, blk, gath, res, parts, send_sems, recv_sems):
        x, y, cc = _me()
        me = 4 * x + 2 * y + cc
        q = 2 * x + y
        cv = c_ref[...]
        ca = cv * jax.nn.sigmoid(cv)
        row = lax.broadcasted_iota(jnp.int32, (8, D), 0)
        blk[...] = jnp.where(row == me, jnp.broadcast_to(ca, (8, D)), 0.0)
        gath[me] = blk[...]
        sends = []
        for k, (dx, dy, dc) in enumerate(_OFFSETS7):
            cp = pltpu.make_async_remote_copy(blk, gath.at[me], send_sems.at[k], recv_sems.at[k],
                                              device_id=(x ^ dx, y ^ dy, cc ^ dc), device_id_type=MESH)
            cp.start()
            sends.append(cp)
        for cp in sends:
            cp.wait_recv()
        cact = gath[0]
        for d in range(1, N_DEV):
            cact = cact + gath[d]
        cact_ref[...] = cact
        res[...] = jnp.dot(cact, w_ref[...], precision=HIGHEST, preferred_element_type=F32) + b_ref[...]
        parts[q] = res[...]
        sends2 = []
        for k, (dx, dy) in enumerate(_CHIP_OFFSETS):
            cp = pltpu.make_async_remote_copy(res, parts.at[q], send_sems.at[7 + k], recv_sems.at[7 + k],
                                              device_id=(x ^ dx, y ^ dy, cc), device_id_type=MESH)
            cp.start()
            sends2.append(cp)
        for cp in sends2:
            cp.wait_recv()
        row2 = lax.broadcasted_iota(jnp.int32, (8, ncol), 0)
        out = jnp.zeros((8, ncol), F32)
        for s in range(NQ):
            mine = jnp.sum(jnp.where(row2 == me, parts[s], 0.0), axis=0, keepdims=True)
            out = out + jnp.where(row2 == s, jnp.broadcast_to(mine, (8, ncol)), 0.0)
        modsel_ref[...] = out
        for cp in sends + sends2:
            cp.wait_send()

    return _call(
        body, name="ada_fwd",
        in_specs=[VMEM, VMEM, VMEM], out_specs=[VMEM, VMEM],
        out_shape=[_sds((8, D), F32), _sds((8, ncol), F32)],
        scratch_shapes=[pltpu.VMEM((8, D), F32), pltpu.VMEM((N_DEV, 8, D), F32), pltpu.VMEM((8, ncol), F32),
                        pltpu.VMEM((NQ, 8, ncol), F32), pltpu.SemaphoreType.DMA((10,)), pltpu.SemaphoreType.DMA((10,))],
        compiler_params=_params(None, VMEM_LIMIT),
    )(c, w_ada_q, b_ada_q)


class _Side:
    def __init__(self, ins, out_shapes, aliases, nsem, start, mid=None, finish=None):
        self.ins, self.out_shapes, self.aliases, self.nsem = list(ins), list(out_shapes), dict(aliases), nsem
        self.start, self.mid, self.finish = start, mid, finish


def _join(*sides):
    ins, outs, aliases, offs, nsem = [], [], {}, [], 0
    for s in sides:
        offs.append((len(ins), len(outs), nsem))
        aliases.update({len(ins) + a: len(outs) + b for a, b in s.aliases.items()})
        ins += s.ins
        outs += s.out_shapes
        nsem += s.nsem

    def hook(name):
        def run(i, o, ss, rs, base):
            for s, (io, oo, so) in zip(sides, offs):
                fn = getattr(s, name)
                if fn is not None:
                    fn(i[io:io + len(s.ins)], o[oo:oo + len(s.out_shapes)], ss, rs, base + so)
        return run

    return _Side(ins, outs, aliases, nsem, hook("start"), hook("mid"), hook("finish"))


def _side_call(body, side, when, *, name, in_specs, out_specs, out_shape, scratch_shapes, args, **kw):
    n_in, n_out = len(in_specs), len(out_specs)
    if side is None:
        return _call(body, name=name, in_specs=in_specs, out_specs=out_specs, out_shape=out_shape,
                     scratch_shapes=scratch_shapes, **kw)(*args), []
    ns_in, ns_out = len(side.ins), len(side.out_shapes)

    def hook(fn, k, operands):
        if fn is None:
            return
        if when is None:
            fn(*operands, 0)
        elif when[k] is not None:
            pl.when(when[k]())(functools.partial(fn, *operands, 0))

    def wrapped(*refs):
        ins, s_ins = refs[:n_in], refs[n_in:n_in + ns_in]
        o0 = n_in + ns_in
        outs, s_outs = refs[o0:o0 + n_out], refs[o0 + n_out:o0 + n_out + ns_out]
        rest = refs[o0 + n_out + ns_out:]
        scratch, operands = rest[:-2], (s_ins, s_outs, rest[-2], rest[-1])
        hook(side.start, 0, operands)
        body(*ins, *outs, *scratch)
        hook(side.mid, 1, operands)
        hook(side.finish, 2, operands)

    res = _call(
        wrapped, name=name,
        in_specs=list(in_specs) + [ANY] * ns_in, out_specs=list(out_specs) + [ANY] * ns_out,
        out_shape=list(out_shape) + side.out_shapes,
        scratch_shapes=list(scratch_shapes) + [pltpu.SemaphoreType.DMA((side.nsem,)),
                                               pltpu.SemaphoreType.DMA((side.nsem,))],
        input_output_aliases={n_in + a: n_out + b for a, b in side.aliases.items()},
        **kw)(*args, *side.ins)
    return res[:n_out], res[n_out:]


def _run_side(side, name):
    return _side_call(lambda: None, side, None, name=name, in_specs=[], out_specs=[], out_shape=[],
                      scratch_shapes=[], args=[])[1]


def _remote(src, dst, ss, rs, k, dev):
    return pltpu.make_async_remote_copy(src, dst, ss.at[k], rs.at[k], device_id=dev, device_id_type=MESH)


def _gather_side(bufs):
    n = len(bufs)

    def walk(outs, half):
        x, y, cc = _me()
        for w in range(n):
            hr = outs[w].shape[1] // 2
            rows = pl.ds((cc if half == "mine" else 1 - cc) * hr, hr)
            for j, (dx, dy) in enumerate(_CHIP_OFFSETS):
                yield w, j, (x ^ dx, y ^ dy, cc), outs[w].at[2 * (x ^ dx) + (y ^ dy), rows], outs[w].at[2 * x + y, rows]

    def start(ins, outs, ss, rs, b):
        for w, j, peer, _, own in walk(outs, "mine"):
            _remote(own, own, ss, rs, b + 6 * w + j, peer).start()

    def mid(ins, outs, ss, rs, b):
        x, y, cc = _me()
        for w, j, peer, land, _ in walk(outs, "mine"):
            _remote(land, land, ss, rs, b + 6 * w + j, peer).wait_recv()
            _remote(land, land, ss, rs, b + 6 * w + 3 + j, (x, y, 1 - cc)).start()

    def finish(ins, outs, ss, rs, b):
        x, y, cc = _me()
        for w, j, _, land, _ in walk(outs, "other"):
            _remote(land, land, ss, rs, b + 6 * w + 3 + j, (x, y, 1 - cc)).wait_recv()
        for w, j, peer, land, own in walk(outs, "mine"):
            _remote(own, own, ss, rs, b + 6 * w + j, peer).wait_send()
            _remote(land, land, ss, rs, b + 6 * w + 3 + j, (x, y, 1 - cc)).wait_send()

    return _Side(bufs, [_sds(tuple(w.shape), w.dtype) for w in bufs], {i: i for i in range(n)}, 6 * n,
                 start, mid, finish)


def _copies_side(ins, out_shapes, nsem, copies):
    def start(*a):
        for cp in copies(*a):
            cp.start()

    def finish(*a):
        for cp in copies(*a):
            cp.wait()

    return _Side(ins, out_shapes, {}, nsem, start, None, finish)


def _swap_side(gs):
    def copies(ins, outs, ss, rs, b):
        x, y, cc = _me()
        return [_remote(ins[w].at[:, 1 - cc], outs[w], ss, rs, b + w, (x, y, 1 - cc)) for w in range(len(gs))]

    return _copies_side(gs, [_sds((NQ,) + tuple(g.shape[2:]), F32) for g in gs], len(gs), copies)


def _exchange_side(cs):
    def copies(ins, outs, ss, rs, b):
        x, y, cc = _me()
        return [_remote(ins[w].at[2 * (x ^ dx) + (y ^ dy)], outs[w].at[2 * x + y], ss, rs, b + 3 * w + j,
                        (x ^ dx, y ^ dy, cc))
                for w in range(len(cs)) for j, (dx, dy) in enumerate(_CHIP_OFFSETS)]

    return _copies_side(cs, [_sds(tuple(c.shape), c.dtype) for c in cs], 3 * len(cs), copies)


def _share_side(fs):
    def copies(ins, outs, ss, rs, b):
        x, y, cc = _me()
        return [_remote(ins[w], outs[w], ss, rs, b + w, (x, y, 1 - cc)) for w in range(len(fs))]

    return _copies_side(fs, [_sds(tuple(f.shape), F32) for f in fs], len(fs), copies)


def _small_allreduce_adamw(g, w, m, v, nd):
    rows = g.shape[0]
    nr = rows - nd
    hr = nr // 2
    assert nd % 8 == 0 and hr % 8 == 0

    def body(g_ref, w_ref, m_ref, v_ref, gs_ref, d_ref, mo_ref, vo_ref, gath, sib, csum, slots, tot, ss, rs):
        x, y, cc = _me()
        me = 4 * x + 2 * y + cc
        q = 2 * x + y
        sibling = (x, y, 1 - cc)
        dm = g_ref.at[pl.ds(0, nd)]
        gath[me] = g_ref[0:nd, :]
        to_all = [_remote(dm, gath.at[me], ss, rs, k, (x ^ dx, y ^ dy, cc ^ dc)) for k, (dx, dy, dc) in enumerate(_OFFSETS7)]
        to_sib = _remote(g_ref.at[pl.ds(nd, nr)], sib, ss, rs, 7, sibling)
        for cp in to_all + [to_sib]:
            cp.start()
        to_sib.wait_recv()
        csum[...] = g_ref[nd:, :] + sib[...]
        mine = pl.ds(pl.multiple_of(cc * hr, 8), hr)
        slots[q] = csum[mine, :]
        to_chips = [_remote(csum.at[mine], slots.at[q], ss, rs, 8 + j, (x ^ dx, y ^ dy, cc))
                    for j, (dx, dy) in enumerate(_CHIP_OFFSETS)]
        for cp in to_chips:
            cp.start()
        for cp in to_chips:
            cp.wait_recv()
        tot[mine, :] = (slots[0] + slots[1]) + (slots[2] + slots[3])
        halves = _remote(tot.at[mine], tot.at[mine], ss, rs, 11, sibling)
        halves.start()
        for cp in to_all:
            cp.wait_recv()
        dsum = gath[0]
        for dev in range(1, N_DEV):
            dsum = dsum + gath[dev]
        halves.wait_recv()
        for lo, n, total in ((0, nd, dsum), (nd, nr, tot[...])):
            gs_ref[lo:lo + n, :] = total
            d, mn, vn = _adamw(w_ref[lo:lo + n, :], total, m_ref[lo:lo + n, :], v_ref[lo:lo + n, :])
            d_ref[lo:lo + n, :] = d
            mo_ref[lo:lo + n, :] = mn
            vo_ref[lo:lo + n, :] = vn
        for cp in to_all + [to_sib, halves] + to_chips:
            cp.wait_send()

    return _call(
        body, name="small_allreduce_adamw",
        in_specs=[VMEM] * 4, out_specs=[VMEM] * 5,
        out_shape=[_sds((rows, LANE), F32)] * 4 + [_sds((N_DEV, nd, LANE), F32)],
        scratch_shapes=[pltpu.VMEM((nr, LANE), F32), pltpu.VMEM((nr, LANE), F32), pltpu.VMEM((NQ, hr, LANE), F32),
                        pltpu.VMEM((nr, LANE), F32), pltpu.SemaphoreType.DMA((12,)), pltpu.SemaphoreType.DMA((12,))],
        compiler_params=_params(None, VMEM_LIMIT),
    )(g, w, m, v)


_SMALL = ["b_ada", "norm_ffn1_g", "norm_mix_g", "pool_scale", "gmlp_ln_g", "gmlp_ln_b", "b_spatial",
          "norm_ffn2_g", "norm_final_g", "w_pool", "w_spatial"]


def _pack(parts):
    blocks, layout, r0 = [], {}, 0
    for name in _SMALL:
        a = parts[name]
        n = a.size
        rows = -(-n // LANE)
        rows8 = -(-rows // 8) * 8
        flat = a.reshape(-1).astype(F32)
        if rows8 * LANE != n:
            flat = jnp.concatenate([flat, jnp.zeros((rows8 * LANE - n,), F32)])
        blocks.append(flat.reshape(rows8, LANE))
        layout[name] = (r0, n, a.shape)
        r0 += rows8
    return jnp.concatenate(blocks, axis=0), layout


def _unpack(packed, layout):
    out = {}
    for name, (r0, n, shape) in layout.items():
        rows = -(-n // LANE)
        out[name] = packed[r0:r0 + rows].reshape(-1)[:n].reshape(shape)
    return out


def _modv(mod9, sub, gain):
    rows = jnp.concatenate([mod9[3 * sub:3 * sub + 3], gain.reshape(1, D), jnp.zeros((4, D), F32)], axis=0)
    return rows


_BIG = ["ffn1_w_in", "ffn1_w_out", "w_mix_in", "w_mix_out", "ffn2_w_in", "ffn2_w_out"]


def kernel(x, c, w_ada, b_ada, norm_ffn1_g, ffn1_w_in, ffn1_w_out, norm_mix_g, w_mix_in, w_pool, pool_scale, gmlp_ln_g, gmlp_ln_b, w_spatial, b_spatial, w_mix_out, norm_ffn2_g, ffn2_w_in, ffn2_w_out, norm_final_g, loss_target, m_w_ada, m_b_ada, m_norm_ffn1_g, m_ffn1_w_in, m_ffn1_w_out, m_norm_mix_g, m_w_mix_in, m_w_pool, m_pool_scale, m_gmlp_ln_g, m_gmlp_ln_b, m_w_spatial, m_b_spatial, m_w_mix_out, m_norm_ffn2_g, m_ffn2_w_in, m_ffn2_w_out, m_norm_final_g, v_w_ada, v_b_ada, v_norm_ffn1_g, v_ffn1_w_in, v_ffn1_w_out, v_norm_mix_g, v_w_mix_in, v_w_pool, v_pool_scale, v_gmlp_ln_g, v_gmlp_ln_b, v_w_spatial, v_b_spatial, v_w_mix_out, v_norm_ffn2_g, v_ffn2_w_in, v_ffn2_w_out, v_norm_final_g):
    names = ["w_ada", "b_ada", "norm_ffn1_g", "ffn1_w_in", "ffn1_w_out", "norm_mix_g", "w_mix_in", "w_pool",
             "pool_scale", "gmlp_ln_g", "gmlp_ln_b", "w_spatial", "b_spatial", "w_mix_out", "norm_ffn2_g",
             "ffn2_w_in", "ffn2_w_out", "norm_final_g"]
    W = dict(zip(names, [w_ada, b_ada, norm_ffn1_g, ffn1_w_in, ffn1_w_out, norm_mix_g, w_mix_in, w_pool, pool_scale,
                         gmlp_ln_g, gmlp_ln_b, w_spatial, b_spatial, w_mix_out, norm_ffn2_g, ffn2_w_in, ffn2_w_out,
                         norm_final_g]))
    M = dict(zip(names, [m_w_ada, m_b_ada, m_norm_ffn1_g, m_ffn1_w_in, m_ffn1_w_out, m_norm_mix_g, m_w_mix_in, m_w_pool,
                         m_pool_scale, m_gmlp_ln_g, m_gmlp_ln_b, m_w_spatial, m_b_spatial, m_w_mix_out, m_norm_ffn2_g,
                         m_ffn2_w_in, m_ffn2_w_out, m_norm_final_g]))
    V = dict(zip(names, [v_w_ada, v_b_ada, v_norm_ffn1_g, v_ffn1_w_in, v_ffn1_w_out, v_norm_mix_g, v_w_mix_in, v_w_pool,
                         v_pool_scale, v_gmlp_ln_g, v_gmlp_ln_b, v_w_spatial, v_b_spatial, v_w_mix_out, v_norm_ffn2_g,
                         v_ffn2_w_in, v_ffn2_w_out, v_norm_final_g]))

    xi, yi, ci = _me()
    q = 2 * xi + yi
    core = ci.astype(jnp.int32).reshape(1)

    ncol = w_ada.shape[2]
    b_q = lax.dynamic_slice_in_dim(b_ada, q * ncol, ncol, axis=1)
    cact_all, modsel = _ada_fwd(c, w_ada[0], b_q)
    mod9 = modsel[:NQ].reshape(9, D)

    chip = q.astype(jnp.int32).reshape(1)
    place = lambda n: _cast_place(W[n][0], chip)
    xs, target = x[0], loss_target[0]
    mv1 = _modv(mod9, 0, norm_ffn1_g[0])
    mv2 = _modv(mod9, 1, norm_mix_g[0])
    mv3 = _modv(mod9, 2, norm_ffn2_g[0])
    wcat, wtcat, bias = _prep_spatial(w_spatial[0], b_spatial[0].T)
    wpool = w_pool[0].astype(BF16)
    vecs = jnp.concatenate([pool_scale, gmlp_ln_g, gmlp_ln_b, jnp.zeros((5, DP), F32)], axis=0)
    gf = jnp.concatenate([norm_final_g.reshape(1, D), jnp.zeros((7, D), F32)], axis=0)

    win1, wout1 = _run_side(_gather_side([place("ffn1_w_in"), place("ffn1_w_out")]), "gather_ffn1")
    later = ["w_mix_in", "w_mix_out", "ffn2_w_in", "ffn2_w_out"]
    (x1, g1s, u1s), got = _ffn_fwd(xs, mv1, win1, wout1.reshape(2, CH, D), side=_gather_side([place(n) for n in later]))
    wmi, wmo, win2, wout2 = got
    wmi = jnp.transpose(wmi, (1, 0, 2)).reshape(D, DPROJ)
    wmo = wmo.reshape(DP + DG, D)
    x2, pooled, zb = _mix_fwd(x1, mv2, wmi, wpool, vecs, wcat, bias, wmo)
    (dx3, g3s, u3s, loss_blk, dgf), _ = _ffn_fwd(x2, mv3, win2, wout2.reshape(2, CH, D), head=(target, gf))
    loss = lax.psum(loss_blk[0, 0], ("x", "y", "c"))

    wo1, wo2 = wout1.reshape(2, CH, D), wout2.reshape(2, CH, D)
    (dx2, oin2, oout2, rin2, rout2, vec3), _ = _ffn_bwd(x2, dx3, g3s, u3s, mv3, win2, wo2)
    cs2 = [_chip_sum_pair(oin2, rin2), _chip_sum_pair(oout2, rout2)]
    (dx1, dwmi, dwmo, dwpool, dwsp, dbsp, v512, vec2), ex2 = _mix_bwd(
        x1, dx2, pooled, zb, mv2, wmi, wpool, vecs, wcat, wtcat, bias, wmo, side=_exchange_side(cs2))
    half2 = [_sum4(cs, e, chip) for cs, e in zip(cs2, ex2)]
    qcols = w_mix_in.shape[2]
    vmix = [jnp.transpose(dwmi.reshape(D, NQ, qcols), (1, 0, 2)).reshape(NQ, 2, D // 2, qcols),
            dwmo.reshape(NQ, 2, (DP + DG) // 8, D)]
    (grad_x, oin1, oout1, rin1, rout1, vec1), got = _ffn_bwd(
        xs, dx1, g1s, u1s, mv1, win1, wo1, side=_join(_swap_side(vmix), _share_side(half2)))
    sibmix, other2 = got[:2], got[2:]
    cs1 = [_chip_sum(g, r, core) for g, r in zip(vmix, sibmix)] + [_chip_sum_pair(oin1, rin1),
                                                                  _chip_sum_pair(oout1, rout1)]

    dmod = jnp.concatenate([vec1[0:3], vec2[0:3], vec3[0:3]], axis=0)
    grads = dict(
        b_ada=dmod.reshape(1, 9 * D), norm_ffn1_g=vec1[3:4], norm_mix_g=vec2[3:4], norm_ffn2_g=vec3[3:4],
        pool_scale=v512[0:1], gmlp_ln_g=v512[1:2], gmlp_ln_b=v512[2:3], b_spatial=dbsp[None],
        norm_final_g=dgf[0], w_pool=dwpool[None], w_spatial=dwsp[None])

    gp, layout = _pack({n: grads[n] for n in _SMALL})
    wp, _ = _pack({n: W[n] for n in _SMALL})
    mp, _ = _pack({n: M[n] for n in _SMALL})
    vp, _ = _pack({n: V[n] for n in _SMALL})
    r0, nb, _ = layout["b_ada"]
    assert r0 == 0
    out_g, out_d, out_m, out_v = {}, {}, {}, {}
    gs, dl, mo, vo, gath = _small_allreduce_adamw(gp, wp, mp, vp, nb // LANE)
    for packed, dst in ((gs, out_g), (dl, out_d), (mo, out_m), (vo, out_v)):
        for n, a in _unpack(packed, layout).items():
            dst[n] = a.reshape(W[n].shape)

    dmod_q = lax.dynamic_slice_in_dim(gath.reshape(N_DEV, nb), q * ncol, ncol, axis=1)
    (ga, da, ma, va), ex1 = _ada_grad_adamw(cact_all.T, dmod_q, w_ada[0], m_w_ada[0], v_w_ada[0],
                                           side=_exchange_side(cs1))
    out_g["w_ada"], out_d["w_ada"], out_m["w_ada"], out_v["w_ada"] = ga[None], da[None], ma[None], va[None]
    half1 = [_sum4(cs, e, chip) for cs, e in zip(cs1, ex1)]
    other1 = _run_side(_share_side(half1), "sibling_share")
    reduced = dict(zip(["w_mix_in", "w_mix_out", "ffn1_w_in", "ffn1_w_out", "ffn2_w_in", "ffn2_w_out"],
                       zip(half1 + half2, list(other1) + list(other2))))
    for n in _BIG:
        own, recv = reduced[n]
        g2, d, mn, vn = _adamw_halves(W[n][0], own, recv, M[n][0], V[n][0], core)
        out_g[n], out_d[n], out_m[n], out_v[n] = g2[None], d[None], mn[None], vn[None]

    return (loss, grad_x[None], *[out_g[n] for n in names], *[out_d[n] for n in names],
            *[out_m[n] for n in names], *[out_v[n] for n in names])
```

```python
import functools
import math

import jax
import jax.numpy as jnp
from jax import lax
from jax.experimental import pallas as pl
from jax.experimental.pallas import tpu as pltpu

F32 = jnp.float32
BF16 = jnp.bfloat16
MESH = pl.DeviceIdType.MESH
HIGHEST = lax.Precision.HIGHEST

EPS = 1e-6
D = 1024
DFF = 2816
CH = DFF // 2
NQ = 4
DP = 512
DG = 512
DPROJ = DP + 2 * DG
POOL_WINDOWS = (2, 4, 8, 16)
HALO = 16
CHUNK = 128
LANE = 128
N_DEV = 8

ADAM_LR = 0.001
ADAM_B1 = 0.9
ADAM_B2 = 0.999
ADAM_EPS = 1e-08
ADAM_WD = 0.01
ADAM_STEP = 10

VMEM_LIMIT = 62 * 1024 * 1024

TM_FFN_FWD = 512
TM_FFN_BWD = 512
TM_MIX = 256


def _call(body, **kw):
    return pl.pallas_call(body, interpret=False, **kw)


def _params(sem=None, vmem=None):
    return pltpu.CompilerParams(dimension_semantics=sem, vmem_limit_bytes=vmem)


def _sds(shape, dtype):
    return jax.ShapeDtypeStruct(shape, dtype)


ANY = pl.BlockSpec(memory_space=pl.ANY)
VMEM = pl.BlockSpec(memory_space=pltpu.VMEM)
SMEM = pl.BlockSpec(memory_space=pltpu.SMEM)


def _norm_mod(x, gn, sc, sh):
    r = lax.rsqrt(jnp.mean(x * x, axis=-1, keepdims=True) + EPS)
    xn = x * r
    hp = xn * gn
    return r, xn, hp, hp * (1.0 + sc) + sh


def _norm_mod_bwd(dh, r, xn, hp, gn, sc):
    one_sc = 1.0 + sc
    dsh = jnp.sum(dh, axis=0, keepdims=True)
    dsc = jnp.sum(dh * hp, axis=0, keepdims=True)
    dgn = jnp.sum(dh * one_sc * xn, axis=0, keepdims=True)
    dxn = dh * (gn * one_sc)
    dx = r * (dxn - xn * jnp.mean(dxn * xn, axis=-1, keepdims=True))
    return dsh, dsc, dgn, dx


def _dot(a, b):
    return jnp.dot(a, b, preferred_element_type=F32)


def _dot_nt(a, b):
    return lax.dot_general(a, b, (((1,), (1,)), ((), ())), preferred_element_type=F32)


def _dot_tn(a, b):
    return lax.dot_general(a, b, (((0,), (0,)), ((), ())), preferred_element_type=F32)


_GELU_C = math.sqrt(2.0 / math.pi)
_GELU_A = 0.044715


def _gelu_fwd_bwd(x):
    x2 = x * x
    t = jnp.tanh(_GELU_C * (x + _GELU_A * x * x2))
    g = 0.5 * x * (1.0 + t)
    dg = 0.5 * (1.0 + t) + 0.5 * x * (1.0 - t * t) * (_GELU_C * (1.0 + 3.0 * _GELU_A * x2))
    return g, dg


def _adamw(w, g, m, v):
    m = ADAM_B1 * m + (1.0 - ADAM_B1) * g
    v = ADAM_B2 * v + (1.0 - ADAM_B2) * (g * g)
    m_hat = m / (1.0 - ADAM_B1 ** ADAM_STEP)
    v_hat = v / (1.0 - ADAM_B2 ** ADAM_STEP)
    delta = -ADAM_LR * (m_hat / (jnp.sqrt(v_hat) + ADAM_EPS) + ADAM_WD * w)
    return delta, m, v


def _row_block(rows, cap=256, mult=16):
    best = None
    for t in range(mult, min(rows, cap) + 1, mult):
        if rows % t == 0:
            best = t
    assert best is not None, rows
    return best


def _head_math(x, target, gf):
    r = lax.rsqrt(jnp.mean(x * x, axis=-1, keepdims=True) + EPS)
    xn = x * r
    err = xn * gf - target
    dy = err * (1.0 / D)
    dxn = dy * gf
    dx = r * (dxn - xn * jnp.mean(dxn * xn, axis=-1, keepdims=True))
    return (0.5 / D) * jnp.sum(err * err), jnp.sum(dy * xn, axis=0, keepdims=True), dx


def _ffn_fwd(x, modv, win, wout, side=None, head=None):
    S = x.shape[0]
    tm = TM_FFN_FWD
    nt = S // tm

    def body(*refs):
        if head is None:
            x_ref, mod_ref, wg_ref, wu_ref, wo_ref, xo_ref, gs_ref, us_ref, acc_scr = refs
        else:
            (x_ref, mod_ref, wg_ref, wu_ref, wo_ref, t_ref, gf_ref,
             xo_ref, gs_ref, us_ref, loss_ref, dgf_ref, acc_scr) = refs

        @pl.when((pl.program_id(0) == 0) & (pl.program_id(1) == 0))
        def _():
            acc_scr[...] = jnp.zeros_like(acc_scr)
            if head is not None:
                loss_ref[...] = jnp.zeros_like(loss_ref)
                dgf_ref[...] = jnp.zeros_like(dgf_ref)

        j = pl.program_id(1)
        h = _norm_mod(x_ref[...], mod_ref[3:4, :], mod_ref[1:2, :], mod_ref[0:1, :])[3].astype(BF16)
        g = _dot(h, wg_ref[...]).astype(BF16)
        u = _dot(h, wu_ref[...]).astype(BF16)
        gs_ref[...] = g
        us_ref[...] = u
        gf = g.astype(F32)
        a = (gf * jax.nn.sigmoid(gf) * u.astype(F32)).astype(BF16)
        acc = jnp.where(j == 0, 0.0, acc_scr[...]) + _dot(a, wo_ref[...])
        acc_scr[...] = acc
        xo = x_ref[...] + (0.5 * mod_ref[2:3, :]) * acc
        if head is None:
            xo_ref[...] = xo
        else:
            @pl.when(j == 1)
            def _():
                loss, dgf, dx = _head_math(xo, t_ref[...], gf_ref[0:1, :])
                loss_ref[...] += loss
                dgf_ref[0:1, :] += dgf
                xo_ref[...] = dx

    step = lambda i, j: lambda: (pl.program_id(0) == i) & (pl.program_id(1) == j)
    tile = pl.BlockSpec((tm, D), lambda i, j: (i, 0))
    const = lambda shape: pl.BlockSpec(shape, lambda i, j: (0, 0))
    chunk = pl.BlockSpec((tm, CH), lambda i, j: (i, j))
    in_specs = [tile, const((8, D)), pl.BlockSpec((None, D, CH), lambda i, j: (j, 0, 0)),
                pl.BlockSpec((None, D, CH), lambda i, j: (2 + j, 0, 0)), pl.BlockSpec((None, CH, D), lambda i, j: (j, 0, 0))]
    out_specs = [tile, chunk, chunk]
    out_shape = [_sds((S, D), F32), _sds((S, DFF), BF16), _sds((S, DFF), BF16)]
    args = (x, modv, win, win, wout)
    if head is not None:
        in_specs += [tile, const((8, D))]
        out_specs += [const((8, LANE)), const((8, D))]
        out_shape += [_sds((8, LANE), F32), _sds((8, D), F32)]
        args += tuple(head)
    return _side_call(
        body, side, (step(0, 0), step((7 * nt) // 10, 0), step(nt - 1, 1)), name="ffn_fwd",
        grid=(nt, 2), in_specs=in_specs, out_specs=out_specs, out_shape=out_shape,
        scratch_shapes=[pltpu.VMEM((tm, D), F32)],
        compiler_params=_params(("arbitrary", "arbitrary"), VMEM_LIMIT),
        args=args)


def _ffn_bwd(x, dxo, gs, us, modv, win, wout, side=None):
    S = x.shape[0]
    tm = TM_FFN_BWD
    nsub = tm // 256
    nt = S // tm
    assert nt >= 2
    hi, ho = D // 2, CH // 4
    once = pl.Buffered(1)

    def body(x_ref, dxo_ref, gs_ref, us_ref, mod_ref, wg_ref, wu_ref, wo_ref,
             dx_ref, dwin_ref, dwout_ref, rwin_ref, rwout_ref, vec_ref, dhbuf_ref,
             accg, accu, accw, dh_st, dh_ld, sems, fsend, frecv):
        j = pl.program_id(0)
        i = pl.program_id(1)
        store = lambda t: pltpu.make_async_copy(dh_st, dhbuf_ref.at[t], sems.at[4])
        load = lambda t: pltpu.make_async_copy(dhbuf_ref.at[t], dh_ld, sems.at[5])

        @pl.when(i == 0)
        def _():
            accg[...] = jnp.zeros_like(accg)
            accu[...] = jnp.zeros_like(accu)
            accw[...] = jnp.zeros_like(accw)

        @pl.when((i == 0) & (j == 0))
        def _():
            vec_ref[...] = jnp.zeros_like(vec_ref)

        @pl.when((j == 1) & (i == 0))
        def _():
            store(nt - 1).wait()

        @pl.when(j == 1)
        def _():
            load(i).start()

        gn, sc, sh, gate = mod_ref[3:4, :], mod_ref[1:2, :], mod_ref[0:1, :], mod_ref[2:3, :]

        parts = []
        for s in range(nsub):
            rs = slice(s * (tm // nsub), (s + 1) * (tm // nsub))
            hb = _norm_mod(x_ref[rs, :], gn, sc, sh)[3].astype(BF16)
            dxo = dxo_ref[rs, :]
            dy = (dxo * (0.5 * gate)).astype(BF16)
            g = gs_ref[rs, :].astype(F32)
            u = us_ref[rs, :].astype(F32)
            sig = jax.nn.sigmoid(g)
            sl = g * sig
            a = (sl * u).astype(BF16)
            da = _dot_nt(dy, wo_ref[...])
            dg = (da * u * (sig * (1.0 + g * (1.0 - sig)))).astype(BF16)
            du = (da * sl).astype(BF16)
            dhp = _dot_nt(dg, wg_ref[...]) + _dot_nt(du, wu_ref[...])
            parts.append((hb, a, dg, du, dxo.astype(BF16), dhp))

        hb, a, dg, du, dxb = [jnp.concatenate(p, axis=0) if nsub > 1 else p[0] for p in list(zip(*parts))[:5]]
        accw[...] += _dot_tn(a, dxb)
        accg[...] += _dot_tn(hb, dg)
        accu[...] += _dot_tn(hb, du)

        @pl.when(j == 0)
        def _():
            @pl.when(i > 0)
            def _():
                store(i - 1).wait()

            for s in range(nsub):
                dh_st[s * (tm // nsub):(s + 1) * (tm // nsub), :] = parts[s][5]
            store(i).start()

        @pl.when(j == 1)
        def _():
            load(i).wait()
            for s in range(nsub):
                rs = slice(s * (tm // nsub), (s + 1) * (tm // nsub))
                dh = dh_ld[rs, :] + parts[s][5]
                r, xn, hp, _ = _norm_mod(x_ref[rs, :], gn, sc, sh)
                dsh, dsc, dgn, dxin = _norm_mod_bwd(dh, r, xn, hp, gn, sc)
                vec_ref[0:1, :] += dsh
                vec_ref[1:2, :] += dsc
                vec_ref[3:4, :] += dgn
                dx_ref[rs, :] = dxo_ref[rs, :] + dxin

        def flush(jj):
            mx, my, cc = _me()
            rows = lambda base, n, c: pl.ds(base + c * n, n)
            pieces = [(accg, 0, hi, dwin_ref, rwin_ref, jj), (accu, 0, hi, dwin_ref, rwin_ref, 2 + jj),
                      (accw, 0, ho, dwout_ref, rwout_ref, 2 * jj), (accw, 2 * ho, ho, dwout_ref, rwout_ref, 2 * jj + 1)]
            loc = [pltpu.make_async_copy(acc.at[rows(base, n, cc)], own.at[slot], sems.at[k])
                   for k, (acc, base, n, own, _, slot) in enumerate(pieces)]
            rem = [pltpu.make_async_remote_copy(acc.at[rows(base, n, 1 - cc)], sib.at[slot], fsend.at[4 * jj + k],
                                                frecv.at[4 * jj + k], device_id=(mx, my, 1 - cc), device_id_type=MESH)
                   for k, (acc, base, n, _, sib, slot) in enumerate(pieces)]
            return loc, rem

        for jj in range(2):
            @pl.when((i == nt - 1) & (j == jj))
            def _(jj=jj):
                gw = accw[...]
                vec_ref[2:3, :] += 0.5 * jnp.sum(wo_ref[...].astype(F32) * gw, axis=0, keepdims=True)
                accw[...] = gw * (0.5 * gate)
                loc, rem = flush(jj)
                for cp in loc + rem:
                    cp.start()
                for cp in loc:
                    cp.wait()
                for cp in rem:
                    cp.wait_send()

        @pl.when((i == nt - 1) & (j == 1))
        def _():
            for jj in range(2):
                for cp in flush(jj)[1]:
                    cp.wait_recv()

    step = lambda jj, ii: lambda: (pl.program_id(0) == jj) & (pl.program_id(1) == ii)
    (dx, dwin, dwout, rwin, rwout, vec, _), extra = _side_call(
        body, side, (step(0, 0), None, step(1, nt - 1)), name="ffn_bwd",
        grid=(2, nt),
        in_specs=[
            pl.BlockSpec((tm, D), lambda j, i: (i, 0)),
            pl.BlockSpec((tm, D), lambda j, i: (i, 0)),
            pl.BlockSpec((tm, CH), lambda j, i: (i, j)),
            pl.BlockSpec((tm, CH), lambda j, i: (i, j)),
            pl.BlockSpec((8, D), lambda j, i: (0, 0)),
            pl.BlockSpec((None, D, CH), lambda j, i: (j, 0, 0), pipeline_mode=once),
            pl.BlockSpec((None, D, CH), lambda j, i: (2 + j, 0, 0), pipeline_mode=once),
            pl.BlockSpec((None, CH, D), lambda j, i: (j, 0, 0), pipeline_mode=once),
        ],
        out_specs=[
            pl.BlockSpec((tm, D), lambda j, i: (i * j, 0)),
            ANY, ANY, ANY, ANY,
            pl.BlockSpec((8, D), lambda j, i: (0, 0)),
            ANY,
        ],
        out_shape=[_sds((S, D), F32), _sds((NQ, hi, CH), F32), _sds((NQ, ho, D), F32), _sds((NQ, hi, CH), F32),
                   _sds((NQ, ho, D), F32), _sds((8, D), F32), _sds((nt, tm, D), F32)],
        scratch_shapes=[pltpu.VMEM((D, CH), F32), pltpu.VMEM((D, CH), F32), pltpu.VMEM((CH, D), F32),
                        pltpu.VMEM((tm, D), F32), pltpu.VMEM((tm, D), F32), pltpu.SemaphoreType.DMA((6,)),
                        pltpu.SemaphoreType.DMA((8,)), pltpu.SemaphoreType.DMA((8,))],
        compiler_params=_params(("arbitrary", "arbitrary"), VMEM_LIMIT),
        args=(x, dxo, gs, us, modv, win, win, wout))
    return (dx, dwin, dwout, rwin, rwout, vec), extra


def _prep_spatial(w_spatial, b_spatial_t):
    def body(w_ref, b_ref, wcat_ref, wtcat_ref, bias_ref):
        row = lax.broadcasted_iota(jnp.int32, (CHUNK, CHUNK), 0)
        col = lax.broadcasted_iota(jnp.int32, (CHUNK, CHUNK), 1)
        tril = col <= row
        for p in range(4):
            wa = jnp.where(tril, w_ref[2 * p], 0.0)
            wb = jnp.where(tril, w_ref[2 * p + 1], 0.0)
            wcat_ref[p] = jnp.concatenate([wa, wb], axis=1).astype(BF16)
            wtcat_ref[p] = jnp.concatenate([wa.T, wb.T], axis=1).astype(BF16)
        head = lax.broadcasted_iota(jnp.int32, (8, DG), 0)
        ch = lax.broadcasted_iota(jnp.int32, (8, DG), 1)
        spread = jnp.where(ch // 64 == head, 1.0, 0.0).astype(F32)
        bias_ref[...] = jnp.dot(b_ref[...], spread, precision=HIGHEST, preferred_element_type=F32)

    return _call(
        body, name="prep_spatial",
        in_specs=[VMEM, VMEM], out_specs=[VMEM, VMEM, VMEM],
        out_shape=[_sds((4, CHUNK, 2 * CHUNK), BF16), _sds((4, CHUNK, 2 * CHUNK), BF16), _sds((CHUNK, DG), F32)],
    )(w_spatial, b_spatial_t)


def _pair_rhs(blocks):
    lane = lax.broadcasted_iota(jnp.int32, (CHUNK, LANE), 1)
    lo = lane < 64
    top = jnp.concatenate([jnp.where(lo, b, 0.0) for b in blocks], axis=1)
    bot = jnp.concatenate([jnp.where(lo, 0.0, b) for b in blocks], axis=1)
    return top, bot


def _gmlp_branch(zb, vecs, wcat_ref, bias_ref, nchunks):
    z, dz = _gelu_fwd_bwd(zb)
    u = z[:, :DG]
    v = z[:, DG:]
    ln_g, ln_b = vecs[1:2, :], vecs[2:3, :]
    mu = jnp.mean(v, axis=-1, keepdims=True)
    vc = v - mu
    rstd = lax.rsqrt(jnp.mean(vc * vc, axis=-1, keepdims=True) + EPS)
    vhat = vc * rstd
    vl = vhat * ln_g + ln_b
    sv_cols = []
    for p in range(4):
        blocks = [vl[k * CHUNK:(k + 1) * CHUNK, p * LANE:(p + 1) * LANE] for k in range(nchunks)]
        top, bot = _pair_rhs(blocks)
        rhs = jnp.concatenate([top, bot], axis=0).astype(BF16)
        out = _dot(wcat_ref[p], rhs)
        bias = bias_ref[:, p * LANE:(p + 1) * LANE]
        sv_cols.append(jnp.concatenate([out[:, k * LANE:(k + 1) * LANE] + bias for k in range(nchunks)], axis=0))
    sv = jnp.concatenate(sv_cols, axis=1)
    return dict(u=u, dz=dz, rstd=rstd, vhat=vhat, vl=vl, sv=sv, yb=u * sv)


def _mix_fwd(x, modv, win, wpool, vecs, wcat, bias, wout):
    S = x.shape[0]
    tm = TM_MIX
    nt = S // tm
    nchunks = tm // CHUNK

    def body(x_ref, mod_ref, win_ref, wpool_ref, vec_ref, wcat_ref, bias_ref, wout_ref,
             xo_ref, pooled_ref, zb_ref, ext):
        i = pl.program_id(0)

        @pl.when(i == 0)
        def _():
            ext[0:HALO, :] = jnp.zeros((HALO, DP), F32)

        x = x_ref[...]
        _, _, _, h = _norm_mod(x, mod_ref[3:4, :], mod_ref[1:2, :], mod_ref[0:1, :])
        proj = _dot(h.astype(BF16), win_ref[...])
        xa = proj[:, :DP]
        zb = proj[:, DP:]
        zb_ref[...] = zb
        ext[HALO:HALO + tm, :] = xa
        pos = i * tm + lax.broadcasted_iota(jnp.int32, (tm, 1), 0)
        vecs = vec_ref[...]
        ya_cols = []
        pooled_cols = []
        for gi, w in enumerate(POOL_WINDOWS):
            cols = slice(gi * LANE, (gi + 1) * LANE)
            s = xa[:, cols]
            for k in range(1, w):
                s = s + ext[HALO - k:HALO - k + tm, cols]
            cnt = jnp.minimum(pos + 1, w).astype(F32)
            pooled = (s / cnt - xa[:, cols]).astype(BF16)
            pooled_cols.append(pooled)
            ya_cols.append(_dot(pooled, wpool_ref[gi]) * vecs[0:1, cols])
        pooled_ref[...] = jnp.concatenate(pooled_cols, axis=1)
        ext[0:HALO, :] = ext[tm:tm + HALO, :]

        gm = _gmlp_branch(zb, vecs, wcat_ref, bias_ref, nchunks)
        cat = jnp.concatenate(ya_cols + [gm["yb"]], axis=1).astype(BF16)
        xo_ref[...] = x + mod_ref[2:3, :] * _dot(cat, wout_ref[...])

    full = lambda shape: pl.BlockSpec(shape, lambda i: (0,) * len(shape))
    return _call(
        body, name="mix_fwd",
        grid=(nt,),
        in_specs=[pl.BlockSpec((tm, D), lambda i: (i, 0)), full((8, D)), full((D, DPROJ)),
                  full((4, LANE, LANE)), full((8, DP)), full((4, CHUNK, 2 * CHUNK)), full((CHUNK, DG)),
                  full((DP + DG, D))],
        out_specs=[pl.BlockSpec((tm, D), lambda i: (i, 0)), pl.BlockSpec((tm, DP), lambda i: (i, 0)),
                   pl.BlockSpec((tm, 2 * DG), lambda i: (i, 0))],
        out_shape=[_sds((S, D), F32), _sds((S, DP), BF16), _sds((S, 2 * DG), F32)],
        scratch_shapes=[pltpu.VMEM((tm + HALO, DP), F32)],
        compiler_params=_params(("arbitrary",), VMEM_LIMIT),
    )(x, modv, win, wpool, vecs, wcat, bias, wout)


def _mix_bwd(x, dxo, pooled, zb, modv, win, wpool, vecs, wcat, wtcat, bias, wout, side=None):
    S = x.shape[0]
    tm = TM_MIX
    nt = S // tm
    nchunks = tm // CHUNK

    def body(x_ref, dxo_ref, pooled_ref, zb_ref, mod_ref, win_ref, wpool_ref, vec_ref, wcat_ref, wtcat_ref,
             bias_ref, wout_ref,
             dx_ref, dwin_ref, dwout_ref, dwpool_ref, dwsp_ref, dbsp_ref, v512_ref, vd_ref, qext, dsv_acc):
        step = pl.program_id(0)
        tile = nt - 1 - step

        @pl.when(step == 0)
        def _():
            dwin_ref[...] = jnp.zeros_like(dwin_ref)
            dwout_ref[...] = jnp.zeros_like(dwout_ref)
            dwpool_ref[...] = jnp.zeros_like(dwpool_ref)
            dwsp_ref[...] = jnp.zeros_like(dwsp_ref)
            v512_ref[...] = jnp.zeros_like(v512_ref)
            vd_ref[...] = jnp.zeros_like(vd_ref)
            dsv_acc[...] = jnp.zeros_like(dsv_acc)
            qext[tm:tm + HALO, :] = jnp.zeros((HALO, DP), F32)

        gn, sc, sh, gate = mod_ref[3:4, :], mod_ref[1:2, :], mod_ref[0:1, :], mod_ref[2:3, :]
        vecs = vec_ref[...]
        x = x_ref[...]
        r, xn, hp, h = _norm_mod(x, gn, sc, sh)
        hb = h.astype(BF16)
        dxo = dxo_ref[...]

        pooled = pooled_ref[...]
        mixed_cols = [_dot(pooled[:, gi * LANE:(gi + 1) * LANE], wpool_ref[gi]) for gi in range(4)]
        mixed = jnp.concatenate(mixed_cols, axis=1)
        scale = vecs[0:1, :]
        gm = _gmlp_branch(zb_ref[...], vecs, wcat_ref, bias_ref, nchunks)
        cat = jnp.concatenate([mixed * scale, gm["yb"]], axis=1).astype(BF16)

        dwout_ref[...] += _dot_tn(cat, dxo.astype(BF16))
        dcat = _dot_nt((dxo * gate).astype(BF16), wout_ref[...])
        dya = dcat[:, :DP]
        dyb = dcat[:, DP:]

        v512_ref[0:1, :] += jnp.sum(dya * mixed, axis=0, keepdims=True)
        dmixed = (dya * scale).astype(BF16)
        pos = tile * tm + lax.broadcasted_iota(jnp.int32, (tm, 1), 0)
        dpooled_cols = []
        for gi, w in enumerate(POOL_WINDOWS):
            cols = slice(gi * LANE, (gi + 1) * LANE)
            dp = _dot_nt(dmixed[:, cols], wpool_ref[gi])
            dwpool_ref[gi] += _dot_tn(pooled[:, cols], dmixed[:, cols])
            cnt = jnp.minimum(pos + 1, w).astype(F32)
            qext[0:tm, cols] = dp / cnt
            dpooled_cols.append(dp)
        dxa_cols = []
        for gi, w in enumerate(POOL_WINDOWS):
            cols = slice(gi * LANE, (gi + 1) * LANE)
            s = qext[0:tm, cols]
            for k in range(1, w):
                s = s + qext[k:k + tm, cols]
            dxa_cols.append(s - dpooled_cols[gi])
        qext[tm:tm + HALO, :] = qext[0:HALO, :]

        u, sv, vl = gm["u"], gm["sv"], gm["vl"]
        du = dyb * sv
        dsv = dyb * u
        dvl_cols = []
        for p in range(4):
            cols = slice(p * LANE, (p + 1) * LANE)
            dblocks = [dsv[k * CHUNK:(k + 1) * CHUNK, cols] for k in range(nchunks)]
            vblocks = [vl[k * CHUNK:(k + 1) * CHUNK, cols] for k in range(nchunks)]
            tot = dblocks[0]
            for b in dblocks[1:]:
                tot = tot + b
            dsv_acc[:, cols] += tot
            top, bot = _pair_rhs(dblocks)
            out = _dot(wtcat_ref[p], jnp.concatenate([top, bot], axis=0).astype(BF16))
            dvl_cols.append(jnp.concatenate([out[:, k * LANE:(k + 1) * LANE] for k in range(nchunks)], axis=0))
            vcat = jnp.concatenate(vblocks, axis=1).astype(BF16)
            dwsp_ref[2 * p] += _dot_nt(top.astype(BF16), vcat)
            dwsp_ref[2 * p + 1] += _dot_nt(bot.astype(BF16), vcat)
        dvl = jnp.concatenate(dvl_cols, axis=1)
        vhat, rstd = gm["vhat"], gm["rstd"]
        v512_ref[1:2, :] += jnp.sum(dvl * vhat, axis=0, keepdims=True)
        v512_ref[2:3, :] += jnp.sum(dvl, axis=0, keepdims=True)
        dvh = dvl * vecs[1:2, :]
        dv = rstd * (dvh - jnp.mean(dvh, axis=-1, keepdims=True)
                     - vhat * jnp.mean(dvh * vhat, axis=-1, keepdims=True))
        dzb = jnp.concatenate([du, dv], axis=1) * gm["dz"]

        dproj = jnp.concatenate(dxa_cols + [dzb], axis=1).astype(BF16)
        dwin_ref[...] += _dot_tn(hb, dproj)
        dh = _dot_nt(dproj, win_ref[...])
        dsh, dsc, dgn, dxin = _norm_mod_bwd(dh, r, xn, hp, gn, sc)
        vd_ref[0:1, :] += dsh
        vd_ref[1:2, :] += dsc
        vd_ref[3:4, :] += dgn
        dx_ref[...] = dxo + dxin

        @pl.when(step == nt - 1)
        def _():
            gw = dwout_ref[...]
            vd_ref[2:3, :] += jnp.sum(wout_ref[...].astype(F32) * gw, axis=0, keepdims=True)
            dwout_ref[...] = gw * gate
            row = lax.broadcasted_iota(jnp.int32, (CHUNK, CHUNK), 0)
            col = lax.broadcasted_iota(jnp.int32, (CHUNK, CHUNK), 1)
            for hh in range(8):
                dwsp_ref[hh] = jnp.where(col <= row, dwsp_ref[hh], 0.0)
            head = lax.broadcasted_iota(jnp.int32, (8, DG), 0)
            ch = lax.broadcasted_iota(jnp.int32, (8, DG), 1)
            spread = jnp.where(ch // 64 == head, 1.0, 0.0).astype(F32)
            dbsp_ref[...] = lax.dot_general(spread, dsv_acc[...], (((1,), (1,)), ((), ())),
                                            precision=HIGHEST, preferred_element_type=F32)

    full = lambda shape: pl.BlockSpec(shape, lambda s: (0,) * len(shape))
    rev = lambda cols: pl.BlockSpec((tm, cols), lambda s: (nt - 1 - s, 0))
    step = lambda s: lambda: pl.program_id(0) == s
    return _side_call(
        body, side, (step(0), None, step(nt - 1)), name="mix_bwd",
        grid=(nt,),
        in_specs=[rev(D), rev(D), rev(DP), rev(2 * DG), full((8, D)), full((D, DPROJ)), full((4, LANE, LANE)),
                  full((8, DP)), full((4, CHUNK, 2 * CHUNK)), full((4, CHUNK, 2 * CHUNK)), full((CHUNK, DG)),
                  full((DP + DG, D))],
        out_specs=[rev(D), full((D, DPROJ)), full((DP + DG, D)), full((4, LANE, LANE)), full((8, CHUNK, CHUNK)),
                   full((8, CHUNK)), full((8, DP)), full((8, D))],
        out_shape=[_sds((S, D), F32), _sds((D, DPROJ), F32), _sds((DP + DG, D), F32), _sds((4, LANE, LANE), F32),
                   _sds((8, CHUNK, CHUNK), F32), _sds((8, CHUNK), F32), _sds((8, DP), F32), _sds((8, D), F32)],
        scratch_shapes=[pltpu.VMEM((tm + HALO, DP), F32), pltpu.VMEM((CHUNK, DG), F32)],
        compiler_params=_params(("arbitrary",), VMEM_LIMIT),
        args=(x, dxo, pooled, zb, modv, win, wpool, vecs, wcat, wtcat, bias, wout))


def _chip_sum(g, rbuf, core):
    _, _, hr, cols = g.shape
    tr = _row_block(hr)

    def body(c_ref, g_ref, r_ref, o_ref):
        o_ref[...] = (g_ref[...] + r_ref[...]).astype(BF16)

    return pl.pallas_call(
        body, name="chip_sum", interpret=False,
        grid_spec=pltpu.PrefetchScalarGridSpec(
            num_scalar_prefetch=1, grid=(NQ, hr // tr),
            in_specs=[pl.BlockSpec((None, None, tr, cols), lambda q, i, c: (q, c[0], i, 0)),
                      pl.BlockSpec((None, tr, cols), lambda q, i, c: (q, i, 0))],
            out_specs=pl.BlockSpec((None, tr, cols), lambda q, i, c: (q, i, 0))),
        out_shape=_sds((NQ, hr, cols), BF16),
        compiler_params=_params(("arbitrary", "arbitrary"), None),
    )(core, g, rbuf)


def _chip_sum_pair(own, rbuf):
    _, hr, cols = own.shape
    tr = _row_block(hr)

    def body(a_ref, b_ref, o_ref):
        o_ref[...] = (a_ref[...] + b_ref[...]).astype(BF16)

    spec = pl.BlockSpec((None, tr, cols), lambda q, i: (q, i, 0))
    return _call(
        body, name="chip_sum_pair",
        grid=(NQ, hr // tr),
        in_specs=[spec, spec], out_specs=spec,
        out_shape=_sds((NQ, hr, cols), BF16),
        compiler_params=_params(("arbitrary", "arbitrary"), None),
    )(own, rbuf)


def _sum4(cs, rbuf, chip):
    _, hr, cols = rbuf.shape
    tr = _row_block(hr)

    def body(q_ref, c_ref, r1_ref, r2_ref, r3_ref, o_ref):
        acc = c_ref[...].astype(F32)
        for r in (r1_ref, r2_ref, r3_ref):
            acc = acc + r[...].astype(F32)
        o_ref[...] = acc

    slot = lambda k: pl.BlockSpec((None, tr, cols), lambda i, q: ((q[0] + k) % NQ, i, 0))
    return pl.pallas_call(
        body, name="sum4", interpret=False,
        grid_spec=pltpu.PrefetchScalarGridSpec(
            num_scalar_prefetch=1, grid=(hr // tr,),
            in_specs=[slot(0), slot(1), slot(2), slot(3)],
            out_specs=pl.BlockSpec((tr, cols), lambda i, q: (i, 0))),
        out_shape=_sds((hr, cols), F32),
        compiler_params=_params(("arbitrary",), None),
    )(chip, cs, rbuf, rbuf, rbuf)


def _adamw_halves(w, own, recv, m, v, core):
    rows, cols = w.shape
    hr = rows // 2
    tr = _row_block(hr, mult=8)
    nb = hr // tr

    def body(c_ref, w_ref, own_ref, recv_ref, m_ref, v_ref, g_ref, d_ref, mo_ref, vo_ref):
        g = jnp.where(pl.program_id(0) == c_ref[0], own_ref[...], recv_ref[...])
        d, mn, vn = _adamw(w_ref[...], g, m_ref[...], v_ref[...])
        g_ref[...] = g
        d_ref[...] = d
        mo_ref[...] = mn
        vo_ref[...] = vn

    full = pl.BlockSpec((tr, cols), lambda h, i, c: (h * nb + i, 0))
    half = pl.BlockSpec((tr, cols), lambda h, i, c: (i, 0))
    return pl.pallas_call(
        body, name="adamw_halves", interpret=False,
        grid_spec=pltpu.PrefetchScalarGridSpec(
            num_scalar_prefetch=1, grid=(2, nb),
            in_specs=[full, half, half, full, full],
            out_specs=[full] * 4),
        out_shape=[_sds((rows, cols), F32)] * 4,
        compiler_params=_params(("arbitrary", "arbitrary"), None),
    )(core, w, own, recv, m, v)


def _cast_place(w, chip):
    rows, cols = w.shape
    tr = _row_block(rows)

    def body(q_ref, w_ref, o_ref):
        o_ref[...] = w_ref[...].astype(BF16)

    return pl.pallas_call(
        body, name="cast_place", interpret=False,
        grid_spec=pltpu.PrefetchScalarGridSpec(
            num_scalar_prefetch=1, grid=(rows // tr,),
            in_specs=[pl.BlockSpec((tr, cols), lambda i, q: (i, 0))],
            out_specs=pl.BlockSpec((None, tr, cols), lambda i, q: (q[0], i, 0))),
        out_shape=_sds((NQ, rows, cols), BF16),
        compiler_params=_params(("arbitrary",), None),
    )(chip, w)


def _ada_grad_adamw(cact_t, dmod_q, w, m, v, side=None):
    rows, cols = w.shape
    tc = 256
    assert cols % tc == 0

    def body(c_ref, d_ref, w_ref, m_ref, v_ref, g_ref, dl_ref, mo_ref, vo_ref):
        g = jnp.dot(c_ref[...], d_ref[...], precision=HIGHEST, preferred_element_type=F32)
        d, mn, vn = _adamw(w_ref[...], g, m_ref[...], v_ref[...])
        g_ref[...] = g
        dl_ref[...] = d
        mo_ref[...] = mn
        vo_ref[...] = vn

    spec = pl.BlockSpec((rows, tc), lambda i: (0, i))
    step = lambda s: lambda: pl.program_id(0) == s
    return _side_call(
        body, side, (step(0), None, step(cols // tc - 1)), name="ada_grad_adamw",
        grid=(cols // tc,),
        in_specs=[pl.BlockSpec((rows, 8), lambda i: (0, 0)), pl.BlockSpec((8, tc), lambda i: (0, i)),
                  spec, spec, spec],
        out_specs=[spec] * 4,
        out_shape=[_sds((rows, cols), F32)] * 4,
        scratch_shapes=[],
        compiler_params=_params(("arbitrary",), None),
        args=(cact_t, dmod_q, w, m, v))


def _me():
    x, y, c = lax.axis_index("x"), lax.axis_index("y"), lax.axis_index("c")
    return x, y, c


_OFFSETS7 = [(dx, dy, dc) for dx in (0, 1) for dy in (0, 1) for dc in (0, 1) if (dx, dy, dc) != (0, 0, 0)]
_CHIP_OFFSETS = [(1, 0), (0, 1), (1, 1)]


def _ada_fwd(c, w_ada_q, b_ada_q):
    ncol = w_ada_q.shape[1]

    def body(c_ref, w_ref, b_ref, cact_ref, modsel_ref, blk, gath, res, parts, send_sems, recv_sems):
        x, y, cc = _me()
        me = 4 * x + 2 * y + cc
        q = 2 * x + y
        cv = c_ref[...]
        ca = cv * jax.nn.sigmoid(cv)
        row = lax.broadcasted_iota(jnp.int32, (8, D), 0)
        blk[...] = jnp.where(row == me, jnp.broadcast_to(ca, (8, D)), 0.0)
        gath[me] = blk[...]
        sends = []
        for k, (dx, dy, dc) in enumerate(_OFFSETS7):
            cp = pltpu.make_async_remote_copy(blk, gath.at[me], send_sems.at[k], recv_sems.at[k],
                                              device_id=(x ^ dx, y ^ dy, cc ^ dc), device_id_type=MESH)
            cp.start()
            sends.append(cp)
        for cp in sends:
            cp.wait_recv()
        cact = gath[0]
        for d in range(1, N_DEV):
            cact = cact + gath[d]
        cact_ref[...] = cact
        res[...] = jnp.dot(cact, w_ref[...], precision=HIGHEST, preferred_element_type=F32) + b_ref[...]
        parts[q] = res[...]
        sends2 = []
        for k, (dx, dy) in enumerate(_CHIP_OFFSETS):
            cp = pltpu.make_async_remote_copy(res, parts.at[q], send_sems.at[7 + k], recv_sems.at[7 + k],
                                              device_id=(x ^ dx, y ^ dy, cc), device_id_type=MESH)
            cp.start()
            sends2.append(cp)
        for cp in sends2:
            cp.wait_recv()
        row2 = lax.broadcasted_iota(jnp.int32, (8, ncol), 0)
        out = jnp.zeros((8, ncol), F32)
        for s in range(NQ):
            mine = jnp.sum(jnp.where(row2 == me, parts[s], 0.0), axis=0, keepdims=True)
            out = out + jnp.where(row2 == s, jnp.broadcast_to(mine, (8, ncol)), 0.0)
        modsel_ref[...] = out
        for cp in sends + sends2:
            cp.wait_send()

    return _call(
        body, name="ada_fwd",
        in_specs=[VMEM, VMEM, VMEM], out_specs=[VMEM, VMEM],
        out_shape=[_sds((8, D), F32), _sds((8, ncol), F32)],
        scratch_shapes=[pltpu.VMEM((8, D), F32), pltpu.VMEM((N_DEV, 8, D), F32), pltpu.VMEM((8, ncol), F32),
                        pltpu.VMEM((NQ, 8, ncol), F32), pltpu.SemaphoreType.DMA((10,)), pltpu.SemaphoreType.DMA((10,))],
        compiler_params=_params(None, VMEM_LIMIT),
    )(c, w_ada_q, b_ada_q)


class _Side:
    def __init__(self, ins, out_shapes, aliases, nsem, start, mid=None, finish=None):
        self.ins, self.out_shapes, self.aliases, self.nsem = list(ins), list(out_shapes), dict(aliases), nsem
        self.start, self.mid, self.finish = start, mid, finish


def _join(*sides):
    ins, outs, aliases, offs, nsem = [], [], {}, [], 0
    for s in sides:
        offs.append((len(ins), len(outs), nsem))
        aliases.update({len(ins) + a: len(outs) + b for a, b in s.aliases.items()})
        ins += s.ins
        outs += s.out_shapes
        nsem += s.nsem

    def hook(name):
        def run(i, o, ss, rs, base):
            for s, (io, oo, so) in zip(sides, offs):
                fn = getattr(s, name)
                if fn is not None:
                    fn(i[io:io + len(s.ins)], o[oo:oo + len(s.out_shapes)], ss, rs, base + so)
        return run

    return _Side(ins, outs, aliases, nsem, hook("start"), hook("mid"), hook("finish"))


def _side_call(body, side, when, *, name, in_specs, out_specs, out_shape, scratch_shapes, args, **kw):
    n_in, n_out = len(in_specs), len(out_specs)
    if side is None:
        return _call(body, name=name, in_specs=in_specs, out_specs=out_specs, out_shape=out_shape,
                     scratch_shapes=scratch_shapes, **kw)(*args), []
    ns_in, ns_out = len(side.ins), len(side.out_shapes)

    def hook(fn, k, operands):
        if fn is None:
            return
        if when is None:
            fn(*operands, 0)
        elif when[k] is not None:
            pl.when(when[k]())(functools.partial(fn, *operands, 0))

    def wrapped(*refs):
        ins, s_ins = refs[:n_in], refs[n_in:n_in + ns_in]
        o0 = n_in + ns_in
        outs, s_outs = refs[o0:o0 + n_out], refs[o0 + n_out:o0 + n_out + ns_out]
        rest = refs[o0 + n_out + ns_out:]
        scratch, operands = rest[:-2], (s_ins, s_outs, rest[-2], rest[-1])
        hook(side.start, 0, operands)
        body(*ins, *outs, *scratch)
        hook(side.mid, 1, operands)
        hook(side.finish, 2, operands)

    res = _call(
        wrapped, name=name,
        in_specs=list(in_specs) + [ANY] * ns_in, out_specs=list(out_specs) + [ANY] * ns_out,
        out_shape=list(out_shape) + side.out_shapes,
        scratch_shapes=list(scratch_shapes) + [pltpu.SemaphoreType.DMA((side.nsem,)),
                                               pltpu.SemaphoreType.DMA((side.nsem,))],
        input_output_aliases={n_in + a: n_out + b for a, b in side.aliases.items()},
        **kw)(*args, *side.ins)
    return res[:n_out], res[n_out:]


def _run_side(side, name):
    return _side_call(lambda: None, side, None, name=name, in_specs=[], out_specs=[], out_shape=[],
                      scratch_shapes=[], args=[])[1]


def _remote(src, dst, ss, rs, k, dev):
    return pltpu.make_async_remote_copy(src, dst, ss.at[k], rs.at[k], device_id=dev, device_id_type=MESH)


def _gather_side(bufs):
    n = len(bufs)

    def walk(outs, half):
        x, y, cc = _me()
        for w in range(n):
            hr = outs[w].shape[1] // 2
            rows = pl.ds((cc if half == "mine" else 1 - cc) * hr, hr)
            for j, (dx, dy) in enumerate(_CHIP_OFFSETS):
                yield w, j, (x ^ dx, y ^ dy, cc), outs[w].at[2 * (x ^ dx) + (y ^ dy), rows], outs[w].at[2 * x + y, rows]

    def start(ins, outs, ss, rs, b):
        for w, j, peer, _, own in walk(outs, "mine"):
            _remote(own, own, ss, rs, b + 6 * w + j, peer).start()

    def mid(ins, outs, ss, rs, b):
        x, y, cc = _me()
        for w, j, peer, land, _ in walk(outs, "mine"):
            _remote(land, land, ss, rs, b + 6 * w + j, peer).wait_recv()
            _remote(land, land, ss, rs, b + 6 * w + 3 + j, (x, y, 1 - cc)).start()

    def finish(ins, outs, ss, rs, b):
        x, y, cc = _me()
        for w, j, _, land, _ in walk(outs, "other"):
            _remote(land, land, ss, rs, b + 6 * w + 3 + j, (x, y, 1 - cc)).wait_recv()
        for w, j, peer, land, own in walk(outs, "mine"):
            _remote(own, own, ss, rs, b + 6 * w + j, peer).wait_send()
            _remote(land, land, ss, rs, b + 6 * w + 3 + j, (x, y, 1 - cc)).wait_send()

    return _Side(bufs, [_sds(tuple(w.shape), w.dtype) for w in bufs], {i: i for i in range(n)}, 6 * n,
                 start, mid, finish)


def _copies_side(ins, out_shapes, nsem, copies):
    def start(*a):
        for cp in copies(*a):
            cp.start()

    def finish(*a):
        for cp in copies(*a):
            cp.wait()

    return _Side(ins, out_shapes, {}, nsem, start, None, finish)


def _swap_side(gs):
    def copies(ins, outs, ss, rs, b):
        x, y, cc = _me()
        return [_remote(ins[w].at[:, 1 - cc], outs[w], ss, rs, b + w, (x, y, 1 - cc)) for w in range(len(gs))]

    return _copies_side(gs, [_sds((NQ,) + tuple(g.shape[2:]), F32) for g in gs], len(gs), copies)


def _exchange_side(cs):
    def copies(ins, outs, ss, rs, b):
        x, y, cc = _me()
        return [_remote(ins[w].at[2 * (x ^ dx) + (y ^ dy)], outs[w].at[2 * x + y], ss, rs, b + 3 * w + j,
                        (x ^ dx, y ^ dy, cc))
                for w in range(len(cs)) for j, (dx, dy) in enumerate(_CHIP_OFFSETS)]

    return _copies_side(cs, [_sds(tuple(c.shape), c.dtype) for c in cs], 3 * len(cs), copies)


def _share_side(fs):
    def copies(ins, outs, ss, rs, b):
        x, y, cc = _me()
        return [_remote(ins[w], outs[w], ss, rs, b + w, (x, y, 1 - cc)) for w in range(len(fs))]

    return _copies_side(fs, [_sds(tuple(f.shape), F32) for f in fs], len(fs), copies)


def _small_allreduce_adamw(g, w, m, v, nd):
    rows = g.shape[0]
    nr = rows - nd
    hr = nr // 2
    assert nd % 8 == 0 and hr % 8 == 0

    def body(g_ref, w_ref, m_ref, v_ref, gs_ref, d_ref, mo_ref, vo_ref, gath, sib, csum, slots, tot, ss, rs):
        x, y, cc = _me()
        me = 4 * x + 2 * y + cc
        q = 2 * x + y
        sibling = (x, y, 1 - cc)
        dm = g_ref.at[pl.ds(0, nd)]
        gath[me] = g_ref[0:nd, :]
        to_all = [_remote(dm, gath.at[me], ss, rs, k, (x ^ dx, y ^ dy, cc ^ dc)) for k, (dx, dy, dc) in enumerate(_OFFSETS7)]
        to_sib = _remote(g_ref.at[pl.ds(nd, nr)], sib, ss, rs, 7, sibling)
        for cp in to_all + [to_sib]:
            cp.start()
        to_sib.wait_recv()
        csum[...] = g_ref[nd:, :] + sib[...]
        mine = pl.ds(pl.multiple_of(cc * hr, 8), hr)
        slots[q] = csum[mine, :]
        to_chips = [_remote(csum.at[mine], slots.at[q], ss, rs, 8 + j, (x ^ dx, y ^ dy, cc))
                    for j, (dx, dy) in enumerate(_CHIP_OFFSETS)]
        for cp in to_chips:
            cp.start()
        for cp in to_chips:
            cp.wait_recv()
        tot[mine, :] = (slots[0] + slots[1]) + (slots[2] + slots[3])
        halves = _remote(tot.at[mine], tot.at[mine], ss, rs, 11, sibling)
        halves.start()
        for cp in to_all:
            cp.wait_recv()
        dsum = gath[0]
        for dev in range(1, N_DEV):
            dsum = dsum + gath[dev]
        halves.wait_recv()
        for lo, n, total in ((0, nd, dsum), (nd, nr, tot[...])):
            gs_ref[lo:lo + n, :] = total
            d, mn, vn = _adamw(w_ref[lo:lo + n, :], total, m_ref[lo:lo + n, :], v_ref[lo:lo + n, :])
            d_ref[lo:lo + n, :] = d
            mo_ref[lo:lo + n, :] = mn
            vo_ref[lo:lo + n, :] = vn
        for cp in to_all + [to_sib, halves] + to_chips:
            cp.wait_send()

    return _call(
        body, name="small_allreduce_adamw",
        in_specs=[VMEM] * 4, out_specs=[VMEM] * 5,
        out_shape=[_sds((rows, LANE), F32)] * 4 + [_sds((N_DEV, nd, LANE), F32)],
        scratch_shapes=[pltpu.VMEM((nr, LANE), F32), pltpu.VMEM((nr, LANE), F32), pltpu.VMEM((NQ, hr, LANE), F32),
                        pltpu.VMEM((nr, LANE), F32), pltpu.SemaphoreType.DMA((12,)), pltpu.SemaphoreType.DMA((12,))],
        compiler_params=_params(None, VMEM_LIMIT),
    )(g, w, m, v)


_SMALL = ["b_ada", "norm_ffn1_g", "norm_mix_g", "pool_scale", "gmlp_ln_g", "gmlp_ln_b", "b_spatial",
          "norm_ffn2_g", "norm_final_g", "w_pool", "w_spatial"]


def _pack(parts):
    blocks, layout, r0 = [], {}, 0
    for name in _SMALL:
        a = parts[name]
        n = a.size
        rows = -(-n // LANE)
        rows8 = -(-rows // 8) * 8
        flat = a.reshape(-1).astype(F32)
        if rows8 * LANE != n:
            flat = jnp.concatenate([flat, jnp.zeros((rows8 * LANE - n,), F32)])
        blocks.append(flat.reshape(rows8, LANE))
        layout[name] = (r0, n, a.shape)
        r0 += rows8
    return jnp.concatenate(blocks, axis=0), layout


def _unpack(packed, layout):
    out = {}
    for name, (r0, n, shape) in layout.items():
        rows = -(-n // LANE)
        out[name] = packed[r0:r0 + rows].reshape(-1)[:n].reshape(shape)
    return out


def _modv(mod9, sub, gain):
    rows = jnp.concatenate([mod9[3 * sub:3 * sub + 3], gain.reshape(1, D), jnp.zeros((4, D), F32)], axis=0)
    return rows


_BIG = ["ffn1_w_in", "ffn1_w_out", "w_mix_in", "w_mix_out", "ffn2_w_in", "ffn2_w_out"]


def kernel(x, c, w_ada, b_ada, norm_ffn1_g, ffn1_w_in, ffn1_w_out, norm_mix_g, w_mix_in, w_pool, pool_scale, gmlp_ln_g, gmlp_ln_b, w_spatial, b_spatial, w_mix_out, norm_ffn2_g, ffn2_w_in, ffn2_w_out, norm_final_g, loss_target, m_w_ada, m_b_ada, m_norm_ffn1_g, m_ffn1_w_in, m_ffn1_w_out, m_norm_mix_g, m_w_mix_in, m_w_pool, m_pool_scale, m_gmlp_ln_g, m_gmlp_ln_b, m_w_spatial, m_b_spatial, m_w_mix_out, m_norm_ffn2_g, m_ffn2_w_in, m_ffn2_w_out, m_norm_final_g, v_w_ada, v_b_ada, v_norm_ffn1_g, v_ffn1_w_in, v_ffn1_w_out, v_norm_mix_g, v_w_mix_in, v_w_pool, v_pool_scale, v_gmlp_ln_g, v_gmlp_ln_b, v_w_spatial, v_b_spatial, v_w_mix_out, v_norm_ffn2_g, v_ffn2_w_in, v_ffn2_w_out, v_norm_final_g):
    names = ["w_ada", "b_ada", "norm_ffn1_g", "ffn1_w_in", "ffn1_w_out", "norm_mix_g", "w_mix_in", "w_pool",
             "pool_scale", "gmlp_ln_g", "gmlp_ln_b", "w_spatial", "b_spatial", "w_mix_out", "norm_ffn2_g",
             "ffn2_w_in", "ffn2_w_out", "norm_final_g"]
    W = dict(zip(names, [w_ada, b_ada, norm_ffn1_g, ffn1_w_in, ffn1_w_out, norm_mix_g, w_mix_in, w_pool, pool_scale,
                         gmlp_ln_g, gmlp_ln_b, w_spatial, b_spatial, w_mix_out, norm_ffn2_g, ffn2_w_in, ffn2_w_out,
                         norm_final_g]))
    M = dict(zip(names, [m_w_ada, m_b_ada, m_norm_ffn1_g, m_ffn1_w_in, m_ffn1_w_out, m_norm_mix_g, m_w_mix_in, m_w_pool,
                         m_pool_scale, m_gmlp_ln_g, m_gmlp_ln_b, m_w_spatial, m_b_spatial, m_w_mix_out, m_norm_ffn2_g,
                         m_ffn2_w_in, m_ffn2_w_out, m_norm_final_g]))
    V = dict(zip(names, [v_w_ada, v_b_ada, v_norm_ffn1_g, v_ffn1_w_in, v_ffn1_w_out, v_norm_mix_g, v_w_mix_in, v_w_pool,
                         v_pool_scale, v_gmlp_ln_g, v_gmlp_ln_b, v_w_spatial, v_b_spatial, v_w_mix_out, v_norm_ffn2_g,
                         v_ffn2_w_in, v_ffn2_w_out, v_norm_final_g]))

    xi, yi, ci = _me()
    q = 2 * xi + yi
    core = ci.astype(jnp.int32).reshape(1)

    ncol = w_ada.shape[2]
    b_q = lax.dynamic_slice_in_dim(b_ada, q * ncol, ncol, axis=1)
    cact_all, modsel = _ada_fwd(c, w_ada[0], b_q)
    mod9 = modsel[:NQ].reshape(9, D)

    chip = q.astype(jnp.int32).reshape(1)
    place = lambda n: _cast_place(W[n][0], chip)
    xs, target = x[0], loss_target[0]
    mv1 = _modv(mod9, 0, norm_ffn1_g[0])
    mv2 = _modv(mod9, 1, norm_mix_g[0])
    mv3 = _modv(mod9, 2, norm_ffn2_g[0])
    wcat, wtcat, bias = _prep_spatial(w_spatial[0], b_spatial[0].T)
    wpool = w_pool[0].astype(BF16)
    vecs = jnp.concatenate([pool_scale, gmlp_ln_g, gmlp_ln_b, jnp.zeros((5, DP), F32)], axis=0)
    gf = jnp.concatenate([norm_final_g.reshape(1, D), jnp.zeros((7, D), F32)], axis=0)

    win1, wout1 = _run_side(_gather_side([place("ffn1_w_in"), place("ffn1_w_out")]), "gather_ffn1")
    later = ["w_mix_in", "w_mix_out", "ffn2_w_in", "ffn2_w_out"]
    (x1, g1s, u1s), got = _ffn_fwd(xs, mv1, win1, wout1.reshape(2, CH, D), side=_gather_side([place(n) for n in later]))
    wmi, wmo, win2, wout2 = got
    wmi = jnp.transpose(wmi, (1, 0, 2)).reshape(D, DPROJ)
    wmo = wmo.reshape(DP + DG, D)
    x2, pooled, zb = _mix_fwd(x1, mv2, wmi, wpool, vecs, wcat, bias, wmo)
    (dx3, g3s, u3s, loss_blk, dgf), _ = _ffn_fwd(x2, mv3, win2, wout2.reshape(2, CH, D), head=(target, gf))
    loss = lax.psum(loss_blk[0, 0], ("x", "y", "c"))

    wo1, wo2 = wout1.reshape(2, CH, D), wout2.reshape(2, CH, D)
    (dx2, oin2, oout2, rin2, rout2, vec3), _ = _ffn_bwd(x2, dx3, g3s, u3s, mv3, win2, wo2)
    cs2 = [_chip_sum_pair(oin2, rin2), _chip_sum_pair(oout2, rout2)]
    (dx1, dwmi, dwmo, dwpool, dwsp, dbsp, v512, vec2), ex2 = _mix_bwd(
        x1, dx2, pooled, zb, mv2, wmi, wpool, vecs, wcat, wtcat, bias, wmo, side=_exchange_side(cs2))
    half2 = [_sum4(cs, e, chip) for cs, e in zip(cs2, ex2)]
    qcols = w_mix_in.shape[2]
    vmix = [jnp.transpose(dwmi.reshape(D, NQ, qcols), (1, 0, 2)).reshape(NQ, 2, D // 2, qcols),
            dwmo.reshape(NQ, 2, (DP + DG) // 8, D)]
    (grad_x, oin1, oout1, rin1, rout1, vec1), got = _ffn_bwd(
        xs, dx1, g1s, u1s, mv1, win1, wo1, side=_join(_swap_side(vmix), _share_side(half2)))
    sibmix, other2 = got[:2], got[2:]
    cs1 = [_chip_sum(g, r, core) for g, r in zip(vmix, sibmix)] + [_chip_sum_pair(oin1, rin1),
                                                                  _chip_sum_pair(oout1, rout1)]

    dmod = jnp.concatenate([vec1[0:3], vec2[0:3], vec3[0:3]], axis=0)
    grads = dict(
        b_ada=dmod.reshape(1, 9 * D), norm_ffn1_g=vec1[3:4], norm_mix_g=vec2[3:4], norm_ffn2_g=vec3[3:4],
        pool_scale=v512[0:1], gmlp_ln_g=v512[1:2], gmlp_ln_b=v512[2:3], b_spatial=dbsp[None],
        norm_final_g=dgf[0], w_pool=dwpool[None], w_spatial=dwsp[None])

    gp, layout = _pack({n: grads[n] for n in _SMALL})
    wp, _ = _pack({n: W[n] for n in _SMALL})
    mp, _ = _pack({n: M[n] for n in _SMALL})
    vp, _ = _pack({n: V[n] for n in _SMALL})
    r0, nb, _ = layout["b_ada"]
    assert r0 == 0
    out_g, out_d, out_m, out_v = {}, {}, {}, {}
    gs, dl, mo, vo, gath = _small_allreduce_adamw(gp, wp, mp, vp, nb // LANE)
    for packed, dst in ((gs, out_g), (dl, out_d), (mo, out_m), (vo, out_v)):
        for n, a in _unpack(packed, layout).items():
            dst[n] = a.reshape(W[n].shape)

    dmod_q = lax.dynamic_slice_in_dim(gath.reshape(N_DEV, nb), q * ncol, ncol, axis=1)
    (ga, da, ma, va), ex1 = _ada_grad_adamw(cact_all.T, dmod_q, w_ada[0], m_w_ada[0], v_w_ada[0],
                                           side=_exchange_side(cs1))
    out_g["w_ada"], out_d["w_ada"], out_m["w_ada"], out_v["w_ada"] = ga[None], da[None], ma[None], va[None]
    half1 = [_sum4(cs, e, chip) for cs, e in zip(cs1, ex1)]
    other1 = _run_side(_share_side(half1), "sibling_share")
    reduced = dict(zip(["w_mix_in", "w_mix_out", "ffn1_w_in", "ffn1_w_out", "ffn2_w_in", "ffn2_w_out"],
                       zip(half1 + half2, list(other1) + list(other2))))
    for n in _BIG:
        own, recv = reduced[n]
        g2, d, mn, vn = _adamw_halves(W[n][0], own, recv, M[n][0], V[n][0], core)
        out_g[n], out_d[n], out_m[n], out_v[n] = g2[None], d[None], mn[None], vn[None]

    return (loss, grad_x[None], *[out_g[n] for n in names], *[out_d[n] for n in names],
            *[out_m[n] for n in names], *[out_v[n] for n in names])
```

```python
import functools
import math

import jax
import jax.numpy as jnp
from jax import lax
from jax.experimental import pallas as pl
from jax.experimental.pallas import tpu as pltpu

F32 = jnp.float32
BF16 = jnp.bfloat16
MESH = pl.DeviceIdType.MESH
HIGHEST = lax.Precision.HIGHEST

EPS = 1e-6
D = 1024
DFF = 2816
CH = DFF // 2
NQ = 4
DP = 512
DG = 512
DPROJ = DP + 2 * DG
POOL_WINDOWS = (2, 4, 8, 16)
HALO = 16
CHUNK = 128
LANE = 128
N_DEV = 8

ADAM_LR = 0.001
ADAM_B1 = 0.9
ADAM_B2 = 0.999
ADAM_EPS = 1e-08
ADAM_WD = 0.01
ADAM_STEP = 10

VMEM_LIMIT = 62 * 1024 * 1024

TM_FFN_FWD = 512
TM_FFN_BWD = 512
TM_MIX = 256


def _call(body, **kw):
    return pl.pallas_call(body, interpret=False, **kw)


def _params(sem=None, vmem=None):
    return pltpu.CompilerParams(dimension_semantics=sem, vmem_limit_bytes=vmem)


def _sds(shape, dtype):
    return jax.ShapeDtypeStruct(shape, dtype)


ANY = pl.BlockSpec(memory_space=pl.ANY)
VMEM = pl.BlockSpec(memory_space=pltpu.VMEM)
SMEM = pl.BlockSpec(memory_space=pltpu.SMEM)


def _norm_mod(x, gn, sc, sh):
    r = lax.rsqrt(jnp.mean(x * x, axis=-1, keepdims=True) + EPS)
    xn = x * r
    hp = xn * gn
    return r, xn, hp, hp * (1.0 + sc) + sh


def _norm_mod_bwd(dh, r, xn, hp, gn, sc):
    one_sc = 1.0 + sc
    dsh = jnp.sum(dh, axis=0, keepdims=True)
    dsc = jnp.sum(dh * hp, axis=0, keepdims=True)
    dgn = jnp.sum(dh * one_sc * xn, axis=0, keepdims=True)
    dxn = dh * (gn * one_sc)
    dx = r * (dxn - xn * jnp.mean(dxn * xn, axis=-1, keepdims=True))
    return dsh, dsc, dgn, dx


def _dot(a, b):
    return jnp.dot(a, b, preferred_element_type=F32)


def _dot_nt(a, b):
    return lax.dot_general(a, b, (((1,), (1,)), ((), ())), preferred_element_type=F32)


def _dot_tn(a, b):
    return lax.dot_general(a, b, (((0,), (0,)), ((), ())), preferred_element_type=F32)


_GELU_C = math.sqrt(2.0 / math.pi)
_GELU_A = 0.044715


def _gelu_fwd_bwd(x):
    x2 = x * x
    t = jnp.tanh(_GELU_C * (x + _GELU_A * x * x2))
    g = 0.5 * x * (1.0 + t)
    dg = 0.5 * (1.0 + t) + 0.5 * x * (1.0 - t * t) * (_GELU_C * (1.0 + 3.0 * _GELU_A * x2))
    return g, dg


def _adamw(w, g, m, v):
    m = ADAM_B1 * m + (1.0 - ADAM_B1) * g
    v = ADAM_B2 * v + (1.0 - ADAM_B2) * (g * g)
    m_hat = m / (1.0 - ADAM_B1 ** ADAM_STEP)
    v_hat = v / (1.0 - ADAM_B2 ** ADAM_STEP)
    delta = -ADAM_LR * (m_hat / (jnp.sqrt(v_hat) + ADAM_EPS) + ADAM_WD * w)
    return delta, m, v


def _row_block(rows, cap=256, mult=16):
    best = None
    for t in range(mult, min(rows, cap) + 1, mult):
        if rows % t == 0:
            best = t
    assert best is not None, rows
    return best


def _head_math(x, target, gf):
    r = lax.rsqrt(jnp.mean(x * x, axis=-1, keepdims=True) + EPS)
    xn = x * r
    err = xn * gf - target
    dy = err * (1.0 / D)
    dxn = dy * gf
    dx = r * (dxn - xn * jnp.mean(dxn * xn, axis=-1, keepdims=True))
    return (0.5 / D) * jnp.sum(err * err), jnp.sum(dy * xn, axis=0, keepdims=True), dx


def _ffn_fwd(x, modv, win, wout, side=None, head=None):
    S = x.shape[0]
    tm = TM_FFN_FWD
    nt = S // tm

    def body(*refs):
        if head is None:
            x_ref, mod_ref, wg_ref, wu_ref, wo_ref, xo_ref, gs_ref, us_ref, acc_scr = refs
        else:
            (x_ref, mod_ref, wg_ref, wu_ref, wo_ref, t_ref, gf_ref,
             xo_ref, gs_ref, us_ref, loss_ref, dgf_ref, acc_scr) = refs

        @pl.when((pl.program_id(0) == 0) & (pl.program_id(1) == 0))
        def _():
            acc_scr[...] = jnp.zeros_like(acc_scr)
            if head is not None:
                loss_ref[...] = jnp.zeros_like(loss_ref)
                dgf_ref[...] = jnp.zeros_like(dgf_ref)

        j = pl.program_id(1)
        h = _norm_mod(x_ref[...], mod_ref[3:4, :], mod_ref[1:2, :], mod_ref[0:1, :])[3].astype(BF16)
        g = _dot(h, wg_ref[...]).astype(BF16)
        u = _dot(h, wu_ref[...]).astype(BF16)
        gs_ref[...] = g
        us_ref[...] = u
        gf = g.astype(F32)
        a = (gf * jax.nn.sigmoid(gf) * u.astype(F32)).astype(BF16)
        acc = jnp.where(j == 0, 0.0, acc_scr[...]) + _dot(a, wo_ref[...])
        acc_scr[...] = acc
        xo = x_ref[...] + (0.5 * mod_ref[2:3, :]) * acc
        if head is None:
            xo_ref[...] = xo
        else:
            @pl.when(j == 1)
            def _():
                loss, dgf, dx = _head_math(xo, t_ref[...], gf_ref[0:1, :])
                loss_ref[...] += loss
                dgf_ref[0:1, :] += dgf
                xo_ref[...] = dx

    step = lambda i, j: lambda: (pl.program_id(0) == i) & (pl.program_id(1) == j)
    tile = pl.BlockSpec((tm, D), lambda i, j: (i, 0))
    const = lambda shape: pl.BlockSpec(shape, lambda i, j: (0, 0))
    chunk = pl.BlockSpec((tm, CH), lambda i, j: (i, j))
    in_specs = [tile, const((8, D)), pl.BlockSpec((None, D, CH), lambda i, j: (j, 0, 0)),
                pl.BlockSpec((None, D, CH), lambda i, j: (2 + j, 0, 0)), pl.BlockSpec((None, CH, D), lambda i, j: (j, 0, 0))]
    out_specs = [tile, chunk, chunk]
    out_shape = [_sds((S, D), F32), _sds((S, DFF), BF16), _sds((S, DFF), BF16)]
    args = (x, modv, win, win, wout)
    if head is not None:
        in_specs += [tile, const((8, D))]
        out_specs += [const((8, LANE)), const((8, D))]
        out_shape += [_sds((8, LANE), F32), _sds((8, D), F32)]
        args += tuple(head)
    return _side_call(
        body, side, (step(0, 0), step((7 * nt) // 10, 0), step(nt - 1, 1)), name="ffn_fwd",
        grid=(nt, 2), in_specs=in_specs, out_specs=out_specs, out_shape=out_shape,
        scratch_shapes=[pltpu.VMEM((tm, D), F32)],
        compiler_params=_params(("arbitrary", "arbitrary"), VMEM_LIMIT),
        args=args)


def _ffn_bwd(x, dxo, gs, us, modv, win, wout, side=None):
    S = x.shape[0]
    tm = TM_FFN_BWD
    nsub = tm // 256
    nt = S // tm
    assert nt >= 2
    hi, ho = D // 2, CH // 4
    once = pl.Buffered(1)

    def body(x_ref, dxo_ref, gs_ref, us_ref, mod_ref, wg_ref, wu_ref, wo_ref,
             dx_ref, dwin_ref, dwout_ref, rwin_ref, rwout_ref, vec_ref, dhbuf_ref,
             accg, accu, accw, dh_st, dh_ld, sems, fsend, frecv):
        j = pl.program_id(0)
        i = pl.program_id(1)
        store = lambda t: pltpu.make_async_copy(dh_st, dhbuf_ref.at[t], sems.at[4])
        load = lambda t: pltpu.make_async_copy(dhbuf_ref.at[t], dh_ld, sems.at[5])

        @pl.when(i == 0)
        def _():
            accg[...] = jnp.zeros_like(accg)
            accu[...] = jnp.zeros_like(accu)
            accw[...] = jnp.zeros_like(accw)

        @pl.when((i == 0) & (j == 0))
        def _():
            vec_ref[...] = jnp.zeros_like(vec_ref)

        @pl.when((j == 1) & (i == 0))
        def _():
            store(nt - 1).wait()

        @pl.when(j == 1)
        def _():
            load(i).start()

        gn, sc, sh, gate = mod_ref[3:4, :], mod_ref[1:2, :], mod_ref[0:1, :], mod_ref[2:3, :]

        parts = []
        for s in range(nsub):
            rs = slice(s * (tm // nsub), (s + 1) * (tm // nsub))
            hb = _norm_mod(x_ref[rs, :], gn, sc, sh)[3].astype(BF16)
            dxo = dxo_ref[rs, :]
            dy = (dxo * (0.5 * gate)).astype(BF16)
            g = gs_ref[rs, :].astype(F32)
            u = us_ref[rs, :].astype(F32)
            sig = jax.nn.sigmoid(g)
            sl = g * sig
            a = (sl * u).astype(BF16)
            da = _dot_nt(dy, wo_ref[...])
            dg = (da * u * (sig * (1.0 + g * (1.0 - sig)))).astype(BF16)
            du = (da * sl).astype(BF16)
            dhp = _dot_nt(dg, wg_ref[...]) + _dot_nt(du, wu_ref[...])
            parts.append((hb, a, dg, du, dxo.astype(BF16), dhp))

        hb, a, dg, du, dxb = [jnp.concatenate(p, axis=0) if nsub > 1 else p[0] for p in list(zip(*parts))[:5]]
        accw[...] += _dot_tn(a, dxb)
        accg[...] += _dot_tn(hb, dg)
        accu[...] += _dot_tn(hb, du)

        @pl.when(j == 0)
        def _():
            @pl.when(i > 0)
            def _():
                store(i - 1).wait()

            for s in range(nsub):
                dh_st[s * (tm // nsub):(s + 1) * (tm // nsub), :] = parts[s][5]
            store(i).start()

        @pl.when(j == 1)
        def _():
            load(i).wait()
            for s in range(nsub):
                rs = slice(s * (tm // nsub), (s + 1) * (tm // nsub))
                dh = dh_ld[rs, :] + parts[s][5]
                r, xn, hp, _ = _norm_mod(x_ref[rs, :], gn, sc, sh)
                dsh, dsc, dgn, dxin = _norm_mod_bwd(dh, r, xn, hp, gn, sc)
                vec_ref[0:1, :] += dsh
                vec_ref[1:2, :] += dsc
                vec_ref[3:4, :] += dgn
                dx_ref[rs, :] = dxo_ref[rs, :] + dxin

        def flush(jj):
            mx, my, cc = _me()
            rows = lambda base, n, c: pl.ds(base + c * n, n)
            pieces = [(accg, 0, hi, dwin_ref, rwin_ref, jj), (accu, 0, hi, dwin_ref, rwin_ref, 2 + jj),
                      (accw, 0, ho, dwout_ref, rwout_ref, 2 * jj), (accw, 2 * ho, ho, dwout_ref, rwout_ref, 2 * jj + 1)]
            loc = [pltpu.make_async_copy(acc.at[rows(base, n, cc)], own.at[slot], sems.at[k])
                   for k, (acc, base, n, own, _, slot) in enumerate(pieces)]
            rem = [pltpu.make_async_remote_copy(acc.at[rows(base, n, 1 - cc)], sib.at[slot], fsend.at[4 * jj + k],
                                                frecv.at[4 * jj + k], device_id=(mx, my, 1 - cc), device_id_type=MESH)
                   for k, (acc, base, n, _, sib, slot) in enumerate(pieces)]
            return loc, rem

        for jj in range(2):
            @pl.when((i == nt - 1) & (j == jj))
            def _(jj=jj):
                gw = accw[...]
                vec_ref[2:3, :] += 0.5 * jnp.sum(wo_ref[...].astype(F32) * gw, axis=0, keepdims=True)
                accw[...] = gw * (0.5 * gate)
                loc, rem = flush(jj)
                for cp in loc + rem:
                    cp.start()
                for cp in loc:
                    cp.wait()
                for cp in rem:
                    cp.wait_send()

        @pl.when((i == nt - 1) & (j == 1))
        def _():
            for jj in range(2):
                for cp in flush(jj)[1]:
                    cp.wait_recv()

    step = lambda jj, ii: lambda: (pl.program_id(0) == jj) & (pl.program_id(1) == ii)
    (dx, dwin, dwout, rwin, rwout, vec, _), extra = _side_call(
        body, side, (step(0, 0), None, step(1, nt - 1)), name="ffn_bwd",
        grid=(2, nt),
        in_specs=[
            pl.BlockSpec((tm, D), lambda j, i: (i, 0)),
            pl.BlockSpec((tm, D), lambda j, i: (i, 0)),
            pl.BlockSpec((tm, CH), lambda j, i: (i, j)),
            pl.BlockSpec((tm, CH), lambda j, i: (i, j)),
            pl.BlockSpec((8, D), lambda j, i: (0, 0)),
            pl.BlockSpec((None, D, CH), lambda j, i: (j, 0, 0), pipeline_mode=once),
            pl.BlockSpec((None, D, CH), lambda j, i: (2 + j, 0, 0), pipeline_mode=once),
            pl.BlockSpec((None, CH, D), lambda j, i: (j, 0, 0), pipeline_mode=once),
        ],
        out_specs=[
            pl.BlockSpec((tm, D), lambda j, i: (i * j, 0)),
            ANY, ANY, ANY, ANY,
            pl.BlockSpec((8, D), lambda j, i: (0, 0)),
            ANY,
        ],
        out_shape=[_sds((S, D), F32), _sds((NQ, hi, CH), F32), _sds((NQ, ho, D), F32), _sds((NQ, hi, CH), F32),
                   _sds((NQ, ho, D), F32), _sds((8, D), F32), _sds((nt, tm, D), F32)],
        scratch_shapes=[pltpu.VMEM((D, CH), F32), pltpu.VMEM((D, CH), F32), pltpu.VMEM((CH, D), F32),
                        pltpu.VMEM((tm, D), F32), pltpu.VMEM((tm, D), F32), pltpu.SemaphoreType.DMA((6,)),
                        pltpu.SemaphoreType.DMA((8,)), pltpu.SemaphoreType.DMA((8,))],
        compiler_params=_params(("arbitrary", "arbitrary"), VMEM_LIMIT),
        args=(x, dxo, gs, us, modv, win, win, wout))
    return (dx, dwin, dwout, rwin, rwout, vec), extra


def _prep_spatial(w_spatial, b_spatial_t):
    def body(w_ref, b_ref, wcat_ref, wtcat_ref, bias_ref):
        row = lax.broadcasted_iota(jnp.int32, (CHUNK, CHUNK), 0)
        col = lax.broadcasted_iota(jnp.int32, (CHUNK, CHUNK), 1)
        tril = col <= row
        for p in range(4):
            wa = jnp.where(tril, w_ref[2 * p], 0.0)
            wb = jnp.where(tril, w_ref[2 * p + 1], 0.0)
            wcat_ref[p] = jnp.concatenate([wa, wb], axis=1).astype(BF16)
            wtcat_ref[p] = jnp.concatenate([wa.T, wb.T], axis=1).astype(BF16)
        head = lax.broadcasted_iota(jnp.int32, (8, DG), 0)
        ch = lax.broadcasted_iota(jnp.int32, (8, DG), 1)
        spread = jnp.where(ch // 64 == head, 1.0, 0.0).astype(F32)
        bias_ref[...] = jnp.dot(b_ref[...], spread, precision=HIGHEST, preferred_element_type=F32)

    return _call(
        body, name="prep_spatial",
        in_specs=[VMEM, VMEM], out_specs=[VMEM, VMEM, VMEM],
        out_shape=[_sds((4, CHUNK, 2 * CHUNK), BF16), _sds((4, CHUNK, 2 * CHUNK), BF16), _sds((CHUNK, DG), F32)],
    )(w_spatial, b_spatial_t)


def _pair_rhs(blocks):
    lane = lax.broadcasted_iota(jnp.int32, (CHUNK, LANE), 1)
    lo = lane < 64
    top = jnp.concatenate([jnp.where(lo, b, 0.0) for b in blocks], axis=1)
    bot = jnp.concatenate([jnp.where(lo, 0.0, b) for b in blocks], axis=1)
    return top, bot


def _gmlp_branch(zb, vecs, wcat_ref, bias_ref, nchunks):
    z, dz = _gelu_fwd_bwd(zb)
    u = z[:, :DG]
    v = z[:, DG:]
    ln_g, ln_b = vecs[1:2, :], vecs[2:3, :]
    mu = jnp.mean(v, axis=-1, keepdims=True)
    vc = v - mu
    rstd = lax.rsqrt(jnp.mean(vc * vc, axis=-1, keepdims=True) + EPS)
    vhat = vc * rstd
    vl = vhat * ln_g + ln_b
    sv_cols = []
    for p in range(4):
        blocks = [vl[k * CHUNK:(k + 1) * CHUNK, p * LANE:(p + 1) * LANE] for k in range(nchunks)]
        top, bot = _pair_rhs(blocks)
        rhs = jnp.concatenate([top, bot], axis=0).astype(BF16)
        out = _dot(wcat_ref[p], rhs)
        bias = bias_ref[:, p * LANE:(p + 1) * LANE]
        sv_cols.append(jnp.concatenate([out[:, k * LANE:(k + 1) * LANE] + bias for k in range(nchunks)], axis=0))
    sv = jnp.concatenate(sv_cols, axis=1)
    return dict(u=u, dz=dz, rstd=rstd, vhat=vhat, vl=vl, sv=sv, yb=u * sv)


def _mix_fwd(x, modv, win, wpool, vecs, wcat, bias, wout):
    S = x.shape[0]
    tm = TM_MIX
    nt = S // tm
    nchunks = tm // CHUNK

    def body(x_ref, mod_ref, win_ref, wpool_ref, vec_ref, wcat_ref, bias_ref, wout_ref,
             xo_ref, pooled_ref, zb_ref, ext):
        i = pl.program_id(0)

        @pl.when(i == 0)
        def _():
            ext[0:HALO, :] = jnp.zeros((HALO, DP), F32)

        x = x_ref[...]
        _, _, _, h = _norm_mod(x, mod_ref[3:4, :], mod_ref[1:2, :], mod_ref[0:1, :])
        proj = _dot(h.astype(BF16), win_ref[...])
        xa = proj[:, :DP]
        zb = proj[:, DP:]
        zb_ref[...] = zb
        ext[HALO:HALO + tm, :] = xa
        pos = i * tm + lax.broadcasted_iota(jnp.int32, (tm, 1), 0)
        vecs = vec_ref[...]
        ya_cols = []
        pooled_cols = []
        for gi, w in enumerate(POOL_WINDOWS):
            cols = slice(gi * LANE, (gi + 1) * LANE)
            s = xa[:, cols]
            for k in range(1, w):
                s = s + ext[HALO - k:HALO - k + tm, cols]
            cnt = jnp.minimum(pos + 1, w).astype(F32)
            pooled = (s / cnt - xa[:, cols]).astype(BF16)
            pooled_cols.append(pooled)
            ya_cols.append(_dot(pooled, wpool_ref[gi]) * vecs[0:1, cols])
        pooled_ref[...] = jnp.concatenate(pooled_cols, axis=1)
        ext[0:HALO, :] = ext[tm:tm + HALO, :]

        gm = _gmlp_branch(zb, vecs, wcat_ref, bias_ref, nchunks)
        cat = jnp.concatenate(ya_cols + [gm["yb"]], axis=1).astype(BF16)
        xo_ref[...] = x + mod_ref[2:3, :] * _dot(cat, wout_ref[...])

    full = lambda shape: pl.BlockSpec(shape, lambda i: (0,) * len(shape))
    return _call(
        body, name="mix_fwd",
        grid=(nt,),
        in_specs=[pl.BlockSpec((tm, D), lambda i: (i, 0)), full((8, D)), full((D, DPROJ)),
                  full((4, LANE, LANE)), full((8, DP)), full((4, CHUNK, 2 * CHUNK)), full((CHUNK, DG)),
                  full((DP + DG, D))],
        out_specs=[pl.BlockSpec((tm, D), lambda i: (i, 0)), pl.BlockSpec((tm, DP), lambda i: (i, 0)),
                   pl.BlockSpec((tm, 2 * DG), lambda i: (i, 0))],
        out_shape=[_sds((S, D), F32), _sds((S, DP), BF16), _sds((S, 2 * DG), F32)],
        scratch_shapes=[pltpu.VMEM((tm + HALO, DP), F32)],
        compiler_params=_params(("arbitrary",), VMEM_LIMIT),
    )(x, modv, win, wpool, vecs, wcat, bias, wout)


def _mix_bwd(x, dxo, pooled, zb, modv, win, wpool, vecs, wcat, wtcat, bias, wout, side=None):
    S = x.shape[0]
    tm = TM_MIX
    nt = S // tm
    nchunks = tm // CHUNK

    def body(x_ref, dxo_ref, pooled_ref, zb_ref, mod_ref, win_ref, wpool_ref, vec_ref, wcat_ref, wtcat_ref,
             bias_ref, wout_ref,
             dx_ref, dwin_ref, dwout_ref, dwpool_ref, dwsp_ref, dbsp_ref, v512_ref, vd_ref, qext, dsv_acc):
        step = pl.program_id(0)
        tile = nt - 1 - step

        @pl.when(step == 0)
        def _():
            dwin_ref[...] = jnp.zeros_like(dwin_ref)
            dwout_ref[...] = jnp.zeros_like(dwout_ref)
            dwpool_ref[...] = jnp.zeros_like(dwpool_ref)
            dwsp_ref[...] = jnp.zeros_like(dwsp_ref)
            v512_ref[...] = jnp.zeros_like(v512_ref)
            vd_ref[...] = jnp.zeros_like(vd_ref)
            dsv_acc[...] = jnp.zeros_like(dsv_acc)
            qext[tm:tm + HALO, :] = jnp.zeros((HALO, DP), F32)

        gn, sc, sh, gate = mod_ref[3:4, :], mod_ref[1:2, :], mod_ref[0:1, :], mod_ref[2:3, :]
        vecs = vec_ref[...]
        x = x_ref[...]
        r, xn, hp, h = _norm_mod(x, gn, sc, sh)
        hb = h.astype(BF16)
        dxo = dxo_ref[...]

        pooled = pooled_ref[...]
        mixed_cols = [_dot(pooled[:, gi * LANE:(gi + 1) * LANE], wpool_ref[gi]) for gi in range(4)]
        mixed = jnp.concatenate(mixed_cols, axis=1)
        scale = vecs[0:1, :]
        gm = _gmlp_branch(zb_ref[...], vecs, wcat_ref, bias_ref, nchunks)
        cat = jnp.concatenate([mixed * scale, gm["yb"]], axis=1).astype(BF16)

        dwout_ref[...] += _dot_tn(cat, dxo.astype(BF16))
        dcat = _dot_nt((dxo * gate).astype(BF16), wout_ref[...])
        dya = dcat[:, :DP]
        dyb = dcat[:, DP:]

        v512_ref[0:1, :] += jnp.sum(dya * mixed, axis=0, keepdims=True)
        dmixed = (dya * scale).astype(BF16)
        pos = tile * tm + lax.broadcasted_iota(jnp.int32, (tm, 1), 0)
        dpooled_cols = []
        for gi, w in enumerate(POOL_WINDOWS):
            cols = slice(gi * LANE, (gi + 1) * LANE)
            dp = _dot_nt(dmixed[:, cols], wpool_ref[gi])
            dwpool_ref[gi] += _dot_tn(pooled[:, cols], dmixed[:, cols])
            cnt = jnp.minimum(pos + 1, w).astype(F32)
            qext[0:tm, cols] = dp / cnt
            dpooled_cols.append(dp)
        dxa_cols = []
        for gi, w in enumerate(POOL_WINDOWS):
            cols = slice(gi * LANE, (gi + 1) * LANE)
            s = qext[0:tm, cols]
            for k in range(1, w):
                s = s + qext[k:k + tm, cols]
            dxa_cols.append(s - dpooled_cols[gi])
        qext[tm:tm + HALO, :] = qext[0:HALO, :]

        u, sv, vl = gm["u"], gm["sv"], gm["vl"]
        du = dyb * sv
        dsv = dyb * u
        dvl_cols = []
        for p in range(4):
            cols = slice(p * LANE, (p + 1) * LANE)
            dblocks = [dsv[k * CHUNK:(k + 1) * CHUNK, cols] for k in range(nchunks)]
            vblocks = [vl[k * CHUNK:(k + 1) * CHUNK, cols] for k in range(nchunks)]
            tot = dblocks[0]
            for b in dblocks[1:]:
                tot = tot + b
            dsv_acc[:, cols] += tot
            top, bot = _pair_rhs(dblocks)
            out = _dot(wtcat_ref[p], jnp.concatenate([top, bot], axis=0).astype(BF16))
            dvl_cols.append(jnp.concatenate([out[:, k * LANE:(k + 1) * LANE] for k in range(nchunks)], axis=0))
            vcat = jnp.concatenate(vblocks, axis=1).astype(BF16)
            dwsp_ref[2 * p] += _dot_nt(top.astype(BF16), vcat)
            dwsp_ref[2 * p + 1] += _dot_nt(bot.astype(BF16), vcat)
        dvl = jnp.concatenate(dvl_cols, axis=1)
        vhat, rstd = gm["vhat"], gm["rstd"]
        v512_ref[1:2, :] += jnp.sum(dvl * vhat, axis=0, keepdims=True)
        v512_ref[2:3, :] += jnp.sum(dvl, axis=0, keepdims=True)
        dvh = dvl * vecs[1:2, :]
        dv = rstd * (dvh - jnp.mean(dvh, axis=-1, keepdims=True)
                     - vhat * jnp.mean(dvh * vhat, axis=-1, keepdims=True))
        dzb = jnp.concatenate([du, dv], axis=1) * gm["dz"]

        dproj = jnp.concatenate(dxa_cols + [dzb], axis=1).astype(BF16)
        dwin_ref[...] += _dot_tn(hb, dproj)
        dh = _dot_nt(dproj, win_ref[...])
        dsh, dsc, dgn, dxin = _norm_mod_bwd(dh, r, xn, hp, gn, sc)
        vd_ref[0:1, :] += dsh
        vd_ref[1:2, :] += dsc
        vd_ref[3:4, :] += dgn
        dx_ref[...] = dxo + dxin

        @pl.when(step == nt - 1)
        def _():
            gw = dwout_ref[...]
            vd_ref[2:3, :] += jnp.sum(wout_ref[...].astype(F32) * gw, axis=0, keepdims=True)
            dwout_ref[...] = gw * gate
            row = lax.broadcasted_iota(jnp.int32, (CHUNK, CHUNK), 0)
            col = lax.broadcasted_iota(jnp.int32, (CHUNK, CHUNK), 1)
            for hh in range(8):
                dwsp_ref[hh] = jnp.where(col <= row, dwsp_ref[hh], 0.0)
            head = lax.broadcasted_iota(jnp.int32, (8, DG), 0)
            ch = lax.broadcasted_iota(jnp.int32, (8, DG), 1)
            spread = jnp.where(ch // 64 == head, 1.0, 0.0).astype(F32)
            dbsp_ref[...] = lax.dot_general(spread, dsv_acc[...], (((1,), (1,)), ((), ())),
                                            precision=HIGHEST, preferred_element_type=F32)

    full = lambda shape: pl.BlockSpec(shape, lambda s: (0,) * len(shape))
    rev = lambda cols: pl.BlockSpec((tm, cols), lambda s: (nt - 1 - s, 0))
    step = lambda s: lambda: pl.program_id(0) == s
    return _side_call(
        body, side, (step(0), None, step(nt - 1)), name="mix_bwd",
        grid=(nt,),
        in_specs=[rev(D), rev(D), rev(DP), rev(2 * DG), full((8, D)), full((D, DPROJ)), full((4, LANE, LANE)),
                  full((8, DP)), full((4, CHUNK, 2 * CHUNK)), full((4, CHUNK, 2 * CHUNK)), full((CHUNK, DG)),
                  full((DP + DG, D))],
        out_specs=[rev(D), full((D, DPROJ)), full((DP + DG, D)), full((4, LANE, LANE)), full((8, CHUNK, CHUNK)),
                   full((8, CHUNK)), full((8, DP)), full((8, D))],
        out_shape=[_sds((S, D), F32), _sds((D, DPROJ), F32), _sds((DP + DG, D), F32), _sds((4, LANE, LANE), F32),
                   _sds((8, CHUNK, CHUNK), F32), _sds((8, CHUNK), F32), _sds((8, DP), F32), _sds((8, D), F32)],
        scratch_shapes=[pltpu.VMEM((tm + HALO, DP), F32), pltpu.VMEM((CHUNK, DG), F32)],
        compiler_params=_params(("arbitrary",), VMEM_LIMIT),
        args=(x, dxo, pooled, zb, modv, win, wpool, vecs, wcat, wtcat, bias, wout))


def _chip_sum(g, rbuf, core):
    _, _, hr, cols = g.shape
    tr = _row_block(hr)

    def body(c_ref, g_ref, r_ref, o_ref):
        o_ref[...] = (g_ref[...] + r_ref[...]).astype(BF16)

    return pl.pallas_call(
        body, name="chip_sum", interpret=False,
        grid_spec=pltpu.PrefetchScalarGridSpec(
            num_scalar_prefetch=1, grid=(NQ, hr // tr),
            in_specs=[pl.BlockSpec((None, None, tr, cols), lambda q, i, c: (q, c[0], i, 0)),
                      pl.BlockSpec((None, tr, cols), lambda q, i, c: (q, i, 0))],
            out_specs=pl.BlockSpec((None, tr, cols), lambda q, i, c: (q, i, 0))),
        out_shape=_sds((NQ, hr, cols), BF16),
        compiler_params=_params(("arbitrary", "arbitrary"), None),
    )(core, g, rbuf)


def _chip_sum_pair(own, rbuf):
    _, hr, cols = own.shape
    tr = _row_block(hr)

    def body(a_ref, b_ref, o_ref):
        o_ref[...] = (a_ref[...] + b_ref[...]).astype(BF16)

    spec = pl.BlockSpec((None, tr, cols), lambda q, i: (q, i, 0))
    return _call(
        body, name="chip_sum_pair",
        grid=(NQ, hr // tr),
        in_specs=[spec, spec], out_specs=spec,
        out_shape=_sds((NQ, hr, cols), BF16),
        compiler_params=_params(("arbitrary", "arbitrary"), None),
    )(own, rbuf)


def _sum4(cs, rbuf, chip):
    _, hr, cols = rbuf.shape
    tr = _row_block(hr)

    def body(q_ref, c_ref, r1_ref, r2_ref, r3_ref, o_ref):
        acc = c_ref[...].astype(F32)
        for r in (r1_ref, r2_ref, r3_ref):
            acc = acc + r[...].astype(F32)
        o_ref[...] = acc

    slot = lambda k: pl.BlockSpec((None, tr, cols), lambda i, q: ((q[0] + k) % NQ, i, 0))
    return pl.pallas_call(
        body, name="sum4", interpret=False,
        grid_spec=pltpu.PrefetchScalarGridSpec(
            num_scalar_prefetch=1, grid=(hr // tr,),
            in_specs=[slot(0), slot(1), slot(2), slot(3)],
            out_specs=pl.BlockSpec((tr, cols), lambda i, q: (i, 0))),
        out_shape=_sds((hr, cols), F32),
        compiler_params=_params(("arbitrary",), None),
    )(chip, cs, rbuf, rbuf, rbuf)


def _adamw_halves(w, own, recv, m, v, side=None):
    rows, cols = w.shape
    hr = rows // 2
    tr = _row_block(hr, mult=8)
    nb = hr // tr

    def body(w_ref, own_ref, recv_ref, m_ref, v_ref, g_ref, d_ref, mo_ref, vo_ref):
        g = jnp.where(pl.program_id(0) == lax.axis_index("c"), own_ref[...], recv_ref[...])
        d, mn, vn = _adamw(w_ref[...], g, m_ref[...], v_ref[...])
        g_ref[...] = g
        d_ref[...] = d
        mo_ref[...] = mn
        vo_ref[...] = vn

    full = pl.BlockSpec((tr, cols), lambda h, i: (h * nb + i, 0))
    half = pl.BlockSpec((tr, cols), lambda h, i: (i, 0))
    step = lambda h, i: lambda: (pl.program_id(0) == h) & (pl.program_id(1) == i)
    return _side_call(
        body, side, (step(0, 0), None, step(1, nb - 1)), name="adamw_halves",
        grid=(2, nb), in_specs=[full, half, half, full, full], out_specs=[full] * 4,
        out_shape=[_sds((rows, cols), F32)] * 4, scratch_shapes=[],
        compiler_params=_params(("arbitrary", "arbitrary"), None),
        args=(w, own, recv, m, v))


def _cast_place(w, chip):
    rows, cols = w.shape
    tr = _row_block(rows)

    def body(q_ref, w_ref, o_ref):
        o_ref[...] = w_ref[...].astype(BF16)

    return pl.pallas_call(
        body, name="cast_place", interpret=False,
        grid_spec=pltpu.PrefetchScalarGridSpec(
            num_scalar_prefetch=1, grid=(rows // tr,),
            in_specs=[pl.BlockSpec((tr, cols), lambda i, q: (i, 0))],
            out_specs=pl.BlockSpec((None, tr, cols), lambda i, q: (q[0], i, 0))),
        out_shape=_sds((NQ, rows, cols), BF16),
        compiler_params=_params(("arbitrary",), None),
    )(chip, w)


def _ada_grad_adamw(cact_t, dmod_q, w, m, v, side=None):
    rows, cols = w.shape
    tc = 256
    assert cols % tc == 0

    def body(c_ref, d_ref, w_ref, m_ref, v_ref, g_ref, dl_ref, mo_ref, vo_ref):
        g = jnp.dot(c_ref[...], d_ref[...], precision=HIGHEST, preferred_element_type=F32)
        d, mn, vn = _adamw(w_ref[...], g, m_ref[...], v_ref[...])
        g_ref[...] = g
        dl_ref[...] = d
        mo_ref[...] = mn
        vo_ref[...] = vn

    spec = pl.BlockSpec((rows, tc), lambda i: (0, i))
    step = lambda s: lambda: pl.program_id(0) == s
    return _side_call(
        body, side, (step(0), None, step(cols // tc - 1)), name="ada_grad_adamw",
        grid=(cols // tc,),
        in_specs=[pl.BlockSpec((rows, 8), lambda i: (0, 0)), pl.BlockSpec((8, tc), lambda i: (0, i)),
                  spec, spec, spec],
        out_specs=[spec] * 4,
        out_shape=[_sds((rows, cols), F32)] * 4,
        scratch_shapes=[],
        compiler_params=_params(("arbitrary",), None),
        args=(cact_t, dmod_q, w, m, v))


def _me():
    x, y, c = lax.axis_index("x"), lax.axis_index("y"), lax.axis_index("c")
    return x, y, c


_OFFSETS7 = [(dx, dy, dc) for dx in (0, 1) for dy in (0, 1) for dc in (0, 1) if (dx, dy, dc) != (0, 0, 0)]
_CHIP_OFFSETS = [(1, 0), (0, 1), (1, 1)]


def _ada_fwd(c, w_ada_q, b_ada_q, side=None):
    ncol = w_ada_q.shape[1]

    def body(c_ref, w_ref, b_ref, cact_ref, modsel_ref, blk, gath, res, parts, send_sems, recv_sems):
        x, y, cc = _me()
        me = 4 * x + 2 * y + cc
        q = 2 * x + y
        cv = c_ref[...]
        ca = cv * jax.nn.sigmoid(cv)
        row = lax.broadcasted_iota(jnp.int32, (8, D), 0)
        blk[...] = jnp.where(row == me, jnp.broadcast_to(ca, (8, D)), 0.0)
        gath[me] = blk[...]
        sends = []
        for k, (dx, dy, dc) in enumerate(_OFFSETS7):
            cp = pltpu.make_async_remote_copy(blk, gath.at[me], send_sems.at[k], recv_sems.at[k],
                                              device_id=(x ^ dx, y ^ dy, cc ^ dc), device_id_type=MESH)
            cp.start()
            sends.append(cp)
        for cp in sends:
            cp.wait_recv()
        cact = gath[0]
        for d in range(1, N_DEV):
            cact = cact + gath[d]
        cact_ref[...] = cact
        res[...] = jnp.dot(cact, w_ref[...], precision=HIGHEST, preferred_element_type=F32) + b_ref[...]
        parts[q] = res[...]
        sends2 = []
        for k, (dx, dy) in enumerate(_CHIP_OFFSETS):
            cp = pltpu.make_async_remote_copy(res, parts.at[q], send_sems.at[7 + k], recv_sems.at[7 + k],
                                              device_id=(x ^ dx, y ^ dy, cc), device_id_type=MESH)
            cp.start()
            sends2.append(cp)
        for cp in sends2:
            cp.wait_recv()
        row2 = lax.broadcasted_iota(jnp.int32, (8, ncol), 0)
        out = jnp.zeros((8, ncol), F32)
        for s in range(NQ):
            mine = jnp.sum(jnp.where(row2 == me, parts[s], 0.0), axis=0, keepdims=True)
            out = out + jnp.where(row2 == s, jnp.broadcast_to(mine, (8, ncol)), 0.0)
        modsel_ref[...] = out
        for cp in sends + sends2:
            cp.wait_send()

    return _side_call(
        body, side, None, name="ada_fwd",
        in_specs=[VMEM, VMEM, VMEM], out_specs=[VMEM, VMEM],
        out_shape=[_sds((8, D), F32), _sds((8, ncol), F32)],
        scratch_shapes=[pltpu.VMEM((8, D), F32), pltpu.VMEM((N_DEV, 8, D), F32), pltpu.VMEM((8, ncol), F32),
                        pltpu.VMEM((NQ, 8, ncol), F32), pltpu.SemaphoreType.DMA((10,)), pltpu.SemaphoreType.DMA((10,))],
        compiler_params=_params(None, VMEM_LIMIT),
        args=(c, w_ada_q, b_ada_q))


class _Side:
    def __init__(self, ins, out_shapes, aliases, nsem, start, mid=None, finish=None):
        self.ins, self.out_shapes, self.aliases, self.nsem = list(ins), list(out_shapes), dict(aliases), nsem
        self.start, self.mid, self.finish = start, mid, finish


def _join(*sides):
    ins, outs, aliases, offs, nsem = [], [], {}, [], 0
    for s in sides:
        offs.append((len(ins), len(outs), nsem))
        aliases.update({len(ins) + a: len(outs) + b for a, b in s.aliases.items()})
        ins += s.ins
        outs += s.out_shapes
        nsem += s.nsem

    def hook(name):
        def run(i, o, ss, rs, base):
            for s, (io, oo, so) in zip(sides, offs):
                fn = getattr(s, name)
                if fn is not None:
                    fn(i[io:io + len(s.ins)], o[oo:oo + len(s.out_shapes)], ss, rs, base + so)
        return run

    return _Side(ins, outs, aliases, nsem, hook("start"), hook("mid"), hook("finish"))


def _side_call(body, side, when, *, name, in_specs, out_specs, out_shape, scratch_shapes, args, **kw):
    n_in, n_out = len(in_specs), len(out_specs)
    if side is None:
        return _call(body, name=name, in_specs=in_specs, out_specs=out_specs, out_shape=out_shape,
                     scratch_shapes=scratch_shapes, **kw)(*args), []
    ns_in, ns_out = len(side.ins), len(side.out_shapes)

    def hook(fn, k, operands):
        if fn is None:
            return
        if when is None:
            fn(*operands, 0)
        elif when[k] is not None:
            pl.when(when[k]())(functools.partial(fn, *operands, 0))

    def wrapped(*refs):
        ins, s_ins = refs[:n_in], refs[n_in:n_in + ns_in]
        o0 = n_in + ns_in
        outs, s_outs = refs[o0:o0 + n_out], refs[o0 + n_out:o0 + n_out + ns_out]
        rest = refs[o0 + n_out + ns_out:]
        scratch, operands = rest[:-2], (s_ins, s_outs, rest[-2], rest[-1])
        hook(side.start, 0, operands)
        body(*ins, *outs, *scratch)
        hook(side.mid, 1, operands)
        hook(side.finish, 2, operands)

    res = _call(
        wrapped, name=name,
        in_specs=list(in_specs) + [ANY] * ns_in, out_specs=list(out_specs) + [ANY] * ns_out,
        out_shape=list(out_shape) + side.out_shapes,
        scratch_shapes=list(scratch_shapes) + [pltpu.SemaphoreType.DMA((side.nsem,)),
                                               pltpu.SemaphoreType.DMA((side.nsem,))],
        input_output_aliases={n_in + a: n_out + b for a, b in side.aliases.items()},
        **kw)(*args, *side.ins)
    return res[:n_out], res[n_out:]


def _run_side(side, name):
    return _side_call(lambda: None, side, None, name=name, in_specs=[], out_specs=[], out_shape=[],
                      scratch_shapes=[], args=[])[1]


def _remote(src, dst, ss, rs, k, dev):
    return pltpu.make_async_remote_copy(src, dst, ss.at[k], rs.at[k], device_id=dev, device_id_type=MESH)


def _gather_side(bufs):
    n = len(bufs)

    def walk(outs, half):
        x, y, cc = _me()
        for w in range(n):
            hr = outs[w].shape[1] // 2
            rows = pl.ds((cc if half == "mine" else 1 - cc) * hr, hr)
            for j, (dx, dy) in enumerate(_CHIP_OFFSETS):
                yield w, j, (x ^ dx, y ^ dy, cc), outs[w].at[2 * (x ^ dx) + (y ^ dy), rows], outs[w].at[2 * x + y, rows]

    def start(ins, outs, ss, rs, b):
        for w, j, peer, _, own in walk(outs, "mine"):
            _remote(own, own, ss, rs, b + 6 * w + j, peer).start()

    def mid(ins, outs, ss, rs, b):
        x, y, cc = _me()
        for w, j, peer, land, _ in walk(outs, "mine"):
            _remote(land, land, ss, rs, b + 6 * w + j, peer).wait_recv()
            _remote(land, land, ss, rs, b + 6 * w + 3 + j, (x, y, 1 - cc)).start()

    def finish(ins, outs, ss, rs, b):
        x, y, cc = _me()
        for w, j, _, land, _ in walk(outs, "other"):
            _remote(land, land, ss, rs, b + 6 * w + 3 + j, (x, y, 1 - cc)).wait_recv()
        for w, j, peer, land, own in walk(outs, "mine"):
            _remote(own, own, ss, rs, b + 6 * w + j, peer).wait_send()
            _remote(land, land, ss, rs, b + 6 * w + 3 + j, (x, y, 1 - cc)).wait_send()

    return _Side(bufs, [_sds(tuple(w.shape), w.dtype) for w in bufs], {i: i for i in range(n)}, 6 * n,
                 start, mid, finish)


def _copies_side(ins, out_shapes, nsem, copies):
    def start(*a):
        for cp in copies(*a):
            cp.start()

    def finish(*a):
        for cp in copies(*a):
            cp.wait()

    return _Side(ins, out_shapes, {}, nsem, start, None, finish)


def _swap_side(gs):
    def copies(ins, outs, ss, rs, b):
        x, y, cc = _me()
        return [_remote(ins[w].at[:, 1 - cc], outs[w], ss, rs, b + w, (x, y, 1 - cc)) for w in range(len(gs))]

    return _copies_side(gs, [_sds((NQ,) + tuple(g.shape[2:]), F32) for g in gs], len(gs), copies)


def _exchange_side(cs):
    def copies(ins, outs, ss, rs, b):
        x, y, cc = _me()
        return [_remote(ins[w].at[2 * (x ^ dx) + (y ^ dy)], outs[w].at[2 * x + y], ss, rs, b + 3 * w + j,
                        (x ^ dx, y ^ dy, cc))
                for w in range(len(cs)) for j, (dx, dy) in enumerate(_CHIP_OFFSETS)]

    return _copies_side(cs, [_sds(tuple(c.shape), c.dtype) for c in cs], 3 * len(cs), copies)


def _share_side(fs):
    def copies(ins, outs, ss, rs, b):
        x, y, cc = _me()
        return [_remote(ins[w], outs[w], ss, rs, b + w, (x, y, 1 - cc)) for w in range(len(fs))]

    return _copies_side(fs, [_sds(tuple(f.shape), F32) for f in fs], len(fs), copies)


def _small_allreduce_adamw(g, w, m, v, nd):
    rows = g.shape[0]
    nr = rows - nd
    hr = nr // 2
    assert nd % 8 == 0 and hr % 8 == 0

    def body(g_ref, w_ref, m_ref, v_ref, gs_ref, d_ref, mo_ref, vo_ref, gath, sib, csum, slots, tot, ss, rs):
        x, y, cc = _me()
        me = 4 * x + 2 * y + cc
        q = 2 * x + y
        sibling = (x, y, 1 - cc)
        dm = g_ref.at[pl.ds(0, nd)]
        gath[me] = g_ref[0:nd, :]
        to_all = [_remote(dm, gath.at[me], ss, rs, k, (x ^ dx, y ^ dy, cc ^ dc)) for k, (dx, dy, dc) in enumerate(_OFFSETS7)]
        to_sib = _remote(g_ref.at[pl.ds(nd, nr)], sib, ss, rs, 7, sibling)
        for cp in to_all + [to_sib]:
            cp.start()
        to_sib.wait_recv()
        csum[...] = g_ref[nd:, :] + sib[...]
        mine = pl.ds(pl.multiple_of(cc * hr, 8), hr)
        slots[q] = csum[mine, :]
        to_chips = [_remote(csum.at[mine], slots.at[q], ss, rs, 8 + j, (x ^ dx, y ^ dy, cc))
                    for j, (dx, dy) in enumerate(_CHIP_OFFSETS)]
        for cp in to_chips:
            cp.start()
        for cp in to_chips:
            cp.wait_recv()
        tot[mine, :] = (slots[0] + slots[1]) + (slots[2] + slots[3])
        halves = _remote(tot.at[mine], tot.at[mine], ss, rs, 11, sibling)
        halves.start()
        for cp in to_all:
            cp.wait_recv()
        dsum = gath[0]
        for dev in range(1, N_DEV):
            dsum = dsum + gath[dev]
        halves.wait_recv()
        for lo, n, total in ((0, nd, dsum), (nd, nr, tot[...])):
            gs_ref[lo:lo + n, :] = total
            d, mn, vn = _adamw(w_ref[lo:lo + n, :], total, m_ref[lo:lo + n, :], v_ref[lo:lo + n, :])
            d_ref[lo:lo + n, :] = d
            mo_ref[lo:lo + n, :] = mn
            vo_ref[lo:lo + n, :] = vn
        for cp in to_all + [to_sib, halves] + to_chips:
            cp.wait_send()

    return _call(
        body, name="small_allreduce_adamw",
        in_specs=[VMEM] * 4, out_specs=[VMEM] * 5,
        out_shape=[_sds((rows, LANE), F32)] * 4 + [_sds((N_DEV, nd, LANE), F32)],
        scratch_shapes=[pltpu.VMEM((nr, LANE), F32), pltpu.VMEM((nr, LANE), F32), pltpu.VMEM((NQ, hr, LANE), F32),
                        pltpu.VMEM((nr, LANE), F32), pltpu.SemaphoreType.DMA((12,)), pltpu.SemaphoreType.DMA((12,))],
        compiler_params=_params(None, VMEM_LIMIT),
    )(g, w, m, v)


_SMALL = ["b_ada", "norm_ffn1_g", "norm_mix_g", "pool_scale", "gmlp_ln_g", "gmlp_ln_b", "b_spatial",
          "norm_ffn2_g", "norm_final_g", "w_pool", "w_spatial"]


def _pack(parts):
    blocks, layout, r0 = [], {}, 0
    for name in _SMALL:
        a = parts[name]
        n = a.size
        rows = -(-n // LANE)
        rows8 = -(-rows // 8) * 8
        flat = a.reshape(-1).astype(F32)
        if rows8 * LANE != n:
            flat = jnp.concatenate([flat, jnp.zeros((rows8 * LANE - n,), F32)])
        blocks.append(flat.reshape(rows8, LANE))
        layout[name] = (r0, n, a.shape)
        r0 += rows8
    return jnp.concatenate(blocks, axis=0), layout


def _unpack(packed, layout):
    out = {}
    for name, (r0, n, shape) in layout.items():
        rows = -(-n // LANE)
        out[name] = packed[r0:r0 + rows].reshape(-1)[:n].reshape(shape)
    return out


def _modv(mod9, sub, gain):
    rows = jnp.concatenate([mod9[3 * sub:3 * sub + 3], gain.reshape(1, D), jnp.zeros((4, D), F32)], axis=0)
    return rows


_BIG = ["ffn1_w_in", "ffn1_w_out", "w_mix_in", "w_mix_out", "ffn2_w_in", "ffn2_w_out"]


def kernel(x, c, w_ada, b_ada, norm_ffn1_g, ffn1_w_in, ffn1_w_out, norm_mix_g, w_mix_in, w_pool, pool_scale, gmlp_ln_g, gmlp_ln_b, w_spatial, b_spatial, w_mix_out, norm_ffn2_g, ffn2_w_in, ffn2_w_out, norm_final_g, loss_target, m_w_ada, m_b_ada, m_norm_ffn1_g, m_ffn1_w_in, m_ffn1_w_out, m_norm_mix_g, m_w_mix_in, m_w_pool, m_pool_scale, m_gmlp_ln_g, m_gmlp_ln_b, m_w_spatial, m_b_spatial, m_w_mix_out, m_norm_ffn2_g, m_ffn2_w_in, m_ffn2_w_out, m_norm_final_g, v_w_ada, v_b_ada, v_norm_ffn1_g, v_ffn1_w_in, v_ffn1_w_out, v_norm_mix_g, v_w_mix_in, v_w_pool, v_pool_scale, v_gmlp_ln_g, v_gmlp_ln_b, v_w_spatial, v_b_spatial, v_w_mix_out, v_norm_ffn2_g, v_ffn2_w_in, v_ffn2_w_out, v_norm_final_g):
    names = ["w_ada", "b_ada", "norm_ffn1_g", "ffn1_w_in", "ffn1_w_out", "norm_mix_g", "w_mix_in", "w_pool",
             "pool_scale", "gmlp_ln_g", "gmlp_ln_b", "w_spatial", "b_spatial", "w_mix_out", "norm_ffn2_g",
             "ffn2_w_in", "ffn2_w_out", "norm_final_g"]
    W = dict(zip(names, [w_ada, b_ada, norm_ffn1_g, ffn1_w_in, ffn1_w_out, norm_mix_g, w_mix_in, w_pool, pool_scale,
                         gmlp_ln_g, gmlp_ln_b, w_spatial, b_spatial, w_mix_out, norm_ffn2_g, ffn2_w_in, ffn2_w_out,
                         norm_final_g]))
    M = dict(zip(names, [m_w_ada, m_b_ada, m_norm_ffn1_g, m_ffn1_w_in, m_ffn1_w_out, m_norm_mix_g, m_w_mix_in, m_w_pool,
                         m_pool_scale, m_gmlp_ln_g, m_gmlp_ln_b, m_w_spatial, m_b_spatial, m_w_mix_out, m_norm_ffn2_g,
                         m_ffn2_w_in, m_ffn2_w_out, m_norm_final_g]))
    V = dict(zip(names, [v_w_ada, v_b_ada, v_norm_ffn1_g, v_ffn1_w_in, v_ffn1_w_out, v_norm_mix_g, v_w_mix_in, v_w_pool,
                         v_pool_scale, v_gmlp_ln_g, v_gmlp_ln_b, v_w_spatial, v_b_spatial, v_w_mix_out, v_norm_ffn2_g,
                         v_ffn2_w_in, v_ffn2_w_out, v_norm_final_g]))

    xi, yi, ci = _me()
    q = 2 * xi + yi
    core = ci.astype(jnp.int32).reshape(1)

    chip = q.astype(jnp.int32).reshape(1)
    place = lambda n: _cast_place(W[n][0], chip)

    ncol = w_ada.shape[2]
    b_q = lax.dynamic_slice_in_dim(b_ada, q * ncol, ncol, axis=1)
    (cact_all, modsel), (win1, wout1) = _ada_fwd(
        c, w_ada[0], b_q, side=_gather_side([place("ffn1_w_in"), place("ffn1_w_out")]))
    mod9 = modsel[:NQ].reshape(9, D)
    xs, target = x[0], loss_target[0]
    mv1 = _modv(mod9, 0, norm_ffn1_g[0])
    mv2 = _modv(mod9, 1, norm_mix_g[0])
    mv3 = _modv(mod9, 2, norm_ffn2_g[0])
    wcat, wtcat, bias = _prep_spatial(w_spatial[0], b_spatial[0].T)
    wpool = w_pool[0].astype(BF16)
    vecs = jnp.concatenate([pool_scale, gmlp_ln_g, gmlp_ln_b, jnp.zeros((5, DP), F32)], axis=0)
    gf = jnp.concatenate([norm_final_g.reshape(1, D), jnp.zeros((7, D), F32)], axis=0)

    later =["w_mix_in", "w_mix_out", "ffn2_w_in", "ffn2_w_out"]
    (x1, g1s, u1s), got = _ffn_fwd(xs, mv1, win1, wout1.reshape(2, CH, D), side=_gather_side([place(n) for n in later]))
    wmi, wmo, win2, wout2 = got
    wmi = jnp.transpose(wmi, (1, 0, 2)).reshape(D, DPROJ)
    wmo = wmo.reshape(DP + DG, D)
    x2, pooled, zb = _mix_fwd(x1, mv2, wmi, wpool, vecs, wcat, bias, wmo)
    (dx3, g3s, u3s, loss_blk, dgf), _ = _ffn_fwd(x2, mv3, win2, wout2.reshape(2, CH, D), head=(target, gf))
    loss = lax.psum(loss_blk[0, 0], ("x", "y", "c"))

    wo1, wo2 = wout1.reshape(2, CH, D), wout2.reshape(2, CH, D)
    (dx2, oin2, oout2, rin2, rout2, vec3), _ = _ffn_bwd(x2, dx3, g3s, u3s, mv3, win2, wo2)
    cs2 = [_chip_sum_pair(oin2, rin2), _chip_sum_pair(oout2, rout2)]
    (dx1, dwmi, dwmo, dwpool, dwsp, dbsp, v512, vec2), ex2 = _mix_bwd(
        x1, dx2, pooled, zb, mv2, wmi, wpool, vecs, wcat, wtcat, bias, wmo, side=_exchange_side(cs2))
    half2 = [_sum4(cs, e, chip) for cs, e in zip(cs2, ex2)]
    qcols = w_mix_in.shape[2]
    vmix = [jnp.transpose(dwmi.reshape(D, NQ, qcols), (1, 0, 2)).reshape(NQ, 2, D // 2, qcols),
            dwmo.reshape(NQ, 2, (DP + DG) // 8, D)]
    (grad_x, oin1, oout1, rin1, rout1, vec1), got = _ffn_bwd(
        xs, dx1, g1s, u1s, mv1, win1, wo1, side=_join(_swap_side(vmix), _share_side(half2)))
    sibmix, other2 = got[:2], got[2:]
    cs1 = [_chip_sum(g, r, core) for g, r in zip(vmix, sibmix)] + [_chip_sum_pair(oin1, rin1),
                                                                  _chip_sum_pair(oout1, rout1)]

    dmod = jnp.concatenate([vec1[0:3], vec2[0:3], vec3[0:3]], axis=0)
    grads = dict(
        b_ada=dmod.reshape(1, 9 * D), norm_ffn1_g=vec1[3:4], norm_mix_g=vec2[3:4], norm_ffn2_g=vec3[3:4],
        pool_scale=v512[0:1], gmlp_ln_g=v512[1:2], gmlp_ln_b=v512[2:3], b_spatial=dbsp[None],
        norm_final_g=dgf[0], w_pool=dwpool[None], w_spatial=dwsp[None])

    gp, layout = _pack({n: grads[n] for n in _SMALL})
    wp, _ = _pack({n: W[n] for n in _SMALL})
    mp, _ = _pack({n: M[n] for n in _SMALL})
    vp, _ = _pack({n: V[n] for n in _SMALL})
    r0, nb, _ = layout["b_ada"]
    assert r0 == 0
    out_g, out_d, out_m, out_v = {}, {}, {}, {}
    gs, dl, mo, vo, gath = _small_allreduce_adamw(gp, wp, mp, vp, nb // LANE)
    for packed, dst in ((gs, out_g), (dl, out_d), (mo, out_m), (vo, out_v)):
        for n, a in _unpack(packed, layout).items():
            dst[n] = a.reshape(W[n].shape)

    def update(n, own, recv, side=None):
        (g2, d, mn, vn), got = _adamw_halves(W[n][0], own, recv, M[n][0], V[n][0], side=side)
        out_g[n], out_d[n], out_m[n], out_v[n] = g2[None], d[None], mn[None], vn[None]
        return got

    ex_out1 = update("ffn2_w_in", half2[0], other2[0], side=_exchange_side(cs1[3:4]))
    ex_mix = update("ffn2_w_out", half2[1], other2[1], side=_exchange_side(cs1[0:2]))
    dmod_q = lax.dynamic_slice_in_dim(gath.reshape(N_DEV, nb), q * ncol, ncol, axis=1)
    (ga, da, ma, va), ex_in1 = _ada_grad_adamw(cact_all.T, dmod_q, w_ada[0], m_w_ada[0], v_w_ada[0],
                                              side=_exchange_side(cs1[2:3]))
    out_g["w_ada"], out_d["w_ada"], out_m["w_ada"], out_v["w_ada"] = ga[None], da[None], ma[None], va[None]
    ex1 = list(ex_mix) + list(ex_in1) + list(ex_out1)
    half1 = [_sum4(cs, e, chip) for cs, e in zip(cs1, ex1)]
    other1 = _run_side(_share_side(half1), "sibling_share")
    for n, own, recv in zip(["w_mix_in", "w_mix_out", "ffn1_w_in", "ffn1_w_out"], half1, other1):
        update(n, own, recv)

    return (loss, grad_x[None], *[out_g[n] for n in names], *[out_d[n] for n in names],
            *[out_m[n] for n in names], *[out_v[n] for n in names])
```

```python
import functools
import math

import jax
import jax.numpy as jnp
from jax import lax
from jax.experimental import pallas as pl
from jax.experimental.pallas import tpu as pltpu

F32 = jnp.float32
BF16 = jnp.bfloat16
MESH = pl.DeviceIdType.MESH
HIGHEST = lax.Precision.HIGHEST

EPS = 1e-6
D = 1024
DFF = 2816
CH = DFF // 2
NQ = 4
DP = 512
DG = 512
DPROJ = DP + 2 * DG
POOL_WINDOWS = (2, 4, 8, 16)
HALO = 16
CHUNK = 128
LANE = 128
N_DEV = 8

ADAM_LR = 0.001
ADAM_B1 = 0.9
ADAM_B2 = 0.999
ADAM_EPS = 1e-08
ADAM_WD = 0.01
ADAM_STEP = 10

VMEM_LIMIT = 62 * 1024 * 1024

TM_FFN_FWD = 512
TM_FFN_BWD = 512
TM_MIX = 256


def _call(body, **kw):
    return pl.pallas_call(body, interpret=False, **kw)


def _params(sem=None, vmem=None):
    return pltpu.CompilerParams(dimension_semantics=sem, vmem_limit_bytes=vmem)


def _sds(shape, dtype):
    return jax.ShapeDtypeStruct(shape, dtype)


ANY = pl.BlockSpec(memory_space=pl.ANY)
VMEM = pl.BlockSpec(memory_space=pltpu.VMEM)
SMEM = pl.BlockSpec(memory_space=pltpu.SMEM)


def _norm_mod(x, gn, sc, sh):
    r = lax.rsqrt(jnp.mean(x * x, axis=-1, keepdims=True) + EPS)
    xn = x * r
    hp = xn * gn
    return r, xn, hp, hp * (1.0 + sc) + sh


def _norm_mod_bwd(dh, r, xn, hp, gn, sc):
    one_sc = 1.0 + sc
    dsh = jnp.sum(dh, axis=0, keepdims=True)
    dsc = jnp.sum(dh * hp, axis=0, keepdims=True)
    dgn = jnp.sum(dh * one_sc * xn, axis=0, keepdims=True)
    dxn = dh * (gn * one_sc)
    dx = r * (dxn - xn * jnp.mean(dxn * xn, axis=-1, keepdims=True))
    return dsh, dsc, dgn, dx


def _dot(a, b):
    return jnp.dot(a, b, preferred_element_type=F32)


def _dot_nt(a, b):
    return lax.dot_general(a, b, (((1,), (1,)), ((), ())), preferred_element_type=F32)


def _dot_tn(a, b):
    return lax.dot_general(a, b, (((0,), (0,)), ((), ())), preferred_element_type=F32)


_GELU_C = math.sqrt(2.0 / math.pi)
_GELU_A = 0.044715


def _gelu_fwd_bwd(x):
    x2 = x * x
    t = jnp.tanh(_GELU_C * (x + _GELU_A * x * x2))
    g = 0.5 * x * (1.0 + t)
    dg = 0.5 * (1.0 + t) + 0.5 * x * (1.0 - t * t) * (_GELU_C * (1.0 + 3.0 * _GELU_A * x2))
    return g, dg


def _adamw(w, g, m, v):
    m = ADAM_B1 * m + (1.0 - ADAM_B1) * g
    v = ADAM_B2 * v + (1.0 - ADAM_B2) * (g * g)
    m_hat = m / (1.0 - ADAM_B1 ** ADAM_STEP)
    v_hat = v / (1.0 - ADAM_B2 ** ADAM_STEP)
    delta = -ADAM_LR * (m_hat / (jnp.sqrt(v_hat) + ADAM_EPS) + ADAM_WD * w)
    return delta, m, v


def _row_block(rows, cap=256, mult=16):
    best = None
    for t in range(mult, min(rows, cap) + 1, mult):
        if rows % t == 0:
            best = t
    assert best is not None, rows
    return best


def _head_math(x, target, gf):
    r = lax.rsqrt(jnp.mean(x * x, axis=-1, keepdims=True) + EPS)
    xn = x * r
    err = xn * gf - target
    dy = err * (1.0 / D)
    dxn = dy * gf
    dx = r * (dxn - xn * jnp.mean(dxn * xn, axis=-1, keepdims=True))
    return (0.5 / D) * jnp.sum(err * err), jnp.sum(dy * xn, axis=0, keepdims=True), dx


def _ffn_fwd(x, modv, win, wout, side=None, head=None):
    S = x.shape[0]
    tm = TM_FFN_FWD
    nt = S // tm

    def body(*refs):
        if head is None:
            x_ref, mod_ref, wg_ref, wu_ref, wo_ref, xo_ref, gs_ref, us_ref, acc_scr = refs
        else:
            (x_ref, mod_ref, wg_ref, wu_ref, wo_ref, t_ref, gf_ref,
             xo_ref, gs_ref, us_ref, loss_ref, dgf_ref, acc_scr) = refs

        @pl.when((pl.program_id(0) == 0) & (pl.program_id(1) == 0))
        def _():
            acc_scr[...] = jnp.zeros_like(acc_scr)
            if head is not None:
                loss_ref[...] = jnp.zeros_like(loss_ref)
                dgf_ref[...] = jnp.zeros_like(dgf_ref)

        j = pl.program_id(1)
        h = _norm_mod(x_ref[...], mod_ref[3:4, :], mod_ref[1:2, :], mod_ref[0:1, :])[3].astype(BF16)
        g = _dot(h, wg_ref[...]).astype(BF16)
        u = _dot(h, wu_ref[...]).astype(BF16)
        gs_ref[...] = g
        us_ref[...] = u
        gf = g.astype(F32)
        a = (gf * jax.nn.sigmoid(gf) * u.astype(F32)).astype(BF16)
        acc = jnp.where(j == 0, 0.0, acc_scr[...]) + _dot(a, wo_ref[...])
        acc_scr[...] = acc
        xo = x_ref[...] + (0.5 * mod_ref[2:3, :]) * acc
        if head is None:
            xo_ref[...] = xo
        else:
            @pl.when(j == 1)
            def _():
                loss, dgf, dx = _head_math(xo, t_ref[...], gf_ref[0:1, :])
                loss_ref[...] += loss
                dgf_ref[0:1, :] += dgf
                xo_ref[...] = dx

    step = lambda i, j: lambda: (pl.program_id(0) == i) & (pl.program_id(1) == j)
    tile = pl.BlockSpec((tm, D), lambda i, j: (i, 0))
    const = lambda shape: pl.BlockSpec(shape, lambda i, j: (0, 0))
    chunk = pl.BlockSpec((tm, CH), lambda i, j: (i, j))
    in_specs = [tile, const((8, D)), pl.BlockSpec((None, D, CH), lambda i, j: (j, 0, 0)),
                pl.BlockSpec((None, D, CH), lambda i, j: (2 + j, 0, 0)), pl.BlockSpec((None, CH, D), lambda i, j: (j, 0, 0))]
    out_specs = [tile, chunk, chunk]
    out_shape = [_sds((S, D), F32), _sds((S, DFF), BF16), _sds((S, DFF), BF16)]
    args = (x, modv, win, win, wout)
    if head is not None:
        in_specs += [tile, const((8, D))]
        out_specs += [const((8, LANE)), const((8, D))]
        out_shape += [_sds((8, LANE), F32), _sds((8, D), F32)]
        args += tuple(head)
    return _side_call(
        body, side, (step(0, 0), step((7 * nt) // 10, 0), step(nt - 1, 1)), name="ffn_fwd",
        grid=(nt, 2), in_specs=in_specs, out_specs=out_specs, out_shape=out_shape,
        scratch_shapes=[pltpu.VMEM((tm, D), F32)],
        compiler_params=_params(("arbitrary", "arbitrary"), VMEM_LIMIT),
        args=args)


def _ffn_bwd_pass(jj, x, dxo, gs, us, modv, win, wout, prev=None, side=None):
    S = x.shape[0]
    tm = TM_FFN_BWD
    nsub = tm // 256
    nt = S // tm
    hi, ho = D // 2, CH // 4
    last = prev is not None
    assert last == (jj == 1)

    def body(*refs):
        x_ref, dxo_ref, gs_ref, us_ref, mod_ref, wg_ref, wu_ref, wo_ref = refs[:8]
        k = 13 if last else 8
        out_ref, dwin_ref, dwout_ref, rwin_ref, rwout_ref, vec_ref = refs[k:k + 6]
        accg, accu, accw, sems, fsend, frecv = refs[k + 6:]
        i = pl.program_id(0)

        @pl.when(i == 0)
        def _():
            accg[...] = jnp.zeros_like(accg)
            accu[...] = jnp.zeros_like(accu)
            accw[...] = jnp.zeros_like(accw)
            vec_ref[...] = jnp.zeros_like(vec_ref)

        gn, sc, sh, gate = mod_ref[3:4, :], mod_ref[1:2, :], mod_ref[0:1, :], mod_ref[2:3, :]

        parts = []
        for s in range(nsub):
            rs = slice(s * (tm // nsub), (s + 1) * (tm // nsub))
            r, xn, hp, h = _norm_mod(x_ref[rs, :], gn, sc, sh)
            dxo = dxo_ref[rs, :]
            dy = (dxo * (0.5 * gate)).astype(BF16)
            g = gs_ref[rs, :].astype(F32)
            u = us_ref[rs, :].astype(F32)
            sig = jax.nn.sigmoid(g)
            sl = g * sig
            a = (sl * u).astype(BF16)
            da = _dot_nt(dy, wo_ref[...])
            dg = (da * u * (sig * (1.0 + g * (1.0 - sig)))).astype(BF16)
            du = (da * sl).astype(BF16)
            dhp = _dot_nt(dg, wg_ref[...]) + _dot_nt(du, wu_ref[...])
            parts.append((h.astype(BF16), a, dg, du, dxo.astype(BF16)))
            if last:
                dsh, dsc, dgn, dxin = _norm_mod_bwd(refs[8][rs, :] + dhp, r, xn, hp, gn, sc)
                vec_ref[0:1, :] += dsh
                vec_ref[1:2, :] += dsc
                vec_ref[3:4, :] += dgn
                out_ref[rs, :] = dxo + dxin
            else:
                out_ref[rs, :] = dhp

        hb, a, dg, du, dxb = [jnp.concatenate(p, axis=0) if nsub > 1 else p[0] for p in zip(*parts)]
        accw[...] += _dot_tn(a, dxb)
        accg[...] += _dot_tn(hb, dg)
        accu[...] += _dot_tn(hb, du)

        @pl.when(i == nt - 1)
        def _():
            gw = accw[...]
            vec_ref[2:3, :] += 0.5 * jnp.sum(wo_ref[...].astype(F32) * gw, axis=0, keepdims=True)
            accw[...] = gw * (0.5 * gate)
            mx, my, cc = _me()
            rows = lambda base, n, c: pl.ds(base + c * n, n)
            pieces = [(accg, 0, hi, dwin_ref, rwin_ref, jj), (accu, 0, hi, dwin_ref, rwin_ref, 2 + jj),
                      (accw, 0, ho, dwout_ref, rwout_ref, 2 * jj), (accw, 2 * ho, ho, dwout_ref, rwout_ref, 2 * jj + 1)]
            loc = [pltpu.make_async_copy(acc.at[rows(base, n, cc)], own.at[slot], sems.at[p])
                   for p, (acc, base, n, own, _, slot) in enumerate(pieces)]
            rem = [pltpu.make_async_remote_copy(acc.at[rows(base, n, 1 - cc)], sib.at[slot], fsend.at[p], frecv.at[p],
                                                device_id=(mx, my, 1 - cc), device_id_type=MESH)
                   for p, (acc, base, n, _, sib, slot) in enumerate(pieces)]
            for cp in loc + rem:
                cp.start()
            for cp in loc:
                cp.wait()
            for cp in rem:
                cp.wait()

    once = pl.Buffered(1)
    tile = pl.BlockSpec((tm, D), lambda i: (i, 0))
    chunk = pl.BlockSpec((tm, CH), lambda i: (i, jj))
    in_specs = [tile, tile, chunk, chunk, pl.BlockSpec((8, D), lambda i: (0, 0)),
                pl.BlockSpec((None, D, CH), lambda i: (jj, 0, 0), pipeline_mode=once),
                pl.BlockSpec((None, D, CH), lambda i: (2 + jj, 0, 0), pipeline_mode=once),
                pl.BlockSpec((None, CH, D), lambda i: (jj, 0, 0), pipeline_mode=once)]
    args = (x, dxo, gs, us, modv, win, win, wout)
    if last:
        in_specs += [tile, ANY, ANY, ANY, ANY]
        args += tuple(prev)
    step = lambda s: lambda: pl.program_id(0) == s
    return _side_call(
        body, side, (step(0), None, step(nt - 1)), name="ffn_bwd",
        grid=(nt,), in_specs=in_specs,
        out_specs=[tile, ANY, ANY, ANY, ANY, pl.BlockSpec((8, D), lambda i: (0, 0))],
        out_shape=[_sds((S, D), F32), _sds((NQ, hi, CH), F32), _sds((NQ, ho, D), F32), _sds((NQ, hi, CH), F32),
                   _sds((NQ, ho, D), F32), _sds((8, D), F32)],
        scratch_shapes=[pltpu.VMEM((D, CH), F32), pltpu.VMEM((D, CH), F32), pltpu.VMEM((CH, D), F32),
                        pltpu.SemaphoreType.DMA((4,)), pltpu.SemaphoreType.DMA((4,)), pltpu.SemaphoreType.DMA((4,))],
        aliases={9 + p: 1 + p for p in range(4)} if last else {},
        compiler_params=_params(("arbitrary",), VMEM_LIMIT),
        args=args)


def _ffn_bwd(x, dxo, gs, us, modv, win, wout, side=None):
    first, extra = _ffn_bwd_pass(0, x, dxo, gs, us, modv, win, wout, side=side)
    (dx, dwin, dwout, rwin, rwout, vec), _ = _ffn_bwd_pass(1, x, dxo, gs, us, modv, win, wout, prev=first[:5])
    return (dx, dwin, dwout, rwin, rwout, first[5] + vec), extra


def _prep_spatial(w_spatial, b_spatial_t):
    def body(w_ref, b_ref, wcat_ref, wtcat_ref, bias_ref):
        row = lax.broadcasted_iota(jnp.int32, (CHUNK, CHUNK), 0)
        col = lax.broadcasted_iota(jnp.int32, (CHUNK, CHUNK), 1)
        tril = col <= row
        for p in range(4):
            wa = jnp.where(tril, w_ref[2 * p], 0.0)
            wb = jnp.where(tril, w_ref[2 * p + 1], 0.0)
            wcat_ref[p] = jnp.concatenate([wa, wb], axis=1).astype(BF16)
            wtcat_ref[p] = jnp.concatenate([wa.T, wb.T], axis=1).astype(BF16)
        head = lax.broadcasted_iota(jnp.int32, (8, DG), 0)
        ch = lax.broadcasted_iota(jnp.int32, (8, DG), 1)
        spread = jnp.where(ch // 64 == head, 1.0, 0.0).astype(F32)
        bias_ref[...] = jnp.dot(b_ref[...], spread, precision=HIGHEST, preferred_element_type=F32)

    return _call(
        body, name="prep_spatial",
        in_specs=[VMEM, VMEM], out_specs=[VMEM, VMEM, VMEM],
        out_shape=[_sds((4, CHUNK, 2 * CHUNK), BF16), _sds((4, CHUNK, 2 * CHUNK), BF16), _sds((CHUNK, DG), F32)],
    )(w_spatial, b_spatial_t)


def _pair_rhs(blocks):
    lane = lax.broadcasted_iota(jnp.int32, (CHUNK, LANE), 1)
    lo = lane < 64
    top = jnp.concatenate([jnp.where(lo, b, 0.0) for b in blocks], axis=1)
    bot = jnp.concatenate([jnp.where(lo, 0.0, b) for b in blocks], axis=1)
    return top, bot


def _gmlp_branch(zb, vecs, wcat_ref, bias_ref, nchunks):
    z, dz = _gelu_fwd_bwd(zb)
    u = z[:, :DG]
    v = z[:, DG:]
    ln_g, ln_b = vecs[1:2, :], vecs[2:3, :]
    mu = jnp.mean(v, axis=-1, keepdims=True)
    vc = v - mu
    rstd = lax.rsqrt(jnp.mean(vc * vc, axis=-1, keepdims=True) + EPS)
    vhat = vc * rstd
    vl = vhat * ln_g + ln_b
    sv_cols = []
    for p in range(4):
        blocks = [vl[k * CHUNK:(k + 1) * CHUNK, p * LANE:(p + 1) * LANE] for k in range(nchunks)]
        top, bot = _pair_rhs(blocks)
        rhs = jnp.concatenate([top, bot], axis=0).astype(BF16)
        out = _dot(wcat_ref[p], rhs)
        bias = bias_ref[:, p * LANE:(p + 1) * LANE]
        sv_cols.append(jnp.concatenate([out[:, k * LANE:(k + 1) * LANE] + bias for k in range(nchunks)], axis=0))
    sv = jnp.concatenate(sv_cols, axis=1)
    return dict(u=u, dz=dz, rstd=rstd, vhat=vhat, vl=vl, sv=sv, yb=u * sv)


def _mix_fwd(x, modv, win, wpool, vecs, wcat, bias, wout):
    S = x.shape[0]
    tm = TM_MIX
    nt = S // tm
    nchunks = tm // CHUNK

    def body(x_ref, mod_ref, win_ref, wpool_ref, vec_ref, wcat_ref, bias_ref, wout_ref,
             xo_ref, pooled_ref, zb_ref, ext):
        i = pl.program_id(0)

        @pl.when(i == 0)
        def _():
            ext[0:HALO, :] = jnp.zeros((HALO, DP), F32)

        x = x_ref[...]
        _, _, _, h = _norm_mod(x, mod_ref[3:4, :], mod_ref[1:2, :], mod_ref[0:1, :])
        proj = _dot(h.astype(BF16), win_ref[...])
        xa = proj[:, :DP]
        zb = proj[:, DP:]
        zb_ref[...] = zb
        ext[HALO:HALO + tm, :] = xa
        pos = i * tm + lax.broadcasted_iota(jnp.int32, (tm, 1), 0)
        vecs = vec_ref[...]
        ya_cols = []
        pooled_cols = []
        for gi, w in enumerate(POOL_WINDOWS):
            cols = slice(gi * LANE, (gi + 1) * LANE)
            s = xa[:, cols]
            for k in range(1, w):
                s = s + ext[HALO - k:HALO - k + tm, cols]
            cnt = jnp.minimum(pos + 1, w).astype(F32)
            pooled = (s / cnt - xa[:, cols]).astype(BF16)
            pooled_cols.append(pooled)
            ya_cols.append(_dot(pooled, wpool_ref[gi]) * vecs[0:1, cols])
        pooled_ref[...] = jnp.concatenate(pooled_cols, axis=1)
        ext[0:HALO, :] = ext[tm:tm + HALO, :]

        gm = _gmlp_branch(zb, vecs, wcat_ref, bias_ref, nchunks)
        cat = jnp.concatenate(ya_cols + [gm["yb"]], axis=1).astype(BF16)
        xo_ref[...] = x + mod_ref[2:3, :] * _dot(cat, wout_ref[...])

    full = lambda shape: pl.BlockSpec(shape, lambda i: (0,) * len(shape))
    return _call(
        body, name="mix_fwd",
        grid=(nt,),
        in_specs=[pl.BlockSpec((tm, D), lambda i: (i, 0)), full((8, D)), full((D, DPROJ)),
                  full((4, LANE, LANE)), full((8, DP)), full((4, CHUNK, 2 * CHUNK)), full((CHUNK, DG)),
                  full((DP + DG, D))],
        out_specs=[pl.BlockSpec((tm, D), lambda i: (i, 0)), pl.BlockSpec((tm, DP), lambda i: (i, 0)),
                   pl.BlockSpec((tm, 2 * DG), lambda i: (i, 0))],
        out_shape=[_sds((S, D), F32), _sds((S, DP), BF16), _sds((S, 2 * DG), F32)],
        scratch_shapes=[pltpu.VMEM((tm + HALO, DP), F32)],
        compiler_params=_params(("arbitrary",), VMEM_LIMIT),
    )(x, modv, win, wpool, vecs, wcat, bias, wout)


def _mix_bwd(x, dxo, pooled, zb, modv, win, wpool, vecs, wcat, wtcat, bias, wout, side=None):
    S = x.shape[0]
    tm = TM_MIX
    nt = S // tm
    nchunks = tm // CHUNK

    def body(x_ref, dxo_ref, pooled_ref, zb_ref, mod_ref, win_ref, wpool_ref, vec_ref, wcat_ref, wtcat_ref,
             bias_ref, wout_ref,
             dx_ref, dwin_ref, dwout_ref, dwpool_ref, dwsp_ref, dbsp_ref, v512_ref, vd_ref, qext, dsv_acc):
        step = pl.program_id(0)
        tile = nt - 1 - step

        @pl.when(step == 0)
        def _():
            dwin_ref[...] = jnp.zeros_like(dwin_ref)
            dwout_ref[...] = jnp.zeros_like(dwout_ref)
            dwpool_ref[...] = jnp.zeros_like(dwpool_ref)
            dwsp_ref[...] = jnp.zeros_like(dwsp_ref)
            v512_ref[...] = jnp.zeros_like(v512_ref)
            vd_ref[...] = jnp.zeros_like(vd_ref)
            dsv_acc[...] = jnp.zeros_like(dsv_acc)
            qext[tm:tm + HALO, :] = jnp.zeros((HALO, DP), F32)

        gn, sc, sh, gate = mod_ref[3:4, :], mod_ref[1:2, :], mod_ref[0:1, :], mod_ref[2:3, :]
        vecs = vec_ref[...]
        x = x_ref[...]
        r, xn, hp, h = _norm_mod(x, gn, sc, sh)
        hb = h.astype(BF16)
        dxo = dxo_ref[...]

        pooled = pooled_ref[...]
        mixed_cols = [_dot(pooled[:, gi * LANE:(gi + 1) * LANE], wpool_ref[gi]) for gi in range(4)]
        mixed = jnp.concatenate(mixed_cols, axis=1)
        scale = vecs[0:1, :]
        gm = _gmlp_branch(zb_ref[...], vecs, wcat_ref, bias_ref, nchunks)
        cat = jnp.concatenate([mixed * scale, gm["yb"]], axis=1).astype(BF16)

        dwout_ref[...] += _dot_tn(cat, dxo.astype(BF16))
        dcat = _dot_nt((dxo * gate).astype(BF16), wout_ref[...])
        dya = dcat[:, :DP]
        dyb = dcat[:, DP:]

        v512_ref[0:1, :] += jnp.sum(dya * mixed, axis=0, keepdims=True)
        dmixed = (dya * scale).astype(BF16)
        pos = tile * tm + lax.broadcasted_iota(jnp.int32, (tm, 1), 0)
        dpooled_cols = []
        for gi, w in enumerate(POOL_WINDOWS):
            cols = slice(gi * LANE, (gi + 1) * LANE)
            dp = _dot_nt(dmixed[:, cols], wpool_ref[gi])
            dwpool_ref[gi] += _dot_tn(pooled[:, cols], dmixed[:, cols])
            cnt = jnp.minimum(pos + 1, w).astype(F32)
            qext[0:tm, cols] = dp / cnt
            dpooled_cols.append(dp)
        dxa_cols = []
        for gi, w in enumerate(POOL_WINDOWS):
            cols = slice(gi * LANE, (gi + 1) * LANE)
            s = qext[0:tm, cols]
            for k in range(1, w):
                s = s + qext[k:k + tm, cols]
            dxa_cols.append(s - dpooled_cols[gi])
        qext[tm:tm + HALO, :] = qext[0:HALO, :]

        u, sv, vl = gm["u"], gm["sv"], gm["vl"]
        du = dyb * sv
        dsv = dyb * u
        dvl_cols = []
        for p in range(4):
            cols = slice(p * LANE, (p + 1) * LANE)
            dblocks = [dsv[k * CHUNK:(k + 1) * CHUNK, cols] for k in range(nchunks)]
            vblocks = [vl[k * CHUNK:(k + 1) * CHUNK, cols] for k in range(nchunks)]
            tot = dblocks[0]
            for b in dblocks[1:]:
                tot = tot + b
            dsv_acc[:, cols] += tot
            top, bot = _pair_rhs(dblocks)
            out = _dot(wtcat_ref[p], jnp.concatenate([top, bot], axis=0).astype(BF16))
            dvl_cols.append(jnp.concatenate([out[:, k * LANE:(k + 1) * LANE] for k in range(nchunks)], axis=0))
            vcat = jnp.concatenate(vblocks, axis=1).astype(BF16)
            dwsp_ref[2 * p] += _dot_nt(top.astype(BF16), vcat)
            dwsp_ref[2 * p + 1] += _dot_nt(bot.astype(BF16), vcat)
        dvl = jnp.concatenate(dvl_cols, axis=1)
        vhat, rstd = gm["vhat"], gm["rstd"]
        v512_ref[1:2, :] += jnp.sum(dvl * vhat, axis=0, keepdims=True)
        v512_ref[2:3, :] += jnp.sum(dvl, axis=0, keepdims=True)
        dvh = dvl * vecs[1:2, :]
        dv = rstd * (dvh - jnp.mean(dvh, axis=-1, keepdims=True)
                     - vhat * jnp.mean(dvh * vhat, axis=-1, keepdims=True))
        dzb = jnp.concatenate([du, dv], axis=1) * gm["dz"]

        dproj = jnp.concatenate(dxa_cols + [dzb], axis=1).astype(BF16)
        dwin_ref[...] += _dot_tn(hb, dproj)
        dh = _dot_nt(dproj, win_ref[...])
        dsh, dsc, dgn, dxin = _norm_mod_bwd(dh, r, xn, hp, gn, sc)
        vd_ref[0:1, :] += dsh
        vd_ref[1:2, :] += dsc
        vd_ref[3:4, :] += dgn
        dx_ref[...] = dxo + dxin

        @pl.when(step == nt - 1)
        def _():
            gw = dwout_ref[...]
            vd_ref[2:3, :] += jnp.sum(wout_ref[...].astype(F32) * gw, axis=0, keepdims=True)
            dwout_ref[...] = gw * gate
            row = lax.broadcasted_iota(jnp.int32, (CHUNK, CHUNK), 0)
            col = lax.broadcasted_iota(jnp.int32, (CHUNK, CHUNK), 1)
            for hh in range(8):
                dwsp_ref[hh] = jnp.where(col <= row, dwsp_ref[hh], 0.0)
            head = lax.broadcasted_iota(jnp.int32, (8, DG), 0)
            ch = lax.broadcasted_iota(jnp.int32, (8, DG), 1)
            spread = jnp.where(ch // 64 == head, 1.0, 0.0).astype(F32)
            dbsp_ref[...] = lax.dot_general(spread, dsv_acc[...], (((1,), (1,)), ((), ())),
                                            precision=HIGHEST, preferred_element_type=F32)

    full = lambda shape: pl.BlockSpec(shape, lambda s: (0,) * len(shape))
    rev = lambda cols: pl.BlockSpec((tm, cols), lambda s: (nt - 1 - s, 0))
    step = lambda s: lambda: pl.program_id(0) == s
    return _side_call(
        body, side, (step(0), None, step(nt - 1)), name="mix_bwd",
        grid=(nt,),
        in_specs=[rev(D), rev(D), rev(DP), rev(2 * DG), full((8, D)), full((D, DPROJ)), full((4, LANE, LANE)),
                  full((8, DP)), full((4, CHUNK, 2 * CHUNK)), full((4, CHUNK, 2 * CHUNK)), full((CHUNK, DG)),
                  full((DP + DG, D))],
        out_specs=[rev(D), full((D, DPROJ)), full((DP + DG, D)), full((4, LANE, LANE)), full((8, CHUNK, CHUNK)),
                   full((8, CHUNK)), full((8, DP)), full((8, D))],
        out_shape=[_sds((S, D), F32), _sds((D, DPROJ), F32), _sds((DP + DG, D), F32), _sds((4, LANE, LANE), F32),
                   _sds((8, CHUNK, CHUNK), F32), _sds((8, CHUNK), F32), _sds((8, DP), F32), _sds((8, D), F32)],
        scratch_shapes=[pltpu.VMEM((tm + HALO, DP), F32), pltpu.VMEM((CHUNK, DG), F32)],
        compiler_params=_params(("arbitrary",), VMEM_LIMIT),
        args=(x, dxo, pooled, zb, modv, win, wpool, vecs, wcat, wtcat, bias, wout))


def _chip_sum(g, rbuf, core):
    _, _, hr, cols = g.shape
    tr = _row_block(hr)

    def body(c_ref, g_ref, r_ref, o_ref):
        o_ref[...] = (g_ref[...] + r_ref[...]).astype(BF16)

    return pl.pallas_call(
        body, name="chip_sum", interpret=False,
        grid_spec=pltpu.PrefetchScalarGridSpec(
            num_scalar_prefetch=1, grid=(NQ, hr // tr),
            in_specs=[pl.BlockSpec((None, None, tr, cols), lambda q, i, c: (q, c[0], i, 0)),
                      pl.BlockSpec((None, tr, cols), lambda q, i, c: (q, i, 0))],
            out_specs=pl.BlockSpec((None, tr, cols), lambda q, i, c: (q, i, 0))),
        out_shape=_sds((NQ, hr, cols), BF16),
        compiler_params=_params(("arbitrary", "arbitrary"), None),
    )(core, g, rbuf)


def _chip_sum_pair(own, rbuf):
    _, hr, cols = own.shape
    tr = _row_block(hr)

    def body(a_ref, b_ref, o_ref):
        o_ref[...] = (a_ref[...] + b_ref[...]).astype(BF16)

    spec = pl.BlockSpec((None, tr, cols), lambda q, i: (q, i, 0))
    return _call(
        body, name="chip_sum_pair",
        grid=(NQ, hr // tr),
        in_specs=[spec, spec], out_specs=spec,
        out_shape=_sds((NQ, hr, cols), BF16),
        compiler_params=_params(("arbitrary", "arbitrary"), None),
    )(own, rbuf)


def _sum4(cs, rbuf, chip):
    _, hr, cols = rbuf.shape
    tr = _row_block(hr)

    def body(q_ref, c_ref, r1_ref, r2_ref, r3_ref, o_ref):
        acc = c_ref[...].astype(F32)
        for r in (r1_ref, r2_ref, r3_ref):
            acc = acc + r[...].astype(F32)
        o_ref[...] = acc

    slot = lambda k: pl.BlockSpec((None, tr, cols), lambda i, q: ((q[0] + k) % NQ, i, 0))
    return pl.pallas_call(
        body, name="sum4", interpret=False,
        grid_spec=pltpu.PrefetchScalarGridSpec(
            num_scalar_prefetch=1, grid=(hr // tr,),
            in_specs=[slot(0), slot(1), slot(2), slot(3)],
            out_specs=pl.BlockSpec((tr, cols), lambda i, q: (i, 0))),
        out_shape=_sds((hr, cols), F32),
        compiler_params=_params(("arbitrary",), None),
    )(chip, cs, rbuf, rbuf, rbuf)


def _adamw_halves(w, own, recv, m, v, side=None):
    rows, cols = w.shape
    hr = rows // 2
    tr = _row_block(hr, mult=8)
    nb = hr // tr

    def body(w_ref, own_ref, recv_ref, m_ref, v_ref, g_ref, d_ref, mo_ref, vo_ref):
        g = jnp.where(pl.program_id(0) == lax.axis_index("c"), own_ref[...], recv_ref[...])
        d, mn, vn = _adamw(w_ref[...], g, m_ref[...], v_ref[...])
        g_ref[...] = g
        d_ref[...] = d
        mo_ref[...] = mn
        vo_ref[...] = vn

    full = pl.BlockSpec((tr, cols), lambda h, i: (h * nb + i, 0))
    half = pl.BlockSpec((tr, cols), lambda h, i: (i, 0))
    step = lambda h, i: lambda: (pl.program_id(0) == h) & (pl.program_id(1) == i)
    return _side_call(
        body, side, (step(0, 0), None, step(1, nb - 1)), name="adamw_halves",
        grid=(2, nb), in_specs=[full, half, half, full, full], out_specs=[full] * 4,
        out_shape=[_sds((rows, cols), F32)] * 4, scratch_shapes=[],
        compiler_params=_params(("arbitrary", "arbitrary"), None),
        args=(w, own, recv, m, v))


def _cast_place(w, chip):
    rows, cols = w.shape
    tr = _row_block(rows)

    def body(q_ref, w_ref, o_ref):
        o_ref[...] = w_ref[...].astype(BF16)

    return pl.pallas_call(
        body, name="cast_place", interpret=False,
        grid_spec=pltpu.PrefetchScalarGridSpec(
            num_scalar_prefetch=1, grid=(rows // tr,),
            in_specs=[pl.BlockSpec((tr, cols), lambda i, q: (i, 0))],
            out_specs=pl.BlockSpec((None, tr, cols), lambda i, q: (q[0], i, 0))),
        out_shape=_sds((NQ, rows, cols), BF16),
        compiler_params=_params(("arbitrary",), None),
    )(chip, w)


def _ada_grad_adamw(cact_t, dmod_q, w, m, v, side=None):
    rows, cols = w.shape
    tc = 256
    assert cols % tc == 0

    def body(c_ref, d_ref, w_ref, m_ref, v_ref, g_ref, dl_ref, mo_ref, vo_ref):
        g = jnp.dot(c_ref[...], d_ref[...], precision=HIGHEST, preferred_element_type=F32)
        d, mn, vn = _adamw(w_ref[...], g, m_ref[...], v_ref[...])
        g_ref[...] = g
        dl_ref[...] = d
        mo_ref[...] = mn
        vo_ref[...] = vn

    spec = pl.BlockSpec((rows, tc), lambda i: (0, i))
    step = lambda s: lambda: pl.program_id(0) == s
    return _side_call(
        body, side, (step(0), None, step(cols // tc - 1)), name="ada_grad_adamw",
        grid=(cols // tc,),
        in_specs=[pl.BlockSpec((rows, 8), lambda i: (0, 0)), pl.BlockSpec((8, tc), lambda i: (0, i)),
                  spec, spec, spec],
        out_specs=[spec] * 4,
        out_shape=[_sds((rows, cols), F32)] * 4,
        scratch_shapes=[],
        compiler_params=_params(("arbitrary",), None),
        args=(cact_t, dmod_q, w, m, v))


def _me():
    x, y, c = lax.axis_index("x"), lax.axis_index("y"), lax.axis_index("c")
    return x, y, c


_OFFSETS7 = [(dx, dy, dc) for dx in (0, 1) for dy in (0, 1) for dc in (0, 1) if (dx, dy, dc) != (0, 0, 0)]
_CHIP_OFFSETS = [(1, 0), (0, 1), (1, 1)]


def _ada_fwd(c, w_ada_q, b_ada_q, side=None):
    ncol = w_ada_q.shape[1]

    def body(c_ref, w_ref, b_ref, cact_ref, modsel_ref, blk, gath, res, parts, send_sems, recv_sems):
        x, y, cc = _me()
        me = 4 * x + 2 * y + cc
        q = 2 * x + y
        cv = c_ref[...]
        ca = cv * jax.nn.sigmoid(cv)
        row = lax.broadcasted_iota(jnp.int32, (8, D), 0)
        blk[...] = jnp.where(row == me, jnp.broadcast_to(ca, (8, D)), 0.0)
        gath[me] = blk[...]
        sends = []
        for k, (dx, dy, dc) in enumerate(_OFFSETS7):
            cp = pltpu.make_async_remote_copy(blk, gath.at[me], send_sems.at[k], recv_sems.at[k],
                                              device_id=(x ^ dx, y ^ dy, cc ^ dc), device_id_type=MESH)
            cp.start()
            sends.append(cp)
        for cp in sends:
            cp.wait_recv()
        cact = gath[0]
        for d in range(1, N_DEV):
            cact = cact + gath[d]
        cact_ref[...] = cact
        res[...] = jnp.dot(cact, w_ref[...], precision=HIGHEST, preferred_element_type=F32) + b_ref[...]
        parts[q] = res[...]
        sends2 = []
        for k, (dx, dy) in enumerate(_CHIP_OFFSETS):
            cp = pltpu.make_async_remote_copy(res, parts.at[q], send_sems.at[7 + k], recv_sems.at[7 + k],
                                              device_id=(x ^ dx, y ^ dy, cc), device_id_type=MESH)
            cp.start()
            sends2.append(cp)
        for cp in sends2:
            cp.wait_recv()
        row2 = lax.broadcasted_iota(jnp.int32, (8, ncol), 0)
        out = jnp.zeros((8, ncol), F32)
        for s in range(NQ):
            mine = jnp.sum(jnp.where(row2 == me, parts[s], 0.0), axis=0, keepdims=True)
            out = out + jnp.where(row2 == s, jnp.broadcast_to(mine, (8, ncol)), 0.0)
        modsel_ref[...] = out
        for cp in sends + sends2:
            cp.wait_send()

    return _side_call(
        body, side, None, name="ada_fwd",
        in_specs=[VMEM, VMEM, VMEM], out_specs=[VMEM, VMEM],
        out_shape=[_sds((8, D), F32), _sds((8, ncol), F32)],
        scratch_shapes=[pltpu.VMEM((8, D), F32), pltpu.VMEM((N_DEV, 8, D), F32), pltpu.VMEM((8, ncol), F32),
                        pltpu.VMEM((NQ, 8, ncol), F32), pltpu.SemaphoreType.DMA((10,)), pltpu.SemaphoreType.DMA((10,))],
        compiler_params=_params(None, VMEM_LIMIT),
        args=(c, w_ada_q, b_ada_q))


class _Side:
    def __init__(self, ins, out_shapes, aliases, nsem, start, mid=None, finish=None):
        self.ins, self.out_shapes, self.aliases, self.nsem = list(ins), list(out_shapes), dict(aliases), nsem
        self.start, self.mid, self.finish = start, mid, finish


def _join(*sides):
    ins, outs, aliases, offs, nsem = [], [], {}, [], 0
    for s in sides:
        offs.append((len(ins), len(outs), nsem))
        aliases.update({len(ins) + a: len(outs) + b for a, b in s.aliases.items()})
        ins += s.ins
        outs += s.out_shapes
        nsem += s.nsem

    def hook(name):
        def run(i, o, ss, rs, base):
            for s, (io, oo, so) in zip(sides, offs):
                fn = getattr(s, name)
                if fn is not None:
                    fn(i[io:io + len(s.ins)], o[oo:oo + len(s.out_shapes)], ss, rs, base + so)
        return run

    return _Side(ins, outs, aliases, nsem, hook("start"), hook("mid"), hook("finish"))


def _side_call(body, side, when, *, name, in_specs, out_specs, out_shape, scratch_shapes, args, aliases=None, **kw):
    n_in, n_out = len(in_specs), len(out_specs)
    aliases = dict(aliases or {})
    if side is None:
        return _call(body, name=name, in_specs=in_specs, out_specs=out_specs, out_shape=out_shape,
                     scratch_shapes=scratch_shapes, input_output_aliases=aliases, **kw)(*args), []
    ns_in, ns_out = len(side.ins), len(side.out_shapes)

    def hook(fn, k, operands):
        if fn is None:
            return
        if when is None:
            fn(*operands, 0)
        elif when[k] is not None:
            pl.when(when[k]())(functools.partial(fn, *operands, 0))

    def wrapped(*refs):
        ins, s_ins = refs[:n_in], refs[n_in:n_in + ns_in]
        o0 = n_in + ns_in
        outs, s_outs = refs[o0:o0 + n_out], refs[o0 + n_out:o0 + n_out + ns_out]
        rest = refs[o0 + n_out + ns_out:]
        scratch, operands = rest[:-2], (s_ins, s_outs, rest[-2], rest[-1])
        hook(side.start, 0, operands)
        body(*ins, *outs, *scratch)
        hook(side.mid, 1, operands)
        hook(side.finish, 2, operands)

    res = _call(
        wrapped, name=name,
        in_specs=list(in_specs) + [ANY] * ns_in, out_specs=list(out_specs) + [ANY] * ns_out,
        out_shape=list(out_shape) + side.out_shapes,
        scratch_shapes=list(scratch_shapes) + [pltpu.SemaphoreType.DMA((side.nsem,)),
                                               pltpu.SemaphoreType.DMA((side.nsem,))],
        input_output_aliases={**aliases, **{n_in + a: n_out + b for a, b in side.aliases.items()}},
        **kw)(*args, *side.ins)
    return res[:n_out], res[n_out:]


def _run_side(side, name):
    return _side_call(lambda: None, side, None, name=name, in_specs=[], out_specs=[], out_shape=[],
                      scratch_shapes=[], args=[])[1]


def _remote(src, dst, ss, rs, k, dev):
    return pltpu.make_async_remote_copy(src, dst, ss.at[k], rs.at[k], device_id=dev, device_id_type=MESH)


def _gather_side(bufs):
    n = len(bufs)

    def walk(outs, half):
        x, y, cc = _me()
        for w in range(n):
            hr = outs[w].shape[1] // 2
            rows = pl.ds((cc if half == "mine" else 1 - cc) * hr, hr)
            for j, (dx, dy) in enumerate(_CHIP_OFFSETS):
                yield w, j, (x ^ dx, y ^ dy, cc), outs[w].at[2 * (x ^ dx) + (y ^ dy), rows], outs[w].at[2 * x + y, rows]

    def start(ins, outs, ss, rs, b):
        for w, j, peer, _, own in walk(outs, "mine"):
            _remote(own, own, ss, rs, b + 6 * w + j, peer).start()

    def mid(ins, outs, ss, rs, b):
        x, y, cc = _me()
        for w, j, peer, land, _ in walk(outs, "mine"):
            _remote(land, land, ss, rs, b + 6 * w + j, peer).wait_recv()
            _remote(land, land, ss, rs, b + 6 * w + 3 + j, (x, y, 1 - cc)).start()

    def finish(ins, outs, ss, rs, b):
        x, y, cc = _me()
        for w, j, _, land, _ in walk(outs, "other"):
            _remote(land, land, ss, rs, b + 6 * w + 3 + j, (x, y, 1 - cc)).wait_recv()
        for w, j, peer, land, own in walk(outs, "mine"):
            _remote(own, own, ss, rs, b + 6 * w + j, peer).wait_send()
            _remote(land, land, ss, rs, b + 6 * w + 3 + j, (x, y, 1 - cc)).wait_send()

    return _Side(bufs, [_sds(tuple(w.shape), w.dtype) for w in bufs], {i: i for i in range(n)}, 6 * n,
                 start, mid, finish)


def _copies_side(ins, out_shapes, nsem, copies):
    def start(*a):
        for cp in copies(*a):
            cp.start()

    def finish(*a):
        for cp in copies(*a):
            cp.wait()

    return _Side(ins, out_shapes, {}, nsem, start, None, finish)


def _swap_side(gs):
    def copies(ins, outs, ss, rs, b):
        x, y, cc = _me()
        return [_remote(ins[w].at[:, 1 - cc], outs[w], ss, rs, b + w, (x, y, 1 - cc)) for w in range(len(gs))]

    return _copies_side(gs, [_sds((NQ,) + tuple(g.shape[2:]), F32) for g in gs], len(gs), copies)


def _exchange_side(cs):
    def copies(ins, outs, ss, rs, b):
        x, y, cc = _me()
        return [_remote(ins[w].at[2 * (x ^ dx) + (y ^ dy)], outs[w].at[2 * x + y], ss, rs, b + 3 * w + j,
                        (x ^ dx, y ^ dy, cc))
                for w in range(len(cs)) for j, (dx, dy) in enumerate(_CHIP_OFFSETS)]

    return _copies_side(cs, [_sds(tuple(c.shape), c.dtype) for c in cs], 3 * len(cs), copies)


def _share_side(fs):
    def copies(ins, outs, ss, rs, b):
        x, y, cc = _me()
        return [_remote(ins[w], outs[w], ss, rs, b + w, (x, y, 1 - cc)) for w in range(len(fs))]

    return _copies_side(fs, [_sds(tuple(f.shape), F32) for f in fs], len(fs), copies)


def _small_allreduce_adamw(g, w, m, v, nd):
    rows = g.shape[0]
    nr = rows - nd
    hr = nr // 2
    assert nd % 8 == 0 and hr % 8 == 0

    def body(g_ref, w_ref, m_ref, v_ref, gs_ref, d_ref, mo_ref, vo_ref, gath, sib, csum, slots, tot, ss, rs):
        x, y, cc = _me()
        me = 4 * x + 2 * y + cc
        q = 2 * x + y
        sibling = (x, y, 1 - cc)
        dm = g_ref.at[pl.ds(0, nd)]
        gath[me] = g_ref[0:nd, :]
        to_all = [_remote(dm, gath.at[me], ss, rs, k, (x ^ dx, y ^ dy, cc ^ dc)) for k, (dx, dy, dc) in enumerate(_OFFSETS7)]
        to_sib = _remote(g_ref.at[pl.ds(nd, nr)], sib, ss, rs, 7, sibling)
        for cp in to_all + [to_sib]:
            cp.start()
        to_sib.wait_recv()
        csum[...] = g_ref[nd:, :] + sib[...]
        mine = pl.ds(pl.multiple_of(cc * hr, 8), hr)
        slots[q] = csum[mine, :]
        to_chips = [_remote(csum.at[mine], slots.at[q], ss, rs, 8 + j, (x ^ dx, y ^ dy, cc))
                    for j, (dx, dy) in enumerate(_CHIP_OFFSETS)]
        for cp in to_chips:
            cp.start()
        for cp in to_chips:
            cp.wait_recv()
        tot[mine, :] = (slots[0] + slots[1]) + (slots[2] + slots[3])
        halves = _remote(tot.at[mine], tot.at[mine], ss, rs, 11, sibling)
        halves.start()
        for cp in to_all:
            cp.wait_recv()
        dsum = gath[0]
        for dev in range(1, N_DEV):
            dsum = dsum + gath[dev]
        halves.wait_recv()
        for lo, n, total in ((0, nd, dsum), (nd, nr, tot[...])):
            gs_ref[lo:lo + n, :] = total
            d, mn, vn = _adamw(w_ref[lo:lo + n, :], total, m_ref[lo:lo + n, :], v_ref[lo:lo + n, :])
            d_ref[lo:lo + n, :] = d
            mo_ref[lo:lo + n, :] = mn
            vo_ref[lo:lo + n, :] = vn
        for cp in to_all + [to_sib, halves] + to_chips:
            cp.wait_send()

    return _call(
        body, name="small_allreduce_adamw",
        in_specs=[VMEM] * 4, out_specs=[VMEM] * 5,
        out_shape=[_sds((rows, LANE), F32)] * 4 + [_sds((N_DEV, nd, LANE), F32)],
        scratch_shapes=[pltpu.VMEM((nr, LANE), F32), pltpu.VMEM((nr, LANE), F32), pltpu.VMEM((NQ, hr, LANE), F32),
                        pltpu.VMEM((nr, LANE), F32), pltpu.SemaphoreType.DMA((12,)), pltpu.SemaphoreType.DMA((12,))],
        compiler_params=_params(None, VMEM_LIMIT),
    )(g, w, m, v)


_SMALL = ["b_ada", "norm_ffn1_g", "norm_mix_g", "pool_scale", "gmlp_ln_g", "gmlp_ln_b", "b_spatial",
          "norm_ffn2_g", "norm_final_g", "w_pool", "w_spatial"]


def _pack(parts):
    blocks, layout, r0 = [], {}, 0
    for name in _SMALL:
        a = parts[name]
        n = a.size
        rows = -(-n // LANE)
        rows8 = -(-rows // 8) * 8
        flat = a.reshape(-1).astype(F32)
        if rows8 * LANE != n:
            flat = jnp.concatenate([flat, jnp.zeros((rows8 * LANE - n,), F32)])
        blocks.append(flat.reshape(rows8, LANE))
        layout[name] = (r0, n, a.shape)
        r0 += rows8
    return jnp.concatenate(blocks, axis=0), layout


def _unpack(packed, layout):
    out = {}
    for name, (r0, n, shape) in layout.items():
        rows = -(-n // LANE)
        out[name] = packed[r0:r0 + rows].reshape(-1)[:n].reshape(shape)
    return out


def _modv(mod9, sub, gain):
    rows = jnp.concatenate([mod9[3 * sub:3 * sub + 3], gain.reshape(1, D), jnp.zeros((4, D), F32)], axis=0)
    return rows


_BIG = ["ffn1_w_in", "ffn1_w_out", "w_mix_in", "w_mix_out", "ffn2_w_in", "ffn2_w_out"]


def kernel(x, c, w_ada, b_ada, norm_ffn1_g, ffn1_w_in, ffn1_w_out, norm_mix_g, w_mix_in, w_pool, pool_scale, gmlp_ln_g, gmlp_ln_b, w_spatial, b_spatial, w_mix_out, norm_ffn2_g, ffn2_w_in, ffn2_w_out, norm_final_g, loss_target, m_w_ada, m_b_ada, m_norm_ffn1_g, m_ffn1_w_in, m_ffn1_w_out, m_norm_mix_g, m_w_mix_in, m_w_pool, m_pool_scale, m_gmlp_ln_g, m_gmlp_ln_b, m_w_spatial, m_b_spatial, m_w_mix_out, m_norm_ffn2_g, m_ffn2_w_in, m_ffn2_w_out, m_norm_final_g, v_w_ada, v_b_ada, v_norm_ffn1_g, v_ffn1_w_in, v_ffn1_w_out, v_norm_mix_g, v_w_mix_in, v_w_pool, v_pool_scale, v_gmlp_ln_g, v_gmlp_ln_b, v_w_spatial, v_b_spatial, v_w_mix_out, v_norm_ffn2_g, v_ffn2_w_in, v_ffn2_w_out, v_norm_final_g):
    names = ["w_ada", "b_ada", "norm_ffn1_g", "ffn1_w_in", "ffn1_w_out", "norm_mix_g", "w_mix_in", "w_pool",
             "pool_scale", "gmlp_ln_g", "gmlp_ln_b", "w_spatial", "b_spatial", "w_mix_out", "norm_ffn2_g",
             "ffn2_w_in", "ffn2_w_out", "norm_final_g"]
    W = dict(zip(names, [w_ada, b_ada, norm_ffn1_g, ffn1_w_in, ffn1_w_out, norm_mix_g, w_mix_in, w_pool, pool_scale,
                         gmlp_ln_g, gmlp_ln_b, w_spatial, b_spatial, w_mix_out, norm_ffn2_g, ffn2_w_in, ffn2_w_out,
                         norm_final_g]))
    M = dict(zip(names, [m_w_ada, m_b_ada, m_norm_ffn1_g, m_ffn1_w_in, m_ffn1_w_out, m_norm_mix_g, m_w_mix_in, m_w_pool,
                         m_pool_scale, m_gmlp_ln_g, m_gmlp_ln_b, m_w_spatial, m_b_spatial, m_w_mix_out, m_norm_ffn2_g,
                         m_ffn2_w_in, m_ffn2_w_out, m_norm_final_g]))
    V = dict(zip(names, [v_w_ada, v_b_ada, v_norm_ffn1_g, v_ffn1_w_in, v_ffn1_w_out, v_norm_mix_g, v_w_mix_in, v_w_pool,
                         v_pool_scale, v_gmlp_ln_g, v_gmlp_ln_b, v_w_spatial, v_b_spatial, v_w_mix_out, v_norm_ffn2_g,
                         v_ffn2_w_in, v_ffn2_w_out, v_norm_final_g]))

    xi, yi, ci = _me()
    q = 2 * xi + yi
    core = ci.astype(jnp.int32).reshape(1)

    chip = q.astype(jnp.int32).reshape(1)
    place = lambda n: _cast_place(W[n][0], chip)

    ncol = w_ada.shape[2]
    b_q = lax.dynamic_slice_in_dim(b_ada, q * ncol, ncol, axis=1)
    (cact_all, modsel), (win1, wout1) = _ada_fwd(
        c, w_ada[0], b_q, side=_gather_side([place("ffn1_w_in"), place("ffn1_w_out")]))
    mod9 = modsel[:NQ].reshape(9, D)
    xs, target = x[0], loss_target[0]
    mv1 = _modv(mod9, 0, norm_ffn1_g[0])
    mv2 = _modv(mod9, 1, norm_mix_g[0])
    mv3 = _modv(mod9, 2, norm_ffn2_g[0])
    wcat, wtcat, bias = _prep_spatial(w_spatial[0], b_spatial[0].T)
    wpool = w_pool[0].astype(BF16)
    vecs = jnp.concatenate([pool_scale, gmlp_ln_g, gmlp_ln_b, jnp.zeros((5, DP), F32)], axis=0)
    gf = jnp.concatenate([norm_final_g.reshape(1, D), jnp.zeros((7, D), F32)], axis=0)

    later =["w_mix_in", "w_mix_out", "ffn2_w_in", "ffn2_w_out"]
    (x1, g1s, u1s), got = _ffn_fwd(xs, mv1, win1, wout1.reshape(2, CH, D), side=_gather_side([place(n) for n in later]))
    wmi, wmo, win2, wout2 = got
    wmi = jnp.transpose(wmi, (1, 0, 2)).reshape(D, DPROJ)
    wmo = wmo.reshape(DP + DG, D)
    x2, pooled, zb = _mix_fwd(x1, mv2, wmi, wpool, vecs, wcat, bias, wmo)
    (dx3, g3s, u3s, loss_blk, dgf), _ = _ffn_fwd(x2, mv3, win2, wout2.reshape(2, CH, D), head=(target, gf))
    loss = lax.psum(loss_blk[0, 0], ("x", "y", "c"))

    wo1, wo2 = wout1.reshape(2, CH, D), wout2.reshape(2, CH, D)
    (dx2, oin2, oout2, rin2, rout2, vec3), _ = _ffn_bwd(x2, dx3, g3s, u3s, mv3, win2, wo2)
    cs2 = [_chip_sum_pair(oin2, rin2), _chip_sum_pair(oout2, rout2)]
    (dx1, dwmi, dwmo, dwpool, dwsp, dbsp, v512, vec2), ex2 = _mix_bwd(
        x1, dx2, pooled, zb, mv2, wmi, wpool, vecs, wcat, wtcat, bias, wmo, side=_exchange_side(cs2))
    half2 = [_sum4(cs, e, chip) for cs, e in zip(cs2, ex2)]
    qcols = w_mix_in.shape[2]
    vmix = [jnp.transpose(dwmi.reshape(D, NQ, qcols), (1, 0, 2)).reshape(NQ, 2, D // 2, qcols),
            dwmo.reshape(NQ, 2, (DP + DG) // 8, D)]
    (grad_x, oin1, oout1, rin1, rout1, vec1), got = _ffn_bwd(
        xs, dx1, g1s, u1s, mv1, win1, wo1, side=_join(_swap_side(vmix), _share_side(half2)))
    sibmix, other2 = got[:2], got[2:]
    cs1 = [_chip_sum(g, r, core) for g, r in zip(vmix, sibmix)] + [_chip_sum_pair(oin1, rin1),
                                                                  _chip_sum_pair(oout1, rout1)]

    dmod = jnp.concatenate([vec1[0:3], vec2[0:3], vec3[0:3]], axis=0)
    grads = dict(
        b_ada=dmod.reshape(1, 9 * D), norm_ffn1_g=vec1[3:4], norm_mix_g=vec2[3:4], norm_ffn2_g=vec3[3:4],
        pool_scale=v512[0:1], gmlp_ln_g=v512[1:2], gmlp_ln_b=v512[2:3], b_spatial=dbsp[None],
        norm_final_g=dgf[0], w_pool=dwpool[None], w_spatial=dwsp[None])

    gp, layout = _pack({n: grads[n] for n in _SMALL})
    wp, _ = _pack({n: W[n] for n in _SMALL})
    mp, _ = _pack({n: M[n] for n in _SMALL})
    vp, _ = _pack({n: V[n] for n in _SMALL})
    r0, nb, _ = layout["b_ada"]
    assert r0 == 0
    out_g, out_d, out_m, out_v = {}, {}, {}, {}
    gs, dl, mo, vo, gath = _small_allreduce_adamw(gp, wp, mp, vp, nb // LANE)
    for packed, dst in ((gs, out_g), (dl, out_d), (mo, out_m), (vo, out_v)):
        for n, a in _unpack(packed, layout).items():
            dst[n] = a.reshape(W[n].shape)

    def update(n, own, recv, side=None):
        (g2, d, mn, vn), got = _adamw_halves(W[n][0], own, recv, M[n][0], V[n][0], side=side)
        out_g[n], out_d[n], out_m[n], out_v[n] = g2[None], d[None], mn[None], vn[None]
        return got

    ex_out1 = update("ffn2_w_in", half2[0], other2[0], side=_exchange_side(cs1[3:4]))
    ex_mix = update("ffn2_w_out", half2[1], other2[1], side=_exchange_side(cs1[0:2]))
    dmod_q = lax.dynamic_slice_in_dim(gath.reshape(N_DEV, nb), q * ncol, ncol, axis=1)
    (ga, da, ma, va), ex_in1 = _ada_grad_adamw(cact_all.T, dmod_q, w_ada[0], m_w_ada[0], v_w_ada[0],
                                              side=_exchange_side(cs1[2:3]))
    out_g["w_ada"], out_d["w_ada"], out_m["w_ada"], out_v["w_ada"] = ga[None], da[None], ma[None], va[None]
    ex1 = list(ex_mix) + list(ex_in1) + list(ex_out1)
    half1 = [_sum4(cs, e, chip) for cs, e in zip(cs1, ex1)]
    other1 = _run_side(_share_side(half1), "sibling_share")
    for n, own, recv in zip(["w_mix_in", "w_mix_out", "ffn1_w_in", "ffn1_w_out"], half1, other1):
        update(n, own, recv)

    return (loss, grad_x[None], *[out_g[n] for n in names], *[out_d[n] for n in names],
            *[out_m[n] for n in names], *[out_v[n] for n in names])
```

```python
import functools
import math

import jax
import jax.numpy as jnp
from jax import lax
from jax.experimental import pallas as pl
from jax.experimental.pallas import tpu as pltpu

F32 = jnp.float32
BF16 = jnp.bfloat16
MESH = pl.DeviceIdType.MESH
HIGHEST = lax.Precision.HIGHEST

EPS = 1e-6
D = 1024
DFF = 2816
CH = DFF // 2
NQ = 4
DP = 512
DG = 512
DPROJ = DP + 2 * DG
POOL_WINDOWS = (2, 4, 8, 16)
HALO = 16
CHUNK = 128
LANE = 128
N_DEV = 8

ADAM_LR = 0.001
ADAM_B1 = 0.9
ADAM_B2 = 0.999
ADAM_EPS = 1e-08
ADAM_WD = 0.01
ADAM_STEP = 10

VMEM_LIMIT = 62 * 1024 * 1024

TM_FFN_FWD = 512
TM_FFN_BWD = 512
TM_MIX = 256


def _call(body, **kw):
    return pl.pallas_call(body, interpret=False, **kw)


def _params(sem=None, vmem=None):
    return pltpu.CompilerParams(dimension_semantics=sem, vmem_limit_bytes=vmem)


def _sds(shape, dtype):
    return jax.ShapeDtypeStruct(shape, dtype)


ANY = pl.BlockSpec(memory_space=pl.ANY)
VMEM = pl.BlockSpec(memory_space=pltpu.VMEM)
SMEM = pl.BlockSpec(memory_space=pltpu.SMEM)


def _norm_mod(x, gn, sc, sh):
    r = lax.rsqrt(jnp.mean(x * x, axis=-1, keepdims=True) + EPS)
    xn = x * r
    hp = xn * gn
    return r, xn, hp, hp * (1.0 + sc) + sh


def _norm_mod_bwd(dh, r, xn, hp, gn, sc):
    one_sc = 1.0 + sc
    dsh = jnp.sum(dh, axis=0, keepdims=True)
    dsc = jnp.sum(dh * hp, axis=0, keepdims=True)
    dgn = jnp.sum(dh * one_sc * xn, axis=0, keepdims=True)
    dxn = dh * (gn * one_sc)
    dx = r * (dxn - xn * jnp.mean(dxn * xn, axis=-1, keepdims=True))
    return dsh, dsc, dgn, dx


def _dot(a, b):
    return jnp.dot(a, b, preferred_element_type=F32)


def _dot_nt(a, b):
    return lax.dot_general(a, b, (((1,), (1,)), ((), ())), preferred_element_type=F32)


def _dot_tn(a, b):
    return lax.dot_general(a, b, (((0,), (0,)), ((), ())), preferred_element_type=F32)


_GELU_C = math.sqrt(2.0 / math.pi)
_GELU_A = 0.044715


def _gelu_fwd_bwd(x):
    x2 = x * x
    t = jnp.tanh(_GELU_C * (x + _GELU_A * x * x2))
    g = 0.5 * x * (1.0 + t)
    dg = 0.5 * (1.0 + t) + 0.5 * x * (1.0 - t * t) * (_GELU_C * (1.0 + 3.0 * _GELU_A * x2))
    return g, dg


def _adamw(w, g, m, v):
    m = ADAM_B1 * m + (1.0 - ADAM_B1) * g
    v = ADAM_B2 * v + (1.0 - ADAM_B2) * (g * g)
    m_hat = m / (1.0 - ADAM_B1 ** ADAM_STEP)
    v_hat = v / (1.0 - ADAM_B2 ** ADAM_STEP)
    delta = -ADAM_LR * (m_hat / (jnp.sqrt(v_hat) + ADAM_EPS) + ADAM_WD * w)
    return delta, m, v


def _row_block(rows, cap=256, mult=16):
    best = None
    for t in range(mult, min(rows, cap) + 1, mult):
        if rows % t == 0:
            best = t
    assert best is not None, rows
    return best


def _head_math(x, target, gf):
    r = lax.rsqrt(jnp.mean(x * x, axis=-1, keepdims=True) + EPS)
    xn = x * r
    err = xn * gf - target
    dy = err * (1.0 / D)
    dxn = dy * gf
    dx = r * (dxn - xn * jnp.mean(dxn * xn, axis=-1, keepdims=True))
    return (0.5 / D) * jnp.sum(err * err), jnp.sum(dy * xn, axis=0, keepdims=True), dx


def _ffn_fwd(x, modv, win, wout, side=None, head=None):
    S = x.shape[0]
    tm = TM_FFN_FWD
    nt = S // tm

    def body(*refs):
        if head is None:
            x_ref, mod_ref, wg_ref, wu_ref, wo_ref, xo_ref, gs_ref, us_ref, acc_scr = refs
        else:
            (x_ref, mod_ref, wg_ref, wu_ref, wo_ref, t_ref, gf_ref,
             xo_ref, gs_ref, us_ref, loss_ref, dgf_ref, acc_scr) = refs

        @pl.when((pl.program_id(0) == 0) & (pl.program_id(1) == 0))
        def _():
            acc_scr[...] = jnp.zeros_like(acc_scr)
            if head is not None:
                loss_ref[...] = jnp.zeros_like(loss_ref)
                dgf_ref[...] = jnp.zeros_like(dgf_ref)

        j = pl.program_id(1)
        h = _norm_mod(x_ref[...], mod_ref[3:4, :], mod_ref[1:2, :], mod_ref[0:1, :])[3].astype(BF16)
        g = _dot(h, wg_ref[...]).astype(BF16)
        u = _dot(h, wu_ref[...]).astype(BF16)
        gs_ref[...] = g
        us_ref[...] = u
        gf = g.astype(F32)
        a = (gf * jax.nn.sigmoid(gf) * u.astype(F32)).astype(BF16)
        acc = jnp.where(j == 0, 0.0, acc_scr[...]) + _dot(a, wo_ref[...])
        acc_scr[...] = acc
        xo = x_ref[...] + (0.5 * mod_ref[2:3, :]) * acc
        if head is None:
            xo_ref[...] = xo
        else:
            @pl.when(j == 1)
            def _():
                loss, dgf, dx = _head_math(xo, t_ref[...], gf_ref[0:1, :])
                loss_ref[...] += loss
                dgf_ref[0:1, :] += dgf
                xo_ref[...] = dx

    step = lambda i, j: lambda: (pl.program_id(0) == i) & (pl.program_id(1) == j)
    tile = pl.BlockSpec((tm, D), lambda i, j: (i, 0))
    const = lambda shape: pl.BlockSpec(shape, lambda i, j: (0, 0))
    chunk = pl.BlockSpec((tm, CH), lambda i, j: (i, j))
    in_specs = [tile, const((8, D)), pl.BlockSpec((None, D, CH), lambda i, j: (j, 0, 0)),
                pl.BlockSpec((None, D, CH), lambda i, j: (2 + j, 0, 0)), pl.BlockSpec((None, CH, D), lambda i, j: (j, 0, 0))]
    out_specs = [tile, chunk, chunk]
    out_shape = [_sds((S, D), F32), _sds((S, DFF), BF16), _sds((S, DFF), BF16)]
    args = (x, modv, win, win, wout)
    if head is not None:
        in_specs += [tile, const((8, D))]
        out_specs += [const((8, LANE)), const((8, D))]
        out_shape += [_sds((8, LANE), F32), _sds((8, D), F32)]
        args += tuple(head)
    return _side_call(
        body, side, (step(0, 0), step((7 * nt) // 10, 0), step(nt - 1, 1)), name="ffn_fwd",
        grid=(nt, 2), in_specs=in_specs, out_specs=out_specs, out_shape=out_shape,
        scratch_shapes=[pltpu.VMEM((tm, D), F32)],
        compiler_params=_params(("arbitrary", "arbitrary"), VMEM_LIMIT),
        args=args)


def _ffn_bwd_pass(jj, x, dxo, gs, us, modv, win, wout, prev=None, side=None):
    S = x.shape[0]
    tm = TM_FFN_BWD
    nsub = tm // 256
    nt = S // tm
    hi, ho = D // 2, CH // 4
    last = prev is not None
    assert last == (jj == 1)

    def body(*refs):
        x_ref, dxo_ref, gs_ref, us_ref, mod_ref, wg_ref, wu_ref, wo_ref = refs[:8]
        k = 13 if last else 8
        out_ref, dwin_ref, dwout_ref, rwin_ref, rwout_ref, vec_ref = refs[k:k + 6]
        accg, accu, accw, sems, fsend, frecv = refs[k + 6:]
        i = pl.program_id(0)

        @pl.when(i == 0)
        def _():
            accg[...] = jnp.zeros_like(accg)
            accu[...] = jnp.zeros_like(accu)
            accw[...] = jnp.zeros_like(accw)
            vec_ref[...] = jnp.zeros_like(vec_ref)

        gn, sc, sh, gate = mod_ref[3:4, :], mod_ref[1:2, :], mod_ref[0:1, :], mod_ref[2:3, :]

        parts = []
        for s in range(nsub):
            rs = slice(s * (tm // nsub), (s + 1) * (tm // nsub))
            r, xn, hp, h = _norm_mod(x_ref[rs, :], gn, sc, sh)
            dxo = dxo_ref[rs, :]
            dy = (dxo * (0.5 * gate)).astype(BF16)
            g = gs_ref[rs, :].astype(F32)
            u = us_ref[rs, :].astype(F32)
            sig = jax.nn.sigmoid(g)
            sl = g * sig
            a = (sl * u).astype(BF16)
            da = _dot_nt(dy, wo_ref[...])
            dg = (da * u * (sig * (1.0 + g * (1.0 - sig)))).astype(BF16)
            du = (da * sl).astype(BF16)
            dhp = _dot_nt(dg, wg_ref[...]) + _dot_nt(du, wu_ref[...])
            parts.append((h.astype(BF16), a, dg, du, dxo.astype(BF16)))
            if last:
                dsh, dsc, dgn, dxin = _norm_mod_bwd(refs[8][rs, :] + dhp, r, xn, hp, gn, sc)
                vec_ref[0:1, :] += dsh
                vec_ref[1:2, :] += dsc
                vec_ref[3:4, :] += dgn
                out_ref[rs, :] = dxo + dxin
            else:
                out_ref[rs, :] = dhp

        hb, a, dg, du, dxb = [jnp.concatenate(p, axis=0) if nsub > 1 else p[0] for p in zip(*parts)]
        accw[...] += _dot_tn(a, dxb)
        accg[...] += _dot_tn(hb, dg)
        accu[...] += _dot_tn(hb, du)

        @pl.when(i == nt - 1)
        def _():
            gw = accw[...]
            vec_ref[2:3, :] += 0.5 * jnp.sum(wo_ref[...].astype(F32) * gw, axis=0, keepdims=True)
            accw[...] = gw * (0.5 * gate)
            mx, my, cc = _me()
            rows = lambda base, n, c: pl.ds(base + c * n, n)
            pieces = [(accg, 0, hi, dwin_ref, rwin_ref, jj), (accu, 0, hi, dwin_ref, rwin_ref, 2 + jj),
                      (accw, 0, ho, dwout_ref, rwout_ref, 2 * jj), (accw, 2 * ho, ho, dwout_ref, rwout_ref, 2 * jj + 1)]
            loc = [pltpu.make_async_copy(acc.at[rows(base, n, cc)], own.at[slot], sems.at[p])
                   for p, (acc, base, n, own, _, slot) in enumerate(pieces)]
            rem = [pltpu.make_async_remote_copy(acc.at[rows(base, n, 1 - cc)], sib.at[slot], fsend.at[p], frecv.at[p],
                                                device_id=(mx, my, 1 - cc), device_id_type=MESH)
                   for p, (acc, base, n, _, sib, slot) in enumerate(pieces)]
            for cp in loc + rem:
                cp.start()
            for cp in loc:
                cp.wait()
            for cp in rem:
                cp.wait()

    once = pl.Buffered(1)
    tile = pl.BlockSpec((tm, D), lambda i: (i, 0))
    chunk = pl.BlockSpec((tm, CH), lambda i: (i, jj))
    in_specs = [tile, tile, chunk, chunk, pl.BlockSpec((8, D), lambda i: (0, 0)),
                pl.BlockSpec((None, D, CH), lambda i: (jj, 0, 0), pipeline_mode=once),
                pl.BlockSpec((None, D, CH), lambda i: (2 + jj, 0, 0), pipeline_mode=once),
                pl.BlockSpec((None, CH, D), lambda i: (jj, 0, 0), pipeline_mode=once)]
    args = (x, dxo, gs, us, modv, win, win, wout)
    if last:
        in_specs += [tile, ANY, ANY, ANY, ANY]
        args += tuple(prev)
    step = lambda s: lambda: pl.program_id(0) == s
    return _side_call(
        body, side, (step(0), None, step(nt - 1)), name="ffn_bwd",
        grid=(nt,), in_specs=in_specs,
        out_specs=[tile, ANY, ANY, ANY, ANY, pl.BlockSpec((8, D), lambda i: (0, 0))],
        out_shape=[_sds((S, D), F32), _sds((NQ, hi, CH), F32), _sds((NQ, ho, D), F32), _sds((NQ, hi, CH), F32),
                   _sds((NQ, ho, D), F32), _sds((8, D), F32)],
        scratch_shapes=[pltpu.VMEM((D, CH), F32), pltpu.VMEM((D, CH), F32), pltpu.VMEM((CH, D), F32),
                        pltpu.SemaphoreType.DMA((4,)), pltpu.SemaphoreType.DMA((4,)), pltpu.SemaphoreType.DMA((4,))],
        aliases={9 + p: 1 + p for p in range(4)} if last else {},
        compiler_params=_params(("arbitrary",), VMEM_LIMIT),
        args=args)


def _ffn_bwd(x, dxo, gs, us, modv, win, wout, side=None):
    first, extra = _ffn_bwd_pass(0, x, dxo, gs, us, modv, win, wout, side=side)
    (dx, dwin, dwout, rwin, rwout, vec), _ = _ffn_bwd_pass(1, x, dxo, gs, us, modv, win, wout, prev=first[:5])
    return (dx, dwin, dwout, rwin, rwout, first[5] + vec), extra


def _prep_spatial(w_spatial, b_spatial_t):
    def body(w_ref, b_ref, wcat_ref, wtcat_ref, bias_ref):
        row = lax.broadcasted_iota(jnp.int32, (CHUNK, CHUNK), 0)
        col = lax.broadcasted_iota(jnp.int32, (CHUNK, CHUNK), 1)
        tril = col <= row
        for p in range(4):
            wa = jnp.where(tril, w_ref[2 * p], 0.0)
            wb = jnp.where(tril, w_ref[2 * p + 1], 0.0)
            wcat_ref[p] = jnp.concatenate([wa, wb], axis=1).astype(BF16)
            wtcat_ref[p] = jnp.concatenate([wa.T, wb.T], axis=1).astype(BF16)
        head = lax.broadcasted_iota(jnp.int32, (8, DG), 0)
        ch = lax.broadcasted_iota(jnp.int32, (8, DG), 1)
        spread = jnp.where(ch // 64 == head, 1.0, 0.0).astype(F32)
        bias_ref[...] = jnp.dot(b_ref[...], spread, precision=HIGHEST, preferred_element_type=F32)

    return _call(
        body, name="prep_spatial",
        in_specs=[VMEM, VMEM], out_specs=[VMEM, VMEM, VMEM],
        out_shape=[_sds((4, CHUNK, 2 * CHUNK), BF16), _sds((4, CHUNK, 2 * CHUNK), BF16), _sds((CHUNK, DG), F32)],
    )(w_spatial, b_spatial_t)


def _pair_rhs(blocks):
    lane = lax.broadcasted_iota(jnp.int32, (CHUNK, LANE), 1)
    lo = lane < 64
    top = jnp.concatenate([jnp.where(lo, b, 0.0) for b in blocks], axis=1)
    bot = jnp.concatenate([jnp.where(lo, 0.0, b) for b in blocks], axis=1)
    return top, bot


def _gmlp_branch(zb, vecs, wcat_ref, bias_ref, nchunks):
    z, dz = _gelu_fwd_bwd(zb)
    u = z[:, :DG]
    v = z[:, DG:]
    ln_g, ln_b = vecs[1:2, :], vecs[2:3, :]
    mu = jnp.mean(v, axis=-1, keepdims=True)
    vc = v - mu
    rstd = lax.rsqrt(jnp.mean(vc * vc, axis=-1, keepdims=True) + EPS)
    vhat = vc * rstd
    vl = vhat * ln_g + ln_b
    sv_cols = []
    for p in range(4):
        blocks = [vl[k * CHUNK:(k + 1) * CHUNK, p * LANE:(p + 1) * LANE] for k in range(nchunks)]
        top, bot = _pair_rhs(blocks)
        rhs = jnp.concatenate([top, bot], axis=0).astype(BF16)
        out = _dot(wcat_ref[p], rhs)
        bias = bias_ref[:, p * LANE:(p + 1) * LANE]
        sv_cols.append(jnp.concatenate([out[:, k * LANE:(k + 1) * LANE] + bias for k in range(nchunks)], axis=0))
    sv = jnp.concatenate(sv_cols, axis=1)
    return dict(u=u, dz=dz, rstd=rstd, vhat=vhat, vl=vl, sv=sv, yb=u * sv)


def _mix_fwd(x, modv, win, wpool, vecs, wcat, bias, wout):
    S = x.shape[0]
    tm = TM_MIX
    nt = S // tm
    nchunks = tm // CHUNK

    def body(x_ref, mod_ref, win_ref, wpool_ref, vec_ref, wcat_ref, bias_ref, wout_ref,
             xo_ref, pooled_ref, zb_ref, ext):
        i = pl.program_id(0)

        @pl.when(i == 0)
        def _():
            ext[0:HALO, :] = jnp.zeros((HALO, DP), F32)

        x = x_ref[...]
        _, _, _, h = _norm_mod(x, mod_ref[3:4, :], mod_ref[1:2, :], mod_ref[0:1, :])
        proj = _dot(h.astype(BF16), win_ref[...])
        xa = proj[:, :DP]
        zb = proj[:, DP:]
        zb_ref[...] = zb
        ext[HALO:HALO + tm, :] = xa
        pos = i * tm + lax.broadcasted_iota(jnp.int32, (tm, 1), 0)
        vecs = vec_ref[...]
        ya_cols = []
        pooled_cols = []
        for gi, w in enumerate(POOL_WINDOWS):
            cols = slice(gi * LANE, (gi + 1) * LANE)
            s = xa[:, cols]
            for k in range(1, w):
                s = s + ext[HALO - k:HALO - k + tm, cols]
            cnt = jnp.minimum(pos + 1, w).astype(F32)
            pooled = (s / cnt - xa[:, cols]).astype(BF16)
            pooled_cols.append(pooled)
            ya_cols.append(_dot(pooled, wpool_ref[gi]) * vecs[0:1, cols])
        pooled_ref[...] = jnp.concatenate(pooled_cols, axis=1)
        ext[0:HALO, :] = ext[tm:tm + HALO, :]

        gm = _gmlp_branch(zb, vecs, wcat_ref, bias_ref, nchunks)
        cat = jnp.concatenate(ya_cols + [gm["yb"]], axis=1).astype(BF16)
        xo_ref[...] = x + mod_ref[2:3, :] * _dot(cat, wout_ref[...])

    full = lambda shape: pl.BlockSpec(shape, lambda i: (0,) * len(shape))
    return _call(
        body, name="mix_fwd",
        grid=(nt,),
        in_specs=[pl.BlockSpec((tm, D), lambda i: (i, 0)), full((8, D)), full((D, DPROJ)),
                  full((4, LANE, LANE)), full((8, DP)), full((4, CHUNK, 2 * CHUNK)), full((CHUNK, DG)),
                  full((DP + DG, D))],
        out_specs=[pl.BlockSpec((tm, D), lambda i: (i, 0)), pl.BlockSpec((tm, DP), lambda i: (i, 0)),
                   pl.BlockSpec((tm, 2 * DG), lambda i: (i, 0))],
        out_shape=[_sds((S, D), F32), _sds((S, DP), BF16), _sds((S, 2 * DG), F32)],
        scratch_shapes=[pltpu.VMEM((tm + HALO, DP), F32)],
        compiler_params=_params(("arbitrary",), VMEM_LIMIT),
    )(x, modv, win, wpool, vecs, wcat, bias, wout)


def _mix_bwd(x, dxo, pooled, zb, modv, win, wpool, vecs, wcat, wtcat, bias, wout, side=None):
    S = x.shape[0]
    tm = TM_MIX
    nt = S // tm
    nchunks = tm // CHUNK

    def body(x_ref, dxo_ref, pooled_ref, zb_ref, mod_ref, win_ref, wpool_ref, vec_ref, wcat_ref, wtcat_ref,
             bias_ref, wout_ref,
             dx_ref, dwin_ref, dwout_ref, dwpool_ref, dwsp_ref, dbsp_ref, v512_ref, vd_ref, qext, dsv_acc):
        step = pl.program_id(0)
        tile = nt - 1 - step

        @pl.when(step == 0)
        def _():
            dwin_ref[...] = jnp.zeros_like(dwin_ref)
            dwout_ref[...] = jnp.zeros_like(dwout_ref)
            dwpool_ref[...] = jnp.zeros_like(dwpool_ref)
            dwsp_ref[...] = jnp.zeros_like(dwsp_ref)
            v512_ref[...] = jnp.zeros_like(v512_ref)
            vd_ref[...] = jnp.zeros_like(vd_ref)
            dsv_acc[...] = jnp.zeros_like(dsv_acc)
            qext[tm:tm + HALO, :] = jnp.zeros((HALO, DP), F32)

        gn, sc, sh, gate = mod_ref[3:4, :], mod_ref[1:2, :], mod_ref[0:1, :], mod_ref[2:3, :]
        vecs = vec_ref[...]
        x = x_ref[...]
        r, xn, hp, h = _norm_mod(x, gn, sc, sh)
        hb = h.astype(BF16)
        dxo = dxo_ref[...]

        pooled = pooled_ref[...]
        mixed_cols = [_dot(pooled[:, gi * LANE:(gi + 1) * LANE], wpool_ref[gi]) for gi in range(4)]
        mixed = jnp.concatenate(mixed_cols, axis=1)
        scale = vecs[0:1, :]
        gm = _gmlp_branch(zb_ref[...], vecs, wcat_ref, bias_ref, nchunks)
        cat = jnp.concatenate([mixed * scale, gm["yb"]], axis=1).astype(BF16)

        dwout_ref[...] += _dot_tn(cat, dxo.astype(BF16))
        dcat = _dot_nt((dxo * gate).astype(BF16), wout_ref[...])
        dya = dcat[:, :DP]
        dyb = dcat[:, DP:]

        v512_ref[0:1, :] += jnp.sum(dya * mixed, axis=0, keepdims=True)
        dmixed = (dya * scale).astype(BF16)
        pos = tile * tm + lax.broadcasted_iota(jnp.int32, (tm, 1), 0)
        dpooled_cols = []
        for gi, w in enumerate(POOL_WINDOWS):
            cols = slice(gi * LANE, (gi + 1) * LANE)
            dp = _dot_nt(dmixed[:, cols], wpool_ref[gi])
            dwpool_ref[gi] += _dot_tn(pooled[:, cols], dmixed[:, cols])
            cnt = jnp.minimum(pos + 1, w).astype(F32)
            qext[0:tm, cols] = dp / cnt
            dpooled_cols.append(dp)
        dxa_cols = []
        for gi, w in enumerate(POOL_WINDOWS):
            cols = slice(gi * LANE, (gi + 1) * LANE)
            s = qext[0:tm, cols]
            for k in range(1, w):
                s = s + qext[k:k + tm, cols]
            dxa_cols.append(s - dpooled_cols[gi])
        qext[tm:tm + HALO, :] = qext[0:HALO, :]

        u, sv, vl = gm["u"], gm["sv"], gm["vl"]
        du = dyb * sv
        dsv = dyb * u
        dvl_cols = []
        for p in range(4):
            cols = slice(p * LANE, (p + 1) * LANE)
            dblocks = [dsv[k * CHUNK:(k + 1) * CHUNK, cols] for k in range(nchunks)]
            vblocks = [vl[k * CHUNK:(k + 1) * CHUNK, cols] for k in range(nchunks)]
            tot = dblocks[0]
            for b in dblocks[1:]:
                tot = tot + b
            dsv_acc[:, cols] += tot
            top, bot = _pair_rhs(dblocks)
            out = _dot(wtcat_ref[p], jnp.concatenate([top, bot], axis=0).astype(BF16))
            dvl_cols.append(jnp.concatenate([out[:, k * LANE:(k + 1) * LANE] for k in range(nchunks)], axis=0))
            vcat = jnp.concatenate(vblocks, axis=1).astype(BF16)
            dwsp_ref[2 * p] += _dot_nt(top.astype(BF16), vcat)
            dwsp_ref[2 * p + 1] += _dot_nt(bot.astype(BF16), vcat)
        dvl = jnp.concatenate(dvl_cols, axis=1)
        vhat, rstd = gm["vhat"], gm["rstd"]
        v512_ref[1:2, :] += jnp.sum(dvl * vhat, axis=0, keepdims=True)
        v512_ref[2:3, :] += jnp.sum(dvl, axis=0, keepdims=True)
        dvh = dvl * vecs[1:2, :]
        dv = rstd * (dvh - jnp.mean(dvh, axis=-1, keepdims=True)
                     - vhat * jnp.mean(dvh * vhat, axis=-1, keepdims=True))
        dzb = jnp.concatenate([du, dv], axis=1) * gm["dz"]

        dproj = jnp.concatenate(dxa_cols + [dzb], axis=1).astype(BF16)
        dwin_ref[...] += _dot_tn(hb, dproj)
        dh = _dot_nt(dproj, win_ref[...])
        dsh, dsc, dgn, dxin = _norm_mod_bwd(dh, r, xn, hp, gn, sc)
        vd_ref[0:1, :] += dsh
        vd_ref[1:2, :] += dsc
        vd_ref[3:4, :] += dgn
        dx_ref[...] = dxo + dxin

        @pl.when(step == nt - 1)
        def _():
            gw = dwout_ref[...]
            vd_ref[2:3, :] += jnp.sum(wout_ref[...].astype(F32) * gw, axis=0, keepdims=True)
            dwout_ref[...] = gw * gate
            row = lax.broadcasted_iota(jnp.int32, (CHUNK, CHUNK), 0)
            col = lax.broadcasted_iota(jnp.int32, (CHUNK, CHUNK), 1)
            for hh in range(8):
                dwsp_ref[hh] = jnp.where(col <= row, dwsp_ref[hh], 0.0)
            head = lax.broadcasted_iota(jnp.int32, (8, DG), 0)
            ch = lax.broadcasted_iota(jnp.int32, (8, DG), 1)
            spread = jnp.where(ch // 64 == head, 1.0, 0.0).astype(F32)
            dbsp_ref[...] = lax.dot_general(spread, dsv_acc[...], (((1,), (1,)), ((), ())),
                                            precision=HIGHEST, preferred_element_type=F32)

    full = lambda shape: pl.BlockSpec(shape, lambda s: (0,) * len(shape))
    rev = lambda cols: pl.BlockSpec((tm, cols), lambda s: (nt - 1 - s, 0))
    step = lambda s: lambda: pl.program_id(0) == s
    return _side_call(
        body, side, (step(0), None, step(nt - 1)), name="mix_bwd",
        grid=(nt,),
        in_specs=[rev(D), rev(D), rev(DP), rev(2 * DG), full((8, D)), full((D, DPROJ)), full((4, LANE, LANE)),
                  full((8, DP)), full((4, CHUNK, 2 * CHUNK)), full((4, CHUNK, 2 * CHUNK)), full((CHUNK, DG)),
                  full((DP + DG, D))],
        out_specs=[rev(D), full((D, DPROJ)), full((DP + DG, D)), full((4, LANE, LANE)), full((8, CHUNK, CHUNK)),
                   full((8, CHUNK)), full((8, DP)), full((8, D))],
        out_shape=[_sds((S, D), F32), _sds((D, DPROJ), F32), _sds((DP + DG, D), F32), _sds((4, LANE, LANE), F32),
                   _sds((8, CHUNK, CHUNK), F32), _sds((8, CHUNK), F32), _sds((8, DP), F32), _sds((8, D), F32)],
        scratch_shapes=[pltpu.VMEM((tm + HALO, DP), F32), pltpu.VMEM((CHUNK, DG), F32)],
        compiler_params=_params(("arbitrary",), VMEM_LIMIT),
        args=(x, dxo, pooled, zb, modv, win, wpool, vecs, wcat, wtcat, bias, wout))


def _chip_sum(g, rbuf, core):
    _, _, hr, cols = g.shape
    tr = _row_block(hr)

    def body(c_ref, g_ref, r_ref, o_ref):
        o_ref[...] = (g_ref[...] + r_ref[...]).astype(BF16)

    return pl.pallas_call(
        body, name="chip_sum", interpret=False,
        grid_spec=pltpu.PrefetchScalarGridSpec(
            num_scalar_prefetch=1, grid=(NQ, hr // tr),
            in_specs=[pl.BlockSpec((None, None, tr, cols), lambda q, i, c: (q, c[0], i, 0)),
                      pl.BlockSpec((None, tr, cols), lambda q, i, c: (q, i, 0))],
            out_specs=pl.BlockSpec((None, tr, cols), lambda q, i, c: (q, i, 0))),
        out_shape=_sds((NQ, hr, cols), BF16),
        compiler_params=_params(("arbitrary", "arbitrary"), None),
    )(core, g, rbuf)


def _chip_sum_pair(own, rbuf, slots=(0, 1, 2, 3), prev=None):
    _, hr, cols = own.shape
    tr = _row_block(hr)
    a, b = slots[0], (slots[1] - slots[0] if len(slots) > 1 else 0)
    assert list(slots) == [a + b * k for k in range(len(slots))]

    def body(a_ref, b_ref, *rest):
        rest[-1][...] = (a_ref[...] + b_ref[...]).astype(BF16)

    spec = pl.BlockSpec((None, tr, cols), lambda k, i: (a + b * k, i, 0))
    return _call(
        body, name="chip_sum_pair",
        grid=(len(slots), hr // tr),
        in_specs=[spec, spec] + ([ANY] if prev is not None else []), out_specs=spec,
        out_shape=_sds((NQ, hr, cols), BF16),
        input_output_aliases={2: 0} if prev is not None else {},
        compiler_params=_params(("arbitrary", "arbitrary"), None),
    )(own, rbuf, *([prev] if prev is not None else []))


def _sum4(cs, rbuf, chip):
    _, hr, cols = rbuf.shape
    tr = _row_block(hr)

    def body(q_ref, c_ref, r1_ref, r2_ref, r3_ref, o_ref):
        acc = c_ref[...].astype(F32)
        for r in (r1_ref, r2_ref, r3_ref):
            acc = acc + r[...].astype(F32)
        o_ref[...] = acc

    slot = lambda k: pl.BlockSpec((None, tr, cols), lambda i, q: ((q[0] + k) % NQ, i, 0))
    return pl.pallas_call(
        body, name="sum4", interpret=False,
        grid_spec=pltpu.PrefetchScalarGridSpec(
            num_scalar_prefetch=1, grid=(hr // tr,),
            in_specs=[slot(0), slot(1), slot(2), slot(3)],
            out_specs=pl.BlockSpec((tr, cols), lambda i, q: (i, 0))),
        out_shape=_sds((hr, cols), F32),
        compiler_params=_params(("arbitrary",), None),
    )(chip, cs, rbuf, rbuf, rbuf)


def _adamw_halves(w, own, recv, m, v, side=None):
    rows, cols = w.shape
    hr = rows // 2
    tr = _row_block(hr, mult=8)
    nb = hr // tr

    def body(w_ref, own_ref, recv_ref, m_ref, v_ref, g_ref, d_ref, mo_ref, vo_ref):
        g = jnp.where(pl.program_id(0) == lax.axis_index("c"), own_ref[...], recv_ref[...])
        d, mn, vn = _adamw(w_ref[...], g, m_ref[...], v_ref[...])
        g_ref[...] = g
        d_ref[...] = d
        mo_ref[...] = mn
        vo_ref[...] = vn

    full = pl.BlockSpec((tr, cols), lambda h, i: (h * nb + i, 0))
    half = pl.BlockSpec((tr, cols), lambda h, i: (i, 0))
    step = lambda h, i: lambda: (pl.program_id(0) == h) & (pl.program_id(1) == i)
    return _side_call(
        body, side, (step(0, 0), None, step(1, nb - 1)), name="adamw_halves",
        grid=(2, nb), in_specs=[full, half, half, full, full], out_specs=[full] * 4,
        out_shape=[_sds((rows, cols), F32)] * 4, scratch_shapes=[],
        compiler_params=_params(("arbitrary", "arbitrary"), None),
        args=(w, own, recv, m, v))


def _cast_place(w, chip):
    rows, cols = w.shape
    tr = _row_block(rows)

    def body(q_ref, w_ref, o_ref):
        o_ref[...] = w_ref[...].astype(BF16)

    return pl.pallas_call(
        body, name="cast_place", interpret=False,
        grid_spec=pltpu.PrefetchScalarGridSpec(
            num_scalar_prefetch=1, grid=(rows // tr,),
            in_specs=[pl.BlockSpec((tr, cols), lambda i, q: (i, 0))],
            out_specs=pl.BlockSpec((None, tr, cols), lambda i, q: (q[0], i, 0))),
        out_shape=_sds((NQ, rows, cols), BF16),
        compiler_params=_params(("arbitrary",), None),
    )(chip, w)


def _ada_grad_adamw(cact_t, dmod_q, w, m, v, side=None):
    rows, cols = w.shape
    tc = 256
    assert cols % tc == 0

    def body(c_ref, d_ref, w_ref, m_ref, v_ref, g_ref, dl_ref, mo_ref, vo_ref):
        g = jnp.dot(c_ref[...], d_ref[...], precision=HIGHEST, preferred_element_type=F32)
        d, mn, vn = _adamw(w_ref[...], g, m_ref[...], v_ref[...])
        g_ref[...] = g
        dl_ref[...] = d
        mo_ref[...] = mn
        vo_ref[...] = vn

    spec = pl.BlockSpec((rows, tc), lambda i: (0, i))
    step = lambda s: lambda: pl.program_id(0) == s
    return _side_call(
        body, side, (step(0), None, step(cols // tc - 1)), name="ada_grad_adamw",
        grid=(cols // tc,),
        in_specs=[pl.BlockSpec((rows, 8), lambda i: (0, 0)), pl.BlockSpec((8, tc), lambda i: (0, i)),
                  spec, spec, spec],
        out_specs=[spec] * 4,
        out_shape=[_sds((rows, cols), F32)] * 4,
        scratch_shapes=[],
        compiler_params=_params(("arbitrary",), None),
        args=(cact_t, dmod_q, w, m, v))


def _me():
    x, y, c = lax.axis_index("x"), lax.axis_index("y"), lax.axis_index("c")
    return x, y, c


_OFFSETS7 = [(dx, dy, dc) for dx in (0, 1) for dy in (0, 1) for dc in (0, 1) if (dx, dy, dc) != (0, 0, 0)]
_CHIP_OFFSETS = [(1, 0), (0, 1), (1, 1)]


def _ada_fwd(c, w_ada_q, b_ada_q, side=None):
    ncol = w_ada_q.shape[1]

    def body(c_ref, w_ref, b_ref, cact_ref, modsel_ref, blk, gath, res, parts, send_sems, recv_sems, side_start=None):
        x, y, cc = _me()
        me = 4 * x + 2 * y + cc
        q = 2 * x + y
        cv = c_ref[...]
        ca = cv * jax.nn.sigmoid(cv)
        row = lax.broadcasted_iota(jnp.int32, (8, D), 0)
        blk[...] = jnp.where(row == me, jnp.broadcast_to(ca, (8, D)), 0.0)
        gath[me] = blk[...]
        sends = []
        for k, (dx, dy, dc) in enumerate(_OFFSETS7):
            cp = pltpu.make_async_remote_copy(blk, gath.at[me], send_sems.at[k], recv_sems.at[k],
                                              device_id=(x ^ dx, y ^ dy, cc ^ dc), device_id_type=MESH)
            cp.start()
            sends.append(cp)
        if side_start is not None:
            side_start()
        for cp in sends:
            cp.wait_recv()
        cact = gath[0]
        for d in range(1, N_DEV):
            cact = cact + gath[d]
        cact_ref[...] = cact
        res[...] = jnp.dot(cact, w_ref[...], precision=HIGHEST, preferred_element_type=F32) + b_ref[...]
        parts[q] = res[...]
        sends2 = []
        for k, (dx, dy) in enumerate(_CHIP_OFFSETS):
            cp = pltpu.make_async_remote_copy(res, parts.at[q], send_sems.at[7 + k], recv_sems.at[7 + k],
                                              device_id=(x ^ dx, y ^ dy, cc), device_id_type=MESH)
            cp.start()
            sends2.append(cp)
        for cp in sends2:
            cp.wait_recv()
        row2 = lax.broadcasted_iota(jnp.int32, (8, ncol), 0)
        out = jnp.zeros((8, ncol), F32)
        for s in range(NQ):
            mine = jnp.sum(jnp.where(row2 == me, parts[s], 0.0), axis=0, keepdims=True)
            out = out + jnp.where(row2 == s, jnp.broadcast_to(mine, (8, ncol)), 0.0)
        modsel_ref[...] = out
        for cp in sends + sends2:
            cp.wait_send()

    return _side_call(
        body, side, None, name="ada_fwd",
        in_specs=[VMEM, VMEM, VMEM], out_specs=[VMEM, VMEM],
        out_shape=[_sds((8, D), F32), _sds((8, ncol), F32)],
        scratch_shapes=[pltpu.VMEM((8, D), F32), pltpu.VMEM((N_DEV, 8, D), F32), pltpu.VMEM((8, ncol), F32),
                        pltpu.VMEM((NQ, 8, ncol), F32), pltpu.SemaphoreType.DMA((10,)), pltpu.SemaphoreType.DMA((10,))],
        compiler_params=_params(None, VMEM_LIMIT), start_in_body=side is not None,
        args=(c, w_ada_q, b_ada_q))


class _Side:
    def __init__(self, ins, out_shapes, aliases, nsem, start, mid=None, finish=None):
        self.ins, self.out_shapes, self.aliases, self.nsem = list(ins), list(out_shapes), dict(aliases), nsem
        self.start, self.mid, self.finish = start, mid, finish


def _join(*sides):
    ins, outs, aliases, offs, nsem = [], [], {}, [], 0
    for s in sides:
        offs.append((len(ins), len(outs), nsem))
        aliases.update({len(ins) + a: len(outs) + b for a, b in s.aliases.items()})
        ins += s.ins
        outs += s.out_shapes
        nsem += s.nsem

    def hook(name):
        def run(i, o, ss, rs, base):
            for s, (io, oo, so) in zip(sides, offs):
                fn = getattr(s, name)
                if fn is not None:
                    fn(i[io:io + len(s.ins)], o[oo:oo + len(s.out_shapes)], ss, rs, base + so)
        return run

    return _Side(ins, outs, aliases, nsem, hook("start"), hook("mid"), hook("finish"))


def _side_call(body, side, when, *, name, in_specs, out_specs, out_shape, scratch_shapes, args, aliases=None,
               start_in_body=False, **kw):
    n_in, n_out = len(in_specs), len(out_specs)
    aliases = dict(aliases or {})
    if side is None:
        return _call(body, name=name, in_specs=in_specs, out_specs=out_specs, out_shape=out_shape,
                     scratch_shapes=scratch_shapes, input_output_aliases=aliases, **kw)(*args), []
    ns_in, ns_out = len(side.ins), len(side.out_shapes)

    def hook(fn, k, operands):
        if fn is None:
            return
        if when is None:
            fn(*operands, 0)
        elif when[k] is not None:
            pl.when(when[k]())(functools.partial(fn, *operands, 0))

    def wrapped(*refs):
        ins, s_ins = refs[:n_in], refs[n_in:n_in + ns_in]
        o0 = n_in + ns_in
        outs, s_outs = refs[o0:o0 + n_out], refs[o0 + n_out:o0 + n_out + ns_out]
        rest = refs[o0 + n_out + ns_out:]
        scratch, operands = rest[:-2], (s_ins, s_outs, rest[-2], rest[-1])
        if start_in_body:
            body(*ins, *outs, *scratch, side_start=functools.partial(hook, side.start, 0, operands))
        else:
            hook(side.start, 0, operands)
            body(*ins, *outs, *scratch)
        hook(side.mid, 1, operands)
        hook(side.finish, 2, operands)

    res = _call(
        wrapped, name=name,
        in_specs=list(in_specs) + [ANY] * ns_in, out_specs=list(out_specs) + [ANY] * ns_out,
        out_shape=list(out_shape) + side.out_shapes,
        scratch_shapes=list(scratch_shapes) + [pltpu.SemaphoreType.DMA((side.nsem,)),
                                               pltpu.SemaphoreType.DMA((side.nsem,))],
        input_output_aliases={**aliases, **{n_in + a: n_out + b for a, b in side.aliases.items()}},
        **kw)(*args, *side.ins)
    return res[:n_out], res[n_out:]


def _run_side(side, name):
    return _side_call(lambda: None, side, None, name=name, in_specs=[], out_specs=[], out_shape=[],
                      scratch_shapes=[], args=[])[1]


def _remote(src, dst, ss, rs, k, dev):
    return pltpu.make_async_remote_copy(src, dst, ss.at[k], rs.at[k], device_id=dev, device_id_type=MESH)


def _gather_side(bufs):
    n = len(bufs)

    def walk(outs, half):
        x, y, cc = _me()
        for w in range(n):
            hr = outs[w].shape[1] // 2
            rows = pl.ds((cc if half == "mine" else 1 - cc) * hr, hr)
            for j, (dx, dy) in enumerate(_CHIP_OFFSETS):
                yield w, j, (x ^ dx, y ^ dy, cc), outs[w].at[2 * (x ^ dx) + (y ^ dy), rows], outs[w].at[2 * x + y, rows]

    def start(ins, outs, ss, rs, b):
        for w, j, peer, _, own in walk(outs, "mine"):
            _remote(own, own, ss, rs, b + 6 * w + j, peer).start()

    def mid(ins, outs, ss, rs, b):
        x, y, cc = _me()
        for w, j, peer, land, _ in walk(outs, "mine"):
            _remote(land, land, ss, rs, b + 6 * w + j, peer).wait_recv()
            _remote(land, land, ss, rs, b + 6 * w + 3 + j, (x, y, 1 - cc)).start()

    def finish(ins, outs, ss, rs, b):
        x, y, cc = _me()
        for w, j, _, land, _ in walk(outs, "other"):
            _remote(land, land, ss, rs, b + 6 * w + 3 + j, (x, y, 1 - cc)).wait_recv()
        for w, j, peer, land, own in walk(outs, "mine"):
            _remote(own, own, ss, rs, b + 6 * w + j, peer).wait_send()
            _remote(land, land, ss, rs, b + 6 * w + 3 + j, (x, y, 1 - cc)).wait_send()

    return _Side(bufs, [_sds(tuple(w.shape), w.dtype) for w in bufs], {i: i for i in range(n)}, 6 * n,
                 start, mid, finish)


def _copies_side(ins, out_shapes, nsem, copies):
    def start(*a):
        for cp in copies(*a):
            cp.start()

    def finish(*a):
        for cp in copies(*a):
            cp.wait()

    return _Side(ins, out_shapes, {}, nsem, start, None, finish)


def _swap_side(gs):
    def copies(ins, outs, ss, rs, b):
        x, y, cc = _me()
        return [_remote(ins[w].at[:, 1 - cc], outs[w], ss, rs, b + w, (x, y, 1 - cc)) for w in range(len(gs))]

    return _copies_side(gs, [_sds((NQ,) + tuple(g.shape[2:]), F32) for g in gs], len(gs), copies)


def _exchange_side(cs, slots=None, prev=None):
    n = len(cs)
    slots = slots or [(0, 1, 2, 3)] * n

    def among(chip, allowed):
        hit = chip == allowed[0]
        for s in allowed[1:]:
            hit = hit | (chip == s)
        return hit

    def each(ins, outs, ss, rs, b, do_send, do_recv):
        x, y, cc = _me()
        q = 2 * x + y
        for w in range(n):
            for j, (dx, dy) in enumerate(_CHIP_OFFSETS):
                pq = 2 * (x ^ dx) + (y ^ dy)
                cp = _remote(ins[w].at[pq], outs[w].at[q], ss, rs, b + 3 * w + j, (x ^ dx, y ^ dy, cc))
                if do_send is not None:
                    pl.when(among(pq, slots[w]))(functools.partial(do_send, cp))
                if do_recv is not None:
                    pl.when(among(q, slots[w]))(functools.partial(do_recv, cp))

    def start(ins, outs, ss, rs, b):
        each(ins, outs, ss, rs, b, lambda cp: cp.start(), None)

    def finish(ins, outs, ss, rs, b):
        each(ins, outs, ss, rs, b, lambda cp: cp.wait_send(), lambda cp: cp.wait_recv())

    ins = list(cs) + (list(prev) if prev is not None else [])
    aliases = {n + w: w for w in range(n)} if prev is not None else {}
    return _Side(ins, [_sds(tuple(c.shape), c.dtype) for c in cs], aliases, 3 * n, start, None, finish)


def _share_side(fs):
    def copies(ins, outs, ss, rs, b):
        x, y, cc = _me()
        return [_remote(ins[w], outs[w], ss, rs, b + w, (x, y, 1 - cc)) for w in range(len(fs))]

    return _copies_side(fs, [_sds(tuple(f.shape), F32) for f in fs], len(fs), copies)


def _small_allreduce_adamw(g, w, m, v, nd):
    rows = g.shape[0]
    nr = rows - nd
    hr = nr // 2
    assert nd % 8 == 0 and hr % 8 == 0

    def body(g_ref, w_ref, m_ref, v_ref, gs_ref, d_ref, mo_ref, vo_ref, gath, sib, csum, slots, tot, ss, rs):
        x, y, cc = _me()
        me = 4 * x + 2 * y + cc
        q = 2 * x + y
        sibling = (x, y, 1 - cc)
        dm = g_ref.at[pl.ds(0, nd)]
        gath[me] = g_ref[0:nd, :]
        to_all = [_remote(dm, gath.at[me], ss, rs, k, (x ^ dx, y ^ dy, cc ^ dc)) for k, (dx, dy, dc) in enumerate(_OFFSETS7)]
        to_sib = _remote(g_ref.at[pl.ds(nd, nr)], sib, ss, rs, 7, sibling)
        for cp in to_all + [to_sib]:
            cp.start()
        to_sib.wait_recv()
        csum[...] = g_ref[nd:, :] + sib[...]
        mine = pl.ds(pl.multiple_of(cc * hr, 8), hr)
        slots[q] = csum[mine, :]
        to_chips = [_remote(csum.at[mine], slots.at[q], ss, rs, 8 + j, (x ^ dx, y ^ dy, cc))
                    for j, (dx, dy) in enumerate(_CHIP_OFFSETS)]
        for cp in to_chips:
            cp.start()
        for cp in to_chips:
            cp.wait_recv()
        tot[mine, :] = (slots[0] + slots[1]) + (slots[2] + slots[3])
        halves = _remote(tot.at[mine], tot.at[mine], ss, rs, 11, sibling)
        halves.start()
        for cp in to_all:
            cp.wait_recv()
        dsum = gath[0]
        for dev in range(1, N_DEV):
            dsum = dsum + gath[dev]
        halves.wait_recv()
        for lo, n, total in ((0, nd, dsum), (nd, nr, tot[...])):
            gs_ref[lo:lo + n, :] = total
            d, mn, vn = _adamw(w_ref[lo:lo + n, :], total, m_ref[lo:lo + n, :], v_ref[lo:lo + n, :])
            d_ref[lo:lo + n, :] = d
            mo_ref[lo:lo + n, :] = mn
            vo_ref[lo:lo + n, :] = vn
        for cp in to_all + [to_sib, halves] + to_chips:
            cp.wait_send()

    return _call(
        body, name="small_allreduce_adamw",
        in_specs=[VMEM] * 4, out_specs=[VMEM] * 5,
        out_shape=[_sds((rows, LANE), F32)] * 4 + [_sds((N_DEV, nd, LANE), F32)],
        scratch_shapes=[pltpu.VMEM((nr, LANE), F32), pltpu.VMEM((nr, LANE), F32), pltpu.VMEM((NQ, hr, LANE), F32),
                        pltpu.VMEM((nr, LANE), F32), pltpu.SemaphoreType.DMA((12,)), pltpu.SemaphoreType.DMA((12,))],
        compiler_params=_params(None, VMEM_LIMIT),
    )(g, w, m, v)


_SMALL = ["b_ada", "norm_ffn1_g", "norm_mix_g", "pool_scale", "gmlp_ln_g", "gmlp_ln_b", "b_spatial",
          "norm_ffn2_g", "norm_final_g", "w_pool", "w_spatial"]


def _pack(parts):
    blocks, layout, r0 = [], {}, 0
    for name in _SMALL:
        a = parts[name]
        n = a.size
        rows = -(-n // LANE)
        rows8 = -(-rows // 8) * 8
        flat = a.reshape(-1).astype(F32)
        if rows8 * LANE != n:
            flat = jnp.concatenate([flat, jnp.zeros((rows8 * LANE - n,), F32)])
        blocks.append(flat.reshape(rows8, LANE))
        layout[name] = (r0, n, a.shape)
        r0 += rows8
    return jnp.concatenate(blocks, axis=0), layout


def _unpack(packed, layout):
    out = {}
    for name, (r0, n, shape) in layout.items():
        rows = -(-n // LANE)
        out[name] = packed[r0:r0 + rows].reshape(-1)[:n].reshape(shape)
    return out


def _modv(mod9, sub, gain):
    rows = jnp.concatenate([mod9[3 * sub:3 * sub + 3], gain.reshape(1, D), jnp.zeros((4, D), F32)], axis=0)
    return rows


_BIG = ["ffn1_w_in", "ffn1_w_out", "w_mix_in", "w_mix_out", "ffn2_w_in", "ffn2_w_out"]


def kernel(x, c, w_ada, b_ada, norm_ffn1_g, ffn1_w_in, ffn1_w_out, norm_mix_g, w_mix_in, w_pool, pool_scale, gmlp_ln_g, gmlp_ln_b, w_spatial, b_spatial, w_mix_out, norm_ffn2_g, ffn2_w_in, ffn2_w_out, norm_final_g, loss_target, m_w_ada, m_b_ada, m_norm_ffn1_g, m_ffn1_w_in, m_ffn1_w_out, m_norm_mix_g, m_w_mix_in, m_w_pool, m_pool_scale, m_gmlp_ln_g, m_gmlp_ln_b, m_w_spatial, m_b_spatial, m_w_mix_out, m_norm_ffn2_g, m_ffn2_w_in, m_ffn2_w_out, m_norm_final_g, v_w_ada, v_b_ada, v_norm_ffn1_g, v_ffn1_w_in, v_ffn1_w_out, v_norm_mix_g, v_w_mix_in, v_w_pool, v_pool_scale, v_gmlp_ln_g, v_gmlp_ln_b, v_w_spatial, v_b_spatial, v_w_mix_out, v_norm_ffn2_g, v_ffn2_w_in, v_ffn2_w_out, v_norm_final_g):
    names = ["w_ada", "b_ada", "norm_ffn1_g", "ffn1_w_in", "ffn1_w_out", "norm_mix_g", "w_mix_in", "w_pool",
             "pool_scale", "gmlp_ln_g", "gmlp_ln_b", "w_spatial", "b_spatial", "w_mix_out", "norm_ffn2_g",
             "ffn2_w_in", "ffn2_w_out", "norm_final_g"]
    W = dict(zip(names, [w_ada, b_ada, norm_ffn1_g, ffn1_w_in, ffn1_w_out, norm_mix_g, w_mix_in, w_pool, pool_scale,
                         gmlp_ln_g, gmlp_ln_b, w_spatial, b_spatial, w_mix_out, norm_ffn2_g, ffn2_w_in, ffn2_w_out,
                         norm_final_g]))
    M = dict(zip(names, [m_w_ada, m_b_ada, m_norm_ffn1_g, m_ffn1_w_in, m_ffn1_w_out, m_norm_mix_g, m_w_mix_in, m_w_pool,
                         m_pool_scale, m_gmlp_ln_g, m_gmlp_ln_b, m_w_spatial, m_b_spatial, m_w_mix_out, m_norm_ffn2_g,
                         m_ffn2_w_in, m_ffn2_w_out, m_norm_final_g]))
    V = dict(zip(names, [v_w_ada, v_b_ada, v_norm_ffn1_g, v_ffn1_w_in, v_ffn1_w_out, v_norm_mix_g, v_w_mix_in, v_w_pool,
                         v_pool_scale, v_gmlp_ln_g, v_gmlp_ln_b, v_w_spatial, v_b_spatial, v_w_mix_out, v_norm_ffn2_g,
                         v_ffn2_w_in, v_ffn2_w_out, v_norm_final_g]))

    xi, yi, ci = _me()
    q = 2 * xi + yi
    core = ci.astype(jnp.int32).reshape(1)

    chip = q.astype(jnp.int32).reshape(1)
    place = lambda n: _cast_place(W[n][0], chip)

    ncol = w_ada.shape[2]
    b_q = lax.dynamic_slice_in_dim(b_ada, q * ncol, ncol, axis=1)
    (cact_all, modsel), (win1, wout1) = _ada_fwd(
        c, w_ada[0], b_q, side=_gather_side([place("ffn1_w_in"), place("ffn1_w_out")]))
    mod9 = modsel[:NQ].reshape(9, D)
    xs, target = x[0], loss_target[0]
    mv1 = _modv(mod9, 0, norm_ffn1_g[0])
    mv2 = _modv(mod9, 1, norm_mix_g[0])
    mv3 = _modv(mod9, 2, norm_ffn2_g[0])
    wcat, wtcat, bias = _prep_spatial(w_spatial[0], b_spatial[0].T)
    wpool = w_pool[0].astype(BF16)
    vecs = jnp.concatenate([pool_scale, gmlp_ln_g, gmlp_ln_b, jnp.zeros((5, DP), F32)], axis=0)
    gf = jnp.concatenate([norm_final_g.reshape(1, D), jnp.zeros((7, D), F32)], axis=0)

    later =["w_mix_in", "w_mix_out", "ffn2_w_in", "ffn2_w_out"]
    (x1, g1s, u1s), got = _ffn_fwd(xs, mv1, win1, wout1.reshape(2, CH, D), side=_gather_side([place(n) for n in later]))
    wmi, wmo, win2, wout2 = got
    wmi = jnp.transpose(wmi, (1, 0, 2)).reshape(D, DPROJ)
    wmo = wmo.reshape(DP + DG, D)
    x2, pooled, zb = _mix_fwd(x1, mv2, wmi, wpool, vecs, wcat, bias, wmo)
    (dx3, g3s, u3s, loss_blk, dgf), _ = _ffn_fwd(x2, mv3, win2, wout2.reshape(2, CH, D), head=(target, gf))
    loss = lax.psum(loss_blk[0, 0], ("x", "y", "c"))

    wo1, wo2 = wout1.reshape(2, CH, D), wout2.reshape(2, CH, D)
    (dx2, oin2, oout2, rin2, rout2, vec3), _ = _ffn_bwd(x2, dx3, g3s, u3s, mv3, win2, wo2)
    cs2 = [_chip_sum_pair(oin2, rin2), _chip_sum_pair(oout2, rout2)]
    (dx1, dwmi, dwmo, dwpool, dwsp, dbsp, v512, vec2), ex2 = _mix_bwd(
        x1, dx2, pooled, zb, mv2, wmi, wpool, vecs, wcat, wtcat, bias, wmo, side=_exchange_side(cs2))
    half2 = [_sum4(cs, e, chip) for cs, e in zip(cs2, ex2)]
    qcols = w_mix_in.shape[2]
    vmix = [jnp.transpose(dwmi.reshape(D, NQ, qcols), (1, 0, 2)).reshape(NQ, 2, D // 2, qcols),
            dwmo.reshape(NQ, 2, (DP + DG) // 8, D)]
    first1, got = _ffn_bwd_pass(0, xs, dx1, g1s, u1s, mv1, win1, wo1,
                                side=_join(_swap_side(vmix), _share_side(half2)))
    sibmix, other2 = got[:2], got[2:]
    early, late = [(0, 2), (0, 1)], [(1, 3), (2, 3)]
    cs_a = [_chip_sum_pair(first1[1], first1[3], early[0]), _chip_sum_pair(first1[2], first1[4], early[1])]
    (grad_x, oin1, oout1, rin1, rout1, vec1), ex_a = _ffn_bwd_pass(
        1, xs, dx1, g1s, u1s, mv1, win1, wo1, prev=first1[:5], side=_exchange_side(cs_a, early))
    vec1 = first1[5] + vec1
    cs1 = [_chip_sum(g, r, core) for g, r in zip(vmix, sibmix)] + [
        _chip_sum_pair(oin1, rin1, late[0], prev=cs_a[0]), _chip_sum_pair(oout1, rout1, late[1], prev=cs_a[1])]

    dmod = jnp.concatenate([vec1[0:3], vec2[0:3], vec3[0:3]], axis=0)
    grads = dict(
        b_ada=dmod.reshape(1, 9 * D), norm_ffn1_g=vec1[3:4], norm_mix_g=vec2[3:4], norm_ffn2_g=vec3[3:4],
        pool_scale=v512[0:1], gmlp_ln_g=v512[1:2], gmlp_ln_b=v512[2:3], b_spatial=dbsp[None],
        norm_final_g=dgf[0], w_pool=dwpool[None], w_spatial=dwsp[None])

    gp, layout = _pack({n: grads[n] for n in _SMALL})
    wp, _ = _pack({n: W[n] for n in _SMALL})
    mp, _ = _pack({n: M[n] for n in _SMALL})
    vp, _ = _pack({n: V[n] for n in _SMALL})
    r0, nb, _ = layout["b_ada"]
    assert r0 == 0
    out_g, out_d, out_m, out_v = {}, {}, {}, {}
    gs, dl, mo, vo, gath = _small_allreduce_adamw(gp, wp, mp, vp, nb // LANE)
    for packed, dst in ((gs, out_g), (dl, out_d), (mo, out_m), (vo, out_v)):
        for n, a in _unpack(packed, layout).items():
            dst[n] = a.reshape(W[n].shape)

    def update(n, own, recv, side=None):
        (g2, d, mn, vn), got = _adamw_halves(W[n][0], own, recv, M[n][0], V[n][0], side=side)
        out_g[n], out_d[n], out_m[n], out_v[n] = g2[None], d[None], mn[None], vn[None]
        return got

    ex_mix = update("ffn2_w_in", half2[0], other2[0], side=_exchange_side(cs1[0:2]))
    ex_out1 = update("ffn2_w_out", half2[1], other2[1], side=_exchange_side(cs1[3:4], late[1:2], prev=ex_a[1:2]))
    dmod_q = lax.dynamic_slice_in_dim(gath.reshape(N_DEV, nb), q * ncol, ncol, axis=1)
    (ga, da, ma, va), ex_in1 = _ada_grad_adamw(cact_all.T, dmod_q, w_ada[0], m_w_ada[0], v_w_ada[0],
                                              side=_exchange_side(cs1[2:3], late[0:1], prev=ex_a[0:1]))
    out_g["w_ada"], out_d["w_ada"], out_m["w_ada"], out_v["w_ada"] = ga[None], da[None], ma[None], va[None]
    ex1 = list(ex_mix) + list(ex_in1) + list(ex_out1)
    half1 = [_sum4(cs, e, chip) for cs, e in zip(cs1, ex1)]
    other1 = _run_side(_share_side(half1), "sibling_share")
    for n, own, recv in zip(["w_mix_in", "w_mix_out", "ffn1_w_in", "ffn1_w_out"], half1, other1):
        update(n, own, recv)

    return (loss, grad_x[None], *[out_g[n] for n in names], *[out_d[n] for n in names],
            *[out_m[n] for n in names], *[out_v[n] for n in names])
```

```python
import functools
import math

import jax
import jax.numpy as jnp
from jax import lax
from jax.experimental import pallas as pl
from jax.experimental.pallas import tpu as pltpu

F32 = jnp.float32
BF16 = jnp.bfloat16
MESH = pl.DeviceIdType.MESH
HIGHEST = lax.Precision.HIGHEST

EPS = 1e-6
D = 1024
DFF = 2816
CH = DFF // 2
NQ = 4
DP = 512
DG = 512
DPROJ = DP + 2 * DG
POOL_WINDOWS = (2, 4, 8, 16)
HALO = 16
CHUNK = 128
LANE = 128
N_DEV = 8

ADAM_LR = 0.001
ADAM_B1 = 0.9
ADAM_B2 = 0.999
ADAM_EPS = 1e-08
ADAM_WD = 0.01
ADAM_STEP = 10

VMEM_LIMIT = 62 * 1024 * 1024

TM_FFN_FWD = 512
TM_FFN_BWD = 512
TM_MIX = 256


def _call(body, **kw):
    return pl.pallas_call(body, interpret=False, **kw)


def _params(sem=None, vmem=None):
    return pltpu.CompilerParams(dimension_semantics=sem, vmem_limit_bytes=vmem)


def _sds(shape, dtype):
    return jax.ShapeDtypeStruct(shape, dtype)


ANY = pl.BlockSpec(memory_space=pl.ANY)
VMEM = pl.BlockSpec(memory_space=pltpu.VMEM)
SMEM = pl.BlockSpec(memory_space=pltpu.SMEM)


def _norm_mod(x, gn, sc, sh):
    r = lax.rsqrt(jnp.mean(x * x, axis=-1, keepdims=True) + EPS)
    xn = x * r
    hp = xn * gn
    return r, xn, hp, hp * (1.0 + sc) + sh


def _norm_mod_bwd(dh, r, xn, hp, gn, sc):
    one_sc = 1.0 + sc
    dsh = jnp.sum(dh, axis=0, keepdims=True)
    dsc = jnp.sum(dh * hp, axis=0, keepdims=True)
    dgn = jnp.sum(dh * one_sc * xn, axis=0, keepdims=True)
    dxn = dh * (gn * one_sc)
    dx = r * (dxn - xn * jnp.mean(dxn * xn, axis=-1, keepdims=True))
    return dsh, dsc, dgn, dx


def _dot(a, b):
    return jnp.dot(a, b, preferred_element_type=F32)


def _dot_nt(a, b):
    return lax.dot_general(a, b, (((1,), (1,)), ((), ())), preferred_element_type=F32)


def _dot_tn(a, b):
    return lax.dot_general(a, b, (((0,), (0,)), ((), ())), preferred_element_type=F32)


_GELU_C = math.sqrt(2.0 / math.pi)
_GELU_A = 0.044715


def _gelu_fwd_bwd(x):
    x2 = x * x
    t = jnp.tanh(_GELU_C * (x + _GELU_A * x * x2))
    g = 0.5 * x * (1.0 + t)
    dg = 0.5 * (1.0 + t) + 0.5 * x * (1.0 - t * t) * (_GELU_C * (1.0 + 3.0 * _GELU_A * x2))
    return g, dg


def _adamw(w, g, m, v):
    m = ADAM_B1 * m + (1.0 - ADAM_B1) * g
    v = ADAM_B2 * v + (1.0 - ADAM_B2) * (g * g)
    m_hat = m / (1.0 - ADAM_B1 ** ADAM_STEP)
    v_hat = v / (1.0 - ADAM_B2 ** ADAM_STEP)
    delta = -ADAM_LR * (m_hat / (jnp.sqrt(v_hat) + ADAM_EPS) + ADAM_WD * w)
    return delta, m, v


def _row_block(rows, cap=256, mult=16):
    best = None
    for t in range(mult, min(rows, cap) + 1, mult):
        if rows % t == 0:
            best = t
    assert best is not None, rows
    return best


def _head_math(x, target, gf):
    r = lax.rsqrt(jnp.mean(x * x, axis=-1, keepdims=True) + EPS)
    xn = x * r
    err = xn * gf - target
    dy = err * (1.0 / D)
    dxn = dy * gf
    dx = r * (dxn - xn * jnp.mean(dxn * xn, axis=-1, keepdims=True))
    return (0.5 / D) * jnp.sum(err * err), jnp.sum(dy * xn, axis=0, keepdims=True), dx


def _ffn_fwd(x, modv, win, wout, side=None, head=None):
    S = x.shape[0]
    tm = TM_FFN_FWD
    nt = S // tm

    def body(*refs):
        if head is None:
            x_ref, mod_ref, wg_ref, wu_ref, wo_ref, xo_ref, gs_ref, us_ref, acc_scr = refs
        else:
            (x_ref, mod_ref, wg_ref, wu_ref, wo_ref, t_ref, gf_ref,
             xo_ref, gs_ref, us_ref, loss_ref, dgf_ref, acc_scr) = refs

        @pl.when((pl.program_id(0) == 0) & (pl.program_id(1) == 0))
        def _():
            acc_scr[...] = jnp.zeros_like(acc_scr)
            if head is not None:
                loss_ref[...] = jnp.zeros_like(loss_ref)
                dgf_ref[...] = jnp.zeros_like(dgf_ref)

        j = pl.program_id(1)
        h = _norm_mod(x_ref[...], mod_ref[3:4, :], mod_ref[1:2, :], mod_ref[0:1, :])[3].astype(BF16)
        g = _dot(h, wg_ref[...]).astype(BF16)
        u = _dot(h, wu_ref[...]).astype(BF16)
        gs_ref[...] = g
        us_ref[...] = u
        gf = g.astype(F32)
        a = (gf * jax.nn.sigmoid(gf) * u.astype(F32)).astype(BF16)
        acc = jnp.where(j == 0, 0.0, acc_scr[...]) + _dot(a, wo_ref[...])
        acc_scr[...] = acc
        xo = x_ref[...] + (0.5 * mod_ref[2:3, :]) * acc
        if head is None:
            xo_ref[...] = xo
        else:
            @pl.when(j == 1)
            def _():
                loss, dgf, dx = _head_math(xo, t_ref[...], gf_ref[0:1, :])
                loss_ref[...] += loss
                dgf_ref[0:1, :] += dgf
                xo_ref[...] = dx

    step = lambda i, j: lambda: (pl.program_id(0) == i) & (pl.program_id(1) == j)
    tile = pl.BlockSpec((tm, D), lambda i, j: (i, 0))
    const = lambda shape: pl.BlockSpec(shape, lambda i, j: (0, 0))
    chunk = pl.BlockSpec((tm, CH), lambda i, j: (i, j))
    in_specs = [tile, const((8, D)), pl.BlockSpec((None, D, CH), lambda i, j: (j, 0, 0)),
                pl.BlockSpec((None, D, CH), lambda i, j: (2 + j, 0, 0)), pl.BlockSpec((None, CH, D), lambda i, j: (j, 0, 0))]
    out_specs = [tile, chunk, chunk]
    out_shape = [_sds((S, D), F32), _sds((S, DFF), BF16), _sds((S, DFF), BF16)]
    args = (x, modv, win, win, wout)
    if head is not None:
        in_specs += [tile, const((8, D))]
        out_specs += [const((8, LANE)), const((8, D))]
        out_shape += [_sds((8, LANE), F32), _sds((8, D), F32)]
        args += tuple(head)
    return _side_call(
        body, side, (step(0, 0), step((7 * nt) // 10, 0), step(nt - 1, 1)), name="ffn_fwd",
        grid=(nt, 2), in_specs=in_specs, out_specs=out_specs, out_shape=out_shape,
        scratch_shapes=[pltpu.VMEM((tm, D), F32)],
        compiler_params=_params(("arbitrary", "arbitrary"), VMEM_LIMIT),
        args=args)


def _ffn_bwd_pass(jj, x, dxo, gs, us, modv, win, wout, prev=None, side=None):
    S = x.shape[0]
    tm = TM_FFN_BWD
    nsub = tm // 256
    nt = S // tm
    hi, ho = D // 2, CH // 4
    last = prev is not None
    assert last == (jj == 1)

    def body(*refs):
        x_ref, dxo_ref, gs_ref, us_ref, mod_ref, wg_ref, wu_ref, wo_ref = refs[:8]
        k = 13 if last else 8
        out_ref, dwin_ref, dwout_ref, rwin_ref, rwout_ref, vec_ref = refs[k:k + 6]
        accg, accu, accw, sems, fsend, frecv = refs[k + 6:]
        i = pl.program_id(0)

        @pl.when(i == 0)
        def _():
            accg[...] = jnp.zeros_like(accg)
            accu[...] = jnp.zeros_like(accu)
            accw[...] = jnp.zeros_like(accw)
            vec_ref[...] = jnp.zeros_like(vec_ref)

        gn, sc, sh, gate = mod_ref[3:4, :], mod_ref[1:2, :], mod_ref[0:1, :], mod_ref[2:3, :]

        parts = []
        for s in range(nsub):
            rs = slice(s * (tm // nsub), (s + 1) * (tm // nsub))
            r, xn, hp, h = _norm_mod(x_ref[rs, :], gn, sc, sh)
            dxo = dxo_ref[rs, :]
            dy = (dxo * (0.5 * gate)).astype(BF16)
            g = gs_ref[rs, :].astype(F32)
            u = us_ref[rs, :].astype(F32)
            sig = jax.nn.sigmoid(g)
            sl = g * sig
            a = (sl * u).astype(BF16)
            da = _dot_nt(dy, wo_ref[...])
            dg = (da * u * (sig * (1.0 + g * (1.0 - sig)))).astype(BF16)
            du = (da * sl).astype(BF16)
            dhp = _dot_nt(dg, wg_ref[...]) + _dot_nt(du, wu_ref[...])
            parts.append((h.astype(BF16), a, dg, du, dxo.astype(BF16)))
            if last:
                dsh, dsc, dgn, dxin = _norm_mod_bwd(refs[8][rs, :] + dhp, r, xn, hp, gn, sc)
                vec_ref[0:1, :] += dsh
                vec_ref[1:2, :] += dsc
                vec_ref[3:4, :] += dgn
                out_ref[rs, :] = dxo + dxin
            else:
                out_ref[rs, :] = dhp

        hb, a, dg, du, dxb = [jnp.concatenate(p, axis=0) if nsub > 1 else p[0] for p in zip(*parts)]
        accw[...] += _dot_tn(a, dxb)
        accg[...] += _dot_tn(hb, dg)
        accu[...] += _dot_tn(hb, du)

        @pl.when(i == nt - 1)
        def _():
            gw = accw[...]
            vec_ref[2:3, :] += 0.5 * jnp.sum(wo_ref[...].astype(F32) * gw, axis=0, keepdims=True)
            accw[...] = gw * (0.5 * gate)
            mx, my, cc = _me()
            rows = lambda base, n, c: pl.ds(base + c * n, n)
            pieces = [(accg, 0, hi, dwin_ref, rwin_ref, jj), (accu, 0, hi, dwin_ref, rwin_ref, 2 + jj),
                      (accw, 0, ho, dwout_ref, rwout_ref, 2 * jj), (accw, 2 * ho, ho, dwout_ref, rwout_ref, 2 * jj + 1)]
            loc = [pltpu.make_async_copy(acc.at[rows(base, n, cc)], own.at[slot], sems.at[p])
                   for p, (acc, base, n, own, _, slot) in enumerate(pieces)]
            rem = [pltpu.make_async_remote_copy(acc.at[rows(base, n, 1 - cc)], sib.at[slot], fsend.at[p], frecv.at[p],
                                                device_id=(mx, my, 1 - cc), device_id_type=MESH)
                   for p, (acc, base, n, _, sib, slot) in enumerate(pieces)]
            for cp in loc + rem:
                cp.start()
            for cp in loc:
                cp.wait()
            for cp in rem:
                cp.wait()

    once = pl.Buffered(1)
    tile = pl.BlockSpec((tm, D), lambda i: (i, 0))
    chunk = pl.BlockSpec((tm, CH), lambda i: (i, jj))
    in_specs = [tile, tile, chunk, chunk, pl.BlockSpec((8, D), lambda i: (0, 0)),
                pl.BlockSpec((None, D, CH), lambda i: (jj, 0, 0), pipeline_mode=once),
                pl.BlockSpec((None, D, CH), lambda i: (2 + jj, 0, 0), pipeline_mode=once),
                pl.BlockSpec((None, CH, D), lambda i: (jj, 0, 0), pipeline_mode=once)]
    args = (x, dxo, gs, us, modv, win, win, wout)
    if last:
        in_specs += [tile, ANY, ANY, ANY, ANY]
        args += tuple(prev)
    step = lambda s: lambda: pl.program_id(0) == s
    return _side_call(
        body, side, (step(0), None, step(nt - 1)), name="ffn_bwd",
        grid=(nt,), in_specs=in_specs,
        out_specs=[tile, ANY, ANY, ANY, ANY, pl.BlockSpec((8, D), lambda i: (0, 0))],
        out_shape=[_sds((S, D), F32), _sds((NQ, hi, CH), F32), _sds((NQ, ho, D), F32), _sds((NQ, hi, CH), F32),
                   _sds((NQ, ho, D), F32), _sds((8, D), F32)],
        scratch_shapes=[pltpu.VMEM((D, CH), F32), pltpu.VMEM((D, CH), F32), pltpu.VMEM((CH, D), F32),
                        pltpu.SemaphoreType.DMA((4,)), pltpu.SemaphoreType.DMA((4,)), pltpu.SemaphoreType.DMA((4,))],
        aliases={9 + p: 1 + p for p in range(4)} if last else {},
        compiler_params=_params(("arbitrary",), VMEM_LIMIT),
        args=args)


def _ffn_bwd(x, dxo, gs, us, modv, win, wout, side=None):
    first, extra = _ffn_bwd_pass(0, x, dxo, gs, us, modv, win, wout, side=side)
    (dx, dwin, dwout, rwin, rwout, vec), _ = _ffn_bwd_pass(1, x, dxo, gs, us, modv, win, wout, prev=first[:5])
    return (dx, dwin, dwout, rwin, rwout, first[5] + vec), extra


def _prep_spatial(w_spatial, b_spatial_t):
    def body(w_ref, b_ref, wcat_ref, wtcat_ref, bias_ref):
        row = lax.broadcasted_iota(jnp.int32, (CHUNK, CHUNK), 0)
        col = lax.broadcasted_iota(jnp.int32, (CHUNK, CHUNK), 1)
        tril = col <= row
        for p in range(4):
            wa = jnp.where(tril, w_ref[2 * p], 0.0)
            wb = jnp.where(tril, w_ref[2 * p + 1], 0.0)
            wcat_ref[p] = jnp.concatenate([wa, wb], axis=1).astype(BF16)
            wtcat_ref[p] = jnp.concatenate([wa.T, wb.T], axis=1).astype(BF16)
        head = lax.broadcasted_iota(jnp.int32, (8, DG), 0)
        ch = lax.broadcasted_iota(jnp.int32, (8, DG), 1)
        spread = jnp.where(ch // 64 == head, 1.0, 0.0).astype(F32)
        bias_ref[...] = jnp.dot(b_ref[...], spread, precision=HIGHEST, preferred_element_type=F32)

    return _call(
        body, name="prep_spatial",
        in_specs=[VMEM, VMEM], out_specs=[VMEM, VMEM, VMEM],
        out_shape=[_sds((4, CHUNK, 2 * CHUNK), BF16), _sds((4, CHUNK, 2 * CHUNK), BF16), _sds((CHUNK, DG), F32)],
    )(w_spatial, b_spatial_t)


def _pair_rhs(blocks):
    lane = lax.broadcasted_iota(jnp.int32, (CHUNK, LANE), 1)
    lo = lane < 64
    top = jnp.concatenate([jnp.where(lo, b, 0.0) for b in blocks], axis=1)
    bot = jnp.concatenate([jnp.where(lo, 0.0, b) for b in blocks], axis=1)
    return top, bot


def _gmlp_branch(zb, vecs, wcat_ref, bias_ref, nchunks):
    z, dz = _gelu_fwd_bwd(zb)
    u = z[:, :DG]
    v = z[:, DG:]
    ln_g, ln_b = vecs[1:2, :], vecs[2:3, :]
    mu = jnp.mean(v, axis=-1, keepdims=True)
    vc = v - mu
    rstd = lax.rsqrt(jnp.mean(vc * vc, axis=-1, keepdims=True) + EPS)
    vhat = vc * rstd
    vl = vhat * ln_g + ln_b
    sv_cols = []
    for p in range(4):
        blocks = [vl[k * CHUNK:(k + 1) * CHUNK, p * LANE:(p + 1) * LANE] for k in range(nchunks)]
        top, bot = _pair_rhs(blocks)
        rhs = jnp.concatenate([top, bot], axis=0).astype(BF16)
        out = _dot(wcat_ref[p], rhs)
        bias = bias_ref[:, p * LANE:(p + 1) * LANE]
        sv_cols.append(jnp.concatenate([out[:, k * LANE:(k + 1) * LANE] + bias for k in range(nchunks)], axis=0))
    sv = jnp.concatenate(sv_cols, axis=1)
    return dict(u=u, dz=dz, rstd=rstd, vhat=vhat, vl=vl, sv=sv, yb=u * sv)


def _mix_fwd(x, modv, win, wpool, vecs, wcat, bias, wout):
    S = x.shape[0]
    tm = TM_MIX
    nt = S // tm
    nchunks = tm // CHUNK

    def body(x_ref, mod_ref, win_ref, wpool_ref, vec_ref, wcat_ref, bias_ref, wout_ref,
             xo_ref, pooled_ref, zb_ref, ext):
        i = pl.program_id(0)

        @pl.when(i == 0)
        def _():
            ext[0:HALO, :] = jnp.zeros((HALO, DP), F32)

        x = x_ref[...]
        _, _, _, h = _norm_mod(x, mod_ref[3:4, :], mod_ref[1:2, :], mod_ref[0:1, :])
        proj = _dot(h.astype(BF16), win_ref[...])
        xa = proj[:, :DP]
        zb = proj[:, DP:]
        zb_ref[...] = zb
        ext[HALO:HALO + tm, :] = xa
        pos = i * tm + lax.broadcasted_iota(jnp.int32, (tm, 1), 0)
        vecs = vec_ref[...]
        ya_cols = []
        pooled_cols = []
        for gi, w in enumerate(POOL_WINDOWS):
            cols = slice(gi * LANE, (gi + 1) * LANE)
            s = xa[:, cols]
            for k in range(1, w):
                s = s + ext[HALO - k:HALO - k + tm, cols]
            cnt = jnp.minimum(pos + 1, w).astype(F32)
            pooled = (s / cnt - xa[:, cols]).astype(BF16)
            pooled_cols.append(pooled)
            ya_cols.append(_dot(pooled, wpool_ref[gi]) * vecs[0:1, cols])
        pooled_ref[...] = jnp.concatenate(pooled_cols, axis=1)
        ext[0:HALO, :] = ext[tm:tm + HALO, :]

        gm = _gmlp_branch(zb, vecs, wcat_ref, bias_ref, nchunks)
        cat = jnp.concatenate(ya_cols + [gm["yb"]], axis=1).astype(BF16)
        xo_ref[...] = x + mod_ref[2:3, :] * _dot(cat, wout_ref[...])

    full = lambda shape: pl.BlockSpec(shape, lambda i: (0,) * len(shape))
    return _call(
        body, name="mix_fwd",
        grid=(nt,),
        in_specs=[pl.BlockSpec((tm, D), lambda i: (i, 0)), full((8, D)), full((D, DPROJ)),
                  full((4, LANE, LANE)), full((8, DP)), full((4, CHUNK, 2 * CHUNK)), full((CHUNK, DG)),
                  full((DP + DG, D))],
        out_specs=[pl.BlockSpec((tm, D), lambda i: (i, 0)), pl.BlockSpec((tm, DP), lambda i: (i, 0)),
                   pl.BlockSpec((tm, 2 * DG), lambda i: (i, 0))],
        out_shape=[_sds((S, D), F32), _sds((S, DP), BF16), _sds((S, 2 * DG), F32)],
        scratch_shapes=[pltpu.VMEM((tm + HALO, DP), F32)],
        compiler_params=_params(("arbitrary",), VMEM_LIMIT),
    )(x, modv, win, wpool, vecs, wcat, bias, wout)


def _mix_bwd(x, dxo, pooled, zb, modv, win, wpool, vecs, wcat, wtcat, bias, wout, side=None):
    S = x.shape[0]
    tm = TM_MIX
    nt = S // tm
    nchunks = tm // CHUNK

    def body(x_ref, dxo_ref, pooled_ref, zb_ref, mod_ref, win_ref, wpool_ref, vec_ref, wcat_ref, wtcat_ref,
             bias_ref, wout_ref,
             dx_ref, dwin_ref, dwout_ref, dwpool_ref, dwsp_ref, dbsp_ref, v512_ref, vd_ref, qext, dsv_acc):
        step = pl.program_id(0)
        tile = nt - 1 - step

        @pl.when(step == 0)
        def _():
            dwin_ref[...] = jnp.zeros_like(dwin_ref)
            dwout_ref[...] = jnp.zeros_like(dwout_ref)
            dwpool_ref[...] = jnp.zeros_like(dwpool_ref)
            dwsp_ref[...] = jnp.zeros_like(dwsp_ref)
            v512_ref[...] = jnp.zeros_like(v512_ref)
            vd_ref[...] = jnp.zeros_like(vd_ref)
            dsv_acc[...] = jnp.zeros_like(dsv_acc)
            qext[tm:tm + HALO, :] = jnp.zeros((HALO, DP), F32)

        gn, sc, sh, gate = mod_ref[3:4, :], mod_ref[1:2, :], mod_ref[0:1, :], mod_ref[2:3, :]
        vecs = vec_ref[...]
        x = x_ref[...]
        r, xn, hp, h = _norm_mod(x, gn, sc, sh)
        hb = h.astype(BF16)
        dxo = dxo_ref[...]

        pooled = pooled_ref[...]
        mixed_cols = [_dot(pooled[:, gi * LANE:(gi + 1) * LANE], wpool_ref[gi]) for gi in range(4)]
        mixed = jnp.concatenate(mixed_cols, axis=1)
        scale = vecs[0:1, :]
        gm = _gmlp_branch(zb_ref[...], vecs, wcat_ref, bias_ref, nchunks)
        cat = jnp.concatenate([mixed * scale, gm["yb"]], axis=1).astype(BF16)

        dwout_ref[...] += _dot_tn(cat, dxo.astype(BF16))
        dcat = _dot_nt((dxo * gate).astype(BF16), wout_ref[...])
        dya = dcat[:, :DP]
        dyb = dcat[:, DP:]

        v512_ref[0:1, :] += jnp.sum(dya * mixed, axis=0, keepdims=True)
        dmixed = (dya * scale).astype(BF16)
        pos = tile * tm + lax.broadcasted_iota(jnp.int32, (tm, 1), 0)
        dpooled_cols = []
        for gi, w in enumerate(POOL_WINDOWS):
            cols = slice(gi * LANE, (gi + 1) * LANE)
            dp = _dot_nt(dmixed[:, cols], wpool_ref[gi])
            dwpool_ref[gi] += _dot_tn(pooled[:, cols], dmixed[:, cols])
            cnt = jnp.minimum(pos + 1, w).astype(F32)
            qext[0:tm, cols] = dp / cnt
            dpooled_cols.append(dp)
        dxa_cols = []
        for gi, w in enumerate(POOL_WINDOWS):
            cols = slice(gi * LANE, (gi + 1) * LANE)
            s = qext[0:tm, cols]
            for k in range(1, w):
                s = s + qext[k:k + tm, cols]
            dxa_cols.append(s - dpooled_cols[gi])
        qext[tm:tm + HALO, :] = qext[0:HALO, :]

        u, sv, vl = gm["u"], gm["sv"], gm["vl"]
        du = dyb * sv
        dsv = dyb * u
        dvl_cols = []
        for p in range(4):
            cols = slice(p * LANE, (p + 1) * LANE)
            dblocks = [dsv[k * CHUNK:(k + 1) * CHUNK, cols] for k in range(nchunks)]
            vblocks = [vl[k * CHUNK:(k + 1) * CHUNK, cols] for k in range(nchunks)]
            tot = dblocks[0]
            for b in dblocks[1:]:
                tot = tot + b
            dsv_acc[:, cols] += tot
            top, bot = _pair_rhs(dblocks)
            out = _dot(wtcat_ref[p], jnp.concatenate([top, bot], axis=0).astype(BF16))
            dvl_cols.append(jnp.concatenate([out[:, k * LANE:(k + 1) * LANE] for k in range(nchunks)], axis=0))
            vcat = jnp.concatenate(vblocks, axis=1).astype(BF16)
            dwsp_ref[2 * p] += _dot_nt(top.astype(BF16), vcat)
            dwsp_ref[2 * p + 1] += _dot_nt(bot.astype(BF16), vcat)
        dvl = jnp.concatenate(dvl_cols, axis=1)
        vhat, rstd = gm["vhat"], gm["rstd"]
        v512_ref[1:2, :] += jnp.sum(dvl * vhat, axis=0, keepdims=True)
        v512_ref[2:3, :] += jnp.sum(dvl, axis=0, keepdims=True)
        dvh = dvl * vecs[1:2, :]
        dv = rstd * (dvh - jnp.mean(dvh, axis=-1, keepdims=True)
                     - vhat * jnp.mean(dvh * vhat, axis=-1, keepdims=True))
        dzb = jnp.concatenate([du, dv], axis=1) * gm["dz"]

        dproj = jnp.concatenate(dxa_cols + [dzb], axis=1).astype(BF16)
        dwin_ref[...] += _dot_tn(hb, dproj)
        dh = _dot_nt(dproj, win_ref[...])
        dsh, dsc, dgn, dxin = _norm_mod_bwd(dh, r, xn, hp, gn, sc)
        vd_ref[0:1, :] += dsh
        vd_ref[1:2, :] += dsc
        vd_ref[3:4, :] += dgn
        dx_ref[...] = dxo + dxin

        @pl.when(step == nt - 1)
        def _():
            gw = dwout_ref[...]
            vd_ref[2:3, :] += jnp.sum(wout_ref[...].astype(F32) * gw, axis=0, keepdims=True)
            dwout_ref[...] = gw * gate
            row = lax.broadcasted_iota(jnp.int32, (CHUNK, CHUNK), 0)
            col = lax.broadcasted_iota(jnp.int32, (CHUNK, CHUNK), 1)
            for hh in range(8):
                dwsp_ref[hh] = jnp.where(col <= row, dwsp_ref[hh], 0.0)
            head = lax.broadcasted_iota(jnp.int32, (8, DG), 0)
            ch = lax.broadcasted_iota(jnp.int32, (8, DG), 1)
            spread = jnp.where(ch // 64 == head, 1.0, 0.0).astype(F32)
            dbsp_ref[...] = lax.dot_general(spread, dsv_acc[...], (((1,), (1,)), ((), ())),
                                            precision=HIGHEST, preferred_element_type=F32)

    full = lambda shape: pl.BlockSpec(shape, lambda s: (0,) * len(shape))
    rev = lambda cols: pl.BlockSpec((tm, cols), lambda s: (nt - 1 - s, 0))
    step = lambda s: lambda: pl.program_id(0) == s
    return _side_call(
        body, side, (step(0), None, step(nt - 1)), name="mix_bwd",
        grid=(nt,),
        in_specs=[rev(D), rev(D), rev(DP), rev(2 * DG), full((8, D)), full((D, DPROJ)), full((4, LANE, LANE)),
                  full((8, DP)), full((4, CHUNK, 2 * CHUNK)), full((4, CHUNK, 2 * CHUNK)), full((CHUNK, DG)),
                  full((DP + DG, D))],
        out_specs=[rev(D), full((D, DPROJ)), full((DP + DG, D)), full((4, LANE, LANE)), full((8, CHUNK, CHUNK)),
                   full((8, CHUNK)), full((8, DP)), full((8, D))],
        out_shape=[_sds((S, D), F32), _sds((D, DPROJ), F32), _sds((DP + DG, D), F32), _sds((4, LANE, LANE), F32),
                   _sds((8, CHUNK, CHUNK), F32), _sds((8, CHUNK), F32), _sds((8, DP), F32), _sds((8, D), F32)],
        scratch_shapes=[pltpu.VMEM((tm + HALO, DP), F32), pltpu.VMEM((CHUNK, DG), F32)],
        compiler_params=_params(("arbitrary",), VMEM_LIMIT),
        args=(x, dxo, pooled, zb, modv, win, wpool, vecs, wcat, wtcat, bias, wout))


def _chip_sum(g, rbuf, core):
    _, _, hr, cols = g.shape
    tr = _row_block(hr)

    def body(c_ref, g_ref, r_ref, o_ref):
        o_ref[...] = (g_ref[...] + r_ref[...]).astype(BF16)

    return pl.pallas_call(
        body, name="chip_sum", interpret=False,
        grid_spec=pltpu.PrefetchScalarGridSpec(
            num_scalar_prefetch=1, grid=(NQ, hr // tr),
            in_specs=[pl.BlockSpec((None, None, tr, cols), lambda q, i, c: (q, c[0], i, 0)),
                      pl.BlockSpec((None, tr, cols), lambda q, i, c: (q, i, 0))],
            out_specs=pl.BlockSpec((None, tr, cols), lambda q, i, c: (q, i, 0))),
        out_shape=_sds((NQ, hr, cols), BF16),
        compiler_params=_params(("arbitrary", "arbitrary"), None),
    )(core, g, rbuf)


def _chip_sum_pair(own, rbuf, slots=(0, 1, 2, 3), prev=None):
    _, hr, cols = own.shape
    tr = _row_block(hr)
    a, b = slots[0], (slots[1] - slots[0] if len(slots) > 1 else 0)
    assert list(slots) == [a + b * k for k in range(len(slots))]

    def body(a_ref, b_ref, *rest):
        rest[-1][...] = (a_ref[...] + b_ref[...]).astype(BF16)

    spec = pl.BlockSpec((None, tr, cols), lambda k, i: (a + b * k, i, 0))
    return _call(
        body, name="chip_sum_pair",
        grid=(len(slots), hr // tr),
        in_specs=[spec, spec] + ([ANY] if prev is not None else []), out_specs=spec,
        out_shape=_sds((NQ, hr, cols), BF16),
        input_output_aliases={2: 0} if prev is not None else {},
        compiler_params=_params(("arbitrary", "arbitrary"), None),
    )(own, rbuf, *([prev] if prev is not None else []))


def _sum4(cs, rbuf, chip):
    _, hr, cols = rbuf.shape
    tr = _row_block(hr)

    def body(q_ref, c_ref, r1_ref, r2_ref, r3_ref, o_ref):
        acc = c_ref[...].astype(F32)
        for r in (r1_ref, r2_ref, r3_ref):
            acc = acc + r[...].astype(F32)
        o_ref[...] = acc

    slot = lambda k: pl.BlockSpec((None, tr, cols), lambda i, q: ((q[0] + k) % NQ, i, 0))
    return pl.pallas_call(
        body, name="sum4", interpret=False,
        grid_spec=pltpu.PrefetchScalarGridSpec(
            num_scalar_prefetch=1, grid=(hr // tr,),
            in_specs=[slot(0), slot(1), slot(2), slot(3)],
            out_specs=pl.BlockSpec((tr, cols), lambda i, q: (i, 0))),
        out_shape=_sds((hr, cols), F32),
        compiler_params=_params(("arbitrary",), None),
    )(chip, cs, rbuf, rbuf, rbuf)


def _adamw_halves(w, own, recv, m, v, side=None):
    rows, cols = w.shape
    hr = rows // 2
    tr = _row_block(hr, mult=8)
    nb = hr // tr

    def body(w_ref, own_ref, recv_ref, m_ref, v_ref, g_ref, d_ref, mo_ref, vo_ref):
        g = jnp.where(pl.program_id(0) == lax.axis_index("c"), own_ref[...], recv_ref[...])
        d, mn, vn = _adamw(w_ref[...], g, m_ref[...], v_ref[...])
        g_ref[...] = g
        d_ref[...] = d
        mo_ref[...] = mn
        vo_ref[...] = vn

    full = pl.BlockSpec((tr, cols), lambda h, i: (h * nb + i, 0))
    half = pl.BlockSpec((tr, cols), lambda h, i: (i, 0))
    step = lambda h, i: lambda: (pl.program_id(0) == h) & (pl.program_id(1) == i)
    return _side_call(
        body, side, (step(0, 0), None, step(1, nb - 1)), name="adamw_halves",
        grid=(2, nb), in_specs=[full, half, half, full, full], out_specs=[full] * 4,
        out_shape=[_sds((rows, cols), F32)] * 4, scratch_shapes=[],
        compiler_params=_params(("arbitrary", "arbitrary"), None),
        args=(w, own, recv, m, v))


def _cast_place(w, chip):
    rows, cols = w.shape
    tr = _row_block(rows)

    def body(q_ref, w_ref, o_ref):
        o_ref[...] = w_ref[...].astype(BF16)

    return pl.pallas_call(
        body, name="cast_place", interpret=False,
        grid_spec=pltpu.PrefetchScalarGridSpec(
            num_scalar_prefetch=1, grid=(rows // tr,),
            in_specs=[pl.BlockSpec((tr, cols), lambda i, q: (i, 0))],
            out_specs=pl.BlockSpec((None, tr, cols), lambda i, q: (q[0], i, 0))),
        out_shape=_sds((NQ, rows, cols), BF16),
        compiler_params=_params(("arbitrary",), None),
    )(chip, w)


def _ada_grad_adamw(cact_t, dmod_q, w, m, v, side=None):
    rows, cols = w.shape
    tc = 256
    assert cols % tc == 0

    def body(c_ref, d_ref, w_ref, m_ref, v_ref, g_ref, dl_ref, mo_ref, vo_ref):
        g = jnp.dot(c_ref[...], d_ref[...], precision=HIGHEST, preferred_element_type=F32)
        d, mn, vn = _adamw(w_ref[...], g, m_ref[...], v_ref[...])
        g_ref[...] = g
        dl_ref[...] = d
        mo_ref[...] = mn
        vo_ref[...] = vn

    spec = pl.BlockSpec((rows, tc), lambda i: (0, i))
    step = lambda s: lambda: pl.program_id(0) == s
    return _side_call(
        body, side, (step(0), None, step(cols // tc - 1)), name="ada_grad_adamw",
        grid=(cols // tc,),
        in_specs=[pl.BlockSpec((rows, 8), lambda i: (0, 0)), pl.BlockSpec((8, tc), lambda i: (0, i)),
                  spec, spec, spec],
        out_specs=[spec] * 4,
        out_shape=[_sds((rows, cols), F32)] * 4,
        scratch_shapes=[],
        compiler_params=_params(("arbitrary",), None),
        args=(cact_t, dmod_q, w, m, v))


def _me():
    x, y, c = lax.axis_index("x"), lax.axis_index("y"), lax.axis_index("c")
    return x, y, c


_OFFSETS7 = [(dx, dy, dc) for dx in (0, 1) for dy in (0, 1) for dc in (0, 1) if (dx, dy, dc) != (0, 0, 0)]
_CHIP_OFFSETS = [(1, 0), (0, 1), (1, 1)]


def _ada_fwd(c, w_ada_q, b_ada_q, side=None):
    ncol = w_ada_q.shape[1]

    def body(c_ref, w_ref, b_ref, cact_ref, modsel_ref, blk, gath, res, parts, send_sems, recv_sems, side_start=None):
        x, y, cc = _me()
        me = 4 * x + 2 * y + cc
        q = 2 * x + y
        cv = c_ref[...]
        ca = cv * jax.nn.sigmoid(cv)
        row = lax.broadcasted_iota(jnp.int32, (8, D), 0)
        blk[...] = jnp.where(row == me, jnp.broadcast_to(ca, (8, D)), 0.0)
        gath[me] = blk[...]
        sends = []
        for k, (dx, dy, dc) in enumerate(_OFFSETS7):
            cp = pltpu.make_async_remote_copy(blk, gath.at[me], send_sems.at[k], recv_sems.at[k],
                                              device_id=(x ^ dx, y ^ dy, cc ^ dc), device_id_type=MESH)
            cp.start()
            sends.append(cp)
        if side_start is not None:
            side_start()
        for cp in sends:
            cp.wait_recv()
        cact = gath[0]
        for d in range(1, N_DEV):
            cact = cact + gath[d]
        cact_ref[...] = cact
        res[...] = jnp.dot(cact, w_ref[...], precision=HIGHEST, preferred_element_type=F32) + b_ref[...]
        parts[q] = res[...]
        sends2 = []
        for k, (dx, dy) in enumerate(_CHIP_OFFSETS):
            cp = pltpu.make_async_remote_copy(res, parts.at[q], send_sems.at[7 + k], recv_sems.at[7 + k],
                                              device_id=(x ^ dx, y ^ dy, cc), device_id_type=MESH)
            cp.start()
            sends2.append(cp)
        for cp in sends2:
            cp.wait_recv()
        row2 = lax.broadcasted_iota(jnp.int32, (8, ncol), 0)
        out = jnp.zeros((8, ncol), F32)
        for s in range(NQ):
            mine = jnp.sum(jnp.where(row2 == me, parts[s], 0.0), axis=0, keepdims=True)
            out = out + jnp.where(row2 == s, jnp.broadcast_to(mine, (8, ncol)), 0.0)
        modsel_ref[...] = out
        for cp in sends + sends2:
            cp.wait_send()

    return _side_call(
        body, side, None, name="ada_fwd",
        in_specs=[VMEM, VMEM, VMEM], out_specs=[VMEM, VMEM],
        out_shape=[_sds((8, D), F32), _sds((8, ncol), F32)],
        scratch_shapes=[pltpu.VMEM((8, D), F32), pltpu.VMEM((N_DEV, 8, D), F32), pltpu.VMEM((8, ncol), F32),
                        pltpu.VMEM((NQ, 8, ncol), F32), pltpu.SemaphoreType.DMA((10,)), pltpu.SemaphoreType.DMA((10,))],
        compiler_params=_params(None, VMEM_LIMIT), start_in_body=side is not None,
        args=(c, w_ada_q, b_ada_q))


class _Side:
    def __init__(self, ins, out_shapes, aliases, nsem, start, mid=None, finish=None):
        self.ins, self.out_shapes, self.aliases, self.nsem = list(ins), list(out_shapes), dict(aliases), nsem
        self.start, self.mid, self.finish = start, mid, finish


def _join(*sides):
    ins, outs, aliases, offs, nsem = [], [], {}, [], 0
    for s in sides:
        offs.append((len(ins), len(outs), nsem))
        aliases.update({len(ins) + a: len(outs) + b for a, b in s.aliases.items()})
        ins += s.ins
        outs += s.out_shapes
        nsem += s.nsem

    def hook(name):
        def run(i, o, ss, rs, base):
            for s, (io, oo, so) in zip(sides, offs):
                fn = getattr(s, name)
                if fn is not None:
                    fn(i[io:io + len(s.ins)], o[oo:oo + len(s.out_shapes)], ss, rs, base + so)
        return run

    return _Side(ins, outs, aliases, nsem, hook("start"), hook("mid"), hook("finish"))


def _side_call(body, side, when, *, name, in_specs, out_specs, out_shape, scratch_shapes, args, aliases=None,
               start_in_body=False, **kw):
    n_in, n_out = len(in_specs), len(out_specs)
    aliases = dict(aliases or {})
    if side is None:
        return _call(body, name=name, in_specs=in_specs, out_specs=out_specs, out_shape=out_shape,
                     scratch_shapes=scratch_shapes, input_output_aliases=aliases, **kw)(*args), []
    ns_in, ns_out = len(side.ins), len(side.out_shapes)

    def hook(fn, k, operands):
        if fn is None:
            return
        if when is None:
            fn(*operands, 0)
        elif when[k] is not None:
            pl.when(when[k]())(functools.partial(fn, *operands, 0))

    def wrapped(*refs):
        ins, s_ins = refs[:n_in], refs[n_in:n_in + ns_in]
        o0 = n_in + ns_in
        outs, s_outs = refs[o0:o0 + n_out], refs[o0 + n_out:o0 + n_out + ns_out]
        rest = refs[o0 + n_out + ns_out:]
        scratch, operands = rest[:-2], (s_ins, s_outs, rest[-2], rest[-1])
        if start_in_body:
            body(*ins, *outs, *scratch, side_start=functools.partial(hook, side.start, 0, operands))
        else:
            hook(side.start, 0, operands)
            body(*ins, *outs, *scratch)
        hook(side.mid, 1, operands)
        hook(side.finish, 2, operands)

    res = _call(
        wrapped, name=name,
        in_specs=list(in_specs) + [ANY] * ns_in, out_specs=list(out_specs) + [ANY] * ns_out,
        out_shape=list(out_shape) + side.out_shapes,
        scratch_shapes=list(scratch_shapes) + [pltpu.SemaphoreType.DMA((side.nsem,)),
                                               pltpu.SemaphoreType.DMA((side.nsem,))],
        input_output_aliases={**aliases, **{n_in + a: n_out + b for a, b in side.aliases.items()}},
        **kw)(*args, *side.ins)
    return res[:n_out], res[n_out:]


def _run_side(side, name):
    return _side_call(lambda: None, side, None, name=name, in_specs=[], out_specs=[], out_shape=[],
                      scratch_shapes=[], args=[])[1]


def _remote(src, dst, ss, rs, k, dev):
    return pltpu.make_async_remote_copy(src, dst, ss.at[k], rs.at[k], device_id=dev, device_id_type=MESH)


def _gather_side(bufs):
    n = len(bufs)

    def walk(outs, half):
        x, y, cc = _me()
        for w in range(n):
            hr = outs[w].shape[1] // 2
            rows = pl.ds((cc if half == "mine" else 1 - cc) * hr, hr)
            for j, (dx, dy) in enumerate(_CHIP_OFFSETS):
                yield w, j, (x ^ dx, y ^ dy, cc), outs[w].at[2 * (x ^ dx) + (y ^ dy), rows], outs[w].at[2 * x + y, rows]

    def start(ins, outs, ss, rs, b):
        for w, j, peer, _, own in walk(outs, "mine"):
            _remote(own, own, ss, rs, b + 6 * w + j, peer).start()

    def mid(ins, outs, ss, rs, b):
        x, y, cc = _me()
        for w, j, peer, land, _ in walk(outs, "mine"):
            _remote(land, land, ss, rs, b + 6 * w + j, peer).wait_recv()
            _remote(land, land, ss, rs, b + 6 * w + 3 + j, (x, y, 1 - cc)).start()

    def finish(ins, outs, ss, rs, b):
        x, y, cc = _me()
        for w, j, _, land, _ in walk(outs, "other"):
            _remote(land, land, ss, rs, b + 6 * w + 3 + j, (x, y, 1 - cc)).wait_recv()
        for w, j, peer, land, own in walk(outs, "mine"):
            _remote(own, own, ss, rs, b + 6 * w + j, peer).wait_send()
            _remote(land, land, ss, rs, b + 6 * w + 3 + j, (x, y, 1 - cc)).wait_send()

    return _Side(bufs, [_sds(tuple(w.shape), w.dtype) for w in bufs], {i: i for i in range(n)}, 6 * n,
                 start, mid, finish)


def _copies_side(ins, out_shapes, nsem, copies):
    def start(*a):
        for cp in copies(*a):
            cp.start()

    def finish(*a):
        for cp in copies(*a):
            cp.wait()

    return _Side(ins, out_shapes, {}, nsem, start, None, finish)


def _swap_side(gs):
    def copies(ins, outs, ss, rs, b):
        x, y, cc = _me()
        return [_remote(ins[w].at[:, 1 - cc], outs[w], ss, rs, b + w, (x, y, 1 - cc)) for w in range(len(gs))]

    return _copies_side(gs, [_sds((NQ,) + tuple(g.shape[2:]), F32) for g in gs], len(gs), copies)


def _exchange_side(cs, slots=None, prev=None):
    n = len(cs)
    slots = slots or [(0, 1, 2, 3)] * n

    def among(chip, allowed):
        hit = chip == allowed[0]
        for s in allowed[1:]:
            hit = hit | (chip == s)
        return hit

    def each(ins, outs, ss, rs, b, do_send, do_recv):
        x, y, cc = _me()
        q = 2 * x + y
        for w in range(n):
            for j, (dx, dy) in enumerate(_CHIP_OFFSETS):
                pq = 2 * (x ^ dx) + (y ^ dy)
                cp = _remote(ins[w].at[pq], outs[w].at[q], ss, rs, b + 3 * w + j, (x ^ dx, y ^ dy, cc))
                if do_send is not None:
                    pl.when(among(pq, slots[w]))(functools.partial(do_send, cp))
                if do_recv is not None:
                    pl.when(among(q, slots[w]))(functools.partial(do_recv, cp))

    def start(ins, outs, ss, rs, b):
        each(ins, outs, ss, rs, b, lambda cp: cp.start(), None)

    def finish(ins, outs, ss, rs, b):
        each(ins, outs, ss, rs, b, lambda cp: cp.wait_send(), lambda cp: cp.wait_recv())

    ins = list(cs) + (list(prev) if prev is not None else [])
    aliases = {n + w: w for w in range(n)} if prev is not None else {}
    return _Side(ins, [_sds(tuple(c.shape), c.dtype) for c in cs], aliases, 3 * n, start, None, finish)


def _share_side(fs):
    def copies(ins, outs, ss, rs, b):
        x, y, cc = _me()
        return [_remote(ins[w], outs[w], ss, rs, b + w, (x, y, 1 - cc)) for w in range(len(fs))]

    return _copies_side(fs, [_sds(tuple(f.shape), F32) for f in fs], len(fs), copies)


def _small_allreduce_adamw(g, w, m, v, nd):
    rows = g.shape[0]
    nr = rows - nd
    hr = nr // 2
    assert nd % 8 == 0 and hr % 8 == 0

    def body(g_ref, w_ref, m_ref, v_ref, gs_ref, d_ref, mo_ref, vo_ref, gath, sib, csum, slots, tot, ss, rs):
        x, y, cc = _me()
        me = 4 * x + 2 * y + cc
        q = 2 * x + y
        sibling = (x, y, 1 - cc)
        dm = g_ref.at[pl.ds(0, nd)]
        gath[me] = g_ref[0:nd, :]
        to_all = [_remote(dm, gath.at[me], ss, rs, k, (x ^ dx, y ^ dy, cc ^ dc)) for k, (dx, dy, dc) in enumerate(_OFFSETS7)]
        to_sib = _remote(g_ref.at[pl.ds(nd, nr)], sib, ss, rs, 7, sibling)
        for cp in to_all + [to_sib]:
            cp.start()
        to_sib.wait_recv()
        csum[...] = g_ref[nd:, :] + sib[...]
        mine = pl.ds(pl.multiple_of(cc * hr, 8), hr)
        slots[q] = csum[mine, :]
        to_chips = [_remote(csum.at[mine], slots.at[q], ss, rs, 8 + j, (x ^ dx, y ^ dy, cc))
                    for j, (dx, dy) in enumerate(_CHIP_OFFSETS)]
        for cp in to_chips:
            cp.start()
        for cp in to_chips:
            cp.wait_recv()
        tot[mine, :] = (slots[0] + slots[1]) + (slots[2] + slots[3])
        halves = _remote(tot.at[mine], tot.at[mine], ss, rs, 11, sibling)
        halves.start()
        for cp in to_all:
            cp.wait_recv()
        dsum = gath[0]
        for dev in range(1, N_DEV):
            dsum = dsum + gath[dev]
        halves.wait_recv()
        for lo, n, total in ((0, nd, dsum), (nd, nr, tot[...])):
            gs_ref[lo:lo + n, :] = total
            d, mn, vn = _adamw(w_ref[lo:lo + n, :], total, m_ref[lo:lo + n, :], v_ref[lo:lo + n, :])
            d_ref[lo:lo + n, :] = d
            mo_ref[lo:lo + n, :] = mn
            vo_ref[lo:lo + n, :] = vn
        for cp in to_all + [to_sib, halves] + to_chips:
            cp.wait_send()

    return _call(
        body, name="small_allreduce_adamw",
        in_specs=[VMEM] * 4, out_specs=[VMEM] * 5,
        out_shape=[_sds((rows, LANE), F32)] * 4 + [_sds((N_DEV, nd, LANE), F32)],
        scratch_shapes=[pltpu.VMEM((nr, LANE), F32), pltpu.VMEM((nr, LANE), F32), pltpu.VMEM((NQ, hr, LANE), F32),
                        pltpu.VMEM((nr, LANE), F32), pltpu.SemaphoreType.DMA((12,)), pltpu.SemaphoreType.DMA((12,))],
        compiler_params=_params(None, VMEM_LIMIT),
    )(g, w, m, v)


_SMALL = ["b_ada", "norm_ffn1_g", "norm_mix_g", "pool_scale", "gmlp_ln_g", "gmlp_ln_b", "b_spatial",
          "norm_ffn2_g", "norm_final_g", "w_pool", "w_spatial"]


def _pack(parts):
    blocks, layout, r0 = [], {}, 0
    for name in _SMALL:
        a = parts[name]
        n = a.size
        rows = -(-n // LANE)
        rows8 = -(-rows // 8) * 8
        flat = a.reshape(-1).astype(F32)
        if rows8 * LANE != n:
            flat = jnp.concatenate([flat, jnp.zeros((rows8 * LANE - n,), F32)])
        blocks.append(flat.reshape(rows8, LANE))
        layout[name] = (r0, n, a.shape)
        r0 += rows8
    return jnp.concatenate(blocks, axis=0), layout


def _unpack(packed, layout):
    out = {}
    for name, (r0, n, shape) in layout.items():
        rows = -(-n // LANE)
        out[name] = packed[r0:r0 + rows].reshape(-1)[:n].reshape(shape)
    return out


def _modv(mod9, sub, gain):
    rows = jnp.concatenate([mod9[3 * sub:3 * sub + 3], gain.reshape(1, D), jnp.zeros((4, D), F32)], axis=0)
    return rows


_BIG = ["ffn1_w_in", "ffn1_w_out", "w_mix_in", "w_mix_out", "ffn2_w_in", "ffn2_w_out"]


def kernel(x, c, w_ada, b_ada, norm_ffn1_g, ffn1_w_in, ffn1_w_out, norm_mix_g, w_mix_in, w_pool, pool_scale, gmlp_ln_g, gmlp_ln_b, w_spatial, b_spatial, w_mix_out, norm_ffn2_g, ffn2_w_in, ffn2_w_out, norm_final_g, loss_target, m_w_ada, m_b_ada, m_norm_ffn1_g, m_ffn1_w_in, m_ffn1_w_out, m_norm_mix_g, m_w_mix_in, m_w_pool, m_pool_scale, m_gmlp_ln_g, m_gmlp_ln_b, m_w_spatial, m_b_spatial, m_w_mix_out, m_norm_ffn2_g, m_ffn2_w_in, m_ffn2_w_out, m_norm_final_g, v_w_ada, v_b_ada, v_norm_ffn1_g, v_ffn1_w_in, v_ffn1_w_out, v_norm_mix_g, v_w_mix_in, v_w_pool, v_pool_scale, v_gmlp_ln_g, v_gmlp_ln_b, v_w_spatial, v_b_spatial, v_w_mix_out, v_norm_ffn2_g, v_ffn2_w_in, v_ffn2_w_out, v_norm_final_g):
    names = ["w_ada", "b_ada", "norm_ffn1_g", "ffn1_w_in", "ffn1_w_out", "norm_mix_g", "w_mix_in", "w_pool",
             "pool_scale", "gmlp_ln_g", "gmlp_ln_b", "w_spatial", "b_spatial", "w_mix_out", "norm_ffn2_g",
             "ffn2_w_in", "ffn2_w_out", "norm_final_g"]
    W = dict(zip(names, [w_ada, b_ada, norm_ffn1_g, ffn1_w_in, ffn1_w_out, norm_mix_g, w_mix_in, w_pool, pool_scale,
                         gmlp_ln_g, gmlp_ln_b, w_spatial, b_spatial, w_mix_out, norm_ffn2_g, ffn2_w_in, ffn2_w_out,
                         norm_final_g]))
    M = dict(zip(names, [m_w_ada, m_b_ada, m_norm_ffn1_g, m_ffn1_w_in, m_ffn1_w_out, m_norm_mix_g, m_w_mix_in, m_w_pool,
                         m_pool_scale, m_gmlp_ln_g, m_gmlp_ln_b, m_w_spatial, m_b_spatial, m_w_mix_out, m_norm_ffn2_g,
                         m_ffn2_w_in, m_ffn2_w_out, m_norm_final_g]))
    V = dict(zip(names, [v_w_ada, v_b_ada, v_norm_ffn1_g, v_ffn1_w_in, v_ffn1_w_out, v_norm_mix_g, v_w_mix_in, v_w_pool,
                         v_pool_scale, v_gmlp_ln_g, v_gmlp_ln_b, v_w_spatial, v_b_spatial, v_w_mix_out, v_norm_ffn2_g,
                         v_ffn2_w_in, v_ffn2_w_out, v_norm_final_g]))

    xi, yi, ci = _me()
    q = 2 * xi + yi
    core = ci.astype(jnp.int32).reshape(1)

    chip = q.astype(jnp.int32).reshape(1)
    place = lambda n: _cast_place(W[n][0], chip)

    ncol = w_ada.shape[2]
    b_q = lax.dynamic_slice_in_dim(b_ada, q * ncol, ncol, axis=1)
    (cact_all, modsel), (win1, wout1) = _ada_fwd(
        c, w_ada[0], b_q, side=_gather_side([place("ffn1_w_in"), place("ffn1_w_out")]))
    mod9 = modsel[:NQ].reshape(9, D)
    xs, target = x[0], loss_target[0]
    mv1 = _modv(mod9, 0, norm_ffn1_g[0])
    mv2 = _modv(mod9, 1, norm_mix_g[0])
    mv3 = _modv(mod9, 2, norm_ffn2_g[0])
    wcat, wtcat, bias = _prep_spatial(w_spatial[0], b_spatial[0].T)
    wpool = w_pool[0].astype(BF16)
    vecs = jnp.concatenate([pool_scale, gmlp_ln_g, gmlp_ln_b, jnp.zeros((5, DP), F32)], axis=0)
    gf = jnp.concatenate([norm_final_g.reshape(1, D), jnp.zeros((7, D), F32)], axis=0)

    later =["w_mix_in", "w_mix_out", "ffn2_w_in", "ffn2_w_out"]
    (x1, g1s, u1s), got = _ffn_fwd(xs, mv1, win1, wout1.reshape(2, CH, D), side=_gather_side([place(n) for n in later]))
    wmi, wmo, win2, wout2 = got
    wmi = jnp.transpose(wmi, (1, 0, 2)).reshape(D, DPROJ)
    wmo = wmo.reshape(DP + DG, D)
    x2, pooled, zb = _mix_fwd(x1, mv2, wmi, wpool, vecs, wcat, bias, wmo)
    (dx3, g3s, u3s, loss_blk, dgf), _ = _ffn_fwd(x2, mv3, win2, wout2.reshape(2, CH, D), head=(target, gf))

    wo1, wo2 = wout1.reshape(2, CH, D), wout2.reshape(2, CH, D)
    (dx2, oin2, oout2, rin2, rout2, vec3), _ = _ffn_bwd(x2, dx3, g3s, u3s, mv3, win2, wo2)
    cs2 = [_chip_sum_pair(oin2, rin2), _chip_sum_pair(oout2, rout2)]
    (dx1, dwmi, dwmo, dwpool, dwsp, dbsp, v512, vec2), ex2 = _mix_bwd(
        x1, dx2, pooled, zb, mv2, wmi, wpool, vecs, wcat, wtcat, bias, wmo, side=_exchange_side(cs2))
    half2 = [_sum4(cs, e, chip) for cs, e in zip(cs2, ex2)]
    qcols = w_mix_in.shape[2]
    vmix = [jnp.transpose(dwmi.reshape(D, NQ, qcols), (1, 0, 2)).reshape(NQ, 2, D // 2, qcols),
            dwmo.reshape(NQ, 2, (DP + DG) // 8, D)]
    first1, got = _ffn_bwd_pass(0, xs, dx1, g1s, u1s, mv1, win1, wo1,
                                side=_join(_swap_side(vmix), _share_side(half2)))
    sibmix, other2 = got[:2], got[2:]
    cs_mix = [_chip_sum(g, r, core) for g, r in zip(vmix, sibmix)]
    (grad_x, oin1, oout1, rin1, rout1, vec1), ex_mix = _ffn_bwd_pass(
        1, xs, dx1, g1s, u1s, mv1, win1, wo1, prev=first1[:5], side=_exchange_side(cs_mix))
    vec1 = first1[5] + vec1
    cs_ffn1 = [_chip_sum_pair(oin1, rin1), _chip_sum_pair(oout1, rout1)]

    dmod = jnp.concatenate([vec1[0:3], vec2[0:3], vec3[0:3]], axis=0)
    grads = dict(
        b_ada=dmod.reshape(1, 9 * D), norm_ffn1_g=vec1[3:4], norm_mix_g=vec2[3:4], norm_ffn2_g=vec3[3:4],
        pool_scale=v512[0:1], gmlp_ln_g=v512[1:2], gmlp_ln_b=v512[2:3], b_spatial=dbsp[None],
        norm_final_g=dgf[0], w_pool=dwpool[None], w_spatial=dwsp[None])

    gp, layout = _pack({n: grads[n] for n in _SMALL})
    gp = jnp.concatenate([gp, loss_blk, loss_blk], axis=0)
    pad = jnp.zeros((16, LANE), F32)
    wp, mp, vp = [jnp.concatenate([_pack({n: src[n] for n in _SMALL})[0], pad], axis=0) for src in (W, M, V)]
    r0, nb, _ = layout["b_ada"]
    assert r0 == 0
    out_g, out_d, out_m, out_v = {}, {}, {}, {}
    gs, dl, mo, vo, gath = _small_allreduce_adamw(gp, wp, mp, vp, nb // LANE)
    loss = gs[-16, 0]
    for packed, dst in ((gs, out_g), (dl, out_d), (mo, out_m), (vo, out_v)):
        for n, a in _unpack(packed, layout).items():
            dst[n] = a.reshape(W[n].shape)

    def update(n, own, recv):
        (g2, d, mn, vn), _ = _adamw_halves(W[n][0], own, recv, M[n][0], V[n][0])
        out_g[n], out_d[n], out_m[n], out_v[n] = g2[None], d[None], mn[None], vn[None]

    ex_ffn1 = _run_side(_exchange_side(cs_ffn1), "chip_exchange")
    dmod_q = lax.dynamic_slice_in_dim(gath.reshape(N_DEV, nb), q * ncol, ncol, axis=1)
    (ga, da, ma, va), _ = _ada_grad_adamw(cact_all.T, dmod_q, w_ada[0], m_w_ada[0], v_w_ada[0])
    out_g["w_ada"], out_d["w_ada"], out_m["w_ada"], out_v["w_ada"] = ga[None], da[None], ma[None], va[None]
    half1 = [_sum4(cs, e, chip) for cs, e in zip(cs_mix + cs_ffn1, list(ex_mix) + list(ex_ffn1))]
    other1 = _run_side(_share_side(half1), "sibling_share")
    for n, own, recv in zip(["w_mix_in", "w_mix_out", "ffn1_w_in", "ffn1_w_out", "ffn2_w_in", "ffn2_w_out"],
                            half1 + half2, list(other1) + list(other2)):
        update(n, own, recv)

    return (loss, grad_x[None], *[out_g[n] for n in names], *[out_d[n] for n in names],
            *[out_m[n] for n in names], *[out_v[n] for n in names])
```

```python
import functools
import math

import jax
import jax.numpy as jnp
from jax import lax
from jax.experimental import pallas as pl
from jax.experimental.pallas import tpu as pltpu

F32 = jnp.float32
BF16 = jnp.bfloat16
MESH = pl.DeviceIdType.MESH
HIGHEST = lax.Precision.HIGHEST

EPS = 1e-6
D = 1024
DFF = 2816
CH = DFF // 2
NQ = 4
DP = 512
DG = 512
DPROJ = DP + 2 * DG
POOL_WINDOWS = (2, 4, 8, 16)
HALO = 16
CHUNK = 128
LANE = 128
N_DEV = 8

ADAM_LR = 0.001
ADAM_B1 = 0.9
ADAM_B2 = 0.999
ADAM_EPS = 1e-08
ADAM_WD = 0.01
ADAM_STEP = 10

VMEM_LIMIT = 62 * 1024 * 1024

TM_FFN_FWD = 512
TM_FFN_BWD = 256
TM_MIX = 256


def _call(body, **kw):
    return pl.pallas_call(body, interpret=False, **kw)


def _params(sem=None, vmem=None):
    return pltpu.CompilerParams(dimension_semantics=sem, vmem_limit_bytes=vmem)


def _sds(shape, dtype):
    return jax.ShapeDtypeStruct(shape, dtype)


ANY = pl.BlockSpec(memory_space=pl.ANY)
VMEM = pl.BlockSpec(memory_space=pltpu.VMEM)
SMEM = pl.BlockSpec(memory_space=pltpu.SMEM)


def _norm_mod(x, gn, sc, sh):
    r = lax.rsqrt(jnp.mean(x * x, axis=-1, keepdims=True) + EPS)
    xn = x * r
    hp = xn * gn
    return r, xn, hp, hp * (1.0 + sc) + sh


def _norm_mod_bwd(dh, r, xn, hp, gn, sc):
    one_sc = 1.0 + sc
    dsh = jnp.sum(dh, axis=0, keepdims=True)
    dsc = jnp.sum(dh * hp, axis=0, keepdims=True)
    dgn = jnp.sum(dh * one_sc * xn, axis=0, keepdims=True)
    dxn = dh * (gn * one_sc)
    dx = r * (dxn - xn * jnp.mean(dxn * xn, axis=-1, keepdims=True))
    return dsh, dsc, dgn, dx


def _dot(a, b):
    return jnp.dot(a, b, preferred_element_type=F32)


def _dot_nt(a, b):
    return lax.dot_general(a, b, (((1,), (1,)), ((), ())), preferred_element_type=F32)


def _dot_tn(a, b):
    return lax.dot_general(a, b, (((0,), (0,)), ((), ())), preferred_element_type=F32)


_GELU_C = math.sqrt(2.0 / math.pi)
_GELU_A = 0.044715


def _gelu_fwd_bwd(x):
    x2 = x * x
    t = jnp.tanh(_GELU_C * (x + _GELU_A * x * x2))
    g = 0.5 * x * (1.0 + t)
    dg = 0.5 * (1.0 + t) + 0.5 * x * (1.0 - t * t) * (_GELU_C * (1.0 + 3.0 * _GELU_A * x2))
    return g, dg


def _adamw(w, g, m, v):
    m = ADAM_B1 * m + (1.0 - ADAM_B1) * g
    v = ADAM_B2 * v + (1.0 - ADAM_B2) * (g * g)
    m_hat = m / (1.0 - ADAM_B1 ** ADAM_STEP)
    v_hat = v / (1.0 - ADAM_B2 ** ADAM_STEP)
    delta = -ADAM_LR * (m_hat / (jnp.sqrt(v_hat) + ADAM_EPS) + ADAM_WD * w)
    return delta, m, v


def _row_block(rows, cap=256, mult=16):
    best = None
    for t in range(mult, min(rows, cap) + 1, mult):
        if rows % t == 0:
            best = t
    assert best is not None, rows
    return best


def _head_math(x, target, gf):
    r = lax.rsqrt(jnp.mean(x * x, axis=-1, keepdims=True) + EPS)
    xn = x * r
    err = xn * gf - target
    dy = err * (1.0 / D)
    dxn = dy * gf
    dx = r * (dxn - xn * jnp.mean(dxn * xn, axis=-1, keepdims=True))
    return (0.5 / D) * jnp.sum(err * err), jnp.sum(dy * xn, axis=0, keepdims=True), dx


def _ffn_fwd(x, modv, win, wout, side=None, head=None):
    S = x.shape[0]
    tm = TM_FFN_FWD
    nt = S // tm

    def body(*refs):
        if head is None:
            x_ref, mod_ref, wg_ref, wu_ref, wo_ref, xo_ref, gs_ref, us_ref, acc_scr = refs
        else:
            (x_ref, mod_ref, wg_ref, wu_ref, wo_ref, t_ref, gf_ref,
             xo_ref, gs_ref, us_ref, loss_ref, dgf_ref, acc_scr) = refs

        @pl.when((pl.program_id(0) == 0) & (pl.program_id(1) == 0))
        def _():
            acc_scr[...] = jnp.zeros_like(acc_scr)
            if head is not None:
                loss_ref[...] = jnp.zeros_like(loss_ref)
                dgf_ref[...] = jnp.zeros_like(dgf_ref)

        j = pl.program_id(1)
        h = _norm_mod(x_ref[...], mod_ref[3:4, :], mod_ref[1:2, :], mod_ref[0:1, :])[3].astype(BF16)
        g = _dot(h, wg_ref[...]).astype(BF16)
        u = _dot(h, wu_ref[...]).astype(BF16)
        gs_ref[...] = g
        us_ref[...] = u
        gf = g.astype(F32)
        a = (gf * jax.nn.sigmoid(gf) * u.astype(F32)).astype(BF16)
        acc = jnp.where(j == 0, 0.0, acc_scr[...]) + _dot(a, wo_ref[...])
        acc_scr[...] = acc
        xo = x_ref[...] + (0.5 * mod_ref[2:3, :]) * acc
        if head is None:
            xo_ref[...] = xo
        else:
            @pl.when(j == 1)
            def _():
                loss, dgf, dx = _head_math(xo, t_ref[...], gf_ref[0:1, :])
                loss_ref[...] += loss
                dgf_ref[0:1, :] += dgf
                xo_ref[...] = dx

    step = lambda i, j: lambda: (pl.program_id(0) == i) & (pl.program_id(1) == j)
    tile = pl.BlockSpec((tm, D), lambda i, j: (i, 0))
    const = lambda shape: pl.BlockSpec(shape, lambda i, j: (0, 0))
    chunk = pl.BlockSpec((tm, CH), lambda i, j: (i, j))
    in_specs = [tile, const((8, D)), pl.BlockSpec((None, D, CH), lambda i, j: (j, 0, 0)),
                pl.BlockSpec((None, D, CH), lambda i, j: (2 + j, 0, 0)), pl.BlockSpec((None, CH, D), lambda i, j: (j, 0, 0))]
    out_specs = [tile, chunk, chunk]
    out_shape = [_sds((S, D), F32), _sds((S, DFF), BF16), _sds((S, DFF), BF16)]
    args = (x, modv, win, win, wout)
    if head is not None:
        in_specs += [tile, const((8, D))]
        out_specs += [const((8, LANE)), const((8, D))]
        out_shape += [_sds((8, LANE), F32), _sds((8, D), F32)]
        args += tuple(head)
    return _side_call(
        body, side, (step(0, 0), step((7 * nt) // 10, 0), step(nt - 1, 1)), name="ffn_fwd",
        grid=(nt, 2), in_specs=in_specs, out_specs=out_specs, out_shape=out_shape,
        scratch_shapes=[pltpu.VMEM((tm, D), F32)],
        compiler_params=_params(("arbitrary", "arbitrary"), VMEM_LIMIT),
        args=args)


def _ffn_bwd_pass(jj, x, dxo, gs, us, modv, win, wout, prev=None, side=None):
    S = x.shape[0]
    tm = TM_FFN_BWD
    nsub = tm // 256
    nt = S // tm
    hi, ho = D // 2, CH // 4
    last = prev is not None
    assert last == (jj == 1)

    def body(*refs):
        x_ref, dxo_ref, gs_ref, us_ref, mod_ref, wg_ref, wu_ref, wo_ref = refs[:8]
        k = 13 if last else 8
        out_ref, dwin_ref, dwout_ref, rwin_ref, rwout_ref, vec_ref = refs[k:k + 6]
        accg, accu, accw, sems, fsend, frecv = refs[k + 6:]
        i = pl.program_id(0)

        @pl.when(i == 0)
        def _():
            accg[...] = jnp.zeros_like(accg)
            accu[...] = jnp.zeros_like(accu)
            accw[...] = jnp.zeros_like(accw)
            vec_ref[...] = jnp.zeros_like(vec_ref)

        gn, sc, sh, gate = mod_ref[3:4, :], mod_ref[1:2, :], mod_ref[0:1, :], mod_ref[2:3, :]

        parts = []
        for s in range(nsub):
            rs = slice(s * (tm // nsub), (s + 1) * (tm // nsub))
            r, xn, hp, h = _norm_mod(x_ref[rs, :], gn, sc, sh)
            dxo = dxo_ref[rs, :]
            dy = (dxo * (0.5 * gate)).astype(BF16)
            g = gs_ref[rs, :].astype(F32)
            u = us_ref[rs, :].astype(F32)
            sig = jax.nn.sigmoid(g)
            sl = g * sig
            a = (sl * u).astype(BF16)
            da = _dot_nt(dy, wo_ref[...])
            dg = (da * u * (sig * (1.0 + g * (1.0 - sig)))).astype(BF16)
            du = (da * sl).astype(BF16)
            dhp = _dot_nt(dg, wg_ref[...]) + _dot_nt(du, wu_ref[...])
            parts.append((h.astype(BF16), a, dg, du, dxo.astype(BF16)))
            if last:
                dsh, dsc, dgn, dxin = _norm_mod_bwd(refs[8][rs, :] + dhp, r, xn, hp, gn, sc)
                vec_ref[0:1, :] += dsh
                vec_ref[1:2, :] += dsc
                vec_ref[3:4, :] += dgn
                out_ref[rs, :] = dxo + dxin
            else:
                out_ref[rs, :] = dhp

        hb, a, dg, du, dxb = [jnp.concatenate(p, axis=0) if nsub > 1 else p[0] for p in zip(*parts)]
        accw[...] += _dot_tn(a, dxb)
        accg[...] += _dot_tn(hb, dg)
        accu[...] += _dot_tn(hb, du)

        @pl.when(i == nt - 1)
        def _():
            gw = accw[...]
            vec_ref[2:3, :] += 0.5 * jnp.sum(wo_ref[...].astype(F32) * gw, axis=0, keepdims=True)
            accw[...] = gw * (0.5 * gate)
            mx, my, cc = _me()
            rows = lambda base, n, c: pl.ds(base + c * n, n)
            pieces = [(accg, 0, hi, dwin_ref, rwin_ref, jj), (accu, 0, hi, dwin_ref, rwin_ref, 2 + jj),
                      (accw, 0, ho, dwout_ref, rwout_ref, 2 * jj), (accw, 2 * ho, ho, dwout_ref, rwout_ref, 2 * jj + 1)]
            loc = [pltpu.make_async_copy(acc.at[rows(base, n, cc)], own.at[slot], sems.at[p])
                   for p, (acc, base, n, own, _, slot) in enumerate(pieces)]
            rem = [pltpu.make_async_remote_copy(acc.at[rows(base, n, 1 - cc)], sib.at[slot], fsend.at[p], frecv.at[p],
                                                device_id=(mx, my, 1 - cc), device_id_type=MESH)
                   for p, (acc, base, n, _, sib, slot) in enumerate(pieces)]
            for cp in loc + rem:
                cp.start()
            for cp in loc:
                cp.wait()
            for cp in rem:
                cp.wait()

    once = pl.Buffered(1)
    tile = pl.BlockSpec((tm, D), lambda i: (i, 0))
    chunk = pl.BlockSpec((tm, CH), lambda i: (i, jj))
    in_specs = [tile, tile, chunk, chunk, pl.BlockSpec((8, D), lambda i: (0, 0)),
                pl.BlockSpec((None, D, CH), lambda i: (jj, 0, 0), pipeline_mode=once),
                pl.BlockSpec((None, D, CH), lambda i: (2 + jj, 0, 0), pipeline_mode=once),
                pl.BlockSpec((None, CH, D), lambda i: (jj, 0, 0), pipeline_mode=once)]
    args = (x, dxo, gs, us, modv, win, win, wout)
    if last:
        in_specs += [tile, ANY, ANY, ANY, ANY]
        args += tuple(prev)
    step = lambda s: lambda: pl.program_id(0) == s
    return _side_call(
        body, side, (step(0), None, step(nt - 1)), name="ffn_bwd",
        grid=(nt,), in_specs=in_specs,
        out_specs=[tile, ANY, ANY, ANY, ANY, pl.BlockSpec((8, D), lambda i: (0, 0))],
        out_shape=[_sds((S, D), F32), _sds((NQ, hi, CH), F32), _sds((NQ, ho, D), F32), _sds((NQ, hi, CH), F32),
                   _sds((NQ, ho, D), F32), _sds((8, D), F32)],
        scratch_shapes=[pltpu.VMEM((D, CH), F32), pltpu.VMEM((D, CH), F32), pltpu.VMEM((CH, D), F32),
                        pltpu.SemaphoreType.DMA((4,)), pltpu.SemaphoreType.DMA((4,)), pltpu.SemaphoreType.DMA((4,))],
        aliases={9 + p: 1 + p for p in range(4)} if last else {},
        compiler_params=_params(("arbitrary",), VMEM_LIMIT),
        args=args)


def _ffn_bwd(x, dxo, gs, us, modv, win, wout, side=None):
    first, extra = _ffn_bwd_pass(0, x, dxo, gs, us, modv, win, wout, side=side)
    (dx, dwin, dwout, rwin, rwout, vec), _ = _ffn_bwd_pass(1, x, dxo, gs, us, modv, win, wout, prev=first[:5])
    return (dx, dwin, dwout, rwin, rwout, first[5] + vec), extra


def _prep_spatial(w_spatial, b_spatial_t):
    def body(w_ref, b_ref, wcat_ref, wtcat_ref, bias_ref):
        row = lax.broadcasted_iota(jnp.int32, (CHUNK, CHUNK), 0)
        col = lax.broadcasted_iota(jnp.int32, (CHUNK, CHUNK), 1)
        tril = col <= row
        for p in range(4):
            wa = jnp.where(tril, w_ref[2 * p], 0.0)
            wb = jnp.where(tril, w_ref[2 * p + 1], 0.0)
            wcat_ref[p] = jnp.concatenate([wa, wb], axis=1).astype(BF16)
            wtcat_ref[p] = jnp.concatenate([wa.T, wb.T], axis=1).astype(BF16)
        head = lax.broadcasted_iota(jnp.int32, (8, DG), 0)
        ch = lax.broadcasted_iota(jnp.int32, (8, DG), 1)
        spread = jnp.where(ch // 64 == head, 1.0, 0.0).astype(F32)
        bias_ref[...] = jnp.dot(b_ref[...], spread, precision=HIGHEST, preferred_element_type=F32)

    return _call(
        body, name="prep_spatial",
        in_specs=[VMEM, VMEM], out_specs=[VMEM, VMEM, VMEM],
        out_shape=[_sds((4, CHUNK, 2 * CHUNK), BF16), _sds((4, CHUNK, 2 * CHUNK), BF16), _sds((CHUNK, DG), F32)],
    )(w_spatial, b_spatial_t)


def _pair_rhs(blocks):
    lane = lax.broadcasted_iota(jnp.int32, (CHUNK, LANE), 1)
    lo = lane < 64
    top = jnp.concatenate([jnp.where(lo, b, 0.0) for b in blocks], axis=1)
    bot = jnp.concatenate([jnp.where(lo, 0.0, b) for b in blocks], axis=1)
    return top, bot


def _gmlp_branch(zb, vecs, wcat_ref, bias_ref, nchunks):
    z, dz = _gelu_fwd_bwd(zb)
    u = z[:, :DG]
    v = z[:, DG:]
    ln_g, ln_b = vecs[1:2, :], vecs[2:3, :]
    mu = jnp.mean(v, axis=-1, keepdims=True)
    vc = v - mu
    rstd = lax.rsqrt(jnp.mean(vc * vc, axis=-1, keepdims=True) + EPS)
    vhat = vc * rstd
    vl = vhat * ln_g + ln_b
    sv_cols = []
    for p in range(4):
        blocks = [vl[k * CHUNK:(k + 1) * CHUNK, p * LANE:(p + 1) * LANE] for k in range(nchunks)]
        top, bot = _pair_rhs(blocks)
        rhs = jnp.concatenate([top, bot], axis=0).astype(BF16)
        out = _dot(wcat_ref[p], rhs)
        bias = bias_ref[:, p * LANE:(p + 1) * LANE]
        sv_cols.append(jnp.concatenate([out[:, k * LANE:(k + 1) * LANE] + bias for k in range(nchunks)], axis=0))
    sv = jnp.concatenate(sv_cols, axis=1)
    return dict(u=u, dz=dz, rstd=rstd, vhat=vhat, vl=vl, sv=sv, yb=u * sv)


def _mix_fwd(x, modv, win, wpool, vecs, wcat, bias, wout):
    S = x.shape[0]
    tm = TM_MIX
    nt = S // tm
    nchunks = tm // CHUNK

    def body(x_ref, mod_ref, win_ref, wpool_ref, vec_ref, wcat_ref, bias_ref, wout_ref,
             xo_ref, pooled_ref, zb_ref, ext):
        i = pl.program_id(0)

        @pl.when(i == 0)
        def _():
            ext[0:HALO, :] = jnp.zeros((HALO, DP), F32)

        x = x_ref[...]
        _, _, _, h = _norm_mod(x, mod_ref[3:4, :], mod_ref[1:2, :], mod_ref[0:1, :])
        proj = _dot(h.astype(BF16), win_ref[...])
        xa = proj[:, :DP]
        zb = proj[:, DP:]
        zb_ref[...] = zb
        ext[HALO:HALO + tm, :] = xa
        pos = i * tm + lax.broadcasted_iota(jnp.int32, (tm, 1), 0)
        vecs = vec_ref[...]
        ya_cols = []
        pooled_cols = []
        for gi, w in enumerate(POOL_WINDOWS):
            cols = slice(gi * LANE, (gi + 1) * LANE)
            s = xa[:, cols]
            for k in range(1, w):
                s = s + ext[HALO - k:HALO - k + tm, cols]
            cnt = jnp.minimum(pos + 1, w).astype(F32)
            pooled = (s / cnt - xa[:, cols]).astype(BF16)
            pooled_cols.append(pooled)
            ya_cols.append(_dot(pooled, wpool_ref[gi]) * vecs[0:1, cols])
        pooled_ref[...] = jnp.concatenate(pooled_cols, axis=1)
        ext[0:HALO, :] = ext[tm:tm + HALO, :]

        gm = _gmlp_branch(zb, vecs, wcat_ref, bias_ref, nchunks)
        cat = jnp.concatenate(ya_cols + [gm["yb"]], axis=1).astype(BF16)
        xo_ref[...] = x + mod_ref[2:3, :] * _dot(cat, wout_ref[...])

    full = lambda shape: pl.BlockSpec(shape, lambda i: (0,) * len(shape))
    return _call(
        body, name="mix_fwd",
        grid=(nt,),
        in_specs=[pl.BlockSpec((tm, D), lambda i: (i, 0)), full((8, D)), full((D, DPROJ)),
                  full((4, LANE, LANE)), full((8, DP)), full((4, CHUNK, 2 * CHUNK)), full((CHUNK, DG)),
                  full((DP + DG, D))],
        out_specs=[pl.BlockSpec((tm, D), lambda i: (i, 0)), pl.BlockSpec((tm, DP), lambda i: (i, 0)),
                   pl.BlockSpec((tm, 2 * DG), lambda i: (i, 0))],
        out_shape=[_sds((S, D), F32), _sds((S, DP), BF16), _sds((S, 2 * DG), F32)],
        scratch_shapes=[pltpu.VMEM((tm + HALO, DP), F32)],
        compiler_params=_params(("arbitrary",), VMEM_LIMIT),
    )(x, modv, win, wpool, vecs, wcat, bias, wout)


def _mix_bwd(x, dxo, pooled, zb, modv, win, wpool, vecs, wcat, wtcat, bias, wout, side=None):
    S = x.shape[0]
    tm = TM_MIX
    nt = S // tm
    nchunks = tm // CHUNK

    def body(x_ref, dxo_ref, pooled_ref, zb_ref, mod_ref, win_ref, wpool_ref, vec_ref, wcat_ref, wtcat_ref,
             bias_ref, wout_ref,
             dx_ref, dwin_ref, dwout_ref, dwpool_ref, dwsp_ref, dbsp_ref, v512_ref, vd_ref, qext, dsv_acc):
        step = pl.program_id(0)
        tile = nt - 1 - step

        @pl.when(step == 0)
        def _():
            dwin_ref[...] = jnp.zeros_like(dwin_ref)
            dwout_ref[...] = jnp.zeros_like(dwout_ref)
            dwpool_ref[...] = jnp.zeros_like(dwpool_ref)
            dwsp_ref[...] = jnp.zeros_like(dwsp_ref)
            v512_ref[...] = jnp.zeros_like(v512_ref)
            vd_ref[...] = jnp.zeros_like(vd_ref)
            dsv_acc[...] = jnp.zeros_like(dsv_acc)
            qext[tm:tm + HALO, :] = jnp.zeros((HALO, DP), F32)

        gn, sc, sh, gate = mod_ref[3:4, :], mod_ref[1:2, :], mod_ref[0:1, :], mod_ref[2:3, :]
        vecs = vec_ref[...]
        x = x_ref[...]
        r, xn, hp, h = _norm_mod(x, gn, sc, sh)
        hb = h.astype(BF16)
        dxo = dxo_ref[...]

        pooled = pooled_ref[...]
        mixed_cols = [_dot(pooled[:, gi * LANE:(gi + 1) * LANE], wpool_ref[gi]) for gi in range(4)]
        mixed = jnp.concatenate(mixed_cols, axis=1)
        scale = vecs[0:1, :]
        gm = _gmlp_branch(zb_ref[...], vecs, wcat_ref, bias_ref, nchunks)
        cat = jnp.concatenate([mixed * scale, gm["yb"]], axis=1).astype(BF16)

        dwout_ref[...] += _dot_tn(cat, dxo.astype(BF16))
        dcat = _dot_nt((dxo * gate).astype(BF16), wout_ref[...])
        dya = dcat[:, :DP]
        dyb = dcat[:, DP:]

        v512_ref[0:1, :] += jnp.sum(dya * mixed, axis=0, keepdims=True)
        dmixed = (dya * scale).astype(BF16)
        pos = tile * tm + lax.broadcasted_iota(jnp.int32, (tm, 1), 0)
        dpooled_cols = []
        for gi, w in enumerate(POOL_WINDOWS):
            cols = slice(gi * LANE, (gi + 1) * LANE)
            dp = _dot_nt(dmixed[:, cols], wpool_ref[gi])
            dwpool_ref[gi] += _dot_tn(pooled[:, cols], dmixed[:, cols])
            cnt = jnp.minimum(pos + 1, w).astype(F32)
            qext[0:tm, cols] = dp / cnt
            dpooled_cols.append(dp)
        dxa_cols = []
        for gi, w in enumerate(POOL_WINDOWS):
            cols = slice(gi * LANE, (gi + 1) * LANE)
            s = qext[0:tm, cols]
            for k in range(1, w):
                s = s + qext[k:k + tm, cols]
            dxa_cols.append(s - dpooled_cols[gi])
        qext[tm:tm + HALO, :] = qext[0:HALO, :]

        u, sv, vl = gm["u"], gm["sv"], gm["vl"]
        du = dyb * sv
        dsv = dyb * u
        dvl_cols = []
        for p in range(4):
            cols = slice(p * LANE, (p + 1) * LANE)
            dblocks = [dsv[k * CHUNK:(k + 1) * CHUNK, cols] for k in range(nchunks)]
            vblocks = [vl[k * CHUNK:(k + 1) * CHUNK, cols] for k in range(nchunks)]
            tot = dblocks[0]
            for b in dblocks[1:]:
                tot = tot + b
            dsv_acc[:, cols] += tot
            top, bot = _pair_rhs(dblocks)
            out = _dot(wtcat_ref[p], jnp.concatenate([top, bot], axis=0).astype(BF16))
            dvl_cols.append(jnp.concatenate([out[:, k * LANE:(k + 1) * LANE] for k in range(nchunks)], axis=0))
            vcat = jnp.concatenate(vblocks, axis=1).astype(BF16)
            dwsp_ref[2 * p] += _dot_nt(top.astype(BF16), vcat)
            dwsp_ref[2 * p + 1] += _dot_nt(bot.astype(BF16), vcat)
        dvl = jnp.concatenate(dvl_cols, axis=1)
        vhat, rstd = gm["vhat"], gm["rstd"]
        v512_ref[1:2, :] += jnp.sum(dvl * vhat, axis=0, keepdims=True)
        v512_ref[2:3, :] += jnp.sum(dvl, axis=0, keepdims=True)
        dvh = dvl * vecs[1:2, :]
        dv = rstd * (dvh - jnp.mean(dvh, axis=-1, keepdims=True)
                     - vhat * jnp.mean(dvh * vhat, axis=-1, keepdims=True))
        dzb = jnp.concatenate([du, dv], axis=1) * gm["dz"]

        dproj = jnp.concatenate(dxa_cols + [dzb], axis=1).astype(BF16)
        dwin_ref[...] += _dot_tn(hb, dproj)
        dh = _dot_nt(dproj, win_ref[...])
        dsh, dsc, dgn, dxin = _norm_mod_bwd(dh, r, xn, hp, gn, sc)
        vd_ref[0:1, :] += dsh
        vd_ref[1:2, :] += dsc
        vd_ref[3:4, :] += dgn
        dx_ref[...] = dxo + dxin

        @pl.when(step == nt - 1)
        def _():
            gw = dwout_ref[...]
            vd_ref[2:3, :] += jnp.sum(wout_ref[...].astype(F32) * gw, axis=0, keepdims=True)
            dwout_ref[...] = gw * gate
            row = lax.broadcasted_iota(jnp.int32, (CHUNK, CHUNK), 0)
            col = lax.broadcasted_iota(jnp.int32, (CHUNK, CHUNK), 1)
            for hh in range(8):
                dwsp_ref[hh] = jnp.where(col <= row, dwsp_ref[hh], 0.0)
            head = lax.broadcasted_iota(jnp.int32, (8, DG), 0)
            ch = lax.broadcasted_iota(jnp.int32, (8, DG), 1)
            spread = jnp.where(ch // 64 == head, 1.0, 0.0).astype(F32)
            dbsp_ref[...] = lax.dot_general(spread, dsv_acc[...], (((1,), (1,)), ((), ())),
                                            precision=HIGHEST, preferred_element_type=F32)

    full = lambda shape: pl.BlockSpec(shape, lambda s: (0,) * len(shape))
    rev = lambda cols: pl.BlockSpec((tm, cols), lambda s: (nt - 1 - s, 0))
    step = lambda s: lambda: pl.program_id(0) == s
    return _side_call(
        body, side, (step(0), None, step(nt - 1)), name="mix_bwd",
        grid=(nt,),
        in_specs=[rev(D), rev(D), rev(DP), rev(2 * DG), full((8, D)), full((D, DPROJ)), full((4, LANE, LANE)),
                  full((8, DP)), full((4, CHUNK, 2 * CHUNK)), full((4, CHUNK, 2 * CHUNK)), full((CHUNK, DG)),
                  full((DP + DG, D))],
        out_specs=[rev(D), full((D, DPROJ)), full((DP + DG, D)), full((4, LANE, LANE)), full((8, CHUNK, CHUNK)),
                   full((8, CHUNK)), full((8, DP)), full((8, D))],
        out_shape=[_sds((S, D), F32), _sds((D, DPROJ), F32), _sds((DP + DG, D), F32), _sds((4, LANE, LANE), F32),
                   _sds((8, CHUNK, CHUNK), F32), _sds((8, CHUNK), F32), _sds((8, DP), F32), _sds((8, D), F32)],
        scratch_shapes=[pltpu.VMEM((tm + HALO, DP), F32), pltpu.VMEM((CHUNK, DG), F32)],
        compiler_params=_params(("arbitrary",), VMEM_LIMIT),
        args=(x, dxo, pooled, zb, modv, win, wpool, vecs, wcat, wtcat, bias, wout))


def _chip_sum(g, rbuf, core):
    _, _, hr, cols = g.shape
    tr = _row_block(hr)

    def body(c_ref, g_ref, r_ref, o_ref):
        o_ref[...] = (g_ref[...] + r_ref[...]).astype(BF16)

    return pl.pallas_call(
        body, name="chip_sum", interpret=False,
        grid_spec=pltpu.PrefetchScalarGridSpec(
            num_scalar_prefetch=1, grid=(NQ, hr // tr),
            in_specs=[pl.BlockSpec((None, None, tr, cols), lambda q, i, c: (q, c[0], i, 0)),
                      pl.BlockSpec((None, tr, cols), lambda q, i, c: (q, i, 0))],
            out_specs=pl.BlockSpec((None, tr, cols), lambda q, i, c: (q, i, 0))),
        out_shape=_sds((NQ, hr, cols), BF16),
        compiler_params=_params(("arbitrary", "arbitrary"), None),
    )(core, g, rbuf)


def _chip_sum_pair(own, rbuf, slots=(0, 1, 2, 3), prev=None):
    _, hr, cols = own.shape
    tr = _row_block(hr)
    a, b = slots[0], (slots[1] - slots[0] if len(slots) > 1 else 0)
    assert list(slots) == [a + b * k for k in range(len(slots))]

    def body(a_ref, b_ref, *rest):
        rest[-1][...] = (a_ref[...] + b_ref[...]).astype(BF16)

    spec = pl.BlockSpec((None, tr, cols), lambda k, i: (a + b * k, i, 0))
    return _call(
        body, name="chip_sum_pair",
        grid=(len(slots), hr // tr),
        in_specs=[spec, spec] + ([ANY] if prev is not None else []), out_specs=spec,
        out_shape=_sds((NQ, hr, cols), BF16),
        input_output_aliases={2: 0} if prev is not None else {},
        compiler_params=_params(("arbitrary", "arbitrary"), None),
    )(own, rbuf, *([prev] if prev is not None else []))


def _sum4(cs, rbuf, chip):
    _, hr, cols = rbuf.shape
    tr = _row_block(hr)

    def body(q_ref, c_ref, r1_ref, r2_ref, r3_ref, o_ref):
        acc = c_ref[...].astype(F32)
        for r in (r1_ref, r2_ref, r3_ref):
            acc = acc + r[...].astype(F32)
        o_ref[...] = acc

    slot = lambda k: pl.BlockSpec((None, tr, cols), lambda i, q: ((q[0] + k) % NQ, i, 0))
    return pl.pallas_call(
        body, name="sum4", interpret=False,
        grid_spec=pltpu.PrefetchScalarGridSpec(
            num_scalar_prefetch=1, grid=(hr // tr,),
            in_specs=[slot(0), slot(1), slot(2), slot(3)],
            out_specs=pl.BlockSpec((tr, cols), lambda i, q: (i, 0))),
        out_shape=_sds((hr, cols), F32),
        compiler_params=_params(("arbitrary",), None),
    )(chip, cs, rbuf, rbuf, rbuf)


def _adamw_halves(w, own, recv, m, v, side=None):
    rows, cols = w.shape
    hr = rows // 2
    tr = _row_block(hr, mult=8)
    nb = hr // tr

    def body(w_ref, own_ref, recv_ref, m_ref, v_ref, g_ref, d_ref, mo_ref, vo_ref):
        g = jnp.where(pl.program_id(0) == lax.axis_index("c"), own_ref[...], recv_ref[...])
        d, mn, vn = _adamw(w_ref[...], g, m_ref[...], v_ref[...])
        g_ref[...] = g
        d_ref[...] = d
        mo_ref[...] = mn
        vo_ref[...] = vn

    full = pl.BlockSpec((tr, cols), lambda h, i: (h * nb + i, 0))
    half = pl.BlockSpec((tr, cols), lambda h, i: (i, 0))
    step = lambda h, i: lambda: (pl.program_id(0) == h) & (pl.program_id(1) == i)
    return _side_call(
        body, side, (step(0, 0), None, step(1, nb - 1)), name="adamw_halves",
        grid=(2, nb), in_specs=[full, half, half, full, full], out_specs=[full] * 4,
        out_shape=[_sds((rows, cols), F32)] * 4, scratch_shapes=[],
        compiler_params=_params(("arbitrary", "arbitrary"), None),
        args=(w, own, recv, m, v))


def _cast_place(w, chip):
    rows, cols = w.shape
    tr = _row_block(rows)

    def body(q_ref, w_ref, o_ref):
        o_ref[...] = w_ref[...].astype(BF16)

    return pl.pallas_call(
        body, name="cast_place", interpret=False,
        grid_spec=pltpu.PrefetchScalarGridSpec(
            num_scalar_prefetch=1, grid=(rows // tr,),
            in_specs=[pl.BlockSpec((tr, cols), lambda i, q: (i, 0))],
            out_specs=pl.BlockSpec((None, tr, cols), lambda i, q: (q[0], i, 0))),
        out_shape=_sds((NQ, rows, cols), BF16),
        compiler_params=_params(("arbitrary",), None),
    )(chip, w)


def _ada_grad_adamw(cact_t, dmod_q, w, m, v, side=None):
    rows, cols = w.shape
    tc = 256
    assert cols % tc == 0

    def body(c_ref, d_ref, w_ref, m_ref, v_ref, g_ref, dl_ref, mo_ref, vo_ref):
        g = jnp.dot(c_ref[...], d_ref[...], precision=HIGHEST, preferred_element_type=F32)
        d, mn, vn = _adamw(w_ref[...], g, m_ref[...], v_ref[...])
        g_ref[...] = g
        dl_ref[...] = d
        mo_ref[...] = mn
        vo_ref[...] = vn

    spec = pl.BlockSpec((rows, tc), lambda i: (0, i))
    step = lambda s: lambda: pl.program_id(0) == s
    return _side_call(
        body, side, (step(0), None, step(cols // tc - 1)), name="ada_grad_adamw",
        grid=(cols // tc,),
        in_specs=[pl.BlockSpec((rows, 8), lambda i: (0, 0)), pl.BlockSpec((8, tc), lambda i: (0, i)),
                  spec, spec, spec],
        out_specs=[spec] * 4,
        out_shape=[_sds((rows, cols), F32)] * 4,
        scratch_shapes=[],
        compiler_params=_params(("arbitrary",), None),
        args=(cact_t, dmod_q, w, m, v))


def _me():
    x, y, c = lax.axis_index("x"), lax.axis_index("y"), lax.axis_index("c")
    return x, y, c


_OFFSETS7 = [(dx, dy, dc) for dx in (0, 1) for dy in (0, 1) for dc in (0, 1) if (dx, dy, dc) != (0, 0, 0)]
_CHIP_OFFSETS = [(1, 0), (0, 1), (1, 1)]


def _ada_fwd(c, w_ada_q, b_ada_q, side=None):
    ncol = w_ada_q.shape[1]

    def body(c_ref, w_ref, b_ref, cact_ref, modsel_ref, blk, gath, res, parts, send_sems, recv_sems, side_start=None):
        x, y, cc = _me()
        me = 4 * x + 2 * y + cc
        q = 2 * x + y
        cv = c_ref[...]
        ca = cv * jax.nn.sigmoid(cv)
        row = lax.broadcasted_iota(jnp.int32, (8, D), 0)
        blk[...] = jnp.where(row == me, jnp.broadcast_to(ca, (8, D)), 0.0)
        gath[me] = blk[...]
        sends = []
        for k, (dx, dy, dc) in enumerate(_OFFSETS7):
            cp = pltpu.make_async_remote_copy(blk, gath.at[me], send_sems.at[k], recv_sems.at[k],
                                              device_id=(x ^ dx, y ^ dy, cc ^ dc), device_id_type=MESH)
            cp.start()
            sends.append(cp)
        if side_start is not None:
            side_start()
        for cp in sends:
            cp.wait_recv()
        cact = gath[0]
        for d in range(1, N_DEV):
            cact = cact + gath[d]
        cact_ref[...] = cact
        res[...] = jnp.dot(cact, w_ref[...], precision=HIGHEST, preferred_element_type=F32) + b_ref[...]
        parts[q] = res[...]
        sends2 = []
        for k, (dx, dy) in enumerate(_CHIP_OFFSETS):
            cp = pltpu.make_async_remote_copy(res, parts.at[q], send_sems.at[7 + k], recv_sems.at[7 + k],
                                              device_id=(x ^ dx, y ^ dy, cc), device_id_type=MESH)
            cp.start()
            sends2.append(cp)
        for cp in sends2:
            cp.wait_recv()
        row2 = lax.broadcasted_iota(jnp.int32, (8, ncol), 0)
        out = jnp.zeros((8, ncol), F32)
        for s in range(NQ):
            mine = jnp.sum(jnp.where(row2 == me, parts[s], 0.0), axis=0, keepdims=True)
            out = out + jnp.where(row2 == s, jnp.broadcast_to(mine, (8, ncol)), 0.0)
        modsel_ref[...] = out
        for cp in sends + sends2:
            cp.wait_send()

    return _side_call(
        body, side, None, name="ada_fwd",
        in_specs=[VMEM, VMEM, VMEM], out_specs=[VMEM, VMEM],
        out_shape=[_sds((8, D), F32), _sds((8, ncol), F32)],
        scratch_shapes=[pltpu.VMEM((8, D), F32), pltpu.VMEM((N_DEV, 8, D), F32), pltpu.VMEM((8, ncol), F32),
                        pltpu.VMEM((NQ, 8, ncol), F32), pltpu.SemaphoreType.DMA((10,)), pltpu.SemaphoreType.DMA((10,))],
        compiler_params=_params(None, VMEM_LIMIT), start_in_body=side is not None,
        args=(c, w_ada_q, b_ada_q))


class _Side:
    def __init__(self, ins, out_shapes, aliases, nsem, start, mid=None, finish=None):
        self.ins, self.out_shapes, self.aliases, self.nsem = list(ins), list(out_shapes), dict(aliases), nsem
        self.start, self.mid, self.finish = start, mid, finish


def _join(*sides):
    ins, outs, aliases, offs, nsem = [], [], {}, [], 0
    for s in sides:
        offs.append((len(ins), len(outs), nsem))
        aliases.update({len(ins) + a: len(outs) + b for a, b in s.aliases.items()})
        ins += s.ins
        outs += s.out_shapes
        nsem += s.nsem

    def hook(name):
        def run(i, o, ss, rs, base):
            for s, (io, oo, so) in zip(sides, offs):
                fn = getattr(s, name)
                if fn is not None:
                    fn(i[io:io + len(s.ins)], o[oo:oo + len(s.out_shapes)], ss, rs, base + so)
        return run

    return _Side(ins, outs, aliases, nsem, hook("start"), hook("mid"), hook("finish"))


def _side_call(body, side, when, *, name, in_specs, out_specs, out_shape, scratch_shapes, args, aliases=None,
               start_in_body=False, **kw):
    n_in, n_out = len(in_specs), len(out_specs)
    aliases = dict(aliases or {})
    if side is None:
        return _call(body, name=name, in_specs=in_specs, out_specs=out_specs, out_shape=out_shape,
                     scratch_shapes=scratch_shapes, input_output_aliases=aliases, **kw)(*args), []
    ns_in, ns_out = len(side.ins), len(side.out_shapes)

    def hook(fn, k, operands):
        if fn is None:
            return
        if when is None:
            fn(*operands, 0)
        elif when[k] is not None:
            pl.when(when[k]())(functools.partial(fn, *operands, 0))

    def wrapped(*refs):
        ins, s_ins = refs[:n_in], refs[n_in:n_in + ns_in]
        o0 = n_in + ns_in
        outs, s_outs = refs[o0:o0 + n_out], refs[o0 + n_out:o0 + n_out + ns_out]
        rest = refs[o0 + n_out + ns_out:]
        scratch, operands = rest[:-2], (s_ins, s_outs, rest[-2], rest[-1])
        if start_in_body:
            body(*ins, *outs, *scratch, side_start=functools.partial(hook, side.start, 0, operands))
        else:
            hook(side.start, 0, operands)
            body(*ins, *outs, *scratch)
        hook(side.mid, 1, operands)
        hook(side.finish, 2, operands)

    res = _call(
        wrapped, name=name,
        in_specs=list(in_specs) + [ANY] * ns_in, out_specs=list(out_specs) + [ANY] * ns_out,
        out_shape=list(out_shape) + side.out_shapes,
        scratch_shapes=list(scratch_shapes) + [pltpu.SemaphoreType.DMA((side.nsem,)),
                                               pltpu.SemaphoreType.DMA((side.nsem,))],
        input_output_aliases={**aliases, **{n_in + a: n_out + b for a, b in side.aliases.items()}},
        **kw)(*args, *side.ins)
    return res[:n_out], res[n_out:]


def _run_side(side, name):
    return _side_call(lambda: None, side, None, name=name, in_specs=[], out_specs=[], out_shape=[],
                      scratch_shapes=[], args=[])[1]


def _remote(src, dst, ss, rs, k, dev):
    return pltpu.make_async_remote_copy(src, dst, ss.at[k], rs.at[k], device_id=dev, device_id_type=MESH)


def _gather_side(bufs):
    n = len(bufs)

    def walk(outs, half):
        x, y, cc = _me()
        for w in range(n):
            hr = outs[w].shape[1] // 2
            rows = pl.ds((cc if half == "mine" else 1 - cc) * hr, hr)
            for j, (dx, dy) in enumerate(_CHIP_OFFSETS):
                yield w, j, (x ^ dx, y ^ dy, cc), outs[w].at[2 * (x ^ dx) + (y ^ dy), rows], outs[w].at[2 * x + y, rows]

    def start(ins, outs, ss, rs, b):
        for w, j, peer, _, own in walk(outs, "mine"):
            _remote(own, own, ss, rs, b + 6 * w + j, peer).start()

    def mid(ins, outs, ss, rs, b):
        x, y, cc = _me()
        for w, j, peer, land, _ in walk(outs, "mine"):
            _remote(land, land, ss, rs, b + 6 * w + j, peer).wait_recv()
            _remote(land, land, ss, rs, b + 6 * w + 3 + j, (x, y, 1 - cc)).start()

    def finish(ins, outs, ss, rs, b):
        x, y, cc = _me()
        for w, j, _, land, _ in walk(outs, "other"):
            _remote(land, land, ss, rs, b + 6 * w + 3 + j, (x, y, 1 - cc)).wait_recv()
        for w, j, peer, land, own in walk(outs, "mine"):
            _remote(own, own, ss, rs, b + 6 * w + j, peer).wait_send()
            _remote(land, land, ss, rs, b + 6 * w + 3 + j, (x, y, 1 - cc)).wait_send()

    return _Side(bufs, [_sds(tuple(w.shape), w.dtype) for w in bufs], {i: i for i in range(n)}, 6 * n,
                 start, mid, finish)


def _copies_side(ins, out_shapes, nsem, copies):
    def start(*a):
        for cp in copies(*a):
            cp.start()

    def finish(*a):
        for cp in copies(*a):
            cp.wait()

    return _Side(ins, out_shapes, {}, nsem, start, None, finish)


def _swap_side(gs):
    def copies(ins, outs, ss, rs, b):
        x, y, cc = _me()
        return [_remote(ins[w].at[:, 1 - cc], outs[w], ss, rs, b + w, (x, y, 1 - cc)) for w in range(len(gs))]

    return _copies_side(gs, [_sds((NQ,) + tuple(g.shape[2:]), F32) for g in gs], len(gs), copies)


def _exchange_side(cs, slots=None, prev=None):
    n = len(cs)
    slots = slots or [(0, 1, 2, 3)] * n

    def among(chip, allowed):
        hit = chip == allowed[0]
        for s in allowed[1:]:
            hit = hit | (chip == s)
        return hit

    def each(ins, outs, ss, rs, b, do_send, do_recv):
        x, y, cc = _me()
        q = 2 * x + y
        for w in range(n):
            for j, (dx, dy) in enumerate(_CHIP_OFFSETS):
                pq = 2 * (x ^ dx) + (y ^ dy)
                cp = _remote(ins[w].at[pq], outs[w].at[q], ss, rs, b + 3 * w + j, (x ^ dx, y ^ dy, cc))
                if do_send is not None:
                    pl.when(among(pq, slots[w]))(functools.partial(do_send, cp))
                if do_recv is not None:
                    pl.when(among(q, slots[w]))(functools.partial(do_recv, cp))

    def start(ins, outs, ss, rs, b):
        each(ins, outs, ss, rs, b, lambda cp: cp.start(), None)

    def finish(ins, outs, ss, rs, b):
        each(ins, outs, ss, rs, b, lambda cp: cp.wait_send(), lambda cp: cp.wait_recv())

    ins = list(cs) + (list(prev) if prev is not None else [])
    aliases = {n + w: w for w in range(n)} if prev is not None else {}
    return _Side(ins, [_sds(tuple(c.shape), c.dtype) for c in cs], aliases, 3 * n, start, None, finish)


def _share_side(fs):
    def copies(ins, outs, ss, rs, b):
        x, y, cc = _me()
        return [_remote(ins[w], outs[w], ss, rs, b + w, (x, y, 1 - cc)) for w in range(len(fs))]

    return _copies_side(fs, [_sds(tuple(f.shape), F32) for f in fs], len(fs), copies)


def _small_allreduce_adamw(g, w, m, v, nd):
    rows = g.shape[0]
    nr = rows - nd
    hr = nr // 2
    assert nd % 8 == 0 and hr % 8 == 0

    def body(g_ref, w_ref, m_ref, v_ref, gs_ref, d_ref, mo_ref, vo_ref, gath, sib, csum, slots, tot, ss, rs):
        x, y, cc = _me()
        me = 4 * x + 2 * y + cc
        q = 2 * x + y
        sibling = (x, y, 1 - cc)
        dm = g_ref.at[pl.ds(0, nd)]
        gath[me] = g_ref[0:nd, :]
        to_all = [_remote(dm, gath.at[me], ss, rs, k, (x ^ dx, y ^ dy, cc ^ dc)) for k, (dx, dy, dc) in enumerate(_OFFSETS7)]
        to_sib = _remote(g_ref.at[pl.ds(nd, nr)], sib, ss, rs, 7, sibling)
        for cp in to_all + [to_sib]:
            cp.start()
        to_sib.wait_recv()
        csum[...] = g_ref[nd:, :] + sib[...]
        mine = pl.ds(pl.multiple_of(cc * hr, 8), hr)
        slots[q] = csum[mine, :]
        to_chips = [_remote(csum.at[mine], slots.at[q], ss, rs, 8 + j, (x ^ dx, y ^ dy, cc))
                    for j, (dx, dy) in enumerate(_CHIP_OFFSETS)]
        for cp in to_chips:
            cp.start()
        for cp in to_chips:
            cp.wait_recv()
        tot[mine, :] = (slots[0] + slots[1]) + (slots[2] + slots[3])
        halves = _remote(tot.at[mine], tot.at[mine], ss, rs, 11, sibling)
        halves.start()
        for cp in to_all:
            cp.wait_recv()
        dsum = gath[0]
        for dev in range(1, N_DEV):
            dsum = dsum + gath[dev]
        halves.wait_recv()
        for lo, n, total in ((0, nd, dsum), (nd, nr, tot[...])):
            gs_ref[lo:lo + n, :] = total
            d, mn, vn = _adamw(w_ref[lo:lo + n, :], total, m_ref[lo:lo + n, :], v_ref[lo:lo + n, :])
            d_ref[lo:lo + n, :] = d
            mo_ref[lo:lo + n, :] = mn
            vo_ref[lo:lo + n, :] = vn
        for cp in to_all + [to_sib, halves] + to_chips:
            cp.wait_send()

    return _call(
        body, name="small_allreduce_adamw",
        in_specs=[VMEM] * 4, out_specs=[VMEM] * 5,
        out_shape=[_sds((rows, LANE), F32)] * 4 + [_sds((N_DEV, nd, LANE), F32)],
        scratch_shapes=[pltpu.VMEM((nr, LANE), F32), pltpu.VMEM((nr, LANE), F32), pltpu.VMEM((NQ, hr, LANE), F32),
                        pltpu.VMEM((nr, LANE), F32), pltpu.SemaphoreType.DMA((12,)), pltpu.SemaphoreType.DMA((12,))],
        compiler_params=_params(None, VMEM_LIMIT),
    )(g, w, m, v)


_SMALL = ["b_ada", "norm_ffn1_g", "norm_mix_g", "pool_scale", "gmlp_ln_g", "gmlp_ln_b", "b_spatial",
          "norm_ffn2_g", "norm_final_g", "w_pool", "w_spatial"]


def _pack(parts):
    blocks, layout, r0 = [], {}, 0
    for name in _SMALL:
        a = parts[name]
        n = a.size
        rows = -(-n // LANE)
        rows8 = -(-rows // 8) * 8
        flat = a.reshape(-1).astype(F32)
        if rows8 * LANE != n:
            flat = jnp.concatenate([flat, jnp.zeros((rows8 * LANE - n,), F32)])
        blocks.append(flat.reshape(rows8, LANE))
        layout[name] = (r0, n, a.shape)
        r0 += rows8
    return jnp.concatenate(blocks, axis=0), layout


def _unpack(packed, layout):
    out = {}
    for name, (r0, n, shape) in layout.items():
        rows = -(-n // LANE)
        out[name] = packed[r0:r0 + rows].reshape(-1)[:n].reshape(shape)
    return out


def _modv(mod9, sub, gain):
    rows = jnp.concatenate([mod9[3 * sub:3 * sub + 3], gain.reshape(1, D), jnp.zeros((4, D), F32)], axis=0)
    return rows


_BIG = ["ffn1_w_in", "ffn1_w_out", "w_mix_in", "w_mix_out", "ffn2_w_in", "ffn2_w_out"]


def kernel(x, c, w_ada, b_ada, norm_ffn1_g, ffn1_w_in, ffn1_w_out, norm_mix_g, w_mix_in, w_pool, pool_scale, gmlp_ln_g, gmlp_ln_b, w_spatial, b_spatial, w_mix_out, norm_ffn2_g, ffn2_w_in, ffn2_w_out, norm_final_g, loss_target, m_w_ada, m_b_ada, m_norm_ffn1_g, m_ffn1_w_in, m_ffn1_w_out, m_norm_mix_g, m_w_mix_in, m_w_pool, m_pool_scale, m_gmlp_ln_g, m_gmlp_ln_b, m_w_spatial, m_b_spatial, m_w_mix_out, m_norm_ffn2_g, m_ffn2_w_in, m_ffn2_w_out, m_norm_final_g, v_w_ada, v_b_ada, v_norm_ffn1_g, v_ffn1_w_in, v_ffn1_w_out, v_norm_mix_g, v_w_mix_in, v_w_pool, v_pool_scale, v_gmlp_ln_g, v_gmlp_ln_b, v_w_spatial, v_b_spatial, v_w_mix_out, v_norm_ffn2_g, v_ffn2_w_in, v_ffn2_w_out, v_norm_final_g):
    names = ["w_ada", "b_ada", "norm_ffn1_g", "ffn1_w_in", "ffn1_w_out", "norm_mix_g", "w_mix_in", "w_pool",
             "pool_scale", "gmlp_ln_g", "gmlp_ln_b", "w_spatial", "b_spatial", "w_mix_out", "norm_ffn2_g",
             "ffn2_w_in", "ffn2_w_out", "norm_final_g"]
    W = dict(zip(names, [w_ada, b_ada, norm_ffn1_g, ffn1_w_in, ffn1_w_out, norm_mix_g, w_mix_in, w_pool, pool_scale,
                         gmlp_ln_g, gmlp_ln_b, w_spatial, b_spatial, w_mix_out, norm_ffn2_g, ffn2_w_in, ffn2_w_out,
                         norm_final_g]))
    M = dict(zip(names, [m_w_ada, m_b_ada, m_norm_ffn1_g, m_ffn1_w_in, m_ffn1_w_out, m_norm_mix_g, m_w_mix_in, m_w_pool,
                         m_pool_scale, m_gmlp_ln_g, m_gmlp_ln_b, m_w_spatial, m_b_spatial, m_w_mix_out, m_norm_ffn2_g,
                         m_ffn2_w_in, m_ffn2_w_out, m_norm_final_g]))
    V = dict(zip(names, [v_w_ada, v_b_ada, v_norm_ffn1_g, v_ffn1_w_in, v_ffn1_w_out, v_norm_mix_g, v_w_mix_in, v_w_pool,
                         v_pool_scale, v_gmlp_ln_g, v_gmlp_ln_b, v_w_spatial, v_b_spatial, v_w_mix_out, v_norm_ffn2_g,
                         v_ffn2_w_in, v_ffn2_w_out, v_norm_final_g]))

    xi, yi, ci = _me()
    q = 2 * xi + yi
    core = ci.astype(jnp.int32).reshape(1)

    chip = q.astype(jnp.int32).reshape(1)
    place = lambda n: _cast_place(W[n][0], chip)

    ncol = w_ada.shape[2]
    b_q = lax.dynamic_slice_in_dim(b_ada, q * ncol, ncol, axis=1)
    (cact_all, modsel), (win1, wout1) = _ada_fwd(
        c, w_ada[0], b_q, side=_gather_side([place("ffn1_w_in"), place("ffn1_w_out")]))
    mod9 = modsel[:NQ].reshape(9, D)
    xs, target = x[0], loss_target[0]
    mv1 = _modv(mod9, 0, norm_ffn1_g[0])
    mv2 = _modv(mod9, 1, norm_mix_g[0])
    mv3 = _modv(mod9, 2, norm_ffn2_g[0])
    wcat, wtcat, bias = _prep_spatial(w_spatial[0], b_spatial[0].T)
    wpool = w_pool[0].astype(BF16)
    vecs = jnp.concatenate([pool_scale, gmlp_ln_g, gmlp_ln_b, jnp.zeros((5, DP), F32)], axis=0)
    gf = jnp.concatenate([norm_final_g.reshape(1, D), jnp.zeros((7, D), F32)], axis=0)

    later =["w_mix_in", "w_mix_out", "ffn2_w_in", "ffn2_w_out"]
    (x1, g1s, u1s), got = _ffn_fwd(xs, mv1, win1, wout1.reshape(2, CH, D), side=_gather_side([place(n) for n in later]))
    wmi, wmo, win2, wout2 = got
    wmi = jnp.transpose(wmi, (1, 0, 2)).reshape(D, DPROJ)
    wmo = wmo.reshape(DP + DG, D)
    x2, pooled, zb = _mix_fwd(x1, mv2, wmi, wpool, vecs, wcat, bias, wmo)
    (dx3, g3s, u3s, loss_blk, dgf), _ = _ffn_fwd(x2, mv3, win2, wout2.reshape(2, CH, D), head=(target, gf))

    wo1, wo2 = wout1.reshape(2, CH, D), wout2.reshape(2, CH, D)
    (dx2, oin2, oout2, rin2, rout2, vec3), _ = _ffn_bwd(x2, dx3, g3s, u3s, mv3, win2, wo2)
    cs2 = [_chip_sum_pair(oin2, rin2), _chip_sum_pair(oout2, rout2)]
    (dx1, dwmi, dwmo, dwpool, dwsp, dbsp, v512, vec2), ex2 = _mix_bwd(
        x1, dx2, pooled, zb, mv2, wmi, wpool, vecs, wcat, wtcat, bias, wmo, side=_exchange_side(cs2))
    half2 = [_sum4(cs, e, chip) for cs, e in zip(cs2, ex2)]
    qcols = w_mix_in.shape[2]
    vmix = [jnp.transpose(dwmi.reshape(D, NQ, qcols), (1, 0, 2)).reshape(NQ, 2, D // 2, qcols),
            dwmo.reshape(NQ, 2, (DP + DG) // 8, D)]
    first1, got = _ffn_bwd_pass(0, xs, dx1, g1s, u1s, mv1, win1, wo1,
                                side=_join(_swap_side(vmix), _share_side(half2)))
    sibmix, other2 = got[:2], got[2:]
    cs_mix = [_chip_sum(g, r, core) for g, r in zip(vmix, sibmix)]
    (grad_x, oin1, oout1, rin1, rout1, vec1), ex_mix = _ffn_bwd_pass(
        1, xs, dx1, g1s, u1s, mv1, win1, wo1, prev=first1[:5], side=_exchange_side(cs_mix))
    vec1 = first1[5] + vec1
    cs_ffn1 = [_chip_sum_pair(oin1, rin1), _chip_sum_pair(oout1, rout1)]

    dmod = jnp.concatenate([vec1[0:3], vec2[0:3], vec3[0:3]], axis=0)
    grads = dict(
        b_ada=dmod.reshape(1, 9 * D), norm_ffn1_g=vec1[3:4], norm_mix_g=vec2[3:4], norm_ffn2_g=vec3[3:4],
        pool_scale=v512[0:1], gmlp_ln_g=v512[1:2], gmlp_ln_b=v512[2:3], b_spatial=dbsp[None],
        norm_final_g=dgf[0], w_pool=dwpool[None], w_spatial=dwsp[None])

    gp, layout = _pack({n: grads[n] for n in _SMALL})
    gp = jnp.concatenate([gp, loss_blk, loss_blk], axis=0)
    pad = jnp.zeros((16, LANE), F32)
    wp, mp, vp = [jnp.concatenate([_pack({n: src[n] for n in _SMALL})[0], pad], axis=0) for src in (W, M, V)]
    r0, nb, _ = layout["b_ada"]
    assert r0 == 0
    out_g, out_d, out_m, out_v = {}, {}, {}, {}
    gs, dl, mo, vo, gath = _small_allreduce_adamw(gp, wp, mp, vp, nb // LANE)
    loss = gs[-16, 0]
    for packed, dst in ((gs, out_g), (dl, out_d), (mo, out_m), (vo, out_v)):
        for n, a in _unpack(packed, layout).items():
            dst[n] = a.reshape(W[n].shape)

    def update(n, own, recv):
        (g2, d, mn, vn), _ = _adamw_halves(W[n][0], own, recv, M[n][0], V[n][0])
        out_g[n], out_d[n], out_m[n], out_v[n] = g2[None], d[None], mn[None], vn[None]

    ex_ffn1 = _run_side(_exchange_side(cs_ffn1), "chip_exchange")
    dmod_q = lax.dynamic_slice_in_dim(gath.reshape(N_DEV, nb), q * ncol, ncol, axis=1)
    (ga, da, ma, va), _ = _ada_grad_adamw(cact_all.T, dmod_q, w_ada[0], m_w_ada[0], v_w_ada[0])
    out_g["w_ada"], out_d["w_ada"], out_m["w_ada"], out_v["w_ada"] = ga[None], da[None], ma[None], va[None]
    half1 = [_sum4(cs, e, chip) for cs, e in zip(cs_mix + cs_ffn1, list(ex_mix) + list(ex_ffn1))]
    other1 = _run_side(_share_side(half1), "sibling_share")
    for n, own, recv in zip(["w_mix_in", "w_mix_out", "ffn1_w_in", "ffn1_w_out", "ffn2_w_in", "ffn2_w_out"],
                            half1 + half2, list(other1) + list(other2)):
        update(n, own, recv)

    return (loss, grad_x[None], *[out_g[n] for n in names], *[out_d[n] for n in names],
            *[out_m[n] for n in names], *[out_v[n] for n in names])
```

```python
import functools
import math

import jax
import jax.numpy as jnp
from jax import lax
from jax.experimental import pallas as pl
from jax.experimental.pallas import tpu as pltpu

F32 = jnp.float32
BF16 = jnp.bfloat16
MESH = pl.DeviceIdType.MESH
HIGHEST = lax.Precision.HIGHEST

EPS = 1e-6
D = 1024
DFF = 2816
CH = DFF // 2
NQ = 4
DP = 512
DG = 512
DPROJ = DP + 2 * DG
POOL_WINDOWS = (2, 4, 8, 16)
HALO = 16
CHUNK = 128
LANE = 128
N_DEV = 8

ADAM_LR = 0.001
ADAM_B1 = 0.9
ADAM_B2 = 0.999
ADAM_EPS = 1e-08
ADAM_WD = 0.01
ADAM_STEP = 10

VMEM_LIMIT = 62 * 1024 * 1024

TM_FFN_FWD = 512
TM_FFN_BWD = 512
TM_MIX = 256


def _call(body, **kw):
    return pl.pallas_call(body, interpret=False, **kw)


def _params(sem=None, vmem=None):
    return pltpu.CompilerParams(dimension_semantics=sem, vmem_limit_bytes=vmem)


def _sds(shape, dtype):
    return jax.ShapeDtypeStruct(shape, dtype)


ANY = pl.BlockSpec(memory_space=pl.ANY)
VMEM = pl.BlockSpec(memory_space=pltpu.VMEM)
SMEM = pl.BlockSpec(memory_space=pltpu.SMEM)


def _norm_mod(x, gn, sc, sh):
    r = lax.rsqrt(jnp.mean(x * x, axis=-1, keepdims=True) + EPS)
    xn = x * r
    hp = xn * gn
    return r, xn, hp, hp * (1.0 + sc) + sh


def _norm_mod_bwd(dh, r, xn, hp, gn, sc):
    one_sc = 1.0 + sc
    dsh = jnp.sum(dh, axis=0, keepdims=True)
    dsc = jnp.sum(dh * hp, axis=0, keepdims=True)
    dgn = jnp.sum(dh * one_sc * xn, axis=0, keepdims=True)
    dxn = dh * (gn * one_sc)
    dx = r * (dxn - xn * jnp.mean(dxn * xn, axis=-1, keepdims=True))
    return dsh, dsc, dgn, dx


def _dot(a, b):
    return jnp.dot(a, b, preferred_element_type=F32)


def _dot_nt(a, b):
    return lax.dot_general(a, b, (((1,), (1,)), ((), ())), preferred_element_type=F32)


def _dot_tn(a, b):
    return lax.dot_general(a, b, (((0,), (0,)), ((), ())), preferred_element_type=F32)


_GELU_C = math.sqrt(2.0 / math.pi)
_GELU_A = 0.044715


def _gelu_fwd_bwd(x):
    x2 = x * x
    t = jnp.tanh(_GELU_C * (x + _GELU_A * x * x2))
    g = 0.5 * x * (1.0 + t)
    dg = 0.5 * (1.0 + t) + 0.5 * x * (1.0 - t * t) * (_GELU_C * (1.0 + 3.0 * _GELU_A * x2))
    return g, dg


def _adamw(w, g, m, v):
    m = ADAM_B1 * m + (1.0 - ADAM_B1) * g
    v = ADAM_B2 * v + (1.0 - ADAM_B2) * (g * g)
    m_hat = m / (1.0 - ADAM_B1 ** ADAM_STEP)
    v_hat = v / (1.0 - ADAM_B2 ** ADAM_STEP)
    delta = -ADAM_LR * (m_hat / (jnp.sqrt(v_hat) + ADAM_EPS) + ADAM_WD * w)
    return delta, m, v


def _row_block(rows, cap=256, mult=16):
    best = None
    for t in range(mult, min(rows, cap) + 1, mult):
        if rows % t == 0:
            best = t
    assert best is not None, rows
    return best


def _head_math(x, target, gf):
    r = lax.rsqrt(jnp.mean(x * x, axis=-1, keepdims=True) + EPS)
    xn = x * r
    err = xn * gf - target
    dy = err * (1.0 / D)
    dxn = dy * gf
    dx = r * (dxn - xn * jnp.mean(dxn * xn, axis=-1, keepdims=True))
    return (0.5 / D) * jnp.sum(err * err), jnp.sum(dy * xn, axis=0, keepdims=True), dx


def _ffn_fwd(x, modv, win, wout, side=None, head=None):
    S = x.shape[0]
    tm = TM_FFN_FWD
    nt = S // tm

    def body(*refs):
        if head is None:
            x_ref, mod_ref, wg_ref, wu_ref, wo_ref, xo_ref, gs_ref, us_ref, acc_scr = refs
        else:
            (x_ref, mod_ref, wg_ref, wu_ref, wo_ref, t_ref, gf_ref,
             xo_ref, gs_ref, us_ref, loss_ref, dgf_ref, acc_scr) = refs

        @pl.when((pl.program_id(0) == 0) & (pl.program_id(1) == 0))
        def _():
            acc_scr[...] = jnp.zeros_like(acc_scr)
            if head is not None:
                loss_ref[...] = jnp.zeros_like(loss_ref)
                dgf_ref[...] = jnp.zeros_like(dgf_ref)

        j = pl.program_id(1)
        h = _norm_mod(x_ref[...], mod_ref[3:4, :], mod_ref[1:2, :], mod_ref[0:1, :])[3].astype(BF16)
        g = _dot(h, wg_ref[...]).astype(BF16)
        u = _dot(h, wu_ref[...]).astype(BF16)
        gs_ref[...] = g
        us_ref[...] = u
        gf = g.astype(F32)
        a = (gf * jax.nn.sigmoid(gf) * u.astype(F32)).astype(BF16)
        acc = jnp.where(j == 0, 0.0, acc_scr[...]) + _dot(a, wo_ref[...])
        acc_scr[...] = acc
        xo = x_ref[...] + (0.5 * mod_ref[2:3, :]) * acc
        if head is None:
            xo_ref[...] = xo
        else:
            @pl.when(j == 1)
            def _():
                loss, dgf, dx = _head_math(xo, t_ref[...], gf_ref[0:1, :])
                loss_ref[...] += loss
                dgf_ref[0:1, :] += dgf
                xo_ref[...] = dx

    step = lambda i, j: lambda: (pl.program_id(0) == i) & (pl.program_id(1) == j)
    tile = pl.BlockSpec((tm, D), lambda i, j: (i, 0))
    const = lambda shape: pl.BlockSpec(shape, lambda i, j: (0, 0))
    chunk = pl.BlockSpec((tm, CH), lambda i, j: (i, j))
    in_specs = [tile, const((8, D)), pl.BlockSpec((None, D, CH), lambda i, j: (j, 0, 0)),
                pl.BlockSpec((None, D, CH), lambda i, j: (2 + j, 0, 0)), pl.BlockSpec((None, CH, D), lambda i, j: (j, 0, 0))]
    out_specs = [tile, chunk, chunk]
    out_shape = [_sds((S, D), F32), _sds((S, DFF), BF16), _sds((S, DFF), BF16)]
    args = (x, modv, win, win, wout)
    if head is not None:
        in_specs += [tile, const((8, D))]
        out_specs += [const((8, LANE)), const((8, D))]
        out_shape += [_sds((8, LANE), F32), _sds((8, D), F32)]
        args += tuple(head)
    return _side_call(
        body, side, (step(0, 0), step((7 * nt) // 10, 0), step(nt - 1, 1)), name="ffn_fwd",
        grid=(nt, 2), in_specs=in_specs, out_specs=out_specs, out_shape=out_shape,
        scratch_shapes=[pltpu.VMEM((tm, D), F32)],
        compiler_params=_params(("arbitrary", "arbitrary"), VMEM_LIMIT),
        args=args)


def _ffn_bwd_pass(jj, x, dxo, gs, us, modv, win, wout, prev=None, side=None):
    S = x.shape[0]
    tm = TM_FFN_BWD
    nsub = tm // 256
    nt = S // tm
    hi, ho = D // 2, CH // 4
    last = prev is not None
    assert last == (jj == 1)

    def body(*refs):
        x_ref, dxo_ref, gs_ref, us_ref, mod_ref, wg_ref, wu_ref, wo_ref = refs[:8]
        k = 13 if last else 8
        out_ref, dwin_ref, dwout_ref, rwin_ref, rwout_ref, vec_ref = refs[k:k + 6]
        accg, accu, accw, sems, fsend, frecv = refs[k + 6:]
        i = pl.program_id(0)

        @pl.when(i == 0)
        def _():
            accg[...] = jnp.zeros_like(accg)
            accu[...] = jnp.zeros_like(accu)
            accw[...] = jnp.zeros_like(accw)
            vec_ref[...] = jnp.zeros_like(vec_ref)

        gn, sc, sh, gate = mod_ref[3:4, :], mod_ref[1:2, :], mod_ref[0:1, :], mod_ref[2:3, :]

        parts = []
        for s in range(nsub):
            rs = slice(s * (tm // nsub), (s + 1) * (tm // nsub))
            r, xn, hp, h = _norm_mod(x_ref[rs, :], gn, sc, sh)
            dxo = dxo_ref[rs, :]
            dy = (dxo * (0.5 * gate)).astype(BF16)
            g = gs_ref[rs, :].astype(F32)
            u = us_ref[rs, :].astype(F32)
            sig = jax.nn.sigmoid(g)
            sl = g * sig
            a = (sl * u).astype(BF16)
            da = _dot_nt(dy, wo_ref[...])
            dg = (da * u * (sig * (1.0 + g * (1.0 - sig)))).astype(BF16)
            du = (da * sl).astype(BF16)
            dhp = _dot_nt(dg, wg_ref[...]) + _dot_nt(du, wu_ref[...])
            parts.append((h.astype(BF16), a, dg, du, dxo.astype(BF16)))
            if last:
                dsh, dsc, dgn, dxin = _norm_mod_bwd(refs[8][rs, :] + dhp, r, xn, hp, gn, sc)
                vec_ref[0:1, :] += dsh
                vec_ref[1:2, :] += dsc
                vec_ref[3:4, :] += dgn
                out_ref[rs, :] = dxo + dxin
            else:
                out_ref[rs, :] = dhp

        hb, a, dg, du, dxb = [jnp.concatenate(p, axis=0) if nsub > 1 else p[0] for p in zip(*parts)]
        accw[...] += _dot_tn(a, dxb)
        accg[...] += _dot_tn(hb, dg)
        accu[...] += _dot_tn(hb, du)

        @pl.when(i == nt - 1)
        def _():
            gw = accw[...]
            vec_ref[2:3, :] += 0.5 * jnp.sum(wo_ref[...].astype(F32) * gw, axis=0, keepdims=True)
            accw[...] = gw * (0.5 * gate)
            mx, my, cc = _me()
            rows = lambda base, n, c: pl.ds(base + c * n, n)
            pieces = [(accg, 0, hi, dwin_ref, rwin_ref, jj), (accu, 0, hi, dwin_ref, rwin_ref, 2 + jj),
                      (accw, 0, ho, dwout_ref, rwout_ref, 2 * jj), (accw, 2 * ho, ho, dwout_ref, rwout_ref, 2 * jj + 1)]
            loc = [pltpu.make_async_copy(acc.at[rows(base, n, cc)], own.at[slot], sems.at[p])
                   for p, (acc, base, n, own, _, slot) in enumerate(pieces)]
            rem = [pltpu.make_async_remote_copy(acc.at[rows(base, n, 1 - cc)], sib.at[slot], fsend.at[p], frecv.at[p],
                                                device_id=(mx, my, 1 - cc), device_id_type=MESH)
                   for p, (acc, base, n, _, sib, slot) in enumerate(pieces)]
            for cp in loc + rem:
                cp.start()
            for cp in loc:
                cp.wait()
            for cp in rem:
                cp.wait()

    once = pl.Buffered(1)
    tile = pl.BlockSpec((tm, D), lambda i: (i, 0))
    chunk = pl.BlockSpec((tm, CH), lambda i: (i, jj))
    in_specs = [tile, tile, chunk, chunk, pl.BlockSpec((8, D), lambda i: (0, 0)),
                pl.BlockSpec((None, D, CH), lambda i: (jj, 0, 0), pipeline_mode=once),
                pl.BlockSpec((None, D, CH), lambda i: (2 + jj, 0, 0), pipeline_mode=once),
                pl.BlockSpec((None, CH, D), lambda i: (jj, 0, 0), pipeline_mode=once)]
    args = (x, dxo, gs, us, modv, win, win, wout)
    if last:
        in_specs += [tile, ANY, ANY, ANY, ANY]
        args += tuple(prev)
    step = lambda s: lambda: pl.program_id(0) == s
    return _side_call(
        body, side, (step(0), None, step(nt - 1)), name="ffn_bwd",
        grid=(nt,), in_specs=in_specs,
        out_specs=[tile, ANY, ANY, ANY, ANY, pl.BlockSpec((8, D), lambda i: (0, 0))],
        out_shape=[_sds((S, D), F32), _sds((NQ, hi, CH), F32), _sds((NQ, ho, D), F32), _sds((NQ, hi, CH), F32),
                   _sds((NQ, ho, D), F32), _sds((8, D), F32)],
        scratch_shapes=[pltpu.VMEM((D, CH), F32), pltpu.VMEM((D, CH), F32), pltpu.VMEM((CH, D), F32),
                        pltpu.SemaphoreType.DMA((4,)), pltpu.SemaphoreType.DMA((4,)), pltpu.SemaphoreType.DMA((4,))],
        aliases={9 + p: 1 + p for p in range(4)} if last else {},
        compiler_params=_params(("arbitrary",), VMEM_LIMIT),
        args=args)


def _ffn_bwd(x, dxo, gs, us, modv, win, wout, side=None):
    first, extra = _ffn_bwd_pass(0, x, dxo, gs, us, modv, win, wout, side=side)
    (dx, dwin, dwout, rwin, rwout, vec), _ = _ffn_bwd_pass(1, x, dxo, gs, us, modv, win, wout, prev=first[:5])
    return (dx, dwin, dwout, rwin, rwout, first[5] + vec), extra


def _prep_spatial(w_spatial, b_spatial_t):
    def body(w_ref, b_ref, wcat_ref, wtcat_ref, bias_ref):
        row = lax.broadcasted_iota(jnp.int32, (CHUNK, CHUNK), 0)
        col = lax.broadcasted_iota(jnp.int32, (CHUNK, CHUNK), 1)
        tril = col <= row
        for p in range(4):
            wa = jnp.where(tril, w_ref[2 * p], 0.0)
            wb = jnp.where(tril, w_ref[2 * p + 1], 0.0)
            wcat_ref[p] = jnp.concatenate([wa, wb], axis=1).astype(BF16)
            wtcat_ref[p] = jnp.concatenate([wa.T, wb.T], axis=1).astype(BF16)
        head = lax.broadcasted_iota(jnp.int32, (8, DG), 0)
        ch = lax.broadcasted_iota(jnp.int32, (8, DG), 1)
        spread = jnp.where(ch // 64 == head, 1.0, 0.0).astype(F32)
        bias_ref[...] = jnp.dot(b_ref[...], spread, precision=HIGHEST, preferred_element_type=F32)

    return _call(
        body, name="prep_spatial",
        in_specs=[VMEM, VMEM], out_specs=[VMEM, VMEM, VMEM],
        out_shape=[_sds((4, CHUNK, 2 * CHUNK), BF16), _sds((4, CHUNK, 2 * CHUNK), BF16), _sds((CHUNK, DG), F32)],
    )(w_spatial, b_spatial_t)


def _pair_rhs(blocks):
    lane = lax.broadcasted_iota(jnp.int32, (CHUNK, LANE), 1)
    lo = lane < 64
    top = jnp.concatenate([jnp.where(lo, b, 0.0) for b in blocks], axis=1)
    bot = jnp.concatenate([jnp.where(lo, 0.0, b) for b in blocks], axis=1)
    return top, bot


def _gmlp_branch(zb, vecs, wcat_ref, bias_ref, nchunks):
    z, dz = _gelu_fwd_bwd(zb)
    u = z[:, :DG]
    v = z[:, DG:]
    ln_g, ln_b = vecs[1:2, :], vecs[2:3, :]
    mu = jnp.mean(v, axis=-1, keepdims=True)
    vc = v - mu
    rstd = lax.rsqrt(jnp.mean(vc * vc, axis=-1, keepdims=True) + EPS)
    vhat = vc * rstd
    vl = vhat * ln_g + ln_b
    sv_cols = []
    for p in range(4):
        blocks = [vl[k * CHUNK:(k + 1) * CHUNK, p * LANE:(p + 1) * LANE] for k in range(nchunks)]
        top, bot = _pair_rhs(blocks)
        rhs = jnp.concatenate([top, bot], axis=0).astype(BF16)
        out = _dot(wcat_ref[p], rhs)
        bias = bias_ref[:, p * LANE:(p + 1) * LANE]
        sv_cols.append(jnp.concatenate([out[:, k * LANE:(k + 1) * LANE] + bias for k in range(nchunks)], axis=0))
    sv = jnp.concatenate(sv_cols, axis=1)
    return dict(u=u, dz=dz, rstd=rstd, vhat=vhat, vl=vl, sv=sv, yb=u * sv)


def _mix_fwd(x, modv, win, wpool, vecs, wcat, bias, wout):
    S = x.shape[0]
    tm = TM_MIX
    nt = S // tm
    nchunks = tm // CHUNK

    def body(x_ref, mod_ref, win_ref, wpool_ref, vec_ref, wcat_ref, bias_ref, wout_ref,
             xo_ref, pooled_ref, zb_ref, ext):
        i = pl.program_id(0)

        @pl.when(i == 0)
        def _():
            ext[0:HALO, :] = jnp.zeros((HALO, DP), F32)

        x = x_ref[...]
        _, _, _, h = _norm_mod(x, mod_ref[3:4, :], mod_ref[1:2, :], mod_ref[0:1, :])
        proj = _dot(h.astype(BF16), win_ref[...])
        xa = proj[:, :DP]
        zb = proj[:, DP:]
        zb_ref[...] = zb
        ext[HALO:HALO + tm, :] = xa
        pos = i * tm + lax.broadcasted_iota(jnp.int32, (tm, 1), 0)
        vecs = vec_ref[...]
        ya_cols = []
        pooled_cols = []
        for gi, w in enumerate(POOL_WINDOWS):
            cols = slice(gi * LANE, (gi + 1) * LANE)
            s = xa[:, cols]
            for k in range(1, w):
                s = s + ext[HALO - k:HALO - k + tm, cols]
            cnt = jnp.minimum(pos + 1, w).astype(F32)
            pooled = (s / cnt - xa[:, cols]).astype(BF16)
            pooled_cols.append(pooled)
            ya_cols.append(_dot(pooled, wpool_ref[gi]) * vecs[0:1, cols])
        pooled_ref[...] = jnp.concatenate(pooled_cols, axis=1)
        ext[0:HALO, :] = ext[tm:tm + HALO, :]

        gm = _gmlp_branch(zb, vecs, wcat_ref, bias_ref, nchunks)
        cat = jnp.concatenate(ya_cols + [gm["yb"]], axis=1).astype(BF16)
        xo_ref[...] = x + mod_ref[2:3, :] * _dot(cat, wout_ref[...])

    full = lambda shape: pl.BlockSpec(shape, lambda i: (0,) * len(shape))
    return _call(
        body, name="mix_fwd",
        grid=(nt,),
        in_specs=[pl.BlockSpec((tm, D), lambda i: (i, 0)), full((8, D)), full((D, DPROJ)),
                  full((4, LANE, LANE)), full((8, DP)), full((4, CHUNK, 2 * CHUNK)), full((CHUNK, DG)),
                  full((DP + DG, D))],
        out_specs=[pl.BlockSpec((tm, D), lambda i: (i, 0)), pl.BlockSpec((tm, DP), lambda i: (i, 0)),
                   pl.BlockSpec((tm, 2 * DG), lambda i: (i, 0))],
        out_shape=[_sds((S, D), F32), _sds((S, DP), BF16), _sds((S, 2 * DG), F32)],
        scratch_shapes=[pltpu.VMEM((tm + HALO, DP), F32)],
        compiler_params=_params(("arbitrary",), VMEM_LIMIT),
    )(x, modv, win, wpool, vecs, wcat, bias, wout)


def _mix_bwd(x, dxo, pooled, zb, modv, win, wpool, vecs, wcat, wtcat, bias, wout, side=None):
    S = x.shape[0]
    tm = TM_MIX
    nt = S // tm
    nchunks = tm // CHUNK

    def body(x_ref, dxo_ref, pooled_ref, zb_ref, mod_ref, win_ref, wpool_ref, vec_ref, wcat_ref, wtcat_ref,
             bias_ref, wout_ref,
             dx_ref, dwin_ref, dwout_ref, dwpool_ref, dwsp_ref, dbsp_ref, v512_ref, vd_ref, qext, dsv_acc):
        step = pl.program_id(0)
        tile = nt - 1 - step

        @pl.when(step == 0)
        def _():
            dwin_ref[...] = jnp.zeros_like(dwin_ref)
            dwout_ref[...] = jnp.zeros_like(dwout_ref)
            dwpool_ref[...] = jnp.zeros_like(dwpool_ref)
            dwsp_ref[...] = jnp.zeros_like(dwsp_ref)
            v512_ref[...] = jnp.zeros_like(v512_ref)
            vd_ref[...] = jnp.zeros_like(vd_ref)
            dsv_acc[...] = jnp.zeros_like(dsv_acc)
            qext[tm:tm + HALO, :] = jnp.zeros((HALO, DP), F32)

        gn, sc, sh, gate = mod_ref[3:4, :], mod_ref[1:2, :], mod_ref[0:1, :], mod_ref[2:3, :]
        vecs = vec_ref[...]
        x = x_ref[...]
        r, xn, hp, h = _norm_mod(x, gn, sc, sh)
        hb = h.astype(BF16)
        dxo = dxo_ref[...]

        pooled = pooled_ref[...]
        mixed_cols = [_dot(pooled[:, gi * LANE:(gi + 1) * LANE], wpool_ref[gi]) for gi in range(4)]
        mixed = jnp.concatenate(mixed_cols, axis=1)
        scale = vecs[0:1, :]
        gm = _gmlp_branch(zb_ref[...], vecs, wcat_ref, bias_ref, nchunks)
        cat = jnp.concatenate([mixed * scale, gm["yb"]], axis=1).astype(BF16)

        dwout_ref[...] += _dot_tn(cat, dxo.astype(BF16))
        dcat = _dot_nt((dxo * gate).astype(BF16), wout_ref[...])
        dya = dcat[:, :DP]
        dyb = dcat[:, DP:]

        v512_ref[0:1, :] += jnp.sum(dya * mixed, axis=0, keepdims=True)
        dmixed = (dya * scale).astype(BF16)
        pos = tile * tm + lax.broadcasted_iota(jnp.int32, (tm, 1), 0)
        dpooled_cols = []
        for gi, w in enumerate(POOL_WINDOWS):
            cols = slice(gi * LANE, (gi + 1) * LANE)
            dp = _dot_nt(dmixed[:, cols], wpool_ref[gi])
            dwpool_ref[gi] += _dot_tn(pooled[:, cols], dmixed[:, cols])
            cnt = jnp.minimum(pos + 1, w).astype(F32)
            qext[0:tm, cols] = dp / cnt
            dpooled_cols.append(dp)
        dxa_cols = []
        for gi, w in enumerate(POOL_WINDOWS):
            cols = slice(gi * LANE, (gi + 1) * LANE)
            s = qext[0:tm, cols]
            for k in range(1, w):
                s = s + qext[k:k + tm, cols]
            dxa_cols.append(s - dpooled_cols[gi])
        qext[tm:tm + HALO, :] = qext[0:HALO, :]

        u, sv, vl = gm["u"], gm["sv"], gm["vl"]
        du = dyb * sv
        dsv = dyb * u
        dvl_cols = []
        for p in range(4):
            cols = slice(p * LANE, (p + 1) * LANE)
            dblocks = [dsv[k * CHUNK:(k + 1) * CHUNK, cols] for k in range(nchunks)]
            vblocks = [vl[k * CHUNK:(k + 1) * CHUNK, cols] for k in range(nchunks)]
            tot = dblocks[0]
            for b in dblocks[1:]:
                tot = tot + b
            dsv_acc[:, cols] += tot
            top, bot = _pair_rhs(dblocks)
            out = _dot(wtcat_ref[p], jnp.concatenate([top, bot], axis=0).astype(BF16))
            dvl_cols.append(jnp.concatenate([out[:, k * LANE:(k + 1) * LANE] for k in range(nchunks)], axis=0))
            vcat = jnp.concatenate(vblocks, axis=1).astype(BF16)
            dwsp_ref[2 * p] += _dot_nt(top.astype(BF16), vcat)
            dwsp_ref[2 * p + 1] += _dot_nt(bot.astype(BF16), vcat)
        dvl = jnp.concatenate(dvl_cols, axis=1)
        vhat, rstd = gm["vhat"], gm["rstd"]
        v512_ref[1:2, :] += jnp.sum(dvl * vhat, axis=0, keepdims=True)
        v512_ref[2:3, :] += jnp.sum(dvl, axis=0, keepdims=True)
        dvh = dvl * vecs[1:2, :]
        dv = rstd * (dvh - jnp.mean(dvh, axis=-1, keepdims=True)
                     - vhat * jnp.mean(dvh * vhat, axis=-1, keepdims=True))
        dzb = jnp.concatenate([du, dv], axis=1) * gm["dz"]

        dproj = jnp.concatenate(dxa_cols + [dzb], axis=1).astype(BF16)
        dwin_ref[...] += _dot_tn(hb, dproj)
        dh = _dot_nt(dproj, win_ref[...])
        dsh, dsc, dgn, dxin = _norm_mod_bwd(dh, r, xn, hp, gn, sc)
        vd_ref[0:1, :] += dsh
        vd_ref[1:2, :] += dsc
        vd_ref[3:4, :] += dgn
        dx_ref[...] = dxo + dxin

        @pl.when(step == nt - 1)
        def _():
            gw = dwout_ref[...]
            vd_ref[2:3, :] += jnp.sum(wout_ref[...].astype(F32) * gw, axis=0, keepdims=True)
            dwout_ref[...] = gw * gate
            row = lax.broadcasted_iota(jnp.int32, (CHUNK, CHUNK), 0)
            col = lax.broadcasted_iota(jnp.int32, (CHUNK, CHUNK), 1)
            for hh in range(8):
                dwsp_ref[hh] = jnp.where(col <= row, dwsp_ref[hh], 0.0)
            head = lax.broadcasted_iota(jnp.int32, (8, DG), 0)
            ch = lax.broadcasted_iota(jnp.int32, (8, DG), 1)
            spread = jnp.where(ch // 64 == head, 1.0, 0.0).astype(F32)
            dbsp_ref[...] = lax.dot_general(spread, dsv_acc[...], (((1,), (1,)), ((), ())),
                                            precision=HIGHEST, preferred_element_type=F32)

    full = lambda shape: pl.BlockSpec(shape, lambda s: (0,) * len(shape))
    rev = lambda cols: pl.BlockSpec((tm, cols), lambda s: (nt - 1 - s, 0))
    step = lambda s: lambda: pl.program_id(0) == s
    return _side_call(
        body, side, (step(0), None, step(nt - 1)), name="mix_bwd",
        grid=(nt,),
        in_specs=[rev(D), rev(D), rev(DP), rev(2 * DG), full((8, D)), full((D, DPROJ)), full((4, LANE, LANE)),
                  full((8, DP)), full((4, CHUNK, 2 * CHUNK)), full((4, CHUNK, 2 * CHUNK)), full((CHUNK, DG)),
                  full((DP + DG, D))],
        out_specs=[rev(D), full((D, DPROJ)), full((DP + DG, D)), full((4, LANE, LANE)), full((8, CHUNK, CHUNK)),
                   full((8, CHUNK)), full((8, DP)), full((8, D))],
        out_shape=[_sds((S, D), F32), _sds((D, DPROJ), F32), _sds((DP + DG, D), F32), _sds((4, LANE, LANE), F32),
                   _sds((8, CHUNK, CHUNK), F32), _sds((8, CHUNK), F32), _sds((8, DP), F32), _sds((8, D), F32)],
        scratch_shapes=[pltpu.VMEM((tm + HALO, DP), F32), pltpu.VMEM((CHUNK, DG), F32)],
        compiler_params=_params(("arbitrary",), VMEM_LIMIT),
        args=(x, dxo, pooled, zb, modv, win, wpool, vecs, wcat, wtcat, bias, wout))


def _chip_sum(g, rbuf, core):
    _, _, hr, cols = g.shape
    tr = _row_block(hr)

    def body(c_ref, g_ref, r_ref, o_ref):
        o_ref[...] = (g_ref[...] + r_ref[...]).astype(BF16)

    return pl.pallas_call(
        body, name="chip_sum", interpret=False,
        grid_spec=pltpu.PrefetchScalarGridSpec(
            num_scalar_prefetch=1, grid=(NQ, hr // tr),
            in_specs=[pl.BlockSpec((None, None, tr, cols), lambda q, i, c: (q, c[0], i, 0)),
                      pl.BlockSpec((None, tr, cols), lambda q, i, c: (q, i, 0))],
            out_specs=pl.BlockSpec((None, tr, cols), lambda q, i, c: (q, i, 0))),
        out_shape=_sds((NQ, hr, cols), BF16),
        compiler_params=_params(("arbitrary", "arbitrary"), None),
    )(core, g, rbuf)


def _chip_sum_pair(own, rbuf, slots=(0, 1, 2, 3), prev=None):
    _, hr, cols = own.shape
    tr = _row_block(hr)
    a, b = slots[0], (slots[1] - slots[0] if len(slots) > 1 else 0)
    assert list(slots) == [a + b * k for k in range(len(slots))]

    def body(a_ref, b_ref, *rest):
        rest[-1][...] = (a_ref[...] + b_ref[...]).astype(BF16)

    spec = pl.BlockSpec((None, tr, cols), lambda k, i: (a + b * k, i, 0))
    return _call(
        body, name="chip_sum_pair",
        grid=(len(slots), hr // tr),
        in_specs=[spec, spec] + ([ANY] if prev is not None else []), out_specs=spec,
        out_shape=_sds((NQ, hr, cols), BF16),
        input_output_aliases={2: 0} if prev is not None else {},
        compiler_params=_params(("arbitrary", "arbitrary"), None),
    )(own, rbuf, *([prev] if prev is not None else []))


def _sum4(cs, rbuf, chip):
    _, hr, cols = rbuf.shape
    tr = _row_block(hr)

    def body(q_ref, c_ref, r1_ref, r2_ref, r3_ref, o_ref):
        acc = c_ref[...].astype(F32)
        for r in (r1_ref, r2_ref, r3_ref):
            acc = acc + r[...].astype(F32)
        o_ref[...] = acc

    slot = lambda k: pl.BlockSpec((None, tr, cols), lambda i, q: ((q[0] + k) % NQ, i, 0))
    return pl.pallas_call(
        body, name="sum4", interpret=False,
        grid_spec=pltpu.PrefetchScalarGridSpec(
            num_scalar_prefetch=1, grid=(hr // tr,),
            in_specs=[slot(0), slot(1), slot(2), slot(3)],
            out_specs=pl.BlockSpec((tr, cols), lambda i, q: (i, 0))),
        out_shape=_sds((hr, cols), F32),
        compiler_params=_params(("arbitrary",), None),
    )(chip, cs, rbuf, rbuf, rbuf)


def _adamw_halves(w, own, recv, m, v, side=None, after=()):
    rows, cols = w.shape
    hr = rows // 2
    tr = _row_block(hr, mult=8)
    nb = hr // tr

    def body(w_ref, own_ref, recv_ref, m_ref, v_ref, *rest):
        g_ref, d_ref, mo_ref, vo_ref = rest[len(after):]
        g = jnp.where(pl.program_id(0) == lax.axis_index("c"), own_ref[...], recv_ref[...])
        d, mn, vn = _adamw(w_ref[...], g, m_ref[...], v_ref[...])
        g_ref[...] = g
        d_ref[...] = d
        mo_ref[...] = mn
        vo_ref[...] = vn

    full = pl.BlockSpec((tr, cols), lambda h, i: (h * nb + i, 0))
    half = pl.BlockSpec((tr, cols), lambda h, i: (i, 0))
    step = lambda h, i: lambda: (pl.program_id(0) == h) & (pl.program_id(1) == i)
    return _side_call(
        body, side, (step(0, 0), None, step(1, nb - 1)), name="adamw_halves",
        grid=(2, nb), in_specs=[full, half, half, full, full] + [ANY] * len(after), out_specs=[full] * 4,
        out_shape=[_sds((rows, cols), F32)] * 4, scratch_shapes=[],
        compiler_params=_params(("arbitrary", "arbitrary"), None),
        args=(w, own, recv, m, v, *after))


def _cast_place(w, chip):
    rows, cols = w.shape
    tr = _row_block(rows)

    def body(q_ref, w_ref, o_ref):
        o_ref[...] = w_ref[...].astype(BF16)

    return pl.pallas_call(
        body, name="cast_place", interpret=False,
        grid_spec=pltpu.PrefetchScalarGridSpec(
            num_scalar_prefetch=1, grid=(rows // tr,),
            in_specs=[pl.BlockSpec((tr, cols), lambda i, q: (i, 0))],
            out_specs=pl.BlockSpec((None, tr, cols), lambda i, q: (q[0], i, 0))),
        out_shape=_sds((NQ, rows, cols), BF16),
        compiler_params=_params(("arbitrary",), None),
    )(chip, w)


def _ada_grad_adamw(cact_t, dmod_q, w, m, v, side=None):
    rows, cols = w.shape
    tc = 256
    assert cols % tc == 0

    def body(c_ref, d_ref, w_ref, m_ref, v_ref, g_ref, dl_ref, mo_ref, vo_ref):
        g = jnp.dot(c_ref[...], d_ref[...], precision=HIGHEST, preferred_element_type=F32)
        d, mn, vn = _adamw(w_ref[...], g, m_ref[...], v_ref[...])
        g_ref[...] = g
        dl_ref[...] = d
        mo_ref[...] = mn
        vo_ref[...] = vn

    spec = pl.BlockSpec((rows, tc), lambda i: (0, i))
    step = lambda s: lambda: pl.program_id(0) == s
    return _side_call(
        body, side, (step(0), None, step(cols // tc - 1)), name="ada_grad_adamw",
        grid=(cols // tc,),
        in_specs=[pl.BlockSpec((rows, 8), lambda i: (0, 0)), pl.BlockSpec((8, tc), lambda i: (0, i)),
                  spec, spec, spec],
        out_specs=[spec] * 4,
        out_shape=[_sds((rows, cols), F32)] * 4,
        scratch_shapes=[],
        compiler_params=_params(("arbitrary",), None),
        args=(cact_t, dmod_q, w, m, v))


def _me():
    x, y, c = lax.axis_index("x"), lax.axis_index("y"), lax.axis_index("c")
    return x, y, c


_OFFSETS7 = [(dx, dy, dc) for dx in (0, 1) for dy in (0, 1) for dc in (0, 1) if (dx, dy, dc) != (0, 0, 0)]
_CHIP_OFFSETS = [(1, 0), (0, 1), (1, 1)]


def _ada_fwd(c, w_ada_q, b_ada_q, side=None):
    ncol = w_ada_q.shape[1]

    def body(c_ref, w_ref, b_ref, cact_ref, modsel_ref, blk, gath, res, parts, send_sems, recv_sems, side_start=None):
        x, y, cc = _me()
        me = 4 * x + 2 * y + cc
        q = 2 * x + y
        cv = c_ref[...]
        ca = cv * jax.nn.sigmoid(cv)
        row = lax.broadcasted_iota(jnp.int32, (8, D), 0)
        blk[...] = jnp.where(row == me, jnp.broadcast_to(ca, (8, D)), 0.0)
        gath[me] = blk[...]
        sends = []
        for k, (dx, dy, dc) in enumerate(_OFFSETS7):
            cp = pltpu.make_async_remote_copy(blk, gath.at[me], send_sems.at[k], recv_sems.at[k],
                                              device_id=(x ^ dx, y ^ dy, cc ^ dc), device_id_type=MESH)
            cp.start()
            sends.append(cp)
        if side_start is not None:
            side_start()
        for cp in sends:
            cp.wait_recv()
        cact = gath[0]
        for d in range(1, N_DEV):
            cact = cact + gath[d]
        cact_ref[...] = cact
        res[...] = jnp.dot(cact, w_ref[...], precision=HIGHEST, preferred_element_type=F32) + b_ref[...]
        parts[q] = res[...]
        sends2 = []
        for k, (dx, dy) in enumerate(_CHIP_OFFSETS):
            cp = pltpu.make_async_remote_copy(res, parts.at[q], send_sems.at[7 + k], recv_sems.at[7 + k],
                                              device_id=(x ^ dx, y ^ dy, cc), device_id_type=MESH)
            cp.start()
            sends2.append(cp)
        for cp in sends2:
            cp.wait_recv()
        row2 = lax.broadcasted_iota(jnp.int32, (8, ncol), 0)
        out = jnp.zeros((8, ncol), F32)
        for s in range(NQ):
            mine = jnp.sum(jnp.where(row2 == me, parts[s], 0.0), axis=0, keepdims=True)
            out = out + jnp.where(row2 == s, jnp.broadcast_to(mine, (8, ncol)), 0.0)
        modsel_ref[...] = out
        for cp in sends + sends2:
            cp.wait_send()

    return _side_call(
        body, side, None, name="ada_fwd",
        in_specs=[VMEM, VMEM, VMEM], out_specs=[VMEM, VMEM],
        out_shape=[_sds((8, D), F32), _sds((8, ncol), F32)],
        scratch_shapes=[pltpu.VMEM((8, D), F32), pltpu.VMEM((N_DEV, 8, D), F32), pltpu.VMEM((8, ncol), F32),
                        pltpu.VMEM((NQ, 8, ncol), F32), pltpu.SemaphoreType.DMA((10,)), pltpu.SemaphoreType.DMA((10,))],
        compiler_params=_params(None, VMEM_LIMIT), start_in_body=side is not None,
        args=(c, w_ada_q, b_ada_q))


class _Side:
    def __init__(self, ins, out_shapes, aliases, nsem, start, mid=None, finish=None):
        self.ins, self.out_shapes, self.aliases, self.nsem = list(ins), list(out_shapes), dict(aliases), nsem
        self.start, self.mid, self.finish = start, mid, finish


def _join(*sides):
    ins, outs, aliases, offs, nsem = [], [], {}, [], 0
    for s in sides:
        offs.append((len(ins), len(outs), nsem))
        aliases.update({len(ins) + a: len(outs) + b for a, b in s.aliases.items()})
        ins += s.ins
        outs += s.out_shapes
        nsem += s.nsem

    def hook(name):
        def run(i, o, ss, rs, base):
            for s, (io, oo, so) in zip(sides, offs):
                fn = getattr(s, name)
                if fn is not None:
                    fn(i[io:io + len(s.ins)], o[oo:oo + len(s.out_shapes)], ss, rs, base + so)
        return run

    return _Side(ins, outs, aliases, nsem, hook("start"), hook("mid"), hook("finish"))


def _side_call(body, side, when, *, name, in_specs, out_specs, out_shape, scratch_shapes, args, aliases=None,
               start_in_body=False, **kw):
    n_in, n_out = len(in_specs), len(out_specs)
    aliases = dict(aliases or {})
    if side is None:
        return _call(body, name=name, in_specs=in_specs, out_specs=out_specs, out_shape=out_shape,
                     scratch_shapes=scratch_shapes, input_output_aliases=aliases, **kw)(*args), []
    ns_in, ns_out = len(side.ins), len(side.out_shapes)

    def hook(fn, k, operands):
        if fn is None:
            return
        if when is None:
            fn(*operands, 0)
        elif when[k] is not None:
            pl.when(when[k]())(functools.partial(fn, *operands, 0))

    def wrapped(*refs):
        ins, s_ins = refs[:n_in], refs[n_in:n_in + ns_in]
        o0 = n_in + ns_in
        outs, s_outs = refs[o0:o0 + n_out], refs[o0 + n_out:o0 + n_out + ns_out]
        rest = refs[o0 + n_out + ns_out:]
        scratch, operands = rest[:-2], (s_ins, s_outs, rest[-2], rest[-1])
        if start_in_body:
            body(*ins, *outs, *scratch, side_start=functools.partial(hook, side.start, 0, operands))
        else:
            hook(side.start, 0, operands)
            body(*ins, *outs, *scratch)
        hook(side.mid, 1, operands)
        hook(side.finish, 2, operands)

    res = _call(
        wrapped, name=name,
        in_specs=list(in_specs) + [ANY] * ns_in, out_specs=list(out_specs) + [ANY] * ns_out,
        out_shape=list(out_shape) + side.out_shapes,
        scratch_shapes=list(scratch_shapes) + [pltpu.SemaphoreType.DMA((side.nsem,)),
                                               pltpu.SemaphoreType.DMA((side.nsem,))],
        input_output_aliases={**aliases, **{n_in + a: n_out + b for a, b in side.aliases.items()}},
        **kw)(*args, *side.ins)
    return res[:n_out], res[n_out:]


def _run_side(side, name):
    return _side_call(lambda: None, side, None, name=name, in_specs=[], out_specs=[], out_shape=[],
                      scratch_shapes=[], args=[])[1]


def _remote(src, dst, ss, rs, k, dev):
    return pltpu.make_async_remote_copy(src, dst, ss.at[k], rs.at[k], device_id=dev, device_id_type=MESH)


def _gather_side(bufs):
    n = len(bufs)

    def walk(outs, half):
        x, y, cc = _me()
        for w in range(n):
            hr = outs[w].shape[1] // 2
            rows = pl.ds((cc if half == "mine" else 1 - cc) * hr, hr)
            for j, (dx, dy) in enumerate(_CHIP_OFFSETS):
                yield w, j, (x ^ dx, y ^ dy, cc), outs[w].at[2 * (x ^ dx) + (y ^ dy), rows], outs[w].at[2 * x + y, rows]

    def start(ins, outs, ss, rs, b):
        for w, j, peer, _, own in walk(outs, "mine"):
            _remote(own, own, ss, rs, b + 6 * w + j, peer).start()

    def mid(ins, outs, ss, rs, b):
        x, y, cc = _me()
        for w, j, peer, land, _ in walk(outs, "mine"):
            _remote(land, land, ss, rs, b + 6 * w + j, peer).wait_recv()
            _remote(land, land, ss, rs, b + 6 * w + 3 + j, (x, y, 1 - cc)).start()

    def finish(ins, outs, ss, rs, b):
        x, y, cc = _me()
        for w, j, _, land, _ in walk(outs, "other"):
            _remote(land, land, ss, rs, b + 6 * w + 3 + j, (x, y, 1 - cc)).wait_recv()
        for w, j, peer, land, own in walk(outs, "mine"):
            _remote(own, own, ss, rs, b + 6 * w + j, peer).wait_send()
            _remote(land, land, ss, rs, b + 6 * w + 3 + j, (x, y, 1 - cc)).wait_send()

    return _Side(bufs, [_sds(tuple(w.shape), w.dtype) for w in bufs], {i: i for i in range(n)}, 6 * n,
                 start, mid, finish)


def _copies_side(ins, out_shapes, nsem, copies):
    def start(*a):
        for cp in copies(*a):
            cp.start()

    def finish(*a):
        for cp in copies(*a):
            cp.wait()

    return _Side(ins, out_shapes, {}, nsem, start, None, finish)


def _swap_side(gs):
    def copies(ins, outs, ss, rs, b):
        x, y, cc = _me()
        return [_remote(ins[w].at[:, 1 - cc], outs[w], ss, rs, b + w, (x, y, 1 - cc)) for w in range(len(gs))]

    return _copies_side(gs, [_sds((NQ,) + tuple(g.shape[2:]), F32) for g in gs], len(gs), copies)


def _exchange_side(cs, slots=None, prev=None):
    n = len(cs)
    slots = slots or [(0, 1, 2, 3)] * n

    def among(chip, allowed):
        hit = chip == allowed[0]
        for s in allowed[1:]:
            hit = hit | (chip == s)
        return hit

    def each(ins, outs, ss, rs, b, do_send, do_recv):
        x, y, cc = _me()
        q = 2 * x + y
        for w in range(n):
            for j, (dx, dy) in enumerate(_CHIP_OFFSETS):
                pq = 2 * (x ^ dx) + (y ^ dy)
                cp = _remote(ins[w].at[pq], outs[w].at[q], ss, rs, b + 3 * w + j, (x ^ dx, y ^ dy, cc))
                if do_send is not None:
                    pl.when(among(pq, slots[w]))(functools.partial(do_send, cp))
                if do_recv is not None:
                    pl.when(among(q, slots[w]))(functools.partial(do_recv, cp))

    def start(ins, outs, ss, rs, b):
        each(ins, outs, ss, rs, b, lambda cp: cp.start(), None)

    def finish(ins, outs, ss, rs, b):
        each(ins, outs, ss, rs, b, lambda cp: cp.wait_send(), lambda cp: cp.wait_recv())

    ins = list(cs) + (list(prev) if prev is not None else [])
    aliases = {n + w: w for w in range(n)} if prev is not None else {}
    return _Side(ins, [_sds(tuple(c.shape), c.dtype) for c in cs], aliases, 3 * n, start, None, finish)


def _exchange_copies(srcs, lands, send_sems, recv_sems):
    x, y, cc = _me()
    return [pltpu.make_async_remote_copy(srcs[w].at[2 * (x ^ dx) + (y ^ dy)], lands[w].at[2 * x + y],
                                         send_sems.at[3 * w + j], recv_sems.at[3 * w + j],
                                         device_id=(x ^ dx, y ^ dy, cc), device_id_type=MESH)
            for w in range(len(srcs)) for j, (dx, dy) in enumerate(_CHIP_OFFSETS)]


def _exchange_start(cs):
    n = len(cs)
    hbm, sem = pl.BlockSpec(memory_space=pltpu.HBM), pl.BlockSpec(memory_space=pltpu.SEMAPHORE)
    srcs = [pltpu.with_memory_space_constraint(c, pltpu.HBM) for c in cs]
    lands = [pltpu.with_memory_space_constraint(lax.empty(c.shape, c.dtype), pltpu.HBM) for c in cs]

    def body(*refs):
        for cp in _exchange_copies(refs[:n], refs[n:2 * n], refs[2 * n], refs[2 * n + 1]):
            cp.start()
        refs[-1][...] = jnp.zeros_like(refs[-1])

    res = pl.pallas_call(
        body, name="exchange_start", interpret=False,
        out_shape=(pltpu.SemaphoreType.DMA((3 * n,)), pltpu.SemaphoreType.DMA((3 * n,)),
                   *[pltpu.HBM(c.shape, c.dtype) for c in cs], *[pltpu.HBM(c.shape, c.dtype) for c in cs],
                   _sds((8, LANE), F32)),
        in_specs=(hbm,) * (2 * n), out_specs=(sem, sem) + (hbm,) * (2 * n) + (VMEM,),
        input_output_aliases={i: 2 + i for i in range(2 * n)},
        compiler_params=pltpu.CompilerParams(has_side_effects=pltpu.SideEffectType.DATAFLOW_SIDE_EFFECTING),
    )(*srcs, *lands)
    return res[0], res[1], list(res[2:2 + n]), list(res[2 + n:2 + 2 * n]), res[-1]


def _exchange_wait(send_sems, recv_sems, srcs, lands, after):
    n = len(srcs)
    hbm, sem = pl.BlockSpec(memory_space=pltpu.HBM), pl.BlockSpec(memory_space=pltpu.SEMAPHORE)

    def body(*refs):
        for cp in _exchange_copies(refs[:n], refs[n:2 * n], refs[2 * n], refs[2 * n + 1]):
            cp.wait_send()
            cp.wait_recv()

    res = pl.pallas_call(
        body, name="exchange_wait", interpret=False,
        out_shape=[pltpu.HBM(c.shape, c.dtype) for c in srcs + lands],
        in_specs=(hbm,) * (2 * n) + (sem, sem) + (ANY,) * len(after), out_specs=(hbm,) * (2 * n),
        input_output_aliases={i: i for i in range(2 * n)},
        compiler_params=pltpu.CompilerParams(has_side_effects=pltpu.SideEffectType.DATAFLOW_SIDE_EFFECTING),
    )(*srcs, *lands, send_sems, recv_sems, *after)
    return list(res[:n]), list(res[n:])


def _share_side(fs):
    def copies(ins, outs, ss, rs, b):
        x, y, cc = _me()
        return [_remote(ins[w], outs[w], ss, rs, b + w, (x, y, 1 - cc)) for w in range(len(fs))]

    return _copies_side(fs, [_sds(tuple(f.shape), F32) for f in fs], len(fs), copies)


def _small_allreduce_adamw(g, w, m, v, nd):
    rows = g.shape[0]
    nr = rows - nd
    hr = nr // 2
    assert nd % 8 == 0 and hr % 8 == 0

    def body(g_ref, w_ref, m_ref, v_ref, gs_ref, d_ref, mo_ref, vo_ref, gath, sib, csum, slots, tot, ss, rs):
        x, y, cc = _me()
        me = 4 * x + 2 * y + cc
        q = 2 * x + y
        sibling = (x, y, 1 - cc)
        dm = g_ref.at[pl.ds(0, nd)]
        gath[me] = g_ref[0:nd, :]
        to_all = [_remote(dm, gath.at[me], ss, rs, k, (x ^ dx, y ^ dy, cc ^ dc)) for k, (dx, dy, dc) in enumerate(_OFFSETS7)]
        to_sib = _remote(g_ref.at[pl.ds(nd, nr)], sib, ss, rs, 7, sibling)
        for cp in to_all + [to_sib]:
            cp.start()
        to_sib.wait_recv()
        csum[...] = g_ref[nd:, :] + sib[...]
        mine = pl.ds(pl.multiple_of(cc * hr, 8), hr)
        slots[q] = csum[mine, :]
        to_chips = [_remote(csum.at[mine], slots.at[q], ss, rs, 8 + j, (x ^ dx, y ^ dy, cc))
                    for j, (dx, dy) in enumerate(_CHIP_OFFSETS)]
        for cp in to_chips:
            cp.start()
        for cp in to_chips:
            cp.wait_recv()
        tot[mine, :] = (slots[0] + slots[1]) + (slots[2] + slots[3])
        halves = _remote(tot.at[mine], tot.at[mine], ss, rs, 11, sibling)
        halves.start()
        for cp in to_all:
            cp.wait_recv()
        dsum = gath[0]
        for dev in range(1, N_DEV):
            dsum = dsum + gath[dev]
        halves.wait_recv()
        for lo, n, total in ((0, nd, dsum), (nd, nr, tot[...])):
            gs_ref[lo:lo + n, :] = total
            d, mn, vn = _adamw(w_ref[lo:lo + n, :], total, m_ref[lo:lo + n, :], v_ref[lo:lo + n, :])
            d_ref[lo:lo + n, :] = d
            mo_ref[lo:lo + n, :] = mn
            vo_ref[lo:lo + n, :] = vn
        for cp in to_all + [to_sib, halves] + to_chips:
            cp.wait_send()

    return _call(
        body, name="small_allreduce_adamw",
        in_specs=[VMEM] * 4, out_specs=[VMEM] * 5,
        out_shape=[_sds((rows, LANE), F32)] * 4 + [_sds((N_DEV, nd, LANE), F32)],
        scratch_shapes=[pltpu.VMEM((nr, LANE), F32), pltpu.VMEM((nr, LANE), F32), pltpu.VMEM((NQ, hr, LANE), F32),
                        pltpu.VMEM((nr, LANE), F32), pltpu.SemaphoreType.DMA((12,)), pltpu.SemaphoreType.DMA((12,))],
        compiler_params=_params(None, VMEM_LIMIT),
    )(g, w, m, v)


_SMALL = ["b_ada", "norm_ffn1_g", "norm_mix_g", "pool_scale", "gmlp_ln_g", "gmlp_ln_b", "b_spatial",
          "norm_ffn2_g", "norm_final_g", "w_pool", "w_spatial"]


def _pack(parts):
    blocks, layout, r0 = [], {}, 0
    for name in _SMALL:
        a = parts[name]
        n = a.size
        rows = -(-n // LANE)
        rows8 = -(-rows // 8) * 8
        flat = a.reshape(-1).astype(F32)
        if rows8 * LANE != n:
            flat = jnp.concatenate([flat, jnp.zeros((rows8 * LANE - n,), F32)])
        blocks.append(flat.reshape(rows8, LANE))
        layout[name] = (r0, n, a.shape)
        r0 += rows8
    return jnp.concatenate(blocks, axis=0), layout


def _unpack(packed, layout):
    out = {}
    for name, (r0, n, shape) in layout.items():
        rows = -(-n // LANE)
        out[name] = packed[r0:r0 + rows].reshape(-1)[:n].reshape(shape)
    return out


def _modv(mod9, sub, gain):
    rows = jnp.concatenate([mod9[3 * sub:3 * sub + 3], gain.reshape(1, D), jnp.zeros((4, D), F32)], axis=0)
    return rows


_BIG = ["ffn1_w_in", "ffn1_w_out", "w_mix_in", "w_mix_out", "ffn2_w_in", "ffn2_w_out"]


def kernel(x, c, w_ada, b_ada, norm_ffn1_g, ffn1_w_in, ffn1_w_out, norm_mix_g, w_mix_in, w_pool, pool_scale, gmlp_ln_g, gmlp_ln_b, w_spatial, b_spatial, w_mix_out, norm_ffn2_g, ffn2_w_in, ffn2_w_out, norm_final_g, loss_target, m_w_ada, m_b_ada, m_norm_ffn1_g, m_ffn1_w_in, m_ffn1_w_out, m_norm_mix_g, m_w_mix_in, m_w_pool, m_pool_scale, m_gmlp_ln_g, m_gmlp_ln_b, m_w_spatial, m_b_spatial, m_w_mix_out, m_norm_ffn2_g, m_ffn2_w_in, m_ffn2_w_out, m_norm_final_g, v_w_ada, v_b_ada, v_norm_ffn1_g, v_ffn1_w_in, v_ffn1_w_out, v_norm_mix_g, v_w_mix_in, v_w_pool, v_pool_scale, v_gmlp_ln_g, v_gmlp_ln_b, v_w_spatial, v_b_spatial, v_w_mix_out, v_norm_ffn2_g, v_ffn2_w_in, v_ffn2_w_out, v_norm_final_g):
    names = ["w_ada", "b_ada", "norm_ffn1_g", "ffn1_w_in", "ffn1_w_out", "norm_mix_g", "w_mix_in", "w_pool",
             "pool_scale", "gmlp_ln_g", "gmlp_ln_b", "w_spatial", "b_spatial", "w_mix_out", "norm_ffn2_g",
             "ffn2_w_in", "ffn2_w_out", "norm_final_g"]
    W = dict(zip(names, [w_ada, b_ada, norm_ffn1_g, ffn1_w_in, ffn1_w_out, norm_mix_g, w_mix_in, w_pool, pool_scale,
                         gmlp_ln_g, gmlp_ln_b, w_spatial, b_spatial, w_mix_out, norm_ffn2_g, ffn2_w_in, ffn2_w_out,
                         norm_final_g]))
    M = dict(zip(names, [m_w_ada, m_b_ada, m_norm_ffn1_g, m_ffn1_w_in, m_ffn1_w_out, m_norm_mix_g, m_w_mix_in, m_w_pool,
                         m_pool_scale, m_gmlp_ln_g, m_gmlp_ln_b, m_w_spatial, m_b_spatial, m_w_mix_out, m_norm_ffn2_g,
                         m_ffn2_w_in, m_ffn2_w_out, m_norm_final_g]))
    V = dict(zip(names, [v_w_ada, v_b_ada, v_norm_ffn1_g, v_ffn1_w_in, v_ffn1_w_out, v_norm_mix_g, v_w_mix_in, v_w_pool,
                         v_pool_scale, v_gmlp_ln_g, v_gmlp_ln_b, v_w_spatial, v_b_spatial, v_w_mix_out, v_norm_ffn2_g,
                         v_ffn2_w_in, v_ffn2_w_out, v_norm_final_g]))

    xi, yi, ci = _me()
    q = 2 * xi + yi
    core = ci.astype(jnp.int32).reshape(1)

    chip = q.astype(jnp.int32).reshape(1)
    place = lambda n: _cast_place(W[n][0], chip)

    ncol = w_ada.shape[2]
    b_q = lax.dynamic_slice_in_dim(b_ada, q * ncol, ncol, axis=1)
    (cact_all, modsel), (win1, wout1) = _ada_fwd(
        c, w_ada[0], b_q, side=_gather_side([place("ffn1_w_in"), place("ffn1_w_out")]))
    mod9 = modsel[:NQ].reshape(9, D)
    xs, target = x[0], loss_target[0]
    mv1 = _modv(mod9, 0, norm_ffn1_g[0])
    mv2 = _modv(mod9, 1, norm_mix_g[0])
    mv3 = _modv(mod9, 2, norm_ffn2_g[0])
    wcat, wtcat, bias = _prep_spatial(w_spatial[0], b_spatial[0].T)
    wpool = w_pool[0].astype(BF16)
    vecs = jnp.concatenate([pool_scale, gmlp_ln_g, gmlp_ln_b, jnp.zeros((5, DP), F32)], axis=0)
    gf = jnp.concatenate([norm_final_g.reshape(1, D), jnp.zeros((7, D), F32)], axis=0)

    later =["w_mix_in", "w_mix_out", "ffn2_w_in", "ffn2_w_out"]
    (x1, g1s, u1s), got = _ffn_fwd(xs, mv1, win1, wout1.reshape(2, CH, D), side=_gather_side([place(n) for n in later]))
    wmi, wmo, win2, wout2 = got
    wmi = jnp.transpose(wmi, (1, 0, 2)).reshape(D, DPROJ)
    wmo = wmo.reshape(DP + DG, D)
    x2, pooled, zb = _mix_fwd(x1, mv2, wmi, wpool, vecs, wcat, bias, wmo)
    (dx3, g3s, u3s, loss_blk, dgf), _ = _ffn_fwd(x2, mv3, win2, wout2.reshape(2, CH, D), head=(target, gf))

    wo1, wo2 = wout1.reshape(2, CH, D), wout2.reshape(2, CH, D)
    (dx2, oin2, oout2, rin2, rout2, vec3), _ = _ffn_bwd(x2, dx3, g3s, u3s, mv3, win2, wo2)
    cs2 = [_chip_sum_pair(oin2, rin2), _chip_sum_pair(oout2, rout2)]
    (dx1, dwmi, dwmo, dwpool, dwsp, dbsp, v512, vec2), ex2 = _mix_bwd(
        x1, dx2, pooled, zb, mv2, wmi, wpool, vecs, wcat, wtcat, bias, wmo, side=_exchange_side(cs2))
    half2 = [_sum4(cs, e, chip) for cs, e in zip(cs2, ex2)]
    qcols = w_mix_in.shape[2]
    vmix = [jnp.transpose(dwmi.reshape(D, NQ, qcols), (1, 0, 2)).reshape(NQ, 2, D // 2, qcols),
            dwmo.reshape(NQ, 2, (DP + DG) // 8, D)]
    first1, got = _ffn_bwd_pass(0, xs, dx1, g1s, u1s, mv1, win1, wo1,
                                side=_join(_swap_side(vmix), _share_side(half2)))
    sibmix, other2 = got[:2], got[2:]
    cs_mix = [_chip_sum(g, r, core) for g, r in zip(vmix, sibmix)]
    (grad_x, oin1, oout1, rin1, rout1, vec1), ex_mix = _ffn_bwd_pass(
        1, xs, dx1, g1s, u1s, mv1, win1, wo1, prev=first1[:5], side=_exchange_side(cs_mix))
    vec1 = first1[5] + vec1
    cs_ffn1 = [_chip_sum_pair(oin1, rin1), _chip_sum_pair(oout1, rout1)]
    ssem, rsem, cs_fly, land_fly, token = _exchange_start(cs_ffn1)

    dmod =jnp.concatenate([vec1[0:3], vec2[0:3], vec3[0:3]], axis=0)
    grads = dict(
        b_ada=dmod.reshape(1, 9 * D), norm_ffn1_g=vec1[3:4], norm_mix_g=vec2[3:4], norm_ffn2_g=vec3[3:4],
        pool_scale=v512[0:1], gmlp_ln_g=v512[1:2], gmlp_ln_b=v512[2:3], b_spatial=dbsp[None],
        norm_final_g=dgf[0], w_pool=dwpool[None], w_spatial=dwsp[None])

    gp, layout = _pack({n: grads[n] for n in _SMALL})
    gp = jnp.concatenate([gp, loss_blk + token[0:8, :], loss_blk], axis=0)
    pad = jnp.zeros((16, LANE), F32)
    wp, mp, vp = [jnp.concatenate([_pack({n: src[n] for n in _SMALL})[0], pad], axis=0) for src in (W, M, V)]
    r0, nb, _ = layout["b_ada"]
    assert r0 == 0
    out_g, out_d, out_m, out_v = {}, {}, {}, {}
    gs, dl, mo, vo, gath = _small_allreduce_adamw(gp, wp, mp, vp, nb // LANE)
    loss = gs[-16, 0]
    for packed, dst in ((gs, out_g), (dl, out_d), (mo, out_m), (vo, out_v)):
        for n, a in _unpack(packed, layout).items():
            dst[n] = a.reshape(W[n].shape)

    def update(n, own, recv, after=()):
        (g2, d, mn, vn), _ = _adamw_halves(W[n][0], own, recv, M[n][0], V[n][0], after=after)
        out_g[n], out_d[n], out_m[n], out_v[n] = g2[None], d[None], mn[None], vn[None]
        return g2

    dmod_q = lax.dynamic_slice_in_dim(gath.reshape(N_DEV, nb), q * ncol, ncol, axis=1)
    (ga, da, ma, va), _ = _ada_grad_adamw(cact_all.T, dmod_q, w_ada[0], m_w_ada[0], v_w_ada[0])
    out_g["w_ada"], out_d["w_ada"], out_m["w_ada"], out_v["w_ada"] = ga[None], da[None], ma[None], va[None]
    done = [ga, update("ffn2_w_in", half2[0], other2[0], after=(token,)),
            update("ffn2_w_out", half2[1], other2[1], after=(token,))]
    cs_ffn1, ex_ffn1 = _exchange_wait(ssem, rsem, cs_fly, land_fly, after=done)
    half1 = [_sum4(cs, e, chip) for cs, e in zip(cs_mix + cs_ffn1, list(ex_mix) + list(ex_ffn1))]
    other1 = _run_side(_share_side(half1), "sibling_share")
    for n, own, recv in zip(["w_mix_in", "w_mix_out", "ffn1_w_in", "ffn1_w_out"], half1, other1):
        update(n, own, recv)

    return (loss, grad_x[None], *[out_g[n] for n in names], *[out_d[n] for n in names],
            *[out_m[n] for n in names], *[out_v[n] for n in names])
```

```python
import functools
import math

import jax
import jax.numpy as jnp
from jax import lax
from jax.experimental import pallas as pl
from jax.experimental.pallas import tpu as pltpu

F32 = jnp.float32
BF16 = jnp.bfloat16
MESH = pl.DeviceIdType.MESH
HIGHEST = lax.Precision.HIGHEST

EPS = 1e-6
D = 1024
DFF = 2816
CH = DFF // 2
NQ = 4
DP = 512
DG = 512
DPROJ = DP + 2 * DG
POOL_WINDOWS = (2, 4, 8, 16)
HALO = 16
CHUNK = 128
LANE = 128
N_DEV = 8

ADAM_LR = 0.001
ADAM_B1 = 0.9
ADAM_B2 = 0.999
ADAM_EPS = 1e-08
ADAM_WD = 0.01
ADAM_STEP = 10

VMEM_LIMIT = 62 * 1024 * 1024

TM_FFN_FWD = 512
TM_FFN_BWD = 512
TM_MIX = 256


def _call(body, **kw):
    return pl.pallas_call(body, interpret=False, **kw)


def _params(sem=None, vmem=None):
    return pltpu.CompilerParams(dimension_semantics=sem, vmem_limit_bytes=vmem)


def _sds(shape, dtype):
    return jax.ShapeDtypeStruct(shape, dtype)


ANY = pl.BlockSpec(memory_space=pl.ANY)
VMEM = pl.BlockSpec(memory_space=pltpu.VMEM)
SMEM = pl.BlockSpec(memory_space=pltpu.SMEM)


def _norm_mod(x, gn, sc, sh):
    r = lax.rsqrt(jnp.mean(x * x, axis=-1, keepdims=True) + EPS)
    xn = x * r
    hp = xn * gn
    return r, xn, hp, hp * (1.0 + sc) + sh


def _norm_mod_bwd(dh, r, xn, hp, gn, sc):
    one_sc = 1.0 + sc
    dsh = jnp.sum(dh, axis=0, keepdims=True)
    dsc = jnp.sum(dh * hp, axis=0, keepdims=True)
    dgn = jnp.sum(dh * one_sc * xn, axis=0, keepdims=True)
    dxn = dh * (gn * one_sc)
    dx = r * (dxn - xn * jnp.mean(dxn * xn, axis=-1, keepdims=True))
    return dsh, dsc, dgn, dx


def _dot(a, b):
    return jnp.dot(a, b, preferred_element_type=F32)


def _dot_nt(a, b):
    return lax.dot_general(a, b, (((1,), (1,)), ((), ())), preferred_element_type=F32)


def _dot_tn(a, b):
    return lax.dot_general(a, b, (((0,), (0,)), ((), ())), preferred_element_type=F32)


_GELU_C = math.sqrt(2.0 / math.pi)
_GELU_A = 0.044715


def _gelu_fwd_bwd(x):
    x2 = x * x
    t = jnp.tanh(_GELU_C * (x + _GELU_A * x * x2))
    g = 0.5 * x * (1.0 + t)
    dg = 0.5 * (1.0 + t) + 0.5 * x * (1.0 - t * t) * (_GELU_C * (1.0 + 3.0 * _GELU_A * x2))
    return g, dg


def _adamw(w, g, m, v):
    m = ADAM_B1 * m + (1.0 - ADAM_B1) * g
    v = ADAM_B2 * v + (1.0 - ADAM_B2) * (g * g)
    m_hat = m / (1.0 - ADAM_B1 ** ADAM_STEP)
    v_hat = v / (1.0 - ADAM_B2 ** ADAM_STEP)
    delta = -ADAM_LR * (m_hat / (jnp.sqrt(v_hat) + ADAM_EPS) + ADAM_WD * w)
    return delta, m, v


def _row_block(rows, cap=256, mult=16):
    best = None
    for t in range(mult, min(rows, cap) + 1, mult):
        if rows % t == 0:
            best = t
    assert best is not None, rows
    return best


def _head_math(x, target, gf):
    r = lax.rsqrt(jnp.mean(x * x, axis=-1, keepdims=True) + EPS)
    xn = x * r
    err = xn * gf - target
    dy = err * (1.0 / D)
    dxn = dy * gf
    dx = r * (dxn - xn * jnp.mean(dxn * xn, axis=-1, keepdims=True))
    return (0.5 / D) * jnp.sum(err * err), jnp.sum(dy * xn, axis=0, keepdims=True), dx


def _ffn_fwd(x, modv, win, wout, side=None, head=None):
    S = x.shape[0]
    tm = TM_FFN_FWD
    nt = S // tm

    def body(*refs):
        if head is None:
            x_ref, mod_ref, wg_ref, wu_ref, wo_ref, xo_ref, gs_ref, us_ref, acc_scr = refs
        else:
            (x_ref, mod_ref, wg_ref, wu_ref, wo_ref, t_ref, gf_ref,
             xo_ref, gs_ref, us_ref, loss_ref, dgf_ref, acc_scr) = refs

        @pl.when((pl.program_id(0) == 0) & (pl.program_id(1) == 0))
        def _():
            acc_scr[...] = jnp.zeros_like(acc_scr)
            if head is not None:
                loss_ref[...] = jnp.zeros_like(loss_ref)
                dgf_ref[...] = jnp.zeros_like(dgf_ref)

        j = pl.program_id(1)
        h = _norm_mod(x_ref[...], mod_ref[3:4, :], mod_ref[1:2, :], mod_ref[0:1, :])[3].astype(BF16)
        g = _dot(h, wg_ref[...]).astype(BF16)
        u = _dot(h, wu_ref[...]).astype(BF16)
        gs_ref[...] = g
        us_ref[...] = u
        gf = g.astype(F32)
        a = (gf * jax.nn.sigmoid(gf) * u.astype(F32)).astype(BF16)
        acc = jnp.where(j == 0, 0.0, acc_scr[...]) + _dot(a, wo_ref[...])
        acc_scr[...] = acc
        xo = x_ref[...] + (0.5 * mod_ref[2:3, :]) * acc
        if head is None:
            xo_ref[...] = xo
        else:
            @pl.when(j == 1)
            def _():
                loss, dgf, dx = _head_math(xo, t_ref[...], gf_ref[0:1, :])
                loss_ref[...] += loss
                dgf_ref[0:1, :] += dgf
                xo_ref[...] = dx

    step = lambda i, j: lambda: (pl.program_id(0) == i) & (pl.program_id(1) == j)
    tile = pl.BlockSpec((tm, D), lambda i, j: (i, 0))
    const = lambda shape: pl.BlockSpec(shape, lambda i, j: (0, 0))
    chunk = pl.BlockSpec((tm, CH), lambda i, j: (i, j))
    in_specs = [tile, const((8, D)), pl.BlockSpec((None, D, CH), lambda i, j: (j, 0, 0)),
                pl.BlockSpec((None, D, CH), lambda i, j: (2 + j, 0, 0)), pl.BlockSpec((None, CH, D), lambda i, j: (j, 0, 0))]
    out_specs = [tile, chunk, chunk]
    out_shape = [_sds((S, D), F32), _sds((S, DFF), BF16), _sds((S, DFF), BF16)]
    args = (x, modv, win, win, wout)
    if head is not None:
        in_specs += [tile, const((8, D))]
        out_specs += [const((8, LANE)), const((8, D))]
        out_shape += [_sds((8, LANE), F32), _sds((8, D), F32)]
        args += tuple(head)
    return _side_call(
        body, side, (step(0, 0), step((7 * nt) // 10, 0), step(nt - 1, 1)), name="ffn_fwd",
        grid=(nt, 2), in_specs=in_specs, out_specs=out_specs, out_shape=out_shape,
        scratch_shapes=[pltpu.VMEM((tm, D), F32)],
        compiler_params=_params(("arbitrary", "arbitrary"), VMEM_LIMIT),
        args=args)


def _ffn_bwd_pass(jj, x, dxo, gs, us, modv, win, wout, prev=None, side=None):
    S = x.shape[0]
    tm = TM_FFN_BWD
    nsub = tm // 256
    nt = S // tm
    hi, ho = D // 2, CH // 4
    last = prev is not None
    assert last == (jj == 1)

    def body(*refs):
        x_ref, dxo_ref, gs_ref, us_ref, mod_ref, wg_ref, wu_ref, wo_ref = refs[:8]
        k = 13 if last else 8
        out_ref, dwin_ref, dwout_ref, rwin_ref, rwout_ref, vec_ref = refs[k:k + 6]
        accg, accu, accw, sems, fsend, frecv = refs[k + 6:]
        i = pl.program_id(0)

        @pl.when(i == 0)
        def _():
            accg[...] = jnp.zeros_like(accg)
            accu[...] = jnp.zeros_like(accu)
            accw[...] = jnp.zeros_like(accw)
            vec_ref[...] = jnp.zeros_like(vec_ref)

        gn, sc, sh, gate = mod_ref[3:4, :], mod_ref[1:2, :], mod_ref[0:1, :], mod_ref[2:3, :]

        parts = []
        for s in range(nsub):
            rs = slice(s * (tm // nsub), (s + 1) * (tm // nsub))
            r, xn, hp, h = _norm_mod(x_ref[rs, :], gn, sc, sh)
            dxo = dxo_ref[rs, :]
            dy = (dxo * (0.5 * gate)).astype(BF16)
            g = gs_ref[rs, :].astype(F32)
            u = us_ref[rs, :].astype(F32)
            sig = jax.nn.sigmoid(g)
            sl = g * sig
            a = (sl * u).astype(BF16)
            da = _dot_nt(dy, wo_ref[...])
            dg = (da * u * (sig * (1.0 + g * (1.0 - sig)))).astype(BF16)
            du = (da * sl).astype(BF16)
            dhp = _dot_nt(dg, wg_ref[...]) + _dot_nt(du, wu_ref[...])
            parts.append((h.astype(BF16), a, dg, du, dxo.astype(BF16)))
            if last:
                dsh, dsc, dgn, dxin = _norm_mod_bwd(refs[8][rs, :] + dhp, r, xn, hp, gn, sc)
                vec_ref[0:1, :] += dsh
                vec_ref[1:2, :] += dsc
                vec_ref[3:4, :] += dgn
                out_ref[rs, :] = dxo + dxin
            else:
                out_ref[rs, :] = dhp

        hb, a, dg, du, dxb = [jnp.concatenate(p, axis=0) if nsub > 1 else p[0] for p in zip(*parts)]
        accw[...] += _dot_tn(a, dxb)
        accg[...] += _dot_tn(hb, dg)
        accu[...] += _dot_tn(hb, du)

        @pl.when(i == nt - 1)
        def _():
            gw = accw[...]
            vec_ref[2:3, :] += 0.5 * jnp.sum(wo_ref[...].astype(F32) * gw, axis=0, keepdims=True)
            accw[...] = gw * (0.5 * gate)
            mx, my, cc = _me()
            rows = lambda base, n, c: pl.ds(base + c * n, n)
            pieces = [(accg, 0, hi, dwin_ref, rwin_ref, jj), (accu, 0, hi, dwin_ref, rwin_ref, 2 + jj),
                      (accw, 0, ho, dwout_ref, rwout_ref, 2 * jj), (accw, 2 * ho, ho, dwout_ref, rwout_ref, 2 * jj + 1)]
            loc = [pltpu.make_async_copy(acc.at[rows(base, n, cc)], own.at[slot], sems.at[p])
                   for p, (acc, base, n, own, _, slot) in enumerate(pieces)]
            rem = [pltpu.make_async_remote_copy(acc.at[rows(base, n, 1 - cc)], sib.at[slot], fsend.at[p], frecv.at[p],
                                                device_id=(mx, my, 1 - cc), device_id_type=MESH)
                   for p, (acc, base, n, _, sib, slot) in enumerate(pieces)]
            for cp in loc + rem:
                cp.start()
            for cp in loc:
                cp.wait()
            for cp in rem:
                cp.wait()

    once = pl.Buffered(1)
    tile = pl.BlockSpec((tm, D), lambda i: (i, 0))
    chunk = pl.BlockSpec((tm, CH), lambda i: (i, jj))
    in_specs = [tile, tile, chunk, chunk, pl.BlockSpec((8, D), lambda i: (0, 0)),
                pl.BlockSpec((None, D, CH), lambda i: (jj, 0, 0), pipeline_mode=once),
                pl.BlockSpec((None, D, CH), lambda i: (2 + jj, 0, 0), pipeline_mode=once),
                pl.BlockSpec((None, CH, D), lambda i: (jj, 0, 0), pipeline_mode=once)]
    args = (x, dxo, gs, us, modv, win, win, wout)
    if last:
        in_specs += [tile, ANY, ANY, ANY, ANY]
        args += tuple(prev)
    step = lambda s: lambda: pl.program_id(0) == s
    return _side_call(
        body, side, (step(0), None, step(nt - 1)), name="ffn_bwd",
        grid=(nt,), in_specs=in_specs,
        out_specs=[tile, ANY, ANY, ANY, ANY, pl.BlockSpec((8, D), lambda i: (0, 0))],
        out_shape=[_sds((S, D), F32), _sds((NQ, hi, CH), F32), _sds((NQ, ho, D), F32), _sds((NQ, hi, CH), F32),
                   _sds((NQ, ho, D), F32), _sds((8, D), F32)],
        scratch_shapes=[pltpu.VMEM((D, CH), F32), pltpu.VMEM((D, CH), F32), pltpu.VMEM((CH, D), F32),
                        pltpu.SemaphoreType.DMA((4,)), pltpu.SemaphoreType.DMA((4,)), pltpu.SemaphoreType.DMA((4,))],
        aliases={9 + p: 1 + p for p in range(4)} if last else {},
        compiler_params=_params(("arbitrary",), VMEM_LIMIT),
        args=args)


def _ffn_bwd(x, dxo, gs, us, modv, win, wout, side=None):
    first, extra = _ffn_bwd_pass(0, x, dxo, gs, us, modv, win, wout, side=side)
    (dx, dwin, dwout, rwin, rwout, vec), _ = _ffn_bwd_pass(1, x, dxo, gs, us, modv, win, wout, prev=first[:5])
    return (dx, dwin, dwout, rwin, rwout, first[5] + vec), extra


def _prep_spatial(w_spatial, b_spatial_t):
    def body(w_ref, b_ref, wcat_ref, wtcat_ref, bias_ref):
        row = lax.broadcasted_iota(jnp.int32, (CHUNK, CHUNK), 0)
        col = lax.broadcasted_iota(jnp.int32, (CHUNK, CHUNK), 1)
        tril = col <= row
        for p in range(4):
            wa = jnp.where(tril, w_ref[2 * p], 0.0)
            wb = jnp.where(tril, w_ref[2 * p + 1], 0.0)
            wcat_ref[p] = jnp.concatenate([wa, wb], axis=1).astype(BF16)
            wtcat_ref[p] = jnp.concatenate([wa.T, wb.T], axis=1).astype(BF16)
        head = lax.broadcasted_iota(jnp.int32, (8, DG), 0)
        ch = lax.broadcasted_iota(jnp.int32, (8, DG), 1)
        spread = jnp.where(ch // 64 == head, 1.0, 0.0).astype(F32)
        bias_ref[...] = jnp.dot(b_ref[...], spread, precision=HIGHEST, preferred_element_type=F32)

    return _call(
        body, name="prep_spatial",
        in_specs=[VMEM, VMEM], out_specs=[VMEM, VMEM, VMEM],
        out_shape=[_sds((4, CHUNK, 2 * CHUNK), BF16), _sds((4, CHUNK, 2 * CHUNK), BF16), _sds((CHUNK, DG), F32)],
    )(w_spatial, b_spatial_t)


def _pair_rhs(blocks):
    lane = lax.broadcasted_iota(jnp.int32, (CHUNK, LANE), 1)
    lo = lane < 64
    top = jnp.concatenate([jnp.where(lo, b, 0.0) for b in blocks], axis=1)
    bot = jnp.concatenate([jnp.where(lo, 0.0, b) for b in blocks], axis=1)
    return top, bot


def _gmlp_branch(zb, vecs, wcat_ref, bias_ref, nchunks):
    z, dz = _gelu_fwd_bwd(zb)
    u = z[:, :DG]
    v = z[:, DG:]
    ln_g, ln_b = vecs[1:2, :], vecs[2:3, :]
    mu = jnp.mean(v, axis=-1, keepdims=True)
    vc = v - mu
    rstd = lax.rsqrt(jnp.mean(vc * vc, axis=-1, keepdims=True) + EPS)
    vhat = vc * rstd
    vl = vhat * ln_g + ln_b
    sv_cols = []
    for p in range(4):
        blocks = [vl[k * CHUNK:(k + 1) * CHUNK, p * LANE:(p + 1) * LANE] for k in range(nchunks)]
        top, bot = _pair_rhs(blocks)
        rhs = jnp.concatenate([top, bot], axis=0).astype(BF16)
        out = _dot(wcat_ref[p], rhs)
        bias = bias_ref[:, p * LANE:(p + 1) * LANE]
        sv_cols.append(jnp.concatenate([out[:, k * LANE:(k + 1) * LANE] + bias for k in range(nchunks)], axis=0))
    sv = jnp.concatenate(sv_cols, axis=1)
    return dict(u=u, dz=dz, rstd=rstd, vhat=vhat, vl=vl, sv=sv, yb=u * sv)


def _mix_fwd(x, modv, win, wpool, vecs, wcat, bias, wout):
    S = x.shape[0]
    tm = TM_MIX
    nt = S // tm
    nchunks = tm // CHUNK

    def body(x_ref, mod_ref, win_ref, wpool_ref, vec_ref, wcat_ref, bias_ref, wout_ref,
             xo_ref, pooled_ref, zb_ref, ext):
        i = pl.program_id(0)

        @pl.when(i == 0)
        def _():
            ext[0:HALO, :] = jnp.zeros((HALO, DP), F32)

        x = x_ref[...]
        _, _, _, h = _norm_mod(x, mod_ref[3:4, :], mod_ref[1:2, :], mod_ref[0:1, :])
        proj = _dot(h.astype(BF16), win_ref[...])
        xa = proj[:, :DP]
        zb = proj[:, DP:]
        zb_ref[...] = zb
        ext[HALO:HALO + tm, :] = xa
        pos = i * tm + lax.broadcasted_iota(jnp.int32, (tm, 1), 0)
        vecs = vec_ref[...]
        ya_cols = []
        pooled_cols = []
        for gi, w in enumerate(POOL_WINDOWS):
            cols = slice(gi * LANE, (gi + 1) * LANE)
            s = xa[:, cols]
            for k in range(1, w):
                s = s + ext[HALO - k:HALO - k + tm, cols]
            cnt = jnp.minimum(pos + 1, w).astype(F32)
            pooled = (s / cnt - xa[:, cols]).astype(BF16)
            pooled_cols.append(pooled)
            ya_cols.append(_dot(pooled, wpool_ref[gi]) * vecs[0:1, cols])
        pooled_ref[...] = jnp.concatenate(pooled_cols, axis=1)
        ext[0:HALO, :] = ext[tm:tm + HALO, :]

        gm = _gmlp_branch(zb, vecs, wcat_ref, bias_ref, nchunks)
        cat = jnp.concatenate(ya_cols + [gm["yb"]], axis=1).astype(BF16)
        xo_ref[...] = x + mod_ref[2:3, :] * _dot(cat, wout_ref[...])

    full = lambda shape: pl.BlockSpec(shape, lambda i: (0,) * len(shape))
    return _call(
        body, name="mix_fwd",
        grid=(nt,),
        in_specs=[pl.BlockSpec((tm, D), lambda i: (i, 0)), full((8, D)), full((D, DPROJ)),
                  full((4, LANE, LANE)), full((8, DP)), full((4, CHUNK, 2 * CHUNK)), full((CHUNK, DG)),
                  full((DP + DG, D))],
        out_specs=[pl.BlockSpec((tm, D), lambda i: (i, 0)), pl.BlockSpec((tm, DP), lambda i: (i, 0)),
                   pl.BlockSpec((tm, 2 * DG), lambda i: (i, 0))],
        out_shape=[_sds((S, D), F32), _sds((S, DP), BF16), _sds((S, 2 * DG), F32)],
        scratch_shapes=[pltpu.VMEM((tm + HALO, DP), F32)],
        compiler_params=_params(("arbitrary",), VMEM_LIMIT),
    )(x, modv, win, wpool, vecs, wcat, bias, wout)


def _mix_bwd(x, dxo, pooled, zb, modv, win, wpool, vecs, wcat, wtcat, bias, wout, side=None):
    S = x.shape[0]
    tm = TM_MIX
    nt = S // tm
    nchunks = tm // CHUNK

    def body(x_ref, dxo_ref, pooled_ref, zb_ref, mod_ref, win_ref, wpool_ref, vec_ref, wcat_ref, wtcat_ref,
             bias_ref, wout_ref,
             dx_ref, dwin_ref, dwout_ref, dwpool_ref, dwsp_ref, dbsp_ref, v512_ref, vd_ref, qext, dsv_acc):
        step = pl.program_id(0)
        tile = nt - 1 - step

        @pl.when(step == 0)
        def _():
            dwin_ref[...] = jnp.zeros_like(dwin_ref)
            dwout_ref[...] = jnp.zeros_like(dwout_ref)
            dwpool_ref[...] = jnp.zeros_like(dwpool_ref)
            dwsp_ref[...] = jnp.zeros_like(dwsp_ref)
            v512_ref[...] = jnp.zeros_like(v512_ref)
            vd_ref[...] = jnp.zeros_like(vd_ref)
            dsv_acc[...] = jnp.zeros_like(dsv_acc)
            qext[tm:tm + HALO, :] = jnp.zeros((HALO, DP), F32)

        gn, sc, sh, gate = mod_ref[3:4, :], mod_ref[1:2, :], mod_ref[0:1, :], mod_ref[2:3, :]
        vecs = vec_ref[...]
        x = x_ref[...]
        r, xn, hp, h = _norm_mod(x, gn, sc, sh)
        hb = h.astype(BF16)
        dxo = dxo_ref[...]

        pooled = pooled_ref[...]
        mixed_cols = [_dot(pooled[:, gi * LANE:(gi + 1) * LANE], wpool_ref[gi]) for gi in range(4)]
        mixed = jnp.concatenate(mixed_cols, axis=1)
        scale = vecs[0:1, :]
        gm = _gmlp_branch(zb_ref[...], vecs, wcat_ref, bias_ref, nchunks)
        cat = jnp.concatenate([mixed * scale, gm["yb"]], axis=1).astype(BF16)

        dwout_ref[...] += _dot_tn(cat, dxo.astype(BF16))
        dcat = _dot_nt((dxo * gate).astype(BF16), wout_ref[...])
        dya = dcat[:, :DP]
        dyb = dcat[:, DP:]

        v512_ref[0:1, :] += jnp.sum(dya * mixed, axis=0, keepdims=True)
        dmixed = (dya * scale).astype(BF16)
        pos = tile * tm + lax.broadcasted_iota(jnp.int32, (tm, 1), 0)
        dpooled_cols = []
        for gi, w in enumerate(POOL_WINDOWS):
            cols = slice(gi * LANE, (gi + 1) * LANE)
            dp = _dot_nt(dmixed[:, cols], wpool_ref[gi])
            dwpool_ref[gi] += _dot_tn(pooled[:, cols], dmixed[:, cols])
            cnt = jnp.minimum(pos + 1, w).astype(F32)
            qext[0:tm, cols] = dp / cnt
            dpooled_cols.append(dp)
        dxa_cols = []
        for gi, w in enumerate(POOL_WINDOWS):
            cols = slice(gi * LANE, (gi + 1) * LANE)
            s = qext[0:tm, cols]
            for k in range(1, w):
                s = s + qext[k:k + tm, cols]
            dxa_cols.append(s - dpooled_cols[gi])
        qext[tm:tm + HALO, :] = qext[0:HALO, :]

        u, sv, vl = gm["u"], gm["sv"], gm["vl"]
        du = dyb * sv
        dsv = dyb * u
        dvl_cols = []
        for p in range(4):
            cols = slice(p * LANE, (p + 1) * LANE)
            dblocks = [dsv[k * CHUNK:(k + 1) * CHUNK, cols] for k in range(nchunks)]
            vblocks = [vl[k * CHUNK:(k + 1) * CHUNK, cols] for k in range(nchunks)]
            tot = dblocks[0]
            for b in dblocks[1:]:
                tot = tot + b
            dsv_acc[:, cols] += tot
            top, bot = _pair_rhs(dblocks)
            out = _dot(wtcat_ref[p], jnp.concatenate([top, bot], axis=0).astype(BF16))
            dvl_cols.append(jnp.concatenate([out[:, k * LANE:(k + 1) * LANE] for k in range(nchunks)], axis=0))
            vcat = jnp.concatenate(vblocks, axis=1).astype(BF16)
            dwsp_ref[2 * p] += _dot_nt(top.astype(BF16), vcat)
            dwsp_ref[2 * p + 1] += _dot_nt(bot.astype(BF16), vcat)
        dvl = jnp.concatenate(dvl_cols, axis=1)
        vhat, rstd = gm["vhat"], gm["rstd"]
        v512_ref[1:2, :] += jnp.sum(dvl * vhat, axis=0, keepdims=True)
        v512_ref[2:3, :] += jnp.sum(dvl, axis=0, keepdims=True)
        dvh = dvl * vecs[1:2, :]
        dv = rstd * (dvh - jnp.mean(dvh, axis=-1, keepdims=True)
                     - vhat * jnp.mean(dvh * vhat, axis=-1, keepdims=True))
        dzb = jnp.concatenate([du, dv], axis=1) * gm["dz"]

        dproj = jnp.concatenate(dxa_cols + [dzb], axis=1).astype(BF16)
        dwin_ref[...] += _dot_tn(hb, dproj)
        dh = _dot_nt(dproj, win_ref[...])
        dsh, dsc, dgn, dxin = _norm_mod_bwd(dh, r, xn, hp, gn, sc)
        vd_ref[0:1, :] += dsh
        vd_ref[1:2, :] += dsc
        vd_ref[3:4, :] += dgn
        dx_ref[...] = dxo + dxin

        @pl.when(step == nt - 1)
        def _():
            gw = dwout_ref[...]
            vd_ref[2:3, :] += jnp.sum(wout_ref[...].astype(F32) * gw, axis=0, keepdims=True)
            dwout_ref[...] = gw * gate
            row = lax.broadcasted_iota(jnp.int32, (CHUNK, CHUNK), 0)
            col = lax.broadcasted_iota(jnp.int32, (CHUNK, CHUNK), 1)
            for hh in range(8):
                dwsp_ref[hh] = jnp.where(col <= row, dwsp_ref[hh], 0.0)
            head = lax.broadcasted_iota(jnp.int32, (8, DG), 0)
            ch = lax.broadcasted_iota(jnp.int32, (8, DG), 1)
            spread = jnp.where(ch // 64 == head, 1.0, 0.0).astype(F32)
            dbsp_ref[...] = lax.dot_general(spread, dsv_acc[...], (((1,), (1,)), ((), ())),
                                            precision=HIGHEST, preferred_element_type=F32)

    full = lambda shape: pl.BlockSpec(shape, lambda s: (0,) * len(shape))
    rev = lambda cols: pl.BlockSpec((tm, cols), lambda s: (nt - 1 - s, 0))
    step = lambda s: lambda: pl.program_id(0) == s
    return _side_call(
        body, side, (step(0), None, step(nt - 1)), name="mix_bwd",
        grid=(nt,),
        in_specs=[rev(D), rev(D), rev(DP), rev(2 * DG), full((8, D)), full((D, DPROJ)), full((4, LANE, LANE)),
                  full((8, DP)), full((4, CHUNK, 2 * CHUNK)), full((4, CHUNK, 2 * CHUNK)), full((CHUNK, DG)),
                  full((DP + DG, D))],
        out_specs=[rev(D), full((D, DPROJ)), full((DP + DG, D)), full((4, LANE, LANE)), full((8, CHUNK, CHUNK)),
                   full((8, CHUNK)), full((8, DP)), full((8, D))],
        out_shape=[_sds((S, D), F32), _sds((D, DPROJ), F32), _sds((DP + DG, D), F32), _sds((4, LANE, LANE), F32),
                   _sds((8, CHUNK, CHUNK), F32), _sds((8, CHUNK), F32), _sds((8, DP), F32), _sds((8, D), F32)],
        scratch_shapes=[pltpu.VMEM((tm + HALO, DP), F32), pltpu.VMEM((CHUNK, DG), F32)],
        compiler_params=_params(("arbitrary",), VMEM_LIMIT),
        args=(x, dxo, pooled, zb, modv, win, wpool, vecs, wcat, wtcat, bias, wout))


def _chip_sum(g, rbuf, core):
    _, _, hr, cols = g.shape
    tr = _row_block(hr)

    def body(c_ref, g_ref, r_ref, o_ref):
        o_ref[...] = (g_ref[...] + r_ref[...]).astype(BF16)

    return pl.pallas_call(
        body, name="chip_sum", interpret=False,
        grid_spec=pltpu.PrefetchScalarGridSpec(
            num_scalar_prefetch=1, grid=(NQ, hr // tr),
            in_specs=[pl.BlockSpec((None, None, tr, cols), lambda q, i, c: (q, c[0], i, 0)),
                      pl.BlockSpec((None, tr, cols), lambda q, i, c: (q, i, 0))],
            out_specs=pl.BlockSpec((None, tr, cols), lambda q, i, c: (q, i, 0))),
        out_shape=_sds((NQ, hr, cols), BF16),
        compiler_params=_params(("arbitrary", "arbitrary"), None),
    )(core, g, rbuf)


def _chip_sum_pair(own, rbuf, slots=(0, 1, 2, 3), prev=None):
    _, hr, cols = own.shape
    tr = _row_block(hr)
    a, b = slots[0], (slots[1] - slots[0] if len(slots) > 1 else 0)
    assert list(slots) == [a + b * k for k in range(len(slots))]

    def body(a_ref, b_ref, *rest):
        rest[-1][...] = (a_ref[...] + b_ref[...]).astype(BF16)

    spec = pl.BlockSpec((None, tr, cols), lambda k, i: (a + b * k, i, 0))
    return _call(
        body, name="chip_sum_pair",
        grid=(len(slots), hr // tr),
        in_specs=[spec, spec] + ([ANY] if prev is not None else []), out_specs=spec,
        out_shape=_sds((NQ, hr, cols), BF16),
        input_output_aliases={2: 0} if prev is not None else {},
        compiler_params=_params(("arbitrary", "arbitrary"), None),
    )(own, rbuf, *([prev] if prev is not None else []))


def _sum4(cs, rbuf, chip):
    _, hr, cols = rbuf.shape
    tr = _row_block(hr)

    def body(q_ref, c_ref, r1_ref, r2_ref, r3_ref, o_ref):
        acc = c_ref[...].astype(F32)
        for r in (r1_ref, r2_ref, r3_ref):
            acc = acc + r[...].astype(F32)
        o_ref[...] = acc

    slot = lambda k: pl.BlockSpec((None, tr, cols), lambda i, q: ((q[0] + k) % NQ, i, 0))
    return pl.pallas_call(
        body, name="sum4", interpret=False,
        grid_spec=pltpu.PrefetchScalarGridSpec(
            num_scalar_prefetch=1, grid=(hr // tr,),
            in_specs=[slot(0), slot(1), slot(2), slot(3)],
            out_specs=pl.BlockSpec((tr, cols), lambda i, q: (i, 0))),
        out_shape=_sds((hr, cols), F32),
        compiler_params=_params(("arbitrary",), None),
    )(chip, cs, rbuf, rbuf, rbuf)


def _adamw_halves(w, own, recv, m, v, side=None, after=()):
    rows, cols = w.shape
    hr = rows // 2
    tr = _row_block(hr, mult=8)
    nb = hr // tr

    def body(w_ref, own_ref, recv_ref, m_ref, v_ref, *rest):
        g_ref, d_ref, mo_ref, vo_ref = rest[len(after):]
        g = jnp.where(pl.program_id(0) == lax.axis_index("c"), own_ref[...], recv_ref[...])
        d, mn, vn = _adamw(w_ref[...], g, m_ref[...], v_ref[...])
        g_ref[...] = g
        d_ref[...] = d
        mo_ref[...] = mn
        vo_ref[...] = vn

    full = pl.BlockSpec((tr, cols), lambda h, i: (h * nb + i, 0))
    half = pl.BlockSpec((tr, cols), lambda h, i: (i, 0))
    step = lambda h, i: lambda: (pl.program_id(0) == h) & (pl.program_id(1) == i)
    return _side_call(
        body, side, (step(0, 0), None, step(1, nb - 1)), name="adamw_halves",
        grid=(2, nb), in_specs=[full, half, half, full, full] + [ANY] * len(after), out_specs=[full] * 4,
        out_shape=[_sds((rows, cols), F32)] * 4, scratch_shapes=[],
        compiler_params=_params(("arbitrary", "arbitrary"), None),
        args=(w, own, recv, m, v, *after))


def _cast_place(w, chip):
    rows, cols = w.shape
    tr = _row_block(rows)

    def body(q_ref, w_ref, o_ref):
        o_ref[...] = w_ref[...].astype(BF16)

    return pl.pallas_call(
        body, name="cast_place", interpret=False,
        grid_spec=pltpu.PrefetchScalarGridSpec(
            num_scalar_prefetch=1, grid=(rows // tr,),
            in_specs=[pl.BlockSpec((tr, cols), lambda i, q: (i, 0))],
            out_specs=pl.BlockSpec((None, tr, cols), lambda i, q: (q[0], i, 0))),
        out_shape=_sds((NQ, rows, cols), BF16),
        compiler_params=_params(("arbitrary",), None),
    )(chip, w)


def _ada_grad_adamw(cact_t, dmod_q, w, m, v, side=None):
    rows, cols = w.shape
    tc = 256
    assert cols % tc == 0

    def body(c_ref, d_ref, w_ref, m_ref, v_ref, g_ref, dl_ref, mo_ref, vo_ref):
        g = jnp.dot(c_ref[...], d_ref[...], precision=HIGHEST, preferred_element_type=F32)
        d, mn, vn = _adamw(w_ref[...], g, m_ref[...], v_ref[...])
        g_ref[...] = g
        dl_ref[...] = d
        mo_ref[...] = mn
        vo_ref[...] = vn

    spec = pl.BlockSpec((rows, tc), lambda i: (0, i))
    step = lambda s: lambda: pl.program_id(0) == s
    return _side_call(
        body, side, (step(0), None, step(cols // tc - 1)), name="ada_grad_adamw",
        grid=(cols // tc,),
        in_specs=[pl.BlockSpec((rows, 8), lambda i: (0, 0)), pl.BlockSpec((8, tc), lambda i: (0, i)),
                  spec, spec, spec],
        out_specs=[spec] * 4,
        out_shape=[_sds((rows, cols), F32)] * 4,
        scratch_shapes=[],
        compiler_params=_params(("arbitrary",), None),
        args=(cact_t, dmod_q, w, m, v))


def _me():
    x, y, c = lax.axis_index("x"), lax.axis_index("y"), lax.axis_index("c")
    return x, y, c


_OFFSETS7 = [(dx, dy, dc) for dx in (0, 1) for dy in (0, 1) for dc in (0, 1) if (dx, dy, dc) != (0, 0, 0)]
_CHIP_OFFSETS = [(1, 0), (0, 1), (1, 1)]


def _ada_fwd(c, w_ada_q, b_ada_q, side=None):
    ncol = w_ada_q.shape[1]

    def body(c_ref, w_ref, b_ref, cact_ref, modsel_ref, blk, gath, res, parts, send_sems, recv_sems, side_start=None):
        x, y, cc = _me()
        me = 4 * x + 2 * y + cc
        q = 2 * x + y
        cv = c_ref[...]
        ca = cv * jax.nn.sigmoid(cv)
        row = lax.broadcasted_iota(jnp.int32, (8, D), 0)
        blk[...] = jnp.where(row == me, jnp.broadcast_to(ca, (8, D)), 0.0)
        gath[me] = blk[...]
        sends = []
        for k, (dx, dy, dc) in enumerate(_OFFSETS7):
            cp = pltpu.make_async_remote_copy(blk, gath.at[me], send_sems.at[k], recv_sems.at[k],
                                              device_id=(x ^ dx, y ^ dy, cc ^ dc), device_id_type=MESH)
            cp.start()
            sends.append(cp)
        if side_start is not None:
            side_start()
        for cp in sends:
            cp.wait_recv()
        cact = gath[0]
        for d in range(1, N_DEV):
            cact = cact + gath[d]
        cact_ref[...] = cact
        res[...] = jnp.dot(cact, w_ref[...], precision=HIGHEST, preferred_element_type=F32) + b_ref[...]
        parts[q] = res[...]
        sends2 = []
        for k, (dx, dy) in enumerate(_CHIP_OFFSETS):
            cp = pltpu.make_async_remote_copy(res, parts.at[q], send_sems.at[7 + k], recv_sems.at[7 + k],
                                              device_id=(x ^ dx, y ^ dy, cc), device_id_type=MESH)
            cp.start()
            sends2.append(cp)
        for cp in sends2:
            cp.wait_recv()
        row2 = lax.broadcasted_iota(jnp.int32, (8, ncol), 0)
        out = jnp.zeros((8, ncol), F32)
        for s in range(NQ):
            mine = jnp.sum(jnp.where(row2 == me, parts[s], 0.0), axis=0, keepdims=True)
            out = out + jnp.where(row2 == s, jnp.broadcast_to(mine, (8, ncol)), 0.0)
        modsel_ref[...] = out
        for cp in sends + sends2:
            cp.wait_send()

    return _side_call(
        body, side, None, name="ada_fwd",
        in_specs=[VMEM, VMEM, VMEM], out_specs=[VMEM, VMEM],
        out_shape=[_sds((8, D), F32), _sds((8, ncol), F32)],
        scratch_shapes=[pltpu.VMEM((8, D), F32), pltpu.VMEM((N_DEV, 8, D), F32), pltpu.VMEM((8, ncol), F32),
                        pltpu.VMEM((NQ, 8, ncol), F32), pltpu.SemaphoreType.DMA((10,)), pltpu.SemaphoreType.DMA((10,))],
        compiler_params=_params(None, VMEM_LIMIT), start_in_body=side is not None,
        args=(c, w_ada_q, b_ada_q))


class _Side:
    def __init__(self, ins, out_shapes, aliases, nsem, start, mid=None, finish=None):
        self.ins, self.out_shapes, self.aliases, self.nsem = list(ins), list(out_shapes), dict(aliases), nsem
        self.start, self.mid, self.finish = start, mid, finish


def _join(*sides):
    ins, outs, aliases, offs, nsem = [], [], {}, [], 0
    for s in sides:
        offs.append((len(ins), len(outs), nsem))
        aliases.update({len(ins) + a: len(outs) + b for a, b in s.aliases.items()})
        ins += s.ins
        outs += s.out_shapes
        nsem += s.nsem

    def hook(name):
        def run(i, o, ss, rs, base):
            for s, (io, oo, so) in zip(sides, offs):
                fn = getattr(s, name)
                if fn is not None:
                    fn(i[io:io + len(s.ins)], o[oo:oo + len(s.out_shapes)], ss, rs, base + so)
        return run

    return _Side(ins, outs, aliases, nsem, hook("start"), hook("mid"), hook("finish"))


def _side_call(body, side, when, *, name, in_specs, out_specs, out_shape, scratch_shapes, args, aliases=None,
               start_in_body=False, **kw):
    n_in, n_out = len(in_specs), len(out_specs)
    aliases = dict(aliases or {})
    if side is None:
        return _call(body, name=name, in_specs=in_specs, out_specs=out_specs, out_shape=out_shape,
                     scratch_shapes=scratch_shapes, input_output_aliases=aliases, **kw)(*args), []
    ns_in, ns_out = len(side.ins), len(side.out_shapes)

    def hook(fn, k, operands):
        if fn is None:
            return
        if when is None:
            fn(*operands, 0)
        elif when[k] is not None:
            pl.when(when[k]())(functools.partial(fn, *operands, 0))

    def wrapped(*refs):
        ins, s_ins = refs[:n_in], refs[n_in:n_in + ns_in]
        o0 = n_in + ns_in
        outs, s_outs = refs[o0:o0 + n_out], refs[o0 + n_out:o0 + n_out + ns_out]
        rest = refs[o0 + n_out + ns_out:]
        scratch, operands = rest[:-2], (s_ins, s_outs, rest[-2], rest[-1])
        if start_in_body:
            body(*ins, *outs, *scratch, side_start=functools.partial(hook, side.start, 0, operands))
        else:
            hook(side.start, 0, operands)
            body(*ins, *outs, *scratch)
        hook(side.mid, 1, operands)
        hook(side.finish, 2, operands)

    res = _call(
        wrapped, name=name,
        in_specs=list(in_specs) + [ANY] * ns_in, out_specs=list(out_specs) + [ANY] * ns_out,
        out_shape=list(out_shape) + side.out_shapes,
        scratch_shapes=list(scratch_shapes) + [pltpu.SemaphoreType.DMA((side.nsem,)),
                                               pltpu.SemaphoreType.DMA((side.nsem,))],
        input_output_aliases={**aliases, **{n_in + a: n_out + b for a, b in side.aliases.items()}},
        **kw)(*args, *side.ins)
    return res[:n_out], res[n_out:]


def _run_side(side, name):
    return _side_call(lambda: None, side, None, name=name, in_specs=[], out_specs=[], out_shape=[],
                      scratch_shapes=[], args=[])[1]


def _remote(src, dst, ss, rs, k, dev):
    return pltpu.make_async_remote_copy(src, dst, ss.at[k], rs.at[k], device_id=dev, device_id_type=MESH)


def _gather_side(bufs):
    n = len(bufs)

    def walk(outs, half):
        x, y, cc = _me()
        for w in range(n):
            hr = outs[w].shape[1] // 2
            rows = pl.ds((cc if half == "mine" else 1 - cc) * hr, hr)
            for j, (dx, dy) in enumerate(_CHIP_OFFSETS):
                yield w, j, (x ^ dx, y ^ dy, cc), outs[w].at[2 * (x ^ dx) + (y ^ dy), rows], outs[w].at[2 * x + y, rows]

    def start(ins, outs, ss, rs, b):
        for w, j, peer, _, own in walk(outs, "mine"):
            _remote(own, own, ss, rs, b + 6 * w + j, peer).start()

    def mid(ins, outs, ss, rs, b):
        x, y, cc = _me()
        for w, j, peer, land, _ in walk(outs, "mine"):
            _remote(land, land, ss, rs, b + 6 * w + j, peer).wait_recv()
            _remote(land, land, ss, rs, b + 6 * w + 3 + j, (x, y, 1 - cc)).start()

    def finish(ins, outs, ss, rs, b):
        x, y, cc = _me()
        for w, j, _, land, _ in walk(outs, "other"):
            _remote(land, land, ss, rs, b + 6 * w + 3 + j, (x, y, 1 - cc)).wait_recv()
        for w, j, peer, land, own in walk(outs, "mine"):
            _remote(own, own, ss, rs, b + 6 * w + j, peer).wait_send()
            _remote(land, land, ss, rs, b + 6 * w + 3 + j, (x, y, 1 - cc)).wait_send()

    return _Side(bufs, [_sds(tuple(w.shape), w.dtype) for w in bufs], {i: i for i in range(n)}, 6 * n,
                 start, mid, finish)


def _copies_side(ins, out_shapes, nsem, copies):
    def start(*a):
        for cp in copies(*a):
            cp.start()

    def finish(*a):
        for cp in copies(*a):
            cp.wait()

    return _Side(ins, out_shapes, {}, nsem, start, None, finish)


def _swap_side(gs):
    def copies(ins, outs, ss, rs, b):
        x, y, cc = _me()
        return [_remote(ins[w].at[:, 1 - cc], outs[w], ss, rs, b + w, (x, y, 1 - cc)) for w in range(len(gs))]

    return _copies_side(gs, [_sds((NQ,) + tuple(g.shape[2:]), F32) for g in gs], len(gs), copies)


def _exchange_side(cs, slots=None, prev=None):
    n = len(cs)
    slots = slots or [(0, 1, 2, 3)] * n

    def among(chip, allowed):
        hit = chip == allowed[0]
        for s in allowed[1:]:
            hit = hit | (chip == s)
        return hit

    def each(ins, outs, ss, rs, b, do_send, do_recv):
        x, y, cc = _me()
        q = 2 * x + y
        for w in range(n):
            for j, (dx, dy) in enumerate(_CHIP_OFFSETS):
                pq = 2 * (x ^ dx) + (y ^ dy)
                cp = _remote(ins[w].at[pq], outs[w].at[q], ss, rs, b + 3 * w + j, (x ^ dx, y ^ dy, cc))
                if do_send is not None:
                    pl.when(among(pq, slots[w]))(functools.partial(do_send, cp))
                if do_recv is not None:
                    pl.when(among(q, slots[w]))(functools.partial(do_recv, cp))

    def start(ins, outs, ss, rs, b):
        each(ins, outs, ss, rs, b, lambda cp: cp.start(), None)

    def finish(ins, outs, ss, rs, b):
        each(ins, outs, ss, rs, b, lambda cp: cp.wait_send(), lambda cp: cp.wait_recv())

    ins = list(cs) + (list(prev) if prev is not None else [])
    aliases = {n + w: w for w in range(n)} if prev is not None else {}
    return _Side(ins, [_sds(tuple(c.shape), c.dtype) for c in cs], aliases, 3 * n, start, None, finish)


def _exchange_copies(srcs, lands, send_sems, recv_sems):
    x, y, cc = _me()
    return [pltpu.make_async_remote_copy(srcs[w].at[2 * (x ^ dx) + (y ^ dy)], lands[w].at[2 * x + y],
                                         send_sems.at[3 * w + j], recv_sems.at[3 * w + j],
                                         device_id=(x ^ dx, y ^ dy, cc), device_id_type=MESH)
            for w in range(len(srcs)) for j, (dx, dy) in enumerate(_CHIP_OFFSETS)]


def _exchange_start(cs, after=()):
    n = len(cs)
    hbm, sem = pl.BlockSpec(memory_space=pltpu.HBM), pl.BlockSpec(memory_space=pltpu.SEMAPHORE)
    srcs = [pltpu.with_memory_space_constraint(c, pltpu.HBM) for c in cs]
    lands = [pltpu.with_memory_space_constraint(lax.empty(c.shape, c.dtype), pltpu.HBM) for c in cs]

    def body(*refs):
        sems = 2 * n + len(after)
        for cp in _exchange_copies(refs[:n], refs[n:2 * n], refs[sems], refs[sems + 1]):
            cp.start()
        refs[-1][...] = jnp.zeros_like(refs[-1])

    res = pl.pallas_call(
        body, name="exchange_start", interpret=False,
        out_shape=(pltpu.SemaphoreType.DMA((3 * n,)), pltpu.SemaphoreType.DMA((3 * n,)),
                   *[pltpu.HBM(c.shape, c.dtype) for c in cs], *[pltpu.HBM(c.shape, c.dtype) for c in cs],
                   _sds((8, LANE), F32)),
        in_specs=(hbm,) * (2 * n) + (ANY,) * len(after), out_specs=(sem, sem) + (hbm,) * (2 * n) + (VMEM,),
        input_output_aliases={i: 2 + i for i in range(2 * n)},
        compiler_params=pltpu.CompilerParams(has_side_effects=pltpu.SideEffectType.DATAFLOW_SIDE_EFFECTING),
    )(*srcs, *lands, *after)
    return res[0], res[1], list(res[2:2 + n]), list(res[2 + n:2 + 2 * n]), res[-1]


def _exchange_wait(send_sems, recv_sems, srcs, lands, after):
    n = len(srcs)
    hbm, sem = pl.BlockSpec(memory_space=pltpu.HBM), pl.BlockSpec(memory_space=pltpu.SEMAPHORE)

    def body(*refs):
        for cp in _exchange_copies(refs[:n], refs[n:2 * n], refs[2 * n], refs[2 * n + 1]):
            cp.wait_send()
            cp.wait_recv()

    res = pl.pallas_call(
        body, name="exchange_wait", interpret=False,
        out_shape=[pltpu.HBM(c.shape, c.dtype) for c in srcs + lands],
        in_specs=(hbm,) * (2 * n) + (sem, sem) + (ANY,) * len(after), out_specs=(hbm,) * (2 * n),
        input_output_aliases={i: i for i in range(2 * n)},
        compiler_params=pltpu.CompilerParams(has_side_effects=pltpu.SideEffectType.DATAFLOW_SIDE_EFFECTING),
    )(*srcs, *lands, send_sems, recv_sems, *after)
    return list(res[:n]), list(res[n:])


def _share_side(fs):
    def copies(ins, outs, ss, rs, b):
        x, y, cc = _me()
        return [_remote(ins[w], outs[w], ss, rs, b + w, (x, y, 1 - cc)) for w in range(len(fs))]

    return _copies_side(fs, [_sds(tuple(f.shape), F32) for f in fs], len(fs), copies)


def _small_allreduce_adamw(g, w, m, v, nd):
    rows = g.shape[0]
    nr = rows - nd
    hr = nr // 2
    assert nd % 8 == 0 and hr % 8 == 0

    def body(g_ref, w_ref, m_ref, v_ref, gs_ref, d_ref, mo_ref, vo_ref, gath, sib, csum, slots, tot, ss, rs):
        x, y, cc = _me()
        me = 4 * x + 2 * y + cc
        q = 2 * x + y
        sibling = (x, y, 1 - cc)
        dm = g_ref.at[pl.ds(0, nd)]
        gath[me] = g_ref[0:nd, :]
        to_all = [_remote(dm, gath.at[me], ss, rs, k, (x ^ dx, y ^ dy, cc ^ dc)) for k, (dx, dy, dc) in enumerate(_OFFSETS7)]
        to_sib = _remote(g_ref.at[pl.ds(nd, nr)], sib, ss, rs, 7, sibling)
        for cp in to_all + [to_sib]:
            cp.start()
        to_sib.wait_recv()
        csum[...] = g_ref[nd:, :] + sib[...]
        mine = pl.ds(pl.multiple_of(cc * hr, 8), hr)
        slots[q] = csum[mine, :]
        to_chips = [_remote(csum.at[mine], slots.at[q], ss, rs, 8 + j, (x ^ dx, y ^ dy, cc))
                    for j, (dx, dy) in enumerate(_CHIP_OFFSETS)]
        for cp in to_chips:
            cp.start()
        for cp in to_chips:
            cp.wait_recv()
        tot[mine, :] = (slots[0] + slots[1]) + (slots[2] + slots[3])
        halves = _remote(tot.at[mine], tot.at[mine], ss, rs, 11, sibling)
        halves.start()
        for cp in to_all:
            cp.wait_recv()
        dsum = gath[0]
        for dev in range(1, N_DEV):
            dsum = dsum + gath[dev]
        halves.wait_recv()
        for lo, n, total in ((0, nd, dsum), (nd, nr, tot[...])):
            gs_ref[lo:lo + n, :] = total
            d, mn, vn = _adamw(w_ref[lo:lo + n, :], total, m_ref[lo:lo + n, :], v_ref[lo:lo + n, :])
            d_ref[lo:lo + n, :] = d
            mo_ref[lo:lo + n, :] = mn
            vo_ref[lo:lo + n, :] = vn
        for cp in to_all + [to_sib, halves] + to_chips:
            cp.wait_send()

    return _call(
        body, name="small_allreduce_adamw",
        in_specs=[VMEM] * 4, out_specs=[VMEM] * 5,
        out_shape=[_sds((rows, LANE), F32)] * 4 + [_sds((N_DEV, nd, LANE), F32)],
        scratch_shapes=[pltpu.VMEM((nr, LANE), F32), pltpu.VMEM((nr, LANE), F32), pltpu.VMEM((NQ, hr, LANE), F32),
                        pltpu.VMEM((nr, LANE), F32), pltpu.SemaphoreType.DMA((12,)), pltpu.SemaphoreType.DMA((12,))],
        compiler_params=_params(None, VMEM_LIMIT),
    )(g, w, m, v)


_SMALL = ["b_ada", "norm_ffn1_g", "norm_mix_g", "pool_scale", "gmlp_ln_g", "gmlp_ln_b", "b_spatial",
          "norm_ffn2_g", "norm_final_g", "w_pool", "w_spatial"]


def _pack(parts):
    blocks, layout, r0 = [], {}, 0
    for name in _SMALL:
        a = parts[name]
        n = a.size
        rows = -(-n // LANE)
        rows8 = -(-rows // 8) * 8
        flat = a.reshape(-1).astype(F32)
        if rows8 * LANE != n:
            flat = jnp.concatenate([flat, jnp.zeros((rows8 * LANE - n,), F32)])
        blocks.append(flat.reshape(rows8, LANE))
        layout[name] = (r0, n, a.shape)
        r0 += rows8
    return jnp.concatenate(blocks, axis=0), layout


def _unpack(packed, layout):
    out = {}
    for name, (r0, n, shape) in layout.items():
        rows = -(-n // LANE)
        out[name] = packed[r0:r0 + rows].reshape(-1)[:n].reshape(shape)
    return out


def _modv(mod9, sub, gain):
    rows = jnp.concatenate([mod9[3 * sub:3 * sub + 3], gain.reshape(1, D), jnp.zeros((4, D), F32)], axis=0)
    return rows


_BIG = ["ffn1_w_in", "ffn1_w_out", "w_mix_in", "w_mix_out", "ffn2_w_in", "ffn2_w_out"]


def kernel(x, c, w_ada, b_ada, norm_ffn1_g, ffn1_w_in, ffn1_w_out, norm_mix_g, w_mix_in, w_pool, pool_scale, gmlp_ln_g, gmlp_ln_b, w_spatial, b_spatial, w_mix_out, norm_ffn2_g, ffn2_w_in, ffn2_w_out, norm_final_g, loss_target, m_w_ada, m_b_ada, m_norm_ffn1_g, m_ffn1_w_in, m_ffn1_w_out, m_norm_mix_g, m_w_mix_in, m_w_pool, m_pool_scale, m_gmlp_ln_g, m_gmlp_ln_b, m_w_spatial, m_b_spatial, m_w_mix_out, m_norm_ffn2_g, m_ffn2_w_in, m_ffn2_w_out, m_norm_final_g, v_w_ada, v_b_ada, v_norm_ffn1_g, v_ffn1_w_in, v_ffn1_w_out, v_norm_mix_g, v_w_mix_in, v_w_pool, v_pool_scale, v_gmlp_ln_g, v_gmlp_ln_b, v_w_spatial, v_b_spatial, v_w_mix_out, v_norm_ffn2_g, v_ffn2_w_in, v_ffn2_w_out, v_norm_final_g):
    names = ["w_ada", "b_ada", "norm_ffn1_g", "ffn1_w_in", "ffn1_w_out", "norm_mix_g", "w_mix_in", "w_pool",
             "pool_scale", "gmlp_ln_g", "gmlp_ln_b", "w_spatial", "b_spatial", "w_mix_out", "norm_ffn2_g",
             "ffn2_w_in", "ffn2_w_out", "norm_final_g"]
    W = dict(zip(names, [w_ada, b_ada, norm_ffn1_g, ffn1_w_in, ffn1_w_out, norm_mix_g, w_mix_in, w_pool, pool_scale,
                         gmlp_ln_g, gmlp_ln_b, w_spatial, b_spatial, w_mix_out, norm_ffn2_g, ffn2_w_in, ffn2_w_out,
                         norm_final_g]))
    M = dict(zip(names, [m_w_ada, m_b_ada, m_norm_ffn1_g, m_ffn1_w_in, m_ffn1_w_out, m_norm_mix_g, m_w_mix_in, m_w_pool,
                         m_pool_scale, m_gmlp_ln_g, m_gmlp_ln_b, m_w_spatial, m_b_spatial, m_w_mix_out, m_norm_ffn2_g,
                         m_ffn2_w_in, m_ffn2_w_out, m_norm_final_g]))
    V = dict(zip(names, [v_w_ada, v_b_ada, v_norm_ffn1_g, v_ffn1_w_in, v_ffn1_w_out, v_norm_mix_g, v_w_mix_in, v_w_pool,
                         v_pool_scale, v_gmlp_ln_g, v_gmlp_ln_b, v_w_spatial, v_b_spatial, v_w_mix_out, v_norm_ffn2_g,
                         v_ffn2_w_in, v_ffn2_w_out, v_norm_final_g]))

    xi, yi, ci = _me()
    q = 2 * xi + yi
    core = ci.astype(jnp.int32).reshape(1)

    chip = q.astype(jnp.int32).reshape(1)
    place = lambda n: _cast_place(W[n][0], chip)

    ncol = w_ada.shape[2]
    b_q = lax.dynamic_slice_in_dim(b_ada, q * ncol, ncol, axis=1)
    (cact_all, modsel), (win1, wout1) = _ada_fwd(
        c, w_ada[0], b_q, side=_gather_side([place("ffn1_w_in"), place("ffn1_w_out")]))
    mod9 = modsel[:NQ].reshape(9, D)
    xs, target = x[0], loss_target[0]
    mv1 = _modv(mod9, 0, norm_ffn1_g[0])
    mv2 = _modv(mod9, 1, norm_mix_g[0])
    mv3 = _modv(mod9, 2, norm_ffn2_g[0])
    wcat, wtcat, bias = _prep_spatial(w_spatial[0], b_spatial[0].T)
    wpool = w_pool[0].astype(BF16)
    vecs = jnp.concatenate([pool_scale, gmlp_ln_g, gmlp_ln_b, jnp.zeros((5, DP), F32)], axis=0)
    gf = jnp.concatenate([norm_final_g.reshape(1, D), jnp.zeros((7, D), F32)], axis=0)

    later =["w_mix_in", "w_mix_out", "ffn2_w_in", "ffn2_w_out"]
    (x1, g1s, u1s), got = _ffn_fwd(xs, mv1, win1, wout1.reshape(2, CH, D), side=_gather_side([place(n) for n in later]))
    wmi, wmo, win2, wout2 = got
    wmi = jnp.transpose(wmi, (1, 0, 2)).reshape(D, DPROJ)
    wmo = wmo.reshape(DP + DG, D)
    x2, pooled, zb = _mix_fwd(x1, mv2, wmi, wpool, vecs, wcat, bias, wmo)
    (dx3, g3s, u3s, loss_blk, dgf), _ = _ffn_fwd(x2, mv3, win2, wout2.reshape(2, CH, D), head=(target, gf))

    wo1, wo2 = wout1.reshape(2, CH, D), wout2.reshape(2, CH, D)
    (dx2, oin2, oout2, rin2, rout2, vec3), _ = _ffn_bwd(x2, dx3, g3s, u3s, mv3, win2, wo2)
    cs2 = [_chip_sum_pair(oin2, rin2), _chip_sum_pair(oout2, rout2)]
    (dx1, dwmi, dwmo, dwpool, dwsp, dbsp, v512, vec2), ex2 = _mix_bwd(
        x1, dx2, pooled, zb, mv2, wmi, wpool, vecs, wcat, wtcat, bias, wmo, side=_exchange_side(cs2))
    half2 = [_sum4(cs, e, chip) for cs, e in zip(cs2, ex2)]
    qcols = w_mix_in.shape[2]
    vmix = [jnp.transpose(dwmi.reshape(D, NQ, qcols), (1, 0, 2)).reshape(NQ, 2, D // 2, qcols),
            dwmo.reshape(NQ, 2, (DP + DG) // 8, D)]
    first1, got = _ffn_bwd_pass(0, xs, dx1, g1s, u1s, mv1, win1, wo1,
                                side=_join(_swap_side(vmix), _share_side(half2)))
    sibmix, other2 = got[:2], got[2:]
    cs_mix = [_chip_sum(g, r, core) for g, r in zip(vmix, sibmix)]
    (grad_x, oin1, oout1, rin1, rout1, vec1), ex_mix = _ffn_bwd_pass(
        1, xs, dx1, g1s, u1s, mv1, win1, wo1, prev=first1[:5], side=_exchange_side(cs_mix))
    vec1 = first1[5] + vec1
    cs_ffn1 = [_chip_sum_pair(oin1, rin1), _chip_sum_pair(oout1, rout1)]

    dmod =jnp.concatenate([vec1[0:3], vec2[0:3], vec3[0:3]], axis=0)
    grads = dict(
        b_ada=dmod.reshape(1, 9 * D), norm_ffn1_g=vec1[3:4], norm_mix_g=vec2[3:4], norm_ffn2_g=vec3[3:4],
        pool_scale=v512[0:1], gmlp_ln_g=v512[1:2], gmlp_ln_b=v512[2:3], b_spatial=dbsp[None],
        norm_final_g=dgf[0], w_pool=dwpool[None], w_spatial=dwsp[None])

    gp, layout = _pack({n: grads[n] for n in _SMALL})
    gp = jnp.concatenate([gp, loss_blk, loss_blk], axis=0)
    pad = jnp.zeros((16, LANE), F32)
    wp, mp, vp = [jnp.concatenate([_pack({n: src[n] for n in _SMALL})[0], pad], axis=0) for src in (W, M, V)]
    r0, nb, _ = layout["b_ada"]
    assert r0 == 0
    out_g, out_d, out_m, out_v = {}, {}, {}, {}
    gs, dl, mo, vo, gath = _small_allreduce_adamw(gp, wp, mp, vp, nb // LANE)
    loss = gs[-16, 0]
    for packed, dst in ((gs, out_g), (dl, out_d), (mo, out_m), (vo, out_v)):
        for n, a in _unpack(packed, layout).items():
            dst[n] = a.reshape(W[n].shape)

    def update(n, own, recv, after=()):
        (g2, d, mn, vn), _ = _adamw_halves(W[n][0], own, recv, M[n][0], V[n][0], after=after)
        out_g[n], out_d[n], out_m[n], out_v[n] = g2[None], d[None], mn[None], vn[None]
        return g2

    ssem, rsem, cs_fly, land_fly, token = _exchange_start(cs_ffn1, after=(gath,))
    dmod_q = lax.dynamic_slice_in_dim(gath.reshape(N_DEV, nb), q * ncol, ncol, axis=1) + token[0:8, 0:1]
    (ga, da, ma, va), _ = _ada_grad_adamw(cact_all.T, dmod_q, w_ada[0], m_w_ada[0], v_w_ada[0])
    out_g["w_ada"], out_d["w_ada"], out_m["w_ada"], out_v["w_ada"] = ga[None], da[None], ma[None], va[None]
    done = [ga, update("ffn2_w_in", half2[0], other2[0], after=(token,)),
            update("ffn2_w_out", half2[1], other2[1], after=(token,))]
    cs_ffn1, ex_ffn1 = _exchange_wait(ssem, rsem, cs_fly, land_fly, after=done)
    half1 = [_sum4(cs, e, chip) for cs, e in zip(cs_mix + cs_ffn1, list(ex_mix) + list(ex_ffn1))]
    other1 = _run_side(_share_side(half1), "sibling_share")
    for n, own, recv in zip(["w_mix_in", "w_mix_out", "ffn1_w_in", "ffn1_w_out"], half1, other1):
        update(n, own, recv)

    return (loss, grad_x[None], *[out_g[n] for n in names], *[out_d[n] for n in names],
            *[out_m[n] for n in names], *[out_v[n] for n in names])
```

```python
import functools
import math

import jax
import jax.numpy as jnp
from jax import lax
from jax.experimental import pallas as pl
from jax.experimental.pallas import tpu as pltpu

F32 = jnp.float32
BF16 = jnp.bfloat16
MESH = pl.DeviceIdType.MESH
HIGHEST = lax.Precision.HIGHEST

EPS = 1e-6
D = 1024
DFF = 2816
CH = DFF // 2
NQ = 4
DP = 512
DG = 512
DPROJ = DP + 2 * DG
POOL_WINDOWS = (2, 4, 8, 16)
HALO = 16
CHUNK = 128
LANE = 128
N_DEV = 8

ADAM_LR = 0.001
ADAM_B1 = 0.9
ADAM_B2 = 0.999
ADAM_EPS = 1e-08
ADAM_WD = 0.01
ADAM_STEP = 10

VMEM_LIMIT = 62 * 1024 * 1024

TM_FFN_FWD = 512
TM_FFN_BWD = 512
TM_MIX = 256


def _call(body, **kw):
    return pl.pallas_call(body, interpret=False, **kw)


def _params(sem=None, vmem=None):
    return pltpu.CompilerParams(dimension_semantics=sem, vmem_limit_bytes=vmem)


def _sds(shape, dtype):
    return jax.ShapeDtypeStruct(shape, dtype)


ANY = pl.BlockSpec(memory_space=pl.ANY)
VMEM = pl.BlockSpec(memory_space=pltpu.VMEM)
SMEM = pl.BlockSpec(memory_space=pltpu.SMEM)


def _norm_mod(x, gn, sc, sh):
    r = lax.rsqrt(jnp.mean(x * x, axis=-1, keepdims=True) + EPS)
    xn = x * r
    hp = xn * gn
    return r, xn, hp, hp * (1.0 + sc) + sh


def _norm_mod_bwd(dh, r, xn, hp, gn, sc):
    one_sc = 1.0 + sc
    dsh = jnp.sum(dh, axis=0, keepdims=True)
    dsc = jnp.sum(dh * hp, axis=0, keepdims=True)
    dgn = jnp.sum(dh * one_sc * xn, axis=0, keepdims=True)
    dxn = dh * (gn * one_sc)
    dx = r * (dxn - xn * jnp.mean(dxn * xn, axis=-1, keepdims=True))
    return dsh, dsc, dgn, dx


def _dot(a, b):
    return jnp.dot(a, b, preferred_element_type=F32)


def _dot_nt(a, b):
    return lax.dot_general(a, b, (((1,), (1,)), ((), ())), preferred_element_type=F32)


def _dot_tn(a, b):
    return lax.dot_general(a, b, (((0,), (0,)), ((), ())), preferred_element_type=F32)


_GELU_C = math.sqrt(2.0 / math.pi)
_GELU_A = 0.044715


def _gelu_fwd_bwd(x):
    x2 = x * x
    t = jnp.tanh(_GELU_C * (x + _GELU_A * x * x2))
    g = 0.5 * x * (1.0 + t)
    dg = 0.5 * (1.0 + t) + 0.5 * x * (1.0 - t * t) * (_GELU_C * (1.0 + 3.0 * _GELU_A * x2))
    return g, dg


def _adamw(w, g, m, v):
    m = ADAM_B1 * m + (1.0 - ADAM_B1) * g
    v = ADAM_B2 * v + (1.0 - ADAM_B2) * (g * g)
    m_hat = m / (1.0 - ADAM_B1 ** ADAM_STEP)
    v_hat = v / (1.0 - ADAM_B2 ** ADAM_STEP)
    delta = -ADAM_LR * (m_hat / (jnp.sqrt(v_hat) + ADAM_EPS) + ADAM_WD * w)
    return delta, m, v


def _row_block(rows, cap=256, mult=16):
    best = None
    for t in range(mult, min(rows, cap) + 1, mult):
        if rows % t == 0:
            best = t
    assert best is not None, rows
    return best


def _head_math(x, target, gf):
    r = lax.rsqrt(jnp.mean(x * x, axis=-1, keepdims=True) + EPS)
    xn = x * r
    err = xn * gf - target
    dy = err * (1.0 / D)
    dxn = dy * gf
    dx = r * (dxn - xn * jnp.mean(dxn * xn, axis=-1, keepdims=True))
    return (0.5 / D) * jnp.sum(err * err), jnp.sum(dy * xn, axis=0, keepdims=True), dx


def _ffn_fwd(x, modv, win, wout, side=None, head=None):
    S = x.shape[0]
    tm = TM_FFN_FWD
    nt = S // tm

    def body(*refs):
        if head is None:
            x_ref, mod_ref, wg_ref, wu_ref, wo_ref, xo_ref, gs_ref, us_ref, acc_scr = refs
        else:
            (x_ref, mod_ref, wg_ref, wu_ref, wo_ref, t_ref, gf_ref,
             xo_ref, gs_ref, us_ref, loss_ref, dgf_ref, acc_scr) = refs

        @pl.when((pl.program_id(0) == 0) & (pl.program_id(1) == 0))
        def _():
            acc_scr[...] = jnp.zeros_like(acc_scr)
            if head is not None:
                loss_ref[...] = jnp.zeros_like(loss_ref)
                dgf_ref[...] = jnp.zeros_like(dgf_ref)

        j = pl.program_id(1)
        h = _norm_mod(x_ref[...], mod_ref[3:4, :], mod_ref[1:2, :], mod_ref[0:1, :])[3].astype(BF16)
        g = _dot(h, wg_ref[...]).astype(BF16)
        u = _dot(h, wu_ref[...]).astype(BF16)
        gs_ref[...] = g
        us_ref[...] = u
        gf = g.astype(F32)
        a = (gf * jax.nn.sigmoid(gf) * u.astype(F32)).astype(BF16)
        acc = jnp.where(j == 0, 0.0, acc_scr[...]) + _dot(a, wo_ref[...])
        acc_scr[...] = acc
        xo = x_ref[...] + (0.5 * mod_ref[2:3, :]) * acc
        if head is None:
            xo_ref[...] = xo
        else:
            @pl.when(j == 1)
            def _():
                loss, dgf, dx = _head_math(xo, t_ref[...], gf_ref[0:1, :])
                loss_ref[...] += loss
                dgf_ref[0:1, :] += dgf
                xo_ref[...] = dx

    step = lambda i, j: lambda: (pl.program_id(0) == i) & (pl.program_id(1) == j)
    tile = pl.BlockSpec((tm, D), lambda i, j: (i, 0))
    const = lambda shape: pl.BlockSpec(shape, lambda i, j: (0, 0))
    chunk = pl.BlockSpec((tm, CH), lambda i, j: (i, j))
    in_specs = [tile, const((8, D)), pl.BlockSpec((None, D, CH), lambda i, j: (j, 0, 0)),
                pl.BlockSpec((None, D, CH), lambda i, j: (2 + j, 0, 0)), pl.BlockSpec((None, CH, D), lambda i, j: (j, 0, 0))]
    out_specs = [tile, chunk, chunk]
    out_shape = [_sds((S, D), F32), _sds((S, DFF), BF16), _sds((S, DFF), BF16)]
    args = (x, modv, win, win, wout)
    if head is not None:
        in_specs += [tile, const((8, D))]
        out_specs += [const((8, LANE)), const((8, D))]
        out_shape += [_sds((8, LANE), F32), _sds((8, D), F32)]
        args += tuple(head)
    return _side_call(
        body, side, (step(0, 0), step((7 * nt) // 10, 0), step(nt - 1, 1)), name="ffn_fwd",
        grid=(nt, 2), in_specs=in_specs, out_specs=out_specs, out_shape=out_shape,
        scratch_shapes=[pltpu.VMEM((tm, D), F32)],
        compiler_params=_params(("arbitrary", "arbitrary"), VMEM_LIMIT),
        args=args)


def _ffn_bwd_pass(jj, x, dxo, gs, us, modv, win, wout, prev=None, side=None):
    S = x.shape[0]
    tm = TM_FFN_BWD
    nsub = tm // 256
    nt = S // tm
    hi, ho = D // 2, CH // 4
    last = prev is not None
    assert last == (jj == 1)

    def body(*refs):
        x_ref, dxo_ref, gs_ref, us_ref, mod_ref, wg_ref, wu_ref, wo_ref = refs[:8]
        k = 13 if last else 8
        out_ref, dwin_ref, dwout_ref, rwin_ref, rwout_ref, vec_ref = refs[k:k + 6]
        accg, accu, accw, sems, fsend, frecv = refs[k + 6:]
        i = pl.program_id(0)

        @pl.when(i == 0)
        def _():
            accg[...] = jnp.zeros_like(accg)
            accu[...] = jnp.zeros_like(accu)
            accw[...] = jnp.zeros_like(accw)
            vec_ref[...] = jnp.zeros_like(vec_ref)

        gn, sc, sh, gate = mod_ref[3:4, :], mod_ref[1:2, :], mod_ref[0:1, :], mod_ref[2:3, :]

        parts = []
        for s in range(nsub):
            rs = slice(s * (tm // nsub), (s + 1) * (tm // nsub))
            r, xn, hp, h = _norm_mod(x_ref[rs, :], gn, sc, sh)
            dxo = dxo_ref[rs, :]
            dy = (dxo * (0.5 * gate)).astype(BF16)
            g = gs_ref[rs, :].astype(F32)
            u = us_ref[rs, :].astype(F32)
            sig = jax.nn.sigmoid(g)
            sl = g * sig
            a = (sl * u).astype(BF16)
            da = _dot_nt(dy, wo_ref[...])
            dg = (da * u * (sig * (1.0 + g * (1.0 - sig)))).astype(BF16)
            du = (da * sl).astype(BF16)
            dhp = _dot_nt(dg, wg_ref[...]) + _dot_nt(du, wu_ref[...])
            parts.append((h.astype(BF16), a, dg, du, dxo.astype(BF16)))
            if last:
                dsh, dsc, dgn, dxin = _norm_mod_bwd(refs[8][rs, :] + dhp, r, xn, hp, gn, sc)
                vec_ref[0:1, :] += dsh
                vec_ref[1:2, :] += dsc
                vec_ref[3:4, :] += dgn
                out_ref[rs, :] = dxo + dxin
            else:
                out_ref[rs, :] = dhp

        hb, a, dg, du, dxb = [jnp.concatenate(p, axis=0) if nsub > 1 else p[0] for p in zip(*parts)]
        accw[...] += _dot_tn(a, dxb)
        accg[...] += _dot_tn(hb, dg)
        accu[...] += _dot_tn(hb, du)

        @pl.when(i == nt - 1)
        def _():
            gw = accw[...]
            vec_ref[2:3, :] += 0.5 * jnp.sum(wo_ref[...].astype(F32) * gw, axis=0, keepdims=True)
            accw[...] = gw * (0.5 * gate)
            mx, my, cc = _me()
            rows = lambda base, n, c: pl.ds(base + c * n, n)
            pieces = [(accg, 0, hi, dwin_ref, rwin_ref, jj), (accu, 0, hi, dwin_ref, rwin_ref, 2 + jj),
                      (accw, 0, ho, dwout_ref, rwout_ref, 2 * jj), (accw, 2 * ho, ho, dwout_ref, rwout_ref, 2 * jj + 1)]
            loc = [pltpu.make_async_copy(acc.at[rows(base, n, cc)], own.at[slot], sems.at[p])
                   for p, (acc, base, n, own, _, slot) in enumerate(pieces)]
            rem = [pltpu.make_async_remote_copy(acc.at[rows(base, n, 1 - cc)], sib.at[slot], fsend.at[p], frecv.at[p],
                                                device_id=(mx, my, 1 - cc), device_id_type=MESH)
                   for p, (acc, base, n, _, sib, slot) in enumerate(pieces)]
            for cp in loc + rem:
                cp.start()
            for cp in loc:
                cp.wait()
            for cp in rem:
                cp.wait()

    once = pl.Buffered(1)
    tile = pl.BlockSpec((tm, D), lambda i: (i, 0))
    chunk = pl.BlockSpec((tm, CH), lambda i: (i, jj))
    in_specs = [tile, tile, chunk, chunk, pl.BlockSpec((8, D), lambda i: (0, 0)),
                pl.BlockSpec((None, D, CH), lambda i: (jj, 0, 0), pipeline_mode=once),
                pl.BlockSpec((None, D, CH), lambda i: (2 + jj, 0, 0), pipeline_mode=once),
                pl.BlockSpec((None, CH, D), lambda i: (jj, 0, 0), pipeline_mode=once)]
    args = (x, dxo, gs, us, modv, win, win, wout)
    if last:
        in_specs += [tile, ANY, ANY, ANY, ANY]
        args += tuple(prev)
    step = lambda s: lambda: pl.program_id(0) == s
    return _side_call(
        body, side, (step(0), None, step(nt - 1)), name="ffn_bwd",
        grid=(nt,), in_specs=in_specs,
        out_specs=[tile, ANY, ANY, ANY, ANY, pl.BlockSpec((8, D), lambda i: (0, 0))],
        out_shape=[_sds((S, D), F32), _sds((NQ, hi, CH), F32), _sds((NQ, ho, D), F32), _sds((NQ, hi, CH), F32),
                   _sds((NQ, ho, D), F32), _sds((8, D), F32)],
        scratch_shapes=[pltpu.VMEM((D, CH), F32), pltpu.VMEM((D, CH), F32), pltpu.VMEM((CH, D), F32),
                        pltpu.SemaphoreType.DMA((4,)), pltpu.SemaphoreType.DMA((4,)), pltpu.SemaphoreType.DMA((4,))],
        aliases={9 + p: 1 + p for p in range(4)} if last else {},
        compiler_params=_params(("arbitrary",), VMEM_LIMIT),
        args=args)


def _ffn_bwd(x, dxo, gs, us, modv, win, wout, side=None):
    first, extra = _ffn_bwd_pass(0, x, dxo, gs, us, modv, win, wout, side=side)
    (dx, dwin, dwout, rwin, rwout, vec), _ = _ffn_bwd_pass(1, x, dxo, gs, us, modv, win, wout, prev=first[:5])
    return (dx, dwin, dwout, rwin, rwout, first[5] + vec), extra


def _prep_spatial(w_spatial, b_spatial_t):
    def body(w_ref, b_ref, wcat_ref, wtcat_ref, bias_ref):
        row = lax.broadcasted_iota(jnp.int32, (CHUNK, CHUNK), 0)
        col = lax.broadcasted_iota(jnp.int32, (CHUNK, CHUNK), 1)
        tril = col <= row
        for p in range(4):
            wa = jnp.where(tril, w_ref[2 * p], 0.0)
            wb = jnp.where(tril, w_ref[2 * p + 1], 0.0)
            wcat_ref[p] = jnp.concatenate([wa, wb], axis=1).astype(BF16)
            wtcat_ref[p] = jnp.concatenate([wa.T, wb.T], axis=1).astype(BF16)
        head = lax.broadcasted_iota(jnp.int32, (8, DG), 0)
        ch = lax.broadcasted_iota(jnp.int32, (8, DG), 1)
        spread = jnp.where(ch // 64 == head, 1.0, 0.0).astype(F32)
        bias_ref[...] = jnp.dot(b_ref[...], spread, precision=HIGHEST, preferred_element_type=F32)

    return _call(
        body, name="prep_spatial",
        in_specs=[VMEM, VMEM], out_specs=[VMEM, VMEM, VMEM],
        out_shape=[_sds((4, CHUNK, 2 * CHUNK), BF16), _sds((4, CHUNK, 2 * CHUNK), BF16), _sds((CHUNK, DG), F32)],
    )(w_spatial, b_spatial_t)


def _pair_rhs(blocks):
    lane = lax.broadcasted_iota(jnp.int32, (CHUNK, LANE), 1)
    lo = lane < 64
    top = jnp.concatenate([jnp.where(lo, b, 0.0) for b in blocks], axis=1)
    bot = jnp.concatenate([jnp.where(lo, 0.0, b) for b in blocks], axis=1)
    return top, bot


def _gmlp_branch(zb, vecs, wcat_ref, bias_ref, nchunks):
    z, dz = _gelu_fwd_bwd(zb)
    u = z[:, :DG]
    v = z[:, DG:]
    ln_g, ln_b = vecs[1:2, :], vecs[2:3, :]
    mu = jnp.mean(v, axis=-1, keepdims=True)
    vc = v - mu
    rstd = lax.rsqrt(jnp.mean(vc * vc, axis=-1, keepdims=True) + EPS)
    vhat = vc * rstd
    vl = vhat * ln_g + ln_b
    sv_cols = []
    for p in range(4):
        blocks = [vl[k * CHUNK:(k + 1) * CHUNK, p * LANE:(p + 1) * LANE] for k in range(nchunks)]
        top, bot = _pair_rhs(blocks)
        rhs = jnp.concatenate([top, bot], axis=0).astype(BF16)
        out = _dot(wcat_ref[p], rhs)
        bias = bias_ref[:, p * LANE:(p + 1) * LANE]
        sv_cols.append(jnp.concatenate([out[:, k * LANE:(k + 1) * LANE] + bias for k in range(nchunks)], axis=0))
    sv = jnp.concatenate(sv_cols, axis=1)
    return dict(u=u, dz=dz, rstd=rstd, vhat=vhat, vl=vl, sv=sv, yb=u * sv)


def _mix_fwd(x, modv, win, wpool, vecs, wcat, bias, wout):
    S = x.shape[0]
    tm = TM_MIX
    nt = S // tm
    nchunks = tm // CHUNK

    def body(x_ref, mod_ref, win_ref, wpool_ref, vec_ref, wcat_ref, bias_ref, wout_ref,
             xo_ref, pooled_ref, zb_ref, ext):
        i = pl.program_id(0)

        @pl.when(i == 0)
        def _():
            ext[0:HALO, :] = jnp.zeros((HALO, DP), F32)

        x = x_ref[...]
        _, _, _, h = _norm_mod(x, mod_ref[3:4, :], mod_ref[1:2, :], mod_ref[0:1, :])
        proj = _dot(h.astype(BF16), win_ref[...])
        xa = proj[:, :DP]
        zb = proj[:, DP:]
        zb_ref[...] = zb
        ext[HALO:HALO + tm, :] = xa
        pos = i * tm + lax.broadcasted_iota(jnp.int32, (tm, 1), 0)
        vecs = vec_ref[...]
        ya_cols = []
        pooled_cols = []
        for gi, w in enumerate(POOL_WINDOWS):
            cols = slice(gi * LANE, (gi + 1) * LANE)
            s = xa[:, cols]
            for k in range(1, w):
                s = s + ext[HALO - k:HALO - k + tm, cols]
            cnt = jnp.minimum(pos + 1, w).astype(F32)
            pooled = (s / cnt - xa[:, cols]).astype(BF16)
            pooled_cols.append(pooled)
            ya_cols.append(_dot(pooled, wpool_ref[gi]) * vecs[0:1, cols])
        pooled_ref[...] = jnp.concatenate(pooled_cols, axis=1)
        ext[0:HALO, :] = ext[tm:tm + HALO, :]

        gm = _gmlp_branch(zb, vecs, wcat_ref, bias_ref, nchunks)
        cat = jnp.concatenate(ya_cols + [gm["yb"]], axis=1).astype(BF16)
        xo_ref[...] = x + mod_ref[2:3, :] * _dot(cat, wout_ref[...])

    full = lambda shape: pl.BlockSpec(shape, lambda i: (0,) * len(shape))
    return _call(
        body, name="mix_fwd",
        grid=(nt,),
        in_specs=[pl.BlockSpec((tm, D), lambda i: (i, 0)), full((8, D)), full((D, DPROJ)),
                  full((4, LANE, LANE)), full((8, DP)), full((4, CHUNK, 2 * CHUNK)), full((CHUNK, DG)),
                  full((DP + DG, D))],
        out_specs=[pl.BlockSpec((tm, D), lambda i: (i, 0)), pl.BlockSpec((tm, DP), lambda i: (i, 0)),
                   pl.BlockSpec((tm, 2 * DG), lambda i: (i, 0))],
        out_shape=[_sds((S, D), F32), _sds((S, DP), BF16), _sds((S, 2 * DG), F32)],
        scratch_shapes=[pltpu.VMEM((tm + HALO, DP), F32)],
        compiler_params=_params(("arbitrary",), VMEM_LIMIT),
    )(x, modv, win, wpool, vecs, wcat, bias, wout)


def _mix_bwd(x, dxo, pooled, zb, modv, win, wpool, vecs, wcat, wtcat, bias, wout, side=None):
    S = x.shape[0]
    tm = TM_MIX
    nt = S // tm
    nchunks = tm // CHUNK

    def body(x_ref, dxo_ref, pooled_ref, zb_ref, mod_ref, win_ref, wpool_ref, vec_ref, wcat_ref, wtcat_ref,
             bias_ref, wout_ref,
             dx_ref, dwin_ref, dwout_ref, dwpool_ref, dwsp_ref, dbsp_ref, v512_ref, vd_ref, qext, dsv_acc):
        step = pl.program_id(0)
        tile = nt - 1 - step

        @pl.when(step == 0)
        def _():
            dwin_ref[...] = jnp.zeros_like(dwin_ref)
            dwout_ref[...] = jnp.zeros_like(dwout_ref)
            dwpool_ref[...] = jnp.zeros_like(dwpool_ref)
            dwsp_ref[...] = jnp.zeros_like(dwsp_ref)
            v512_ref[...] = jnp.zeros_like(v512_ref)
            vd_ref[...] = jnp.zeros_like(vd_ref)
            dsv_acc[...] = jnp.zeros_like(dsv_acc)
            qext[tm:tm + HALO, :] = jnp.zeros((HALO, DP), F32)

        gn, sc, sh, gate = mod_ref[3:4, :], mod_ref[1:2, :], mod_ref[0:1, :], mod_ref[2:3, :]
        vecs = vec_ref[...]
        x = x_ref[...]
        r, xn, hp, h = _norm_mod(x, gn, sc, sh)
        hb = h.astype(BF16)
        dxo = dxo_ref[...]

        pooled = pooled_ref[...]
        mixed_cols = [_dot(pooled[:, gi * LANE:(gi + 1) * LANE], wpool_ref[gi]) for gi in range(4)]
        mixed = jnp.concatenate(mixed_cols, axis=1)
        scale = vecs[0:1, :]
        gm = _gmlp_branch(zb_ref[...], vecs, wcat_ref, bias_ref, nchunks)
        cat = jnp.concatenate([mixed * scale, gm["yb"]], axis=1).astype(BF16)

        dwout_ref[...] += _dot_tn(cat, dxo.astype(BF16))
        dcat = _dot_nt((dxo * gate).astype(BF16), wout_ref[...])
        dya = dcat[:, :DP]
        dyb = dcat[:, DP:]

        v512_ref[0:1, :] += jnp.sum(dya * mixed, axis=0, keepdims=True)
        dmixed = (dya * scale).astype(BF16)
        pos = tile * tm + lax.broadcasted_iota(jnp.int32, (tm, 1), 0)
        dpooled_cols = []
        for gi, w in enumerate(POOL_WINDOWS):
            cols = slice(gi * LANE, (gi + 1) * LANE)
            dp = _dot_nt(dmixed[:, cols], wpool_ref[gi])
            dwpool_ref[gi] += _dot_tn(pooled[:, cols], dmixed[:, cols])
            cnt = jnp.minimum(pos + 1, w).astype(F32)
            qext[0:tm, cols] = dp / cnt
            dpooled_cols.append(dp)
        dxa_cols = []
        for gi, w in enumerate(POOL_WINDOWS):
            cols = slice(gi * LANE, (gi + 1) * LANE)
            s = qext[0:tm, cols]
            for k in range(1, w):
                s = s + qext[k:k + tm, cols]
            dxa_cols.append(s - dpooled_cols[gi])
        qext[tm:tm + HALO, :] = qext[0:HALO, :]

        u, sv, vl = gm["u"], gm["sv"], gm["vl"]
        du = dyb * sv
        dsv = dyb * u
        dvl_cols = []
        for p in range(4):
            cols = slice(p * LANE, (p + 1) * LANE)
            dblocks = [dsv[k * CHUNK:(k + 1) * CHUNK, cols] for k in range(nchunks)]
            vblocks = [vl[k * CHUNK:(k + 1) * CHUNK, cols] for k in range(nchunks)]
            tot = dblocks[0]
            for b in dblocks[1:]:
                tot = tot + b
            dsv_acc[:, cols] += tot
            top, bot = _pair_rhs(dblocks)
            out = _dot(wtcat_ref[p], jnp.concatenate([top, bot], axis=0).astype(BF16))
            dvl_cols.append(jnp.concatenate([out[:, k * LANE:(k + 1) * LANE] for k in range(nchunks)], axis=0))
            vcat = jnp.concatenate(vblocks, axis=1).astype(BF16)
            dwsp_ref[2 * p] += _dot_nt(top.astype(BF16), vcat)
            dwsp_ref[2 * p + 1] += _dot_nt(bot.astype(BF16), vcat)
        dvl = jnp.concatenate(dvl_cols, axis=1)
        vhat, rstd = gm["vhat"], gm["rstd"]
        v512_ref[1:2, :] += jnp.sum(dvl * vhat, axis=0, keepdims=True)
        v512_ref[2:3, :] += jnp.sum(dvl, axis=0, keepdims=True)
        dvh = dvl * vecs[1:2, :]
        dv = rstd * (dvh - jnp.mean(dvh, axis=-1, keepdims=True)
                     - vhat * jnp.mean(dvh * vhat, axis=-1, keepdims=True))
        dzb = jnp.concatenate([du, dv], axis=1) * gm["dz"]

        dproj = jnp.concatenate(dxa_cols + [dzb], axis=1).astype(BF16)
        dwin_ref[...] += _dot_tn(hb, dproj)
        dh = _dot_nt(dproj, win_ref[...])
        dsh, dsc, dgn, dxin = _norm_mod_bwd(dh, r, xn, hp, gn, sc)
        vd_ref[0:1, :] += dsh
        vd_ref[1:2, :] += dsc
        vd_ref[3:4, :] += dgn
        dx_ref[...] = dxo + dxin

        @pl.when(step == nt - 1)
        def _():
            gw = dwout_ref[...]
            vd_ref[2:3, :] += jnp.sum(wout_ref[...].astype(F32) * gw, axis=0, keepdims=True)
            dwout_ref[...] = gw * gate
            row = lax.broadcasted_iota(jnp.int32, (CHUNK, CHUNK), 0)
            col = lax.broadcasted_iota(jnp.int32, (CHUNK, CHUNK), 1)
            for hh in range(8):
                dwsp_ref[hh] = jnp.where(col <= row, dwsp_ref[hh], 0.0)
            head = lax.broadcasted_iota(jnp.int32, (8, DG), 0)
            ch = lax.broadcasted_iota(jnp.int32, (8, DG), 1)
            spread = jnp.where(ch // 64 == head, 1.0, 0.0).astype(F32)
            dbsp_ref[...] = lax.dot_general(spread, dsv_acc[...], (((1,), (1,)), ((), ())),
                                            precision=HIGHEST, preferred_element_type=F32)

    full = lambda shape: pl.BlockSpec(shape, lambda s: (0,) * len(shape))
    rev = lambda cols: pl.BlockSpec((tm, cols), lambda s: (nt - 1 - s, 0))
    step = lambda s: lambda: pl.program_id(0) == s
    return _side_call(
        body, side, (step(0), None, step(nt - 1)), name="mix_bwd",
        grid=(nt,),
        in_specs=[rev(D), rev(D), rev(DP), rev(2 * DG), full((8, D)), full((D, DPROJ)), full((4, LANE, LANE)),
                  full((8, DP)), full((4, CHUNK, 2 * CHUNK)), full((4, CHUNK, 2 * CHUNK)), full((CHUNK, DG)),
                  full((DP + DG, D))],
        out_specs=[rev(D), full((D, DPROJ)), full((DP + DG, D)), full((4, LANE, LANE)), full((8, CHUNK, CHUNK)),
                   full((8, CHUNK)), full((8, DP)), full((8, D))],
        out_shape=[_sds((S, D), F32), _sds((D, DPROJ), F32), _sds((DP + DG, D), F32), _sds((4, LANE, LANE), F32),
                   _sds((8, CHUNK, CHUNK), F32), _sds((8, CHUNK), F32), _sds((8, DP), F32), _sds((8, D), F32)],
        scratch_shapes=[pltpu.VMEM((tm + HALO, DP), F32), pltpu.VMEM((CHUNK, DG), F32)],
        compiler_params=_params(("arbitrary",), VMEM_LIMIT),
        args=(x, dxo, pooled, zb, modv, win, wpool, vecs, wcat, wtcat, bias, wout))


def _chip_sum(g, rbuf, core):
    _, _, hr, cols = g.shape
    tr = _row_block(hr)

    def body(c_ref, g_ref, r_ref, o_ref):
        o_ref[...] = (g_ref[...] + r_ref[...]).astype(BF16)

    return pl.pallas_call(
        body, name="chip_sum", interpret=False,
        grid_spec=pltpu.PrefetchScalarGridSpec(
            num_scalar_prefetch=1, grid=(NQ, hr // tr),
            in_specs=[pl.BlockSpec((None, None, tr, cols), lambda q, i, c: (q, c[0], i, 0)),
                      pl.BlockSpec((None, tr, cols), lambda q, i, c: (q, i, 0))],
            out_specs=pl.BlockSpec((None, tr, cols), lambda q, i, c: (q, i, 0))),
        out_shape=_sds((NQ, hr, cols), BF16),
        compiler_params=_params(("arbitrary", "arbitrary"), None),
    )(core, g, rbuf)


def _chip_sum_pair(own, rbuf, slots=(0, 1, 2, 3), prev=None):
    _, hr, cols = own.shape
    tr = _row_block(hr)
    a, b = slots[0], (slots[1] - slots[0] if len(slots) > 1 else 0)
    assert list(slots) == [a + b * k for k in range(len(slots))]

    def body(a_ref, b_ref, *rest):
        rest[-1][...] = (a_ref[...] + b_ref[...]).astype(BF16)

    spec = pl.BlockSpec((None, tr, cols), lambda k, i: (a + b * k, i, 0))
    return _call(
        body, name="chip_sum_pair",
        grid=(len(slots), hr // tr),
        in_specs=[spec, spec] + ([ANY] if prev is not None else []), out_specs=spec,
        out_shape=_sds((NQ, hr, cols), BF16),
        input_output_aliases={2: 0} if prev is not None else {},
        compiler_params=_params(("arbitrary", "arbitrary"), None),
    )(own, rbuf, *([prev] if prev is not None else []))


def _sum4(cs, rbuf, chip):
    _, hr, cols = rbuf.shape
    tr = _row_block(hr)

    def body(q_ref, c_ref, r1_ref, r2_ref, r3_ref, o_ref):
        acc = c_ref[...].astype(F32)
        for r in (r1_ref, r2_ref, r3_ref):
            acc = acc + r[...].astype(F32)
        o_ref[...] = acc

    slot = lambda k: pl.BlockSpec((None, tr, cols), lambda i, q: ((q[0] + k) % NQ, i, 0))
    return pl.pallas_call(
        body, name="sum4", interpret=False,
        grid_spec=pltpu.PrefetchScalarGridSpec(
            num_scalar_prefetch=1, grid=(hr // tr,),
            in_specs=[slot(0), slot(1), slot(2), slot(3)],
            out_specs=pl.BlockSpec((tr, cols), lambda i, q: (i, 0))),
        out_shape=_sds((hr, cols), F32),
        compiler_params=_params(("arbitrary",), None),
    )(chip, cs, rbuf, rbuf, rbuf)


def _adamw_halves(w, own, recv, m, v, side=None, after=()):
    rows, cols = w.shape
    hr = rows // 2
    tr = _row_block(hr, mult=8)
    nb = hr // tr

    def body(w_ref, own_ref, recv_ref, m_ref, v_ref, *rest):
        g_ref, d_ref, mo_ref, vo_ref = rest[len(after):]
        g = jnp.where(pl.program_id(0) == lax.axis_index("c"), own_ref[...], recv_ref[...])
        d, mn, vn = _adamw(w_ref[...], g, m_ref[...], v_ref[...])
        g_ref[...] = g
        d_ref[...] = d
        mo_ref[...] = mn
        vo_ref[...] = vn

    full = pl.BlockSpec((tr, cols), lambda h, i: (h * nb + i, 0))
    half = pl.BlockSpec((tr, cols), lambda h, i: (i, 0))
    step = lambda h, i: lambda: (pl.program_id(0) == h) & (pl.program_id(1) == i)
    return _side_call(
        body, side, (step(0, 0), None, step(1, nb - 1)), name="adamw_halves",
        grid=(2, nb), in_specs=[full, half, half, full, full] + [ANY] * len(after), out_specs=[full] * 4,
        out_shape=[_sds((rows, cols), F32)] * 4, scratch_shapes=[],
        compiler_params=_params(("arbitrary", "arbitrary"), None),
        args=(w, own, recv, m, v, *after))


def _cast_place(w, chip):
    rows, cols = w.shape
    tr = _row_block(rows)

    def body(q_ref, w_ref, o_ref):
        o_ref[...] = w_ref[...].astype(BF16)

    return pl.pallas_call(
        body, name="cast_place", interpret=False,
        grid_spec=pltpu.PrefetchScalarGridSpec(
            num_scalar_prefetch=1, grid=(rows // tr,),
            in_specs=[pl.BlockSpec((tr, cols), lambda i, q: (i, 0))],
            out_specs=pl.BlockSpec((None, tr, cols), lambda i, q: (q[0], i, 0))),
        out_shape=_sds((NQ, rows, cols), BF16),
        compiler_params=_params(("arbitrary",), None),
    )(chip, w)


def _ada_grad_adamw(cact_t, dmod_q, w, m, v, side=None):
    rows, cols = w.shape
    tc = 256
    assert cols % tc == 0

    def body(c_ref, d_ref, w_ref, m_ref, v_ref, g_ref, dl_ref, mo_ref, vo_ref):
        g = jnp.dot(c_ref[...], d_ref[...], precision=HIGHEST, preferred_element_type=F32)
        d, mn, vn = _adamw(w_ref[...], g, m_ref[...], v_ref[...])
        g_ref[...] = g
        dl_ref[...] = d
        mo_ref[...] = mn
        vo_ref[...] = vn

    spec = pl.BlockSpec((rows, tc), lambda i: (0, i))
    step = lambda s: lambda: pl.program_id(0) == s
    return _side_call(
        body, side, (step(0), None, step(cols // tc - 1)), name="ada_grad_adamw",
        grid=(cols // tc,),
        in_specs=[pl.BlockSpec((rows, 8), lambda i: (0, 0)), pl.BlockSpec((8, tc), lambda i: (0, i)),
                  spec, spec, spec],
        out_specs=[spec] * 4,
        out_shape=[_sds((rows, cols), F32)] * 4,
        scratch_shapes=[],
        compiler_params=_params(("arbitrary",), None),
        args=(cact_t, dmod_q, w, m, v))


def _me():
    x, y, c = lax.axis_index("x"), lax.axis_index("y"), lax.axis_index("c")
    return x, y, c


_OFFSETS7 = [(dx, dy, dc) for dx in (0, 1) for dy in (0, 1) for dc in (0, 1) if (dx, dy, dc) != (0, 0, 0)]
_CHIP_OFFSETS = [(1, 0), (0, 1), (1, 1)]


def _ada_fwd(c, w_ada_q, b_ada_q, side=None):
    ncol = w_ada_q.shape[1]

    def body(c_ref, w_ref, b_ref, cact_ref, modsel_ref, blk, gath, res, parts, send_sems, recv_sems, side_start=None):
        x, y, cc = _me()
        me = 4 * x + 2 * y + cc
        q = 2 * x + y
        cv = c_ref[...]
        ca = cv * jax.nn.sigmoid(cv)
        row = lax.broadcasted_iota(jnp.int32, (8, D), 0)
        blk[...] = jnp.where(row == me, jnp.broadcast_to(ca, (8, D)), 0.0)
        gath[me] = blk[...]
        sends = []
        for k, (dx, dy, dc) in enumerate(_OFFSETS7):
            cp = pltpu.make_async_remote_copy(blk, gath.at[me], send_sems.at[k], recv_sems.at[k],
                                              device_id=(x ^ dx, y ^ dy, cc ^ dc), device_id_type=MESH)
            cp.start()
            sends.append(cp)
        if side_start is not None:
            side_start()
        for cp in sends:
            cp.wait_recv()
        cact = gath[0]
        for d in range(1, N_DEV):
            cact = cact + gath[d]
        cact_ref[...] = cact
        res[...] = jnp.dot(cact, w_ref[...], precision=HIGHEST, preferred_element_type=F32) + b_ref[...]
        parts[q] = res[...]
        sends2 = []
        for k, (dx, dy) in enumerate(_CHIP_OFFSETS):
            cp = pltpu.make_async_remote_copy(res, parts.at[q], send_sems.at[7 + k], recv_sems.at[7 + k],
                                              device_id=(x ^ dx, y ^ dy, cc), device_id_type=MESH)
            cp.start()
            sends2.append(cp)
        for cp in sends2:
            cp.wait_recv()
        row2 = lax.broadcasted_iota(jnp.int32, (8, ncol), 0)
        out = jnp.zeros((8, ncol), F32)
        for s in range(NQ):
            mine = jnp.sum(jnp.where(row2 == me, parts[s], 0.0), axis=0, keepdims=True)
            out = out + jnp.where(row2 == s, jnp.broadcast_to(mine, (8, ncol)), 0.0)
        modsel_ref[...] = out
        for cp in sends + sends2:
            cp.wait_send()

    return _side_call(
        body, side, None, name="ada_fwd",
        in_specs=[VMEM, VMEM, VMEM], out_specs=[VMEM, VMEM],
        out_shape=[_sds((8, D), F32), _sds((8, ncol), F32)],
        scratch_shapes=[pltpu.VMEM((8, D), F32), pltpu.VMEM((N_DEV, 8, D), F32), pltpu.VMEM((8, ncol), F32),
                        pltpu.VMEM((NQ, 8, ncol), F32), pltpu.SemaphoreType.DMA((10,)), pltpu.SemaphoreType.DMA((10,))],
        compiler_params=_params(None, VMEM_LIMIT), start_in_body=side is not None,
        args=(c, w_ada_q, b_ada_q))


class _Side:
    def __init__(self, ins, out_shapes, aliases, nsem, start, mid=None, finish=None):
        self.ins, self.out_shapes, self.aliases, self.nsem = list(ins), list(out_shapes), dict(aliases), nsem
        self.start, self.mid, self.finish = start, mid, finish


def _join(*sides):
    ins, outs, aliases, offs, nsem = [], [], {}, [], 0
    for s in sides:
        offs.append((len(ins), len(outs), nsem))
        aliases.update({len(ins) + a: len(outs) + b for a, b in s.aliases.items()})
        ins += s.ins
        outs += s.out_shapes
        nsem += s.nsem

    def hook(name):
        def run(i, o, ss, rs, base):
            for s, (io, oo, so) in zip(sides, offs):
                fn = getattr(s, name)
                if fn is not None:
                    fn(i[io:io + len(s.ins)], o[oo:oo + len(s.out_shapes)], ss, rs, base + so)
        return run

    return _Side(ins, outs, aliases, nsem, hook("start"), hook("mid"), hook("finish"))


def _side_call(body, side, when, *, name, in_specs, out_specs, out_shape, scratch_shapes, args, aliases=None,
               start_in_body=False, **kw):
    n_in, n_out = len(in_specs), len(out_specs)
    aliases = dict(aliases or {})
    if side is None:
        return _call(body, name=name, in_specs=in_specs, out_specs=out_specs, out_shape=out_shape,
                     scratch_shapes=scratch_shapes, input_output_aliases=aliases, **kw)(*args), []
    ns_in, ns_out = len(side.ins), len(side.out_shapes)

    def hook(fn, k, operands):
        if fn is None:
            return
        if when is None:
            fn(*operands, 0)
        elif when[k] is not None:
            pl.when(when[k]())(functools.partial(fn, *operands, 0))

    def wrapped(*refs):
        ins, s_ins = refs[:n_in], refs[n_in:n_in + ns_in]
        o0 = n_in + ns_in
        outs, s_outs = refs[o0:o0 + n_out], refs[o0 + n_out:o0 + n_out + ns_out]
        rest = refs[o0 + n_out + ns_out:]
        scratch, operands = rest[:-2], (s_ins, s_outs, rest[-2], rest[-1])
        if start_in_body:
            body(*ins, *outs, *scratch, side_start=functools.partial(hook, side.start, 0, operands))
        else:
            hook(side.start, 0, operands)
            body(*ins, *outs, *scratch)
        hook(side.mid, 1, operands)
        hook(side.finish, 2, operands)

    res = _call(
        wrapped, name=name,
        in_specs=list(in_specs) + [ANY] * ns_in, out_specs=list(out_specs) + [ANY] * ns_out,
        out_shape=list(out_shape) + side.out_shapes,
        scratch_shapes=list(scratch_shapes) + [pltpu.SemaphoreType.DMA((side.nsem,)),
                                               pltpu.SemaphoreType.DMA((side.nsem,))],
        input_output_aliases={**aliases, **{n_in + a: n_out + b for a, b in side.aliases.items()}},
        **kw)(*args, *side.ins)
    return res[:n_out], res[n_out:]


def _run_side(side, name):
    return _side_call(lambda: None, side, None, name=name, in_specs=[], out_specs=[], out_shape=[],
                      scratch_shapes=[], args=[])[1]


def _remote(src, dst, ss, rs, k, dev):
    return pltpu.make_async_remote_copy(src, dst, ss.at[k], rs.at[k], device_id=dev, device_id_type=MESH)


def _gather_side(bufs):
    n = len(bufs)

    def plan(outs, w):
        x, y, cc = _me()
        hr = outs[w].shape[1] // 2
        mine, other = cc * hr, (1 - cc) * hr
        qx, qy, qd, q = 2 * (x ^ 1) + y, 2 * x + (y ^ 1), 2 * (x ^ 1) + (y ^ 1), 2 * x + y
        xn, yn, sib = (x ^ 1, y, cc), (x, y ^ 1, cc), (x, y, 1 - cc)
        at = lambda slot, r0, nr: outs[w].at[slot, pl.ds(r0, nr)]
        send = [(at(q, mine, hr), xn), (at(q, mine, hr), yn),
                (at(qx, mine, hr // 2), yn), (at(qy, mine + hr // 2, hr // 2), xn),
                (at(qx, mine, hr), sib), (at(qy, mine, hr), sib), (at(qd, mine, hr), sib)]
        recv = [at(qx, mine, hr), at(qy, mine, hr), at(qd, mine, hr // 2), at(qd, mine + hr // 2, hr // 2),
                at(qx, other, hr), at(qy, other, hr), at(qd, other, hr)]
        return send, recv

    def op(outs, ss, rs, b, w, k, what):
        send, recv = plan(outs, w)
        if what == "wait_recv":
            _remote(recv[k], recv[k], ss, rs, b + 7 * w + k, send[k][1]).wait_recv()
        else:
            getattr(_remote(send[k][0], send[k][0], ss, rs, b + 7 * w + k, send[k][1]), what)()

    def start(ins, outs, ss, rs, b):
        for w in range(n):
            for k in (0, 1):
                op(outs, ss, rs, b, w, k, "start")

    def mid(ins, outs, ss, rs, b):
        for w in range(n):
            for k in (0, 1):
                op(outs, ss, rs, b, w, k, "wait_recv")
                op(outs, ss, rs, b, w, 2 + k, "start")
                op(outs, ss, rs, b, w, 4 + k, "start")

    def finish(ins, outs, ss, rs, b):
        for w in range(n):
            for k in (2, 3):
                op(outs, ss, rs, b, w, k, "wait_recv")
            op(outs, ss, rs, b, w, 6, "start")
        for w in range(n):
            for k in (4, 5, 6):
                op(outs, ss, rs, b, w, k, "wait_recv")
            for k in range(7):
                op(outs, ss, rs, b, w, k, "wait_send")

    return _Side(bufs, [_sds(tuple(w.shape), w.dtype) for w in bufs], {i: i for i in range(n)}, 7 * n,
                 start, mid, finish)


def _copies_side(ins, out_shapes, nsem, copies):
    def start(*a):
        for cp in copies(*a):
            cp.start()

    def finish(*a):
        for cp in copies(*a):
            cp.wait()

    return _Side(ins, out_shapes, {}, nsem, start, None, finish)


def _swap_side(gs):
    def copies(ins, outs, ss, rs, b):
        x, y, cc = _me()
        return [_remote(ins[w].at[:, 1 - cc], outs[w], ss, rs, b + w, (x, y, 1 - cc)) for w in range(len(gs))]

    return _copies_side(gs, [_sds((NQ,) + tuple(g.shape[2:]), F32) for g in gs], len(gs), copies)


def _exchange_side(cs, slots=None, prev=None):
    n = len(cs)
    slots = slots or [(0, 1, 2, 3)] * n

    def among(chip, allowed):
        hit = chip == allowed[0]
        for s in allowed[1:]:
            hit = hit | (chip == s)
        return hit

    def each(ins, outs, ss, rs, b, do_send, do_recv):
        x, y, cc = _me()
        q = 2 * x + y
        for w in range(n):
            for j, (dx, dy) in enumerate(_CHIP_OFFSETS):
                pq = 2 * (x ^ dx) + (y ^ dy)
                cp = _remote(ins[w].at[pq], outs[w].at[q], ss, rs, b + 3 * w + j, (x ^ dx, y ^ dy, cc))
                if do_send is not None:
                    pl.when(among(pq, slots[w]))(functools.partial(do_send, cp))
                if do_recv is not None:
                    pl.when(among(q, slots[w]))(functools.partial(do_recv, cp))

    def start(ins, outs, ss, rs, b):
        each(ins, outs, ss, rs, b, lambda cp: cp.start(), None)

    def finish(ins, outs, ss, rs, b):
        each(ins, outs, ss, rs, b, lambda cp: cp.wait_send(), lambda cp: cp.wait_recv())

    ins = list(cs) + (list(prev) if prev is not None else [])
    aliases = {n + w: w for w in range(n)} if prev is not None else {}
    return _Side(ins, [_sds(tuple(c.shape), c.dtype) for c in cs], aliases, 3 * n, start, None, finish)


def _exchange_copies(srcs, lands, send_sems, recv_sems):
    x, y, cc = _me()
    return [pltpu.make_async_remote_copy(srcs[w].at[2 * (x ^ dx) + (y ^ dy)], lands[w].at[2 * x + y],
                                         send_sems.at[3 * w + j], recv_sems.at[3 * w + j],
                                         device_id=(x ^ dx, y ^ dy, cc), device_id_type=MESH)
            for w in range(len(srcs)) for j, (dx, dy) in enumerate(_CHIP_OFFSETS)]


def _exchange_start(cs, after=()):
    n = len(cs)
    hbm, sem = pl.BlockSpec(memory_space=pltpu.HBM), pl.BlockSpec(memory_space=pltpu.SEMAPHORE)
    srcs = [pltpu.with_memory_space_constraint(c, pltpu.HBM) for c in cs]
    lands = [pltpu.with_memory_space_constraint(lax.empty(c.shape, c.dtype), pltpu.HBM) for c in cs]

    def body(*refs):
        sems = 2 * n + len(after)
        for cp in _exchange_copies(refs[:n], refs[n:2 * n], refs[sems], refs[sems + 1]):
            cp.start()
        refs[-1][...] = jnp.zeros_like(refs[-1])

    res = pl.pallas_call(
        body, name="exchange_start", interpret=False,
        out_shape=(pltpu.SemaphoreType.DMA((3 * n,)), pltpu.SemaphoreType.DMA((3 * n,)),
                   *[pltpu.HBM(c.shape, c.dtype) for c in cs], *[pltpu.HBM(c.shape, c.dtype) for c in cs],
                   _sds((8, LANE), F32)),
        in_specs=(hbm,) * (2 * n) + (ANY,) * len(after), out_specs=(sem, sem) + (hbm,) * (2 * n) + (VMEM,),
        input_output_aliases={i: 2 + i for i in range(2 * n)},
        compiler_params=pltpu.CompilerParams(has_side_effects=pltpu.SideEffectType.DATAFLOW_SIDE_EFFECTING),
    )(*srcs, *lands, *after)
    return res[0], res[1], list(res[2:2 + n]), list(res[2 + n:2 + 2 * n]), res[-1]


def _exchange_wait(send_sems, recv_sems, srcs, lands, after):
    n = len(srcs)
    hbm, sem = pl.BlockSpec(memory_space=pltpu.HBM), pl.BlockSpec(memory_space=pltpu.SEMAPHORE)

    def body(*refs):
        for cp in _exchange_copies(refs[:n], refs[n:2 * n], refs[2 * n], refs[2 * n + 1]):
            cp.wait_send()
            cp.wait_recv()

    res = pl.pallas_call(
        body, name="exchange_wait", interpret=False,
        out_shape=[pltpu.HBM(c.shape, c.dtype) for c in srcs + lands],
        in_specs=(hbm,) * (2 * n) + (sem, sem) + (ANY,) * len(after), out_specs=(hbm,) * (2 * n),
        input_output_aliases={i: i for i in range(2 * n)},
        compiler_params=pltpu.CompilerParams(has_side_effects=pltpu.SideEffectType.DATAFLOW_SIDE_EFFECTING),
    )(*srcs, *lands, send_sems, recv_sems, *after)
    return list(res[:n]), list(res[n:])


def _share_side(fs):
    def copies(ins, outs, ss, rs, b):
        x, y, cc = _me()
        return [_remote(ins[w], outs[w], ss, rs, b + w, (x, y, 1 - cc)) for w in range(len(fs))]

    return _copies_side(fs, [_sds(tuple(f.shape), F32) for f in fs], len(fs), copies)


def _small_allreduce_adamw(g, w, m, v, nd):
    rows = g.shape[0]
    nr = rows - nd
    hr = nr // 2
    assert nd % 8 == 0 and hr % 8 == 0

    def body(g_ref, w_ref, m_ref, v_ref, gs_ref, d_ref, mo_ref, vo_ref, gath, sib, csum, slots, tot, ss, rs):
        x, y, cc = _me()
        me = 4 * x + 2 * y + cc
        q = 2 * x + y
        sibling = (x, y, 1 - cc)
        dm = g_ref.at[pl.ds(0, nd)]
        gath[me] = g_ref[0:nd, :]
        to_all = [_remote(dm, gath.at[me], ss, rs, k, (x ^ dx, y ^ dy, cc ^ dc)) for k, (dx, dy, dc) in enumerate(_OFFSETS7)]
        to_sib = _remote(g_ref.at[pl.ds(nd, nr)], sib, ss, rs, 7, sibling)
        for cp in to_all + [to_sib]:
            cp.start()
        to_sib.wait_recv()
        csum[...] = g_ref[nd:, :] + sib[...]
        mine = pl.ds(pl.multiple_of(cc * hr, 8), hr)
        slots[q] = csum[mine, :]
        to_chips = [_remote(csum.at[mine], slots.at[q], ss, rs, 8 + j, (x ^ dx, y ^ dy, cc))
                    for j, (dx, dy) in enumerate(_CHIP_OFFSETS)]
        for cp in to_chips:
            cp.start()
        for cp in to_chips:
            cp.wait_recv()
        tot[mine, :] = (slots[0] + slots[1]) + (slots[2] + slots[3])
        halves = _remote(tot.at[mine], tot.at[mine], ss, rs, 11, sibling)
        halves.start()
        for cp in to_all:
            cp.wait_recv()
        dsum = gath[0]
        for dev in range(1, N_DEV):
            dsum = dsum + gath[dev]
        halves.wait_recv()
        for lo, n, total in ((0, nd, dsum), (nd, nr, tot[...])):
            gs_ref[lo:lo + n, :] = total
            d, mn, vn = _adamw(w_ref[lo:lo + n, :], total, m_ref[lo:lo + n, :], v_ref[lo:lo + n, :])
            d_ref[lo:lo + n, :] = d
            mo_ref[lo:lo + n, :] = mn
            vo_ref[lo:lo + n, :] = vn
        for cp in to_all + [to_sib, halves] + to_chips:
            cp.wait_send()

    return _call(
        body, name="small_allreduce_adamw",
        in_specs=[VMEM] * 4, out_specs=[VMEM] * 5,
        out_shape=[_sds((rows, LANE), F32)] * 4 + [_sds((N_DEV, nd, LANE), F32)],
        scratch_shapes=[pltpu.VMEM((nr, LANE), F32), pltpu.VMEM((nr, LANE), F32), pltpu.VMEM((NQ, hr, LANE), F32),
                        pltpu.VMEM((nr, LANE), F32), pltpu.SemaphoreType.DMA((12,)), pltpu.SemaphoreType.DMA((12,))],
        compiler_params=_params(None, VMEM_LIMIT),
    )(g, w, m, v)


_SMALL = ["b_ada", "norm_ffn1_g", "norm_mix_g", "pool_scale", "gmlp_ln_g", "gmlp_ln_b", "b_spatial",
          "norm_ffn2_g", "norm_final_g", "w_pool", "w_spatial"]


def _pack(parts):
    blocks, layout, r0 = [], {}, 0
    for name in _SMALL:
        a = parts[name]
        n = a.size
        rows = -(-n // LANE)
        rows8 = -(-rows // 8) * 8
        flat = a.reshape(-1).astype(F32)
        if rows8 * LANE != n:
            flat = jnp.concatenate([flat, jnp.zeros((rows8 * LANE - n,), F32)])
        blocks.append(flat.reshape(rows8, LANE))
        layout[name] = (r0, n, a.shape)
        r0 += rows8
    return jnp.concatenate(blocks, axis=0), layout


def _unpack(packed, layout):
    out = {}
    for name, (r0, n, shape) in layout.items():
        rows = -(-n // LANE)
        out[name] = packed[r0:r0 + rows].reshape(-1)[:n].reshape(shape)
    return out


def _modv(mod9, sub, gain):
    rows = jnp.concatenate([mod9[3 * sub:3 * sub + 3], gain.reshape(1, D), jnp.zeros((4, D), F32)], axis=0)
    return rows


_BIG = ["ffn1_w_in", "ffn1_w_out", "w_mix_in", "w_mix_out", "ffn2_w_in", "ffn2_w_out"]


def kernel(x, c, w_ada, b_ada, norm_ffn1_g, ffn1_w_in, ffn1_w_out, norm_mix_g, w_mix_in, w_pool, pool_scale, gmlp_ln_g, gmlp_ln_b, w_spatial, b_spatial, w_mix_out, norm_ffn2_g, ffn2_w_in, ffn2_w_out, norm_final_g, loss_target, m_w_ada, m_b_ada, m_norm_ffn1_g, m_ffn1_w_in, m_ffn1_w_out, m_norm_mix_g, m_w_mix_in, m_w_pool, m_pool_scale, m_gmlp_ln_g, m_gmlp_ln_b, m_w_spatial, m_b_spatial, m_w_mix_out, m_norm_ffn2_g, m_ffn2_w_in, m_ffn2_w_out, m_norm_final_g, v_w_ada, v_b_ada, v_norm_ffn1_g, v_ffn1_w_in, v_ffn1_w_out, v_norm_mix_g, v_w_mix_in, v_w_pool, v_pool_scale, v_gmlp_ln_g, v_gmlp_ln_b, v_w_spatial, v_b_spatial, v_w_mix_out, v_norm_ffn2_g, v_ffn2_w_in, v_ffn2_w_out, v_norm_final_g):
    names = ["w_ada", "b_ada", "norm_ffn1_g", "ffn1_w_in", "ffn1_w_out", "norm_mix_g", "w_mix_in", "w_pool",
             "pool_scale", "gmlp_ln_g", "gmlp_ln_b", "w_spatial", "b_spatial", "w_mix_out", "norm_ffn2_g",
             "ffn2_w_in", "ffn2_w_out", "norm_final_g"]
    W = dict(zip(names, [w_ada, b_ada, norm_ffn1_g, ffn1_w_in, ffn1_w_out, norm_mix_g, w_mix_in, w_pool, pool_scale,
                         gmlp_ln_g, gmlp_ln_b, w_spatial, b_spatial, w_mix_out, norm_ffn2_g, ffn2_w_in, ffn2_w_out,
                         norm_final_g]))
    M = dict(zip(names, [m_w_ada, m_b_ada, m_norm_ffn1_g, m_ffn1_w_in, m_ffn1_w_out, m_norm_mix_g, m_w_mix_in, m_w_pool,
                         m_pool_scale, m_gmlp_ln_g, m_gmlp_ln_b, m_w_spatial, m_b_spatial, m_w_mix_out, m_norm_ffn2_g,
                         m_ffn2_w_in, m_ffn2_w_out, m_norm_final_g]))
    V = dict(zip(names, [v_w_ada, v_b_ada, v_norm_ffn1_g, v_ffn1_w_in, v_ffn1_w_out, v_norm_mix_g, v_w_mix_in, v_w_pool,
                         v_pool_scale, v_gmlp_ln_g, v_gmlp_ln_b, v_w_spatial, v_b_spatial, v_w_mix_out, v_norm_ffn2_g,
                         v_ffn2_w_in, v_ffn2_w_out, v_norm_final_g]))

    xi, yi, ci = _me()
    q = 2 * xi + yi
    core = ci.astype(jnp.int32).reshape(1)

    chip = q.astype(jnp.int32).reshape(1)
    place = lambda n: _cast_place(W[n][0], chip)

    ncol = w_ada.shape[2]
    b_q = lax.dynamic_slice_in_dim(b_ada, q * ncol, ncol, axis=1)
    (cact_all, modsel), (win1, wout1) = _ada_fwd(
        c, w_ada[0], b_q, side=_gather_side([place("ffn1_w_in"), place("ffn1_w_out")]))
    mod9 = modsel[:NQ].reshape(9, D)
    xs, target = x[0], loss_target[0]
    mv1 = _modv(mod9, 0, norm_ffn1_g[0])
    mv2 = _modv(mod9, 1, norm_mix_g[0])
    mv3 = _modv(mod9, 2, norm_ffn2_g[0])
    wcat, wtcat, bias = _prep_spatial(w_spatial[0], b_spatial[0].T)
    wpool = w_pool[0].astype(BF16)
    vecs = jnp.concatenate([pool_scale, gmlp_ln_g, gmlp_ln_b, jnp.zeros((5, DP), F32)], axis=0)
    gf = jnp.concatenate([norm_final_g.reshape(1, D), jnp.zeros((7, D), F32)], axis=0)

    later =["w_mix_in", "w_mix_out", "ffn2_w_in", "ffn2_w_out"]
    (x1, g1s, u1s), got = _ffn_fwd(xs, mv1, win1, wout1.reshape(2, CH, D), side=_gather_side([place(n) for n in later]))
    wmi, wmo, win2, wout2 = got
    wmi = jnp.transpose(wmi, (1, 0, 2)).reshape(D, DPROJ)
    wmo = wmo.reshape(DP + DG, D)
    x2, pooled, zb = _mix_fwd(x1, mv2, wmi, wpool, vecs, wcat, bias, wmo)
    (dx3, g3s, u3s, loss_blk, dgf), _ = _ffn_fwd(x2, mv3, win2, wout2.reshape(2, CH, D), head=(target, gf))

    wo1, wo2 = wout1.reshape(2, CH, D), wout2.reshape(2, CH, D)
    (dx2, oin2, oout2, rin2, rout2, vec3), _ = _ffn_bwd(x2, dx3, g3s, u3s, mv3, win2, wo2)
    cs2 = [_chip_sum_pair(oin2, rin2), _chip_sum_pair(oout2, rout2)]
    (dx1, dwmi, dwmo, dwpool, dwsp, dbsp, v512, vec2), ex2 = _mix_bwd(
        x1, dx2, pooled, zb, mv2, wmi, wpool, vecs, wcat, wtcat, bias, wmo, side=_exchange_side(cs2))
    half2 = [_sum4(cs, e, chip) for cs, e in zip(cs2, ex2)]
    qcols = w_mix_in.shape[2]
    vmix = [jnp.transpose(dwmi.reshape(D, NQ, qcols), (1, 0, 2)).reshape(NQ, 2, D // 2, qcols),
            dwmo.reshape(NQ, 2, (DP + DG) // 8, D)]
    first1, got = _ffn_bwd_pass(0, xs, dx1, g1s, u1s, mv1, win1, wo1,
                                side=_join(_swap_side(vmix), _share_side(half2)))
    sibmix, other2 = got[:2], got[2:]
    cs_mix = [_chip_sum(g, r, core) for g, r in zip(vmix, sibmix)]
    (grad_x, oin1, oout1, rin1, rout1, vec1), ex_mix = _ffn_bwd_pass(
        1, xs, dx1, g1s, u1s, mv1, win1, wo1, prev=first1[:5], side=_exchange_side(cs_mix))
    vec1 = first1[5] + vec1
    cs_ffn1 = [_chip_sum_pair(oin1, rin1), _chip_sum_pair(oout1, rout1)]

    dmod =jnp.concatenate([vec1[0:3], vec2[0:3], vec3[0:3]], axis=0)
    grads = dict(
        b_ada=dmod.reshape(1, 9 * D), norm_ffn1_g=vec1[3:4], norm_mix_g=vec2[3:4], norm_ffn2_g=vec3[3:4],
        pool_scale=v512[0:1], gmlp_ln_g=v512[1:2], gmlp_ln_b=v512[2:3], b_spatial=dbsp[None],
        norm_final_g=dgf[0], w_pool=dwpool[None], w_spatial=dwsp[None])

    gp, layout = _pack({n: grads[n] for n in _SMALL})
    gp = jnp.concatenate([gp, loss_blk, loss_blk], axis=0)
    pad = jnp.zeros((16, LANE), F32)
    wp, mp, vp = [jnp.concatenate([_pack({n: src[n] for n in _SMALL})[0], pad], axis=0) for src in (W, M, V)]
    r0, nb, _ = layout["b_ada"]
    assert r0 == 0
    out_g, out_d, out_m, out_v = {}, {}, {}, {}
    gs, dl, mo, vo, gath = _small_allreduce_adamw(gp, wp, mp, vp, nb // LANE)
    loss = gs[-16, 0]
    for packed, dst in ((gs, out_g), (dl, out_d), (mo, out_m), (vo, out_v)):
        for n, a in _unpack(packed, layout).items():
            dst[n] = a.reshape(W[n].shape)

    def update(n, own, recv, after=()):
        (g2, d, mn, vn), _ = _adamw_halves(W[n][0], own, recv, M[n][0], V[n][0], after=after)
        out_g[n], out_d[n], out_m[n], out_v[n] = g2[None], d[None], mn[None], vn[None]
        return g2

    ssem, rsem, cs_fly, land_fly, token = _exchange_start(cs_ffn1, after=(gath,))
    dmod_q = lax.dynamic_slice_in_dim(gath.reshape(N_DEV, nb), q * ncol, ncol, axis=1) + token[0:8, 0:1]
    (ga, da, ma, va), _ = _ada_grad_adamw(cact_all.T, dmod_q, w_ada[0], m_w_ada[0], v_w_ada[0])
    out_g["w_ada"], out_d["w_ada"], out_m["w_ada"], out_v["w_ada"] = ga[None], da[None], ma[None], va[None]
    done = [ga, update("ffn2_w_in", half2[0], other2[0], after=(token,)),
            update("ffn2_w_out", half2[1], other2[1], after=(token,))]
    cs_ffn1, ex_ffn1 = _exchange_wait(ssem, rsem, cs_fly, land_fly, after=done)
    half1 = [_sum4(cs, e, chip) for cs, e in zip(cs_mix + cs_ffn1, list(ex_mix) + list(ex_ffn1))]
    other1 = _run_side(_share_side(half1), "sibling_share")
    for n, own, recv in zip(["w_mix_in", "w_mix_out", "ffn1_w_in", "ffn1_w_out"], half1, other1):
        update(n, own, recv)

    return (loss, grad_x[None], *[out_g[n] for n in names], *[out_d[n] for n in names],
            *[out_m[n] for n in names], *[out_v[n] for n in names])
```

```python
import functools
import math

import jax
import jax.numpy as jnp
from jax import lax
from jax.experimental import pallas as pl
from jax.experimental.pallas import tpu as pltpu

F32 = jnp.float32
BF16 = jnp.bfloat16
MESH = pl.DeviceIdType.MESH
HIGHEST = lax.Precision.HIGHEST

EPS = 1e-6
D = 1024
DFF = 2816
CH = DFF // 2
NQ = 4
DP = 512
DG = 512
DPROJ = DP + 2 * DG
POOL_WINDOWS = (2, 4, 8, 16)
HALO = 16
CHUNK = 128
LANE = 128
N_DEV = 8

ADAM_LR = 0.001
ADAM_B1 = 0.9
ADAM_B2 = 0.999
ADAM_EPS = 1e-08
ADAM_WD = 0.01
ADAM_STEP = 10

VMEM_LIMIT = 62 * 1024 * 1024

TM_FFN_FWD = 512
TM_FFN_BWD = 512
TM_MIX = 256


def _call(body, **kw):
    return pl.pallas_call(body, interpret=False, **kw)


def _params(sem=None, vmem=None):
    return pltpu.CompilerParams(dimension_semantics=sem, vmem_limit_bytes=vmem)


def _sds(shape, dtype):
    return jax.ShapeDtypeStruct(shape, dtype)


ANY = pl.BlockSpec(memory_space=pl.ANY)
VMEM = pl.BlockSpec(memory_space=pltpu.VMEM)
SMEM = pl.BlockSpec(memory_space=pltpu.SMEM)


def _norm_mod(x, gn, sc, sh):
    r = lax.rsqrt(jnp.mean(x * x, axis=-1, keepdims=True) + EPS)
    xn = x * r
    hp = xn * gn
    return r, xn, hp, hp * (1.0 + sc) + sh


def _norm_mod_bwd(dh, r, xn, hp, gn, sc):
    one_sc = 1.0 + sc
    dsh = jnp.sum(dh, axis=0, keepdims=True)
    dsc = jnp.sum(dh * hp, axis=0, keepdims=True)
    dgn = jnp.sum(dh * one_sc * xn, axis=0, keepdims=True)
    dxn = dh * (gn * one_sc)
    dx = r * (dxn - xn * jnp.mean(dxn * xn, axis=-1, keepdims=True))
    return dsh, dsc, dgn, dx


def _dot(a, b):
    return jnp.dot(a, b, preferred_element_type=F32)


def _dot_nt(a, b):
    return lax.dot_general(a, b, (((1,), (1,)), ((), ())), preferred_element_type=F32)


def _dot_tn(a, b):
    return lax.dot_general(a, b, (((0,), (0,)), ((), ())), preferred_element_type=F32)


_GELU_C = math.sqrt(2.0 / math.pi)
_GELU_A = 0.044715


def _gelu_fwd_bwd(x):
    x2 = x * x
    t = jnp.tanh(_GELU_C * (x + _GELU_A * x * x2))
    g = 0.5 * x * (1.0 + t)
    dg = 0.5 * (1.0 + t) + 0.5 * x * (1.0 - t * t) * (_GELU_C * (1.0 + 3.0 * _GELU_A * x2))
    return g, dg


def _adamw(w, g, m, v):
    m = ADAM_B1 * m + (1.0 - ADAM_B1) * g
    v = ADAM_B2 * v + (1.0 - ADAM_B2) * (g * g)
    m_hat = m / (1.0 - ADAM_B1 ** ADAM_STEP)
    v_hat = v / (1.0 - ADAM_B2 ** ADAM_STEP)
    delta = -ADAM_LR * (m_hat / (jnp.sqrt(v_hat) + ADAM_EPS) + ADAM_WD * w)
    return delta, m, v


def _row_block(rows, cap=256, mult=16):
    best = None
    for t in range(mult, min(rows, cap) + 1, mult):
        if rows % t == 0:
            best = t
    assert best is not None, rows
    return best


def _head_math(x, target, gf):
    r = lax.rsqrt(jnp.mean(x * x, axis=-1, keepdims=True) + EPS)
    xn = x * r
    err = xn * gf - target
    dy = err * (1.0 / D)
    dxn = dy * gf
    dx = r * (dxn - xn * jnp.mean(dxn * xn, axis=-1, keepdims=True))
    return (0.5 / D) * jnp.sum(err * err), jnp.sum(dy * xn, axis=0, keepdims=True), dx


def _ffn_fwd(x, modv, win, wout, side=None, head=None):
    S = x.shape[0]
    tm = TM_FFN_FWD
    nt = S // tm

    def body(*refs):
        if head is None:
            x_ref, mod_ref, wg_ref, wu_ref, wo_ref, xo_ref, gs_ref, us_ref, acc_scr = refs
        else:
            (x_ref, mod_ref, wg_ref, wu_ref, wo_ref, t_ref, gf_ref,
             xo_ref, gs_ref, us_ref, loss_ref, dgf_ref, acc_scr) = refs

        @pl.when((pl.program_id(0) == 0) & (pl.program_id(1) == 0))
        def _():
            acc_scr[...] = jnp.zeros_like(acc_scr)
            if head is not None:
                loss_ref[...] = jnp.zeros_like(loss_ref)
                dgf_ref[...] = jnp.zeros_like(dgf_ref)

        j = pl.program_id(1)
        h = _norm_mod(x_ref[...], mod_ref[3:4, :], mod_ref[1:2, :], mod_ref[0:1, :])[3].astype(BF16)
        g = _dot(h, wg_ref[...]).astype(BF16)
        u = _dot(h, wu_ref[...]).astype(BF16)
        gs_ref[...] = g
        us_ref[...] = u
        gf = g.astype(F32)
        a = (gf * jax.nn.sigmoid(gf) * u.astype(F32)).astype(BF16)
        acc = jnp.where(j == 0, 0.0, acc_scr[...]) + _dot(a, wo_ref[...])
        acc_scr[...] = acc
        xo = x_ref[...] + (0.5 * mod_ref[2:3, :]) * acc
        if head is None:
            xo_ref[...] = xo
        else:
            @pl.when(j == 1)
            def _():
                loss, dgf, dx = _head_math(xo, t_ref[...], gf_ref[0:1, :])
                loss_ref[...] += loss
                dgf_ref[0:1, :] += dgf
                xo_ref[...] = dx

    step = lambda i, j: lambda: (pl.program_id(0) == i) & (pl.program_id(1) == j)
    tile = pl.BlockSpec((tm, D), lambda i, j: (i, 0))
    const = lambda shape: pl.BlockSpec(shape, lambda i, j: (0, 0))
    chunk = pl.BlockSpec((tm, CH), lambda i, j: (i, j))
    in_specs = [tile, const((8, D)), pl.BlockSpec((None, D, CH), lambda i, j: (j, 0, 0)),
                pl.BlockSpec((None, D, CH), lambda i, j: (2 + j, 0, 0)), pl.BlockSpec((None, CH, D), lambda i, j: (j, 0, 0))]
    out_specs = [tile, chunk, chunk]
    out_shape = [_sds((S, D), F32), _sds((S, DFF), BF16), _sds((S, DFF), BF16)]
    args = (x, modv, win, win, wout)
    if head is not None:
        in_specs += [tile, const((8, D))]
        out_specs += [const((8, LANE)), const((8, D))]
        out_shape += [_sds((8, LANE), F32), _sds((8, D), F32)]
        args += tuple(head)
    return _side_call(
        body, side, (step(0, 0), step((7 * nt) // 10, 0), step(nt - 1, 1)), name="ffn_fwd",
        grid=(nt, 2), in_specs=in_specs, out_specs=out_specs, out_shape=out_shape,
        scratch_shapes=[pltpu.VMEM((tm, D), F32)],
        compiler_params=_params(("arbitrary", "arbitrary"), VMEM_LIMIT),
        args=args)


def _ffn_bwd_pass(jj, x, dxo, gs, us, modv, win, wout, prev=None, side=None):
    S = x.shape[0]
    tm = TM_FFN_BWD
    nsub = tm // 256
    nt = S // tm
    hi, ho = D // 2, CH // 4
    last = prev is not None
    assert last == (jj == 1)

    def body(*refs):
        x_ref, dxo_ref, gs_ref, us_ref, mod_ref, wg_ref, wu_ref, wo_ref = refs[:8]
        k = 13 if last else 8
        out_ref, dwin_ref, dwout_ref, rwin_ref, rwout_ref, vec_ref = refs[k:k + 6]
        accg, accu, accw, sems, fsend, frecv = refs[k + 6:]
        i = pl.program_id(0)

        @pl.when(i == 0)
        def _():
            accg[...] = jnp.zeros_like(accg)
            accu[...] = jnp.zeros_like(accu)
            accw[...] = jnp.zeros_like(accw)
            vec_ref[...] = jnp.zeros_like(vec_ref)

        gn, sc, sh, gate = mod_ref[3:4, :], mod_ref[1:2, :], mod_ref[0:1, :], mod_ref[2:3, :]

        parts = []
        for s in range(nsub):
            rs = slice(s * (tm // nsub), (s + 1) * (tm // nsub))
            r, xn, hp, h = _norm_mod(x_ref[rs, :], gn, sc, sh)
            dxo = dxo_ref[rs, :]
            dy = (dxo * (0.5 * gate)).astype(BF16)
            g = gs_ref[rs, :].astype(F32)
            u = us_ref[rs, :].astype(F32)
            sig = jax.nn.sigmoid(g)
            sl = g * sig
            a = (sl * u).astype(BF16)
            da = _dot_nt(dy, wo_ref[...])
            dg = (da * u * (sig * (1.0 + g * (1.0 - sig)))).astype(BF16)
            du = (da * sl).astype(BF16)
            dhp = _dot_nt(dg, wg_ref[...]) + _dot_nt(du, wu_ref[...])
            parts.append((h.astype(BF16), a, dg, du, dxo.astype(BF16)))
            if last:
                dsh, dsc, dgn, dxin = _norm_mod_bwd(refs[8][rs, :] + dhp, r, xn, hp, gn, sc)
                vec_ref[0:1, :] += dsh
                vec_ref[1:2, :] += dsc
                vec_ref[3:4, :] += dgn
                out_ref[rs, :] = dxo + dxin
            else:
                out_ref[rs, :] = dhp

        hb, a, dg, du, dxb = [jnp.concatenate(p, axis=0) if nsub > 1 else p[0] for p in zip(*parts)]
        accw[...] += _dot_tn(a, dxb)
        accg[...] += _dot_tn(hb, dg)
        accu[...] += _dot_tn(hb, du)

        @pl.when(i == nt - 1)
        def _():
            gw = accw[...]
            vec_ref[2:3, :] += 0.5 * jnp.sum(wo_ref[...].astype(F32) * gw, axis=0, keepdims=True)
            accw[...] = gw * (0.5 * gate)
            mx, my, cc = _me()
            rows = lambda base, n, c: pl.ds(base + c * n, n)
            pieces = [(accg, 0, hi, dwin_ref, rwin_ref, jj), (accu, 0, hi, dwin_ref, rwin_ref, 2 + jj),
                      (accw, 0, ho, dwout_ref, rwout_ref, 2 * jj), (accw, 2 * ho, ho, dwout_ref, rwout_ref, 2 * jj + 1)]
            loc = [pltpu.make_async_copy(acc.at[rows(base, n, cc)], own.at[slot], sems.at[p])
                   for p, (acc, base, n, own, _, slot) in enumerate(pieces)]
            rem = [pltpu.make_async_remote_copy(acc.at[rows(base, n, 1 - cc)], sib.at[slot], fsend.at[p], frecv.at[p],
                                                device_id=(mx, my, 1 - cc), device_id_type=MESH)
                   for p, (acc, base, n, _, sib, slot) in enumerate(pieces)]
            for cp in loc + rem:
                cp.start()
            for cp in loc:
                cp.wait()
            for cp in rem:
                cp.wait()

    once = pl.Buffered(1)
    tile = pl.BlockSpec((tm, D), lambda i: (i, 0))
    chunk = pl.BlockSpec((tm, CH), lambda i: (i, jj))
    in_specs = [tile, tile, chunk, chunk, pl.BlockSpec((8, D), lambda i: (0, 0)),
                pl.BlockSpec((None, D, CH), lambda i: (jj, 0, 0), pipeline_mode=once),
                pl.BlockSpec((None, D, CH), lambda i: (2 + jj, 0, 0), pipeline_mode=once),
                pl.BlockSpec((None, CH, D), lambda i: (jj, 0, 0), pipeline_mode=once)]
    args = (x, dxo, gs, us, modv, win, win, wout)
    if last:
        in_specs += [tile, ANY, ANY, ANY, ANY]
        args += tuple(prev)
    step = lambda s: lambda: pl.program_id(0) == s
    return _side_call(
        body, side, (step(0), None, step(nt - 1)), name="ffn_bwd",
        grid=(nt,), in_specs=in_specs,
        out_specs=[tile, ANY, ANY, ANY, ANY, pl.BlockSpec((8, D), lambda i: (0, 0))],
        out_shape=[_sds((S, D), F32), _sds((NQ, hi, CH), F32), _sds((NQ, ho, D), F32), _sds((NQ, hi, CH), F32),
                   _sds((NQ, ho, D), F32), _sds((8, D), F32)],
        scratch_shapes=[pltpu.VMEM((D, CH), F32), pltpu.VMEM((D, CH), F32), pltpu.VMEM((CH, D), F32),
                        pltpu.SemaphoreType.DMA((4,)), pltpu.SemaphoreType.DMA((4,)), pltpu.SemaphoreType.DMA((4,))],
        aliases={9 + p: 1 + p for p in range(4)} if last else {},
        compiler_params=_params(("arbitrary",), VMEM_LIMIT),
        args=args)


def _ffn_bwd(x, dxo, gs, us, modv, win, wout, side=None):
    first, extra = _ffn_bwd_pass(0, x, dxo, gs, us, modv, win, wout, side=side)
    (dx, dwin, dwout, rwin, rwout, vec), _ = _ffn_bwd_pass(1, x, dxo, gs, us, modv, win, wout, prev=first[:5])
    return (dx, dwin, dwout, rwin, rwout, first[5] + vec), extra


def _prep_spatial(w_spatial, b_spatial_t):
    def body(w_ref, b_ref, wcat_ref, wtcat_ref, bias_ref):
        row = lax.broadcasted_iota(jnp.int32, (CHUNK, CHUNK), 0)
        col = lax.broadcasted_iota(jnp.int32, (CHUNK, CHUNK), 1)
        tril = col <= row
        for p in range(4):
            wa = jnp.where(tril, w_ref[2 * p], 0.0)
            wb = jnp.where(tril, w_ref[2 * p + 1], 0.0)
            wcat_ref[p] = jnp.concatenate([wa, wb], axis=1).astype(BF16)
            wtcat_ref[p] = jnp.concatenate([wa.T, wb.T], axis=1).astype(BF16)
        head = lax.broadcasted_iota(jnp.int32, (8, DG), 0)
        ch = lax.broadcasted_iota(jnp.int32, (8, DG), 1)
        spread = jnp.where(ch // 64 == head, 1.0, 0.0).astype(F32)
        bias_ref[...] = jnp.dot(b_ref[...], spread, precision=HIGHEST, preferred_element_type=F32)

    return _call(
        body, name="prep_spatial",
        in_specs=[VMEM, VMEM], out_specs=[VMEM, VMEM, VMEM],
        out_shape=[_sds((4, CHUNK, 2 * CHUNK), BF16), _sds((4, CHUNK, 2 * CHUNK), BF16), _sds((CHUNK, DG), F32)],
    )(w_spatial, b_spatial_t)


def _pair_rhs(blocks):
    lane = lax.broadcasted_iota(jnp.int32, (CHUNK, LANE), 1)
    lo = lane < 64
    top = jnp.concatenate([jnp.where(lo, b, 0.0) for b in blocks], axis=1)
    bot = jnp.concatenate([jnp.where(lo, 0.0, b) for b in blocks], axis=1)
    return top, bot


def _gmlp_branch(zb, vecs, wcat_ref, bias_ref, nchunks):
    z, dz = _gelu_fwd_bwd(zb)
    u = z[:, :DG]
    v = z[:, DG:]
    ln_g, ln_b = vecs[1:2, :], vecs[2:3, :]
    mu = jnp.mean(v, axis=-1, keepdims=True)
    vc = v - mu
    rstd = lax.rsqrt(jnp.mean(vc * vc, axis=-1, keepdims=True) + EPS)
    vhat = vc * rstd
    vl = vhat * ln_g + ln_b
    sv_cols = []
    for p in range(4):
        blocks = [vl[k * CHUNK:(k + 1) * CHUNK, p * LANE:(p + 1) * LANE] for k in range(nchunks)]
        top, bot = _pair_rhs(blocks)
        rhs = jnp.concatenate([top, bot], axis=0).astype(BF16)
        out = _dot(wcat_ref[p], rhs)
        bias = bias_ref[:, p * LANE:(p + 1) * LANE]
        sv_cols.append(jnp.concatenate([out[:, k * LANE:(k + 1) * LANE] + bias for k in range(nchunks)], axis=0))
    sv = jnp.concatenate(sv_cols, axis=1)
    return dict(u=u, dz=dz, rstd=rstd, vhat=vhat, vl=vl, sv=sv, yb=u * sv)


def _mix_fwd(x, modv, win, wpool, vecs, wcat, bias, wout):
    S = x.shape[0]
    tm = TM_MIX
    nt = S // tm
    nchunks = tm // CHUNK

    def body(x_ref, mod_ref, win_ref, wpool_ref, vec_ref, wcat_ref, bias_ref, wout_ref,
             xo_ref, pooled_ref, zb_ref, ext):
        i = pl.program_id(0)

        @pl.when(i == 0)
        def _():
            ext[0:HALO, :] = jnp.zeros((HALO, DP), F32)

        x = x_ref[...]
        _, _, _, h = _norm_mod(x, mod_ref[3:4, :], mod_ref[1:2, :], mod_ref[0:1, :])
        proj = _dot(h.astype(BF16), win_ref[...])
        xa = proj[:, :DP]
        zb = proj[:, DP:]
        zb_ref[...] = zb
        ext[HALO:HALO + tm, :] = xa
        pos = i * tm + lax.broadcasted_iota(jnp.int32, (tm, 1), 0)
        vecs = vec_ref[...]
        ya_cols = []
        pooled_cols = []
        for gi, w in enumerate(POOL_WINDOWS):
            cols = slice(gi * LANE, (gi + 1) * LANE)
            s = xa[:, cols]
            for k in range(1, w):
                s = s + ext[HALO - k:HALO - k + tm, cols]
            cnt = jnp.minimum(pos + 1, w).astype(F32)
            pooled = (s / cnt - xa[:, cols]).astype(BF16)
            pooled_cols.append(pooled)
            ya_cols.append(_dot(pooled, wpool_ref[gi]) * vecs[0:1, cols])
        pooled_ref[...] = jnp.concatenate(pooled_cols, axis=1)
        ext[0:HALO, :] = ext[tm:tm + HALO, :]

        gm = _gmlp_branch(zb, vecs, wcat_ref, bias_ref, nchunks)
        cat = jnp.concatenate(ya_cols + [gm["yb"]], axis=1).astype(BF16)
        xo_ref[...] = x + mod_ref[2:3, :] * _dot(cat, wout_ref[...])

    full = lambda shape: pl.BlockSpec(shape, lambda i: (0,) * len(shape))
    return _call(
        body, name="mix_fwd",
        grid=(nt,),
        in_specs=[pl.BlockSpec((tm, D), lambda i: (i, 0)), full((8, D)), full((D, DPROJ)),
                  full((4, LANE, LANE)), full((8, DP)), full((4, CHUNK, 2 * CHUNK)), full((CHUNK, DG)),
                  full((DP + DG, D))],
        out_specs=[pl.BlockSpec((tm, D), lambda i: (i, 0)), pl.BlockSpec((tm, DP), lambda i: (i, 0)),
                   pl.BlockSpec((tm, 2 * DG), lambda i: (i, 0))],
        out_shape=[_sds((S, D), F32), _sds((S, DP), BF16), _sds((S, 2 * DG), F32)],
        scratch_shapes=[pltpu.VMEM((tm + HALO, DP), F32)],
        compiler_params=_params(("arbitrary",), VMEM_LIMIT),
    )(x, modv, win, wpool, vecs, wcat, bias, wout)


def _mix_bwd(x, dxo, pooled, zb, modv, win, wpool, vecs, wcat, wtcat, bias, wout, side=None):
    S = x.shape[0]
    tm = TM_MIX
    nt = S // tm
    nchunks = tm // CHUNK

    def body(x_ref, dxo_ref, pooled_ref, zb_ref, mod_ref, win_ref, wpool_ref, vec_ref, wcat_ref, wtcat_ref,
             bias_ref, wout_ref,
             dx_ref, dwin_ref, dwout_ref, dwpool_ref, dwsp_ref, dbsp_ref, v512_ref, vd_ref, qext, dsv_acc):
        step = pl.program_id(0)
        tile = nt - 1 - step

        @pl.when(step == 0)
        def _():
            dwin_ref[...] = jnp.zeros_like(dwin_ref)
            dwout_ref[...] = jnp.zeros_like(dwout_ref)
            dwpool_ref[...] = jnp.zeros_like(dwpool_ref)
            dwsp_ref[...] = jnp.zeros_like(dwsp_ref)
            v512_ref[...] = jnp.zeros_like(v512_ref)
            vd_ref[...] = jnp.zeros_like(vd_ref)
            dsv_acc[...] = jnp.zeros_like(dsv_acc)
            qext[tm:tm + HALO, :] = jnp.zeros((HALO, DP), F32)

        gn, sc, sh, gate = mod_ref[3:4, :], mod_ref[1:2, :], mod_ref[0:1, :], mod_ref[2:3, :]
        vecs = vec_ref[...]
        x = x_ref[...]
        r, xn, hp, h = _norm_mod(x, gn, sc, sh)
        hb = h.astype(BF16)
        dxo = dxo_ref[...]

        pooled = pooled_ref[...]
        mixed_cols = [_dot(pooled[:, gi * LANE:(gi + 1) * LANE], wpool_ref[gi]) for gi in range(4)]
        mixed = jnp.concatenate(mixed_cols, axis=1)
        scale = vecs[0:1, :]
        gm = _gmlp_branch(zb_ref[...], vecs, wcat_ref, bias_ref, nchunks)
        cat = jnp.concatenate([mixed * scale, gm["yb"]], axis=1).astype(BF16)

        dwout_ref[...] += _dot_tn(cat, dxo.astype(BF16))
        dcat = _dot_nt((dxo * gate).astype(BF16), wout_ref[...])
        dya = dcat[:, :DP]
        dyb = dcat[:, DP:]

        v512_ref[0:1, :] += jnp.sum(dya * mixed, axis=0, keepdims=True)
        dmixed = (dya * scale).astype(BF16)
        pos = tile * tm + lax.broadcasted_iota(jnp.int32, (tm, 1), 0)
        dpooled_cols = []
        for gi, w in enumerate(POOL_WINDOWS):
            cols = slice(gi * LANE, (gi + 1) * LANE)
            dp = _dot_nt(dmixed[:, cols], wpool_ref[gi])
            dwpool_ref[gi] += _dot_tn(pooled[:, cols], dmixed[:, cols])
            cnt = jnp.minimum(pos + 1, w).astype(F32)
            qext[0:tm, cols] = dp / cnt
            dpooled_cols.append(dp)
        dxa_cols = []
        for gi, w in enumerate(POOL_WINDOWS):
            cols = slice(gi * LANE, (gi + 1) * LANE)
            s = qext[0:tm, cols]
            for k in range(1, w):
                s = s + qext[k:k + tm, cols]
            dxa_cols.append(s - dpooled_cols[gi])
        qext[tm:tm + HALO, :] = qext[0:HALO, :]

        u, sv, vl = gm["u"], gm["sv"], gm["vl"]
        du = dyb * sv
        dsv = dyb * u
        dvl_cols = []
        for p in range(4):
            cols = slice(p * LANE, (p + 1) * LANE)
            dblocks = [dsv[k * CHUNK:(k + 1) * CHUNK, cols] for k in range(nchunks)]
            vblocks = [vl[k * CHUNK:(k + 1) * CHUNK, cols] for k in range(nchunks)]
            tot = dblocks[0]
            for b in dblocks[1:]:
                tot = tot + b
            dsv_acc[:, cols] += tot
            top, bot = _pair_rhs(dblocks)
            out = _dot(wtcat_ref[p], jnp.concatenate([top, bot], axis=0).astype(BF16))
            dvl_cols.append(jnp.concatenate([out[:, k * LANE:(k + 1) * LANE] for k in range(nchunks)], axis=0))
            vcat = jnp.concatenate(vblocks, axis=1).astype(BF16)
            dwsp_ref[2 * p] += _dot_nt(top.astype(BF16), vcat)
            dwsp_ref[2 * p + 1] += _dot_nt(bot.astype(BF16), vcat)
        dvl = jnp.concatenate(dvl_cols, axis=1)
        vhat, rstd = gm["vhat"], gm["rstd"]
        v512_ref[1:2, :] += jnp.sum(dvl * vhat, axis=0, keepdims=True)
        v512_ref[2:3, :] += jnp.sum(dvl, axis=0, keepdims=True)
        dvh = dvl * vecs[1:2, :]
        dv = rstd * (dvh - jnp.mean(dvh, axis=-1, keepdims=True)
                     - vhat * jnp.mean(dvh * vhat, axis=-1, keepdims=True))
        dzb = jnp.concatenate([du, dv], axis=1) * gm["dz"]

        dproj = jnp.concatenate(dxa_cols + [dzb], axis=1).astype(BF16)
        dwin_ref[...] += _dot_tn(hb, dproj)
        dh = _dot_nt(dproj, win_ref[...])
        dsh, dsc, dgn, dxin = _norm_mod_bwd(dh, r, xn, hp, gn, sc)
        vd_ref[0:1, :] += dsh
        vd_ref[1:2, :] += dsc
        vd_ref[3:4, :] += dgn
        dx_ref[...] = dxo + dxin

        @pl.when(step == nt - 1)
        def _():
            gw = dwout_ref[...]
            vd_ref[2:3, :] += jnp.sum(wout_ref[...].astype(F32) * gw, axis=0, keepdims=True)
            dwout_ref[...] = gw * gate
            row = lax.broadcasted_iota(jnp.int32, (CHUNK, CHUNK), 0)
            col = lax.broadcasted_iota(jnp.int32, (CHUNK, CHUNK), 1)
            for hh in range(8):
                dwsp_ref[hh] = jnp.where(col <= row, dwsp_ref[hh], 0.0)
            head = lax.broadcasted_iota(jnp.int32, (8, DG), 0)
            ch = lax.broadcasted_iota(jnp.int32, (8, DG), 1)
            spread = jnp.where(ch // 64 == head, 1.0, 0.0).astype(F32)
            dbsp_ref[...] = lax.dot_general(spread, dsv_acc[...], (((1,), (1,)), ((), ())),
                                            precision=HIGHEST, preferred_element_type=F32)

    full = lambda shape: pl.BlockSpec(shape, lambda s: (0,) * len(shape))
    rev = lambda cols: pl.BlockSpec((tm, cols), lambda s: (nt - 1 - s, 0))
    step = lambda s: lambda: pl.program_id(0) == s
    return _side_call(
        body, side, (step(0), None, step(nt - 1)), name="mix_bwd",
        grid=(nt,),
        in_specs=[rev(D), rev(D), rev(DP), rev(2 * DG), full((8, D)), full((D, DPROJ)), full((4, LANE, LANE)),
                  full((8, DP)), full((4, CHUNK, 2 * CHUNK)), full((4, CHUNK, 2 * CHUNK)), full((CHUNK, DG)),
                  full((DP + DG, D))],
        out_specs=[rev(D), full((D, DPROJ)), full((DP + DG, D)), full((4, LANE, LANE)), full((8, CHUNK, CHUNK)),
                   full((8, CHUNK)), full((8, DP)), full((8, D))],
        out_shape=[_sds((S, D), F32), _sds((D, DPROJ), F32), _sds((DP + DG, D), F32), _sds((4, LANE, LANE), F32),
                   _sds((8, CHUNK, CHUNK), F32), _sds((8, CHUNK), F32), _sds((8, DP), F32), _sds((8, D), F32)],
        scratch_shapes=[pltpu.VMEM((tm + HALO, DP), F32), pltpu.VMEM((CHUNK, DG), F32)],
        compiler_params=_params(("arbitrary",), VMEM_LIMIT),
        args=(x, dxo, pooled, zb, modv, win, wpool, vecs, wcat, wtcat, bias, wout))


def _chip_sum(g, rbuf, core):
    _, _, hr, cols = g.shape
    tr = _row_block(hr)

    def body(c_ref, g_ref, r_ref, o_ref):
        o_ref[...] = (g_ref[...] + r_ref[...]).astype(BF16)

    return pl.pallas_call(
        body, name="chip_sum", interpret=False,
        grid_spec=pltpu.PrefetchScalarGridSpec(
            num_scalar_prefetch=1, grid=(NQ, hr // tr),
            in_specs=[pl.BlockSpec((None, None, tr, cols), lambda q, i, c: (q, c[0], i, 0)),
                      pl.BlockSpec((None, tr, cols), lambda q, i, c: (q, i, 0))],
            out_specs=pl.BlockSpec((None, tr, cols), lambda q, i, c: (q, i, 0))),
        out_shape=_sds((NQ, hr, cols), BF16),
        compiler_params=_params(("arbitrary", "arbitrary"), None),
    )(core, g, rbuf)


def _chip_sum_pair(own, rbuf, slots=(0, 1, 2, 3), prev=None):
    _, hr, cols = own.shape
    tr = _row_block(hr)
    a, b = slots[0], (slots[1] - slots[0] if len(slots) > 1 else 0)
    assert list(slots) == [a + b * k for k in range(len(slots))]

    def body(a_ref, b_ref, *rest):
        rest[-1][...] = (a_ref[...] + b_ref[...]).astype(BF16)

    spec = pl.BlockSpec((None, tr, cols), lambda k, i: (a + b * k, i, 0))
    return _call(
        body, name="chip_sum_pair",
        grid=(len(slots), hr // tr),
        in_specs=[spec, spec] + ([ANY] if prev is not None else []), out_specs=spec,
        out_shape=_sds((NQ, hr, cols), BF16),
        input_output_aliases={2: 0} if prev is not None else {},
        compiler_params=_params(("arbitrary", "arbitrary"), None),
    )(own, rbuf, *([prev] if prev is not None else []))


def _sum4(cs, rbuf, chip, after=()):
    _, hr, cols = rbuf.shape
    tr = _row_block(hr)

    def body(q_ref, c_ref, r1_ref, r2_ref, r3_ref, *rest):
        acc = c_ref[...].astype(F32)
        for r in (r1_ref, r2_ref, r3_ref):
            acc = acc + r[...].astype(F32)
        rest[-1][...] = acc

    slot = lambda k: pl.BlockSpec((None, tr, cols), lambda i, q: ((q[0] + k) % NQ, i, 0))
    return pl.pallas_call(
        body, name="sum4", interpret=False,
        grid_spec=pltpu.PrefetchScalarGridSpec(
            num_scalar_prefetch=1, grid=(hr // tr,),
            in_specs=[slot(0), slot(1), slot(2), slot(3)] + [ANY] * len(after),
            out_specs=pl.BlockSpec((tr, cols), lambda i, q: (i, 0))),
        out_shape=_sds((hr, cols), F32),
        compiler_params=_params(("arbitrary",), None),
    )(chip, cs, rbuf, rbuf, rbuf, *after)


def _adamw_halves(w, own, recv, m, v, side=None, after=()):
    rows, cols = w.shape
    hr = rows // 2
    tr = _row_block(hr, mult=8)
    nb = hr // tr

    def body(w_ref, own_ref, recv_ref, m_ref, v_ref, *rest):
        g_ref, d_ref, mo_ref, vo_ref = rest[len(after):]
        g = jnp.where(pl.program_id(0) == lax.axis_index("c"), own_ref[...], recv_ref[...])
        d, mn, vn = _adamw(w_ref[...], g, m_ref[...], v_ref[...])
        g_ref[...] = g
        d_ref[...] = d
        mo_ref[...] = mn
        vo_ref[...] = vn

    full = pl.BlockSpec((tr, cols), lambda h, i: (h * nb + i, 0))
    half = pl.BlockSpec((tr, cols), lambda h, i: (i, 0))
    step = lambda h, i: lambda: (pl.program_id(0) == h) & (pl.program_id(1) == i)
    return _side_call(
        body, side, (step(0, 0), None, step(1, nb - 1)), name="adamw_halves",
        grid=(2, nb), in_specs=[full, half, half, full, full] + [ANY] * len(after), out_specs=[full] * 4,
        out_shape=[_sds((rows, cols), F32)] * 4, scratch_shapes=[],
        compiler_params=_params(("arbitrary", "arbitrary"), None),
        args=(w, own, recv, m, v, *after))


def _cast_place(w, chip):
    rows, cols = w.shape
    tr = _row_block(rows)

    def body(q_ref, w_ref, o_ref):
        o_ref[...] = w_ref[...].astype(BF16)

    return pl.pallas_call(
        body, name="cast_place", interpret=False,
        grid_spec=pltpu.PrefetchScalarGridSpec(
            num_scalar_prefetch=1, grid=(rows // tr,),
            in_specs=[pl.BlockSpec((tr, cols), lambda i, q: (i, 0))],
            out_specs=pl.BlockSpec((None, tr, cols), lambda i, q: (q[0], i, 0))),
        out_shape=_sds((NQ, rows, cols), BF16),
        compiler_params=_params(("arbitrary",), None),
    )(chip, w)


def _ada_grad_adamw(cact_t, dmod_q, w, m, v, side=None):
    rows, cols = w.shape
    tc = 256
    assert cols % tc == 0

    def body(c_ref, d_ref, w_ref, m_ref, v_ref, g_ref, dl_ref, mo_ref, vo_ref):
        g = jnp.dot(c_ref[...], d_ref[...], precision=HIGHEST, preferred_element_type=F32)
        d, mn, vn = _adamw(w_ref[...], g, m_ref[...], v_ref[...])
        g_ref[...] = g
        dl_ref[...] = d
        mo_ref[...] = mn
        vo_ref[...] = vn

    spec = pl.BlockSpec((rows, tc), lambda i: (0, i))
    step = lambda s: lambda: pl.program_id(0) == s
    return _side_call(
        body, side, (step(0), None, step(cols // tc - 1)), name="ada_grad_adamw",
        grid=(cols // tc,),
        in_specs=[pl.BlockSpec((rows, 8), lambda i: (0, 0)), pl.BlockSpec((8, tc), lambda i: (0, i)),
                  spec, spec, spec],
        out_specs=[spec] * 4,
        out_shape=[_sds((rows, cols), F32)] * 4,
        scratch_shapes=[],
        compiler_params=_params(("arbitrary",), None),
        args=(cact_t, dmod_q, w, m, v))


def _me():
    x, y, c = lax.axis_index("x"), lax.axis_index("y"), lax.axis_index("c")
    return x, y, c


_OFFSETS7 = [(dx, dy, dc) for dx in (0, 1) for dy in (0, 1) for dc in (0, 1) if (dx, dy, dc) != (0, 0, 0)]
_CHIP_OFFSETS = [(1, 0), (0, 1), (1, 1)]


def _ada_fwd(c, w_ada_q, b_ada_q, side=None):
    ncol = w_ada_q.shape[1]

    def body(c_ref, w_ref, b_ref, cact_ref, modsel_ref, blk, gath, res, parts, send_sems, recv_sems, side_start=None):
        x, y, cc = _me()
        me = 4 * x + 2 * y + cc
        q = 2 * x + y
        cv = c_ref[...]
        ca = cv * jax.nn.sigmoid(cv)
        row = lax.broadcasted_iota(jnp.int32, (8, D), 0)
        blk[...] = jnp.where(row == me, jnp.broadcast_to(ca, (8, D)), 0.0)
        gath[me] = blk[...]
        sends = []
        for k, (dx, dy, dc) in enumerate(_OFFSETS7):
            cp = pltpu.make_async_remote_copy(blk, gath.at[me], send_sems.at[k], recv_sems.at[k],
                                              device_id=(x ^ dx, y ^ dy, cc ^ dc), device_id_type=MESH)
            cp.start()
            sends.append(cp)
        if side_start is not None:
            side_start()
        for cp in sends:
            cp.wait_recv()
        cact = gath[0]
        for d in range(1, N_DEV):
            cact = cact + gath[d]
        cact_ref[...] = cact
        res[...] = jnp.dot(cact, w_ref[...], precision=HIGHEST, preferred_element_type=F32) + b_ref[...]
        parts[q] = res[...]
        sends2 = []
        for k, (dx, dy) in enumerate(_CHIP_OFFSETS):
            cp = pltpu.make_async_remote_copy(res, parts.at[q], send_sems.at[7 + k], recv_sems.at[7 + k],
                                              device_id=(x ^ dx, y ^ dy, cc), device_id_type=MESH)
            cp.start()
            sends2.append(cp)
        for cp in sends2:
            cp.wait_recv()
        row2 = lax.broadcasted_iota(jnp.int32, (8, ncol), 0)
        out = jnp.zeros((8, ncol), F32)
        for s in range(NQ):
            mine = jnp.sum(jnp.where(row2 == me, parts[s], 0.0), axis=0, keepdims=True)
            out = out + jnp.where(row2 == s, jnp.broadcast_to(mine, (8, ncol)), 0.0)
        modsel_ref[...] = out
        for cp in sends + sends2:
            cp.wait_send()

    return _side_call(
        body, side, None, name="ada_fwd",
        in_specs=[VMEM, VMEM, VMEM], out_specs=[VMEM, VMEM],
        out_shape=[_sds((8, D), F32), _sds((8, ncol), F32)],
        scratch_shapes=[pltpu.VMEM((8, D), F32), pltpu.VMEM((N_DEV, 8, D), F32), pltpu.VMEM((8, ncol), F32),
                        pltpu.VMEM((NQ, 8, ncol), F32), pltpu.SemaphoreType.DMA((10,)), pltpu.SemaphoreType.DMA((10,))],
        compiler_params=_params(None, VMEM_LIMIT), start_in_body=side is not None,
        args=(c, w_ada_q, b_ada_q))


class _Side:
    def __init__(self, ins, out_shapes, aliases, nsem, start, mid=None, finish=None):
        self.ins, self.out_shapes, self.aliases, self.nsem = list(ins), list(out_shapes), dict(aliases), nsem
        self.start, self.mid, self.finish = start, mid, finish


def _join(*sides):
    ins, outs, aliases, offs, nsem = [], [], {}, [], 0
    for s in sides:
        offs.append((len(ins), len(outs), nsem))
        aliases.update({len(ins) + a: len(outs) + b for a, b in s.aliases.items()})
        ins += s.ins
        outs += s.out_shapes
        nsem += s.nsem

    def hook(name):
        def run(i, o, ss, rs, base):
            for s, (io, oo, so) in zip(sides, offs):
                fn = getattr(s, name)
                if fn is not None:
                    fn(i[io:io + len(s.ins)], o[oo:oo + len(s.out_shapes)], ss, rs, base + so)
        return run

    return _Side(ins, outs, aliases, nsem, hook("start"), hook("mid"), hook("finish"))


def _side_call(body, side, when, *, name, in_specs, out_specs, out_shape, scratch_shapes, args, aliases=None,
               start_in_body=False, **kw):
    n_in, n_out = len(in_specs), len(out_specs)
    aliases = dict(aliases or {})
    if side is None:
        return _call(body, name=name, in_specs=in_specs, out_specs=out_specs, out_shape=out_shape,
                     scratch_shapes=scratch_shapes, input_output_aliases=aliases, **kw)(*args), []
    ns_in, ns_out = len(side.ins), len(side.out_shapes)

    def hook(fn, k, operands):
        if fn is None:
            return
        if when is None:
            fn(*operands, 0)
        elif when[k] is not None:
            pl.when(when[k]())(functools.partial(fn, *operands, 0))

    def wrapped(*refs):
        ins, s_ins = refs[:n_in], refs[n_in:n_in + ns_in]
        o0 = n_in + ns_in
        outs, s_outs = refs[o0:o0 + n_out], refs[o0 + n_out:o0 + n_out + ns_out]
        rest = refs[o0 + n_out + ns_out:]
        scratch, operands = rest[:-2], (s_ins, s_outs, rest[-2], rest[-1])
        if start_in_body:
            body(*ins, *outs, *scratch, side_start=functools.partial(hook, side.start, 0, operands))
        else:
            hook(side.start, 0, operands)
            body(*ins, *outs, *scratch)
        hook(side.mid, 1, operands)
        hook(side.finish, 2, operands)

    res = _call(
        wrapped, name=name,
        in_specs=list(in_specs) + [ANY] * ns_in, out_specs=list(out_specs) + [ANY] * ns_out,
        out_shape=list(out_shape) + side.out_shapes,
        scratch_shapes=list(scratch_shapes) + [pltpu.SemaphoreType.DMA((side.nsem,)),
                                               pltpu.SemaphoreType.DMA((side.nsem,))],
        input_output_aliases={**aliases, **{n_in + a: n_out + b for a, b in side.aliases.items()}},
        **kw)(*args, *side.ins)
    return res[:n_out], res[n_out:]


def _run_side(side, name):
    return _side_call(lambda: None, side, None, name=name, in_specs=[], out_specs=[], out_shape=[],
                      scratch_shapes=[], args=[])[1]


def _remote(src, dst, ss, rs, k, dev):
    return pltpu.make_async_remote_copy(src, dst, ss.at[k], rs.at[k], device_id=dev, device_id_type=MESH)


def _gather_side(bufs):
    n = len(bufs)

    def plan(outs, w):
        x, y, cc = _me()
        hr = outs[w].shape[1] // 2
        mine, other = cc * hr, (1 - cc) * hr
        qx, qy, qd, q = 2 * (x ^ 1) + y, 2 * x + (y ^ 1), 2 * (x ^ 1) + (y ^ 1), 2 * x + y
        xn, yn, sib = (x ^ 1, y, cc), (x, y ^ 1, cc), (x, y, 1 - cc)
        at = lambda slot, r0, nr: outs[w].at[slot, pl.ds(r0, nr)]
        send = [(at(q, mine, hr), xn), (at(q, mine, hr), yn),
                (at(qx, mine, hr // 2), yn), (at(qy, mine + hr // 2, hr // 2), xn),
                (at(qx, mine, hr), sib), (at(qy, mine, hr), sib), (at(qd, mine, hr), sib)]
        recv = [at(qx, mine, hr), at(qy, mine, hr), at(qd, mine, hr // 2), at(qd, mine + hr // 2, hr // 2),
                at(qx, other, hr), at(qy, other, hr), at(qd, other, hr)]
        return send, recv

    def op(outs, ss, rs, b, w, k, what):
        send, recv = plan(outs, w)
        if what == "wait_recv":
            _remote(recv[k], recv[k], ss, rs, b + 7 * w + k, send[k][1]).wait_recv()
        else:
            getattr(_remote(send[k][0], send[k][0], ss, rs, b + 7 * w + k, send[k][1]), what)()

    def start(ins, outs, ss, rs, b):
        for w in range(n):
            for k in (0, 1):
                op(outs, ss, rs, b, w, k, "start")

    def mid(ins, outs, ss, rs, b):
        for w in range(n):
            for k in (0, 1):
                op(outs, ss, rs, b, w, k, "wait_recv")
                op(outs, ss, rs, b, w, 2 + k, "start")
                op(outs, ss, rs, b, w, 4 + k, "start")

    def finish(ins, outs, ss, rs, b):
        for w in range(n):
            for k in (2, 3):
                op(outs, ss, rs, b, w, k, "wait_recv")
            op(outs, ss, rs, b, w, 6, "start")
        for w in range(n):
            for k in (4, 5, 6):
                op(outs, ss, rs, b, w, k, "wait_recv")
            for k in range(7):
                op(outs, ss, rs, b, w, k, "wait_send")

    return _Side(bufs, [_sds(tuple(w.shape), w.dtype) for w in bufs], {i: i for i in range(n)}, 7 * n,
                 start, mid, finish)


def _copies_side(ins, out_shapes, nsem, copies):
    def start(*a):
        for cp in copies(*a):
            cp.start()

    def finish(*a):
        for cp in copies(*a):
            cp.wait()

    return _Side(ins, out_shapes, {}, nsem, start, None, finish)


def _swap_side(gs):
    def copies(ins, outs, ss, rs, b):
        x, y, cc = _me()
        return [_remote(ins[w].at[:, 1 - cc], outs[w], ss, rs, b + w, (x, y, 1 - cc)) for w in range(len(gs))]

    return _copies_side(gs, [_sds((NQ,) + tuple(g.shape[2:]), F32) for g in gs], len(gs), copies)


def _exchange_side(cs, slots=None, prev=None):
    n = len(cs)
    slots = slots or [(0, 1, 2, 3)] * n

    def among(chip, allowed):
        hit = chip == allowed[0]
        for s in allowed[1:]:
            hit = hit | (chip == s)
        return hit

    def each(ins, outs, ss, rs, b, do_send, do_recv):
        x, y, cc = _me()
        q = 2 * x + y
        for w in range(n):
            for j, (dx, dy) in enumerate(_CHIP_OFFSETS):
                pq = 2 * (x ^ dx) + (y ^ dy)
                cp = _remote(ins[w].at[pq], outs[w].at[q], ss, rs, b + 3 * w + j, (x ^ dx, y ^ dy, cc))
                if do_send is not None:
                    pl.when(among(pq, slots[w]))(functools.partial(do_send, cp))
                if do_recv is not None:
                    pl.when(among(q, slots[w]))(functools.partial(do_recv, cp))

    def start(ins, outs, ss, rs, b):
        each(ins, outs, ss, rs, b, lambda cp: cp.start(), None)

    def finish(ins, outs, ss, rs, b):
        each(ins, outs, ss, rs, b, lambda cp: cp.wait_send(), lambda cp: cp.wait_recv())

    ins = list(cs) + (list(prev) if prev is not None else [])
    aliases = {n + w: w for w in range(n)} if prev is not None else {}
    return _Side(ins, [_sds(tuple(c.shape), c.dtype) for c in cs], aliases, 3 * n, start, None, finish)


def _exchange_copies(srcs, lands, send_sems, recv_sems):
    x, y, cc = _me()
    return [pltpu.make_async_remote_copy(srcs[w].at[2 * (x ^ dx) + (y ^ dy)], lands[w].at[2 * x + y],
                                         send_sems.at[3 * w + j], recv_sems.at[3 * w + j],
                                         device_id=(x ^ dx, y ^ dy, cc), device_id_type=MESH)
            for w in range(len(srcs)) for j, (dx, dy) in enumerate(_CHIP_OFFSETS)]


def _exchange_start(cs, after=()):
    n = len(cs)
    hbm, sem = pl.BlockSpec(memory_space=pltpu.HBM), pl.BlockSpec(memory_space=pltpu.SEMAPHORE)
    srcs = [pltpu.with_memory_space_constraint(c, pltpu.HBM) for c in cs]
    lands = [pltpu.with_memory_space_constraint(lax.empty(c.shape, c.dtype), pltpu.HBM) for c in cs]

    def body(*refs):
        sems = 2 * n + len(after)
        for cp in _exchange_copies(refs[:n], refs[n:2 * n], refs[sems], refs[sems + 1]):
            cp.start()
        refs[-1][...] = jnp.zeros_like(refs[-1])

    res = pl.pallas_call(
        body, name="exchange_start", interpret=False,
        out_shape=(pltpu.SemaphoreType.DMA((3 * n,)), pltpu.SemaphoreType.DMA((3 * n,)),
                   *[pltpu.HBM(c.shape, c.dtype) for c in cs], *[pltpu.HBM(c.shape, c.dtype) for c in cs],
                   _sds((8, LANE), F32)),
        in_specs=(hbm,) * (2 * n) + (ANY,) * len(after), out_specs=(sem, sem) + (hbm,) * (2 * n) + (VMEM,),
        input_output_aliases={i: 2 + i for i in range(2 * n)},
        compiler_params=pltpu.CompilerParams(has_side_effects=pltpu.SideEffectType.DATAFLOW_SIDE_EFFECTING),
    )(*srcs, *lands, *after)
    return res[0], res[1], list(res[2:2 + n]), list(res[2 + n:2 + 2 * n]), res[-1]


def _exchange_wait(send_sems, recv_sems, srcs, lands, after):
    n = len(srcs)
    hbm, sem = pl.BlockSpec(memory_space=pltpu.HBM), pl.BlockSpec(memory_space=pltpu.SEMAPHORE)

    def body(*refs):
        for cp in _exchange_copies(refs[:n], refs[n:2 * n], refs[2 * n], refs[2 * n + 1]):
            cp.wait_send()
            cp.wait_recv()

    res = pl.pallas_call(
        body, name="exchange_wait", interpret=False,
        out_shape=[pltpu.HBM(c.shape, c.dtype) for c in srcs + lands],
        in_specs=(hbm,) * (2 * n) + (sem, sem) + (ANY,) * len(after), out_specs=(hbm,) * (2 * n),
        input_output_aliases={i: i for i in range(2 * n)},
        compiler_params=pltpu.CompilerParams(has_side_effects=pltpu.SideEffectType.DATAFLOW_SIDE_EFFECTING),
    )(*srcs, *lands, send_sems, recv_sems, *after)
    return list(res[:n]), list(res[n:])


def _share_side(fs):
    def copies(ins, outs, ss, rs, b):
        x, y, cc = _me()
        return [_remote(ins[w], outs[w], ss, rs, b + w, (x, y, 1 - cc)) for w in range(len(fs))]

    return _copies_side(fs, [_sds(tuple(f.shape), F32) for f in fs], len(fs), copies)


def _small_allreduce_adamw(g, w, m, v, nd):
    rows = g.shape[0]
    nr = rows - nd
    hr = nr // 2
    assert nd % 8 == 0 and hr % 8 == 0

    def body(g_ref, w_ref, m_ref, v_ref, gs_ref, d_ref, mo_ref, vo_ref, gath, sib, csum, slots, tot, ss, rs):
        x, y, cc = _me()
        me = 4 * x + 2 * y + cc
        q = 2 * x + y
        sibling = (x, y, 1 - cc)
        dm = g_ref.at[pl.ds(0, nd)]
        gath[me] = g_ref[0:nd, :]
        to_all = [_remote(dm, gath.at[me], ss, rs, k, (x ^ dx, y ^ dy, cc ^ dc)) for k, (dx, dy, dc) in enumerate(_OFFSETS7)]
        to_sib = _remote(g_ref.at[pl.ds(nd, nr)], sib, ss, rs, 7, sibling)
        for cp in to_all + [to_sib]:
            cp.start()
        to_sib.wait_recv()
        csum[...] = g_ref[nd:, :] + sib[...]
        mine = pl.ds(pl.multiple_of(cc * hr, 8), hr)
        slots[q] = csum[mine, :]
        to_chips = [_remote(csum.at[mine], slots.at[q], ss, rs, 8 + j, (x ^ dx, y ^ dy, cc))
                    for j, (dx, dy) in enumerate(_CHIP_OFFSETS)]
        for cp in to_chips:
            cp.start()
        for cp in to_chips:
            cp.wait_recv()
        tot[mine, :] = (slots[0] + slots[1]) + (slots[2] + slots[3])
        halves = _remote(tot.at[mine], tot.at[mine], ss, rs, 11, sibling)
        halves.start()
        for cp in to_all:
            cp.wait_recv()
        dsum = gath[0]
        for dev in range(1, N_DEV):
            dsum = dsum + gath[dev]
        halves.wait_recv()
        for lo, n, total in ((0, nd, dsum), (nd, nr, tot[...])):
            gs_ref[lo:lo + n, :] = total
            d, mn, vn = _adamw(w_ref[lo:lo + n, :], total, m_ref[lo:lo + n, :], v_ref[lo:lo + n, :])
            d_ref[lo:lo + n, :] = d
            mo_ref[lo:lo + n, :] = mn
            vo_ref[lo:lo + n, :] = vn
        for cp in to_all + [to_sib, halves] + to_chips:
            cp.wait_send()

    return _call(
        body, name="small_allreduce_adamw",
        in_specs=[VMEM] * 4, out_specs=[VMEM] * 5,
        out_shape=[_sds((rows, LANE), F32)] * 4 + [_sds((N_DEV, nd, LANE), F32)],
        scratch_shapes=[pltpu.VMEM((nr, LANE), F32), pltpu.VMEM((nr, LANE), F32), pltpu.VMEM((NQ, hr, LANE), F32),
                        pltpu.VMEM((nr, LANE), F32), pltpu.SemaphoreType.DMA((12,)), pltpu.SemaphoreType.DMA((12,))],
        compiler_params=_params(None, VMEM_LIMIT),
    )(g, w, m, v)


_SMALL = ["b_ada", "norm_ffn1_g", "norm_mix_g", "pool_scale", "gmlp_ln_g", "gmlp_ln_b", "b_spatial",
          "norm_ffn2_g", "norm_final_g", "w_pool", "w_spatial"]


def _pack(parts):
    blocks, layout, r0 = [], {}, 0
    for name in _SMALL:
        a = parts[name]
        n = a.size
        rows = -(-n // LANE)
        rows8 = -(-rows // 8) * 8
        flat = a.reshape(-1).astype(F32)
        if rows8 * LANE != n:
            flat = jnp.concatenate([flat, jnp.zeros((rows8 * LANE - n,), F32)])
        blocks.append(flat.reshape(rows8, LANE))
        layout[name] = (r0, n, a.shape)
        r0 += rows8
    return jnp.concatenate(blocks, axis=0), layout


def _unpack(packed, layout):
    out = {}
    for name, (r0, n, shape) in layout.items():
        rows = -(-n // LANE)
        out[name] = packed[r0:r0 + rows].reshape(-1)[:n].reshape(shape)
    return out


def _modv(mod9, sub, gain):
    rows = jnp.concatenate([mod9[3 * sub:3 * sub + 3], gain.reshape(1, D), jnp.zeros((4, D), F32)], axis=0)
    return rows


_BIG = ["ffn1_w_in", "ffn1_w_out", "w_mix_in", "w_mix_out", "ffn2_w_in", "ffn2_w_out"]


def kernel(x, c, w_ada, b_ada, norm_ffn1_g, ffn1_w_in, ffn1_w_out, norm_mix_g, w_mix_in, w_pool, pool_scale, gmlp_ln_g, gmlp_ln_b, w_spatial, b_spatial, w_mix_out, norm_ffn2_g, ffn2_w_in, ffn2_w_out, norm_final_g, loss_target, m_w_ada, m_b_ada, m_norm_ffn1_g, m_ffn1_w_in, m_ffn1_w_out, m_norm_mix_g, m_w_mix_in, m_w_pool, m_pool_scale, m_gmlp_ln_g, m_gmlp_ln_b, m_w_spatial, m_b_spatial, m_w_mix_out, m_norm_ffn2_g, m_ffn2_w_in, m_ffn2_w_out, m_norm_final_g, v_w_ada, v_b_ada, v_norm_ffn1_g, v_ffn1_w_in, v_ffn1_w_out, v_norm_mix_g, v_w_mix_in, v_w_pool, v_pool_scale, v_gmlp_ln_g, v_gmlp_ln_b, v_w_spatial, v_b_spatial, v_w_mix_out, v_norm_ffn2_g, v_ffn2_w_in, v_ffn2_w_out, v_norm_final_g):
    names = ["w_ada", "b_ada", "norm_ffn1_g", "ffn1_w_in", "ffn1_w_out", "norm_mix_g", "w_mix_in", "w_pool",
             "pool_scale", "gmlp_ln_g", "gmlp_ln_b", "w_spatial", "b_spatial", "w_mix_out", "norm_ffn2_g",
             "ffn2_w_in", "ffn2_w_out", "norm_final_g"]
    W = dict(zip(names, [w_ada, b_ada, norm_ffn1_g, ffn1_w_in, ffn1_w_out, norm_mix_g, w_mix_in, w_pool, pool_scale,
                         gmlp_ln_g, gmlp_ln_b, w_spatial, b_spatial, w_mix_out, norm_ffn2_g, ffn2_w_in, ffn2_w_out,
                         norm_final_g]))
    M = dict(zip(names, [m_w_ada, m_b_ada, m_norm_ffn1_g, m_ffn1_w_in, m_ffn1_w_out, m_norm_mix_g, m_w_mix_in, m_w_pool,
                         m_pool_scale, m_gmlp_ln_g, m_gmlp_ln_b, m_w_spatial, m_b_spatial, m_w_mix_out, m_norm_ffn2_g,
                         m_ffn2_w_in, m_ffn2_w_out, m_norm_final_g]))
    V = dict(zip(names, [v_w_ada, v_b_ada, v_norm_ffn1_g, v_ffn1_w_in, v_ffn1_w_out, v_norm_mix_g, v_w_mix_in, v_w_pool,
                         v_pool_scale, v_gmlp_ln_g, v_gmlp_ln_b, v_w_spatial, v_b_spatial, v_w_mix_out, v_norm_ffn2_g,
                         v_ffn2_w_in, v_ffn2_w_out, v_norm_final_g]))

    xi, yi, ci = _me()
    q = 2 * xi + yi
    core = ci.astype(jnp.int32).reshape(1)

    chip = q.astype(jnp.int32).reshape(1)
    place = lambda n: _cast_place(W[n][0], chip)

    ncol = w_ada.shape[2]
    b_q = lax.dynamic_slice_in_dim(b_ada, q * ncol, ncol, axis=1)
    (cact_all, modsel), (win1, wout1) = _ada_fwd(
        c, w_ada[0], b_q, side=_gather_side([place("ffn1_w_in"), place("ffn1_w_out")]))
    mod9 = modsel[:NQ].reshape(9, D)
    xs, target = x[0], loss_target[0]
    mv1 = _modv(mod9, 0, norm_ffn1_g[0])
    mv2 = _modv(mod9, 1, norm_mix_g[0])
    mv3 = _modv(mod9, 2, norm_ffn2_g[0])
    wcat, wtcat, bias = _prep_spatial(w_spatial[0], b_spatial[0].T)
    wpool = w_pool[0].astype(BF16)
    vecs = jnp.concatenate([pool_scale, gmlp_ln_g, gmlp_ln_b, jnp.zeros((5, DP), F32)], axis=0)
    gf = jnp.concatenate([norm_final_g.reshape(1, D), jnp.zeros((7, D), F32)], axis=0)

    later =["w_mix_in", "w_mix_out", "ffn2_w_in", "ffn2_w_out"]
    (x1, g1s, u1s), got = _ffn_fwd(xs, mv1, win1, wout1.reshape(2, CH, D), side=_gather_side([place(n) for n in later]))
    wmi, wmo, win2, wout2 = got
    wmi = jnp.transpose(wmi, (1, 0, 2)).reshape(D, DPROJ)
    wmo = wmo.reshape(DP + DG, D)
    x2, pooled, zb = _mix_fwd(x1, mv2, wmi, wpool, vecs, wcat, bias, wmo)
    (dx3, g3s, u3s, loss_blk, dgf), _ = _ffn_fwd(x2, mv3, win2, wout2.reshape(2, CH, D), head=(target, gf))

    wo1, wo2 = wout1.reshape(2, CH, D), wout2.reshape(2, CH, D)
    (dx2, oin2, oout2, rin2, rout2, vec3), _ = _ffn_bwd(x2, dx3, g3s, u3s, mv3, win2, wo2)
    cs2 = [_chip_sum_pair(oin2, rin2), _chip_sum_pair(oout2, rout2)]
    (dx1, dwmi, dwmo, dwpool, dwsp, dbsp, v512, vec2), ex2 = _mix_bwd(
        x1, dx2, pooled, zb, mv2, wmi, wpool, vecs, wcat, wtcat, bias, wmo, side=_exchange_side(cs2))
    half2 = [_sum4(cs, e, chip) for cs, e in zip(cs2, ex2)]
    qcols = w_mix_in.shape[2]
    vmix = [jnp.transpose(dwmi.reshape(D, NQ, qcols), (1, 0, 2)).reshape(NQ, 2, D // 2, qcols),
            dwmo.reshape(NQ, 2, (DP + DG) // 8, D)]
    first1, got = _ffn_bwd_pass(0, xs, dx1, g1s, u1s, mv1, win1, wo1,
                                side=_join(_swap_side(vmix), _share_side(half2)))
    sibmix, other2 = got[:2], got[2:]
    cs_mix = [_chip_sum(g, r, core) for g, r in zip(vmix, sibmix)]
    (grad_x, oin1, oout1, rin1, rout1, vec1), ex_mix = _ffn_bwd_pass(
        1, xs, dx1, g1s, u1s, mv1, win1, wo1, prev=first1[:5], side=_exchange_side(cs_mix))
    vec1 = first1[5] + vec1
    cs_ffn1 = [_chip_sum_pair(oin1, rin1), _chip_sum_pair(oout1, rout1)]

    dmod =jnp.concatenate([vec1[0:3], vec2[0:3], vec3[0:3]], axis=0)
    grads = dict(
        b_ada=dmod.reshape(1, 9 * D), norm_ffn1_g=vec1[3:4], norm_mix_g=vec2[3:4], norm_ffn2_g=vec3[3:4],
        pool_scale=v512[0:1], gmlp_ln_g=v512[1:2], gmlp_ln_b=v512[2:3], b_spatial=dbsp[None],
        norm_final_g=dgf[0], w_pool=dwpool[None], w_spatial=dwsp[None])

    gp, layout = _pack({n: grads[n] for n in _SMALL})
    gp = jnp.concatenate([gp, loss_blk, loss_blk], axis=0)
    pad = jnp.zeros((16, LANE), F32)
    wp, mp, vp = [jnp.concatenate([_pack({n: src[n] for n in _SMALL})[0], pad], axis=0) for src in (W, M, V)]
    r0, nb, _ = layout["b_ada"]
    assert r0 == 0
    out_g, out_d, out_m, out_v = {}, {}, {}, {}
    gs, dl, mo, vo, gath = _small_allreduce_adamw(gp, wp, mp, vp, nb // LANE)
    loss = gs[-16, 0]
    for packed, dst in ((gs, out_g), (dl, out_d), (mo, out_m), (vo, out_v)):
        for n, a in _unpack(packed, layout).items():
            dst[n] = a.reshape(W[n].shape)

    def update(n, own, recv, after=()):
        (g2, d, mn, vn), _ = _adamw_halves(W[n][0], own, recv, M[n][0], V[n][0], after=after)
        out_g[n], out_d[n], out_m[n], out_v[n] = g2[None], d[None], mn[None], vn[None]
        return g2

    ssem, rsem, cs_fly, land_fly, token = _exchange_start(cs_ffn1, after=(gath,))
    dmod_q = lax.dynamic_slice_in_dim(gath.reshape(N_DEV, nb), q * ncol, ncol, axis=1) + token[0:8, 0:1]
    (ga, da, ma, va), _ = _ada_grad_adamw(cact_all.T, dmod_q, w_ada[0], m_w_ada[0], v_w_ada[0])
    out_g["w_ada"], out_d["w_ada"], out_m["w_ada"], out_v["w_ada"] = ga[None], da[None], ma[None], va[None]
    done = [ga, update("ffn2_w_in", half2[0], other2[0], after=(token,)),
            update("ffn2_w_out", half2[1], other2[1], after=(token,))]
    half_mix = [_sum4(cs, e, chip, after=(token,)) for cs, e in zip(cs_mix, ex_mix)]
    other_mix = _run_side(_share_side(half_mix), "share_mix")
    done += [update(n, own, recv) for n, own, recv in zip(["w_mix_in", "w_mix_out"], half_mix, other_mix)]
    cs_ffn1, ex_ffn1 = _exchange_wait(ssem, rsem, cs_fly, land_fly, after=done)
    half1 = [_sum4(cs, e, chip) for cs, e in zip(cs_ffn1, ex_ffn1)]
    other1 = _run_side(_share_side(half1), "share_ffn1")
    for n, own, recv in zip(["ffn1_w_in", "ffn1_w_out"], half1, other1):
        update(n, own, recv)

    return (loss, grad_x[None], *[out_g[n] for n in names], *[out_d[n] for n in names],
            *[out_m[n] for n in names], *[out_v[n] for n in names])
```

```python
import functools
import math

import jax
import jax.numpy as jnp
from jax import lax
from jax.experimental import pallas as pl
from jax.experimental.pallas import tpu as pltpu

F32 = jnp.float32
BF16 = jnp.bfloat16
MESH = pl.DeviceIdType.MESH
HIGHEST = lax.Precision.HIGHEST

EPS = 1e-6
D = 1024
DFF = 2816
CH = DFF // 2
NQ = 4
DP = 512
DG = 512
DPROJ = DP + 2 * DG
POOL_WINDOWS = (2, 4, 8, 16)
HALO = 16
CHUNK = 128
LANE = 128
N_DEV = 8

ADAM_LR = 0.001
ADAM_B1 = 0.9
ADAM_B2 = 0.999
ADAM_EPS = 1e-08
ADAM_WD = 0.01
ADAM_STEP = 10

VMEM_LIMIT = 62 * 1024 * 1024

TM_FFN_FWD = 512
TM_FFN_BWD = 512
TM_MIX = 256


def _call(body, **kw):
    return pl.pallas_call(body, interpret=False, **kw)


def _params(sem=None, vmem=None):
    return pltpu.CompilerParams(dimension_semantics=sem, vmem_limit_bytes=vmem)


def _sds(shape, dtype):
    return jax.ShapeDtypeStruct(shape, dtype)


ANY = pl.BlockSpec(memory_space=pl.ANY)
VMEM = pl.BlockSpec(memory_space=pltpu.VMEM)
SMEM = pl.BlockSpec(memory_space=pltpu.SMEM)


def _norm_mod(x, gn, sc, sh):
    r = lax.rsqrt(jnp.mean(x * x, axis=-1, keepdims=True) + EPS)
    xn = x * r
    hp = xn * gn
    return r, xn, hp, hp * (1.0 + sc) + sh


def _norm_mod_bwd(dh, r, xn, hp, gn, sc):
    one_sc = 1.0 + sc
    dsh = jnp.sum(dh, axis=0, keepdims=True)
    dsc = jnp.sum(dh * hp, axis=0, keepdims=True)
    dgn = jnp.sum(dh * one_sc * xn, axis=0, keepdims=True)
    dxn = dh * (gn * one_sc)
    dx = r * (dxn - xn * jnp.mean(dxn * xn, axis=-1, keepdims=True))
    return dsh, dsc, dgn, dx


def _dot(a, b):
    return jnp.dot(a, b, preferred_element_type=F32)


def _dot_nt(a, b):
    return lax.dot_general(a, b, (((1,), (1,)), ((), ())), preferred_element_type=F32)


def _dot_tn(a, b):
    return lax.dot_general(a, b, (((0,), (0,)), ((), ())), preferred_element_type=F32)


_GELU_C = math.sqrt(2.0 / math.pi)
_GELU_A = 0.044715


def _gelu_fwd_bwd(x):
    x2 = x * x
    t = jnp.tanh(_GELU_C * (x + _GELU_A * x * x2))
    g = 0.5 * x * (1.0 + t)
    dg = 0.5 * (1.0 + t) + 0.5 * x * (1.0 - t * t) * (_GELU_C * (1.0 + 3.0 * _GELU_A * x2))
    return g, dg


def _adamw(w, g, m, v):
    m = ADAM_B1 * m + (1.0 - ADAM_B1) * g
    v = ADAM_B2 * v + (1.0 - ADAM_B2) * (g * g)
    m_hat = m / (1.0 - ADAM_B1 ** ADAM_STEP)
    v_hat = v / (1.0 - ADAM_B2 ** ADAM_STEP)
    delta = -ADAM_LR * (m_hat / (jnp.sqrt(v_hat) + ADAM_EPS) + ADAM_WD * w)
    return delta, m, v


def _row_block(rows, cap=256, mult=16):
    best = None
    for t in range(mult, min(rows, cap) + 1, mult):
        if rows % t == 0:
            best = t
    assert best is not None, rows
    return best


def _head_math(x, target, gf):
    r = lax.rsqrt(jnp.mean(x * x, axis=-1, keepdims=True) + EPS)
    xn = x * r
    err = xn * gf - target
    dy = err * (1.0 / D)
    dxn = dy * gf
    dx = r * (dxn - xn * jnp.mean(dxn * xn, axis=-1, keepdims=True))
    return (0.5 / D) * jnp.sum(err * err), jnp.sum(dy * xn, axis=0, keepdims=True), dx


def _ffn_fwd(x, modv, win, wout, side=None, head=None):
    S = x.shape[0]
    tm = TM_FFN_FWD
    nt = S // tm

    def body(*refs):
        if head is None:
            x_ref, mod_ref, wg_ref, wu_ref, wo_ref, xo_ref, gs_ref, us_ref, acc_scr = refs
        else:
            (x_ref, mod_ref, wg_ref, wu_ref, wo_ref, t_ref, gf_ref,
             xo_ref, gs_ref, us_ref, loss_ref, dgf_ref, acc_scr) = refs

        @pl.when((pl.program_id(0) == 0) & (pl.program_id(1) == 0))
        def _():
            acc_scr[...] = jnp.zeros_like(acc_scr)
            if head is not None:
                loss_ref[...] = jnp.zeros_like(loss_ref)
                dgf_ref[...] = jnp.zeros_like(dgf_ref)

        j = pl.program_id(1)
        h = _norm_mod(x_ref[...], mod_ref[3:4, :], mod_ref[1:2, :], mod_ref[0:1, :])[3].astype(BF16)
        g = _dot(h, wg_ref[...]).astype(BF16)
        u = _dot(h, wu_ref[...]).astype(BF16)
        gs_ref[...] = g
        us_ref[...] = u
        gf = g.astype(F32)
        a = (gf * jax.nn.sigmoid(gf) * u.astype(F32)).astype(BF16)
        acc = jnp.where(j == 0, 0.0, acc_scr[...]) + _dot(a, wo_ref[...])
        acc_scr[...] = acc
        xo = x_ref[...] + (0.5 * mod_ref[2:3, :]) * acc
        if head is None:
            xo_ref[...] = xo
        else:
            @pl.when(j == 1)
            def _():
                loss, dgf, dx = _head_math(xo, t_ref[...], gf_ref[0:1, :])
                loss_ref[...] += loss
                dgf_ref[0:1, :] += dgf
                xo_ref[...] = dx

    step = lambda i, j: lambda: (pl.program_id(0) == i) & (pl.program_id(1) == j)
    tile = pl.BlockSpec((tm, D), lambda i, j: (i, 0))
    const = lambda shape: pl.BlockSpec(shape, lambda i, j: (0, 0))
    chunk = pl.BlockSpec((tm, CH), lambda i, j: (i, j))
    in_specs = [tile, const((8, D)), pl.BlockSpec((None, D, CH), lambda i, j: (j, 0, 0)),
                pl.BlockSpec((None, D, CH), lambda i, j: (2 + j, 0, 0)), pl.BlockSpec((None, CH, D), lambda i, j: (j, 0, 0))]
    out_specs = [tile, chunk, chunk]
    out_shape = [_sds((S, D), F32), _sds((S, DFF), BF16), _sds((S, DFF), BF16)]
    args = (x, modv, win, win, wout)
    if head is not None:
        in_specs += [tile, const((8, D))]
        out_specs += [const((8, LANE)), const((8, D))]
        out_shape += [_sds((8, LANE), F32), _sds((8, D), F32)]
        args += tuple(head)
    return _side_call(
        body, side, (step(0, 0), step((7 * nt) // 10, 0), step(nt - 1, 1)), name="ffn_fwd",
        grid=(nt, 2), in_specs=in_specs, out_specs=out_specs, out_shape=out_shape,
        scratch_shapes=[pltpu.VMEM((tm, D), F32)],
        compiler_params=_params(("arbitrary", "arbitrary"), VMEM_LIMIT),
        args=args)


def _ffn_bwd_pass(jj, x, dxo, gs, us, modv, win, wout, prev=None, side=None):
    S = x.shape[0]
    tm = TM_FFN_BWD
    nsub = tm // 256
    nt = S // tm
    hi, ho = D // 2, CH // 4
    last = prev is not None
    assert last == (jj == 1)

    def body(*refs):
        x_ref, dxo_ref, gs_ref, us_ref, mod_ref, wg_ref, wu_ref, wo_ref = refs[:8]
        k = 13 if last else 8
        out_ref, dwin_ref, dwout_ref, rwin_ref, rwout_ref, vec_ref = refs[k:k + 6]
        accgu, accw, sems, fsend, frecv = refs[k + 6:]
        i = pl.program_id(0)

        @pl.when(i == 0)
        def _():
            accgu[...] = jnp.zeros_like(accgu)
            accw[...] = jnp.zeros_like(accw)
            vec_ref[...] = jnp.zeros_like(vec_ref)

        gn, sc, sh, gate = mod_ref[3:4, :], mod_ref[1:2, :], mod_ref[0:1, :], mod_ref[2:3, :]

        parts = []
        for s in range(nsub):
            rs = slice(s * (tm // nsub), (s + 1) * (tm // nsub))
            r, xn, hp, h = _norm_mod(x_ref[rs, :], gn, sc, sh)
            dxo = dxo_ref[rs, :]
            dy = (dxo * (0.5 * gate)).astype(BF16)
            g = gs_ref[rs, :].astype(F32)
            u = us_ref[rs, :].astype(F32)
            sig = jax.nn.sigmoid(g)
            sl = g * sig
            a = (sl * u).astype(BF16)
            da = _dot_nt(dy, wo_ref[...])
            dg = (da * u * (sig * (1.0 + g * (1.0 - sig)))).astype(BF16)
            du = (da * sl).astype(BF16)
            dhp = _dot_nt(dg, wg_ref[...]) + _dot_nt(du, wu_ref[...])
            parts.append((h.astype(BF16), a, dg, du, dxo.astype(BF16)))
            if last:
                dsh, dsc, dgn, dxin = _norm_mod_bwd(refs[8][rs, :] + dhp, r, xn, hp, gn, sc)
                vec_ref[0:1, :] += dsh
                vec_ref[1:2, :] += dsc
                vec_ref[3:4, :] += dgn
                out_ref[rs, :] = dxo + dxin
            else:
                out_ref[rs, :] = dhp

        hb, a, dg, du, dxb = [jnp.concatenate(p, axis=0) if nsub > 1 else p[0] for p in zip(*parts)]
        accw[...] += _dot_tn(a, dxb)
        accgu[...] += _dot_tn(hb, jnp.concatenate([dg, du], axis=1))

        @pl.when(i == nt - 1)
        def _():
            gw = accw[...]
            vec_ref[2:3, :] += 0.5 * jnp.sum(wo_ref[...].astype(F32) * gw, axis=0, keepdims=True)
            accw[...] = gw * (0.5 * gate)
            mx, my, cc = _me()
            part = lambda acc, base, n, c, col: acc.at[pl.ds(base + c * n, n), pl.ds(col[0], col[1])]
            pieces = [(accgu, 0, hi, (0, CH), dwin_ref, rwin_ref, jj), (accgu, 0, hi, (CH, CH), dwin_ref, rwin_ref, 2 + jj),
                      (accw, 0, ho, (0, D), dwout_ref, rwout_ref, 2 * jj),
                      (accw, 2 * ho, ho, (0, D), dwout_ref, rwout_ref, 2 * jj + 1)]
            loc = [pltpu.make_async_copy(part(acc, base, n, cc, col), own.at[slot], sems.at[p])
                   for p, (acc, base, n, col, own, _, slot) in enumerate(pieces)]
            rem = [pltpu.make_async_remote_copy(part(acc, base, n, 1 - cc, col), sib.at[slot], fsend.at[p], frecv.at[p],
                                                device_id=(mx, my, 1 - cc), device_id_type=MESH)
                   for p, (acc, base, n, col, _, sib, slot) in enumerate(pieces)]
            for cp in loc + rem:
                cp.start()
            for cp in loc:
                cp.wait()
            for cp in rem:
                cp.wait()

    once = pl.Buffered(1)
    tile = pl.BlockSpec((tm, D), lambda i: (i, 0))
    chunk = pl.BlockSpec((tm, CH), lambda i: (i, jj))
    in_specs = [tile, tile, chunk, chunk, pl.BlockSpec((8, D), lambda i: (0, 0)),
                pl.BlockSpec((None, D, CH), lambda i: (jj, 0, 0), pipeline_mode=once),
                pl.BlockSpec((None, D, CH), lambda i: (2 + jj, 0, 0), pipeline_mode=once),
                pl.BlockSpec((None, CH, D), lambda i: (jj, 0, 0), pipeline_mode=once)]
    args = (x, dxo, gs, us, modv, win, win, wout)
    if last:
        in_specs += [tile, ANY, ANY, ANY, ANY]
        args += tuple(prev)
    step = lambda s: lambda: pl.program_id(0) == s
    return _side_call(
        body, side, (step(0), None, step(nt - 1)), name="ffn_bwd",
        grid=(nt,), in_specs=in_specs,
        out_specs=[tile, ANY, ANY, ANY, ANY, pl.BlockSpec((8, D), lambda i: (0, 0))],
        out_shape=[_sds((S, D), F32), _sds((NQ, hi, CH), F32), _sds((NQ, ho, D), F32), _sds((NQ, hi, CH), F32),
                   _sds((NQ, ho, D), F32), _sds((8, D), F32)],
        scratch_shapes=[pltpu.VMEM((D, 2 * CH), F32), pltpu.VMEM((CH, D), F32),
                        pltpu.SemaphoreType.DMA((4,)), pltpu.SemaphoreType.DMA((4,)), pltpu.SemaphoreType.DMA((4,))],
        aliases={9 + p: 1 + p for p in range(4)} if last else {},
        compiler_params=_params(("arbitrary",), VMEM_LIMIT),
        args=args)


def _ffn_bwd(x, dxo, gs, us, modv, win, wout, side=None):
    first, extra = _ffn_bwd_pass(0, x, dxo, gs, us, modv, win, wout, side=side)
    (dx, dwin, dwout, rwin, rwout, vec), _ = _ffn_bwd_pass(1, x, dxo, gs, us, modv, win, wout, prev=first[:5])
    return (dx, dwin, dwout, rwin, rwout, first[5] + vec), extra


def _prep_spatial(w_spatial, b_spatial_t):
    def body(w_ref, b_ref, wcat_ref, wtcat_ref, bias_ref):
        row = lax.broadcasted_iota(jnp.int32, (CHUNK, CHUNK), 0)
        col = lax.broadcasted_iota(jnp.int32, (CHUNK, CHUNK), 1)
        tril = col <= row
        for p in range(4):
            wa = jnp.where(tril, w_ref[2 * p], 0.0)
            wb = jnp.where(tril, w_ref[2 * p + 1], 0.0)
            wcat_ref[p] = jnp.concatenate([wa, wb], axis=1).astype(BF16)
            wtcat_ref[p] = jnp.concatenate([wa.T, wb.T], axis=1).astype(BF16)
        head = lax.broadcasted_iota(jnp.int32, (8, DG), 0)
        ch = lax.broadcasted_iota(jnp.int32, (8, DG), 1)
        spread = jnp.where(ch // 64 == head, 1.0, 0.0).astype(F32)
        bias_ref[...] = jnp.dot(b_ref[...], spread, precision=HIGHEST, preferred_element_type=F32)

    return _call(
        body, name="prep_spatial",
        in_specs=[VMEM, VMEM], out_specs=[VMEM, VMEM, VMEM],
        out_shape=[_sds((4, CHUNK, 2 * CHUNK), BF16), _sds((4, CHUNK, 2 * CHUNK), BF16), _sds((CHUNK, DG), F32)],
    )(w_spatial, b_spatial_t)


def _pair_rhs(blocks):
    lane = lax.broadcasted_iota(jnp.int32, (CHUNK, LANE), 1)
    lo = lane < 64
    top = jnp.concatenate([jnp.where(lo, b, 0.0) for b in blocks], axis=1)
    bot = jnp.concatenate([jnp.where(lo, 0.0, b) for b in blocks], axis=1)
    return top, bot


def _gmlp_branch(zb, vecs, wcat_ref, bias_ref, nchunks):
    z, dz = _gelu_fwd_bwd(zb)
    u = z[:, :DG]
    v = z[:, DG:]
    ln_g, ln_b = vecs[1:2, :], vecs[2:3, :]
    mu = jnp.mean(v, axis=-1, keepdims=True)
    vc = v - mu
    rstd = lax.rsqrt(jnp.mean(vc * vc, axis=-1, keepdims=True) + EPS)
    vhat = vc * rstd
    vl = vhat * ln_g + ln_b
    sv_cols = []
    for p in range(4):
        blocks = [vl[k * CHUNK:(k + 1) * CHUNK, p * LANE:(p + 1) * LANE] for k in range(nchunks)]
        top, bot = _pair_rhs(blocks)
        rhs = jnp.concatenate([top, bot], axis=0).astype(BF16)
        out = _dot(wcat_ref[p], rhs)
        bias = bias_ref[:, p * LANE:(p + 1) * LANE]
        sv_cols.append(jnp.concatenate([out[:, k * LANE:(k + 1) * LANE] + bias for k in range(nchunks)], axis=0))
    sv = jnp.concatenate(sv_cols, axis=1)
    return dict(u=u, dz=dz, rstd=rstd, vhat=vhat, vl=vl, sv=sv, yb=u * sv)


def _mix_fwd(x, modv, win, wpool, vecs, wcat, bias, wout):
    S = x.shape[0]
    tm = TM_MIX
    nt = S // tm
    nchunks = tm // CHUNK

    def body(x_ref, mod_ref, win_ref, wpool_ref, vec_ref, wcat_ref, bias_ref, wout_ref,
             xo_ref, pooled_ref, zb_ref, ext):
        i = pl.program_id(0)

        @pl.when(i == 0)
        def _():
            ext[0:HALO, :] = jnp.zeros((HALO, DP), F32)

        x = x_ref[...]
        _, _, _, h = _norm_mod(x, mod_ref[3:4, :], mod_ref[1:2, :], mod_ref[0:1, :])
        proj = _dot(h.astype(BF16), win_ref[...])
        xa = proj[:, :DP]
        zb = proj[:, DP:]
        zb_ref[...] = zb
        ext[HALO:HALO + tm, :] = xa
        pos = i * tm + lax.broadcasted_iota(jnp.int32, (tm, 1), 0)
        vecs = vec_ref[...]
        ya_cols = []
        pooled_cols = []
        for gi, w in enumerate(POOL_WINDOWS):
            cols = slice(gi * LANE, (gi + 1) * LANE)
            s = xa[:, cols]
            for k in range(1, w):
                s = s + ext[HALO - k:HALO - k + tm, cols]
            cnt = jnp.minimum(pos + 1, w).astype(F32)
            pooled = (s / cnt - xa[:, cols]).astype(BF16)
            pooled_cols.append(pooled)
            ya_cols.append(_dot(pooled, wpool_ref[gi]) * vecs[0:1, cols])
        pooled_ref[...] = jnp.concatenate(pooled_cols, axis=1)
        ext[0:HALO, :] = ext[tm:tm + HALO, :]

        gm = _gmlp_branch(zb, vecs, wcat_ref, bias_ref, nchunks)
        cat = jnp.concatenate(ya_cols + [gm["yb"]], axis=1).astype(BF16)
        xo_ref[...] = x + mod_ref[2:3, :] * _dot(cat, wout_ref[...])

    full = lambda shape: pl.BlockSpec(shape, lambda i: (0,) * len(shape))
    return _call(
        body, name="mix_fwd",
        grid=(nt,),
        in_specs=[pl.BlockSpec((tm, D), lambda i: (i, 0)), full((8, D)), full((D, DPROJ)),
                  full((4, LANE, LANE)), full((8, DP)), full((4, CHUNK, 2 * CHUNK)), full((CHUNK, DG)),
                  full((DP + DG, D))],
        out_specs=[pl.BlockSpec((tm, D), lambda i: (i, 0)), pl.BlockSpec((tm, DP), lambda i: (i, 0)),
                   pl.BlockSpec((tm, 2 * DG), lambda i: (i, 0))],
        out_shape=[_sds((S, D), F32), _sds((S, DP), BF16), _sds((S, 2 * DG), F32)],
        scratch_shapes=[pltpu.VMEM((tm + HALO, DP), F32)],
        compiler_params=_params(("arbitrary",), VMEM_LIMIT),
    )(x, modv, win, wpool, vecs, wcat, bias, wout)


def _mix_bwd(x, dxo, pooled, zb, modv, win, wpool, vecs, wcat, wtcat, bias, wout, side=None):
    S = x.shape[0]
    tm = TM_MIX
    nt = S // tm
    nchunks = tm // CHUNK

    def body(x_ref, dxo_ref, pooled_ref, zb_ref, mod_ref, win_ref, wpool_ref, vec_ref, wcat_ref, wtcat_ref,
             bias_ref, wout_ref,
             dx_ref, dwin_ref, dwout_ref, dwpool_ref, dwsp_ref, dbsp_ref, v512_ref, vd_ref, qext, dsv_acc):
        step = pl.program_id(0)
        tile = nt - 1 - step

        @pl.when(step == 0)
        def _():
            dwin_ref[...] = jnp.zeros_like(dwin_ref)
            dwout_ref[...] = jnp.zeros_like(dwout_ref)
            dwpool_ref[...] = jnp.zeros_like(dwpool_ref)
            dwsp_ref[...] = jnp.zeros_like(dwsp_ref)
            v512_ref[...] = jnp.zeros_like(v512_ref)
            vd_ref[...] = jnp.zeros_like(vd_ref)
            dsv_acc[...] = jnp.zeros_like(dsv_acc)
            qext[tm:tm + HALO, :] = jnp.zeros((HALO, DP), F32)

        gn, sc, sh, gate = mod_ref[3:4, :], mod_ref[1:2, :], mod_ref[0:1, :], mod_ref[2:3, :]
        vecs = vec_ref[...]
        x = x_ref[...]
        r, xn, hp, h = _norm_mod(x, gn, sc, sh)
        hb = h.astype(BF16)
        dxo = dxo_ref[...]

        pooled = pooled_ref[...]
        mixed_cols = [_dot(pooled[:, gi * LANE:(gi + 1) * LANE], wpool_ref[gi]) for gi in range(4)]
        mixed = jnp.concatenate(mixed_cols, axis=1)
        scale = vecs[0:1, :]
        gm = _gmlp_branch(zb_ref[...], vecs, wcat_ref, bias_ref, nchunks)
        cat = jnp.concatenate([mixed * scale, gm["yb"]], axis=1).astype(BF16)

        dwout_ref[...] += _dot_tn(cat, dxo.astype(BF16))
        dcat = _dot_nt((dxo * gate).astype(BF16), wout_ref[...])
        dya = dcat[:, :DP]
        dyb = dcat[:, DP:]

        v512_ref[0:1, :] += jnp.sum(dya * mixed, axis=0, keepdims=True)
        dmixed = (dya * scale).astype(BF16)
        pos = tile * tm + lax.broadcasted_iota(jnp.int32, (tm, 1), 0)
        dpooled_cols = []
        for gi, w in enumerate(POOL_WINDOWS):
            cols = slice(gi * LANE, (gi + 1) * LANE)
            dp = _dot_nt(dmixed[:, cols], wpool_ref[gi])
            dwpool_ref[gi] += _dot_tn(pooled[:, cols], dmixed[:, cols])
            cnt = jnp.minimum(pos + 1, w).astype(F32)
            qext[0:tm, cols] = dp / cnt
            dpooled_cols.append(dp)
        dxa_cols = []
        for gi, w in enumerate(POOL_WINDOWS):
            cols = slice(gi * LANE, (gi + 1) * LANE)
            s = qext[0:tm, cols]
            for k in range(1, w):
                s = s + qext[k:k + tm, cols]
            dxa_cols.append(s - dpooled_cols[gi])
        qext[tm:tm + HALO, :] = qext[0:HALO, :]

        u, sv, vl = gm["u"], gm["sv"], gm["vl"]
        du = dyb * sv
        dsv = dyb * u
        dvl_cols = []
        for p in range(4):
            cols = slice(p * LANE, (p + 1) * LANE)
            dblocks = [dsv[k * CHUNK:(k + 1) * CHUNK, cols] for k in range(nchunks)]
            vblocks = [vl[k * CHUNK:(k + 1) * CHUNK, cols] for k in range(nchunks)]
            tot = dblocks[0]
            for b in dblocks[1:]:
                tot = tot + b
            dsv_acc[:, cols] += tot
            top, bot = _pair_rhs(dblocks)
            out = _dot(wtcat_ref[p], jnp.concatenate([top, bot], axis=0).astype(BF16))
            dvl_cols.append(jnp.concatenate([out[:, k * LANE:(k + 1) * LANE] for k in range(nchunks)], axis=0))
            vcat = jnp.concatenate(vblocks, axis=1).astype(BF16)
            dwsp_ref[2 * p] += _dot_nt(top.astype(BF16), vcat)
            dwsp_ref[2 * p + 1] += _dot_nt(bot.astype(BF16), vcat)
        dvl = jnp.concatenate(dvl_cols, axis=1)
        vhat, rstd = gm["vhat"], gm["rstd"]
        v512_ref[1:2, :] += jnp.sum(dvl * vhat, axis=0, keepdims=True)
        v512_ref[2:3, :] += jnp.sum(dvl, axis=0, keepdims=True)
        dvh = dvl * vecs[1:2, :]
        dv = rstd * (dvh - jnp.mean(dvh, axis=-1, keepdims=True)
                     - vhat * jnp.mean(dvh * vhat, axis=-1, keepdims=True))
        dzb = jnp.concatenate([du, dv], axis=1) * gm["dz"]

        dproj = jnp.concatenate(dxa_cols + [dzb], axis=1).astype(BF16)
        dwin_ref[...] += _dot_tn(hb, dproj)
        dh = _dot_nt(dproj, win_ref[...])
        dsh, dsc, dgn, dxin = _norm_mod_bwd(dh, r, xn, hp, gn, sc)
        vd_ref[0:1, :] += dsh
        vd_ref[1:2, :] += dsc
        vd_ref[3:4, :] += dgn
        dx_ref[...] = dxo + dxin

        @pl.when(step == nt - 1)
        def _():
            gw = dwout_ref[...]
            vd_ref[2:3, :] += jnp.sum(wout_ref[...].astype(F32) * gw, axis=0, keepdims=True)
            dwout_ref[...] = gw * gate
            row = lax.broadcasted_iota(jnp.int32, (CHUNK, CHUNK), 0)
            col = lax.broadcasted_iota(jnp.int32, (CHUNK, CHUNK), 1)
            for hh in range(8):
                dwsp_ref[hh] = jnp.where(col <= row, dwsp_ref[hh], 0.0)
            head = lax.broadcasted_iota(jnp.int32, (8, DG), 0)
            ch = lax.broadcasted_iota(jnp.int32, (8, DG), 1)
            spread = jnp.where(ch // 64 == head, 1.0, 0.0).astype(F32)
            dbsp_ref[...] = lax.dot_general(spread, dsv_acc[...], (((1,), (1,)), ((), ())),
                                            precision=HIGHEST, preferred_element_type=F32)

    full = lambda shape: pl.BlockSpec(shape, lambda s: (0,) * len(shape))
    rev = lambda cols: pl.BlockSpec((tm, cols), lambda s: (nt - 1 - s, 0))
    step = lambda s: lambda: pl.program_id(0) == s
    return _side_call(
        body, side, (step(0), None, step(nt - 1)), name="mix_bwd",
        grid=(nt,),
        in_specs=[rev(D), rev(D), rev(DP), rev(2 * DG), full((8, D)), full((D, DPROJ)), full((4, LANE, LANE)),
                  full((8, DP)), full((4, CHUNK, 2 * CHUNK)), full((4, CHUNK, 2 * CHUNK)), full((CHUNK, DG)),
                  full((DP + DG, D))],
        out_specs=[rev(D), full((D, DPROJ)), full((DP + DG, D)), full((4, LANE, LANE)), full((8, CHUNK, CHUNK)),
                   full((8, CHUNK)), full((8, DP)), full((8, D))],
        out_shape=[_sds((S, D), F32), _sds((D, DPROJ), F32), _sds((DP + DG, D), F32), _sds((4, LANE, LANE), F32),
                   _sds((8, CHUNK, CHUNK), F32), _sds((8, CHUNK), F32), _sds((8, DP), F32), _sds((8, D), F32)],
        scratch_shapes=[pltpu.VMEM((tm + HALO, DP), F32), pltpu.VMEM((CHUNK, DG), F32)],
        compiler_params=_params(("arbitrary",), VMEM_LIMIT),
        args=(x, dxo, pooled, zb, modv, win, wpool, vecs, wcat, wtcat, bias, wout))


def _chip_sum(g, rbuf, core):
    _, _, hr, cols = g.shape
    tr = _row_block(hr)

    def body(c_ref, g_ref, r_ref, o_ref):
        o_ref[...] = (g_ref[...] + r_ref[...]).astype(BF16)

    return pl.pallas_call(
        body, name="chip_sum", interpret=False,
        grid_spec=pltpu.PrefetchScalarGridSpec(
            num_scalar_prefetch=1, grid=(NQ, hr // tr),
            in_specs=[pl.BlockSpec((None, None, tr, cols), lambda q, i, c: (q, c[0], i, 0)),
                      pl.BlockSpec((None, tr, cols), lambda q, i, c: (q, i, 0))],
            out_specs=pl.BlockSpec((None, tr, cols), lambda q, i, c: (q, i, 0))),
        out_shape=_sds((NQ, hr, cols), BF16),
        compiler_params=_params(("arbitrary", "arbitrary"), None),
    )(core, g, rbuf)


def _chip_sum_pair(own, rbuf, slots=(0, 1, 2, 3), prev=None):
    _, hr, cols = own.shape
    tr = _row_block(hr)
    a, b = slots[0], (slots[1] - slots[0] if len(slots) > 1 else 0)
    assert list(slots) == [a + b * k for k in range(len(slots))]

    def body(a_ref, b_ref, *rest):
        rest[-1][...] = (a_ref[...] + b_ref[...]).astype(BF16)

    spec = pl.BlockSpec((None, tr, cols), lambda k, i: (a + b * k, i, 0))
    return _call(
        body, name="chip_sum_pair",
        grid=(len(slots), hr // tr),
        in_specs=[spec, spec] + ([ANY] if prev is not None else []), out_specs=spec,
        out_shape=_sds((NQ, hr, cols), BF16),
        input_output_aliases={2: 0} if prev is not None else {},
        compiler_params=_params(("arbitrary", "arbitrary"), None),
    )(own, rbuf, *([prev] if prev is not None else []))


def _sum4(cs, rbuf, chip, after=()):
    _, hr, cols = rbuf.shape
    tr = _row_block(hr)

    def body(q_ref, c_ref, r1_ref, r2_ref, r3_ref, *rest):
        acc = c_ref[...].astype(F32)
        for r in (r1_ref, r2_ref, r3_ref):
            acc = acc + r[...].astype(F32)
        rest[-1][...] = acc

    slot = lambda k: pl.BlockSpec((None, tr, cols), lambda i, q: ((q[0] + k) % NQ, i, 0))
    return pl.pallas_call(
        body, name="sum4", interpret=False,
        grid_spec=pltpu.PrefetchScalarGridSpec(
            num_scalar_prefetch=1, grid=(hr // tr,),
            in_specs=[slot(0), slot(1), slot(2), slot(3)] + [ANY] * len(after),
            out_specs=pl.BlockSpec((tr, cols), lambda i, q: (i, 0))),
        out_shape=_sds((hr, cols), F32),
        compiler_params=_params(("arbitrary",), None),
    )(chip, cs, rbuf, rbuf, rbuf, *after)


def _adamw_halves(w, own, recv, m, v, side=None, after=()):
    rows, cols = w.shape
    hr = rows // 2
    tr = _row_block(hr, mult=8)
    nb = hr // tr

    def body(w_ref, own_ref, recv_ref, m_ref, v_ref, *rest):
        g_ref, d_ref, mo_ref, vo_ref = rest[len(after):]
        g = jnp.where(pl.program_id(0) == lax.axis_index("c"), own_ref[...], recv_ref[...])
        d, mn, vn = _adamw(w_ref[...], g, m_ref[...], v_ref[...])
        g_ref[...] = g
        d_ref[...] = d
        mo_ref[...] = mn
        vo_ref[...] = vn

    full = pl.BlockSpec((tr, cols), lambda h, i: (h * nb + i, 0))
    half = pl.BlockSpec((tr, cols), lambda h, i: (i, 0))
    step = lambda h, i: lambda: (pl.program_id(0) == h) & (pl.program_id(1) == i)
    return _side_call(
        body, side, (step(0, 0), None, step(1, nb - 1)), name="adamw_halves",
        grid=(2, nb), in_specs=[full, half, half, full, full] + [ANY] * len(after), out_specs=[full] * 4,
        out_shape=[_sds((rows, cols), F32)] * 4, scratch_shapes=[],
        compiler_params=_params(("arbitrary", "arbitrary"), None),
        args=(w, own, recv, m, v, *after))


def _cast_place(w, chip):
    rows, cols = w.shape
    tr = _row_block(rows)

    def body(q_ref, w_ref, o_ref):
        o_ref[...] = w_ref[...].astype(BF16)

    return pl.pallas_call(
        body, name="cast_place", interpret=False,
        grid_spec=pltpu.PrefetchScalarGridSpec(
            num_scalar_prefetch=1, grid=(rows // tr,),
            in_specs=[pl.BlockSpec((tr, cols), lambda i, q: (i, 0))],
            out_specs=pl.BlockSpec((None, tr, cols), lambda i, q: (q[0], i, 0))),
        out_shape=_sds((NQ, rows, cols), BF16),
        compiler_params=_params(("arbitrary",), None),
    )(chip, w)


def _ada_grad_adamw(cact_t, dmod_q, w, m, v, side=None):
    rows, cols = w.shape
    tc = 256
    assert cols % tc == 0

    def body(c_ref, d_ref, w_ref, m_ref, v_ref, g_ref, dl_ref, mo_ref, vo_ref):
        g = jnp.dot(c_ref[...], d_ref[...], precision=HIGHEST, preferred_element_type=F32)
        d, mn, vn = _adamw(w_ref[...], g, m_ref[...], v_ref[...])
        g_ref[...] = g
        dl_ref[...] = d
        mo_ref[...] = mn
        vo_ref[...] = vn

    spec = pl.BlockSpec((rows, tc), lambda i: (0, i))
    step = lambda s: lambda: pl.program_id(0) == s
    return _side_call(
        body, side, (step(0), None, step(cols // tc - 1)), name="ada_grad_adamw",
        grid=(cols // tc,),
        in_specs=[pl.BlockSpec((rows, 8), lambda i: (0, 0)), pl.BlockSpec((8, tc), lambda i: (0, i)),
                  spec, spec, spec],
        out_specs=[spec] * 4,
        out_shape=[_sds((rows, cols), F32)] * 4,
        scratch_shapes=[],
        compiler_params=_params(("arbitrary",), None),
        args=(cact_t, dmod_q, w, m, v))


def _me():
    x, y, c = lax.axis_index("x"), lax.axis_index("y"), lax.axis_index("c")
    return x, y, c


_OFFSETS7 = [(dx, dy, dc) for dx in (0, 1) for dy in (0, 1) for dc in (0, 1) if (dx, dy, dc) != (0, 0, 0)]
_CHIP_OFFSETS = [(1, 0), (0, 1), (1, 1)]


def _ada_fwd(c, w_ada_q, b_ada_q, side=None):
    ncol = w_ada_q.shape[1]

    def body(c_ref, w_ref, b_ref, cact_ref, modsel_ref, blk, gath, res, parts, send_sems, recv_sems, side_start=None):
        x, y, cc = _me()
        me = 4 * x + 2 * y + cc
        q = 2 * x + y
        cv = c_ref[...]
        ca = cv * jax.nn.sigmoid(cv)
        row = lax.broadcasted_iota(jnp.int32, (8, D), 0)
        blk[...] = jnp.where(row == me, jnp.broadcast_to(ca, (8, D)), 0.0)
        gath[me] = blk[...]
        sends = []
        for k, (dx, dy, dc) in enumerate(_OFFSETS7):
            cp = pltpu.make_async_remote_copy(blk, gath.at[me], send_sems.at[k], recv_sems.at[k],
                                              device_id=(x ^ dx, y ^ dy, cc ^ dc), device_id_type=MESH)
            cp.start()
            sends.append(cp)
        if side_start is not None:
            side_start()
        for cp in sends:
            cp.wait_recv()
        cact = gath[0]
        for d in range(1, N_DEV):
            cact = cact + gath[d]
        cact_ref[...] = cact
        res[...] = jnp.dot(cact, w_ref[...], precision=HIGHEST, preferred_element_type=F32) + b_ref[...]
        parts[q] = res[...]
        sends2 = []
        for k, (dx, dy) in enumerate(_CHIP_OFFSETS):
            cp = pltpu.make_async_remote_copy(res, parts.at[q], send_sems.at[7 + k], recv_sems.at[7 + k],
                                              device_id=(x ^ dx, y ^ dy, cc), device_id_type=MESH)
            cp.start()
            sends2.append(cp)
        for cp in sends2:
            cp.wait_recv()
        row2 = lax.broadcasted_iota(jnp.int32, (8, ncol), 0)
        out = jnp.zeros((8, ncol), F32)
        for s in range(NQ):
            mine = jnp.sum(jnp.where(row2 == me, parts[s], 0.0), axis=0, keepdims=True)
            out = out + jnp.where(row2 == s, jnp.broadcast_to(mine, (8, ncol)), 0.0)
        modsel_ref[...] = out
        for cp in sends + sends2:
            cp.wait_send()

    return _side_call(
        body, side, None, name="ada_fwd",
        in_specs=[VMEM, VMEM, VMEM], out_specs=[VMEM, VMEM],
        out_shape=[_sds((8, D), F32), _sds((8, ncol), F32)],
        scratch_shapes=[pltpu.VMEM((8, D), F32), pltpu.VMEM((N_DEV, 8, D), F32), pltpu.VMEM((8, ncol), F32),
                        pltpu.VMEM((NQ, 8, ncol), F32), pltpu.SemaphoreType.DMA((10,)), pltpu.SemaphoreType.DMA((10,))],
        compiler_params=_params(None, VMEM_LIMIT), start_in_body=side is not None,
        args=(c, w_ada_q, b_ada_q))


class _Side:
    def __init__(self, ins, out_shapes, aliases, nsem, start, mid=None, finish=None):
        self.ins, self.out_shapes, self.aliases, self.nsem = list(ins), list(out_shapes), dict(aliases), nsem
        self.start, self.mid, self.finish = start, mid, finish


def _join(*sides):
    ins, outs, aliases, offs, nsem = [], [], {}, [], 0
    for s in sides:
        offs.append((len(ins), len(outs), nsem))
        aliases.update({len(ins) + a: len(outs) + b for a, b in s.aliases.items()})
        ins += s.ins
        outs += s.out_shapes
        nsem += s.nsem

    def hook(name):
        def run(i, o, ss, rs, base):
            for s, (io, oo, so) in zip(sides, offs):
                fn = getattr(s, name)
                if fn is not None:
                    fn(i[io:io + len(s.ins)], o[oo:oo + len(s.out_shapes)], ss, rs, base + so)
        return run

    return _Side(ins, outs, aliases, nsem, hook("start"), hook("mid"), hook("finish"))


def _side_call(body, side, when, *, name, in_specs, out_specs, out_shape, scratch_shapes, args, aliases=None,
               start_in_body=False, **kw):
    n_in, n_out = len(in_specs), len(out_specs)
    aliases = dict(aliases or {})
    if side is None:
        return _call(body, name=name, in_specs=in_specs, out_specs=out_specs, out_shape=out_shape,
                     scratch_shapes=scratch_shapes, input_output_aliases=aliases, **kw)(*args), []
    ns_in, ns_out = len(side.ins), len(side.out_shapes)

    def hook(fn, k, operands):
        if fn is None:
            return
        if when is None:
            fn(*operands, 0)
        elif when[k] is not None:
            pl.when(when[k]())(functools.partial(fn, *operands, 0))

    def wrapped(*refs):
        ins, s_ins = refs[:n_in], refs[n_in:n_in + ns_in]
        o0 = n_in + ns_in
        outs, s_outs = refs[o0:o0 + n_out], refs[o0 + n_out:o0 + n_out + ns_out]
        rest = refs[o0 + n_out + ns_out:]
        scratch, operands = rest[:-2], (s_ins, s_outs, rest[-2], rest[-1])
        if start_in_body:
            body(*ins, *outs, *scratch, side_start=functools.partial(hook, side.start, 0, operands))
        else:
            hook(side.start, 0, operands)
            body(*ins, *outs, *scratch)
        hook(side.mid, 1, operands)
        hook(side.finish, 2, operands)

    res = _call(
        wrapped, name=name,
        in_specs=list(in_specs) + [ANY] * ns_in, out_specs=list(out_specs) + [ANY] * ns_out,
        out_shape=list(out_shape) + side.out_shapes,
        scratch_shapes=list(scratch_shapes) + [pltpu.SemaphoreType.DMA((side.nsem,)),
                                               pltpu.SemaphoreType.DMA((side.nsem,))],
        input_output_aliases={**aliases, **{n_in + a: n_out + b for a, b in side.aliases.items()}},
        **kw)(*args, *side.ins)
    return res[:n_out], res[n_out:]


def _run_side(side, name):
    return _side_call(lambda: None, side, None, name=name, in_specs=[], out_specs=[], out_shape=[],
                      scratch_shapes=[], args=[])[1]


def _remote(src, dst, ss, rs, k, dev):
    return pltpu.make_async_remote_copy(src, dst, ss.at[k], rs.at[k], device_id=dev, device_id_type=MESH)


def _gather_side(bufs):
    n = len(bufs)

    def plan(outs, w):
        x, y, cc = _me()
        hr = outs[w].shape[1] // 2
        mine, other = cc * hr, (1 - cc) * hr
        qx, qy, qd, q = 2 * (x ^ 1) + y, 2 * x + (y ^ 1), 2 * (x ^ 1) + (y ^ 1), 2 * x + y
        xn, yn, sib = (x ^ 1, y, cc), (x, y ^ 1, cc), (x, y, 1 - cc)
        at = lambda slot, r0, nr: outs[w].at[slot, pl.ds(r0, nr)]
        send = [(at(q, mine, hr), xn), (at(q, mine, hr), yn),
                (at(qx, mine, hr // 2), yn), (at(qy, mine + hr // 2, hr // 2), xn),
                (at(qx, mine, hr), sib), (at(qy, mine, hr), sib), (at(qd, mine, hr), sib)]
        recv = [at(qx, mine, hr), at(qy, mine, hr), at(qd, mine, hr // 2), at(qd, mine + hr // 2, hr // 2),
                at(qx, other, hr), at(qy, other, hr), at(qd, other, hr)]
        return send, recv

    def op(outs, ss, rs, b, w, k, what):
        send, recv = plan(outs, w)
        if what == "wait_recv":
            _remote(recv[k], recv[k], ss, rs, b + 7 * w + k, send[k][1]).wait_recv()
        else:
            getattr(_remote(send[k][0], send[k][0], ss, rs, b + 7 * w + k, send[k][1]), what)()

    def start(ins, outs, ss, rs, b):
        for w in range(n):
            for k in (0, 1):
                op(outs, ss, rs, b, w, k, "start")

    def mid(ins, outs, ss, rs, b):
        for w in range(n):
            for k in (0, 1):
                op(outs, ss, rs, b, w, k, "wait_recv")
                op(outs, ss, rs, b, w, 2 + k, "start")
                op(outs, ss, rs, b, w, 4 + k, "start")

    def finish(ins, outs, ss, rs, b):
        for w in range(n):
            for k in (2, 3):
                op(outs, ss, rs, b, w, k, "wait_recv")
            op(outs, ss, rs, b, w, 6, "start")
        for w in range(n):
            for k in (4, 5, 6):
                op(outs, ss, rs, b, w, k, "wait_recv")
            for k in range(7):
                op(outs, ss, rs, b, w, k, "wait_send")

    return _Side(bufs, [_sds(tuple(w.shape), w.dtype) for w in bufs], {i: i for i in range(n)}, 7 * n,
                 start, mid, finish)


def _copies_side(ins, out_shapes, nsem, copies):
    def start(*a):
        for cp in copies(*a):
            cp.start()

    def finish(*a):
        for cp in copies(*a):
            cp.wait()

    return _Side(ins, out_shapes, {}, nsem, start, None, finish)


def _swap_side(gs):
    def copies(ins, outs, ss, rs, b):
        x, y, cc = _me()
        return [_remote(ins[w].at[:, 1 - cc], outs[w], ss, rs, b + w, (x, y, 1 - cc)) for w in range(len(gs))]

    return _copies_side(gs, [_sds((NQ,) + tuple(g.shape[2:]), F32) for g in gs], len(gs), copies)


def _exchange_side(cs, slots=None, prev=None):
    n = len(cs)
    slots = slots or [(0, 1, 2, 3)] * n

    def among(chip, allowed):
        hit = chip == allowed[0]
        for s in allowed[1:]:
            hit = hit | (chip == s)
        return hit

    def each(ins, outs, ss, rs, b, do_send, do_recv):
        x, y, cc = _me()
        q = 2 * x + y
        for w in range(n):
            for j, (dx, dy) in enumerate(_CHIP_OFFSETS):
                pq = 2 * (x ^ dx) + (y ^ dy)
                cp = _remote(ins[w].at[pq], outs[w].at[q], ss, rs, b + 3 * w + j, (x ^ dx, y ^ dy, cc))
                if do_send is not None:
                    pl.when(among(pq, slots[w]))(functools.partial(do_send, cp))
                if do_recv is not None:
                    pl.when(among(q, slots[w]))(functools.partial(do_recv, cp))

    def start(ins, outs, ss, rs, b):
        each(ins, outs, ss, rs, b, lambda cp: cp.start(), None)

    def finish(ins, outs, ss, rs, b):
        each(ins, outs, ss, rs, b, lambda cp: cp.wait_send(), lambda cp: cp.wait_recv())

    ins = list(cs) + (list(prev) if prev is not None else [])
    aliases = {n + w: w for w in range(n)} if prev is not None else {}
    return _Side(ins, [_sds(tuple(c.shape), c.dtype) for c in cs], aliases, 3 * n, start, None, finish)


def _exchange_copies(srcs, lands, send_sems, recv_sems):
    x, y, cc = _me()
    return [pltpu.make_async_remote_copy(srcs[w].at[2 * (x ^ dx) + (y ^ dy)], lands[w].at[2 * x + y],
                                         send_sems.at[3 * w + j], recv_sems.at[3 * w + j],
                                         device_id=(x ^ dx, y ^ dy, cc), device_id_type=MESH)
            for w in range(len(srcs)) for j, (dx, dy) in enumerate(_CHIP_OFFSETS)]


def _exchange_start(cs, after=()):
    n = len(cs)
    hbm, sem = pl.BlockSpec(memory_space=pltpu.HBM), pl.BlockSpec(memory_space=pltpu.SEMAPHORE)
    srcs = [pltpu.with_memory_space_constraint(c, pltpu.HBM) for c in cs]
    lands = [pltpu.with_memory_space_constraint(lax.empty(c.shape, c.dtype), pltpu.HBM) for c in cs]

    def body(*refs):
        sems = 2 * n + len(after)
        for cp in _exchange_copies(refs[:n], refs[n:2 * n], refs[sems], refs[sems + 1]):
            cp.start()
        refs[-1][...] = jnp.zeros_like(refs[-1])

    res = pl.pallas_call(
        body, name="exchange_start", interpret=False,
        out_shape=(pltpu.SemaphoreType.DMA((3 * n,)), pltpu.SemaphoreType.DMA((3 * n,)),
                   *[pltpu.HBM(c.shape, c.dtype) for c in cs], *[pltpu.HBM(c.shape, c.dtype) for c in cs],
                   _sds((8, LANE), F32)),
        in_specs=(hbm,) * (2 * n) + (ANY,) * len(after), out_specs=(sem, sem) + (hbm,) * (2 * n) + (VMEM,),
        input_output_aliases={i: 2 + i for i in range(2 * n)},
        compiler_params=pltpu.CompilerParams(has_side_effects=pltpu.SideEffectType.DATAFLOW_SIDE_EFFECTING),
    )(*srcs, *lands, *after)
    return res[0], res[1], list(res[2:2 + n]), list(res[2 + n:2 + 2 * n]), res[-1]


def _exchange_wait(send_sems, recv_sems, srcs, lands, after):
    n = len(srcs)
    hbm, sem = pl.BlockSpec(memory_space=pltpu.HBM), pl.BlockSpec(memory_space=pltpu.SEMAPHORE)

    def body(*refs):
        for cp in _exchange_copies(refs[:n], refs[n:2 * n], refs[2 * n], refs[2 * n + 1]):
            cp.wait_send()
            cp.wait_recv()

    res = pl.pallas_call(
        body, name="exchange_wait", interpret=False,
        out_shape=[pltpu.HBM(c.shape, c.dtype) for c in srcs + lands],
        in_specs=(hbm,) * (2 * n) + (sem, sem) + (ANY,) * len(after), out_specs=(hbm,) * (2 * n),
        input_output_aliases={i: i for i in range(2 * n)},
        compiler_params=pltpu.CompilerParams(has_side_effects=pltpu.SideEffectType.DATAFLOW_SIDE_EFFECTING),
    )(*srcs, *lands, send_sems, recv_sems, *after)
    return list(res[:n]), list(res[n:])


def _share_side(fs):
    def copies(ins, outs, ss, rs, b):
        x, y, cc = _me()
        return [_remote(ins[w], outs[w], ss, rs, b + w, (x, y, 1 - cc)) for w in range(len(fs))]

    return _copies_side(fs, [_sds(tuple(f.shape), F32) for f in fs], len(fs), copies)


def _small_allreduce_adamw(g, w, m, v, nd):
    rows = g.shape[0]
    nr = rows - nd
    hr = nr // 2
    assert nd % 8 == 0 and hr % 8 == 0

    def body(g_ref, w_ref, m_ref, v_ref, gs_ref, d_ref, mo_ref, vo_ref, gath, sib, csum, slots, tot, ss, rs):
        x, y, cc = _me()
        me = 4 * x + 2 * y + cc
        q = 2 * x + y
        sibling = (x, y, 1 - cc)
        dm = g_ref.at[pl.ds(0, nd)]
        gath[me] = g_ref[0:nd, :]
        to_all = [_remote(dm, gath.at[me], ss, rs, k, (x ^ dx, y ^ dy, cc ^ dc)) for k, (dx, dy, dc) in enumerate(_OFFSETS7)]
        to_sib = _remote(g_ref.at[pl.ds(nd, nr)], sib, ss, rs, 7, sibling)
        for cp in to_all + [to_sib]:
            cp.start()
        to_sib.wait_recv()
        csum[...] = g_ref[nd:, :] + sib[...]
        mine = pl.ds(pl.multiple_of(cc * hr, 8), hr)
        slots[q] = csum[mine, :]
        to_chips = [_remote(csum.at[mine], slots.at[q], ss, rs, 8 + j, (x ^ dx, y ^ dy, cc))
                    for j, (dx, dy) in enumerate(_CHIP_OFFSETS)]
        for cp in to_chips:
            cp.start()
        for cp in to_chips:
            cp.wait_recv()
        tot[mine, :] = (slots[0] + slots[1]) + (slots[2] + slots[3])
        halves = _remote(tot.at[mine], tot.at[mine], ss, rs, 11, sibling)
        halves.start()
        for cp in to_all:
            cp.wait_recv()
        dsum = gath[0]
        for dev in range(1, N_DEV):
            dsum = dsum + gath[dev]
        halves.wait_recv()
        for lo, n, total in ((0, nd, dsum), (nd, nr, tot[...])):
            gs_ref[lo:lo + n, :] = total
            d, mn, vn = _adamw(w_ref[lo:lo + n, :], total, m_ref[lo:lo + n, :], v_ref[lo:lo + n, :])
            d_ref[lo:lo + n, :] = d
            mo_ref[lo:lo + n, :] = mn
            vo_ref[lo:lo + n, :] = vn
        for cp in to_all + [to_sib, halves] + to_chips:
            cp.wait_send()

    return _call(
        body, name="small_allreduce_adamw",
        in_specs=[VMEM] * 4, out_specs=[VMEM] * 5,
        out_shape=[_sds((rows, LANE), F32)] * 4 + [_sds((N_DEV, nd, LANE), F32)],
        scratch_shapes=[pltpu.VMEM((nr, LANE), F32), pltpu.VMEM((nr, LANE), F32), pltpu.VMEM((NQ, hr, LANE), F32),
                        pltpu.VMEM((nr, LANE), F32), pltpu.SemaphoreType.DMA((12,)), pltpu.SemaphoreType.DMA((12,))],
        compiler_params=_params(None, VMEM_LIMIT),
    )(g, w, m, v)


_SMALL = ["b_ada", "norm_ffn1_g", "norm_mix_g", "pool_scale", "gmlp_ln_g", "gmlp_ln_b", "b_spatial",
          "norm_ffn2_g", "norm_final_g", "w_pool", "w_spatial"]


def _pack(parts):
    blocks, layout, r0 = [], {}, 0
    for name in _SMALL:
        a = parts[name]
        n = a.size
        rows = -(-n // LANE)
        rows8 = -(-rows // 8) * 8
        flat = a.reshape(-1).astype(F32)
        if rows8 * LANE != n:
            flat = jnp.concatenate([flat, jnp.zeros((rows8 * LANE - n,), F32)])
        blocks.append(flat.reshape(rows8, LANE))
        layout[name] = (r0, n, a.shape)
        r0 += rows8
    return jnp.concatenate(blocks, axis=0), layout


def _unpack(packed, layout):
    out = {}
    for name, (r0, n, shape) in layout.items():
        rows = -(-n // LANE)
        out[name] = packed[r0:r0 + rows].reshape(-1)[:n].reshape(shape)
    return out


def _modv(mod9, sub, gain):
    rows = jnp.concatenate([mod9[3 * sub:3 * sub + 3], gain.reshape(1, D), jnp.zeros((4, D), F32)], axis=0)
    return rows


_BIG = ["ffn1_w_in", "ffn1_w_out", "w_mix_in", "w_mix_out", "ffn2_w_in", "ffn2_w_out"]


def kernel(x, c, w_ada, b_ada, norm_ffn1_g, ffn1_w_in, ffn1_w_out, norm_mix_g, w_mix_in, w_pool, pool_scale, gmlp_ln_g, gmlp_ln_b, w_spatial, b_spatial, w_mix_out, norm_ffn2_g, ffn2_w_in, ffn2_w_out, norm_final_g, loss_target, m_w_ada, m_b_ada, m_norm_ffn1_g, m_ffn1_w_in, m_ffn1_w_out, m_norm_mix_g, m_w_mix_in, m_w_pool, m_pool_scale, m_gmlp_ln_g, m_gmlp_ln_b, m_w_spatial, m_b_spatial, m_w_mix_out, m_norm_ffn2_g, m_ffn2_w_in, m_ffn2_w_out, m_norm_final_g, v_w_ada, v_b_ada, v_norm_ffn1_g, v_ffn1_w_in, v_ffn1_w_out, v_norm_mix_g, v_w_mix_in, v_w_pool, v_pool_scale, v_gmlp_ln_g, v_gmlp_ln_b, v_w_spatial, v_b_spatial, v_w_mix_out, v_norm_ffn2_g, v_ffn2_w_in, v_ffn2_w_out, v_norm_final_g):
    names = ["w_ada", "b_ada", "norm_ffn1_g", "ffn1_w_in", "ffn1_w_out", "norm_mix_g", "w_mix_in", "w_pool",
             "pool_scale", "gmlp_ln_g", "gmlp_ln_b", "w_spatial", "b_spatial", "w_mix_out", "norm_ffn2_g",
             "ffn2_w_in", "ffn2_w_out", "norm_final_g"]
    W = dict(zip(names, [w_ada, b_ada, norm_ffn1_g, ffn1_w_in, ffn1_w_out, norm_mix_g, w_mix_in, w_pool, pool_scale,
                         gmlp_ln_g, gmlp_ln_b, w_spatial, b_spatial, w_mix_out, norm_ffn2_g, ffn2_w_in, ffn2_w_out,
                         norm_final_g]))
    M = dict(zip(names, [m_w_ada, m_b_ada, m_norm_ffn1_g, m_ffn1_w_in, m_ffn1_w_out, m_norm_mix_g, m_w_mix_in, m_w_pool,
                         m_pool_scale, m_gmlp_ln_g, m_gmlp_ln_b, m_w_spatial, m_b_spatial, m_w_mix_out, m_norm_ffn2_g,
                         m_ffn2_w_in, m_ffn2_w_out, m_norm_final_g]))
    V = dict(zip(names, [v_w_ada, v_b_ada, v_norm_ffn1_g, v_ffn1_w_in, v_ffn1_w_out, v_norm_mix_g, v_w_mix_in, v_w_pool,
                         v_pool_scale, v_gmlp_ln_g, v_gmlp_ln_b, v_w_spatial, v_b_spatial, v_w_mix_out, v_norm_ffn2_g,
                         v_ffn2_w_in, v_ffn2_w_out, v_norm_final_g]))

    xi, yi, ci = _me()
    q = 2 * xi + yi
    core = ci.astype(jnp.int32).reshape(1)

    chip = q.astype(jnp.int32).reshape(1)
    place = lambda n: _cast_place(W[n][0], chip)

    ncol = w_ada.shape[2]
    b_q = lax.dynamic_slice_in_dim(b_ada, q * ncol, ncol, axis=1)
    (cact_all, modsel), (win1, wout1) = _ada_fwd(
        c, w_ada[0], b_q, side=_gather_side([place("ffn1_w_in"), place("ffn1_w_out")]))
    mod9 = modsel[:NQ].reshape(9, D)
    xs, target = x[0], loss_target[0]
    mv1 = _modv(mod9, 0, norm_ffn1_g[0])
    mv2 = _modv(mod9, 1, norm_mix_g[0])
    mv3 = _modv(mod9, 2, norm_ffn2_g[0])
    wcat, wtcat, bias = _prep_spatial(w_spatial[0], b_spatial[0].T)
    wpool = w_pool[0].astype(BF16)
    vecs = jnp.concatenate([pool_scale, gmlp_ln_g, gmlp_ln_b, jnp.zeros((5, DP), F32)], axis=0)
    gf = jnp.concatenate([norm_final_g.reshape(1, D), jnp.zeros((7, D), F32)], axis=0)

    later =["w_mix_in", "w_mix_out", "ffn2_w_in", "ffn2_w_out"]
    (x1, g1s, u1s), got = _ffn_fwd(xs, mv1, win1, wout1.reshape(2, CH, D), side=_gather_side([place(n) for n in later]))
    wmi, wmo, win2, wout2 = got
    wmi = jnp.transpose(wmi, (1, 0, 2)).reshape(D, DPROJ)
    wmo = wmo.reshape(DP + DG, D)
    x2, pooled, zb = _mix_fwd(x1, mv2, wmi, wpool, vecs, wcat, bias, wmo)
    (dx3, g3s, u3s, loss_blk, dgf), _ = _ffn_fwd(x2, mv3, win2, wout2.reshape(2, CH, D), head=(target, gf))

    wo1, wo2 = wout1.reshape(2, CH, D), wout2.reshape(2, CH, D)
    (dx2, oin2, oout2, rin2, rout2, vec3), _ = _ffn_bwd(x2, dx3, g3s, u3s, mv3, win2, wo2)
    cs2 = [_chip_sum_pair(oin2, rin2), _chip_sum_pair(oout2, rout2)]
    (dx1, dwmi, dwmo, dwpool, dwsp, dbsp, v512, vec2), ex2 = _mix_bwd(
        x1, dx2, pooled, zb, mv2, wmi, wpool, vecs, wcat, wtcat, bias, wmo, side=_exchange_side(cs2))
    half2 = [_sum4(cs, e, chip) for cs, e in zip(cs2, ex2)]
    qcols = w_mix_in.shape[2]
    vmix = [jnp.transpose(dwmi.reshape(D, NQ, qcols), (1, 0, 2)).reshape(NQ, 2, D // 2, qcols),
            dwmo.reshape(NQ, 2, (DP + DG) // 8, D)]
    first1, got = _ffn_bwd_pass(0, xs, dx1, g1s, u1s, mv1, win1, wo1,
                                side=_join(_swap_side(vmix), _share_side(half2)))
    sibmix, other2 = got[:2], got[2:]
    cs_mix = [_chip_sum(g, r, core) for g, r in zip(vmix, sibmix)]
    (grad_x, oin1, oout1, rin1, rout1, vec1), ex_mix = _ffn_bwd_pass(
        1, xs, dx1, g1s, u1s, mv1, win1, wo1, prev=first1[:5], side=_exchange_side(cs_mix))
    vec1 = first1[5] + vec1
    cs_ffn1 = [_chip_sum_pair(oin1, rin1), _chip_sum_pair(oout1, rout1)]

    dmod =jnp.concatenate([vec1[0:3], vec2[0:3], vec3[0:3]], axis=0)
    grads = dict(
        b_ada=dmod.reshape(1, 9 * D), norm_ffn1_g=vec1[3:4], norm_mix_g=vec2[3:4], norm_ffn2_g=vec3[3:4],
        pool_scale=v512[0:1], gmlp_ln_g=v512[1:2], gmlp_ln_b=v512[2:3], b_spatial=dbsp[None],
        norm_final_g=dgf[0], w_pool=dwpool[None], w_spatial=dwsp[None])

    gp, layout = _pack({n: grads[n] for n in _SMALL})
    gp = jnp.concatenate([gp, loss_blk, loss_blk], axis=0)
    pad = jnp.zeros((16, LANE), F32)
    wp, mp, vp = [jnp.concatenate([_pack({n: src[n] for n in _SMALL})[0], pad], axis=0) for src in (W, M, V)]
    r0, nb, _ = layout["b_ada"]
    assert r0 == 0
    out_g, out_d, out_m, out_v = {}, {}, {}, {}
    gs, dl, mo, vo, gath = _small_allreduce_adamw(gp, wp, mp, vp, nb // LANE)
    loss = gs[-16, 0]
    for packed, dst in ((gs, out_g), (dl, out_d), (mo, out_m), (vo, out_v)):
        for n, a in _unpack(packed, layout).items():
            dst[n] = a.reshape(W[n].shape)

    def update(n, own, recv, after=()):
        (g2, d, mn, vn), _ = _adamw_halves(W[n][0], own, recv, M[n][0], V[n][0], after=after)
        out_g[n], out_d[n], out_m[n], out_v[n] = g2[None], d[None], mn[None], vn[None]
        return g2

    ssem, rsem, cs_fly, land_fly, token = _exchange_start(cs_ffn1, after=(gath,))
    dmod_q = lax.dynamic_slice_in_dim(gath.reshape(N_DEV, nb), q * ncol, ncol, axis=1) + token[0:8, 0:1]
    (ga, da, ma, va), _ = _ada_grad_adamw(cact_all.T, dmod_q, w_ada[0], m_w_ada[0], v_w_ada[0])
    out_g["w_ada"], out_d["w_ada"], out_m["w_ada"], out_v["w_ada"] = ga[None], da[None], ma[None], va[None]
    done = [ga, update("ffn2_w_in", half2[0], other2[0], after=(token,)),
            update("ffn2_w_out", half2[1], other2[1], after=(token,))]
    half_mix = [_sum4(cs, e, chip, after=(token,)) for cs, e in zip(cs_mix, ex_mix)]
    other_mix = _run_side(_share_side(half_mix), "share_mix")
    done += [update(n, own, recv) for n, own, recv in zip(["w_mix_in", "w_mix_out"], half_mix, other_mix)]
    cs_ffn1, ex_ffn1 = _exchange_wait(ssem, rsem, cs_fly, land_fly, after=done)
    half1 = [_sum4(cs, e, chip) for cs, e in zip(cs_ffn1, ex_ffn1)]
    other1 = _run_side(_share_side(half1), "share_ffn1")
    for n, own, recv in zip(["ffn1_w_in", "ffn1_w_out"], half1, other1):
        update(n, own, recv)

    return (loss, grad_x[None], *[out_g[n] for n in names], *[out_d[n] for n in names],
            *[out_m[n] for n in names], *[out_v[n] for n in names])
```

```python
import functools
import math

import jax
import jax.numpy as jnp
from jax import lax
from jax.experimental import pallas as pl
from jax.experimental.pallas import tpu as pltpu

F32 = jnp.float32
BF16 = jnp.bfloat16
MESH = pl.DeviceIdType.MESH
HIGHEST = lax.Precision.HIGHEST

EPS = 1e-6
D = 1024
DFF = 2816
CH = DFF // 2
NQ = 4
DP = 512
DG = 512
DPROJ = DP + 2 * DG
POOL_WINDOWS = (2, 4, 8, 16)
HALO = 16
CHUNK = 128
LANE = 128
N_DEV = 8

ADAM_LR = 0.001
ADAM_B1 = 0.9
ADAM_B2 = 0.999
ADAM_EPS = 1e-08
ADAM_WD = 0.01
ADAM_STEP = 10

VMEM_LIMIT = 62 * 1024 * 1024

TM_FFN_FWD = 512
TM_FFN_BWD = 512
TM_MIX = 256


def _call(body, **kw):
    return pl.pallas_call(body, interpret=False, **kw)


def _params(sem=None, vmem=None):
    return pltpu.CompilerParams(dimension_semantics=sem, vmem_limit_bytes=vmem)


def _sds(shape, dtype):
    return jax.ShapeDtypeStruct(shape, dtype)


ANY = pl.BlockSpec(memory_space=pl.ANY)
VMEM = pl.BlockSpec(memory_space=pltpu.VMEM)
SMEM = pl.BlockSpec(memory_space=pltpu.SMEM)


def _norm_mod(x, gn, sc, sh):
    r = lax.rsqrt(jnp.mean(x * x, axis=-1, keepdims=True) + EPS)
    xn = x * r
    hp = xn * gn
    return r, xn, hp, hp * (1.0 + sc) + sh


def _norm_mod_bwd(dh, r, xn, hp, gn, sc):
    one_sc = 1.0 + sc
    dsh = jnp.sum(dh, axis=0, keepdims=True)
    dsc = jnp.sum(dh * hp, axis=0, keepdims=True)
    dgn = jnp.sum(dh * one_sc * xn, axis=0, keepdims=True)
    dxn = dh * (gn * one_sc)
    dx = r * (dxn - xn * jnp.mean(dxn * xn, axis=-1, keepdims=True))
    return dsh, dsc, dgn, dx


def _dot(a, b):
    return jnp.dot(a, b, preferred_element_type=F32)


def _dot_nt(a, b):
    return lax.dot_general(a, b, (((1,), (1,)), ((), ())), preferred_element_type=F32)


def _dot_tn(a, b):
    return lax.dot_general(a, b, (((0,), (0,)), ((), ())), preferred_element_type=F32)


_GELU_C = math.sqrt(2.0 / math.pi)
_GELU_A = 0.044715


def _gelu_fwd_bwd(x):
    x2 = x * x
    t = jnp.tanh(_GELU_C * (x + _GELU_A * x * x2))
    g = 0.5 * x * (1.0 + t)
    dg = 0.5 * (1.0 + t) + 0.5 * x * (1.0 - t * t) * (_GELU_C * (1.0 + 3.0 * _GELU_A * x2))
    return g, dg


def _adamw(w, g, m, v):
    m = ADAM_B1 * m + (1.0 - ADAM_B1) * g
    v = ADAM_B2 * v + (1.0 - ADAM_B2) * (g * g)
    m_hat = m / (1.0 - ADAM_B1 ** ADAM_STEP)
    v_hat = v / (1.0 - ADAM_B2 ** ADAM_STEP)
    delta = -ADAM_LR * (m_hat / (jnp.sqrt(v_hat) + ADAM_EPS) + ADAM_WD * w)
    return delta, m, v


def _row_block(rows, cap=256, mult=16):
    best = None
    for t in range(mult, min(rows, cap) + 1, mult):
        if rows % t == 0:
            best = t
    assert best is not None, rows
    return best


def _head_math(x, target, gf):
    r = lax.rsqrt(jnp.mean(x * x, axis=-1, keepdims=True) + EPS)
    xn = x * r
    err = xn * gf - target
    dy = err * (1.0 / D)
    dxn = dy * gf
    dx = r * (dxn - xn * jnp.mean(dxn * xn, axis=-1, keepdims=True))
    return (0.5 / D) * jnp.sum(err * err), jnp.sum(dy * xn, axis=0, keepdims=True), dx


def _ffn_fwd(x, modv, win, wout, side=None, head=None):
    S = x.shape[0]
    tm = TM_FFN_FWD
    nt = S // tm

    def body(*refs):
        if head is None:
            x_ref, mod_ref, wg_ref, wu_ref, wo_ref, xo_ref, gs_ref, us_ref, acc_scr = refs
        else:
            (x_ref, mod_ref, wg_ref, wu_ref, wo_ref, t_ref, gf_ref,
             xo_ref, gs_ref, us_ref, loss_ref, dgf_ref, acc_scr) = refs

        @pl.when((pl.program_id(0) == 0) & (pl.program_id(1) == 0))
        def _():
            acc_scr[...] = jnp.zeros_like(acc_scr)
            if head is not None:
                loss_ref[...] = jnp.zeros_like(loss_ref)
                dgf_ref[...] = jnp.zeros_like(dgf_ref)

        j = pl.program_id(1)
        h = _norm_mod(x_ref[...], mod_ref[3:4, :], mod_ref[1:2, :], mod_ref[0:1, :])[3].astype(BF16)
        g = _dot(h, wg_ref[...]).astype(BF16)
        u = _dot(h, wu_ref[...]).astype(BF16)
        gs_ref[...] = g
        us_ref[...] = u
        gf = g.astype(F32)
        a = (gf * jax.nn.sigmoid(gf) * u.astype(F32)).astype(BF16)
        acc = jnp.where(j == 0, 0.0, acc_scr[...]) + _dot(a, wo_ref[...])
        acc_scr[...] = acc
        xo = x_ref[...] + (0.5 * mod_ref[2:3, :]) * acc
        if head is None:
            xo_ref[...] = xo
        else:
            @pl.when(j == 1)
            def _():
                loss, dgf, dx = _head_math(xo, t_ref[...], gf_ref[0:1, :])
                loss_ref[...] += loss
                dgf_ref[0:1, :] += dgf
                xo_ref[...] = dx

    step = lambda i, j: lambda: (pl.program_id(0) == i) & (pl.program_id(1) == j)
    tile = pl.BlockSpec((tm, D), lambda i, j: (i, 0))
    const = lambda shape: pl.BlockSpec(shape, lambda i, j: (0, 0))
    chunk = pl.BlockSpec((tm, CH), lambda i, j: (i, j))
    in_specs = [tile, const((8, D)), pl.BlockSpec((None, D, CH), lambda i, j: (j, 0, 0)),
                pl.BlockSpec((None, D, CH), lambda i, j: (2 + j, 0, 0)), pl.BlockSpec((None, CH, D), lambda i, j: (j, 0, 0))]
    out_specs = [tile, chunk, chunk]
    out_shape = [_sds((S, D), F32), _sds((S, DFF), BF16), _sds((S, DFF), BF16)]
    args = (x, modv, win, win, wout)
    if head is not None:
        in_specs += [tile, const((8, D))]
        out_specs += [const((8, LANE)), const((8, D))]
        out_shape += [_sds((8, LANE), F32), _sds((8, D), F32)]
        args += tuple(head)
    return _side_call(
        body, side, (step(0, 0), step((7 * nt) // 10, 0), step(nt - 1, 1)), name="ffn_fwd",
        grid=(nt, 2), in_specs=in_specs, out_specs=out_specs, out_shape=out_shape,
        scratch_shapes=[pltpu.VMEM((tm, D), F32)],
        compiler_params=_params(("arbitrary", "arbitrary"), VMEM_LIMIT),
        args=args)


def _ffn_bwd_pass(jj, x, dxo, gs, us, modv, win, wout, prev=None, side=None):
    S = x.shape[0]
    tm = TM_FFN_BWD
    nsub = tm // 256
    nt = S // tm
    hi, ho = D // 2, CH // 4
    last = prev is not None
    assert last == (jj == 1)

    def body(*refs):
        x_ref, dxo_ref, gs_ref, us_ref, mod_ref, win_hbm, wo_ref = refs[:7]
        k = 12 if last else 7
        out_ref, dwin_ref, dwout_ref, rwin_ref, rwout_ref, vec_ref = refs[k:k + 6]
        accgu, accw, wgu, sems, fsend, frecv = refs[k + 6:]
        i = pl.program_id(0)

        @pl.when(i == 0)
        def _():
            loads = [pltpu.make_async_copy(win_hbm.at[jj], wgu.at[:, pl.ds(0, CH)], sems.at[0]),
                     pltpu.make_async_copy(win_hbm.at[2 + jj], wgu.at[:, pl.ds(CH, CH)], sems.at[1])]
            for cp in loads:
                cp.start()
            accgu[...] = jnp.zeros_like(accgu)
            accw[...] = jnp.zeros_like(accw)
            vec_ref[...] = jnp.zeros_like(vec_ref)
            for cp in loads:
                cp.wait()

        gn, sc, sh, gate = mod_ref[3:4, :], mod_ref[1:2, :], mod_ref[0:1, :], mod_ref[2:3, :]

        parts = []
        for s in range(nsub):
            rs = slice(s * (tm // nsub), (s + 1) * (tm // nsub))
            r, xn, hp, h = _norm_mod(x_ref[rs, :], gn, sc, sh)
            dxo = dxo_ref[rs, :]
            dy = (dxo * (0.5 * gate)).astype(BF16)
            g = gs_ref[rs, :].astype(F32)
            u = us_ref[rs, :].astype(F32)
            sig = jax.nn.sigmoid(g)
            sl = g * sig
            a = (sl * u).astype(BF16)
            da = _dot_nt(dy, wo_ref[...])
            dg = (da * u * (sig * (1.0 + g * (1.0 - sig)))).astype(BF16)
            du = (da * sl).astype(BF16)
            dgu = jnp.concatenate([dg, du], axis=1)
            dhp = _dot_nt(dgu, wgu[...])
            parts.append((h.astype(BF16), a, dgu, dxo.astype(BF16)))
            if last:
                dsh, dsc, dgn, dxin = _norm_mod_bwd(refs[7][rs, :] + dhp, r, xn, hp, gn, sc)
                vec_ref[0:1, :] += dsh
                vec_ref[1:2, :] += dsc
                vec_ref[3:4, :] += dgn
                out_ref[rs, :] = dxo + dxin
            else:
                out_ref[rs, :] = dhp

        hb, a, dgu, dxb = [jnp.concatenate(p, axis=0) if nsub > 1 else p[0] for p in zip(*parts)]
        accw[...] += _dot_tn(a, dxb)
        accgu[...] += _dot_tn(hb, dgu)

        @pl.when(i == nt - 1)
        def _():
            gw = accw[...]
            vec_ref[2:3, :] += 0.5 * jnp.sum(wo_ref[...].astype(F32) * gw, axis=0, keepdims=True)
            accw[...] = gw * (0.5 * gate)
            mx, my, cc = _me()
            part = lambda acc, base, n, c, col: acc.at[pl.ds(base + c * n, n), pl.ds(col[0], col[1])]
            pieces = [(accgu, 0, hi, (0, CH), dwin_ref, rwin_ref, jj), (accgu, 0, hi, (CH, CH), dwin_ref, rwin_ref, 2 + jj),
                      (accw, 0, ho, (0, D), dwout_ref, rwout_ref, 2 * jj),
                      (accw, 2 * ho, ho, (0, D), dwout_ref, rwout_ref, 2 * jj + 1)]
            loc = [pltpu.make_async_copy(part(acc, base, n, cc, col), own.at[slot], sems.at[p])
                   for p, (acc, base, n, col, own, _, slot) in enumerate(pieces)]
            rem = [pltpu.make_async_remote_copy(part(acc, base, n, 1 - cc, col), sib.at[slot], fsend.at[p], frecv.at[p],
                                                device_id=(mx, my, 1 - cc), device_id_type=MESH)
                   for p, (acc, base, n, col, _, sib, slot) in enumerate(pieces)]
            for cp in loc + rem:
                cp.start()
            for cp in loc:
                cp.wait()
            for cp in rem:
                cp.wait()

    once = pl.Buffered(1)
    tile = pl.BlockSpec((tm, D), lambda i: (i, 0))
    chunk = pl.BlockSpec((tm, CH), lambda i: (i, jj))
    in_specs = [tile, tile, chunk, chunk, pl.BlockSpec((8, D), lambda i: (0, 0)), ANY,
                pl.BlockSpec((None, CH, D), lambda i: (jj, 0, 0), pipeline_mode=once)]
    args = (x, dxo, gs, us, modv, win, wout)
    if last:
        in_specs += [tile, ANY, ANY, ANY, ANY]
        args += tuple(prev)
    step = lambda s: lambda: pl.program_id(0) == s
    return _side_call(
        body, side, (step(0), None, step(nt - 1)), name="ffn_bwd",
        grid=(nt,), in_specs=in_specs,
        out_specs=[tile, ANY, ANY, ANY, ANY, pl.BlockSpec((8, D), lambda i: (0, 0))],
        out_shape=[_sds((S, D), F32), _sds((NQ, hi, CH), F32), _sds((NQ, ho, D), F32), _sds((NQ, hi, CH), F32),
                   _sds((NQ, ho, D), F32), _sds((8, D), F32)],
        scratch_shapes=[pltpu.VMEM((D, 2 * CH), F32), pltpu.VMEM((CH, D), F32), pltpu.VMEM((D, 2 * CH), BF16),
                        pltpu.SemaphoreType.DMA((4,)), pltpu.SemaphoreType.DMA((4,)), pltpu.SemaphoreType.DMA((4,))],
        aliases={8 + p: 1 + p for p in range(4)} if last else {},
        compiler_params=_params(("arbitrary",), VMEM_LIMIT),
        args=args)


def _ffn_bwd(x, dxo, gs, us, modv, win, wout, side=None):
    first, extra = _ffn_bwd_pass(0, x, dxo, gs, us, modv, win, wout, side=side)
    (dx, dwin, dwout, rwin, rwout, vec), _ = _ffn_bwd_pass(1, x, dxo, gs, us, modv, win, wout, prev=first[:5])
    return (dx, dwin, dwout, rwin, rwout, first[5] + vec), extra


def _prep_spatial(w_spatial, b_spatial_t):
    def body(w_ref, b_ref, wcat_ref, wtcat_ref, bias_ref):
        row = lax.broadcasted_iota(jnp.int32, (CHUNK, CHUNK), 0)
        col = lax.broadcasted_iota(jnp.int32, (CHUNK, CHUNK), 1)
        tril = col <= row
        for p in range(4):
            wa = jnp.where(tril, w_ref[2 * p], 0.0)
            wb = jnp.where(tril, w_ref[2 * p + 1], 0.0)
            wcat_ref[p] = jnp.concatenate([wa, wb], axis=1).astype(BF16)
            wtcat_ref[p] = jnp.concatenate([wa.T, wb.T], axis=1).astype(BF16)
        head = lax.broadcasted_iota(jnp.int32, (8, DG), 0)
        ch = lax.broadcasted_iota(jnp.int32, (8, DG), 1)
        spread = jnp.where(ch // 64 == head, 1.0, 0.0).astype(F32)
        bias_ref[...] = jnp.dot(b_ref[...], spread, precision=HIGHEST, preferred_element_type=F32)

    return _call(
        body, name="prep_spatial",
        in_specs=[VMEM, VMEM], out_specs=[VMEM, VMEM, VMEM],
        out_shape=[_sds((4, CHUNK, 2 * CHUNK), BF16), _sds((4, CHUNK, 2 * CHUNK), BF16), _sds((CHUNK, DG), F32)],
    )(w_spatial, b_spatial_t)


def _pair_rhs(blocks):
    lane = lax.broadcasted_iota(jnp.int32, (CHUNK, LANE), 1)
    lo = lane < 64
    top = jnp.concatenate([jnp.where(lo, b, 0.0) for b in blocks], axis=1)
    bot = jnp.concatenate([jnp.where(lo, 0.0, b) for b in blocks], axis=1)
    return top, bot


def _gmlp_branch(zb, vecs, wcat_ref, bias_ref, nchunks):
    z, dz = _gelu_fwd_bwd(zb)
    u = z[:, :DG]
    v = z[:, DG:]
    ln_g, ln_b = vecs[1:2, :], vecs[2:3, :]
    mu = jnp.mean(v, axis=-1, keepdims=True)
    vc = v - mu
    rstd = lax.rsqrt(jnp.mean(vc * vc, axis=-1, keepdims=True) + EPS)
    vhat = vc * rstd
    vl = vhat * ln_g + ln_b
    sv_cols = []
    for p in range(4):
        blocks = [vl[k * CHUNK:(k + 1) * CHUNK, p * LANE:(p + 1) * LANE] for k in range(nchunks)]
        top, bot = _pair_rhs(blocks)
        rhs = jnp.concatenate([top, bot], axis=0).astype(BF16)
        out = _dot(wcat_ref[p], rhs)
        bias = bias_ref[:, p * LANE:(p + 1) * LANE]
        sv_cols.append(jnp.concatenate([out[:, k * LANE:(k + 1) * LANE] + bias for k in range(nchunks)], axis=0))
    sv = jnp.concatenate(sv_cols, axis=1)
    return dict(u=u, dz=dz, rstd=rstd, vhat=vhat, vl=vl, sv=sv, yb=u * sv)


def _mix_fwd(x, modv, win, wpool, vecs, wcat, bias, wout):
    S = x.shape[0]
    tm = TM_MIX
    nt = S // tm
    nchunks = tm // CHUNK

    def body(x_ref, mod_ref, win_ref, wpool_ref, vec_ref, wcat_ref, bias_ref, wout_ref,
             xo_ref, pooled_ref, zb_ref, ext):
        i = pl.program_id(0)

        @pl.when(i == 0)
        def _():
            ext[0:HALO, :] = jnp.zeros((HALO, DP), F32)

        x = x_ref[...]
        _, _, _, h = _norm_mod(x, mod_ref[3:4, :], mod_ref[1:2, :], mod_ref[0:1, :])
        proj = _dot(h.astype(BF16), win_ref[...])
        xa = proj[:, :DP]
        zb = proj[:, DP:]
        zb_ref[...] = zb
        ext[HALO:HALO + tm, :] = xa
        pos = i * tm + lax.broadcasted_iota(jnp.int32, (tm, 1), 0)
        vecs = vec_ref[...]
        ya_cols = []
        pooled_cols = []
        for gi, w in enumerate(POOL_WINDOWS):
            cols = slice(gi * LANE, (gi + 1) * LANE)
            s = xa[:, cols]
            for k in range(1, w):
                s = s + ext[HALO - k:HALO - k + tm, cols]
            cnt = jnp.minimum(pos + 1, w).astype(F32)
            pooled = (s / cnt - xa[:, cols]).astype(BF16)
            pooled_cols.append(pooled)
            ya_cols.append(_dot(pooled, wpool_ref[gi]) * vecs[0:1, cols])
        pooled_ref[...] = jnp.concatenate(pooled_cols, axis=1)
        ext[0:HALO, :] = ext[tm:tm + HALO, :]

        gm = _gmlp_branch(zb, vecs, wcat_ref, bias_ref, nchunks)
        cat = jnp.concatenate(ya_cols + [gm["yb"]], axis=1).astype(BF16)
        xo_ref[...] = x + mod_ref[2:3, :] * _dot(cat, wout_ref[...])

    full = lambda shape: pl.BlockSpec(shape, lambda i: (0,) * len(shape))
    return _call(
        body, name="mix_fwd",
        grid=(nt,),
        in_specs=[pl.BlockSpec((tm, D), lambda i: (i, 0)), full((8, D)), full((D, DPROJ)),
                  full((4, LANE, LANE)), full((8, DP)), full((4, CHUNK, 2 * CHUNK)), full((CHUNK, DG)),
                  full((DP + DG, D))],
        out_specs=[pl.BlockSpec((tm, D), lambda i: (i, 0)), pl.BlockSpec((tm, DP), lambda i: (i, 0)),
                   pl.BlockSpec((tm, 2 * DG), lambda i: (i, 0))],
        out_shape=[_sds((S, D), F32), _sds((S, DP), BF16), _sds((S, 2 * DG), F32)],
        scratch_shapes=[pltpu.VMEM((tm + HALO, DP), F32)],
        compiler_params=_params(("arbitrary",), VMEM_LIMIT),
    )(x, modv, win, wpool, vecs, wcat, bias, wout)


def _mix_bwd(x, dxo, pooled, zb, modv, win, wpool, vecs, wcat, wtcat, bias, wout, side=None):
    S = x.shape[0]
    tm = TM_MIX
    nt = S // tm
    nchunks = tm // CHUNK

    def body(x_ref, dxo_ref, pooled_ref, zb_ref, mod_ref, win_ref, wpool_ref, vec_ref, wcat_ref, wtcat_ref,
             bias_ref, wout_ref,
             dx_ref, dwin_ref, dwout_ref, dwpool_ref, dwsp_ref, dbsp_ref, v512_ref, vd_ref, qext, dsv_acc):
        step = pl.program_id(0)
        tile = nt - 1 - step

        @pl.when(step == 0)
        def _():
            dwin_ref[...] = jnp.zeros_like(dwin_ref)
            dwout_ref[...] = jnp.zeros_like(dwout_ref)
            dwpool_ref[...] = jnp.zeros_like(dwpool_ref)
            dwsp_ref[...] = jnp.zeros_like(dwsp_ref)
            v512_ref[...] = jnp.zeros_like(v512_ref)
            vd_ref[...] = jnp.zeros_like(vd_ref)
            dsv_acc[...] = jnp.zeros_like(dsv_acc)
            qext[tm:tm + HALO, :] = jnp.zeros((HALO, DP), F32)

        gn, sc, sh, gate = mod_ref[3:4, :], mod_ref[1:2, :], mod_ref[0:1, :], mod_ref[2:3, :]
        vecs = vec_ref[...]
        x = x_ref[...]
        r, xn, hp, h = _norm_mod(x, gn, sc, sh)
        hb = h.astype(BF16)
        dxo = dxo_ref[...]

        pooled = pooled_ref[...]
        mixed_cols = [_dot(pooled[:, gi * LANE:(gi + 1) * LANE], wpool_ref[gi]) for gi in range(4)]
        mixed = jnp.concatenate(mixed_cols, axis=1)
        scale = vecs[0:1, :]
        gm = _gmlp_branch(zb_ref[...], vecs, wcat_ref, bias_ref, nchunks)
        cat = jnp.concatenate([mixed * scale, gm["yb"]], axis=1).astype(BF16)

        dwout_ref[...] += _dot_tn(cat, dxo.astype(BF16))
        dcat = _dot_nt((dxo * gate).astype(BF16), wout_ref[...])
        dya = dcat[:, :DP]
        dyb = dcat[:, DP:]

        v512_ref[0:1, :] += jnp.sum(dya * mixed, axis=0, keepdims=True)
        dmixed = (dya * scale).astype(BF16)
        pos = tile * tm + lax.broadcasted_iota(jnp.int32, (tm, 1), 0)
        dpooled_cols = []
        for gi, w in enumerate(POOL_WINDOWS):
            cols = slice(gi * LANE, (gi + 1) * LANE)
            dp = _dot_nt(dmixed[:, cols], wpool_ref[gi])
            dwpool_ref[gi] += _dot_tn(pooled[:, cols], dmixed[:, cols])
            cnt = jnp.minimum(pos + 1, w).astype(F32)
            qext[0:tm, cols] = dp / cnt
            dpooled_cols.append(dp)
        dxa_cols = []
        for gi, w in enumerate(POOL_WINDOWS):
            cols = slice(gi * LANE, (gi + 1) * LANE)
            s = qext[0:tm, cols]
            for k in range(1, w):
                s = s + qext[k:k + tm, cols]
            dxa_cols.append(s - dpooled_cols[gi])
        qext[tm:tm + HALO, :] = qext[0:HALO, :]

        u, sv, vl = gm["u"], gm["sv"], gm["vl"]
        du = dyb * sv
        dsv = dyb * u
        dvl_cols = []
        for p in range(4):
            cols = slice(p * LANE, (p + 1) * LANE)
            dblocks = [dsv[k * CHUNK:(k + 1) * CHUNK, cols] for k in range(nchunks)]
            vblocks = [vl[k * CHUNK:(k + 1) * CHUNK, cols] for k in range(nchunks)]
            tot = dblocks[0]
            for b in dblocks[1:]:
                tot = tot + b
            dsv_acc[:, cols] += tot
            top, bot = _pair_rhs(dblocks)
            out = _dot(wtcat_ref[p], jnp.concatenate([top, bot], axis=0).astype(BF16))
            dvl_cols.append(jnp.concatenate([out[:, k * LANE:(k + 1) * LANE] for k in range(nchunks)], axis=0))
            vcat = jnp.concatenate(vblocks, axis=1).astype(BF16)
            dwsp_ref[2 * p] += _dot_nt(top.astype(BF16), vcat)
            dwsp_ref[2 * p + 1] += _dot_nt(bot.astype(BF16), vcat)
        dvl = jnp.concatenate(dvl_cols, axis=1)
        vhat, rstd = gm["vhat"], gm["rstd"]
        v512_ref[1:2, :] += jnp.sum(dvl * vhat, axis=0, keepdims=True)
        v512_ref[2:3, :] += jnp.sum(dvl, axis=0, keepdims=True)
        dvh = dvl * vecs[1:2, :]
        dv = rstd * (dvh - jnp.mean(dvh, axis=-1, keepdims=True)
                     - vhat * jnp.mean(dvh * vhat, axis=-1, keepdims=True))
        dzb = jnp.concatenate([du, dv], axis=1) * gm["dz"]

        dproj = jnp.concatenate(dxa_cols + [dzb], axis=1).astype(BF16)
        dwin_ref[...] += _dot_tn(hb, dproj)
        dh = _dot_nt(dproj, win_ref[...])
        dsh, dsc, dgn, dxin = _norm_mod_bwd(dh, r, xn, hp, gn, sc)
        vd_ref[0:1, :] += dsh
        vd_ref[1:2, :] += dsc
        vd_ref[3:4, :] += dgn
        dx_ref[...] = dxo + dxin

        @pl.when(step == nt - 1)
        def _():
            gw = dwout_ref[...]
            vd_ref[2:3, :] += jnp.sum(wout_ref[...].astype(F32) * gw, axis=0, keepdims=True)
            dwout_ref[...] = gw * gate
            row = lax.broadcasted_iota(jnp.int32, (CHUNK, CHUNK), 0)
            col = lax.broadcasted_iota(jnp.int32, (CHUNK, CHUNK), 1)
            for hh in range(8):
                dwsp_ref[hh] = jnp.where(col <= row, dwsp_ref[hh], 0.0)
            head = lax.broadcasted_iota(jnp.int32, (8, DG), 0)
            ch = lax.broadcasted_iota(jnp.int32, (8, DG), 1)
            spread = jnp.where(ch // 64 == head, 1.0, 0.0).astype(F32)
            dbsp_ref[...] = lax.dot_general(spread, dsv_acc[...], (((1,), (1,)), ((), ())),
                                            precision=HIGHEST, preferred_element_type=F32)

    full = lambda shape: pl.BlockSpec(shape, lambda s: (0,) * len(shape))
    rev = lambda cols: pl.BlockSpec((tm, cols), lambda s: (nt - 1 - s, 0))
    step = lambda s: lambda: pl.program_id(0) == s
    return _side_call(
        body, side, (step(0), None, step(nt - 1)), name="mix_bwd",
        grid=(nt,),
        in_specs=[rev(D), rev(D), rev(DP), rev(2 * DG), full((8, D)), full((D, DPROJ)), full((4, LANE, LANE)),
                  full((8, DP)), full((4, CHUNK, 2 * CHUNK)), full((4, CHUNK, 2 * CHUNK)), full((CHUNK, DG)),
                  full((DP + DG, D))],
        out_specs=[rev(D), full((D, DPROJ)), full((DP + DG, D)), full((4, LANE, LANE)), full((8, CHUNK, CHUNK)),
                   full((8, CHUNK)), full((8, DP)), full((8, D))],
        out_shape=[_sds((S, D), F32), _sds((D, DPROJ), F32), _sds((DP + DG, D), F32), _sds((4, LANE, LANE), F32),
                   _sds((8, CHUNK, CHUNK), F32), _sds((8, CHUNK), F32), _sds((8, DP), F32), _sds((8, D), F32)],
        scratch_shapes=[pltpu.VMEM((tm + HALO, DP), F32), pltpu.VMEM((CHUNK, DG), F32)],
        compiler_params=_params(("arbitrary",), VMEM_LIMIT),
        args=(x, dxo, pooled, zb, modv, win, wpool, vecs, wcat, wtcat, bias, wout))


def _chip_sum(g, rbuf, core):
    _, _, hr, cols = g.shape
    tr = _row_block(hr)

    def body(c_ref, g_ref, r_ref, o_ref):
        o_ref[...] = (g_ref[...] + r_ref[...]).astype(BF16)

    return pl.pallas_call(
        body, name="chip_sum", interpret=False,
        grid_spec=pltpu.PrefetchScalarGridSpec(
            num_scalar_prefetch=1, grid=(NQ, hr // tr),
            in_specs=[pl.BlockSpec((None, None, tr, cols), lambda q, i, c: (q, c[0], i, 0)),
                      pl.BlockSpec((None, tr, cols), lambda q, i, c: (q, i, 0))],
            out_specs=pl.BlockSpec((None, tr, cols), lambda q, i, c: (q, i, 0))),
        out_shape=_sds((NQ, hr, cols), BF16),
        compiler_params=_params(("arbitrary", "arbitrary"), None),
    )(core, g, rbuf)


def _chip_sum_pair(own, rbuf, slots=(0, 1, 2, 3), prev=None):
    _, hr, cols = own.shape
    tr = _row_block(hr)
    a, b = slots[0], (slots[1] - slots[0] if len(slots) > 1 else 0)
    assert list(slots) == [a + b * k for k in range(len(slots))]

    def body(a_ref, b_ref, *rest):
        rest[-1][...] = (a_ref[...] + b_ref[...]).astype(BF16)

    spec = pl.BlockSpec((None, tr, cols), lambda k, i: (a + b * k, i, 0))
    return _call(
        body, name="chip_sum_pair",
        grid=(len(slots), hr // tr),
        in_specs=[spec, spec] + ([ANY] if prev is not None else []), out_specs=spec,
        out_shape=_sds((NQ, hr, cols), BF16),
        input_output_aliases={2: 0} if prev is not None else {},
        compiler_params=_params(("arbitrary", "arbitrary"), None),
    )(own, rbuf, *([prev] if prev is not None else []))


def _sum4(cs, rbuf, chip, after=()):
    _, hr, cols = rbuf.shape
    tr = _row_block(hr)

    def body(q_ref, c_ref, r1_ref, r2_ref, r3_ref, *rest):
        acc = c_ref[...].astype(F32)
        for r in (r1_ref, r2_ref, r3_ref):
            acc = acc + r[...].astype(F32)
        rest[-1][...] = acc

    slot = lambda k: pl.BlockSpec((None, tr, cols), lambda i, q: ((q[0] + k) % NQ, i, 0))
    return pl.pallas_call(
        body, name="sum4", interpret=False,
        grid_spec=pltpu.PrefetchScalarGridSpec(
            num_scalar_prefetch=1, grid=(hr // tr,),
            in_specs=[slot(0), slot(1), slot(2), slot(3)] + [ANY] * len(after),
            out_specs=pl.BlockSpec((tr, cols), lambda i, q: (i, 0))),
        out_shape=_sds((hr, cols), F32),
        compiler_params=_params(("arbitrary",), None),
    )(chip, cs, rbuf, rbuf, rbuf, *after)


def _adamw_halves(w, own, recv, m, v, side=None, after=()):
    rows, cols = w.shape
    hr = rows // 2
    tr = _row_block(hr, mult=8)
    nb = hr // tr

    def body(w_ref, own_ref, recv_ref, m_ref, v_ref, *rest):
        g_ref, d_ref, mo_ref, vo_ref = rest[len(after):]
        g = jnp.where(pl.program_id(0) == lax.axis_index("c"), own_ref[...], recv_ref[...])
        d, mn, vn = _adamw(w_ref[...], g, m_ref[...], v_ref[...])
        g_ref[...] = g
        d_ref[...] = d
        mo_ref[...] = mn
        vo_ref[...] = vn

    full = pl.BlockSpec((tr, cols), lambda h, i: (h * nb + i, 0))
    half = pl.BlockSpec((tr, cols), lambda h, i: (i, 0))
    step = lambda h, i: lambda: (pl.program_id(0) == h) & (pl.program_id(1) == i)
    return _side_call(
        body, side, (step(0, 0), None, step(1, nb - 1)), name="adamw_halves",
        grid=(2, nb), in_specs=[full, half, half, full, full] + [ANY] * len(after), out_specs=[full] * 4,
        out_shape=[_sds((rows, cols), F32)] * 4, scratch_shapes=[],
        compiler_params=_params(("arbitrary", "arbitrary"), None),
        args=(w, own, recv, m, v, *after))


def _cast_place(w, chip):
    rows, cols = w.shape
    tr = _row_block(rows)

    def body(q_ref, w_ref, o_ref):
        o_ref[...] = w_ref[...].astype(BF16)

    return pl.pallas_call(
        body, name="cast_place", interpret=False,
        grid_spec=pltpu.PrefetchScalarGridSpec(
            num_scalar_prefetch=1, grid=(rows // tr,),
            in_specs=[pl.BlockSpec((tr, cols), lambda i, q: (i, 0))],
            out_specs=pl.BlockSpec((None, tr, cols), lambda i, q: (q[0], i, 0))),
        out_shape=_sds((NQ, rows, cols), BF16),
        compiler_params=_params(("arbitrary",), None),
    )(chip, w)


def _ada_grad_adamw(cact_t, dmod_q, w, m, v, side=None):
    rows, cols = w.shape
    tc = 256
    assert cols % tc == 0

    def body(c_ref, d_ref, w_ref, m_ref, v_ref, g_ref, dl_ref, mo_ref, vo_ref):
        g = jnp.dot(c_ref[...], d_ref[...], precision=HIGHEST, preferred_element_type=F32)
        d, mn, vn = _adamw(w_ref[...], g, m_ref[...], v_ref[...])
        g_ref[...] = g
        dl_ref[...] = d
        mo_ref[...] = mn
        vo_ref[...] = vn

    spec = pl.BlockSpec((rows, tc), lambda i: (0, i))
    step = lambda s: lambda: pl.program_id(0) == s
    return _side_call(
        body, side, (step(0), None, step(cols // tc - 1)), name="ada_grad_adamw",
        grid=(cols // tc,),
        in_specs=[pl.BlockSpec((rows, 8), lambda i: (0, 0)), pl.BlockSpec((8, tc), lambda i: (0, i)),
                  spec, spec, spec],
        out_specs=[spec] * 4,
        out_shape=[_sds((rows, cols), F32)] * 4,
        scratch_shapes=[],
        compiler_params=_params(("arbitrary",), None),
        args=(cact_t, dmod_q, w, m, v))


def _me():
    x, y, c = lax.axis_index("x"), lax.axis_index("y"), lax.axis_index("c")
    return x, y, c


_OFFSETS7 = [(dx, dy, dc) for dx in (0, 1) for dy in (0, 1) for dc in (0, 1) if (dx, dy, dc) != (0, 0, 0)]
_CHIP_OFFSETS = [(1, 0), (0, 1), (1, 1)]


def _ada_fwd(c, w_ada_q, b_ada_q, side=None):
    ncol = w_ada_q.shape[1]

    def body(c_ref, w_ref, b_ref, cact_ref, modsel_ref, blk, gath, res, parts, send_sems, recv_sems, side_start=None):
        x, y, cc = _me()
        me = 4 * x + 2 * y + cc
        q = 2 * x + y
        cv = c_ref[...]
        ca = cv * jax.nn.sigmoid(cv)
        row = lax.broadcasted_iota(jnp.int32, (8, D), 0)
        blk[...] = jnp.where(row == me, jnp.broadcast_to(ca, (8, D)), 0.0)
        gath[me] = blk[...]
        sends = []
        for k, (dx, dy, dc) in enumerate(_OFFSETS7):
            cp = pltpu.make_async_remote_copy(blk, gath.at[me], send_sems.at[k], recv_sems.at[k],
                                              device_id=(x ^ dx, y ^ dy, cc ^ dc), device_id_type=MESH)
            cp.start()
            sends.append(cp)
        if side_start is not None:
            side_start()
        for cp in sends:
            cp.wait_recv()
        cact = gath[0]
        for d in range(1, N_DEV):
            cact = cact + gath[d]
        cact_ref[...] = cact
        res[...] = jnp.dot(cact, w_ref[...], precision=HIGHEST, preferred_element_type=F32) + b_ref[...]
        parts[q] = res[...]
        sends2 = []
        for k, (dx, dy) in enumerate(_CHIP_OFFSETS):
            cp = pltpu.make_async_remote_copy(res, parts.at[q], send_sems.at[7 + k], recv_sems.at[7 + k],
                                              device_id=(x ^ dx, y ^ dy, cc), device_id_type=MESH)
            cp.start()
            sends2.append(cp)
        for cp in sends2:
            cp.wait_recv()
        row2 = lax.broadcasted_iota(jnp.int32, (8, ncol), 0)
        out = jnp.zeros((8, ncol), F32)
        for s in range(NQ):
            mine = jnp.sum(jnp.where(row2 == me, parts[s], 0.0), axis=0, keepdims=True)
            out = out + jnp.where(row2 == s, jnp.broadcast_to(mine, (8, ncol)), 0.0)
        modsel_ref[...] = out
        for cp in sends + sends2:
            cp.wait_send()

    return _side_call(
        body, side, None, name="ada_fwd",
        in_specs=[VMEM, VMEM, VMEM], out_specs=[VMEM, VMEM],
        out_shape=[_sds((8, D), F32), _sds((8, ncol), F32)],
        scratch_shapes=[pltpu.VMEM((8, D), F32), pltpu.VMEM((N_DEV, 8, D), F32), pltpu.VMEM((8, ncol), F32),
                        pltpu.VMEM((NQ, 8, ncol), F32), pltpu.SemaphoreType.DMA((10,)), pltpu.SemaphoreType.DMA((10,))],
        compiler_params=_params(None, VMEM_LIMIT), start_in_body=side is not None,
        args=(c, w_ada_q, b_ada_q))


class _Side:
    def __init__(self, ins, out_shapes, aliases, nsem, start, mid=None, finish=None):
        self.ins, self.out_shapes, self.aliases, self.nsem = list(ins), list(out_shapes), dict(aliases), nsem
        self.start, self.mid, self.finish = start, mid, finish


def _join(*sides):
    ins, outs, aliases, offs, nsem = [], [], {}, [], 0
    for s in sides:
        offs.append((len(ins), len(outs), nsem))
        aliases.update({len(ins) + a: len(outs) + b for a, b in s.aliases.items()})
        ins += s.ins
        outs += s.out_shapes
        nsem += s.nsem

    def hook(name):
        def run(i, o, ss, rs, base):
            for s, (io, oo, so) in zip(sides, offs):
                fn = getattr(s, name)
                if fn is not None:
                    fn(i[io:io + len(s.ins)], o[oo:oo + len(s.out_shapes)], ss, rs, base + so)
        return run

    return _Side(ins, outs, aliases, nsem, hook("start"), hook("mid"), hook("finish"))


def _side_call(body, side, when, *, name, in_specs, out_specs, out_shape, scratch_shapes, args, aliases=None,
               start_in_body=False, **kw):
    n_in, n_out = len(in_specs), len(out_specs)
    aliases = dict(aliases or {})
    if side is None:
        return _call(body, name=name, in_specs=in_specs, out_specs=out_specs, out_shape=out_shape,
                     scratch_shapes=scratch_shapes, input_output_aliases=aliases, **kw)(*args), []
    ns_in, ns_out = len(side.ins), len(side.out_shapes)

    def hook(fn, k, operands):
        if fn is None:
            return
        if when is None:
            fn(*operands, 0)
        elif when[k] is not None:
            pl.when(when[k]())(functools.partial(fn, *operands, 0))

    def wrapped(*refs):
        ins, s_ins = refs[:n_in], refs[n_in:n_in + ns_in]
        o0 = n_in + ns_in
        outs, s_outs = refs[o0:o0 + n_out], refs[o0 + n_out:o0 + n_out + ns_out]
        rest = refs[o0 + n_out + ns_out:]
        scratch, operands = rest[:-2], (s_ins, s_outs, rest[-2], rest[-1])
        if start_in_body:
            body(*ins, *outs, *scratch, side_start=functools.partial(hook, side.start, 0, operands))
        else:
            hook(side.start, 0, operands)
            body(*ins, *outs, *scratch)
        hook(side.mid, 1, operands)
        hook(side.finish, 2, operands)

    res = _call(
        wrapped, name=name,
        in_specs=list(in_specs) + [ANY] * ns_in, out_specs=list(out_specs) + [ANY] * ns_out,
        out_shape=list(out_shape) + side.out_shapes,
        scratch_shapes=list(scratch_shapes) + [pltpu.SemaphoreType.DMA((side.nsem,)),
                                               pltpu.SemaphoreType.DMA((side.nsem,))],
        input_output_aliases={**aliases, **{n_in + a: n_out + b for a, b in side.aliases.items()}},
        **kw)(*args, *side.ins)
    return res[:n_out], res[n_out:]


def _run_side(side, name):
    return _side_call(lambda: None, side, None, name=name, in_specs=[], out_specs=[], out_shape=[],
                      scratch_shapes=[], args=[])[1]


def _remote(src, dst, ss, rs, k, dev):
    return pltpu.make_async_remote_copy(src, dst, ss.at[k], rs.at[k], device_id=dev, device_id_type=MESH)


def _gather_side(bufs):
    n = len(bufs)

    def plan(outs, w):
        x, y, cc = _me()
        hr = outs[w].shape[1] // 2
        mine, other = cc * hr, (1 - cc) * hr
        qx, qy, qd, q = 2 * (x ^ 1) + y, 2 * x + (y ^ 1), 2 * (x ^ 1) + (y ^ 1), 2 * x + y
        xn, yn, sib = (x ^ 1, y, cc), (x, y ^ 1, cc), (x, y, 1 - cc)
        at = lambda slot, r0, nr: outs[w].at[slot, pl.ds(r0, nr)]
        send = [(at(q, mine, hr), xn), (at(q, mine, hr), yn),
                (at(qx, mine, hr // 2), yn), (at(qy, mine + hr // 2, hr // 2), xn),
                (at(qx, mine, hr), sib), (at(qy, mine, hr), sib), (at(qd, mine, hr), sib)]
        recv = [at(qx, mine, hr), at(qy, mine, hr), at(qd, mine, hr // 2), at(qd, mine + hr // 2, hr // 2),
                at(qx, other, hr), at(qy, other, hr), at(qd, other, hr)]
        return send, recv

    def op(outs, ss, rs, b, w, k, what):
        send, recv = plan(outs, w)
        if what == "wait_recv":
            _remote(recv[k], recv[k], ss, rs, b + 7 * w + k, send[k][1]).wait_recv()
        else:
            getattr(_remote(send[k][0], send[k][0], ss, rs, b + 7 * w + k, send[k][1]), what)()

    def start(ins, outs, ss, rs, b):
        for w in range(n):
            for k in (0, 1):
                op(outs, ss, rs, b, w, k, "start")

    def mid(ins, outs, ss, rs, b):
        for w in range(n):
            for k in (0, 1):
                op(outs, ss, rs, b, w, k, "wait_recv")
                op(outs, ss, rs, b, w, 2 + k, "start")
                op(outs, ss, rs, b, w, 4 + k, "start")

    def finish(ins, outs, ss, rs, b):
        for w in range(n):
            for k in (2, 3):
                op(outs, ss, rs, b, w, k, "wait_recv")
            op(outs, ss, rs, b, w, 6, "start")
        for w in range(n):
            for k in (4, 5, 6):
                op(outs, ss, rs, b, w, k, "wait_recv")
            for k in range(7):
                op(outs, ss, rs, b, w, k, "wait_send")

    return _Side(bufs, [_sds(tuple(w.shape), w.dtype) for w in bufs], {i: i for i in range(n)}, 7 * n,
                 start, mid, finish)


def _copies_side(ins, out_shapes, nsem, copies):
    def start(*a):
        for cp in copies(*a):
            cp.start()

    def finish(*a):
        for cp in copies(*a):
            cp.wait()

    return _Side(ins, out_shapes, {}, nsem, start, None, finish)


def _swap_side(gs):
    def copies(ins, outs, ss, rs, b):
        x, y, cc = _me()
        return [_remote(ins[w].at[:, 1 - cc], outs[w], ss, rs, b + w, (x, y, 1 - cc)) for w in range(len(gs))]

    return _copies_side(gs, [_sds((NQ,) + tuple(g.shape[2:]), F32) for g in gs], len(gs), copies)


def _exchange_side(cs, slots=None, prev=None):
    n = len(cs)
    slots = slots or [(0, 1, 2, 3)] * n

    def among(chip, allowed):
        hit = chip == allowed[0]
        for s in allowed[1:]:
            hit = hit | (chip == s)
        return hit

    def each(ins, outs, ss, rs, b, do_send, do_recv):
        x, y, cc = _me()
        q = 2 * x + y
        for w in range(n):
            for j, (dx, dy) in enumerate(_CHIP_OFFSETS):
                pq = 2 * (x ^ dx) + (y ^ dy)
                cp = _remote(ins[w].at[pq], outs[w].at[q], ss, rs, b + 3 * w + j, (x ^ dx, y ^ dy, cc))
                if do_send is not None:
                    pl.when(among(pq, slots[w]))(functools.partial(do_send, cp))
                if do_recv is not None:
                    pl.when(among(q, slots[w]))(functools.partial(do_recv, cp))

    def start(ins, outs, ss, rs, b):
        each(ins, outs, ss, rs, b, lambda cp: cp.start(), None)

    def finish(ins, outs, ss, rs, b):
        each(ins, outs, ss, rs, b, lambda cp: cp.wait_send(), lambda cp: cp.wait_recv())

    ins = list(cs) + (list(prev) if prev is not None else [])
    aliases = {n + w: w for w in range(n)} if prev is not None else {}
    return _Side(ins, [_sds(tuple(c.shape), c.dtype) for c in cs], aliases, 3 * n, start, None, finish)


def _exchange_copies(srcs, lands, send_sems, recv_sems):
    x, y, cc = _me()
    return [pltpu.make_async_remote_copy(srcs[w].at[2 * (x ^ dx) + (y ^ dy)], lands[w].at[2 * x + y],
                                         send_sems.at[3 * w + j], recv_sems.at[3 * w + j],
                                         device_id=(x ^ dx, y ^ dy, cc), device_id_type=MESH)
            for w in range(len(srcs)) for j, (dx, dy) in enumerate(_CHIP_OFFSETS)]


def _exchange_start(cs, after=()):
    n = len(cs)
    hbm, sem = pl.BlockSpec(memory_space=pltpu.HBM), pl.BlockSpec(memory_space=pltpu.SEMAPHORE)
    srcs = [pltpu.with_memory_space_constraint(c, pltpu.HBM) for c in cs]
    lands = [pltpu.with_memory_space_constraint(lax.empty(c.shape, c.dtype), pltpu.HBM) for c in cs]

    def body(*refs):
        sems = 2 * n + len(after)
        for cp in _exchange_copies(refs[:n], refs[n:2 * n], refs[sems], refs[sems + 1]):
            cp.start()
        refs[-1][...] = jnp.zeros_like(refs[-1])

    res = pl.pallas_call(
        body, name="exchange_start", interpret=False,
        out_shape=(pltpu.SemaphoreType.DMA((3 * n,)), pltpu.SemaphoreType.DMA((3 * n,)),
                   *[pltpu.HBM(c.shape, c.dtype) for c in cs], *[pltpu.HBM(c.shape, c.dtype) for c in cs],
                   _sds((8, LANE), F32)),
        in_specs=(hbm,) * (2 * n) + (ANY,) * len(after), out_specs=(sem, sem) + (hbm,) * (2 * n) + (VMEM,),
        input_output_aliases={i: 2 + i for i in range(2 * n)},
        compiler_params=pltpu.CompilerParams(has_side_effects=pltpu.SideEffectType.DATAFLOW_SIDE_EFFECTING),
    )(*srcs, *lands, *after)
    return res[0], res[1], list(res[2:2 + n]), list(res[2 + n:2 + 2 * n]), res[-1]


def _exchange_wait(send_sems, recv_sems, srcs, lands, after):
    n = len(srcs)
    hbm, sem = pl.BlockSpec(memory_space=pltpu.HBM), pl.BlockSpec(memory_space=pltpu.SEMAPHORE)

    def body(*refs):
        for cp in _exchange_copies(refs[:n], refs[n:2 * n], refs[2 * n], refs[2 * n + 1]):
            cp.wait_send()
            cp.wait_recv()

    res = pl.pallas_call(
        body, name="exchange_wait", interpret=False,
        out_shape=[pltpu.HBM(c.shape, c.dtype) for c in srcs + lands],
        in_specs=(hbm,) * (2 * n) + (sem, sem) + (ANY,) * len(after), out_specs=(hbm,) * (2 * n),
        input_output_aliases={i: i for i in range(2 * n)},
        compiler_params=pltpu.CompilerParams(has_side_effects=pltpu.SideEffectType.DATAFLOW_SIDE_EFFECTING),
    )(*srcs, *lands, send_sems, recv_sems, *after)
    return list(res[:n]), list(res[n:])


def _share_side(fs):
    def copies(ins, outs, ss, rs, b):
        x, y, cc = _me()
        return [_remote(ins[w], outs[w], ss, rs, b + w, (x, y, 1 - cc)) for w in range(len(fs))]

    return _copies_side(fs, [_sds(tuple(f.shape), F32) for f in fs], len(fs), copies)


def _small_allreduce_adamw(g, w, m, v, nd):
    rows = g.shape[0]
    nr = rows - nd
    hr = nr // 2
    assert nd % 8 == 0 and hr % 8 == 0

    def body(g_ref, w_ref, m_ref, v_ref, gs_ref, d_ref, mo_ref, vo_ref, gath, sib, csum, slots, tot, ss, rs):
        x, y, cc = _me()
        me = 4 * x + 2 * y + cc
        q = 2 * x + y
        sibling = (x, y, 1 - cc)
        dm = g_ref.at[pl.ds(0, nd)]
        gath[me] = g_ref[0:nd, :]
        to_all = [_remote(dm, gath.at[me], ss, rs, k, (x ^ dx, y ^ dy, cc ^ dc)) for k, (dx, dy, dc) in enumerate(_OFFSETS7)]
        to_sib = _remote(g_ref.at[pl.ds(nd, nr)], sib, ss, rs, 7, sibling)
        for cp in to_all + [to_sib]:
            cp.start()
        to_sib.wait_recv()
        csum[...] = g_ref[nd:, :] + sib[...]
        mine = pl.ds(pl.multiple_of(cc * hr, 8), hr)
        slots[q] = csum[mine, :]
        to_chips = [_remote(csum.at[mine], slots.at[q], ss, rs, 8 + j, (x ^ dx, y ^ dy, cc))
                    for j, (dx, dy) in enumerate(_CHIP_OFFSETS)]
        for cp in to_chips:
            cp.start()
        for cp in to_chips:
            cp.wait_recv()
        tot[mine, :] = (slots[0] + slots[1]) + (slots[2] + slots[3])
        halves = _remote(tot.at[mine], tot.at[mine], ss, rs, 11, sibling)
        halves.start()
        for cp in to_all:
            cp.wait_recv()
        dsum = gath[0]
        for dev in range(1, N_DEV):
            dsum = dsum + gath[dev]
        halves.wait_recv()
        for lo, n, total in ((0, nd, dsum), (nd, nr, tot[...])):
            gs_ref[lo:lo + n, :] = total
            d, mn, vn = _adamw(w_ref[lo:lo + n, :], total, m_ref[lo:lo + n, :], v_ref[lo:lo + n, :])
            d_ref[lo:lo + n, :] = d
            mo_ref[lo:lo + n, :] = mn
            vo_ref[lo:lo + n, :] = vn
        for cp in to_all + [to_sib, halves] + to_chips:
            cp.wait_send()

    return _call(
        body, name="small_allreduce_adamw",
        in_specs=[VMEM] * 4, out_specs=[VMEM] * 5,
        out_shape=[_sds((rows, LANE), F32)] * 4 + [_sds((N_DEV, nd, LANE), F32)],
        scratch_shapes=[pltpu.VMEM((nr, LANE), F32), pltpu.VMEM((nr, LANE), F32), pltpu.VMEM((NQ, hr, LANE), F32),
                        pltpu.VMEM((nr, LANE), F32), pltpu.SemaphoreType.DMA((12,)), pltpu.SemaphoreType.DMA((12,))],
        compiler_params=_params(None, VMEM_LIMIT),
    )(g, w, m, v)


_SMALL = ["b_ada", "norm_ffn1_g", "norm_mix_g", "pool_scale", "gmlp_ln_g", "gmlp_ln_b", "b_spatial",
          "norm_ffn2_g", "norm_final_g", "w_pool", "w_spatial"]


def _pack(parts):
    blocks, layout, r0 = [], {}, 0
    for name in _SMALL:
        a = parts[name]
        n = a.size
        rows = -(-n // LANE)
        rows8 = -(-rows // 8) * 8
        flat = a.reshape(-1).astype(F32)
        if rows8 * LANE != n:
            flat = jnp.concatenate([flat, jnp.zeros((rows8 * LANE - n,), F32)])
        blocks.append(flat.reshape(rows8, LANE))
        layout[name] = (r0, n, a.shape)
        r0 += rows8
    return jnp.concatenate(blocks, axis=0), layout


def _unpack(packed, layout):
    out = {}
    for name, (r0, n, shape) in layout.items():
        rows = -(-n // LANE)
        out[name] = packed[r0:r0 + rows].reshape(-1)[:n].reshape(shape)
    return out


def _modv(mod9, sub, gain):
    rows = jnp.concatenate([mod9[3 * sub:3 * sub + 3], gain.reshape(1, D), jnp.zeros((4, D), F32)], axis=0)
    return rows


_BIG = ["ffn1_w_in", "ffn1_w_out", "w_mix_in", "w_mix_out", "ffn2_w_in", "ffn2_w_out"]


def kernel(x, c, w_ada, b_ada, norm_ffn1_g, ffn1_w_in, ffn1_w_out, norm_mix_g, w_mix_in, w_pool, pool_scale, gmlp_ln_g, gmlp_ln_b, w_spatial, b_spatial, w_mix_out, norm_ffn2_g, ffn2_w_in, ffn2_w_out, norm_final_g, loss_target, m_w_ada, m_b_ada, m_norm_ffn1_g, m_ffn1_w_in, m_ffn1_w_out, m_norm_mix_g, m_w_mix_in, m_w_pool, m_pool_scale, m_gmlp_ln_g, m_gmlp_ln_b, m_w_spatial, m_b_spatial, m_w_mix_out, m_norm_ffn2_g, m_ffn2_w_in, m_ffn2_w_out, m_norm_final_g, v_w_ada, v_b_ada, v_norm_ffn1_g, v_ffn1_w_in, v_ffn1_w_out, v_norm_mix_g, v_w_mix_in, v_w_pool, v_pool_scale, v_gmlp_ln_g, v_gmlp_ln_b, v_w_spatial, v_b_spatial, v_w_mix_out, v_norm_ffn2_g, v_ffn2_w_in, v_ffn2_w_out, v_norm_final_g):
    names = ["w_ada", "b_ada", "norm_ffn1_g", "ffn1_w_in", "ffn1_w_out", "norm_mix_g", "w_mix_in", "w_pool",
             "pool_scale", "gmlp_ln_g", "gmlp_ln_b", "w_spatial", "b_spatial", "w_mix_out", "norm_ffn2_g",
             "ffn2_w_in", "ffn2_w_out", "norm_final_g"]
    W = dict(zip(names, [w_ada, b_ada, norm_ffn1_g, ffn1_w_in, ffn1_w_out, norm_mix_g, w_mix_in, w_pool, pool_scale,
                         gmlp_ln_g, gmlp_ln_b, w_spatial, b_spatial, w_mix_out, norm_ffn2_g, ffn2_w_in, ffn2_w_out,
                         norm_final_g]))
    M = dict(zip(names, [m_w_ada, m_b_ada, m_norm_ffn1_g, m_ffn1_w_in, m_ffn1_w_out, m_norm_mix_g, m_w_mix_in, m_w_pool,
                         m_pool_scale, m_gmlp_ln_g, m_gmlp_ln_b, m_w_spatial, m_b_spatial, m_w_mix_out, m_norm_ffn2_g,
                         m_ffn2_w_in, m_ffn2_w_out, m_norm_final_g]))
    V = dict(zip(names, [v_w_ada, v_b_ada, v_norm_ffn1_g, v_ffn1_w_in, v_ffn1_w_out, v_norm_mix_g, v_w_mix_in, v_w_pool,
                         v_pool_scale, v_gmlp_ln_g, v_gmlp_ln_b, v_w_spatial, v_b_spatial, v_w_mix_out, v_norm_ffn2_g,
                         v_ffn2_w_in, v_ffn2_w_out, v_norm_final_g]))

    xi, yi, ci = _me()
    q = 2 * xi + yi
    core = ci.astype(jnp.int32).reshape(1)

    chip = q.astype(jnp.int32).reshape(1)
    place = lambda n: _cast_place(W[n][0], chip)

    ncol = w_ada.shape[2]
    b_q = lax.dynamic_slice_in_dim(b_ada, q * ncol, ncol, axis=1)
    (cact_all, modsel), (win1, wout1) = _ada_fwd(
        c, w_ada[0], b_q, side=_gather_side([place("ffn1_w_in"), place("ffn1_w_out")]))
    mod9 = modsel[:NQ].reshape(9, D)
    xs, target = x[0], loss_target[0]
    mv1 = _modv(mod9, 0, norm_ffn1_g[0])
    mv2 = _modv(mod9, 1, norm_mix_g[0])
    mv3 = _modv(mod9, 2, norm_ffn2_g[0])
    wcat, wtcat, bias = _prep_spatial(w_spatial[0], b_spatial[0].T)
    wpool = w_pool[0].astype(BF16)
    vecs = jnp.concatenate([pool_scale, gmlp_ln_g, gmlp_ln_b, jnp.zeros((5, DP), F32)], axis=0)
    gf = jnp.concatenate([norm_final_g.reshape(1, D), jnp.zeros((7, D), F32)], axis=0)

    later =["w_mix_in", "w_mix_out", "ffn2_w_in", "ffn2_w_out"]
    (x1, g1s, u1s), got = _ffn_fwd(xs, mv1, win1, wout1.reshape(2, CH, D), side=_gather_side([place(n) for n in later]))
    wmi, wmo, win2, wout2 = got
    wmi = jnp.transpose(wmi, (1, 0, 2)).reshape(D, DPROJ)
    wmo = wmo.reshape(DP + DG, D)
    x2, pooled, zb = _mix_fwd(x1, mv2, wmi, wpool, vecs, wcat, bias, wmo)
    (dx3, g3s, u3s, loss_blk, dgf), _ = _ffn_fwd(x2, mv3, win2, wout2.reshape(2, CH, D), head=(target, gf))

    wo1, wo2 = wout1.reshape(2, CH, D), wout2.reshape(2, CH, D)
    (dx2, oin2, oout2, rin2, rout2, vec3), _ = _ffn_bwd(x2, dx3, g3s, u3s, mv3, win2, wo2)
    cs2 = [_chip_sum_pair(oin2, rin2), _chip_sum_pair(oout2, rout2)]
    (dx1, dwmi, dwmo, dwpool, dwsp, dbsp, v512, vec2), ex2 = _mix_bwd(
        x1, dx2, pooled, zb, mv2, wmi, wpool, vecs, wcat, wtcat, bias, wmo, side=_exchange_side(cs2))
    half2 = [_sum4(cs, e, chip) for cs, e in zip(cs2, ex2)]
    qcols = w_mix_in.shape[2]
    vmix = [jnp.transpose(dwmi.reshape(D, NQ, qcols), (1, 0, 2)).reshape(NQ, 2, D // 2, qcols),
            dwmo.reshape(NQ, 2, (DP + DG) // 8, D)]
    first1, got = _ffn_bwd_pass(0, xs, dx1, g1s, u1s, mv1, win1, wo1,
                                side=_join(_swap_side(vmix), _share_side(half2)))
    sibmix, other2 = got[:2], got[2:]
    cs_mix = [_chip_sum(g, r, core) for g, r in zip(vmix, sibmix)]
    (grad_x, oin1, oout1, rin1, rout1, vec1), ex_mix = _ffn_bwd_pass(
        1, xs, dx1, g1s, u1s, mv1, win1, wo1, prev=first1[:5], side=_exchange_side(cs_mix))
    vec1 = first1[5] + vec1
    cs_ffn1 = [_chip_sum_pair(oin1, rin1), _chip_sum_pair(oout1, rout1)]

    dmod =jnp.concatenate([vec1[0:3], vec2[0:3], vec3[0:3]], axis=0)
    grads = dict(
        b_ada=dmod.reshape(1, 9 * D), norm_ffn1_g=vec1[3:4], norm_mix_g=vec2[3:4], norm_ffn2_g=vec3[3:4],
        pool_scale=v512[0:1], gmlp_ln_g=v512[1:2], gmlp_ln_b=v512[2:3], b_spatial=dbsp[None],
        norm_final_g=dgf[0], w_pool=dwpool[None], w_spatial=dwsp[None])

    gp, layout = _pack({n: grads[n] for n in _SMALL})
    gp = jnp.concatenate([gp, loss_blk, loss_blk], axis=0)
    pad = jnp.zeros((16, LANE), F32)
    wp, mp, vp = [jnp.concatenate([_pack({n: src[n] for n in _SMALL})[0], pad], axis=0) for src in (W, M, V)]
    r0, nb, _ = layout["b_ada"]
    assert r0 == 0
    out_g, out_d, out_m, out_v = {}, {}, {}, {}
    gs, dl, mo, vo, gath = _small_allreduce_adamw(gp, wp, mp, vp, nb // LANE)
    loss = gs[-16, 0]
    for packed, dst in ((gs, out_g), (dl, out_d), (mo, out_m), (vo, out_v)):
        for n, a in _unpack(packed, layout).items():
            dst[n] = a.reshape(W[n].shape)

    def update(n, own, recv, after=()):
        (g2, d, mn, vn), _ = _adamw_halves(W[n][0], own, recv, M[n][0], V[n][0], after=after)
        out_g[n], out_d[n], out_m[n], out_v[n] = g2[None], d[None], mn[None], vn[None]
        return g2

    ssem, rsem, cs_fly, land_fly, token = _exchange_start(cs_ffn1, after=(gath,))
    dmod_q = lax.dynamic_slice_in_dim(gath.reshape(N_DEV, nb), q * ncol, ncol, axis=1) + token[0:8, 0:1]
    (ga, da, ma, va), _ = _ada_grad_adamw(cact_all.T, dmod_q, w_ada[0], m_w_ada[0], v_w_ada[0])
    out_g["w_ada"], out_d["w_ada"], out_m["w_ada"], out_v["w_ada"] = ga[None], da[None], ma[None], va[None]
    done = [ga, update("ffn2_w_in", half2[0], other2[0], after=(token,)),
            update("ffn2_w_out", half2[1], other2[1], after=(token,))]
    half_mix = [_sum4(cs, e, chip, after=(token,)) for cs, e in zip(cs_mix, ex_mix)]
    other_mix = _run_side(_share_side(half_mix), "share_mix")
    done += [update(n, own, recv) for n, own, recv in zip(["w_mix_in", "w_mix_out"], half_mix, other_mix)]
    cs_ffn1, ex_ffn1 = _exchange_wait(ssem, rsem, cs_fly, land_fly, after=done)
    half1 = [_sum4(cs, e, chip) for cs, e in zip(cs_ffn1, ex_ffn1)]
    other1 = _run_side(_share_side(half1), "share_ffn1")
    for n, own, recv in zip(["ffn1_w_in", "ffn1_w_out"], half1, other1):
        update(n, own, recv)

    return (loss, grad_x[None], *[out_g[n] for n in names], *[out_d[n] for n in names],
            *[out_m[n] for n in names], *[out_v[n] for n in names])
```

```python
import functools
import math

import jax
import jax.numpy as jnp
from jax import lax
from jax.experimental import pallas as pl
from jax.experimental.pallas import tpu as pltpu

F32 = jnp.float32
BF16 = jnp.bfloat16
MESH = pl.DeviceIdType.MESH
HIGHEST = lax.Precision.HIGHEST

EPS = 1e-6
D = 1024
DFF = 2816
CH = DFF // 2
NQ = 4
DP = 512
DG = 512
DPROJ = DP + 2 * DG
POOL_WINDOWS = (2, 4, 8, 16)
HALO = 16
CHUNK = 128
LANE = 128
N_DEV = 8

ADAM_LR = 0.001
ADAM_B1 = 0.9
ADAM_B2 = 0.999
ADAM_EPS = 1e-08
ADAM_WD = 0.01
ADAM_STEP = 10

VMEM_LIMIT = 62 * 1024 * 1024

TM_FFN_FWD = 512
TM_FFN_BWD = 512
TM_MIX = 256


def _call(body, **kw):
    return pl.pallas_call(body, interpret=False, **kw)


def _params(sem=None, vmem=None):
    return pltpu.CompilerParams(dimension_semantics=sem, vmem_limit_bytes=vmem)


def _sds(shape, dtype):
    return jax.ShapeDtypeStruct(shape, dtype)


ANY = pl.BlockSpec(memory_space=pl.ANY)
VMEM = pl.BlockSpec(memory_space=pltpu.VMEM)
SMEM = pl.BlockSpec(memory_space=pltpu.SMEM)


def _norm_mod(x, gn, sc, sh):
    r = lax.rsqrt(jnp.mean(x * x, axis=-1, keepdims=True) + EPS)
    xn = x * r
    hp = xn * gn
    return r, xn, hp, hp * (1.0 + sc) + sh


def _norm_mod_bwd(dh, r, xn, hp, gn, sc):
    one_sc = 1.0 + sc
    dsh = jnp.sum(dh, axis=0, keepdims=True)
    dsc = jnp.sum(dh * hp, axis=0, keepdims=True)
    dgn = jnp.sum(dh * one_sc * xn, axis=0, keepdims=True)
    dxn = dh * (gn * one_sc)
    dx = r * (dxn - xn * jnp.mean(dxn * xn, axis=-1, keepdims=True))
    return dsh, dsc, dgn, dx


def _dot(a, b):
    return jnp.dot(a, b, preferred_element_type=F32)


def _dot_nt(a, b):
    return lax.dot_general(a, b, (((1,), (1,)), ((), ())), preferred_element_type=F32)


def _dot_tn(a, b):
    return lax.dot_general(a, b, (((0,), (0,)), ((), ())), preferred_element_type=F32)


_GELU_C = math.sqrt(2.0 / math.pi)
_GELU_A = 0.044715


def _gelu_fwd_bwd(x):
    x2 = x * x
    t = jnp.tanh(_GELU_C * (x + _GELU_A * x * x2))
    g = 0.5 * x * (1.0 + t)
    dg = 0.5 * (1.0 + t) + 0.5 * x * (1.0 - t * t) * (_GELU_C * (1.0 + 3.0 * _GELU_A * x2))
    return g, dg


def _adamw(w, g, m, v):
    m = ADAM_B1 * m + (1.0 - ADAM_B1) * g
    v = ADAM_B2 * v + (1.0 - ADAM_B2) * (g * g)
    m_hat = m / (1.0 - ADAM_B1 ** ADAM_STEP)
    v_hat = v / (1.0 - ADAM_B2 ** ADAM_STEP)
    delta = -ADAM_LR * (m_hat / (jnp.sqrt(v_hat) + ADAM_EPS) + ADAM_WD * w)
    return delta, m, v


def _row_block(rows, cap=256, mult=16):
    best = None
    for t in range(mult, min(rows, cap) + 1, mult):
        if rows % t == 0:
            best = t
    assert best is not None, rows
    return best


def _head_math(x, target, gf):
    r = lax.rsqrt(jnp.mean(x * x, axis=-1, keepdims=True) + EPS)
    xn = x * r
    err = xn * gf - target
    dy = err * (1.0 / D)
    dxn = dy * gf
    dx = r * (dxn - xn * jnp.mean(dxn * xn, axis=-1, keepdims=True))
    return (0.5 / D) * jnp.sum(err * err), jnp.sum(dy * xn, axis=0, keepdims=True), dx


def _ffn_fwd(x, modv, win, wout, side=None, head=None):
    S = x.shape[0]
    tm = TM_FFN_FWD
    nt = S // tm

    def body(*refs):
        if head is None:
            x_ref, mod_ref, wgu_ref, wo_ref, xo_ref, gs_ref, us_ref, acc_scr = refs
        else:
            (x_ref, mod_ref, wgu_ref, wo_ref, t_ref, gf_ref,
             xo_ref, gs_ref, us_ref, loss_ref, dgf_ref, acc_scr) = refs

        @pl.when((pl.program_id(0) == 0) & (pl.program_id(1) == 0))
        def _():
            acc_scr[...] = jnp.zeros_like(acc_scr)
            if head is not None:
                loss_ref[...] = jnp.zeros_like(loss_ref)
                dgf_ref[...] = jnp.zeros_like(dgf_ref)

        j = pl.program_id(1)
        h = _norm_mod(x_ref[...], mod_ref[3:4, :], mod_ref[1:2, :], mod_ref[0:1, :])[3].astype(BF16)
        gu = _dot(h, wgu_ref[...])
        g = gu[:, :CH].astype(BF16)
        u = gu[:, CH:].astype(BF16)
        gs_ref[...] = g
        us_ref[...] = u
        gf = g.astype(F32)
        a = (gf * jax.nn.sigmoid(gf) * u.astype(F32)).astype(BF16)
        acc = jnp.where(j == 0, 0.0, acc_scr[...]) + _dot(a, wo_ref[...])
        acc_scr[...] = acc
        xo = x_ref[...] + (0.5 * mod_ref[2:3, :]) * acc
        if head is None:
            xo_ref[...] = xo
        else:
            @pl.when(j == 1)
            def _():
                loss, dgf, dx = _head_math(xo, t_ref[...], gf_ref[0:1, :])
                loss_ref[...] += loss
                dgf_ref[0:1, :] += dgf
                xo_ref[...] = dx

    step = lambda i, j: lambda: (pl.program_id(0) == i) & (pl.program_id(1) == j)
    tile = pl.BlockSpec((tm, D), lambda i, j: (i, 0))
    const = lambda shape: pl.BlockSpec(shape, lambda i, j: (0, 0))
    chunk = pl.BlockSpec((tm, CH), lambda i, j: (i, j))
    in_specs = [tile, const((8, D)), pl.BlockSpec((None, D, 2 * CH), lambda i, j: (j, 0, 0)),
                pl.BlockSpec((None, CH, D), lambda i, j: (j, 0, 0))]
    out_specs = [tile, chunk, chunk]
    out_shape = [_sds((S, D), F32), _sds((S, DFF), BF16), _sds((S, DFF), BF16)]
    args = (x, modv, win, wout)
    if head is not None:
        in_specs += [tile, const((8, D))]
        out_specs += [const((8, LANE)), const((8, D))]
        out_shape += [_sds((8, LANE), F32), _sds((8, D), F32)]
        args += tuple(head)
    return _side_call(
        body, side, (step(0, 0), step((7 * nt) // 10, 0), step(nt - 1, 1)), name="ffn_fwd",
        grid=(nt, 2), in_specs=in_specs, out_specs=out_specs, out_shape=out_shape,
        scratch_shapes=[pltpu.VMEM((tm, D), F32)],
        compiler_params=_params(("arbitrary", "arbitrary"), VMEM_LIMIT),
        args=args)


def _ffn_bwd_pass(jj, x, dxo, gs, us, modv, win, wout, prev=None, side=None):
    S = x.shape[0]
    tm = TM_FFN_BWD
    nsub = tm // 256
    nt = S // tm
    hi, ho = D // 2, CH // 4
    last = prev is not None
    assert last == (jj == 1)

    def body(*refs):
        x_ref, dxo_ref, gs_ref, us_ref, mod_ref, win_hbm, wo_ref = refs[:7]
        k = 12 if last else 7
        out_ref, dwin_ref, dwout_ref, rwin_ref, rwout_ref, vec_ref = refs[k:k + 6]
        accgu, accw, wgu, sems, fsend, frecv = refs[k + 6:]
        i = pl.program_id(0)

        @pl.when(i == 0)
        def _():
            load = pltpu.make_async_copy(win_hbm.at[jj], wgu, sems.at[0])
            load.start()
            accgu[...] = jnp.zeros_like(accgu)
            accw[...] = jnp.zeros_like(accw)
            vec_ref[...] = jnp.zeros_like(vec_ref)
            load.wait()

        gn, sc, sh, gate = mod_ref[3:4, :], mod_ref[1:2, :], mod_ref[0:1, :], mod_ref[2:3, :]

        parts = []
        for s in range(nsub):
            rs = slice(s * (tm // nsub), (s + 1) * (tm // nsub))
            r, xn, hp, h = _norm_mod(x_ref[rs, :], gn, sc, sh)
            dxo = dxo_ref[rs, :]
            dy = (dxo * (0.5 * gate)).astype(BF16)
            g = gs_ref[rs, :].astype(F32)
            u = us_ref[rs, :].astype(F32)
            sig = jax.nn.sigmoid(g)
            sl = g * sig
            a = (sl * u).astype(BF16)
            da = _dot_nt(dy, wo_ref[...])
            dg = (da * u * (sig * (1.0 + g * (1.0 - sig)))).astype(BF16)
            du = (da * sl).astype(BF16)
            dgu = jnp.concatenate([dg, du], axis=1)
            dhp = _dot_nt(dgu, wgu[...])
            parts.append((h.astype(BF16), a, dgu, dxo.astype(BF16)))
            if last:
                dsh, dsc, dgn, dxin = _norm_mod_bwd(refs[7][rs, :] + dhp, r, xn, hp, gn, sc)
                vec_ref[0:1, :] += dsh
                vec_ref[1:2, :] += dsc
                vec_ref[3:4, :] += dgn
                out_ref[rs, :] = dxo + dxin
            else:
                out_ref[rs, :] = dhp

        hb, a, dgu, dxb = [jnp.concatenate(p, axis=0) if nsub > 1 else p[0] for p in zip(*parts)]
        accw[...] += _dot_tn(a, dxb)
        accgu[...] += _dot_tn(hb, dgu)

        @pl.when(i == nt - 1)
        def _():
            gw = accw[...]
            vec_ref[2:3, :] += 0.5 * jnp.sum(wo_ref[...].astype(F32) * gw, axis=0, keepdims=True)
            accw[...] = gw * (0.5 * gate)
            mx, my, cc = _me()
            part = lambda acc, base, n, c, col: acc.at[pl.ds(base + c * n, n), pl.ds(col[0], col[1])]
            pieces = [(accgu, 0, hi, (0, CH), dwin_ref, rwin_ref, jj), (accgu, 0, hi, (CH, CH), dwin_ref, rwin_ref, 2 + jj),
                      (accw, 0, ho, (0, D), dwout_ref, rwout_ref, 2 * jj),
                      (accw, 2 * ho, ho, (0, D), dwout_ref, rwout_ref, 2 * jj + 1)]
            loc = [pltpu.make_async_copy(part(acc, base, n, cc, col), own.at[slot], sems.at[p])
                   for p, (acc, base, n, col, own, _, slot) in enumerate(pieces)]
            rem = [pltpu.make_async_remote_copy(part(acc, base, n, 1 - cc, col), sib.at[slot], fsend.at[p], frecv.at[p],
                                                device_id=(mx, my, 1 - cc), device_id_type=MESH)
                   for p, (acc, base, n, col, _, sib, slot) in enumerate(pieces)]
            for cp in loc + rem:
                cp.start()
            for cp in loc:
                cp.wait()
            for cp in rem:
                cp.wait()

    once = pl.Buffered(1)
    tile = pl.BlockSpec((tm, D), lambda i: (i, 0))
    chunk = pl.BlockSpec((tm, CH), lambda i: (i, jj))
    in_specs = [tile, tile, chunk, chunk, pl.BlockSpec((8, D), lambda i: (0, 0)), ANY,
                pl.BlockSpec((None, CH, D), lambda i: (jj, 0, 0), pipeline_mode=once)]
    args = (x, dxo, gs, us, modv, win, wout)
    if last:
        in_specs += [tile, ANY, ANY, ANY, ANY]
        args += tuple(prev)
    step = lambda s: lambda: pl.program_id(0) == s
    return _side_call(
        body, side, (step(0), None, step(nt - 1)), name="ffn_bwd",
        grid=(nt,), in_specs=in_specs,
        out_specs=[tile, ANY, ANY, ANY, ANY, pl.BlockSpec((8, D), lambda i: (0, 0))],
        out_shape=[_sds((S, D), F32), _sds((NQ, hi, CH), F32), _sds((NQ, ho, D), F32), _sds((NQ, hi, CH), F32),
                   _sds((NQ, ho, D), F32), _sds((8, D), F32)],
        scratch_shapes=[pltpu.VMEM((D, 2 * CH), F32), pltpu.VMEM((CH, D), F32), pltpu.VMEM((D, 2 * CH), BF16),
                        pltpu.SemaphoreType.DMA((4,)), pltpu.SemaphoreType.DMA((4,)), pltpu.SemaphoreType.DMA((4,))],
        aliases={8 + p: 1 + p for p in range(4)} if last else {},
        compiler_params=_params(("arbitrary",), VMEM_LIMIT),
        args=args)


def _ffn_bwd(x, dxo, gs, us, modv, win, wout, side=None):
    first, extra = _ffn_bwd_pass(0, x, dxo, gs, us, modv, win, wout, side=side)
    (dx, dwin, dwout, rwin, rwout, vec), _ = _ffn_bwd_pass(1, x, dxo, gs, us, modv, win, wout, prev=first[:5])
    return (dx, dwin, dwout, rwin, rwout, first[5] + vec), extra


def _prep_spatial(w_spatial, b_spatial_t):
    def body(w_ref, b_ref, wcat_ref, wtcat_ref, bias_ref):
        row = lax.broadcasted_iota(jnp.int32, (CHUNK, CHUNK), 0)
        col = lax.broadcasted_iota(jnp.int32, (CHUNK, CHUNK), 1)
        tril = col <= row
        for p in range(4):
            wa = jnp.where(tril, w_ref[2 * p], 0.0)
            wb = jnp.where(tril, w_ref[2 * p + 1], 0.0)
            wcat_ref[p] = jnp.concatenate([wa, wb], axis=1).astype(BF16)
            wtcat_ref[p] = jnp.concatenate([wa.T, wb.T], axis=1).astype(BF16)
        head = lax.broadcasted_iota(jnp.int32, (8, DG), 0)
        ch = lax.broadcasted_iota(jnp.int32, (8, DG), 1)
        spread = jnp.where(ch // 64 == head, 1.0, 0.0).astype(F32)
        bias_ref[...] = jnp.dot(b_ref[...], spread, precision=HIGHEST, preferred_element_type=F32)

    return _call(
        body, name="prep_spatial",
        in_specs=[VMEM, VMEM], out_specs=[VMEM, VMEM, VMEM],
        out_shape=[_sds((4, CHUNK, 2 * CHUNK), BF16), _sds((4, CHUNK, 2 * CHUNK), BF16), _sds((CHUNK, DG), F32)],
    )(w_spatial, b_spatial_t)


def _pair_rhs(blocks):
    lane = lax.broadcasted_iota(jnp.int32, (CHUNK, LANE), 1)
    lo = lane < 64
    top = jnp.concatenate([jnp.where(lo, b, 0.0) for b in blocks], axis=1)
    bot = jnp.concatenate([jnp.where(lo, 0.0, b) for b in blocks], axis=1)
    return top, bot


def _gmlp_branch(zb, vecs, wcat_ref, bias_ref, nchunks):
    z, dz = _gelu_fwd_bwd(zb)
    u = z[:, :DG]
    v = z[:, DG:]
    ln_g, ln_b = vecs[1:2, :], vecs[2:3, :]
    mu = jnp.mean(v, axis=-1, keepdims=True)
    vc = v - mu
    rstd = lax.rsqrt(jnp.mean(vc * vc, axis=-1, keepdims=True) + EPS)
    vhat = vc * rstd
    vl = vhat * ln_g + ln_b
    sv_cols = []
    for p in range(4):
        blocks = [vl[k * CHUNK:(k + 1) * CHUNK, p * LANE:(p + 1) * LANE] for k in range(nchunks)]
        top, bot = _pair_rhs(blocks)
        rhs = jnp.concatenate([top, bot], axis=0).astype(BF16)
        out = _dot(wcat_ref[p], rhs)
        bias = bias_ref[:, p * LANE:(p + 1) * LANE]
        sv_cols.append(jnp.concatenate([out[:, k * LANE:(k + 1) * LANE] + bias for k in range(nchunks)], axis=0))
    sv = jnp.concatenate(sv_cols, axis=1)
    return dict(u=u, dz=dz, rstd=rstd, vhat=vhat, vl=vl, sv=sv, yb=u * sv)


def _mix_fwd(x, modv, win, wpool, vecs, wcat, bias, wout):
    S = x.shape[0]
    tm = TM_MIX
    nt = S // tm
    nchunks = tm // CHUNK

    def body(x_ref, mod_ref, win_ref, wpool_ref, vec_ref, wcat_ref, bias_ref, wout_ref,
             xo_ref, pooled_ref, zb_ref, ext):
        i = pl.program_id(0)

        @pl.when(i == 0)
        def _():
            ext[0:HALO, :] = jnp.zeros((HALO, DP), F32)

        x = x_ref[...]
        _, _, _, h = _norm_mod(x, mod_ref[3:4, :], mod_ref[1:2, :], mod_ref[0:1, :])
        proj = _dot(h.astype(BF16), win_ref[...])
        xa = proj[:, :DP]
        zb = proj[:, DP:]
        zb_ref[...] = zb
        ext[HALO:HALO + tm, :] = xa
        pos = i * tm + lax.broadcasted_iota(jnp.int32, (tm, 1), 0)
        vecs = vec_ref[...]
        ya_cols = []
        pooled_cols = []
        for gi, w in enumerate(POOL_WINDOWS):
            cols = slice(gi * LANE, (gi + 1) * LANE)
            s = xa[:, cols]
            for k in range(1, w):
                s = s + ext[HALO - k:HALO - k + tm, cols]
            cnt = jnp.minimum(pos + 1, w).astype(F32)
            pooled = (s / cnt - xa[:, cols]).astype(BF16)
            pooled_cols.append(pooled)
            ya_cols.append(_dot(pooled, wpool_ref[gi]) * vecs[0:1, cols])
        pooled_ref[...] = jnp.concatenate(pooled_cols, axis=1)
        ext[0:HALO, :] = ext[tm:tm + HALO, :]

        gm = _gmlp_branch(zb, vecs, wcat_ref, bias_ref, nchunks)
        cat = jnp.concatenate(ya_cols + [gm["yb"]], axis=1).astype(BF16)
        xo_ref[...] = x + mod_ref[2:3, :] * _dot(cat, wout_ref[...])

    full = lambda shape: pl.BlockSpec(shape, lambda i: (0,) * len(shape))
    return _call(
        body, name="mix_fwd",
        grid=(nt,),
        in_specs=[pl.BlockSpec((tm, D), lambda i: (i, 0)), full((8, D)), full((D, DPROJ)),
                  full((4, LANE, LANE)), full((8, DP)), full((4, CHUNK, 2 * CHUNK)), full((CHUNK, DG)),
                  full((DP + DG, D))],
        out_specs=[pl.BlockSpec((tm, D), lambda i: (i, 0)), pl.BlockSpec((tm, DP), lambda i: (i, 0)),
                   pl.BlockSpec((tm, 2 * DG), lambda i: (i, 0))],
        out_shape=[_sds((S, D), F32), _sds((S, DP), BF16), _sds((S, 2 * DG), F32)],
        scratch_shapes=[pltpu.VMEM((tm + HALO, DP), F32)],
        compiler_params=_params(("arbitrary",), VMEM_LIMIT),
    )(x, modv, win, wpool, vecs, wcat, bias, wout)


def _mix_bwd(x, dxo, pooled, zb, modv, win, wpool, vecs, wcat, wtcat, bias, wout, side=None):
    S = x.shape[0]
    tm = TM_MIX
    nt = S // tm
    nchunks = tm // CHUNK

    def body(x_ref, dxo_ref, pooled_ref, zb_ref, mod_ref, win_ref, wpool_ref, vec_ref, wcat_ref, wtcat_ref,
             bias_ref, wout_ref,
             dx_ref, dwin_ref, dwout_ref, dwpool_ref, dwsp_ref, dbsp_ref, v512_ref, vd_ref, qext, dsv_acc):
        step = pl.program_id(0)
        tile = nt - 1 - step

        @pl.when(step == 0)
        def _():
            dwin_ref[...] = jnp.zeros_like(dwin_ref)
            dwout_ref[...] = jnp.zeros_like(dwout_ref)
            dwpool_ref[...] = jnp.zeros_like(dwpool_ref)
            dwsp_ref[...] = jnp.zeros_like(dwsp_ref)
            v512_ref[...] = jnp.zeros_like(v512_ref)
            vd_ref[...] = jnp.zeros_like(vd_ref)
            dsv_acc[...] = jnp.zeros_like(dsv_acc)
            qext[tm:tm + HALO, :] = jnp.zeros((HALO, DP), F32)

        gn, sc, sh, gate = mod_ref[3:4, :], mod_ref[1:2, :], mod_ref[0:1, :], mod_ref[2:3, :]
        vecs = vec_ref[...]
        x = x_ref[...]
        r, xn, hp, h = _norm_mod(x, gn, sc, sh)
        hb = h.astype(BF16)
        dxo = dxo_ref[...]

        pooled = pooled_ref[...]
        mixed_cols = [_dot(pooled[:, gi * LANE:(gi + 1) * LANE], wpool_ref[gi]) for gi in range(4)]
        mixed = jnp.concatenate(mixed_cols, axis=1)
        scale = vecs[0:1, :]
        gm = _gmlp_branch(zb_ref[...], vecs, wcat_ref, bias_ref, nchunks)
        cat = jnp.concatenate([mixed * scale, gm["yb"]], axis=1).astype(BF16)

        dwout_ref[...] += _dot_tn(cat, dxo.astype(BF16))
        dcat = _dot_nt((dxo * gate).astype(BF16), wout_ref[...])
        dya = dcat[:, :DP]
        dyb = dcat[:, DP:]

        v512_ref[0:1, :] += jnp.sum(dya * mixed, axis=0, keepdims=True)
        dmixed = (dya * scale).astype(BF16)
        pos = tile * tm + lax.broadcasted_iota(jnp.int32, (tm, 1), 0)
        dpooled_cols = []
        for gi, w in enumerate(POOL_WINDOWS):
            cols = slice(gi * LANE, (gi + 1) * LANE)
            dp = _dot_nt(dmixed[:, cols], wpool_ref[gi])
            dwpool_ref[gi] += _dot_tn(pooled[:, cols], dmixed[:, cols])
            cnt = jnp.minimum(pos + 1, w).astype(F32)
            qext[0:tm, cols] = dp / cnt
            dpooled_cols.append(dp)
        dxa_cols = []
        for gi, w in enumerate(POOL_WINDOWS):
            cols = slice(gi * LANE, (gi + 1) * LANE)
            s = qext[0:tm, cols]
            for k in range(1, w):
                s = s + qext[k:k + tm, cols]
            dxa_cols.append(s - dpooled_cols[gi])
        qext[tm:tm + HALO, :] = qext[0:HALO, :]

        u, sv, vl = gm["u"], gm["sv"], gm["vl"]
        du = dyb * sv
        dsv = dyb * u
        dvl_cols = []
        for p in range(4):
            cols = slice(p * LANE, (p + 1) * LANE)
            dblocks = [dsv[k * CHUNK:(k + 1) * CHUNK, cols] for k in range(nchunks)]
            vblocks = [vl[k * CHUNK:(k + 1) * CHUNK, cols] for k in range(nchunks)]
            tot = dblocks[0]
            for b in dblocks[1:]:
                tot = tot + b
            dsv_acc[:, cols] += tot
            top, bot = _pair_rhs(dblocks)
            out = _dot(wtcat_ref[p], jnp.concatenate([top, bot], axis=0).astype(BF16))
            dvl_cols.append(jnp.concatenate([out[:, k * LANE:(k + 1) * LANE] for k in range(nchunks)], axis=0))
            vcat = jnp.concatenate(vblocks, axis=1).astype(BF16)
            dwsp_ref[2 * p] += _dot_nt(top.astype(BF16), vcat)
            dwsp_ref[2 * p + 1] += _dot_nt(bot.astype(BF16), vcat)
        dvl = jnp.concatenate(dvl_cols, axis=1)
        vhat, rstd = gm["vhat"], gm["rstd"]
        v512_ref[1:2, :] += jnp.sum(dvl * vhat, axis=0, keepdims=True)
        v512_ref[2:3, :] += jnp.sum(dvl, axis=0, keepdims=True)
        dvh = dvl * vecs[1:2, :]
        dv = rstd * (dvh - jnp.mean(dvh, axis=-1, keepdims=True)
                     - vhat * jnp.mean(dvh * vhat, axis=-1, keepdims=True))
        dzb = jnp.concatenate([du, dv], axis=1) * gm["dz"]

        dproj = jnp.concatenate(dxa_cols + [dzb], axis=1).astype(BF16)
        dwin_ref[...] += _dot_tn(hb, dproj)
        dh = _dot_nt(dproj, win_ref[...])
        dsh, dsc, dgn, dxin = _norm_mod_bwd(dh, r, xn, hp, gn, sc)
        vd_ref[0:1, :] += dsh
        vd_ref[1:2, :] += dsc
        vd_ref[3:4, :] += dgn
        dx_ref[...] = dxo + dxin

        @pl.when(step == nt - 1)
        def _():
            gw = dwout_ref[...]
            vd_ref[2:3, :] += jnp.sum(wout_ref[...].astype(F32) * gw, axis=0, keepdims=True)
            dwout_ref[...] = gw * gate
            row = lax.broadcasted_iota(jnp.int32, (CHUNK, CHUNK), 0)
            col = lax.broadcasted_iota(jnp.int32, (CHUNK, CHUNK), 1)
            for hh in range(8):
                dwsp_ref[hh] = jnp.where(col <= row, dwsp_ref[hh], 0.0)
            head = lax.broadcasted_iota(jnp.int32, (8, DG), 0)
            ch = lax.broadcasted_iota(jnp.int32, (8, DG), 1)
            spread = jnp.where(ch // 64 == head, 1.0, 0.0).astype(F32)
            dbsp_ref[...] = lax.dot_general(spread, dsv_acc[...], (((1,), (1,)), ((), ())),
                                            precision=HIGHEST, preferred_element_type=F32)

    full = lambda shape: pl.BlockSpec(shape, lambda s: (0,) * len(shape))
    rev = lambda cols: pl.BlockSpec((tm, cols), lambda s: (nt - 1 - s, 0))
    step = lambda s: lambda: pl.program_id(0) == s
    return _side_call(
        body, side, (step(0), None, step(nt - 1)), name="mix_bwd",
        grid=(nt,),
        in_specs=[rev(D), rev(D), rev(DP), rev(2 * DG), full((8, D)), full((D, DPROJ)), full((4, LANE, LANE)),
                  full((8, DP)), full((4, CHUNK, 2 * CHUNK)), full((4, CHUNK, 2 * CHUNK)), full((CHUNK, DG)),
                  full((DP + DG, D))],
        out_specs=[rev(D), full((D, DPROJ)), full((DP + DG, D)), full((4, LANE, LANE)), full((8, CHUNK, CHUNK)),
                   full((8, CHUNK)), full((8, DP)), full((8, D))],
        out_shape=[_sds((S, D), F32), _sds((D, DPROJ), F32), _sds((DP + DG, D), F32), _sds((4, LANE, LANE), F32),
                   _sds((8, CHUNK, CHUNK), F32), _sds((8, CHUNK), F32), _sds((8, DP), F32), _sds((8, D), F32)],
        scratch_shapes=[pltpu.VMEM((tm + HALO, DP), F32), pltpu.VMEM((CHUNK, DG), F32)],
        compiler_params=_params(("arbitrary",), VMEM_LIMIT),
        args=(x, dxo, pooled, zb, modv, win, wpool, vecs, wcat, wtcat, bias, wout))


def _chip_sum(g, rbuf, core):
    _, _, hr, cols = g.shape
    tr = _row_block(hr)

    def body(c_ref, g_ref, r_ref, o_ref):
        o_ref[...] = (g_ref[...] + r_ref[...]).astype(BF16)

    return pl.pallas_call(
        body, name="chip_sum", interpret=False,
        grid_spec=pltpu.PrefetchScalarGridSpec(
            num_scalar_prefetch=1, grid=(NQ, hr // tr),
            in_specs=[pl.BlockSpec((None, None, tr, cols), lambda q, i, c: (q, c[0], i, 0)),
                      pl.BlockSpec((None, tr, cols), lambda q, i, c: (q, i, 0))],
            out_specs=pl.BlockSpec((None, tr, cols), lambda q, i, c: (q, i, 0))),
        out_shape=_sds((NQ, hr, cols), BF16),
        compiler_params=_params(("arbitrary", "arbitrary"), None),
    )(core, g, rbuf)


def _chip_sum_pair(own, rbuf, slots=(0, 1, 2, 3), prev=None):
    _, hr, cols = own.shape
    tr = _row_block(hr)
    a, b = slots[0], (slots[1] - slots[0] if len(slots) > 1 else 0)
    assert list(slots) == [a + b * k for k in range(len(slots))]

    def body(a_ref, b_ref, *rest):
        rest[-1][...] = (a_ref[...] + b_ref[...]).astype(BF16)

    spec = pl.BlockSpec((None, tr, cols), lambda k, i: (a + b * k, i, 0))
    return _call(
        body, name="chip_sum_pair",
        grid=(len(slots), hr // tr),
        in_specs=[spec, spec] + ([ANY] if prev is not None else []), out_specs=spec,
        out_shape=_sds((NQ, hr, cols), BF16),
        input_output_aliases={2: 0} if prev is not None else {},
        compiler_params=_params(("arbitrary", "arbitrary"), None),
    )(own, rbuf, *([prev] if prev is not None else []))


def _sum4(cs, rbuf, chip, after=()):
    _, hr, cols = rbuf.shape
    tr = _row_block(hr)

    def body(q_ref, c_ref, r1_ref, r2_ref, r3_ref, *rest):
        acc = c_ref[...].astype(F32)
        for r in (r1_ref, r2_ref, r3_ref):
            acc = acc + r[...].astype(F32)
        rest[-1][...] = acc

    slot = lambda k: pl.BlockSpec((None, tr, cols), lambda i, q: ((q[0] + k) % NQ, i, 0))
    return pl.pallas_call(
        body, name="sum4", interpret=False,
        grid_spec=pltpu.PrefetchScalarGridSpec(
            num_scalar_prefetch=1, grid=(hr // tr,),
            in_specs=[slot(0), slot(1), slot(2), slot(3)] + [ANY] * len(after),
            out_specs=pl.BlockSpec((tr, cols), lambda i, q: (i, 0))),
        out_shape=_sds((hr, cols), F32),
        compiler_params=_params(("arbitrary",), None),
    )(chip, cs, rbuf, rbuf, rbuf, *after)


def _adamw_halves(w, own, recv, m, v, side=None, after=()):
    rows, cols = w.shape
    hr = rows // 2
    tr = _row_block(hr, mult=8)
    nb = hr // tr

    def body(w_ref, own_ref, recv_ref, m_ref, v_ref, *rest):
        g_ref, d_ref, mo_ref, vo_ref = rest[len(after):]
        g = jnp.where(pl.program_id(0) == lax.axis_index("c"), own_ref[...], recv_ref[...])
        d, mn, vn = _adamw(w_ref[...], g, m_ref[...], v_ref[...])
        g_ref[...] = g
        d_ref[...] = d
        mo_ref[...] = mn
        vo_ref[...] = vn

    full = pl.BlockSpec((tr, cols), lambda h, i: (h * nb + i, 0))
    half = pl.BlockSpec((tr, cols), lambda h, i: (i, 0))
    step = lambda h, i: lambda: (pl.program_id(0) == h) & (pl.program_id(1) == i)
    return _side_call(
        body, side, (step(0, 0), None, step(1, nb - 1)), name="adamw_halves",
        grid=(2, nb), in_specs=[full, half, half, full, full] + [ANY] * len(after), out_specs=[full] * 4,
        out_shape=[_sds((rows, cols), F32)] * 4, scratch_shapes=[],
        compiler_params=_params(("arbitrary", "arbitrary"), None),
        args=(w, own, recv, m, v, *after))


def _cast_place(w, chip, wide=False):
    rows, cols = w.shape
    tr = _row_block(rows)

    def body(q_ref, w_ref, o_ref):
        o_ref[...] = w_ref[...].astype(BF16)

    if wide:
        out_spec = pl.BlockSpec((None, tr, cols), lambda i, q: (q[0] % 2, i, q[0] // 2))
    else:
        out_spec = pl.BlockSpec((None, tr, cols), lambda i, q: (q[0], i, 0))
    return pl.pallas_call(
        body, name="cast_place", interpret=False,
        grid_spec=pltpu.PrefetchScalarGridSpec(
            num_scalar_prefetch=1, grid=(rows // tr,),
            in_specs=[pl.BlockSpec((tr, cols), lambda i, q: (i, 0))],
            out_specs=out_spec),
        out_shape=_sds((2, rows, 2 * cols) if wide else (NQ, rows, cols), BF16),
        compiler_params=_params(("arbitrary",), None),
    )(chip, w)


def _ada_grad_adamw(cact_t, dmod_q, w, m, v, side=None):
    rows, cols = w.shape
    tc = 256
    assert cols % tc == 0

    def body(c_ref, d_ref, w_ref, m_ref, v_ref, g_ref, dl_ref, mo_ref, vo_ref):
        g = jnp.dot(c_ref[...], d_ref[...], precision=HIGHEST, preferred_element_type=F32)
        d, mn, vn = _adamw(w_ref[...], g, m_ref[...], v_ref[...])
        g_ref[...] = g
        dl_ref[...] = d
        mo_ref[...] = mn
        vo_ref[...] = vn

    spec = pl.BlockSpec((rows, tc), lambda i: (0, i))
    step = lambda s: lambda: pl.program_id(0) == s
    return _side_call(
        body, side, (step(0), None, step(cols // tc - 1)), name="ada_grad_adamw",
        grid=(cols // tc,),
        in_specs=[pl.BlockSpec((rows, 8), lambda i: (0, 0)), pl.BlockSpec((8, tc), lambda i: (0, i)),
                  spec, spec, spec],
        out_specs=[spec] * 4,
        out_shape=[_sds((rows, cols), F32)] * 4,
        scratch_shapes=[],
        compiler_params=_params(("arbitrary",), None),
        args=(cact_t, dmod_q, w, m, v))


def _me():
    x, y, c = lax.axis_index("x"), lax.axis_index("y"), lax.axis_index("c")
    return x, y, c


_OFFSETS7 = [(dx, dy, dc) for dx in (0, 1) for dy in (0, 1) for dc in (0, 1) if (dx, dy, dc) != (0, 0, 0)]
_CHIP_OFFSETS = [(1, 0), (0, 1), (1, 1)]


def _ada_fwd(c, w_ada_q, b_ada_q, side=None):
    ncol = w_ada_q.shape[1]

    def body(c_ref, w_ref, b_ref, cact_ref, modsel_ref, blk, gath, res, parts, send_sems, recv_sems, side_start=None):
        x, y, cc = _me()
        me = 4 * x + 2 * y + cc
        q = 2 * x + y
        cv = c_ref[...]
        ca = cv * jax.nn.sigmoid(cv)
        row = lax.broadcasted_iota(jnp.int32, (8, D), 0)
        blk[...] = jnp.where(row == me, jnp.broadcast_to(ca, (8, D)), 0.0)
        gath[me] = blk[...]
        sends = []
        for k, (dx, dy, dc) in enumerate(_OFFSETS7):
            cp = pltpu.make_async_remote_copy(blk, gath.at[me], send_sems.at[k], recv_sems.at[k],
                                              device_id=(x ^ dx, y ^ dy, cc ^ dc), device_id_type=MESH)
            cp.start()
            sends.append(cp)
        if side_start is not None:
            side_start()
        for cp in sends:
            cp.wait_recv()
        cact = gath[0]
        for d in range(1, N_DEV):
            cact = cact + gath[d]
        cact_ref[...] = cact
        res[...] = jnp.dot(cact, w_ref[...], precision=HIGHEST, preferred_element_type=F32) + b_ref[...]
        parts[q] = res[...]
        sends2 = []
        for k, (dx, dy) in enumerate(_CHIP_OFFSETS):
            cp = pltpu.make_async_remote_copy(res, parts.at[q], send_sems.at[7 + k], recv_sems.at[7 + k],
                                              device_id=(x ^ dx, y ^ dy, cc), device_id_type=MESH)
            cp.start()
            sends2.append(cp)
        for cp in sends2:
            cp.wait_recv()
        row2 = lax.broadcasted_iota(jnp.int32, (8, ncol), 0)
        out = jnp.zeros((8, ncol), F32)
        for s in range(NQ):
            mine = jnp.sum(jnp.where(row2 == me, parts[s], 0.0), axis=0, keepdims=True)
            out = out + jnp.where(row2 == s, jnp.broadcast_to(mine, (8, ncol)), 0.0)
        modsel_ref[...] = out
        for cp in sends + sends2:
            cp.wait_send()

    return _side_call(
        body, side, None, name="ada_fwd",
        in_specs=[VMEM, VMEM, VMEM], out_specs=[VMEM, VMEM],
        out_shape=[_sds((8, D), F32), _sds((8, ncol), F32)],
        scratch_shapes=[pltpu.VMEM((8, D), F32), pltpu.VMEM((N_DEV, 8, D), F32), pltpu.VMEM((8, ncol), F32),
                        pltpu.VMEM((NQ, 8, ncol), F32), pltpu.SemaphoreType.DMA((10,)), pltpu.SemaphoreType.DMA((10,))],
        compiler_params=_params(None, VMEM_LIMIT), start_in_body=side is not None,
        args=(c, w_ada_q, b_ada_q))


class _Side:
    def __init__(self, ins, out_shapes, aliases, nsem, start, mid=None, finish=None):
        self.ins, self.out_shapes, self.aliases, self.nsem = list(ins), list(out_shapes), dict(aliases), nsem
        self.start, self.mid, self.finish = start, mid, finish


def _join(*sides):
    ins, outs, aliases, offs, nsem = [], [], {}, [], 0
    for s in sides:
        offs.append((len(ins), len(outs), nsem))
        aliases.update({len(ins) + a: len(outs) + b for a, b in s.aliases.items()})
        ins += s.ins
        outs += s.out_shapes
        nsem += s.nsem

    def hook(name):
        def run(i, o, ss, rs, base):
            for s, (io, oo, so) in zip(sides, offs):
                fn = getattr(s, name)
                if fn is not None:
                    fn(i[io:io + len(s.ins)], o[oo:oo + len(s.out_shapes)], ss, rs, base + so)
        return run

    return _Side(ins, outs, aliases, nsem, hook("start"), hook("mid"), hook("finish"))


def _side_call(body, side, when, *, name, in_specs, out_specs, out_shape, scratch_shapes, args, aliases=None,
               start_in_body=False, **kw):
    n_in, n_out = len(in_specs), len(out_specs)
    aliases = dict(aliases or {})
    if side is None:
        return _call(body, name=name, in_specs=in_specs, out_specs=out_specs, out_shape=out_shape,
                     scratch_shapes=scratch_shapes, input_output_aliases=aliases, **kw)(*args), []
    ns_in, ns_out = len(side.ins), len(side.out_shapes)

    def hook(fn, k, operands):
        if fn is None:
            return
        if when is None:
            fn(*operands, 0)
        elif when[k] is not None:
            pl.when(when[k]())(functools.partial(fn, *operands, 0))

    def wrapped(*refs):
        ins, s_ins = refs[:n_in], refs[n_in:n_in + ns_in]
        o0 = n_in + ns_in
        outs, s_outs = refs[o0:o0 + n_out], refs[o0 + n_out:o0 + n_out + ns_out]
        rest = refs[o0 + n_out + ns_out:]
        scratch, operands = rest[:-2], (s_ins, s_outs, rest[-2], rest[-1])
        if start_in_body:
            body(*ins, *outs, *scratch, side_start=functools.partial(hook, side.start, 0, operands))
        else:
            hook(side.start, 0, operands)
            body(*ins, *outs, *scratch)
        hook(side.mid, 1, operands)
        hook(side.finish, 2, operands)

    res = _call(
        wrapped, name=name,
        in_specs=list(in_specs) + [ANY] * ns_in, out_specs=list(out_specs) + [ANY] * ns_out,
        out_shape=list(out_shape) + side.out_shapes,
        scratch_shapes=list(scratch_shapes) + [pltpu.SemaphoreType.DMA((side.nsem,)),
                                               pltpu.SemaphoreType.DMA((side.nsem,))],
        input_output_aliases={**aliases, **{n_in + a: n_out + b for a, b in side.aliases.items()}},
        **kw)(*args, *side.ins)
    return res[:n_out], res[n_out:]


def _run_side(side, name):
    return _side_call(lambda: None, side, None, name=name, in_specs=[], out_specs=[], out_shape=[],
                      scratch_shapes=[], args=[])[1]


def _remote(src, dst, ss, rs, k, dev):
    return pltpu.make_async_remote_copy(src, dst, ss.at[k], rs.at[k], device_id=dev, device_id_type=MESH)


def _gather_side(bufs):
    n = len(bufs)

    def plan(outs, w):
        x, y, cc = _me()
        hr = outs[w].shape[1] // 2
        mine, other = cc * hr, (1 - cc) * hr
        qx, qy, qd, q = 2 * (x ^ 1) + y, 2 * x + (y ^ 1), 2 * (x ^ 1) + (y ^ 1), 2 * x + y
        xn, yn, sib = (x ^ 1, y, cc), (x, y ^ 1, cc), (x, y, 1 - cc)
        if outs[w].shape[0] == NQ:
            at = lambda slot, r0, nr: outs[w].at[slot, pl.ds(r0, nr)]
        else:
            cols = outs[w].shape[2] // 2
            at = lambda slot, r0, nr: outs[w].at[slot % 2, pl.ds(r0, nr), pl.ds((slot // 2) * cols, cols)]
        send = [(at(q, mine, hr), xn), (at(q, mine, hr), yn),
                (at(qx, mine, hr // 2), yn), (at(qy, mine + hr // 2, hr // 2), xn),
                (at(qx, mine, hr), sib), (at(qy, mine, hr), sib), (at(qd, mine, hr), sib)]
        recv = [at(qx, mine, hr), at(qy, mine, hr), at(qd, mine, hr // 2), at(qd, mine + hr // 2, hr // 2),
                at(qx, other, hr), at(qy, other, hr), at(qd, other, hr)]
        return send, recv

    def op(outs, ss, rs, b, w, k, what):
        send, recv = plan(outs, w)
        if what == "wait_recv":
            _remote(recv[k], recv[k], ss, rs, b + 7 * w + k, send[k][1]).wait_recv()
        else:
            getattr(_remote(send[k][0], send[k][0], ss, rs, b + 7 * w + k, send[k][1]), what)()

    def start(ins, outs, ss, rs, b):
        for w in range(n):
            for k in (0, 1):
                op(outs, ss, rs, b, w, k, "start")

    def mid(ins, outs, ss, rs, b):
        for w in range(n):
            for k in (0, 1):
                op(outs, ss, rs, b, w, k, "wait_recv")
                op(outs, ss, rs, b, w, 2 + k, "start")
                op(outs, ss, rs, b, w, 4 + k, "start")

    def finish(ins, outs, ss, rs, b):
        for w in range(n):
            for k in (2, 3):
                op(outs, ss, rs, b, w, k, "wait_recv")
            op(outs, ss, rs, b, w, 6, "start")
        for w in range(n):
            for k in (4, 5, 6):
                op(outs, ss, rs, b, w, k, "wait_recv")
            for k in range(7):
                op(outs, ss, rs, b, w, k, "wait_send")

    return _Side(bufs, [_sds(tuple(w.shape), w.dtype) for w in bufs], {i: i for i in range(n)}, 7 * n,
                 start, mid, finish)


def _copies_side(ins, out_shapes, nsem, copies):
    def start(*a):
        for cp in copies(*a):
            cp.start()

    def finish(*a):
        for cp in copies(*a):
            cp.wait()

    return _Side(ins, out_shapes, {}, nsem, start, None, finish)


def _swap_side(gs):
    def copies(ins, outs, ss, rs, b):
        x, y, cc = _me()
        return [_remote(ins[w].at[:, 1 - cc], outs[w], ss, rs, b + w, (x, y, 1 - cc)) for w in range(len(gs))]

    return _copies_side(gs, [_sds((NQ,) + tuple(g.shape[2:]), F32) for g in gs], len(gs), copies)


def _exchange_side(cs, slots=None, prev=None):
    n = len(cs)
    slots = slots or [(0, 1, 2, 3)] * n

    def among(chip, allowed):
        hit = chip == allowed[0]
        for s in allowed[1:]:
            hit = hit | (chip == s)
        return hit

    def each(ins, outs, ss, rs, b, do_send, do_recv):
        x, y, cc = _me()
        q = 2 * x + y
        for w in range(n):
            for j, (dx, dy) in enumerate(_CHIP_OFFSETS):
                pq = 2 * (x ^ dx) + (y ^ dy)
                cp = _remote(ins[w].at[pq], outs[w].at[q], ss, rs, b + 3 * w + j, (x ^ dx, y ^ dy, cc))
                if do_send is not None:
                    pl.when(among(pq, slots[w]))(functools.partial(do_send, cp))
                if do_recv is not None:
                    pl.when(among(q, slots[w]))(functools.partial(do_recv, cp))

    def start(ins, outs, ss, rs, b):
        each(ins, outs, ss, rs, b, lambda cp: cp.start(), None)

    def finish(ins, outs, ss, rs, b):
        each(ins, outs, ss, rs, b, lambda cp: cp.wait_send(), lambda cp: cp.wait_recv())

    ins = list(cs) + (list(prev) if prev is not None else [])
    aliases = {n + w: w for w in range(n)} if prev is not None else {}
    return _Side(ins, [_sds(tuple(c.shape), c.dtype) for c in cs], aliases, 3 * n, start, None, finish)


def _exchange_copies(srcs, lands, send_sems, recv_sems):
    x, y, cc = _me()
    return [pltpu.make_async_remote_copy(srcs[w].at[2 * (x ^ dx) + (y ^ dy)], lands[w].at[2 * x + y],
                                         send_sems.at[3 * w + j], recv_sems.at[3 * w + j],
                                         device_id=(x ^ dx, y ^ dy, cc), device_id_type=MESH)
            for w in range(len(srcs)) for j, (dx, dy) in enumerate(_CHIP_OFFSETS)]


def _exchange_start(cs, after=()):
    n = len(cs)
    hbm, sem = pl.BlockSpec(memory_space=pltpu.HBM), pl.BlockSpec(memory_space=pltpu.SEMAPHORE)
    srcs = [pltpu.with_memory_space_constraint(c, pltpu.HBM) for c in cs]
    lands = [pltpu.with_memory_space_constraint(lax.empty(c.shape, c.dtype), pltpu.HBM) for c in cs]

    def body(*refs):
        sems = 2 * n + len(after)
        for cp in _exchange_copies(refs[:n], refs[n:2 * n], refs[sems], refs[sems + 1]):
            cp.start()
        refs[-1][...] = jnp.zeros_like(refs[-1])

    res = pl.pallas_call(
        body, name="exchange_start", interpret=False,
        out_shape=(pltpu.SemaphoreType.DMA((3 * n,)), pltpu.SemaphoreType.DMA((3 * n,)),
                   *[pltpu.HBM(c.shape, c.dtype) for c in cs], *[pltpu.HBM(c.shape, c.dtype) for c in cs],
                   _sds((8, LANE), F32)),
        in_specs=(hbm,) * (2 * n) + (ANY,) * len(after), out_specs=(sem, sem) + (hbm,) * (2 * n) + (VMEM,),
        input_output_aliases={i: 2 + i for i in range(2 * n)},
        compiler_params=pltpu.CompilerParams(has_side_effects=pltpu.SideEffectType.DATAFLOW_SIDE_EFFECTING),
    )(*srcs, *lands, *after)
    return res[0], res[1], list(res[2:2 + n]), list(res[2 + n:2 + 2 * n]), res[-1]


def _exchange_wait(send_sems, recv_sems, srcs, lands, after):
    n = len(srcs)
    hbm, sem = pl.BlockSpec(memory_space=pltpu.HBM), pl.BlockSpec(memory_space=pltpu.SEMAPHORE)

    def body(*refs):
        for cp in _exchange_copies(refs[:n], refs[n:2 * n], refs[2 * n], refs[2 * n + 1]):
            cp.wait_send()
            cp.wait_recv()

    res = pl.pallas_call(
        body, name="exchange_wait", interpret=False,
        out_shape=[pltpu.HBM(c.shape, c.dtype) for c in srcs + lands],
        in_specs=(hbm,) * (2 * n) + (sem, sem) + (ANY,) * len(after), out_specs=(hbm,) * (2 * n),
        input_output_aliases={i: i for i in range(2 * n)},
        compiler_params=pltpu.CompilerParams(has_side_effects=pltpu.SideEffectType.DATAFLOW_SIDE_EFFECTING),
    )(*srcs, *lands, send_sems, recv_sems, *after)
    return list(res[:n]), list(res[n:])


def _share_side(fs):
    def copies(ins, outs, ss, rs, b):
        x, y, cc = _me()
        return [_remote(ins[w], outs[w], ss, rs, b + w, (x, y, 1 - cc)) for w in range(len(fs))]

    return _copies_side(fs, [_sds(tuple(f.shape), F32) for f in fs], len(fs), copies)


def _small_allreduce_adamw(g, w, m, v, nd):
    rows = g.shape[0]
    nr = rows - nd
    hr = nr // 2
    assert nd % 8 == 0 and hr % 8 == 0

    def body(g_ref, w_ref, m_ref, v_ref, gs_ref, d_ref, mo_ref, vo_ref, gath, sib, csum, slots, tot, ss, rs):
        x, y, cc = _me()
        me = 4 * x + 2 * y + cc
        q = 2 * x + y
        sibling = (x, y, 1 - cc)
        dm = g_ref.at[pl.ds(0, nd)]
        gath[me] = g_ref[0:nd, :]
        to_all = [_remote(dm, gath.at[me], ss, rs, k, (x ^ dx, y ^ dy, cc ^ dc)) for k, (dx, dy, dc) in enumerate(_OFFSETS7)]
        to_sib = _remote(g_ref.at[pl.ds(nd, nr)], sib, ss, rs, 7, sibling)
        for cp in to_all + [to_sib]:
            cp.start()
        to_sib.wait_recv()
        csum[...] = g_ref[nd:, :] + sib[...]
        mine = pl.ds(pl.multiple_of(cc * hr, 8), hr)
        slots[q] = csum[mine, :]
        to_chips = [_remote(csum.at[mine], slots.at[q], ss, rs, 8 + j, (x ^ dx, y ^ dy, cc))
                    for j, (dx, dy) in enumerate(_CHIP_OFFSETS)]
        for cp in to_chips:
            cp.start()
        for cp in to_chips:
            cp.wait_recv()
        tot[mine, :] = (slots[0] + slots[1]) + (slots[2] + slots[3])
        halves = _remote(tot.at[mine], tot.at[mine], ss, rs, 11, sibling)
        halves.start()
        for cp in to_all:
            cp.wait_recv()
        dsum = gath[0]
        for dev in range(1, N_DEV):
            dsum = dsum + gath[dev]
        halves.wait_recv()
        for lo, n, total in ((0, nd, dsum), (nd, nr, tot[...])):
            gs_ref[lo:lo + n, :] = total
            d, mn, vn = _adamw(w_ref[lo:lo + n, :], total, m_ref[lo:lo + n, :], v_ref[lo:lo + n, :])
            d_ref[lo:lo + n, :] = d
            mo_ref[lo:lo + n, :] = mn
            vo_ref[lo:lo + n, :] = vn
        for cp in to_all + [to_sib, halves] + to_chips:
            cp.wait_send()

    return _call(
        body, name="small_allreduce_adamw",
        in_specs=[VMEM] * 4, out_specs=[VMEM] * 5,
        out_shape=[_sds((rows, LANE), F32)] * 4 + [_sds((N_DEV, nd, LANE), F32)],
        scratch_shapes=[pltpu.VMEM((nr, LANE), F32), pltpu.VMEM((nr, LANE), F32), pltpu.VMEM((NQ, hr, LANE), F32),
                        pltpu.VMEM((nr, LANE), F32), pltpu.SemaphoreType.DMA((12,)), pltpu.SemaphoreType.DMA((12,))],
        compiler_params=_params(None, VMEM_LIMIT),
    )(g, w, m, v)


_SMALL = ["b_ada", "norm_ffn1_g", "norm_mix_g", "pool_scale", "gmlp_ln_g", "gmlp_ln_b", "b_spatial",
          "norm_ffn2_g", "norm_final_g", "w_pool", "w_spatial"]


def _pack(parts):
    blocks, layout, r0 = [], {}, 0
    for name in _SMALL:
        a = parts[name]
        n = a.size
        rows = -(-n // LANE)
        rows8 = -(-rows // 8) * 8
        flat = a.reshape(-1).astype(F32)
        if rows8 * LANE != n:
            flat = jnp.concatenate([flat, jnp.zeros((rows8 * LANE - n,), F32)])
        blocks.append(flat.reshape(rows8, LANE))
        layout[name] = (r0, n, a.shape)
        r0 += rows8
    return jnp.concatenate(blocks, axis=0), layout


def _unpack(packed, layout):
    out = {}
    for name, (r0, n, shape) in layout.items():
        rows = -(-n // LANE)
        out[name] = packed[r0:r0 + rows].reshape(-1)[:n].reshape(shape)
    return out


def _modv(mod9, sub, gain):
    rows = jnp.concatenate([mod9[3 * sub:3 * sub + 3], gain.reshape(1, D), jnp.zeros((4, D), F32)], axis=0)
    return rows


_BIG = ["ffn1_w_in", "ffn1_w_out", "w_mix_in", "w_mix_out", "ffn2_w_in", "ffn2_w_out"]


def kernel(x, c, w_ada, b_ada, norm_ffn1_g, ffn1_w_in, ffn1_w_out, norm_mix_g, w_mix_in, w_pool, pool_scale, gmlp_ln_g, gmlp_ln_b, w_spatial, b_spatial, w_mix_out, norm_ffn2_g, ffn2_w_in, ffn2_w_out, norm_final_g, loss_target, m_w_ada, m_b_ada, m_norm_ffn1_g, m_ffn1_w_in, m_ffn1_w_out, m_norm_mix_g, m_w_mix_in, m_w_pool, m_pool_scale, m_gmlp_ln_g, m_gmlp_ln_b, m_w_spatial, m_b_spatial, m_w_mix_out, m_norm_ffn2_g, m_ffn2_w_in, m_ffn2_w_out, m_norm_final_g, v_w_ada, v_b_ada, v_norm_ffn1_g, v_ffn1_w_in, v_ffn1_w_out, v_norm_mix_g, v_w_mix_in, v_w_pool, v_pool_scale, v_gmlp_ln_g, v_gmlp_ln_b, v_w_spatial, v_b_spatial, v_w_mix_out, v_norm_ffn2_g, v_ffn2_w_in, v_ffn2_w_out, v_norm_final_g):
    names = ["w_ada", "b_ada", "norm_ffn1_g", "ffn1_w_in", "ffn1_w_out", "norm_mix_g", "w_mix_in", "w_pool",
             "pool_scale", "gmlp_ln_g", "gmlp_ln_b", "w_spatial", "b_spatial", "w_mix_out", "norm_ffn2_g",
             "ffn2_w_in", "ffn2_w_out", "norm_final_g"]
    W = dict(zip(names, [w_ada, b_ada, norm_ffn1_g, ffn1_w_in, ffn1_w_out, norm_mix_g, w_mix_in, w_pool, pool_scale,
                         gmlp_ln_g, gmlp_ln_b, w_spatial, b_spatial, w_mix_out, norm_ffn2_g, ffn2_w_in, ffn2_w_out,
                         norm_final_g]))
    M = dict(zip(names, [m_w_ada, m_b_ada, m_norm_ffn1_g, m_ffn1_w_in, m_ffn1_w_out, m_norm_mix_g, m_w_mix_in, m_w_pool,
                         m_pool_scale, m_gmlp_ln_g, m_gmlp_ln_b, m_w_spatial, m_b_spatial, m_w_mix_out, m_norm_ffn2_g,
                         m_ffn2_w_in, m_ffn2_w_out, m_norm_final_g]))
    V = dict(zip(names, [v_w_ada, v_b_ada, v_norm_ffn1_g, v_ffn1_w_in, v_ffn1_w_out, v_norm_mix_g, v_w_mix_in, v_w_pool,
                         v_pool_scale, v_gmlp_ln_g, v_gmlp_ln_b, v_w_spatial, v_b_spatial, v_w_mix_out, v_norm_ffn2_g,
                         v_ffn2_w_in, v_ffn2_w_out, v_norm_final_g]))

    xi, yi, ci = _me()
    q = 2 * xi + yi
    core = ci.astype(jnp.int32).reshape(1)

    chip = q.astype(jnp.int32).reshape(1)
    place = lambda n: _cast_place(W[n][0], chip, wide=n in ("ffn1_w_in", "ffn2_w_in"))

    ncol = w_ada.shape[2]
    b_q = lax.dynamic_slice_in_dim(b_ada, q * ncol, ncol, axis=1)
    (cact_all, modsel), (win1, wout1) = _ada_fwd(
        c, w_ada[0], b_q, side=_gather_side([place("ffn1_w_in"), place("ffn1_w_out")]))
    mod9 = modsel[:NQ].reshape(9, D)
    xs, target = x[0], loss_target[0]
    mv1 = _modv(mod9, 0, norm_ffn1_g[0])
    mv2 = _modv(mod9, 1, norm_mix_g[0])
    mv3 = _modv(mod9, 2, norm_ffn2_g[0])
    wcat, wtcat, bias = _prep_spatial(w_spatial[0], b_spatial[0].T)
    wpool = w_pool[0].astype(BF16)
    vecs = jnp.concatenate([pool_scale, gmlp_ln_g, gmlp_ln_b, jnp.zeros((5, DP), F32)], axis=0)
    gf = jnp.concatenate([norm_final_g.reshape(1, D), jnp.zeros((7, D), F32)], axis=0)

    later =["w_mix_in", "w_mix_out", "ffn2_w_in", "ffn2_w_out"]
    (x1, g1s, u1s), got = _ffn_fwd(xs, mv1, win1, wout1.reshape(2, CH, D), side=_gather_side([place(n) for n in later]))
    wmi, wmo, win2, wout2 = got
    wmi = jnp.transpose(wmi, (1, 0, 2)).reshape(D, DPROJ)
    wmo = wmo.reshape(DP + DG, D)
    x2, pooled, zb = _mix_fwd(x1, mv2, wmi, wpool, vecs, wcat, bias, wmo)
    (dx3, g3s, u3s, loss_blk, dgf), _ = _ffn_fwd(x2, mv3, win2, wout2.reshape(2, CH, D), head=(target, gf))

    wo1, wo2 = wout1.reshape(2, CH, D), wout2.reshape(2, CH, D)
    (dx2, oin2, oout2, rin2, rout2, vec3), _ = _ffn_bwd(x2, dx3, g3s, u3s, mv3, win2, wo2)
    cs2 = [_chip_sum_pair(oin2, rin2), _chip_sum_pair(oout2, rout2)]
    (dx1, dwmi, dwmo, dwpool, dwsp, dbsp, v512, vec2), ex2 = _mix_bwd(
        x1, dx2, pooled, zb, mv2, wmi, wpool, vecs, wcat, wtcat, bias, wmo, side=_exchange_side(cs2))
    half2 = [_sum4(cs, e, chip) for cs, e in zip(cs2, ex2)]
    qcols = w_mix_in.shape[2]
    vmix = [jnp.transpose(dwmi.reshape(D, NQ, qcols), (1, 0, 2)).reshape(NQ, 2, D // 2, qcols),
            dwmo.reshape(NQ, 2, (DP + DG) // 8, D)]
    first1, got = _ffn_bwd_pass(0, xs, dx1, g1s, u1s, mv1, win1, wo1,
                                side=_join(_swap_side(vmix), _share_side(half2)))
    sibmix, other2 = got[:2], got[2:]
    cs_mix = [_chip_sum(g, r, core) for g, r in zip(vmix, sibmix)]
    (grad_x, oin1, oout1, rin1, rout1, vec1), ex_mix = _ffn_bwd_pass(
        1, xs, dx1, g1s, u1s, mv1, win1, wo1, prev=first1[:5], side=_exchange_side(cs_mix))
    vec1 = first1[5] + vec1
    cs_ffn1 = [_chip_sum_pair(oin1, rin1), _chip_sum_pair(oout1, rout1)]

    dmod =jnp.concatenate([vec1[0:3], vec2[0:3], vec3[0:3]], axis=0)
    grads = dict(
        b_ada=dmod.reshape(1, 9 * D), norm_ffn1_g=vec1[3:4], norm_mix_g=vec2[3:4], norm_ffn2_g=vec3[3:4],
        pool_scale=v512[0:1], gmlp_ln_g=v512[1:2], gmlp_ln_b=v512[2:3], b_spatial=dbsp[None],
        norm_final_g=dgf[0], w_pool=dwpool[None], w_spatial=dwsp[None])

    gp, layout = _pack({n: grads[n] for n in _SMALL})
    gp = jnp.concatenate([gp, loss_blk, loss_blk], axis=0)
    pad = jnp.zeros((16, LANE), F32)
    wp, mp, vp = [jnp.concatenate([_pack({n: src[n] for n in _SMALL})[0], pad], axis=0) for src in (W, M, V)]
    r0, nb, _ = layout["b_ada"]
    assert r0 == 0
    out_g, out_d, out_m, out_v = {}, {}, {}, {}
    gs, dl, mo, vo, gath = _small_allreduce_adamw(gp, wp, mp, vp, nb // LANE)
    loss = gs[-16, 0]
    for packed, dst in ((gs, out_g), (dl, out_d), (mo, out_m), (vo, out_v)):
        for n, a in _unpack(packed, layout).items():
            dst[n] = a.reshape(W[n].shape)

    def update(n, own, recv, after=()):
        (g2, d, mn, vn), _ = _adamw_halves(W[n][0], own, recv, M[n][0], V[n][0], after=after)
        out_g[n], out_d[n], out_m[n], out_v[n] = g2[None], d[None], mn[None], vn[None]
        return g2

    ssem, rsem, cs_fly, land_fly, token = _exchange_start(cs_ffn1, after=(gath,))
    dmod_q = lax.dynamic_slice_in_dim(gath.reshape(N_DEV, nb), q * ncol, ncol, axis=1) + token[0:8, 0:1]
    (ga, da, ma, va), _ = _ada_grad_adamw(cact_all.T, dmod_q, w_ada[0], m_w_ada[0], v_w_ada[0])
    out_g["w_ada"], out_d["w_ada"], out_m["w_ada"], out_v["w_ada"] = ga[None], da[None], ma[None], va[None]
    done = [ga, update("ffn2_w_in", half2[0], other2[0], after=(token,)),
            update("ffn2_w_out", half2[1], other2[1], after=(token,))]
    half_mix = [_sum4(cs, e, chip, after=(token,)) for cs, e in zip(cs_mix, ex_mix)]
    other_mix = _run_side(_share_side(half_mix), "share_mix")
    done += [update(n, own, recv) for n, own, recv in zip(["w_mix_in", "w_mix_out"], half_mix, other_mix)]
    cs_ffn1, ex_ffn1 = _exchange_wait(ssem, rsem, cs_fly, land_fly, after=done)
    half1 = [_sum4(cs, e, chip) for cs, e in zip(cs_ffn1, ex_ffn1)]
    other1 = _run_side(_share_side(half1), "share_ffn1")
    for n, own, recv in zip(["ffn1_w_in", "ffn1_w_out"], half1, other1):
        update(n, own, recv)

    return (loss, grad_x[None], *[out_g[n] for n in names], *[out_d[n] for n in names],
            *[out_m[n] for n in names], *[out_v[n] for n in names])
```

```python
import functools
import math

import jax
import jax.numpy as jnp
from jax import lax
from jax.experimental import pallas as pl
from jax.experimental.pallas import tpu as pltpu

F32 = jnp.float32
BF16 = jnp.bfloat16
MESH = pl.DeviceIdType.MESH
HIGHEST = lax.Precision.HIGHEST

EPS = 1e-6
D = 1024
DFF = 2816
CH = DFF // 2
NQ = 4
DP = 512
DG = 512
DPROJ = DP + 2 * DG
POOL_WINDOWS = (2, 4, 8, 16)
HALO = 16
CHUNK = 128
LANE = 128
N_DEV = 8

ADAM_LR = 0.001
ADAM_B1 = 0.9
ADAM_B2 = 0.999
ADAM_EPS = 1e-08
ADAM_WD = 0.01
ADAM_STEP = 10

VMEM_LIMIT = 62 * 1024 * 1024

TM_FFN_FWD = 512
TM_FFN_BWD = 512
TM_MIX = 256


def _call(body, **kw):
    return pl.pallas_call(body, interpret=False, **kw)


def _params(sem=None, vmem=None):
    return pltpu.CompilerParams(dimension_semantics=sem, vmem_limit_bytes=vmem)


def _sds(shape, dtype):
    return jax.ShapeDtypeStruct(shape, dtype)


ANY = pl.BlockSpec(memory_space=pl.ANY)
VMEM = pl.BlockSpec(memory_space=pltpu.VMEM)
SMEM = pl.BlockSpec(memory_space=pltpu.SMEM)


def _norm_mod(x, gn, sc, sh):
    r = lax.rsqrt(jnp.mean(x * x, axis=-1, keepdims=True) + EPS)
    xn = x * r
    hp = xn * gn
    return r, xn, hp, hp * (1.0 + sc) + sh


def _norm_mod_bwd(dh, r, xn, hp, gn, sc):
    one_sc = 1.0 + sc
    dsh = jnp.sum(dh, axis=0, keepdims=True)
    dsc = jnp.sum(dh * hp, axis=0, keepdims=True)
    dgn = jnp.sum(dh * one_sc * xn, axis=0, keepdims=True)
    dxn = dh * (gn * one_sc)
    dx = r * (dxn - xn * jnp.mean(dxn * xn, axis=-1, keepdims=True))
    return dsh, dsc, dgn, dx


def _dot(a, b):
    return jnp.dot(a, b, preferred_element_type=F32)


def _dot_nt(a, b):
    return lax.dot_general(a, b, (((1,), (1,)), ((), ())), preferred_element_type=F32)


def _dot_tn(a, b):
    return lax.dot_general(a, b, (((0,), (0,)), ((), ())), preferred_element_type=F32)


_GELU_C = math.sqrt(2.0 / math.pi)
_GELU_A = 0.044715


def _gelu_fwd_bwd(x):
    x2 = x * x
    t = jnp.tanh(_GELU_C * (x + _GELU_A * x * x2))
    g = 0.5 * x * (1.0 + t)
    dg = 0.5 * (1.0 + t) + 0.5 * x * (1.0 - t * t) * (_GELU_C * (1.0 + 3.0 * _GELU_A * x2))
    return g, dg


def _adamw(w, g, m, v):
    m = ADAM_B1 * m + (1.0 - ADAM_B1) * g
    v = ADAM_B2 * v + (1.0 - ADAM_B2) * (g * g)
    m_hat = m / (1.0 - ADAM_B1 ** ADAM_STEP)
    v_hat = v / (1.0 - ADAM_B2 ** ADAM_STEP)
    delta = -ADAM_LR * (m_hat / (jnp.sqrt(v_hat) + ADAM_EPS) + ADAM_WD * w)
    return delta, m, v


def _row_block(rows, cap=256, mult=16):
    best = None
    for t in range(mult, min(rows, cap) + 1, mult):
        if rows % t == 0:
            best = t
    assert best is not None, rows
    return best


def _head_math(x, target, gf):
    r = lax.rsqrt(jnp.mean(x * x, axis=-1, keepdims=True) + EPS)
    xn = x * r
    err = xn * gf - target
    dy = err * (1.0 / D)
    dxn = dy * gf
    dx = r * (dxn - xn * jnp.mean(dxn * xn, axis=-1, keepdims=True))
    return (0.5 / D) * jnp.sum(err * err), jnp.sum(dy * xn, axis=0, keepdims=True), dx


def _ffn_fwd(x, modv, win, wout, side=None, head=None):
    S = x.shape[0]
    tm = TM_FFN_FWD
    nt = S // tm

    def body(*refs):
        if head is None:
            x_ref, mod_ref, wgu_ref, wo_ref, xo_ref, gs_ref, us_ref, acc_scr = refs
        else:
            (x_ref, mod_ref, wgu_ref, wo_ref, t_ref, gf_ref,
             xo_ref, gs_ref, us_ref, loss_ref, dgf_ref, acc_scr) = refs

        @pl.when((pl.program_id(0) == 0) & (pl.program_id(1) == 0))
        def _():
            acc_scr[...] = jnp.zeros_like(acc_scr)
            if head is not None:
                loss_ref[...] = jnp.zeros_like(loss_ref)
                dgf_ref[...] = jnp.zeros_like(dgf_ref)

        j = pl.program_id(1)
        h = _norm_mod(x_ref[...], mod_ref[3:4, :], mod_ref[1:2, :], mod_ref[0:1, :])[3].astype(BF16)
        gu = _dot(h, wgu_ref[...])
        g = gu[:, :CH].astype(BF16)
        u = gu[:, CH:].astype(BF16)
        gs_ref[...] = g
        us_ref[...] = u
        gf = g.astype(F32)
        a = (gf * jax.nn.sigmoid(gf) * u.astype(F32)).astype(BF16)
        acc = jnp.where(j == 0, 0.0, acc_scr[...]) + _dot(a, wo_ref[...])
        acc_scr[...] = acc
        xo = x_ref[...] + (0.5 * mod_ref[2:3, :]) * acc
        if head is None:
            xo_ref[...] = xo
        else:
            @pl.when(j == 1)
            def _():
                loss, dgf, dx = _head_math(xo, t_ref[...], gf_ref[0:1, :])
                loss_ref[...] += loss
                dgf_ref[0:1, :] += dgf
                xo_ref[...] = dx

    step = lambda i, j: lambda: (pl.program_id(0) == i) & (pl.program_id(1) == j)
    tile = pl.BlockSpec((tm, D), lambda i, j: (i, 0))
    const = lambda shape: pl.BlockSpec(shape, lambda i, j: (0, 0))
    chunk = pl.BlockSpec((tm, CH), lambda i, j: (i, j))
    in_specs = [tile, const((8, D)), pl.BlockSpec((None, D, 2 * CH), lambda i, j: (j, 0, 0)),
                pl.BlockSpec((None, CH, D), lambda i, j: (j, 0, 0))]
    out_specs = [tile, chunk, chunk]
    out_shape = [_sds((S, D), F32), _sds((S, DFF), BF16), _sds((S, DFF), BF16)]
    args = (x, modv, win, wout)
    if head is not None:
        in_specs += [tile, const((8, D))]
        out_specs += [const((8, LANE)), const((8, D))]
        out_shape += [_sds((8, LANE), F32), _sds((8, D), F32)]
        args += tuple(head)
    return _side_call(
        body, side, (step(0, 0), step((7 * nt) // 10, 0), step(nt - 1, 1)), name="ffn_fwd",
        grid=(nt, 2), in_specs=in_specs, out_specs=out_specs, out_shape=out_shape,
        scratch_shapes=[pltpu.VMEM((tm, D), F32)],
        compiler_params=_params(("arbitrary", "arbitrary"), VMEM_LIMIT),
        args=args)


def _ffn_bwd_pass(jj, x, dxo, gs, us, modv, win, wout, prev=None, side=None):
    S = x.shape[0]
    tm = TM_FFN_BWD
    nsub = 1
    nt = S // tm
    hi, ho = D // 2, CH // 4
    last = prev is not None
    assert last == (jj == 1)

    def body(*refs):
        x_ref, dxo_ref, gs_ref, us_ref, mod_ref, win_hbm, wo_ref = refs[:7]
        k = 12 if last else 7
        out_ref, dwin_ref, dwout_ref, rwin_ref, rwout_ref, vec_ref = refs[k:k + 6]
        accgu, accw, wgu, sems, fsend, frecv = refs[k + 6:]
        i = pl.program_id(0)

        @pl.when(i == 0)
        def _():
            load = pltpu.make_async_copy(win_hbm.at[jj], wgu, sems.at[0])
            load.start()
            accgu[...] = jnp.zeros_like(accgu)
            accw[...] = jnp.zeros_like(accw)
            vec_ref[...] = jnp.zeros_like(vec_ref)
            load.wait()

        gn, sc, sh, gate = mod_ref[3:4, :], mod_ref[1:2, :], mod_ref[0:1, :], mod_ref[2:3, :]

        parts = []
        for s in range(nsub):
            rs = slice(s * (tm // nsub), (s + 1) * (tm // nsub))
            r, xn, hp, h = _norm_mod(x_ref[rs, :], gn, sc, sh)
            dxo = dxo_ref[rs, :]
            dy = (dxo * (0.5 * gate)).astype(BF16)
            g = gs_ref[rs, :].astype(F32)
            u = us_ref[rs, :].astype(F32)
            sig = jax.nn.sigmoid(g)
            sl = g * sig
            a = (sl * u).astype(BF16)
            da = _dot_nt(dy, wo_ref[...])
            dg = (da * u * (sig * (1.0 + g * (1.0 - sig)))).astype(BF16)
            du = (da * sl).astype(BF16)
            dgu = jnp.concatenate([dg, du], axis=1)
            dhp = _dot_nt(dgu, wgu[...])
            parts.append((h.astype(BF16), a, dgu, dxo.astype(BF16)))
            if last:
                dsh, dsc, dgn, dxin = _norm_mod_bwd(refs[7][rs, :] + dhp, r, xn, hp, gn, sc)
                vec_ref[0:1, :] += dsh
                vec_ref[1:2, :] += dsc
                vec_ref[3:4, :] += dgn
                out_ref[rs, :] = dxo + dxin
            else:
                out_ref[rs, :] = dhp

        hb, a, dgu, dxb = [jnp.concatenate(p, axis=0) if nsub > 1 else p[0] for p in zip(*parts)]
        accw[...] += _dot_tn(a, dxb)
        accgu[...] += _dot_tn(hb, dgu)

        @pl.when(i == nt - 1)
        def _():
            gw = accw[...]
            vec_ref[2:3, :] += 0.5 * jnp.sum(wo_ref[...].astype(F32) * gw, axis=0, keepdims=True)
            accw[...] = gw * (0.5 * gate)
            mx, my, cc = _me()
            part = lambda acc, base, n, c, col: acc.at[pl.ds(base + c * n, n), pl.ds(col[0], col[1])]
            pieces = [(accgu, 0, hi, (0, CH), dwin_ref, rwin_ref, jj), (accgu, 0, hi, (CH, CH), dwin_ref, rwin_ref, 2 + jj),
                      (accw, 0, ho, (0, D), dwout_ref, rwout_ref, 2 * jj),
                      (accw, 2 * ho, ho, (0, D), dwout_ref, rwout_ref, 2 * jj + 1)]
            loc = [pltpu.make_async_copy(part(acc, base, n, cc, col), own.at[slot], sems.at[p])
                   for p, (acc, base, n, col, own, _, slot) in enumerate(pieces)]
            rem = [pltpu.make_async_remote_copy(part(acc, base, n, 1 - cc, col), sib.at[slot], fsend.at[p], frecv.at[p],
                                                device_id=(mx, my, 1 - cc), device_id_type=MESH)
                   for p, (acc, base, n, col, _, sib, slot) in enumerate(pieces)]
            for cp in loc + rem:
                cp.start()
            for cp in loc:
                cp.wait()
            for cp in rem:
                cp.wait()

    once = pl.Buffered(1)
    tile = pl.BlockSpec((tm, D), lambda i: (i, 0))
    chunk = pl.BlockSpec((tm, CH), lambda i: (i, jj))
    in_specs = [tile, tile, chunk, chunk, pl.BlockSpec((8, D), lambda i: (0, 0)), ANY,
                pl.BlockSpec((None, CH, D), lambda i: (jj, 0, 0), pipeline_mode=once)]
    args = (x, dxo, gs, us, modv, win, wout)
    if last:
        in_specs += [tile, ANY, ANY, ANY, ANY]
        args += tuple(prev)
    step = lambda s: lambda: pl.program_id(0) == s
    return _side_call(
        body, side, (step(0), None, step(nt - 1)), name="ffn_bwd",
        grid=(nt,), in_specs=in_specs,
        out_specs=[tile, ANY, ANY, ANY, ANY, pl.BlockSpec((8, D), lambda i: (0, 0))],
        out_shape=[_sds((S, D), F32), _sds((NQ, hi, CH), F32), _sds((NQ, ho, D), F32), _sds((NQ, hi, CH), F32),
                   _sds((NQ, ho, D), F32), _sds((8, D), F32)],
        scratch_shapes=[pltpu.VMEM((D, 2 * CH), F32), pltpu.VMEM((CH, D), F32), pltpu.VMEM((D, 2 * CH), BF16),
                        pltpu.SemaphoreType.DMA((4,)), pltpu.SemaphoreType.DMA((4,)), pltpu.SemaphoreType.DMA((4,))],
        aliases={8 + p: 1 + p for p in range(4)} if last else {},
        compiler_params=_params(("arbitrary",), VMEM_LIMIT),
        args=args)


def _ffn_bwd(x, dxo, gs, us, modv, win, wout, side=None):
    first, extra = _ffn_bwd_pass(0, x, dxo, gs, us, modv, win, wout, side=side)
    (dx, dwin, dwout, rwin, rwout, vec), _ = _ffn_bwd_pass(1, x, dxo, gs, us, modv, win, wout, prev=first[:5])
    return (dx, dwin, dwout, rwin, rwout, first[5] + vec), extra


def _prep_spatial(w_spatial, b_spatial_t):
    def body(w_ref, b_ref, wcat_ref, wtcat_ref, bias_ref):
        row = lax.broadcasted_iota(jnp.int32, (CHUNK, CHUNK), 0)
        col = lax.broadcasted_iota(jnp.int32, (CHUNK, CHUNK), 1)
        tril = col <= row
        for p in range(4):
            wa = jnp.where(tril, w_ref[2 * p], 0.0)
            wb = jnp.where(tril, w_ref[2 * p + 1], 0.0)
            wcat_ref[p] = jnp.concatenate([wa, wb], axis=1).astype(BF16)
            wtcat_ref[p] = jnp.concatenate([wa.T, wb.T], axis=1).astype(BF16)
        head = lax.broadcasted_iota(jnp.int32, (8, DG), 0)
        ch = lax.broadcasted_iota(jnp.int32, (8, DG), 1)
        spread = jnp.where(ch // 64 == head, 1.0, 0.0).astype(F32)
        bias_ref[...] = jnp.dot(b_ref[...], spread, precision=HIGHEST, preferred_element_type=F32)

    return _call(
        body, name="prep_spatial",
        in_specs=[VMEM, VMEM], out_specs=[VMEM, VMEM, VMEM],
        out_shape=[_sds((4, CHUNK, 2 * CHUNK), BF16), _sds((4, CHUNK, 2 * CHUNK), BF16), _sds((CHUNK, DG), F32)],
    )(w_spatial, b_spatial_t)


def _pair_rhs(blocks):
    lane = lax.broadcasted_iota(jnp.int32, (CHUNK, LANE), 1)
    lo = lane < 64
    top = jnp.concatenate([jnp.where(lo, b, 0.0) for b in blocks], axis=1)
    bot = jnp.concatenate([jnp.where(lo, 0.0, b) for b in blocks], axis=1)
    return top, bot


def _gmlp_branch(zb, vecs, wcat_ref, bias_ref, nchunks):
    z, dz = _gelu_fwd_bwd(zb)
    u = z[:, :DG]
    v = z[:, DG:]
    ln_g, ln_b = vecs[1:2, :], vecs[2:3, :]
    mu = jnp.mean(v, axis=-1, keepdims=True)
    vc = v - mu
    rstd = lax.rsqrt(jnp.mean(vc * vc, axis=-1, keepdims=True) + EPS)
    vhat = vc * rstd
    vl = vhat * ln_g + ln_b
    sv_cols = []
    for p in range(4):
        blocks = [vl[k * CHUNK:(k + 1) * CHUNK, p * LANE:(p + 1) * LANE] for k in range(nchunks)]
        top, bot = _pair_rhs(blocks)
        rhs = jnp.concatenate([top, bot], axis=0).astype(BF16)
        out = _dot(wcat_ref[p], rhs)
        bias = bias_ref[:, p * LANE:(p + 1) * LANE]
        sv_cols.append(jnp.concatenate([out[:, k * LANE:(k + 1) * LANE] + bias for k in range(nchunks)], axis=0))
    sv = jnp.concatenate(sv_cols, axis=1)
    return dict(u=u, dz=dz, rstd=rstd, vhat=vhat, vl=vl, sv=sv, yb=u * sv)


def _mix_fwd(x, modv, win, wpool, vecs, wcat, bias, wout):
    S = x.shape[0]
    tm = TM_MIX
    nt = S // tm
    nchunks = tm // CHUNK

    def body(x_ref, mod_ref, win_ref, wpool_ref, vec_ref, wcat_ref, bias_ref, wout_ref,
             xo_ref, pooled_ref, zb_ref, ext):
        i = pl.program_id(0)

        @pl.when(i == 0)
        def _():
            ext[0:HALO, :] = jnp.zeros((HALO, DP), F32)

        x = x_ref[...]
        _, _, _, h = _norm_mod(x, mod_ref[3:4, :], mod_ref[1:2, :], mod_ref[0:1, :])
        proj = _dot(h.astype(BF16), win_ref[...])
        xa = proj[:, :DP]
        zb = proj[:, DP:]
        zb_ref[...] = zb
        ext[HALO:HALO + tm, :] = xa
        pos = i * tm + lax.broadcasted_iota(jnp.int32, (tm, 1), 0)
        vecs = vec_ref[...]
        ya_cols = []
        pooled_cols = []
        for gi, w in enumerate(POOL_WINDOWS):
            cols = slice(gi * LANE, (gi + 1) * LANE)
            s = xa[:, cols]
            for k in range(1, w):
                s = s + ext[HALO - k:HALO - k + tm, cols]
            cnt = jnp.minimum(pos + 1, w).astype(F32)
            pooled = (s / cnt - xa[:, cols]).astype(BF16)
            pooled_cols.append(pooled)
            ya_cols.append(_dot(pooled, wpool_ref[gi]) * vecs[0:1, cols])
        pooled_ref[...] = jnp.concatenate(pooled_cols, axis=1)
        ext[0:HALO, :] = ext[tm:tm + HALO, :]

        gm = _gmlp_branch(zb, vecs, wcat_ref, bias_ref, nchunks)
        cat = jnp.concatenate(ya_cols + [gm["yb"]], axis=1).astype(BF16)
        xo_ref[...] = x + mod_ref[2:3, :] * _dot(cat, wout_ref[...])

    full = lambda shape: pl.BlockSpec(shape, lambda i: (0,) * len(shape))
    return _call(
        body, name="mix_fwd",
        grid=(nt,),
        in_specs=[pl.BlockSpec((tm, D), lambda i: (i, 0)), full((8, D)), full((D, DPROJ)),
                  full((4, LANE, LANE)), full((8, DP)), full((4, CHUNK, 2 * CHUNK)), full((CHUNK, DG)),
                  full((DP + DG, D))],
        out_specs=[pl.BlockSpec((tm, D), lambda i: (i, 0)), pl.BlockSpec((tm, DP), lambda i: (i, 0)),
                   pl.BlockSpec((tm, 2 * DG), lambda i: (i, 0))],
        out_shape=[_sds((S, D), F32), _sds((S, DP), BF16), _sds((S, 2 * DG), F32)],
        scratch_shapes=[pltpu.VMEM((tm + HALO, DP), F32)],
        compiler_params=_params(("arbitrary",), VMEM_LIMIT),
    )(x, modv, win, wpool, vecs, wcat, bias, wout)


def _mix_bwd(x, dxo, pooled, zb, modv, win, wpool, vecs, wcat, wtcat, bias, wout, side=None):
    S = x.shape[0]
    tm = TM_MIX
    nt = S // tm
    nchunks = tm // CHUNK

    def body(x_ref, dxo_ref, pooled_ref, zb_ref, mod_ref, win_ref, wpool_ref, vec_ref, wcat_ref, wtcat_ref,
             bias_ref, wout_ref,
             dx_ref, dwin_ref, dwout_ref, dwpool_ref, dwsp_ref, dbsp_ref, v512_ref, vd_ref, qext, dsv_acc):
        step = pl.program_id(0)
        tile = nt - 1 - step

        @pl.when(step == 0)
        def _():
            dwin_ref[...] = jnp.zeros_like(dwin_ref)
            dwout_ref[...] = jnp.zeros_like(dwout_ref)
            dwpool_ref[...] = jnp.zeros_like(dwpool_ref)
            dwsp_ref[...] = jnp.zeros_like(dwsp_ref)
            v512_ref[...] = jnp.zeros_like(v512_ref)
            vd_ref[...] = jnp.zeros_like(vd_ref)
            dsv_acc[...] = jnp.zeros_like(dsv_acc)
            qext[tm:tm + HALO, :] = jnp.zeros((HALO, DP), F32)

        gn, sc, sh, gate = mod_ref[3:4, :], mod_ref[1:2, :], mod_ref[0:1, :], mod_ref[2:3, :]
        vecs = vec_ref[...]
        x = x_ref[...]
        r, xn, hp, h = _norm_mod(x, gn, sc, sh)
        hb = h.astype(BF16)
        dxo = dxo_ref[...]

        pooled = pooled_ref[...]
        mixed_cols = [_dot(pooled[:, gi * LANE:(gi + 1) * LANE], wpool_ref[gi]) for gi in range(4)]
        mixed = jnp.concatenate(mixed_cols, axis=1)
        scale = vecs[0:1, :]
        gm = _gmlp_branch(zb_ref[...], vecs, wcat_ref, bias_ref, nchunks)
        cat = jnp.concatenate([mixed * scale, gm["yb"]], axis=1).astype(BF16)

        dwout_ref[...] += _dot_tn(cat, dxo.astype(BF16))
        dcat = _dot_nt((dxo * gate).astype(BF16), wout_ref[...])
        dya = dcat[:, :DP]
        dyb = dcat[:, DP:]

        v512_ref[0:1, :] += jnp.sum(dya * mixed, axis=0, keepdims=True)
        dmixed = (dya * scale).astype(BF16)
        pos = tile * tm + lax.broadcasted_iota(jnp.int32, (tm, 1), 0)
        dpooled_cols = []
        for gi, w in enumerate(POOL_WINDOWS):
            cols = slice(gi * LANE, (gi + 1) * LANE)
            dp = _dot_nt(dmixed[:, cols], wpool_ref[gi])
            dwpool_ref[gi] += _dot_tn(pooled[:, cols], dmixed[:, cols])
            cnt = jnp.minimum(pos + 1, w).astype(F32)
            qext[0:tm, cols] = dp / cnt
            dpooled_cols.append(dp)
        dxa_cols = []
        for gi, w in enumerate(POOL_WINDOWS):
            cols = slice(gi * LANE, (gi + 1) * LANE)
            s = qext[0:tm, cols]
            for k in range(1, w):
                s = s + qext[k:k + tm, cols]
            dxa_cols.append(s - dpooled_cols[gi])
        qext[tm:tm + HALO, :] = qext[0:HALO, :]

        u, sv, vl = gm["u"], gm["sv"], gm["vl"]
        du = dyb * sv
        dsv = dyb * u
        dvl_cols = []
        for p in range(4):
            cols = slice(p * LANE, (p + 1) * LANE)
            dblocks = [dsv[k * CHUNK:(k + 1) * CHUNK, cols] for k in range(nchunks)]
            vblocks = [vl[k * CHUNK:(k + 1) * CHUNK, cols] for k in range(nchunks)]
            tot = dblocks[0]
            for b in dblocks[1:]:
                tot = tot + b
            dsv_acc[:, cols] += tot
            top, bot = _pair_rhs(dblocks)
            out = _dot(wtcat_ref[p], jnp.concatenate([top, bot], axis=0).astype(BF16))
            dvl_cols.append(jnp.concatenate([out[:, k * LANE:(k + 1) * LANE] for k in range(nchunks)], axis=0))
            vcat = jnp.concatenate(vblocks, axis=1).astype(BF16)
            dwsp_ref[2 * p] += _dot_nt(top.astype(BF16), vcat)
            dwsp_ref[2 * p + 1] += _dot_nt(bot.astype(BF16), vcat)
        dvl = jnp.concatenate(dvl_cols, axis=1)
        vhat, rstd = gm["vhat"], gm["rstd"]
        v512_ref[1:2, :] += jnp.sum(dvl * vhat, axis=0, keepdims=True)
        v512_ref[2:3, :] += jnp.sum(dvl, axis=0, keepdims=True)
        dvh = dvl * vecs[1:2, :]
        dv = rstd * (dvh - jnp.mean(dvh, axis=-1, keepdims=True)
                     - vhat * jnp.mean(dvh * vhat, axis=-1, keepdims=True))
        dzb = jnp.concatenate([du, dv], axis=1) * gm["dz"]

        dproj = jnp.concatenate(dxa_cols + [dzb], axis=1).astype(BF16)
        dwin_ref[...] += _dot_tn(hb, dproj)
        dh = _dot_nt(dproj, win_ref[...])
        dsh, dsc, dgn, dxin = _norm_mod_bwd(dh, r, xn, hp, gn, sc)
        vd_ref[0:1, :] += dsh
        vd_ref[1:2, :] += dsc
        vd_ref[3:4, :] += dgn
        dx_ref[...] = dxo + dxin

        @pl.when(step == nt - 1)
        def _():
            gw = dwout_ref[...]
            vd_ref[2:3, :] += jnp.sum(wout_ref[...].astype(F32) * gw, axis=0, keepdims=True)
            dwout_ref[...] = gw * gate
            row = lax.broadcasted_iota(jnp.int32, (CHUNK, CHUNK), 0)
            col = lax.broadcasted_iota(jnp.int32, (CHUNK, CHUNK), 1)
            for hh in range(8):
                dwsp_ref[hh] = jnp.where(col <= row, dwsp_ref[hh], 0.0)
            head = lax.broadcasted_iota(jnp.int32, (8, DG), 0)
            ch = lax.broadcasted_iota(jnp.int32, (8, DG), 1)
            spread = jnp.where(ch // 64 == head, 1.0, 0.0).astype(F32)
            dbsp_ref[...] = lax.dot_general(spread, dsv_acc[...], (((1,), (1,)), ((), ())),
                                            precision=HIGHEST, preferred_element_type=F32)

    full = lambda shape: pl.BlockSpec(shape, lambda s: (0,) * len(shape))
    rev = lambda cols: pl.BlockSpec((tm, cols), lambda s: (nt - 1 - s, 0))
    step = lambda s: lambda: pl.program_id(0) == s
    return _side_call(
        body, side, (step(0), None, step(nt - 1)), name="mix_bwd",
        grid=(nt,),
        in_specs=[rev(D), rev(D), rev(DP), rev(2 * DG), full((8, D)), full((D, DPROJ)), full((4, LANE, LANE)),
                  full((8, DP)), full((4, CHUNK, 2 * CHUNK)), full((4, CHUNK, 2 * CHUNK)), full((CHUNK, DG)),
                  full((DP + DG, D))],
        out_specs=[rev(D), full((D, DPROJ)), full((DP + DG, D)), full((4, LANE, LANE)), full((8, CHUNK, CHUNK)),
                   full((8, CHUNK)), full((8, DP)), full((8, D))],
        out_shape=[_sds((S, D), F32), _sds((D, DPROJ), F32), _sds((DP + DG, D), F32), _sds((4, LANE, LANE), F32),
                   _sds((8, CHUNK, CHUNK), F32), _sds((8, CHUNK), F32), _sds((8, DP), F32), _sds((8, D), F32)],
        scratch_shapes=[pltpu.VMEM((tm + HALO, DP), F32), pltpu.VMEM((CHUNK, DG), F32)],
        compiler_params=_params(("arbitrary",), VMEM_LIMIT),
        args=(x, dxo, pooled, zb, modv, win, wpool, vecs, wcat, wtcat, bias, wout))


def _chip_sum(g, rbuf, core):
    _, _, hr, cols = g.shape
    tr = _row_block(hr)

    def body(c_ref, g_ref, r_ref, o_ref):
        o_ref[...] = (g_ref[...] + r_ref[...]).astype(BF16)

    return pl.pallas_call(
        body, name="chip_sum", interpret=False,
        grid_spec=pltpu.PrefetchScalarGridSpec(
            num_scalar_prefetch=1, grid=(NQ, hr // tr),
            in_specs=[pl.BlockSpec((None, None, tr, cols), lambda q, i, c: (q, c[0], i, 0)),
                      pl.BlockSpec((None, tr, cols), lambda q, i, c: (q, i, 0))],
            out_specs=pl.BlockSpec((None, tr, cols), lambda q, i, c: (q, i, 0))),
        out_shape=_sds((NQ, hr, cols), BF16),
        compiler_params=_params(("arbitrary", "arbitrary"), None),
    )(core, g, rbuf)


def _chip_sum_pair(own, rbuf, slots=(0, 1, 2, 3), prev=None):
    _, hr, cols = own.shape
    tr = _row_block(hr)
    a, b = slots[0], (slots[1] - slots[0] if len(slots) > 1 else 0)
    assert list(slots) == [a + b * k for k in range(len(slots))]

    def body(a_ref, b_ref, *rest):
        rest[-1][...] = (a_ref[...] + b_ref[...]).astype(BF16)

    spec = pl.BlockSpec((None, tr, cols), lambda k, i: (a + b * k, i, 0))
    return _call(
        body, name="chip_sum_pair",
        grid=(len(slots), hr // tr),
        in_specs=[spec, spec] + ([ANY] if prev is not None else []), out_specs=spec,
        out_shape=_sds((NQ, hr, cols), BF16),
        input_output_aliases={2: 0} if prev is not None else {},
        compiler_params=_params(("arbitrary", "arbitrary"), None),
    )(own, rbuf, *([prev] if prev is not None else []))


def _sum4(cs, rbuf, chip, after=()):
    _, hr, cols = rbuf.shape
    tr = _row_block(hr)

    def body(q_ref, c_ref, r1_ref, r2_ref, r3_ref, *rest):
        acc = c_ref[...].astype(F32)
        for r in (r1_ref, r2_ref, r3_ref):
            acc = acc + r[...].astype(F32)
        rest[-1][...] = acc

    slot = lambda k: pl.BlockSpec((None, tr, cols), lambda i, q: ((q[0] + k) % NQ, i, 0))
    return pl.pallas_call(
        body, name="sum4", interpret=False,
        grid_spec=pltpu.PrefetchScalarGridSpec(
            num_scalar_prefetch=1, grid=(hr // tr,),
            in_specs=[slot(0), slot(1), slot(2), slot(3)] + [ANY] * len(after),
            out_specs=pl.BlockSpec((tr, cols), lambda i, q: (i, 0))),
        out_shape=_sds((hr, cols), F32),
        compiler_params=_params(("arbitrary",), None),
    )(chip, cs, rbuf, rbuf, rbuf, *after)


def _adamw_halves(w, own, recv, m, v, side=None, after=()):
    rows, cols = w.shape
    hr = rows // 2
    tr = _row_block(hr, mult=8)
    nb = hr // tr

    def body(w_ref, own_ref, recv_ref, m_ref, v_ref, *rest):
        g_ref, d_ref, mo_ref, vo_ref = rest[len(after):]
        g = jnp.where(pl.program_id(0) == lax.axis_index("c"), own_ref[...], recv_ref[...])
        d, mn, vn = _adamw(w_ref[...], g, m_ref[...], v_ref[...])
        g_ref[...] = g
        d_ref[...] = d
        mo_ref[...] = mn
        vo_ref[...] = vn

    full = pl.BlockSpec((tr, cols), lambda h, i: (h * nb + i, 0))
    half = pl.BlockSpec((tr, cols), lambda h, i: (i, 0))
    step = lambda h, i: lambda: (pl.program_id(0) == h) & (pl.program_id(1) == i)
    return _side_call(
        body, side, (step(0, 0), None, step(1, nb - 1)), name="adamw_halves",
        grid=(2, nb), in_specs=[full, half, half, full, full] + [ANY] * len(after), out_specs=[full] * 4,
        out_shape=[_sds((rows, cols), F32)] * 4, scratch_shapes=[],
        compiler_params=_params(("arbitrary", "arbitrary"), None),
        args=(w, own, recv, m, v, *after))


def _cast_place(w, chip, wide=False):
    rows, cols = w.shape
    tr = _row_block(rows)

    def body(q_ref, w_ref, o_ref):
        o_ref[...] = w_ref[...].astype(BF16)

    if wide:
        out_spec = pl.BlockSpec((None, tr, cols), lambda i, q: (q[0] % 2, i, q[0] // 2))
    else:
        out_spec = pl.BlockSpec((None, tr, cols), lambda i, q: (q[0], i, 0))
    return pl.pallas_call(
        body, name="cast_place", interpret=False,
        grid_spec=pltpu.PrefetchScalarGridSpec(
            num_scalar_prefetch=1, grid=(rows // tr,),
            in_specs=[pl.BlockSpec((tr, cols), lambda i, q: (i, 0))],
            out_specs=out_spec),
        out_shape=_sds((2, rows, 2 * cols) if wide else (NQ, rows, cols), BF16),
        compiler_params=_params(("arbitrary",), None),
    )(chip, w)


def _ada_grad_adamw(cact_t, dmod_q, w, m, v, side=None):
    rows, cols = w.shape
    tc = 256
    assert cols % tc == 0

    def body(c_ref, d_ref, w_ref, m_ref, v_ref, g_ref, dl_ref, mo_ref, vo_ref):
        g = jnp.dot(c_ref[...], d_ref[...], precision=HIGHEST, preferred_element_type=F32)
        d, mn, vn = _adamw(w_ref[...], g, m_ref[...], v_ref[...])
        g_ref[...] = g
        dl_ref[...] = d
        mo_ref[...] = mn
        vo_ref[...] = vn

    spec = pl.BlockSpec((rows, tc), lambda i: (0, i))
    step = lambda s: lambda: pl.program_id(0) == s
    return _side_call(
        body, side, (step(0), None, step(cols // tc - 1)), name="ada_grad_adamw",
        grid=(cols // tc,),
        in_specs=[pl.BlockSpec((rows, 8), lambda i: (0, 0)), pl.BlockSpec((8, tc), lambda i: (0, i)),
                  spec, spec, spec],
        out_specs=[spec] * 4,
        out_shape=[_sds((rows, cols), F32)] * 4,
        scratch_shapes=[],
        compiler_params=_params(("arbitrary",), None),
        args=(cact_t, dmod_q, w, m, v))


def _me():
    x, y, c = lax.axis_index("x"), lax.axis_index("y"), lax.axis_index("c")
    return x, y, c


_OFFSETS7 = [(dx, dy, dc) for dx in (0, 1) for dy in (0, 1) for dc in (0, 1) if (dx, dy, dc) != (0, 0, 0)]
_CHIP_OFFSETS = [(1, 0), (0, 1), (1, 1)]


def _ada_fwd(c, w_ada_q, b_ada_q, side=None):
    ncol = w_ada_q.shape[1]

    def body(c_ref, w_ref, b_ref, cact_ref, modsel_ref, blk, gath, res, parts, send_sems, recv_sems, side_start=None):
        x, y, cc = _me()
        me = 4 * x + 2 * y + cc
        q = 2 * x + y
        cv = c_ref[...]
        ca = cv * jax.nn.sigmoid(cv)
        row = lax.broadcasted_iota(jnp.int32, (8, D), 0)
        blk[...] = jnp.where(row == me, jnp.broadcast_to(ca, (8, D)), 0.0)
        gath[me] = blk[...]
        sends = []
        for k, (dx, dy, dc) in enumerate(_OFFSETS7):
            cp = pltpu.make_async_remote_copy(blk, gath.at[me], send_sems.at[k], recv_sems.at[k],
                                              device_id=(x ^ dx, y ^ dy, cc ^ dc), device_id_type=MESH)
            cp.start()
            sends.append(cp)
        if side_start is not None:
            side_start()
        for cp in sends:
            cp.wait_recv()
        cact = gath[0]
        for d in range(1, N_DEV):
            cact = cact + gath[d]
        cact_ref[...] = cact
        res[...] = jnp.dot(cact, w_ref[...], precision=HIGHEST, preferred_element_type=F32) + b_ref[...]
        parts[q] = res[...]
        sends2 = []
        for k, (dx, dy) in enumerate(_CHIP_OFFSETS):
            cp = pltpu.make_async_remote_copy(res, parts.at[q], send_sems.at[7 + k], recv_sems.at[7 + k],
                                              device_id=(x ^ dx, y ^ dy, cc), device_id_type=MESH)
            cp.start()
            sends2.append(cp)
        for cp in sends2:
            cp.wait_recv()
        row2 = lax.broadcasted_iota(jnp.int32, (8, ncol), 0)
        out = jnp.zeros((8, ncol), F32)
        for s in range(NQ):
            mine = jnp.sum(jnp.where(row2 == me, parts[s], 0.0), axis=0, keepdims=True)
            out = out + jnp.where(row2 == s, jnp.broadcast_to(mine, (8, ncol)), 0.0)
        modsel_ref[...] = out
        for cp in sends + sends2:
            cp.wait_send()

    return _side_call(
        body, side, None, name="ada_fwd",
        in_specs=[VMEM, VMEM, VMEM], out_specs=[VMEM, VMEM],
        out_shape=[_sds((8, D), F32), _sds((8, ncol), F32)],
        scratch_shapes=[pltpu.VMEM((8, D), F32), pltpu.VMEM((N_DEV, 8, D), F32), pltpu.VMEM((8, ncol), F32),
                        pltpu.VMEM((NQ, 8, ncol), F32), pltpu.SemaphoreType.DMA((10,)), pltpu.SemaphoreType.DMA((10,))],
        compiler_params=_params(None, VMEM_LIMIT), start_in_body=side is not None,
        args=(c, w_ada_q, b_ada_q))


class _Side:
    def __init__(self, ins, out_shapes, aliases, nsem, start, mid=None, finish=None):
        self.ins, self.out_shapes, self.aliases, self.nsem = list(ins), list(out_shapes), dict(aliases), nsem
        self.start, self.mid, self.finish = start, mid, finish


def _join(*sides):
    ins, outs, aliases, offs, nsem = [], [], {}, [], 0
    for s in sides:
        offs.append((len(ins), len(outs), nsem))
        aliases.update({len(ins) + a: len(outs) + b for a, b in s.aliases.items()})
        ins += s.ins
        outs += s.out_shapes
        nsem += s.nsem

    def hook(name):
        def run(i, o, ss, rs, base):
            for s, (io, oo, so) in zip(sides, offs):
                fn = getattr(s, name)
                if fn is not None:
                    fn(i[io:io + len(s.ins)], o[oo:oo + len(s.out_shapes)], ss, rs, base + so)
        return run

    return _Side(ins, outs, aliases, nsem, hook("start"), hook("mid"), hook("finish"))


def _side_call(body, side, when, *, name, in_specs, out_specs, out_shape, scratch_shapes, args, aliases=None,
               start_in_body=False, **kw):
    n_in, n_out = len(in_specs), len(out_specs)
    aliases = dict(aliases or {})
    if side is None:
        return _call(body, name=name, in_specs=in_specs, out_specs=out_specs, out_shape=out_shape,
                     scratch_shapes=scratch_shapes, input_output_aliases=aliases, **kw)(*args), []
    ns_in, ns_out = len(side.ins), len(side.out_shapes)

    def hook(fn, k, operands):
        if fn is None:
            return
        if when is None:
            fn(*operands, 0)
        elif when[k] is not None:
            pl.when(when[k]())(functools.partial(fn, *operands, 0))

    def wrapped(*refs):
        ins, s_ins = refs[:n_in], refs[n_in:n_in + ns_in]
        o0 = n_in + ns_in
        outs, s_outs = refs[o0:o0 + n_out], refs[o0 + n_out:o0 + n_out + ns_out]
        rest = refs[o0 + n_out + ns_out:]
        scratch, operands = rest[:-2], (s_ins, s_outs, rest[-2], rest[-1])
        if start_in_body:
            body(*ins, *outs, *scratch, side_start=functools.partial(hook, side.start, 0, operands))
        else:
            hook(side.start, 0, operands)
            body(*ins, *outs, *scratch)
        hook(side.mid, 1, operands)
        hook(side.finish, 2, operands)

    res = _call(
        wrapped, name=name,
        in_specs=list(in_specs) + [ANY] * ns_in, out_specs=list(out_specs) + [ANY] * ns_out,
        out_shape=list(out_shape) + side.out_shapes,
        scratch_shapes=list(scratch_shapes) + [pltpu.SemaphoreType.DMA((side.nsem,)),
                                               pltpu.SemaphoreType.DMA((side.nsem,))],
        input_output_aliases={**aliases, **{n_in + a: n_out + b for a, b in side.aliases.items()}},
        **kw)(*args, *side.ins)
    return res[:n_out], res[n_out:]


def _run_side(side, name):
    return _side_call(lambda: None, side, None, name=name, in_specs=[], out_specs=[], out_shape=[],
                      scratch_shapes=[], args=[])[1]


def _remote(src, dst, ss, rs, k, dev):
    return pltpu.make_async_remote_copy(src, dst, ss.at[k], rs.at[k], device_id=dev, device_id_type=MESH)


def _gather_side(bufs):
    n = len(bufs)

    def plan(outs, w):
        x, y, cc = _me()
        hr = outs[w].shape[1] // 2
        mine, other = cc * hr, (1 - cc) * hr
        qx, qy, qd, q = 2 * (x ^ 1) + y, 2 * x + (y ^ 1), 2 * (x ^ 1) + (y ^ 1), 2 * x + y
        xn, yn, sib = (x ^ 1, y, cc), (x, y ^ 1, cc), (x, y, 1 - cc)
        if outs[w].shape[0] == NQ:
            at = lambda slot, r0, nr: outs[w].at[slot, pl.ds(r0, nr)]
        else:
            cols = outs[w].shape[2] // 2
            at = lambda slot, r0, nr: outs[w].at[slot % 2, pl.ds(r0, nr), pl.ds((slot // 2) * cols, cols)]
        send = [(at(q, mine, hr), xn), (at(q, mine, hr), yn),
                (at(qx, mine, hr // 2), yn), (at(qy, mine + hr // 2, hr // 2), xn),
                (at(qx, mine, hr), sib), (at(qy, mine, hr), sib), (at(qd, mine, hr), sib)]
        recv = [at(qx, mine, hr), at(qy, mine, hr), at(qd, mine, hr // 2), at(qd, mine + hr // 2, hr // 2),
                at(qx, other, hr), at(qy, other, hr), at(qd, other, hr)]
        return send, recv

    def op(outs, ss, rs, b, w, k, what):
        send, recv = plan(outs, w)
        if what == "wait_recv":
            _remote(recv[k], recv[k], ss, rs, b + 7 * w + k, send[k][1]).wait_recv()
        else:
            getattr(_remote(send[k][0], send[k][0], ss, rs, b + 7 * w + k, send[k][1]), what)()

    def start(ins, outs, ss, rs, b):
        for w in range(n):
            for k in (0, 1):
                op(outs, ss, rs, b, w, k, "start")

    def mid(ins, outs, ss, rs, b):
        for w in range(n):
            for k in (0, 1):
                op(outs, ss, rs, b, w, k, "wait_recv")
                op(outs, ss, rs, b, w, 2 + k, "start")
                op(outs, ss, rs, b, w, 4 + k, "start")

    def finish(ins, outs, ss, rs, b):
        for w in range(n):
            for k in (2, 3):
                op(outs, ss, rs, b, w, k, "wait_recv")
            op(outs, ss, rs, b, w, 6, "start")
        for w in range(n):
            for k in (4, 5, 6):
                op(outs, ss, rs, b, w, k, "wait_recv")
            for k in range(7):
                op(outs, ss, rs, b, w, k, "wait_send")

    return _Side(bufs, [_sds(tuple(w.shape), w.dtype) for w in bufs], {i: i for i in range(n)}, 7 * n,
                 start, mid, finish)


def _copies_side(ins, out_shapes, nsem, copies):
    def start(*a):
        for cp in copies(*a):
            cp.start()

    def finish(*a):
        for cp in copies(*a):
            cp.wait()

    return _Side(ins, out_shapes, {}, nsem, start, None, finish)


def _swap_side(gs):
    def copies(ins, outs, ss, rs, b):
        x, y, cc = _me()
        return [_remote(ins[w].at[:, 1 - cc], outs[w], ss, rs, b + w, (x, y, 1 - cc)) for w in range(len(gs))]

    return _copies_side(gs, [_sds((NQ,) + tuple(g.shape[2:]), F32) for g in gs], len(gs), copies)


def _exchange_side(cs, slots=None, prev=None):
    n = len(cs)
    slots = slots or [(0, 1, 2, 3)] * n

    def among(chip, allowed):
        hit = chip == allowed[0]
        for s in allowed[1:]:
            hit = hit | (chip == s)
        return hit

    def each(ins, outs, ss, rs, b, do_send, do_recv):
        x, y, cc = _me()
        q = 2 * x + y
        for w in range(n):
            for j, (dx, dy) in enumerate(_CHIP_OFFSETS):
                pq = 2 * (x ^ dx) + (y ^ dy)
                cp = _remote(ins[w].at[pq], outs[w].at[q], ss, rs, b + 3 * w + j, (x ^ dx, y ^ dy, cc))
                if do_send is not None:
                    pl.when(among(pq, slots[w]))(functools.partial(do_send, cp))
                if do_recv is not None:
                    pl.when(among(q, slots[w]))(functools.partial(do_recv, cp))

    def start(ins, outs, ss, rs, b):
        each(ins, outs, ss, rs, b, lambda cp: cp.start(), None)

    def finish(ins, outs, ss, rs, b):
        each(ins, outs, ss, rs, b, lambda cp: cp.wait_send(), lambda cp: cp.wait_recv())

    ins = list(cs) + (list(prev) if prev is not None else [])
    aliases = {n + w: w for w in range(n)} if prev is not None else {}
    return _Side(ins, [_sds(tuple(c.shape), c.dtype) for c in cs], aliases, 3 * n, start, None, finish)


def _exchange_copies(srcs, lands, send_sems, recv_sems):
    x, y, cc = _me()
    return [pltpu.make_async_remote_copy(srcs[w].at[2 * (x ^ dx) + (y ^ dy)], lands[w].at[2 * x + y],
                                         send_sems.at[3 * w + j], recv_sems.at[3 * w + j],
                                         device_id=(x ^ dx, y ^ dy, cc), device_id_type=MESH)
            for w in range(len(srcs)) for j, (dx, dy) in enumerate(_CHIP_OFFSETS)]


def _exchange_start(cs, after=()):
    n = len(cs)
    hbm, sem = pl.BlockSpec(memory_space=pltpu.HBM), pl.BlockSpec(memory_space=pltpu.SEMAPHORE)
    srcs = [pltpu.with_memory_space_constraint(c, pltpu.HBM) for c in cs]
    lands = [pltpu.with_memory_space_constraint(lax.empty(c.shape, c.dtype), pltpu.HBM) for c in cs]

    def body(*refs):
        sems = 2 * n + len(after)
        for cp in _exchange_copies(refs[:n], refs[n:2 * n], refs[sems], refs[sems + 1]):
            cp.start()
        refs[-1][...] = jnp.zeros_like(refs[-1])

    res = pl.pallas_call(
        body, name="exchange_start", interpret=False,
        out_shape=(pltpu.SemaphoreType.DMA((3 * n,)), pltpu.SemaphoreType.DMA((3 * n,)),
                   *[pltpu.HBM(c.shape, c.dtype) for c in cs], *[pltpu.HBM(c.shape, c.dtype) for c in cs],
                   _sds((8, LANE), F32)),
        in_specs=(hbm,) * (2 * n) + (ANY,) * len(after), out_specs=(sem, sem) + (hbm,) * (2 * n) + (VMEM,),
        input_output_aliases={i: 2 + i for i in range(2 * n)},
        compiler_params=pltpu.CompilerParams(has_side_effects=pltpu.SideEffectType.DATAFLOW_SIDE_EFFECTING),
    )(*srcs, *lands, *after)
    return res[0], res[1], list(res[2:2 + n]), list(res[2 + n:2 + 2 * n]), res[-1]


def _exchange_wait(send_sems, recv_sems, srcs, lands, after):
    n = len(srcs)
    hbm, sem = pl.BlockSpec(memory_space=pltpu.HBM), pl.BlockSpec(memory_space=pltpu.SEMAPHORE)

    def body(*refs):
        for cp in _exchange_copies(refs[:n], refs[n:2 * n], refs[2 * n], refs[2 * n + 1]):
            cp.wait_send()
            cp.wait_recv()

    res = pl.pallas_call(
        body, name="exchange_wait", interpret=False,
        out_shape=[pltpu.HBM(c.shape, c.dtype) for c in srcs + lands],
        in_specs=(hbm,) * (2 * n) + (sem, sem) + (ANY,) * len(after), out_specs=(hbm,) * (2 * n),
        input_output_aliases={i: i for i in range(2 * n)},
        compiler_params=pltpu.CompilerParams(has_side_effects=pltpu.SideEffectType.DATAFLOW_SIDE_EFFECTING),
    )(*srcs, *lands, send_sems, recv_sems, *after)
    return list(res[:n]), list(res[n:])


def _share_side(fs):
    def copies(ins, outs, ss, rs, b):
        x, y, cc = _me()
        return [_remote(ins[w], outs[w], ss, rs, b + w, (x, y, 1 - cc)) for w in range(len(fs))]

    return _copies_side(fs, [_sds(tuple(f.shape), F32) for f in fs], len(fs), copies)


def _small_allreduce_adamw(g, w, m, v, nd):
    rows = g.shape[0]
    nr = rows - nd
    hr = nr // 2
    assert nd % 8 == 0 and hr % 8 == 0

    def body(g_ref, w_ref, m_ref, v_ref, gs_ref, d_ref, mo_ref, vo_ref, gath, sib, csum, slots, tot, ss, rs):
        x, y, cc = _me()
        me = 4 * x + 2 * y + cc
        q = 2 * x + y
        sibling = (x, y, 1 - cc)
        dm = g_ref.at[pl.ds(0, nd)]
        gath[me] = g_ref[0:nd, :]
        to_all = [_remote(dm, gath.at[me], ss, rs, k, (x ^ dx, y ^ dy, cc ^ dc)) for k, (dx, dy, dc) in enumerate(_OFFSETS7)]
        to_sib = _remote(g_ref.at[pl.ds(nd, nr)], sib, ss, rs, 7, sibling)
        for cp in to_all + [to_sib]:
            cp.start()
        to_sib.wait_recv()
        csum[...] = g_ref[nd:, :] + sib[...]
        mine = pl.ds(pl.multiple_of(cc * hr, 8), hr)
        slots[q] = csum[mine, :]
        to_chips = [_remote(csum.at[mine], slots.at[q], ss, rs, 8 + j, (x ^ dx, y ^ dy, cc))
                    for j, (dx, dy) in enumerate(_CHIP_OFFSETS)]
        for cp in to_chips:
            cp.start()
        for cp in to_chips:
            cp.wait_recv()
        tot[mine, :] = (slots[0] + slots[1]) + (slots[2] + slots[3])
        halves = _remote(tot.at[mine], tot.at[mine], ss, rs, 11, sibling)
        halves.start()
        for cp in to_all:
            cp.wait_recv()
        dsum = gath[0]
        for dev in range(1, N_DEV):
            dsum = dsum + gath[dev]
        halves.wait_recv()
        for lo, n, total in ((0, nd, dsum), (nd, nr, tot[...])):
            gs_ref[lo:lo + n, :] = total
            d, mn, vn = _adamw(w_ref[lo:lo + n, :], total, m_ref[lo:lo + n, :], v_ref[lo:lo + n, :])
            d_ref[lo:lo + n, :] = d
            mo_ref[lo:lo + n, :] = mn
            vo_ref[lo:lo + n, :] = vn
        for cp in to_all + [to_sib, halves] + to_chips:
            cp.wait_send()

    return _call(
        body, name="small_allreduce_adamw",
        in_specs=[VMEM] * 4, out_specs=[VMEM] * 5,
        out_shape=[_sds((rows, LANE), F32)] * 4 + [_sds((N_DEV, nd, LANE), F32)],
        scratch_shapes=[pltpu.VMEM((nr, LANE), F32), pltpu.VMEM((nr, LANE), F32), pltpu.VMEM((NQ, hr, LANE), F32),
                        pltpu.VMEM((nr, LANE), F32), pltpu.SemaphoreType.DMA((12,)), pltpu.SemaphoreType.DMA((12,))],
        compiler_params=_params(None, VMEM_LIMIT),
    )(g, w, m, v)


_SMALL = ["b_ada", "norm_ffn1_g", "norm_mix_g", "pool_scale", "gmlp_ln_g", "gmlp_ln_b", "b_spatial",
          "norm_ffn2_g", "norm_final_g", "w_pool", "w_spatial"]


def _pack(parts):
    blocks, layout, r0 = [], {}, 0
    for name in _SMALL:
        a = parts[name]
        n = a.size
        rows = -(-n // LANE)
        rows8 = -(-rows // 8) * 8
        flat = a.reshape(-1).astype(F32)
        if rows8 * LANE != n:
            flat = jnp.concatenate([flat, jnp.zeros((rows8 * LANE - n,), F32)])
        blocks.append(flat.reshape(rows8, LANE))
        layout[name] = (r0, n, a.shape)
        r0 += rows8
    return jnp.concatenate(blocks, axis=0), layout


def _unpack(packed, layout):
    out = {}
    for name, (r0, n, shape) in layout.items():
        rows = -(-n // LANE)
        out[name] = packed[r0:r0 + rows].reshape(-1)[:n].reshape(shape)
    return out


def _modv(mod9, sub, gain):
    rows = jnp.concatenate([mod9[3 * sub:3 * sub + 3], gain.reshape(1, D), jnp.zeros((4, D), F32)], axis=0)
    return rows


_BIG = ["ffn1_w_in", "ffn1_w_out", "w_mix_in", "w_mix_out", "ffn2_w_in", "ffn2_w_out"]


def kernel(x, c, w_ada, b_ada, norm_ffn1_g, ffn1_w_in, ffn1_w_out, norm_mix_g, w_mix_in, w_pool, pool_scale, gmlp_ln_g, gmlp_ln_b, w_spatial, b_spatial, w_mix_out, norm_ffn2_g, ffn2_w_in, ffn2_w_out, norm_final_g, loss_target, m_w_ada, m_b_ada, m_norm_ffn1_g, m_ffn1_w_in, m_ffn1_w_out, m_norm_mix_g, m_w_mix_in, m_w_pool, m_pool_scale, m_gmlp_ln_g, m_gmlp_ln_b, m_w_spatial, m_b_spatial, m_w_mix_out, m_norm_ffn2_g, m_ffn2_w_in, m_ffn2_w_out, m_norm_final_g, v_w_ada, v_b_ada, v_norm_ffn1_g, v_ffn1_w_in, v_ffn1_w_out, v_norm_mix_g, v_w_mix_in, v_w_pool, v_pool_scale, v_gmlp_ln_g, v_gmlp_ln_b, v_w_spatial, v_b_spatial, v_w_mix_out, v_norm_ffn2_g, v_ffn2_w_in, v_ffn2_w_out, v_norm_final_g):
    names = ["w_ada", "b_ada", "norm_ffn1_g", "ffn1_w_in", "ffn1_w_out", "norm_mix_g", "w_mix_in", "w_pool",
             "pool_scale", "gmlp_ln_g", "gmlp_ln_b", "w_spatial", "b_spatial", "w_mix_out", "norm_ffn2_g",
             "ffn2_w_in", "ffn2_w_out", "norm_final_g"]
    W = dict(zip(names, [w_ada, b_ada, norm_ffn1_g, ffn1_w_in, ffn1_w_out, norm_mix_g, w_mix_in, w_pool, pool_scale,
                         gmlp_ln_g, gmlp_ln_b, w_spatial, b_spatial, w_mix_out, norm_ffn2_g, ffn2_w_in, ffn2_w_out,
                         norm_final_g]))
    M = dict(zip(names, [m_w_ada, m_b_ada, m_norm_ffn1_g, m_ffn1_w_in, m_ffn1_w_out, m_norm_mix_g, m_w_mix_in, m_w_pool,
                         m_pool_scale, m_gmlp_ln_g, m_gmlp_ln_b, m_w_spatial, m_b_spatial, m_w_mix_out, m_norm_ffn2_g,
                         m_ffn2_w_in, m_ffn2_w_out, m_norm_final_g]))
    V = dict(zip(names, [v_w_ada, v_b_ada, v_norm_ffn1_g, v_ffn1_w_in, v_ffn1_w_out, v_norm_mix_g, v_w_mix_in, v_w_pool,
                         v_pool_scale, v_gmlp_ln_g, v_gmlp_ln_b, v_w_spatial, v_b_spatial, v_w_mix_out, v_norm_ffn2_g,
                         v_ffn2_w_in, v_ffn2_w_out, v_norm_final_g]))

    xi, yi, ci = _me()
    q = 2 * xi + yi
    core = ci.astype(jnp.int32).reshape(1)

    chip = q.astype(jnp.int32).reshape(1)
    place = lambda n: _cast_place(W[n][0], chip, wide=n in ("ffn1_w_in", "ffn2_w_in"))

    ncol = w_ada.shape[2]
    b_q = lax.dynamic_slice_in_dim(b_ada, q * ncol, ncol, axis=1)
    (cact_all, modsel), (win1, wout1) = _ada_fwd(
        c, w_ada[0], b_q, side=_gather_side([place("ffn1_w_in"), place("ffn1_w_out")]))
    mod9 = modsel[:NQ].reshape(9, D)
    xs, target = x[0], loss_target[0]
    mv1 = _modv(mod9, 0, norm_ffn1_g[0])
    mv2 = _modv(mod9, 1, norm_mix_g[0])
    mv3 = _modv(mod9, 2, norm_ffn2_g[0])
    wcat, wtcat, bias = _prep_spatial(w_spatial[0], b_spatial[0].T)
    wpool = w_pool[0].astype(BF16)
    vecs = jnp.concatenate([pool_scale, gmlp_ln_g, gmlp_ln_b, jnp.zeros((5, DP), F32)], axis=0)
    gf = jnp.concatenate([norm_final_g.reshape(1, D), jnp.zeros((7, D), F32)], axis=0)

    later =["w_mix_in", "w_mix_out", "ffn2_w_in", "ffn2_w_out"]
    (x1, g1s, u1s), got = _ffn_fwd(xs, mv1, win1, wout1.reshape(2, CH, D), side=_gather_side([place(n) for n in later]))
    wmi, wmo, win2, wout2 = got
    wmi = jnp.transpose(wmi, (1, 0, 2)).reshape(D, DPROJ)
    wmo = wmo.reshape(DP + DG, D)
    x2, pooled, zb = _mix_fwd(x1, mv2, wmi, wpool, vecs, wcat, bias, wmo)
    (dx3, g3s, u3s, loss_blk, dgf), _ = _ffn_fwd(x2, mv3, win2, wout2.reshape(2, CH, D), head=(target, gf))

    wo1, wo2 = wout1.reshape(2, CH, D), wout2.reshape(2, CH, D)
    (dx2, oin2, oout2, rin2, rout2, vec3), _ = _ffn_bwd(x2, dx3, g3s, u3s, mv3, win2, wo2)
    cs2 = [_chip_sum_pair(oin2, rin2), _chip_sum_pair(oout2, rout2)]
    (dx1, dwmi, dwmo, dwpool, dwsp, dbsp, v512, vec2), ex2 = _mix_bwd(
        x1, dx2, pooled, zb, mv2, wmi, wpool, vecs, wcat, wtcat, bias, wmo, side=_exchange_side(cs2))
    half2 = [_sum4(cs, e, chip) for cs, e in zip(cs2, ex2)]
    qcols = w_mix_in.shape[2]
    vmix = [jnp.transpose(dwmi.reshape(D, NQ, qcols), (1, 0, 2)).reshape(NQ, 2, D // 2, qcols),
            dwmo.reshape(NQ, 2, (DP + DG) // 8, D)]
    first1, got = _ffn_bwd_pass(0, xs, dx1, g1s, u1s, mv1, win1, wo1,
                                side=_join(_swap_side(vmix), _share_side(half2)))
    sibmix, other2 = got[:2], got[2:]
    cs_mix = [_chip_sum(g, r, core) for g, r in zip(vmix, sibmix)]
    (grad_x, oin1, oout1, rin1, rout1, vec1), ex_mix = _ffn_bwd_pass(
        1, xs, dx1, g1s, u1s, mv1, win1, wo1, prev=first1[:5], side=_exchange_side(cs_mix))
    vec1 = first1[5] + vec1
    cs_ffn1 = [_chip_sum_pair(oin1, rin1), _chip_sum_pair(oout1, rout1)]

    dmod =jnp.concatenate([vec1[0:3], vec2[0:3], vec3[0:3]], axis=0)
    grads = dict(
        b_ada=dmod.reshape(1, 9 * D), norm_ffn1_g=vec1[3:4], norm_mix_g=vec2[3:4], norm_ffn2_g=vec3[3:4],
        pool_scale=v512[0:1], gmlp_ln_g=v512[1:2], gmlp_ln_b=v512[2:3], b_spatial=dbsp[None],
        norm_final_g=dgf[0], w_pool=dwpool[None], w_spatial=dwsp[None])

    gp, layout = _pack({n: grads[n] for n in _SMALL})
    gp = jnp.concatenate([gp, loss_blk, loss_blk], axis=0)
    pad = jnp.zeros((16, LANE), F32)
    wp, mp, vp = [jnp.concatenate([_pack({n: src[n] for n in _SMALL})[0], pad], axis=0) for src in (W, M, V)]
    r0, nb, _ = layout["b_ada"]
    assert r0 == 0
    out_g, out_d, out_m, out_v = {}, {}, {}, {}
    gs, dl, mo, vo, gath = _small_allreduce_adamw(gp, wp, mp, vp, nb // LANE)
    loss = gs[-16, 0]
    for packed, dst in ((gs, out_g), (dl, out_d), (mo, out_m), (vo, out_v)):
        for n, a in _unpack(packed, layout).items():
            dst[n] = a.reshape(W[n].shape)

    def update(n, own, recv, after=()):
        (g2, d, mn, vn), _ = _adamw_halves(W[n][0], own, recv, M[n][0], V[n][0], after=after)
        out_g[n], out_d[n], out_m[n], out_v[n] = g2[None], d[None], mn[None], vn[None]
        return g2

    ssem, rsem, cs_fly, land_fly, token = _exchange_start(cs_ffn1, after=(gath,))
    dmod_q = lax.dynamic_slice_in_dim(gath.reshape(N_DEV, nb), q * ncol, ncol, axis=1) + token[0:8, 0:1]
    (ga, da, ma, va), _ = _ada_grad_adamw(cact_all.T, dmod_q, w_ada[0], m_w_ada[0], v_w_ada[0])
    out_g["w_ada"], out_d["w_ada"], out_m["w_ada"], out_v["w_ada"] = ga[None], da[None], ma[None], va[None]
    done = [ga, update("ffn2_w_in", half2[0], other2[0], after=(token,)),
            update("ffn2_w_out", half2[1], other2[1], after=(token,))]
    half_mix = [_sum4(cs, e, chip, after=(token,)) for cs, e in zip(cs_mix, ex_mix)]
    other_mix = _run_side(_share_side(half_mix), "share_mix")
    done += [update(n, own, recv) for n, own, recv in zip(["w_mix_in", "w_mix_out"], half_mix, other_mix)]
    cs_ffn1, ex_ffn1 = _exchange_wait(ssem, rsem, cs_fly, land_fly, after=done)
    half1 = [_sum4(cs, e, chip) for cs, e in zip(cs_ffn1, ex_ffn1)]
    other1 = _run_side(_share_side(half1), "share_ffn1")
    for n, own, recv in zip(["ffn1_w_in", "ffn1_w_out"], half1, other1):
        update(n, own, recv)

    return (loss, grad_x[None], *[out_g[n] for n in names], *[out_d[n] for n in names],
            *[out_m[n] for n in names], *[out_v[n] for n in names])
```

```python
import functools
import math

import jax
import jax.numpy as jnp
from jax import lax
from jax.experimental import pallas as pl
from jax.experimental.pallas import tpu as pltpu

F32 = jnp.float32
BF16 = jnp.bfloat16
MESH = pl.DeviceIdType.MESH
HIGHEST = lax.Precision.HIGHEST

EPS = 1e-6
D = 1024
DFF = 2816
CH = DFF // 2
NQ = 4
DP = 512
DG = 512
DPROJ = DP + 2 * DG
POOL_WINDOWS = (2, 4, 8, 16)
HALO = 16
CHUNK = 128
LANE = 128
N_DEV = 8

ADAM_LR = 0.001
ADAM_B1 = 0.9
ADAM_B2 = 0.999
ADAM_EPS = 1e-08
ADAM_WD = 0.01
ADAM_STEP = 10

VMEM_LIMIT = 62 * 1024 * 1024

TM_FFN_FWD = 512
TM_FFN_BWD = 512
TM_MIX = 512


def _call(body, **kw):
    return pl.pallas_call(body, interpret=False, **kw)


def _params(sem=None, vmem=None):
    return pltpu.CompilerParams(dimension_semantics=sem, vmem_limit_bytes=vmem)


def _sds(shape, dtype):
    return jax.ShapeDtypeStruct(shape, dtype)


ANY = pl.BlockSpec(memory_space=pl.ANY)
VMEM = pl.BlockSpec(memory_space=pltpu.VMEM)
SMEM = pl.BlockSpec(memory_space=pltpu.SMEM)


def _norm_mod(x, gn, sc, sh):
    r = lax.rsqrt(jnp.mean(x * x, axis=-1, keepdims=True) + EPS)
    xn = x * r
    hp = xn * gn
    return r, xn, hp, hp * (1.0 + sc) + sh


def _norm_mod_bwd(dh, r, xn, hp, gn, sc):
    one_sc = 1.0 + sc
    dsh = jnp.sum(dh, axis=0, keepdims=True)
    dsc = jnp.sum(dh * hp, axis=0, keepdims=True)
    dgn = jnp.sum(dh * one_sc * xn, axis=0, keepdims=True)
    dxn = dh * (gn * one_sc)
    dx = r * (dxn - xn * jnp.mean(dxn * xn, axis=-1, keepdims=True))
    return dsh, dsc, dgn, dx


def _dot(a, b):
    return jnp.dot(a, b, preferred_element_type=F32)


def _dot_nt(a, b):
    return lax.dot_general(a, b, (((1,), (1,)), ((), ())), preferred_element_type=F32)


def _dot_tn(a, b):
    return lax.dot_general(a, b, (((0,), (0,)), ((), ())), preferred_element_type=F32)


_GELU_C = math.sqrt(2.0 / math.pi)
_GELU_A = 0.044715


def _gelu_fwd_bwd(x):
    x2 = x * x
    t = jnp.tanh(_GELU_C * (x + _GELU_A * x * x2))
    g = 0.5 * x * (1.0 + t)
    dg = 0.5 * (1.0 + t) + 0.5 * x * (1.0 - t * t) * (_GELU_C * (1.0 + 3.0 * _GELU_A * x2))
    return g, dg


def _adamw(w, g, m, v):
    m = ADAM_B1 * m + (1.0 - ADAM_B1) * g
    v = ADAM_B2 * v + (1.0 - ADAM_B2) * (g * g)
    m_hat = m / (1.0 - ADAM_B1 ** ADAM_STEP)
    v_hat = v / (1.0 - ADAM_B2 ** ADAM_STEP)
    delta = -ADAM_LR * (m_hat / (jnp.sqrt(v_hat) + ADAM_EPS) + ADAM_WD * w)
    return delta, m, v


def _row_block(rows, cap=256, mult=16):
    best = None
    for t in range(mult, min(rows, cap) + 1, mult):
        if rows % t == 0:
            best = t
    assert best is not None, rows
    return best


def _head_math(x, target, gf):
    r = lax.rsqrt(jnp.mean(x * x, axis=-1, keepdims=True) + EPS)
    xn = x * r
    err = xn * gf - target
    dy = err * (1.0 / D)
    dxn = dy * gf
    dx = r * (dxn - xn * jnp.mean(dxn * xn, axis=-1, keepdims=True))
    return (0.5 / D) * jnp.sum(err * err), jnp.sum(dy * xn, axis=0, keepdims=True), dx


def _ffn_fwd(x, modv, win, wout, side=None, head=None):
    S = x.shape[0]
    tm = TM_FFN_FWD
    nt = S // tm

    def body(*refs):
        if head is None:
            x_ref, mod_ref, wgu_ref, wo_ref, xo_ref, gs_ref, us_ref, acc_scr = refs
        else:
            (x_ref, mod_ref, wgu_ref, wo_ref, t_ref, gf_ref,
             xo_ref, gs_ref, us_ref, loss_ref, dgf_ref, acc_scr) = refs

        @pl.when((pl.program_id(0) == 0) & (pl.program_id(1) == 0))
        def _():
            acc_scr[...] = jnp.zeros_like(acc_scr)
            if head is not None:
                loss_ref[...] = jnp.zeros_like(loss_ref)
                dgf_ref[...] = jnp.zeros_like(dgf_ref)

        j = pl.program_id(1)
        h = _norm_mod(x_ref[...], mod_ref[3:4, :], mod_ref[1:2, :], mod_ref[0:1, :])[3].astype(BF16)
        gu = _dot(h, wgu_ref[...])
        g = gu[:, :CH].astype(BF16)
        u = gu[:, CH:].astype(BF16)
        gs_ref[...] = g
        us_ref[...] = u
        gf = g.astype(F32)
        a = (gf * jax.nn.sigmoid(gf) * u.astype(F32)).astype(BF16)
        acc = jnp.where(j == 0, 0.0, acc_scr[...]) + _dot(a, wo_ref[...])
        acc_scr[...] = acc
        xo = x_ref[...] + (0.5 * mod_ref[2:3, :]) * acc
        if head is None:
            xo_ref[...] = xo
        else:
            @pl.when(j == 1)
            def _():
                loss, dgf, dx = _head_math(xo, t_ref[...], gf_ref[0:1, :])
                loss_ref[...] += loss
                dgf_ref[0:1, :] += dgf
                xo_ref[...] = dx

    step = lambda i, j: lambda: (pl.program_id(0) == i) & (pl.program_id(1) == j)
    tile = pl.BlockSpec((tm, D), lambda i, j: (i, 0))
    const = lambda shape: pl.BlockSpec(shape, lambda i, j: (0, 0))
    chunk = pl.BlockSpec((tm, CH), lambda i, j: (i, j))
    in_specs = [tile, const((8, D)), pl.BlockSpec((None, D, 2 * CH), lambda i, j: (j, 0, 0)),
                pl.BlockSpec((None, CH, D), lambda i, j: (j, 0, 0))]
    out_specs = [tile, chunk, chunk]
    out_shape = [_sds((S, D), F32), _sds((S, DFF), BF16), _sds((S, DFF), BF16)]
    args = (x, modv, win, wout)
    if head is not None:
        in_specs += [tile, const((8, D))]
        out_specs += [const((8, LANE)), const((8, D))]
        out_shape += [_sds((8, LANE), F32), _sds((8, D), F32)]
        args += tuple(head)
    return _side_call(
        body, side, (step(0, 0), step((7 * nt) // 10, 0), step(nt - 1, 1)), name="ffn_fwd",
        grid=(nt, 2), in_specs=in_specs, out_specs=out_specs, out_shape=out_shape,
        scratch_shapes=[pltpu.VMEM((tm, D), F32)],
        compiler_params=_params(("arbitrary", "arbitrary"), VMEM_LIMIT),
        args=args)


def _ffn_bwd_pass(jj, x, dxo, gs, us, modv, win, wout, prev=None, side=None):
    S = x.shape[0]
    tm = TM_FFN_BWD
    nt = S // tm
    hi, ho = D // 2, CH // 4
    last = prev is not None
    assert last == (jj == 1)

    def body(*refs):
        x_ref, dxo_ref, gs_ref, us_ref, mod_ref, win_hbm, wo_ref = refs[:7]
        k = 12 if last else 7
        out_ref, dwin_ref, dwout_ref, rwin_ref, rwout_ref, vec_ref = refs[k:k + 6]
        accgu, accw, wgu, sems, fsend, frecv = refs[k + 6:]
        i = pl.program_id(0)

        @pl.when(i == 0)
        def _():
            load = pltpu.make_async_copy(win_hbm.at[jj], wgu, sems.at[0])
            load.start()
            accgu[...] = jnp.zeros_like(accgu)
            accw[...] = jnp.zeros_like(accw)
            vec_ref[...] = jnp.zeros_like(vec_ref)
            load.wait()

        gn, sc, sh, gate = mod_ref[3:4, :], mod_ref[1:2, :], mod_ref[0:1, :], mod_ref[2:3, :]

        r, xn, hp, h = _norm_mod(x_ref[...], gn, sc, sh)
        dxo = dxo_ref[...]
        dy = (dxo * (0.5 * gate)).astype(BF16)
        g = gs_ref[...].astype(F32)
        u = us_ref[...].astype(F32)
        sig = jax.nn.sigmoid(g)
        sl = g * sig
        a = (sl * u).astype(BF16)
        da = _dot_nt(dy, wo_ref[...])
        dg = (da * u * (sig * (1.0 + g * (1.0 - sig)))).astype(BF16)
        du = (da * sl).astype(BF16)
        dgu = jnp.concatenate([dg, du], axis=1)
        dhp = _dot_nt(dgu, wgu[...])
        if last:
            dsh, dsc, dgn, dxin = _norm_mod_bwd(refs[7][...] + dhp, r, xn, hp, gn, sc)
            vec_ref[0:1, :] += dsh
            vec_ref[1:2, :] += dsc
            vec_ref[3:4, :] += dgn
            out_ref[...] = dxo + dxin
        else:
            out_ref[...] = dhp
        accw[...] += _dot_tn(a, dxo.astype(BF16))
        accgu[...] += _dot_tn(h.astype(BF16), dgu)

        @pl.when(i == nt - 1)
        def _():
            gw = accw[...]
            vec_ref[2:3, :] += 0.5 * jnp.sum(wo_ref[...].astype(F32) * gw, axis=0, keepdims=True)
            accw[...] = gw * (0.5 * gate)
            mx, my, cc = _me()
            part = lambda acc, base, n, c, col: acc.at[pl.ds(base + c * n, n), pl.ds(col[0], col[1])]
            pieces = [(accgu, 0, hi, (0, CH), dwin_ref, rwin_ref, jj), (accgu, 0, hi, (CH, CH), dwin_ref, rwin_ref, 2 + jj),
                      (accw, 0, ho, (0, D), dwout_ref, rwout_ref, 2 * jj),
                      (accw, 2 * ho, ho, (0, D), dwout_ref, rwout_ref, 2 * jj + 1)]
            loc = [pltpu.make_async_copy(part(acc, base, n, cc, col), own.at[slot], sems.at[p])
                   for p, (acc, base, n, col, own, _, slot) in enumerate(pieces)]
            rem = [pltpu.make_async_remote_copy(part(acc, base, n, 1 - cc, col), sib.at[slot], fsend.at[p], frecv.at[p],
                                                device_id=(mx, my, 1 - cc), device_id_type=MESH)
                   for p, (acc, base, n, col, _, sib, slot) in enumerate(pieces)]
            for cp in loc + rem:
                cp.start()
            for cp in loc:
                cp.wait()
            for cp in rem:
                cp.wait()

    once = pl.Buffered(1)
    tile = pl.BlockSpec((tm, D), lambda i: (i, 0))
    chunk = pl.BlockSpec((tm, CH), lambda i: (i, jj))
    in_specs = [tile, tile, chunk, chunk, pl.BlockSpec((8, D), lambda i: (0, 0)), ANY,
                pl.BlockSpec((None, CH, D), lambda i: (jj, 0, 0), pipeline_mode=once)]
    args = (x, dxo, gs, us, modv, win, wout)
    if last:
        in_specs += [tile, ANY, ANY, ANY, ANY]
        args += tuple(prev)
    step = lambda s: lambda: pl.program_id(0) == s
    return _side_call(
        body, side, (step(0), None, step(nt - 1)), name="ffn_bwd",
        grid=(nt,), in_specs=in_specs,
        out_specs=[tile, ANY, ANY, ANY, ANY, pl.BlockSpec((8, D), lambda i: (0, 0))],
        out_shape=[_sds((S, D), F32), _sds((NQ, hi, CH), F32), _sds((NQ, ho, D), F32), _sds((NQ, hi, CH), F32),
                   _sds((NQ, ho, D), F32), _sds((8, D), F32)],
        scratch_shapes=[pltpu.VMEM((D, 2 * CH), F32), pltpu.VMEM((CH, D), F32), pltpu.VMEM((D, 2 * CH), BF16),
                        pltpu.SemaphoreType.DMA((4,)), pltpu.SemaphoreType.DMA((4,)), pltpu.SemaphoreType.DMA((4,))],
        aliases={8 + p: 1 + p for p in range(4)} if last else {},
        compiler_params=_params(("arbitrary",), VMEM_LIMIT),
        args=args)


def _ffn_bwd(x, dxo, gs, us, modv, win, wout, side=None):
    first, extra = _ffn_bwd_pass(0, x, dxo, gs, us, modv, win, wout, side=side)
    (dx, dwin, dwout, rwin, rwout, vec), _ = _ffn_bwd_pass(1, x, dxo, gs, us, modv, win, wout, prev=first[:5])
    return (dx, dwin, dwout, rwin, rwout, first[5] + vec), extra


def _prep_spatial(w_spatial, b_spatial_t):
    def body(w_ref, b_ref, wcat_ref, wtcat_ref, bias_ref):
        row = lax.broadcasted_iota(jnp.int32, (CHUNK, CHUNK), 0)
        col = lax.broadcasted_iota(jnp.int32, (CHUNK, CHUNK), 1)
        tril = col <= row
        for p in range(4):
            wa = jnp.where(tril, w_ref[2 * p], 0.0)
            wb = jnp.where(tril, w_ref[2 * p + 1], 0.0)
            wcat_ref[p] = jnp.concatenate([wa, wb], axis=1).astype(BF16)
            wtcat_ref[p] = jnp.concatenate([wa.T, wb.T], axis=1).astype(BF16)
        head = lax.broadcasted_iota(jnp.int32, (8, DG), 0)
        ch = lax.broadcasted_iota(jnp.int32, (8, DG), 1)
        spread = jnp.where(ch // 64 == head, 1.0, 0.0).astype(F32)
        bias_ref[...] = jnp.dot(b_ref[...], spread, precision=HIGHEST, preferred_element_type=F32)

    return _call(
        body, name="prep_spatial",
        in_specs=[VMEM, VMEM], out_specs=[VMEM, VMEM, VMEM],
        out_shape=[_sds((4, CHUNK, 2 * CHUNK), BF16), _sds((4, CHUNK, 2 * CHUNK), BF16), _sds((CHUNK, DG), F32)],
    )(w_spatial, b_spatial_t)


def _pair_rhs(blocks):
    lane = lax.broadcasted_iota(jnp.int32, (CHUNK, LANE), 1)
    lo = lane < 64
    top = jnp.concatenate([jnp.where(lo, b, 0.0) for b in blocks], axis=1)
    bot = jnp.concatenate([jnp.where(lo, 0.0, b) for b in blocks], axis=1)
    return top, bot


def _gmlp_branch(zb, vecs, wcat_ref, bias_ref, nchunks):
    z, dz = _gelu_fwd_bwd(zb)
    u = z[:, :DG]
    v = z[:, DG:]
    ln_g, ln_b = vecs[1:2, :], vecs[2:3, :]
    mu = jnp.mean(v, axis=-1, keepdims=True)
    vc = v - mu
    rstd = lax.rsqrt(jnp.mean(vc * vc, axis=-1, keepdims=True) + EPS)
    vhat = vc * rstd
    vl = vhat * ln_g + ln_b
    sv_cols = []
    for p in range(4):
        blocks = [vl[k * CHUNK:(k + 1) * CHUNK, p * LANE:(p + 1) * LANE] for k in range(nchunks)]
        top, bot = _pair_rhs(blocks)
        rhs = jnp.concatenate([top, bot], axis=0).astype(BF16)
        out = _dot(wcat_ref[p], rhs)
        bias = bias_ref[:, p * LANE:(p + 1) * LANE]
        sv_cols.append(jnp.concatenate([out[:, k * LANE:(k + 1) * LANE] + bias for k in range(nchunks)], axis=0))
    sv = jnp.concatenate(sv_cols, axis=1)
    return dict(u=u, dz=dz, rstd=rstd, vhat=vhat, vl=vl, sv=sv, yb=u * sv)


def _mix_fwd(x, modv, win, wpool, vecs, wcat, bias, wout):
    S = x.shape[0]
    tm = TM_MIX
    nt = S // tm
    nchunks = tm // CHUNK

    def body(x_ref, mod_ref, win_ref, wpool_ref, vec_ref, wcat_ref, bias_ref, wout_ref,
             xo_ref, pooled_ref, zb_ref, ext):
        i = pl.program_id(0)

        @pl.when(i == 0)
        def _():
            ext[0:HALO, :] = jnp.zeros((HALO, DP), F32)

        x = x_ref[...]
        _, _, _, h = _norm_mod(x, mod_ref[3:4, :], mod_ref[1:2, :], mod_ref[0:1, :])
        proj = _dot(h.astype(BF16), win_ref[...])
        xa = proj[:, :DP]
        zb = proj[:, DP:]
        zb_ref[...] = zb
        ext[HALO:HALO + tm, :] = xa
        pos = i * tm + lax.broadcasted_iota(jnp.int32, (tm, 1), 0)
        vecs = vec_ref[...]
        ya_cols = []
        pooled_cols = []
        for gi, w in enumerate(POOL_WINDOWS):
            cols = slice(gi * LANE, (gi + 1) * LANE)
            s = xa[:, cols]
            for k in range(1, w):
                s = s + ext[HALO - k:HALO - k + tm, cols]
            cnt = jnp.minimum(pos + 1, w).astype(F32)
            pooled = (s / cnt - xa[:, cols]).astype(BF16)
            pooled_cols.append(pooled)
            ya_cols.append(_dot(pooled, wpool_ref[gi]) * vecs[0:1, cols])
        pooled_ref[...] = jnp.concatenate(pooled_cols, axis=1)
        ext[0:HALO, :] = ext[tm:tm + HALO, :]

        gm = _gmlp_branch(zb, vecs, wcat_ref, bias_ref, nchunks)
        cat = jnp.concatenate(ya_cols + [gm["yb"]], axis=1).astype(BF16)
        xo_ref[...] = x + mod_ref[2:3, :] * _dot(cat, wout_ref[...])

    full = lambda shape: pl.BlockSpec(shape, lambda i: (0,) * len(shape))
    return _call(
        body, name="mix_fwd",
        grid=(nt,),
        in_specs=[pl.BlockSpec((tm, D), lambda i: (i, 0)), full((8, D)), full((D, DPROJ)),
                  full((4, LANE, LANE)), full((8, DP)), full((4, CHUNK, 2 * CHUNK)), full((CHUNK, DG)),
                  full((DP + DG, D))],
        out_specs=[pl.BlockSpec((tm, D), lambda i: (i, 0)), pl.BlockSpec((tm, DP), lambda i: (i, 0)),
                   pl.BlockSpec((tm, 2 * DG), lambda i: (i, 0))],
        out_shape=[_sds((S, D), F32), _sds((S, DP), BF16), _sds((S, 2 * DG), F32)],
        scratch_shapes=[pltpu.VMEM((tm + HALO, DP), F32)],
        compiler_params=_params(("arbitrary",), VMEM_LIMIT),
    )(x, modv, win, wpool, vecs, wcat, bias, wout)


def _mix_bwd(x, dxo, pooled, zb, modv, win, wpool, vecs, wcat, wtcat, bias, wout, side=None):
    S = x.shape[0]
    tm = TM_MIX
    nt = S // tm
    nchunks = tm // CHUNK

    def body(x_ref, dxo_ref, pooled_ref, zb_ref, mod_ref, win_ref, wpool_ref, vec_ref, wcat_ref, wtcat_ref,
             bias_ref, wout_ref,
             dx_ref, dwin_ref, dwout_ref, dwpool_ref, dwsp_ref, dbsp_ref, v512_ref, vd_ref, qext, dsv_acc):
        step = pl.program_id(0)
        tile = nt - 1 - step

        @pl.when(step == 0)
        def _():
            dwin_ref[...] = jnp.zeros_like(dwin_ref)
            dwout_ref[...] = jnp.zeros_like(dwout_ref)
            dwpool_ref[...] = jnp.zeros_like(dwpool_ref)
            dwsp_ref[...] = jnp.zeros_like(dwsp_ref)
            v512_ref[...] = jnp.zeros_like(v512_ref)
            vd_ref[...] = jnp.zeros_like(vd_ref)
            dsv_acc[...] = jnp.zeros_like(dsv_acc)
            qext[tm:tm + HALO, :] = jnp.zeros((HALO, DP), F32)

        gn, sc, sh, gate = mod_ref[3:4, :], mod_ref[1:2, :], mod_ref[0:1, :], mod_ref[2:3, :]
        vecs = vec_ref[...]
        x = x_ref[...]
        r, xn, hp, h = _norm_mod(x, gn, sc, sh)
        hb = h.astype(BF16)
        dxo = dxo_ref[...]

        pooled = pooled_ref[...]
        mixed_cols = [_dot(pooled[:, gi * LANE:(gi + 1) * LANE], wpool_ref[gi]) for gi in range(4)]
        mixed = jnp.concatenate(mixed_cols, axis=1)
        scale = vecs[0:1, :]
        gm = _gmlp_branch(zb_ref[...], vecs, wcat_ref, bias_ref, nchunks)
        cat = jnp.concatenate([mixed * scale, gm["yb"]], axis=1).astype(BF16)

        dwout_ref[...] += _dot_tn(cat, dxo.astype(BF16))
        dcat = _dot_nt((dxo * gate).astype(BF16), wout_ref[...])
        dya = dcat[:, :DP]
        dyb = dcat[:, DP:]

        v512_ref[0:1, :] += jnp.sum(dya * mixed, axis=0, keepdims=True)
        dmixed = (dya * scale).astype(BF16)
        pos = tile * tm + lax.broadcasted_iota(jnp.int32, (tm, 1), 0)
        dpooled_cols = []
        for gi, w in enumerate(POOL_WINDOWS):
            cols = slice(gi * LANE, (gi + 1) * LANE)
            dp = _dot_nt(dmixed[:, cols], wpool_ref[gi])
            dwpool_ref[gi] += _dot_tn(pooled[:, cols], dmixed[:, cols])
            cnt = jnp.minimum(pos + 1, w).astype(F32)
            qext[0:tm, cols] = dp / cnt
            dpooled_cols.append(dp)
        dxa_cols = []
        for gi, w in enumerate(POOL_WINDOWS):
            cols = slice(gi * LANE, (gi + 1) * LANE)
            s = qext[0:tm, cols]
            for k in range(1, w):
                s = s + qext[k:k + tm, cols]
            dxa_cols.append(s - dpooled_cols[gi])
        qext[tm:tm + HALO, :] = qext[0:HALO, :]

        u, sv, vl = gm["u"], gm["sv"], gm["vl"]
        du = dyb * sv
        dsv = dyb * u
        dvl_cols = []
        for p in range(4):
            cols = slice(p * LANE, (p + 1) * LANE)
            dblocks = [dsv[k * CHUNK:(k + 1) * CHUNK, cols] for k in range(nchunks)]
            vblocks = [vl[k * CHUNK:(k + 1) * CHUNK, cols] for k in range(nchunks)]
            tot = dblocks[0]
            for b in dblocks[1:]:
                tot = tot + b
            dsv_acc[:, cols] += tot
            top, bot = _pair_rhs(dblocks)
            out = _dot(wtcat_ref[p], jnp.concatenate([top, bot], axis=0).astype(BF16))
            dvl_cols.append(jnp.concatenate([out[:, k * LANE:(k + 1) * LANE] for k in range(nchunks)], axis=0))
            vcat = jnp.concatenate(vblocks, axis=1).astype(BF16)
            dwsp_ref[2 * p] += _dot_nt(top.astype(BF16), vcat)
            dwsp_ref[2 * p + 1] += _dot_nt(bot.astype(BF16), vcat)
        dvl = jnp.concatenate(dvl_cols, axis=1)
        vhat, rstd = gm["vhat"], gm["rstd"]
        v512_ref[1:2, :] += jnp.sum(dvl * vhat, axis=0, keepdims=True)
        v512_ref[2:3, :] += jnp.sum(dvl, axis=0, keepdims=True)
        dvh = dvl * vecs[1:2, :]
        dv = rstd * (dvh - jnp.mean(dvh, axis=-1, keepdims=True)
                     - vhat * jnp.mean(dvh * vhat, axis=-1, keepdims=True))
        dzb = jnp.concatenate([du, dv], axis=1) * gm["dz"]

        dproj = jnp.concatenate(dxa_cols + [dzb], axis=1).astype(BF16)
        dwin_ref[...] += _dot_tn(hb, dproj)
        dh = _dot_nt(dproj, win_ref[...])
        dsh, dsc, dgn, dxin = _norm_mod_bwd(dh, r, xn, hp, gn, sc)
        vd_ref[0:1, :] += dsh
        vd_ref[1:2, :] += dsc
        vd_ref[3:4, :] += dgn
        dx_ref[...] = dxo + dxin

        @pl.when(step == nt - 1)
        def _():
            gw = dwout_ref[...]
            vd_ref[2:3, :] += jnp.sum(wout_ref[...].astype(F32) * gw, axis=0, keepdims=True)
            dwout_ref[...] = gw * gate
            row = lax.broadcasted_iota(jnp.int32, (CHUNK, CHUNK), 0)
            col = lax.broadcasted_iota(jnp.int32, (CHUNK, CHUNK), 1)
            for hh in range(8):
                dwsp_ref[hh] = jnp.where(col <= row, dwsp_ref[hh], 0.0)
            head = lax.broadcasted_iota(jnp.int32, (8, DG), 0)
            ch = lax.broadcasted_iota(jnp.int32, (8, DG), 1)
            spread = jnp.where(ch // 64 == head, 1.0, 0.0).astype(F32)
            dbsp_ref[...] = lax.dot_general(spread, dsv_acc[...], (((1,), (1,)), ((), ())),
                                            precision=HIGHEST, preferred_element_type=F32)

    full = lambda shape: pl.BlockSpec(shape, lambda s: (0,) * len(shape))
    rev = lambda cols: pl.BlockSpec((tm, cols), lambda s: (nt - 1 - s, 0))
    step = lambda s: lambda: pl.program_id(0) == s
    return _side_call(
        body, side, (step(0), None, step(nt - 1)), name="mix_bwd",
        grid=(nt,),
        in_specs=[rev(D), rev(D), rev(DP), rev(2 * DG), full((8, D)), full((D, DPROJ)), full((4, LANE, LANE)),
                  full((8, DP)), full((4, CHUNK, 2 * CHUNK)), full((4, CHUNK, 2 * CHUNK)), full((CHUNK, DG)),
                  full((DP + DG, D))],
        out_specs=[rev(D), full((D, DPROJ)), full((DP + DG, D)), full((4, LANE, LANE)), full((8, CHUNK, CHUNK)),
                   full((8, CHUNK)), full((8, DP)), full((8, D))],
        out_shape=[_sds((S, D), F32), _sds((D, DPROJ), F32), _sds((DP + DG, D), F32), _sds((4, LANE, LANE), F32),
                   _sds((8, CHUNK, CHUNK), F32), _sds((8, CHUNK), F32), _sds((8, DP), F32), _sds((8, D), F32)],
        scratch_shapes=[pltpu.VMEM((tm + HALO, DP), F32), pltpu.VMEM((CHUNK, DG), F32)],
        compiler_params=_params(("arbitrary",), VMEM_LIMIT),
        args=(x, dxo, pooled, zb, modv, win, wpool, vecs, wcat, wtcat, bias, wout))


def _chip_sum(g, rbuf, core):
    _, _, hr, cols = g.shape
    tr = _row_block(hr)

    def body(c_ref, g_ref, r_ref, o_ref):
        o_ref[...] = (g_ref[...] + r_ref[...]).astype(BF16)

    return pl.pallas_call(
        body, name="chip_sum", interpret=False,
        grid_spec=pltpu.PrefetchScalarGridSpec(
            num_scalar_prefetch=1, grid=(NQ, hr // tr),
            in_specs=[pl.BlockSpec((None, None, tr, cols), lambda q, i, c: (q, c[0], i, 0)),
                      pl.BlockSpec((None, tr, cols), lambda q, i, c: (q, i, 0))],
            out_specs=pl.BlockSpec((None, tr, cols), lambda q, i, c: (q, i, 0))),
        out_shape=_sds((NQ, hr, cols), BF16),
        compiler_params=_params(("arbitrary", "arbitrary"), None),
    )(core, g, rbuf)


def _chip_sum_pair(own, rbuf, slots=(0, 1, 2, 3), prev=None):
    _, hr, cols = own.shape
    tr = _row_block(hr)
    a, b = slots[0], (slots[1] - slots[0] if len(slots) > 1 else 0)
    assert list(slots) == [a + b * k for k in range(len(slots))]

    def body(a_ref, b_ref, *rest):
        rest[-1][...] = (a_ref[...] + b_ref[...]).astype(BF16)

    spec = pl.BlockSpec((None, tr, cols), lambda k, i: (a + b * k, i, 0))
    return _call(
        body, name="chip_sum_pair",
        grid=(len(slots), hr // tr),
        in_specs=[spec, spec] + ([ANY] if prev is not None else []), out_specs=spec,
        out_shape=_sds((NQ, hr, cols), BF16),
        input_output_aliases={2: 0} if prev is not None else {},
        compiler_params=_params(("arbitrary", "arbitrary"), None),
    )(own, rbuf, *([prev] if prev is not None else []))


def _sum4(cs, rbuf, chip, after=()):
    _, hr, cols = rbuf.shape
    tr = _row_block(hr)

    def body(q_ref, c_ref, r1_ref, r2_ref, r3_ref, *rest):
        acc = c_ref[...].astype(F32)
        for r in (r1_ref, r2_ref, r3_ref):
            acc = acc + r[...].astype(F32)
        rest[-1][...] = acc

    slot = lambda k: pl.BlockSpec((None, tr, cols), lambda i, q: ((q[0] + k) % NQ, i, 0))
    return pl.pallas_call(
        body, name="sum4", interpret=False,
        grid_spec=pltpu.PrefetchScalarGridSpec(
            num_scalar_prefetch=1, grid=(hr // tr,),
            in_specs=[slot(0), slot(1), slot(2), slot(3)] + [ANY] * len(after),
            out_specs=pl.BlockSpec((tr, cols), lambda i, q: (i, 0))),
        out_shape=_sds((hr, cols), F32),
        compiler_params=_params(("arbitrary",), None),
    )(chip, cs, rbuf, rbuf, rbuf, *after)


def _adamw_halves(w, own, recv, m, v, side=None, after=()):
    rows, cols = w.shape
    hr = rows // 2
    tr = _row_block(hr, mult=8)
    nb = hr // tr

    def body(w_ref, own_ref, recv_ref, m_ref, v_ref, *rest):
        g_ref, d_ref, mo_ref, vo_ref = rest[len(after):]
        g = jnp.where(pl.program_id(0) == lax.axis_index("c"), own_ref[...], recv_ref[...])
        d, mn, vn = _adamw(w_ref[...], g, m_ref[...], v_ref[...])
        g_ref[...] = g
        d_ref[...] = d
        mo_ref[...] = mn
        vo_ref[...] = vn

    full = pl.BlockSpec((tr, cols), lambda h, i: (h * nb + i, 0))
    half = pl.BlockSpec((tr, cols), lambda h, i: (i, 0))
    step = lambda h, i: lambda: (pl.program_id(0) == h) & (pl.program_id(1) == i)
    return _side_call(
        body, side, (step(0, 0), None, step(1, nb - 1)), name="adamw_halves",
        grid=(2, nb), in_specs=[full, half, half, full, full] + [ANY] * len(after), out_specs=[full] * 4,
        out_shape=[_sds((rows, cols), F32)] * 4, scratch_shapes=[],
        compiler_params=_params(("arbitrary", "arbitrary"), None),
        args=(w, own, recv, m, v, *after))


def _cast_place(w, chip, wide=False):
    rows, cols = w.shape
    tr = _row_block(rows)

    def body(q_ref, w_ref, o_ref):
        o_ref[...] = w_ref[...].astype(BF16)

    if wide:
        out_spec = pl.BlockSpec((None, tr, cols), lambda i, q: (q[0] % 2, i, q[0] // 2))
    else:
        out_spec = pl.BlockSpec((None, tr, cols), lambda i, q: (q[0], i, 0))
    return pl.pallas_call(
        body, name="cast_place", interpret=False,
        grid_spec=pltpu.PrefetchScalarGridSpec(
            num_scalar_prefetch=1, grid=(rows // tr,),
            in_specs=[pl.BlockSpec((tr, cols), lambda i, q: (i, 0))],
            out_specs=out_spec),
        out_shape=_sds((2, rows, 2 * cols) if wide else (NQ, rows, cols), BF16),
        compiler_params=_params(("arbitrary",), None),
    )(chip, w)


def _ada_grad_adamw(cact_t, dmod_q, w, m, v, side=None):
    rows, cols = w.shape
    tc = 256
    assert cols % tc == 0

    def body(c_ref, d_ref, w_ref, m_ref, v_ref, g_ref, dl_ref, mo_ref, vo_ref):
        g = jnp.dot(c_ref[...], d_ref[...], precision=HIGHEST, preferred_element_type=F32)
        d, mn, vn = _adamw(w_ref[...], g, m_ref[...], v_ref[...])
        g_ref[...] = g
        dl_ref[...] = d
        mo_ref[...] = mn
        vo_ref[...] = vn

    spec = pl.BlockSpec((rows, tc), lambda i: (0, i))
    step = lambda s: lambda: pl.program_id(0) == s
    return _side_call(
        body, side, (step(0), None, step(cols // tc - 1)), name="ada_grad_adamw",
        grid=(cols // tc,),
        in_specs=[pl.BlockSpec((rows, 8), lambda i: (0, 0)), pl.BlockSpec((8, tc), lambda i: (0, i)),
                  spec, spec, spec],
        out_specs=[spec] * 4,
        out_shape=[_sds((rows, cols), F32)] * 4,
        scratch_shapes=[],
        compiler_params=_params(("arbitrary",), None),
        args=(cact_t, dmod_q, w, m, v))


def _me():
    x, y, c = lax.axis_index("x"), lax.axis_index("y"), lax.axis_index("c")
    return x, y, c


_OFFSETS7 = [(dx, dy, dc) for dx in (0, 1) for dy in (0, 1) for dc in (0, 1) if (dx, dy, dc) != (0, 0, 0)]
_CHIP_OFFSETS = [(1, 0), (0, 1), (1, 1)]


def _ada_fwd(c, w_ada_q, b_ada_q, side=None):
    ncol = w_ada_q.shape[1]

    def body(c_ref, w_ref, b_ref, cact_ref, modsel_ref, blk, gath, res, parts, send_sems, recv_sems, side_start=None):
        x, y, cc = _me()
        me = 4 * x + 2 * y + cc
        q = 2 * x + y
        cv = c_ref[...]
        ca = cv * jax.nn.sigmoid(cv)
        row = lax.broadcasted_iota(jnp.int32, (8, D), 0)
        blk[...] = jnp.where(row == me, jnp.broadcast_to(ca, (8, D)), 0.0)
        gath[me] = blk[...]
        sends = []
        for k, (dx, dy, dc) in enumerate(_OFFSETS7):
            cp = pltpu.make_async_remote_copy(blk, gath.at[me], send_sems.at[k], recv_sems.at[k],
                                              device_id=(x ^ dx, y ^ dy, cc ^ dc), device_id_type=MESH)
            cp.start()
            sends.append(cp)
        if side_start is not None:
            side_start()
        for cp in sends:
            cp.wait_recv()
        cact = gath[0]
        for d in range(1, N_DEV):
            cact = cact + gath[d]
        cact_ref[...] = cact
        res[...] = jnp.dot(cact, w_ref[...], precision=HIGHEST, preferred_element_type=F32) + b_ref[...]
        parts[q] = res[...]
        sends2 = []
        for k, (dx, dy) in enumerate(_CHIP_OFFSETS):
            cp = pltpu.make_async_remote_copy(res, parts.at[q], send_sems.at[7 + k], recv_sems.at[7 + k],
                                              device_id=(x ^ dx, y ^ dy, cc), device_id_type=MESH)
            cp.start()
            sends2.append(cp)
        for cp in sends2:
            cp.wait_recv()
        row2 = lax.broadcasted_iota(jnp.int32, (8, ncol), 0)
        out = jnp.zeros((8, ncol), F32)
        for s in range(NQ):
            mine = jnp.sum(jnp.where(row2 == me, parts[s], 0.0), axis=0, keepdims=True)
            out = out + jnp.where(row2 == s, jnp.broadcast_to(mine, (8, ncol)), 0.0)
        modsel_ref[...] = out
        for cp in sends + sends2:
            cp.wait_send()

    return _side_call(
        body, side, None, name="ada_fwd",
        in_specs=[VMEM, VMEM, VMEM], out_specs=[VMEM, VMEM],
        out_shape=[_sds((8, D), F32), _sds((8, ncol), F32)],
        scratch_shapes=[pltpu.VMEM((8, D), F32), pltpu.VMEM((N_DEV, 8, D), F32), pltpu.VMEM((8, ncol), F32),
                        pltpu.VMEM((NQ, 8, ncol), F32), pltpu.SemaphoreType.DMA((10,)), pltpu.SemaphoreType.DMA((10,))],
        compiler_params=_params(None, VMEM_LIMIT), start_in_body=side is not None,
        args=(c, w_ada_q, b_ada_q))


class _Side:
    def __init__(self, ins, out_shapes, aliases, nsem, start, mid=None, finish=None):
        self.ins, self.out_shapes, self.aliases, self.nsem = list(ins), list(out_shapes), dict(aliases), nsem
        self.start, self.mid, self.finish = start, mid, finish


def _join(*sides):
    ins, outs, aliases, offs, nsem = [], [], {}, [], 0
    for s in sides:
        offs.append((len(ins), len(outs), nsem))
        aliases.update({len(ins) + a: len(outs) + b for a, b in s.aliases.items()})
        ins += s.ins
        outs += s.out_shapes
        nsem += s.nsem

    def hook(name):
        def run(i, o, ss, rs, base):
            for s, (io, oo, so) in zip(sides, offs):
                fn = getattr(s, name)
                if fn is not None:
                    fn(i[io:io + len(s.ins)], o[oo:oo + len(s.out_shapes)], ss, rs, base + so)
        return run

    return _Side(ins, outs, aliases, nsem, hook("start"), hook("mid"), hook("finish"))


def _side_call(body, side, when, *, name, in_specs, out_specs, out_shape, scratch_shapes, args, aliases=None,
               start_in_body=False, **kw):
    n_in, n_out = len(in_specs), len(out_specs)
    aliases = dict(aliases or {})
    if side is None:
        return _call(body, name=name, in_specs=in_specs, out_specs=out_specs, out_shape=out_shape,
                     scratch_shapes=scratch_shapes, input_output_aliases=aliases, **kw)(*args), []
    ns_in, ns_out = len(side.ins), len(side.out_shapes)

    def hook(fn, k, operands):
        if fn is None:
            return
        if when is None:
            fn(*operands, 0)
        elif when[k] is not None:
            pl.when(when[k]())(functools.partial(fn, *operands, 0))

    def wrapped(*refs):
        ins, s_ins = refs[:n_in], refs[n_in:n_in + ns_in]
        o0 = n_in + ns_in
        outs, s_outs = refs[o0:o0 + n_out], refs[o0 + n_out:o0 + n_out + ns_out]
        rest = refs[o0 + n_out + ns_out:]
        scratch, operands = rest[:-2], (s_ins, s_outs, rest[-2], rest[-1])
        if start_in_body:
            body(*ins, *outs, *scratch, side_start=functools.partial(hook, side.start, 0, operands))
        else:
            hook(side.start, 0, operands)
            body(*ins, *outs, *scratch)
        hook(side.mid, 1, operands)
        hook(side.finish, 2, operands)

    res = _call(
        wrapped, name=name,
        in_specs=list(in_specs) + [ANY] * ns_in, out_specs=list(out_specs) + [ANY] * ns_out,
        out_shape=list(out_shape) + side.out_shapes,
        scratch_shapes=list(scratch_shapes) + [pltpu.SemaphoreType.DMA((side.nsem,)),
                                               pltpu.SemaphoreType.DMA((side.nsem,))],
        input_output_aliases={**aliases, **{n_in + a: n_out + b for a, b in side.aliases.items()}},
        **kw)(*args, *side.ins)
    return res[:n_out], res[n_out:]


def _run_side(side, name):
    return _side_call(lambda: None, side, None, name=name, in_specs=[], out_specs=[], out_shape=[],
                      scratch_shapes=[], args=[])[1]


def _remote(src, dst, ss, rs, k, dev):
    return pltpu.make_async_remote_copy(src, dst, ss.at[k], rs.at[k], device_id=dev, device_id_type=MESH)


def _gather_side(bufs):
    n = len(bufs)

    def plan(outs, w):
        x, y, cc = _me()
        hr = outs[w].shape[1] // 2
        mine, other = cc * hr, (1 - cc) * hr
        qx, qy, qd, q = 2 * (x ^ 1) + y, 2 * x + (y ^ 1), 2 * (x ^ 1) + (y ^ 1), 2 * x + y
        xn, yn, sib = (x ^ 1, y, cc), (x, y ^ 1, cc), (x, y, 1 - cc)
        if outs[w].shape[0] == NQ:
            at = lambda slot, r0, nr: outs[w].at[slot, pl.ds(r0, nr)]
        else:
            cols = outs[w].shape[2] // 2
            at = lambda slot, r0, nr: outs[w].at[slot % 2, pl.ds(r0, nr), pl.ds((slot // 2) * cols, cols)]
        send = [(at(q, mine, hr), xn), (at(q, mine, hr), yn),
                (at(qx, mine, hr // 2), yn), (at(qy, mine + hr // 2, hr // 2), xn),
                (at(qx, mine, hr), sib), (at(qy, mine, hr), sib), (at(qd, mine, hr), sib)]
        recv = [at(qx, mine, hr), at(qy, mine, hr), at(qd, mine, hr // 2), at(qd, mine + hr // 2, hr // 2),
                at(qx, other, hr), at(qy, other, hr), at(qd, other, hr)]
        return send, recv

    def op(outs, ss, rs, b, w, k, what):
        send, recv = plan(outs, w)
        if what == "wait_recv":
            _remote(recv[k], recv[k], ss, rs, b + 7 * w + k, send[k][1]).wait_recv()
        else:
            getattr(_remote(send[k][0], send[k][0], ss, rs, b + 7 * w + k, send[k][1]), what)()

    def start(ins, outs, ss, rs, b):
        for w in range(n):
            for k in (0, 1):
                op(outs, ss, rs, b, w, k, "start")

    def mid(ins, outs, ss, rs, b):
        for w in range(n):
            for k in (0, 1):
                op(outs, ss, rs, b, w, k, "wait_recv")
                op(outs, ss, rs, b, w, 2 + k, "start")
                op(outs, ss, rs, b, w, 4 + k, "start")

    def finish(ins, outs, ss, rs, b):
        for w in range(n):
            for k in (2, 3):
                op(outs, ss, rs, b, w, k, "wait_recv")
            op(outs, ss, rs, b, w, 6, "start")
        for w in range(n):
            for k in (4, 5, 6):
                op(outs, ss, rs, b, w, k, "wait_recv")
            for k in range(7):
                op(outs, ss, rs, b, w, k, "wait_send")

    return _Side(bufs, [_sds(tuple(w.shape), w.dtype) for w in bufs], {i: i for i in range(n)}, 7 * n,
                 start, mid, finish)


def _copies_side(ins, out_shapes, nsem, copies):
    def start(*a):
        for cp in copies(*a):
            cp.start()

    def finish(*a):
        for cp in copies(*a):
            cp.wait()

    return _Side(ins, out_shapes, {}, nsem, start, None, finish)


def _swap_side(gs):
    def copies(ins, outs, ss, rs, b):
        x, y, cc = _me()
        return [_remote(ins[w].at[:, 1 - cc], outs[w], ss, rs, b + w, (x, y, 1 - cc)) for w in range(len(gs))]

    return _copies_side(gs, [_sds((NQ,) + tuple(g.shape[2:]), F32) for g in gs], len(gs), copies)


def _exchange_side(cs, slots=None, prev=None):
    n = len(cs)
    slots = slots or [(0, 1, 2, 3)] * n

    def among(chip, allowed):
        hit = chip == allowed[0]
        for s in allowed[1:]:
            hit = hit | (chip == s)
        return hit

    def each(ins, outs, ss, rs, b, do_send, do_recv):
        x, y, cc = _me()
        q = 2 * x + y
        for w in range(n):
            for j, (dx, dy) in enumerate(_CHIP_OFFSETS):
                pq = 2 * (x ^ dx) + (y ^ dy)
                cp = _remote(ins[w].at[pq], outs[w].at[q], ss, rs, b + 3 * w + j, (x ^ dx, y ^ dy, cc))
                if do_send is not None:
                    pl.when(among(pq, slots[w]))(functools.partial(do_send, cp))
                if do_recv is not None:
                    pl.when(among(q, slots[w]))(functools.partial(do_recv, cp))

    def start(ins, outs, ss, rs, b):
        each(ins, outs, ss, rs, b, lambda cp: cp.start(), None)

    def finish(ins, outs, ss, rs, b):
        each(ins, outs, ss, rs, b, lambda cp: cp.wait_send(), lambda cp: cp.wait_recv())

    ins = list(cs) + (list(prev) if prev is not None else [])
    aliases = {n + w: w for w in range(n)} if prev is not None else {}
    return _Side(ins, [_sds(tuple(c.shape), c.dtype) for c in cs], aliases, 3 * n, start, None, finish)


def _exchange_copies(srcs, lands, send_sems, recv_sems):
    x, y, cc = _me()
    return [pltpu.make_async_remote_copy(srcs[w].at[2 * (x ^ dx) + (y ^ dy)], lands[w].at[2 * x + y],
                                         send_sems.at[3 * w + j], recv_sems.at[3 * w + j],
                                         device_id=(x ^ dx, y ^ dy, cc), device_id_type=MESH)
            for w in range(len(srcs)) for j, (dx, dy) in enumerate(_CHIP_OFFSETS)]


def _exchange_start(cs, after=()):
    n = len(cs)
    hbm, sem = pl.BlockSpec(memory_space=pltpu.HBM), pl.BlockSpec(memory_space=pltpu.SEMAPHORE)
    srcs = [pltpu.with_memory_space_constraint(c, pltpu.HBM) for c in cs]
    lands = [pltpu.with_memory_space_constraint(lax.empty(c.shape, c.dtype), pltpu.HBM) for c in cs]

    def body(*refs):
        sems = 2 * n + len(after)
        for cp in _exchange_copies(refs[:n], refs[n:2 * n], refs[sems], refs[sems + 1]):
            cp.start()
        refs[-1][...] = jnp.zeros_like(refs[-1])

    res = pl.pallas_call(
        body, name="exchange_start", interpret=False,
        out_shape=(pltpu.SemaphoreType.DMA((3 * n,)), pltpu.SemaphoreType.DMA((3 * n,)),
                   *[pltpu.HBM(c.shape, c.dtype) for c in cs], *[pltpu.HBM(c.shape, c.dtype) for c in cs],
                   _sds((8, LANE), F32)),
        in_specs=(hbm,) * (2 * n) + (ANY,) * len(after), out_specs=(sem, sem) + (hbm,) * (2 * n) + (VMEM,),
        input_output_aliases={i: 2 + i for i in range(2 * n)},
        compiler_params=pltpu.CompilerParams(has_side_effects=pltpu.SideEffectType.DATAFLOW_SIDE_EFFECTING),
    )(*srcs, *lands, *after)
    return res[0], res[1], list(res[2:2 + n]), list(res[2 + n:2 + 2 * n]), res[-1]


def _exchange_wait(send_sems, recv_sems, srcs, lands, after):
    n = len(srcs)
    hbm, sem = pl.BlockSpec(memory_space=pltpu.HBM), pl.BlockSpec(memory_space=pltpu.SEMAPHORE)

    def body(*refs):
        for cp in _exchange_copies(refs[:n], refs[n:2 * n], refs[2 * n], refs[2 * n + 1]):
            cp.wait_send()
            cp.wait_recv()

    res = pl.pallas_call(
        body, name="exchange_wait", interpret=False,
        out_shape=[pltpu.HBM(c.shape, c.dtype) for c in srcs + lands],
        in_specs=(hbm,) * (2 * n) + (sem, sem) + (ANY,) * len(after), out_specs=(hbm,) * (2 * n),
        input_output_aliases={i: i for i in range(2 * n)},
        compiler_params=pltpu.CompilerParams(has_side_effects=pltpu.SideEffectType.DATAFLOW_SIDE_EFFECTING),
    )(*srcs, *lands, send_sems, recv_sems, *after)
    return list(res[:n]), list(res[n:])


def _share_side(fs):
    def copies(ins, outs, ss, rs, b):
        x, y, cc = _me()
        return [_remote(ins[w], outs[w], ss, rs, b + w, (x, y, 1 - cc)) for w in range(len(fs))]

    return _copies_side(fs, [_sds(tuple(f.shape), F32) for f in fs], len(fs), copies)


def _small_allreduce_adamw(g, w, m, v, nd):
    rows = g.shape[0]
    nr = rows - nd
    hr = nr // 2
    assert nd % 8 == 0 and hr % 8 == 0

    def body(g_ref, w_ref, m_ref, v_ref, gs_ref, d_ref, mo_ref, vo_ref, gath, sib, csum, slots, tot, ss, rs):
        x, y, cc = _me()
        me = 4 * x + 2 * y + cc
        q = 2 * x + y
        sibling = (x, y, 1 - cc)
        dm = g_ref.at[pl.ds(0, nd)]
        gath[me] = g_ref[0:nd, :]
        to_all = [_remote(dm, gath.at[me], ss, rs, k, (x ^ dx, y ^ dy, cc ^ dc)) for k, (dx, dy, dc) in enumerate(_OFFSETS7)]
        to_sib = _remote(g_ref.at[pl.ds(nd, nr)], sib, ss, rs, 7, sibling)
        for cp in to_all + [to_sib]:
            cp.start()
        to_sib.wait_recv()
        csum[...] = g_ref[nd:, :] + sib[...]
        mine = pl.ds(pl.multiple_of(cc * hr, 8), hr)
        slots[q] = csum[mine, :]
        to_chips = [_remote(csum.at[mine], slots.at[q], ss, rs, 8 + j, (x ^ dx, y ^ dy, cc))
                    for j, (dx, dy) in enumerate(_CHIP_OFFSETS)]
        for cp in to_chips:
            cp.start()
        for cp in to_chips:
            cp.wait_recv()
        tot[mine, :] = (slots[0] + slots[1]) + (slots[2] + slots[3])
        halves = _remote(tot.at[mine], tot.at[mine], ss, rs, 11, sibling)
        halves.start()
        for cp in to_all:
            cp.wait_recv()
        dsum = gath[0]
        for dev in range(1, N_DEV):
            dsum = dsum + gath[dev]
        halves.wait_recv()
        for lo, n, total in ((0, nd, dsum), (nd, nr, tot[...])):
            gs_ref[lo:lo + n, :] = total
            d, mn, vn = _adamw(w_ref[lo:lo + n, :], total, m_ref[lo:lo + n, :], v_ref[lo:lo + n, :])
            d_ref[lo:lo + n, :] = d
            mo_ref[lo:lo + n, :] = mn
            vo_ref[lo:lo + n, :] = vn
        for cp in to_all + [to_sib, halves] + to_chips:
            cp.wait_send()

    return _call(
        body, name="small_allreduce_adamw",
        in_specs=[VMEM] * 4, out_specs=[VMEM] * 5,
        out_shape=[_sds((rows, LANE), F32)] * 4 + [_sds((N_DEV, nd, LANE), F32)],
        scratch_shapes=[pltpu.VMEM((nr, LANE), F32), pltpu.VMEM((nr, LANE), F32), pltpu.VMEM((NQ, hr, LANE), F32),
                        pltpu.VMEM((nr, LANE), F32), pltpu.SemaphoreType.DMA((12,)), pltpu.SemaphoreType.DMA((12,))],
        compiler_params=_params(None, VMEM_LIMIT),
    )(g, w, m, v)


_SMALL = ["b_ada", "norm_ffn1_g", "norm_mix_g", "pool_scale", "gmlp_ln_g", "gmlp_ln_b", "b_spatial",
          "norm_ffn2_g", "norm_final_g", "w_pool", "w_spatial"]


def _pack(parts):
    blocks, layout, r0 = [], {}, 0
    for name in _SMALL:
        a = parts[name]
        n = a.size
        rows = -(-n // LANE)
        rows8 = -(-rows // 8) * 8
        flat = a.reshape(-1).astype(F32)
        if rows8 * LANE != n:
            flat = jnp.concatenate([flat, jnp.zeros((rows8 * LANE - n,), F32)])
        blocks.append(flat.reshape(rows8, LANE))
        layout[name] = (r0, n, a.shape)
        r0 += rows8
    return jnp.concatenate(blocks, axis=0), layout


def _unpack(packed, layout):
    out = {}
    for name, (r0, n, shape) in layout.items():
        rows = -(-n // LANE)
        out[name] = packed[r0:r0 + rows].reshape(-1)[:n].reshape(shape)
    return out


def _modv(mod9, sub, gain):
    rows = jnp.concatenate([mod9[3 * sub:3 * sub + 3], gain.reshape(1, D), jnp.zeros((4, D), F32)], axis=0)
    return rows


_BIG = ["ffn1_w_in", "ffn1_w_out", "w_mix_in", "w_mix_out", "ffn2_w_in", "ffn2_w_out"]


def kernel(x, c, w_ada, b_ada, norm_ffn1_g, ffn1_w_in, ffn1_w_out, norm_mix_g, w_mix_in, w_pool, pool_scale, gmlp_ln_g, gmlp_ln_b, w_spatial, b_spatial, w_mix_out, norm_ffn2_g, ffn2_w_in, ffn2_w_out, norm_final_g, loss_target, m_w_ada, m_b_ada, m_norm_ffn1_g, m_ffn1_w_in, m_ffn1_w_out, m_norm_mix_g, m_w_mix_in, m_w_pool, m_pool_scale, m_gmlp_ln_g, m_gmlp_ln_b, m_w_spatial, m_b_spatial, m_w_mix_out, m_norm_ffn2_g, m_ffn2_w_in, m_ffn2_w_out, m_norm_final_g, v_w_ada, v_b_ada, v_norm_ffn1_g, v_ffn1_w_in, v_ffn1_w_out, v_norm_mix_g, v_w_mix_in, v_w_pool, v_pool_scale, v_gmlp_ln_g, v_gmlp_ln_b, v_w_spatial, v_b_spatial, v_w_mix_out, v_norm_ffn2_g, v_ffn2_w_in, v_ffn2_w_out, v_norm_final_g):
    names = ["w_ada", "b_ada", "norm_ffn1_g", "ffn1_w_in", "ffn1_w_out", "norm_mix_g", "w_mix_in", "w_pool",
             "pool_scale", "gmlp_ln_g", "gmlp_ln_b", "w_spatial", "b_spatial", "w_mix_out", "norm_ffn2_g",
             "ffn2_w_in", "ffn2_w_out", "norm_final_g"]
    W = dict(zip(names, [w_ada, b_ada, norm_ffn1_g, ffn1_w_in, ffn1_w_out, norm_mix_g, w_mix_in, w_pool, pool_scale,
                         gmlp_ln_g, gmlp_ln_b, w_spatial, b_spatial, w_mix_out, norm_ffn2_g, ffn2_w_in, ffn2_w_out,
                         norm_final_g]))
    M = dict(zip(names, [m_w_ada, m_b_ada, m_norm_ffn1_g, m_ffn1_w_in, m_ffn1_w_out, m_norm_mix_g, m_w_mix_in, m_w_pool,
                         m_pool_scale, m_gmlp_ln_g, m_gmlp_ln_b, m_w_spatial, m_b_spatial, m_w_mix_out, m_norm_ffn2_g,
                         m_ffn2_w_in, m_ffn2_w_out, m_norm_final_g]))
    V = dict(zip(names, [v_w_ada, v_b_ada, v_norm_ffn1_g, v_ffn1_w_in, v_ffn1_w_out, v_norm_mix_g, v_w_mix_in, v_w_pool,
                         v_pool_scale, v_gmlp_ln_g, v_gmlp_ln_b, v_w_spatial, v_b_spatial, v_w_mix_out, v_norm_ffn2_g,
                         v_ffn2_w_in, v_ffn2_w_out, v_norm_final_g]))

    xi, yi, ci = _me()
    q = 2 * xi + yi
    core = ci.astype(jnp.int32).reshape(1)

    chip = q.astype(jnp.int32).reshape(1)
    place = lambda n: _cast_place(W[n][0], chip, wide=n in ("ffn1_w_in", "ffn2_w_in"))

    ncol = w_ada.shape[2]
    b_q = lax.dynamic_slice_in_dim(b_ada, q * ncol, ncol, axis=1)
    (cact_all, modsel), (win1, wout1) = _ada_fwd(
        c, w_ada[0], b_q, side=_gather_side([place("ffn1_w_in"), place("ffn1_w_out")]))
    mod9 = modsel[:NQ].reshape(9, D)
    xs, target = x[0], loss_target[0]
    mv1 = _modv(mod9, 0, norm_ffn1_g[0])
    mv2 = _modv(mod9, 1, norm_mix_g[0])
    mv3 = _modv(mod9, 2, norm_ffn2_g[0])
    wcat, wtcat, bias = _prep_spatial(w_spatial[0], b_spatial[0].T)
    wpool = w_pool[0].astype(BF16)
    vecs = jnp.concatenate([pool_scale, gmlp_ln_g, gmlp_ln_b, jnp.zeros((5, DP), F32)], axis=0)
    gf = jnp.concatenate([norm_final_g.reshape(1, D), jnp.zeros((7, D), F32)], axis=0)

    later =["w_mix_in", "w_mix_out", "ffn2_w_in", "ffn2_w_out"]
    (x1, g1s, u1s), got = _ffn_fwd(xs, mv1, win1, wout1.reshape(2, CH, D), side=_gather_side([place(n) for n in later]))
    wmi, wmo, win2, wout2 = got
    wmi = jnp.transpose(wmi, (1, 0, 2)).reshape(D, DPROJ)
    wmo = wmo.reshape(DP + DG, D)
    x2, pooled, zb = _mix_fwd(x1, mv2, wmi, wpool, vecs, wcat, bias, wmo)
    (dx3, g3s, u3s, loss_blk, dgf), _ = _ffn_fwd(x2, mv3, win2, wout2.reshape(2, CH, D), head=(target, gf))

    wo1, wo2 = wout1.reshape(2, CH, D), wout2.reshape(2, CH, D)
    (dx2, oin2, oout2, rin2, rout2, vec3), _ = _ffn_bwd(x2, dx3, g3s, u3s, mv3, win2, wo2)
    cs2 = [_chip_sum_pair(oin2, rin2), _chip_sum_pair(oout2, rout2)]
    (dx1, dwmi, dwmo, dwpool, dwsp, dbsp, v512, vec2), ex2 = _mix_bwd(
        x1, dx2, pooled, zb, mv2, wmi, wpool, vecs, wcat, wtcat, bias, wmo, side=_exchange_side(cs2))
    half2 = [_sum4(cs, e, chip) for cs, e in zip(cs2, ex2)]
    qcols = w_mix_in.shape[2]
    vmix = [jnp.transpose(dwmi.reshape(D, NQ, qcols), (1, 0, 2)).reshape(NQ, 2, D // 2, qcols),
            dwmo.reshape(NQ, 2, (DP + DG) // 8, D)]
    first1, got = _ffn_bwd_pass(0, xs, dx1, g1s, u1s, mv1, win1, wo1,
                                side=_join(_swap_side(vmix), _share_side(half2)))
    sibmix, other2 = got[:2], got[2:]
    cs_mix = [_chip_sum(g, r, core) for g, r in zip(vmix, sibmix)]
    (grad_x, oin1, oout1, rin1, rout1, vec1), ex_mix = _ffn_bwd_pass(
        1, xs, dx1, g1s, u1s, mv1, win1, wo1, prev=first1[:5], side=_exchange_side(cs_mix))
    vec1 = first1[5] + vec1
    cs_ffn1 = [_chip_sum_pair(oin1, rin1), _chip_sum_pair(oout1, rout1)]

    dmod =jnp.concatenate([vec1[0:3], vec2[0:3], vec3[0:3]], axis=0)
    grads = dict(
        b_ada=dmod.reshape(1, 9 * D), norm_ffn1_g=vec1[3:4], norm_mix_g=vec2[3:4], norm_ffn2_g=vec3[3:4],
        pool_scale=v512[0:1], gmlp_ln_g=v512[1:2], gmlp_ln_b=v512[2:3], b_spatial=dbsp[None],
        norm_final_g=dgf[0], w_pool=dwpool[None], w_spatial=dwsp[None])

    gp, layout = _pack({n: grads[n] for n in _SMALL})
    gp = jnp.concatenate([gp, loss_blk, loss_blk], axis=0)
    pad = jnp.zeros((16, LANE), F32)
    wp, mp, vp = [jnp.concatenate([_pack({n: src[n] for n in _SMALL})[0], pad], axis=0) for src in (W, M, V)]
    r0, nb, _ = layout["b_ada"]
    assert r0 == 0
    out_g, out_d, out_m, out_v = {}, {}, {}, {}
    gs, dl, mo, vo, gath = _small_allreduce_adamw(gp, wp, mp, vp, nb // LANE)
    loss = gs[-16, 0]
    for packed, dst in ((gs, out_g), (dl, out_d), (mo, out_m), (vo, out_v)):
        for n, a in _unpack(packed, layout).items():
            dst[n] = a.reshape(W[n].shape)

    def update(n, own, recv, after=()):
        (g2, d, mn, vn), _ = _adamw_halves(W[n][0], own, recv, M[n][0], V[n][0], after=after)
        out_g[n], out_d[n], out_m[n], out_v[n] = g2[None], d[None], mn[None], vn[None]
        return g2

    ssem, rsem, cs_fly, land_fly, token = _exchange_start(cs_ffn1, after=(gath,))
    dmod_q = lax.dynamic_slice_in_dim(gath.reshape(N_DEV, nb), q * ncol, ncol, axis=1) + token[0:8, 0:1]
    (ga, da, ma, va), _ = _ada_grad_adamw(cact_all.T, dmod_q, w_ada[0], m_w_ada[0], v_w_ada[0])
    out_g["w_ada"], out_d["w_ada"], out_m["w_ada"], out_v["w_ada"] = ga[None], da[None], ma[None], va[None]
    done = [ga, update("ffn2_w_in", half2[0], other2[0], after=(token,)),
            update("ffn2_w_out", half2[1], other2[1], after=(token,))]
    half_mix = [_sum4(cs, e, chip, after=(token,)) for cs, e in zip(cs_mix, ex_mix)]
    other_mix = _run_side(_share_side(half_mix), "share_mix")
    done += [update(n, own, recv) for n, own, recv in zip(["w_mix_in", "w_mix_out"], half_mix, other_mix)]
    cs_ffn1, ex_ffn1 = _exchange_wait(ssem, rsem, cs_fly, land_fly, after=done)
    half1 = [_sum4(cs, e, chip) for cs, e in zip(cs_ffn1, ex_ffn1)]
    other1 = _run_side(_share_side(half1), "share_ffn1")
    for n, own, recv in zip(["ffn1_w_in", "ffn1_w_out"], half1, other1):
        update(n, own, recv)

    return (loss, grad_x[None], *[out_g[n] for n in names], *[out_d[n] for n in names],
            *[out_m[n] for n in names], *[out_v[n] for n in names])
```

```python
import functools
import math

import jax
import jax.numpy as jnp
from jax import lax
from jax.experimental import pallas as pl
from jax.experimental.pallas import tpu as pltpu

F32 = jnp.float32
BF16 = jnp.bfloat16
MESH = pl.DeviceIdType.MESH
HIGHEST = lax.Precision.HIGHEST

EPS = 1e-6
D = 1024
DFF = 2816
CH = DFF // 2
NQ = 4
DP = 512
DG = 512
DPROJ = DP + 2 * DG
POOL_WINDOWS = (2, 4, 8, 16)
HALO = 16
CHUNK = 128
LANE = 128
N_DEV = 8

ADAM_LR = 0.001
ADAM_B1 = 0.9
ADAM_B2 = 0.999
ADAM_EPS = 1e-08
ADAM_WD = 0.01
ADAM_STEP = 10

VMEM_LIMIT = 62 * 1024 * 1024

TM_FFN_FWD = 512
TM_FFN_BWD = 512
TM_MIX = 512


def _call(body, **kw):
    return pl.pallas_call(body, interpret=False, **kw)


def _params(sem=None, vmem=None):
    return pltpu.CompilerParams(dimension_semantics=sem, vmem_limit_bytes=vmem)


def _sds(shape, dtype):
    return jax.ShapeDtypeStruct(shape, dtype)


ANY = pl.BlockSpec(memory_space=pl.ANY)
VMEM = pl.BlockSpec(memory_space=pltpu.VMEM)
SMEM = pl.BlockSpec(memory_space=pltpu.SMEM)


def _norm_mod(x, gn, sc, sh):
    r = lax.rsqrt(jnp.mean(x * x, axis=-1, keepdims=True) + EPS)
    xn = x * r
    hp = xn * gn
    return r, xn, hp, hp * (1.0 + sc) + sh


def _norm_mod_bwd(dh, r, xn, hp, gn, sc):
    one_sc = 1.0 + sc
    dsh = jnp.sum(dh, axis=0, keepdims=True)
    dsc = jnp.sum(dh * hp, axis=0, keepdims=True)
    dgn = jnp.sum(dh * one_sc * xn, axis=0, keepdims=True)
    dxn = dh * (gn * one_sc)
    dx = r * (dxn - xn * jnp.mean(dxn * xn, axis=-1, keepdims=True))
    return dsh, dsc, dgn, dx


def _dot(a, b):
    return jnp.dot(a, b, preferred_element_type=F32)


def _dot_nt(a, b):
    return lax.dot_general(a, b, (((1,), (1,)), ((), ())), preferred_element_type=F32)


def _dot_tn(a, b):
    return lax.dot_general(a, b, (((0,), (0,)), ((), ())), preferred_element_type=F32)


_GELU_C = math.sqrt(2.0 / math.pi)
_GELU_A = 0.044715


def _gelu_fwd_bwd(x):
    x2 = x * x
    t = jnp.tanh(_GELU_C * (x + _GELU_A * x * x2))
    g = 0.5 * x * (1.0 + t)
    dg = 0.5 * (1.0 + t) + 0.5 * x * (1.0 - t * t) * (_GELU_C * (1.0 + 3.0 * _GELU_A * x2))
    return g, dg


def _adamw(w, g, m, v):
    m = ADAM_B1 * m + (1.0 - ADAM_B1) * g
    v = ADAM_B2 * v + (1.0 - ADAM_B2) * (g * g)
    m_hat = m / (1.0 - ADAM_B1 ** ADAM_STEP)
    v_hat = v / (1.0 - ADAM_B2 ** ADAM_STEP)
    delta = -ADAM_LR * (m_hat / (jnp.sqrt(v_hat) + ADAM_EPS) + ADAM_WD * w)
    return delta, m, v


def _row_block(rows, cap=256, mult=16):
    best = None
    for t in range(mult, min(rows, cap) + 1, mult):
        if rows % t == 0:
            best = t
    assert best is not None, rows
    return best


def _head_math(x, target, gf):
    r = lax.rsqrt(jnp.mean(x * x, axis=-1, keepdims=True) + EPS)
    xn = x * r
    err = xn * gf - target
    dy = err * (1.0 / D)
    dxn = dy * gf
    dx = r * (dxn - xn * jnp.mean(dxn * xn, axis=-1, keepdims=True))
    return (0.5 / D) * jnp.sum(err * err), jnp.sum(dy * xn, axis=0, keepdims=True), dx


def _ffn_fwd(x, modv, win, wout, side=None, head=None):
    S = x.shape[0]
    tm = TM_FFN_FWD
    nt = S // tm

    def body(*refs):
        if head is None:
            x_ref, mod_ref, wgu_ref, wo_ref, xo_ref, gs_ref, us_ref, acc_scr = refs
        else:
            (x_ref, mod_ref, wgu_ref, wo_ref, t_ref, gf_ref,
             xo_ref, gs_ref, us_ref, loss_ref, dgf_ref, acc_scr) = refs

        @pl.when((pl.program_id(0) == 0) & (pl.program_id(1) == 0))
        def _():
            acc_scr[...] = jnp.zeros_like(acc_scr)
            if head is not None:
                loss_ref[...] = jnp.zeros_like(loss_ref)
                dgf_ref[...] = jnp.zeros_like(dgf_ref)

        j = pl.program_id(1)
        h = _norm_mod(x_ref[...], mod_ref[3:4, :], mod_ref[1:2, :], mod_ref[0:1, :])[3].astype(BF16)
        gu = _dot(h, wgu_ref[...])
        g = gu[:, :CH].astype(BF16)
        u = gu[:, CH:].astype(BF16)
        gs_ref[...] = g
        us_ref[...] = u
        gf = g.astype(F32)
        a = (gf * jax.nn.sigmoid(gf) * u.astype(F32)).astype(BF16)
        acc = jnp.where(j == 0, 0.0, acc_scr[...]) + _dot(a, wo_ref[...])
        acc_scr[...] = acc
        xo = x_ref[...] + (0.5 * mod_ref[2:3, :]) * acc
        if head is None:
            xo_ref[...] = xo
        else:
            @pl.when(j == 1)
            def _():
                loss, dgf, dx = _head_math(xo, t_ref[...], gf_ref[0:1, :])
                loss_ref[...] += loss
                dgf_ref[0:1, :] += dgf
                xo_ref[...] = dx

    step = lambda i, j: lambda: (pl.program_id(0) == i) & (pl.program_id(1) == j)
    tile = pl.BlockSpec((tm, D), lambda i, j: (i, 0))
    const = lambda shape: pl.BlockSpec(shape, lambda i, j: (0, 0))
    chunk = pl.BlockSpec((tm, CH), lambda i, j: (i, j))
    in_specs = [tile, const((8, D)), pl.BlockSpec((None, D, 2 * CH), lambda i, j: (j, 0, 0)),
                pl.BlockSpec((None, CH, D), lambda i, j: (j, 0, 0))]
    out_specs = [tile, chunk, chunk]
    out_shape = [_sds((S, D), F32), _sds((S, DFF), BF16), _sds((S, DFF), BF16)]
    args = (x, modv, win, wout)
    if head is not None:
        in_specs += [tile, const((8, D))]
        out_specs += [const((8, LANE)), const((8, D))]
        out_shape += [_sds((8, LANE), F32), _sds((8, D), F32)]
        args += tuple(head)
    return _side_call(
        body, side, (step(0, 0), step((7 * nt) // 10, 0), step(nt - 1, 1)), name="ffn_fwd",
        grid=(nt, 2), in_specs=in_specs, out_specs=out_specs, out_shape=out_shape,
        scratch_shapes=[pltpu.VMEM((tm, D), F32)],
        compiler_params=_params(("arbitrary", "arbitrary"), VMEM_LIMIT),
        args=args)


def _ffn_bwd_pass(jj, x, dxo, gs, us, modv, win, wout, prev=None, side=None):
    S = x.shape[0]
    tm = TM_FFN_BWD
    nt = S // tm
    hi, ho = D // 2, CH // 4
    last = prev is not None
    assert last == (jj == 1)

    def body(*refs):
        x_ref, dxo_ref, gs_ref, us_ref, mod_ref, win_hbm, wo_ref = refs[:7]
        k = 12 if last else 7
        out_ref, dwin_ref, dwout_ref, rwin_ref, rwout_ref, vec_ref = refs[k:k + 6]
        accgu, accw, wgu, sems, fsend, frecv = refs[k + 6:]
        i = pl.program_id(0)

        @pl.when(i == 0)
        def _():
            load = pltpu.make_async_copy(win_hbm.at[jj], wgu, sems.at[0])
            load.start()
            accgu[...] = jnp.zeros_like(accgu)
            accw[...] = jnp.zeros_like(accw)
            vec_ref[...] = jnp.zeros_like(vec_ref)
            load.wait()

        gn, sc, sh, gate = mod_ref[3:4, :], mod_ref[1:2, :], mod_ref[0:1, :], mod_ref[2:3, :]

        r, xn, hp, h = _norm_mod(x_ref[...], gn, sc, sh)
        dxo = dxo_ref[...]
        dy = (dxo * (0.5 * gate)).astype(BF16)
        g = gs_ref[...].astype(F32)
        u = us_ref[...].astype(F32)
        sig = jax.nn.sigmoid(g)
        sl = g * sig
        a = (sl * u).astype(BF16)
        da = _dot_nt(dy, wo_ref[...])
        dg = (da * u * (sig * (1.0 + g * (1.0 - sig)))).astype(BF16)
        du = (da * sl).astype(BF16)
        dgu = jnp.concatenate([dg, du], axis=1)
        dhp = _dot_nt(dgu, wgu[...])
        if last:
            dsh, dsc, dgn, dxin = _norm_mod_bwd(refs[7][...] + dhp, r, xn, hp, gn, sc)
            vec_ref[0:1, :] += dsh
            vec_ref[1:2, :] += dsc
            vec_ref[3:4, :] += dgn
            out_ref[...] = dxo + dxin
        else:
            out_ref[...] = dhp
        accw[...] += _dot_tn(a, dxo.astype(BF16))
        accgu[...] += _dot_tn(h.astype(BF16), dgu)

        @pl.when(i == nt - 1)
        def _():
            gw = accw[...]
            vec_ref[2:3, :] += 0.5 * jnp.sum(wo_ref[...].astype(F32) * gw, axis=0, keepdims=True)
            accw[...] = gw * (0.5 * gate)
            mx, my, cc = _me()
            part = lambda acc, base, n, c, col: acc.at[pl.ds(base + c * n, n), pl.ds(col[0], col[1])]
            pieces = [(accgu, 0, hi, (0, CH), dwin_ref, rwin_ref, jj), (accgu, 0, hi, (CH, CH), dwin_ref, rwin_ref, 2 + jj),
                      (accw, 0, ho, (0, D), dwout_ref, rwout_ref, 2 * jj),
                      (accw, 2 * ho, ho, (0, D), dwout_ref, rwout_ref, 2 * jj + 1)]
            loc = [pltpu.make_async_copy(part(acc, base, n, cc, col), own.at[slot], sems.at[p])
                   for p, (acc, base, n, col, own, _, slot) in enumerate(pieces)]
            rem = [pltpu.make_async_remote_copy(part(acc, base, n, 1 - cc, col), sib.at[slot], fsend.at[p], frecv.at[p],
                                                device_id=(mx, my, 1 - cc), device_id_type=MESH)
                   for p, (acc, base, n, col, _, sib, slot) in enumerate(pieces)]
            for cp in loc + rem:
                cp.start()
            for cp in loc:
                cp.wait()
            for cp in rem:
                cp.wait()

    once = pl.Buffered(1)
    tile = pl.BlockSpec((tm, D), lambda i: (i, 0))
    chunk = pl.BlockSpec((tm, CH), lambda i: (i, jj))
    in_specs = [tile, tile, chunk, chunk, pl.BlockSpec((8, D), lambda i: (0, 0)), ANY,
                pl.BlockSpec((None, CH, D), lambda i: (jj, 0, 0), pipeline_mode=once)]
    args = (x, dxo, gs, us, modv, win, wout)
    if last:
        in_specs += [tile, ANY, ANY, ANY, ANY]
        args += tuple(prev)
    step = lambda s: lambda: pl.program_id(0) == s
    return _side_call(
        body, side, (step(0), None, step(nt - 1)), name="ffn_bwd",
        grid=(nt,), in_specs=in_specs,
        out_specs=[tile, ANY, ANY, ANY, ANY, pl.BlockSpec((8, D), lambda i: (0, 0))],
        out_shape=[_sds((S, D), F32), _sds((NQ, hi, CH), F32), _sds((NQ, ho, D), F32), _sds((NQ, hi, CH), F32),
                   _sds((NQ, ho, D), F32), _sds((8, D), F32)],
        scratch_shapes=[pltpu.VMEM((D, 2 * CH), F32), pltpu.VMEM((CH, D), F32), pltpu.VMEM((D, 2 * CH), BF16),
                        pltpu.SemaphoreType.DMA((4,)), pltpu.SemaphoreType.DMA((4,)), pltpu.SemaphoreType.DMA((4,))],
        aliases={8 + p: 1 + p for p in range(4)} if last else {},
        compiler_params=_params(("arbitrary",), VMEM_LIMIT),
        args=args)


def _ffn_bwd(x, dxo, gs, us, modv, win, wout, side=None):
    first, extra = _ffn_bwd_pass(0, x, dxo, gs, us, modv, win, wout, side=side)
    (dx, dwin, dwout, rwin, rwout, vec), _ = _ffn_bwd_pass(1, x, dxo, gs, us, modv, win, wout, prev=first[:5])
    return (dx, dwin, dwout, rwin, rwout, first[5] + vec), extra


def _prep_spatial(w_spatial, b_spatial_t):
    def body(w_ref, b_ref, wcat_ref, wtcat_ref, bias_ref):
        row = lax.broadcasted_iota(jnp.int32, (CHUNK, CHUNK), 0)
        col = lax.broadcasted_iota(jnp.int32, (CHUNK, CHUNK), 1)
        tril = col <= row
        for p in range(4):
            wa = jnp.where(tril, w_ref[2 * p], 0.0)
            wb = jnp.where(tril, w_ref[2 * p + 1], 0.0)
            wcat_ref[p] = jnp.concatenate([wa, wb], axis=1).astype(BF16)
            wtcat_ref[p] = jnp.concatenate([wa.T, wb.T], axis=1).astype(BF16)
        head = lax.broadcasted_iota(jnp.int32, (8, DG), 0)
        ch = lax.broadcasted_iota(jnp.int32, (8, DG), 1)
        spread = jnp.where(ch // 64 == head, 1.0, 0.0).astype(F32)
        bias_ref[...] = jnp.dot(b_ref[...], spread, precision=HIGHEST, preferred_element_type=F32)

    return _call(
        body, name="prep_spatial",
        in_specs=[VMEM, VMEM], out_specs=[VMEM, VMEM, VMEM],
        out_shape=[_sds((4, CHUNK, 2 * CHUNK), BF16), _sds((4, CHUNK, 2 * CHUNK), BF16), _sds((CHUNK, DG), F32)],
    )(w_spatial, b_spatial_t)


def _pair_rhs(blocks):
    lane = lax.broadcasted_iota(jnp.int32, (CHUNK, LANE), 1)
    lo = lane < 64
    top = jnp.concatenate([jnp.where(lo, b, 0.0) for b in blocks], axis=1)
    bot = jnp.concatenate([jnp.where(lo, 0.0, b) for b in blocks], axis=1)
    return top, bot


def _gmlp_branch(zb, vecs, wcat_ref, bias_ref, nchunks):
    z, dz = _gelu_fwd_bwd(zb)
    u = z[:, :DG]
    v = z[:, DG:]
    ln_g, ln_b = vecs[1:2, :], vecs[2:3, :]
    mu = jnp.mean(v, axis=-1, keepdims=True)
    vc = v - mu
    rstd = lax.rsqrt(jnp.mean(vc * vc, axis=-1, keepdims=True) + EPS)
    vhat = vc * rstd
    vl = vhat * ln_g + ln_b
    sv_cols = []
    for p in range(4):
        blocks = [vl[k * CHUNK:(k + 1) * CHUNK, p * LANE:(p + 1) * LANE] for k in range(nchunks)]
        top, bot = _pair_rhs(blocks)
        rhs = jnp.concatenate([top, bot], axis=0).astype(BF16)
        out = _dot(wcat_ref[p], rhs)
        bias = bias_ref[:, p * LANE:(p + 1) * LANE]
        sv_cols.append(jnp.concatenate([out[:, k * LANE:(k + 1) * LANE] + bias for k in range(nchunks)], axis=0))
    sv = jnp.concatenate(sv_cols, axis=1)
    return dict(u=u, dz=dz, rstd=rstd, vhat=vhat, vl=vl, sv=sv, yb=u * sv)


def _mix_fwd(x, modv, win, wpool, vecs, wcat, bias, wout):
    S = x.shape[0]
    tm = TM_MIX
    nt = S // tm
    nchunks = tm // CHUNK

    def body(x_ref, mod_ref, win_ref, wpool_ref, vec_ref, wcat_ref, bias_ref, wout_ref,
             xo_ref, pooled_ref, zb_ref, ext):
        i = pl.program_id(0)

        @pl.when(i == 0)
        def _():
            ext[0:HALO, :] = jnp.zeros((HALO, DP), F32)

        x = x_ref[...]
        _, _, _, h = _norm_mod(x, mod_ref[3:4, :], mod_ref[1:2, :], mod_ref[0:1, :])
        proj = _dot(h.astype(BF16), win_ref[...])
        xa = proj[:, :DP]
        zb = proj[:, DP:]
        zb_ref[...] = zb
        ext[HALO:HALO + tm, :] = xa
        pos = i * tm + lax.broadcasted_iota(jnp.int32, (tm, 1), 0)
        vecs = vec_ref[...]
        ya_cols = []
        pooled_cols = []
        for gi, w in enumerate(POOL_WINDOWS):
            cols = slice(gi * LANE, (gi + 1) * LANE)
            s = xa[:, cols]
            for k in range(1, w):
                s = s + ext[HALO - k:HALO - k + tm, cols]
            cnt = jnp.minimum(pos + 1, w).astype(F32)
            pooled = (s / cnt - xa[:, cols]).astype(BF16)
            pooled_cols.append(pooled)
            ya_cols.append(_dot(pooled, wpool_ref[gi]) * vecs[0:1, cols])
        pooled_ref[...] = jnp.concatenate(pooled_cols, axis=1)
        ext[0:HALO, :] = ext[tm:tm + HALO, :]

        gm = _gmlp_branch(zb, vecs, wcat_ref, bias_ref, nchunks)
        cat = jnp.concatenate(ya_cols + [gm["yb"]], axis=1).astype(BF16)
        xo_ref[...] = x + mod_ref[2:3, :] * _dot(cat, wout_ref[...])

    full = lambda shape: pl.BlockSpec(shape, lambda i: (0,) * len(shape))
    return _call(
        body, name="mix_fwd",
        grid=(nt,),
        in_specs=[pl.BlockSpec((tm, D), lambda i: (i, 0)), full((8, D)), full((D, DPROJ)),
                  full((4, LANE, LANE)), full((8, DP)), full((4, CHUNK, 2 * CHUNK)), full((CHUNK, DG)),
                  full((DP + DG, D))],
        out_specs=[pl.BlockSpec((tm, D), lambda i: (i, 0)), pl.BlockSpec((tm, DP), lambda i: (i, 0)),
                   pl.BlockSpec((tm, 2 * DG), lambda i: (i, 0))],
        out_shape=[_sds((S, D), F32), _sds((S, DP), BF16), _sds((S, 2 * DG), F32)],
        scratch_shapes=[pltpu.VMEM((tm + HALO, DP), F32)],
        compiler_params=_params(("arbitrary",), VMEM_LIMIT),
    )(x, modv, win, wpool, vecs, wcat, bias, wout)


def _mix_bwd(x, dxo, pooled, zb, modv, win, wpool, vecs, wcat, wtcat, bias, wout, side=None):
    S = x.shape[0]
    tm = TM_MIX
    nt = S // tm
    nchunks = tm // CHUNK

    def body(x_ref, dxo_ref, pooled_ref, zb_ref, mod_ref, win_ref, wpool_ref, vec_ref, wcat_ref, wtcat_ref,
             bias_ref, wout_ref,
             dx_ref, dwin_ref, dwout_ref, dwpool_ref, dwsp_ref, dbsp_ref, v512_ref, vd_ref, qext, dsv_acc):
        step = pl.program_id(0)
        tile = nt - 1 - step

        @pl.when(step == 0)
        def _():
            dwin_ref[...] = jnp.zeros_like(dwin_ref)
            dwout_ref[...] = jnp.zeros_like(dwout_ref)
            dwpool_ref[...] = jnp.zeros_like(dwpool_ref)
            dwsp_ref[...] = jnp.zeros_like(dwsp_ref)
            v512_ref[...] = jnp.zeros_like(v512_ref)
            vd_ref[...] = jnp.zeros_like(vd_ref)
            dsv_acc[...] = jnp.zeros_like(dsv_acc)
            qext[tm:tm + HALO, :] = jnp.zeros((HALO, DP), F32)

        gn, sc, sh, gate = mod_ref[3:4, :], mod_ref[1:2, :], mod_ref[0:1, :], mod_ref[2:3, :]
        vecs = vec_ref[...]
        x = x_ref[...]
        r, xn, hp, h = _norm_mod(x, gn, sc, sh)
        hb = h.astype(BF16)
        dxo = dxo_ref[...]

        pooled = pooled_ref[...]
        mixed_cols = [_dot(pooled[:, gi * LANE:(gi + 1) * LANE], wpool_ref[gi]) for gi in range(4)]
        mixed = jnp.concatenate(mixed_cols, axis=1)
        scale = vecs[0:1, :]
        gm = _gmlp_branch(zb_ref[...], vecs, wcat_ref, bias_ref, nchunks)
        cat = jnp.concatenate([mixed * scale, gm["yb"]], axis=1).astype(BF16)

        dwout_ref[...] += _dot_tn(cat, dxo.astype(BF16))
        dcat = _dot_nt((dxo * gate).astype(BF16), wout_ref[...])
        dya = dcat[:, :DP]
        dyb = dcat[:, DP:]

        v512_ref[0:1, :] += jnp.sum(dya * mixed, axis=0, keepdims=True)
        dmixed = (dya * scale).astype(BF16)
        pos = tile * tm + lax.broadcasted_iota(jnp.int32, (tm, 1), 0)
        dpooled_cols = []
        for gi, w in enumerate(POOL_WINDOWS):
            cols = slice(gi * LANE, (gi + 1) * LANE)
            dp = _dot_nt(dmixed[:, cols], wpool_ref[gi])
            dwpool_ref[gi] += _dot_tn(pooled[:, cols], dmixed[:, cols])
            cnt = jnp.minimum(pos + 1, w).astype(F32)
            qext[0:tm, cols] = dp / cnt
            dpooled_cols.append(dp)
        dxa_cols = []
        for gi, w in enumerate(POOL_WINDOWS):
            cols = slice(gi * LANE, (gi + 1) * LANE)
            s = qext[0:tm, cols]
            for k in range(1, w):
                s = s + qext[k:k + tm, cols]
            dxa_cols.append(s - dpooled_cols[gi])
        qext[tm:tm + HALO, :] = qext[0:HALO, :]

        u, sv, vl = gm["u"], gm["sv"], gm["vl"]
        du = dyb * sv
        dsv = dyb * u
        dvl_cols = []
        for p in range(4):
            cols = slice(p * LANE, (p + 1) * LANE)
            dblocks = [dsv[k * CHUNK:(k + 1) * CHUNK, cols] for k in range(nchunks)]
            vblocks = [vl[k * CHUNK:(k + 1) * CHUNK, cols] for k in range(nchunks)]
            tot = dblocks[0]
            for b in dblocks[1:]:
                tot = tot + b
            dsv_acc[:, cols] += tot
            top, bot = _pair_rhs(dblocks)
            out = _dot(wtcat_ref[p], jnp.concatenate([top, bot], axis=0).astype(BF16))
            dvl_cols.append(jnp.concatenate([out[:, k * LANE:(k + 1) * LANE] for k in range(nchunks)], axis=0))
            vcat = jnp.concatenate(vblocks, axis=1).astype(BF16)
            dwsp_ref[2 * p] += _dot_nt(top.astype(BF16), vcat)
            dwsp_ref[2 * p + 1] += _dot_nt(bot.astype(BF16), vcat)
        dvl = jnp.concatenate(dvl_cols, axis=1)
        vhat, rstd = gm["vhat"], gm["rstd"]
        v512_ref[1:2, :] += jnp.sum(dvl * vhat, axis=0, keepdims=True)
        v512_ref[2:3, :] += jnp.sum(dvl, axis=0, keepdims=True)
        dvh = dvl * vecs[1:2, :]
        dv = rstd * (dvh - jnp.mean(dvh, axis=-1, keepdims=True)
                     - vhat * jnp.mean(dvh * vhat, axis=-1, keepdims=True))
        dzb = jnp.concatenate([du, dv], axis=1) * gm["dz"]

        dproj = jnp.concatenate(dxa_cols + [dzb], axis=1).astype(BF16)
        dwin_ref[...] += _dot_tn(hb, dproj)
        dh = _dot_nt(dproj, win_ref[...])
        dsh, dsc, dgn, dxin = _norm_mod_bwd(dh, r, xn, hp, gn, sc)
        vd_ref[0:1, :] += dsh
        vd_ref[1:2, :] += dsc
        vd_ref[3:4, :] += dgn
        dx_ref[...] = dxo + dxin

        @pl.when(step == nt - 1)
        def _():
            gw = dwout_ref[...]
            vd_ref[2:3, :] += jnp.sum(wout_ref[...].astype(F32) * gw, axis=0, keepdims=True)
            dwout_ref[...] = gw * gate
            row = lax.broadcasted_iota(jnp.int32, (CHUNK, CHUNK), 0)
            col = lax.broadcasted_iota(jnp.int32, (CHUNK, CHUNK), 1)
            for hh in range(8):
                dwsp_ref[hh] = jnp.where(col <= row, dwsp_ref[hh], 0.0)
            head = lax.broadcasted_iota(jnp.int32, (8, DG), 0)
            ch = lax.broadcasted_iota(jnp.int32, (8, DG), 1)
            spread = jnp.where(ch // 64 == head, 1.0, 0.0).astype(F32)
            dbsp_ref[...] = lax.dot_general(spread, dsv_acc[...], (((1,), (1,)), ((), ())),
                                            precision=HIGHEST, preferred_element_type=F32)

    full = lambda shape: pl.BlockSpec(shape, lambda s: (0,) * len(shape))
    rev = lambda cols: pl.BlockSpec((tm, cols), lambda s: (nt - 1 - s, 0))
    step = lambda s: lambda: pl.program_id(0) == s
    return _side_call(
        body, side, (step(0), None, step(nt - 1)), name="mix_bwd",
        grid=(nt,),
        in_specs=[rev(D), rev(D), rev(DP), rev(2 * DG), full((8, D)), full((D, DPROJ)), full((4, LANE, LANE)),
                  full((8, DP)), full((4, CHUNK, 2 * CHUNK)), full((4, CHUNK, 2 * CHUNK)), full((CHUNK, DG)),
                  full((DP + DG, D))],
        out_specs=[rev(D), full((D, DPROJ)), full((DP + DG, D)), full((4, LANE, LANE)), full((8, CHUNK, CHUNK)),
                   full((8, CHUNK)), full((8, DP)), full((8, D))],
        out_shape=[_sds((S, D), F32), _sds((D, DPROJ), F32), _sds((DP + DG, D), F32), _sds((4, LANE, LANE), F32),
                   _sds((8, CHUNK, CHUNK), F32), _sds((8, CHUNK), F32), _sds((8, DP), F32), _sds((8, D), F32)],
        scratch_shapes=[pltpu.VMEM((tm + HALO, DP), F32), pltpu.VMEM((CHUNK, DG), F32)],
        compiler_params=_params(("arbitrary",), VMEM_LIMIT),
        args=(x, dxo, pooled, zb, modv, win, wpool, vecs, wcat, wtcat, bias, wout))


def _chip_sum(g, rbuf, core):
    _, _, hr, cols = g.shape
    tr = _row_block(hr)

    def body(c_ref, g_ref, r_ref, o_ref):
        o_ref[...] = (g_ref[...] + r_ref[...]).astype(BF16)

    return pl.pallas_call(
        body, name="chip_sum", interpret=False,
        grid_spec=pltpu.PrefetchScalarGridSpec(
            num_scalar_prefetch=1, grid=(NQ, hr // tr),
            in_specs=[pl.BlockSpec((None, None, tr, cols), lambda q, i, c: (q, c[0], i, 0)),
                      pl.BlockSpec((None, tr, cols), lambda q, i, c: (q, i, 0))],
            out_specs=pl.BlockSpec((None, tr, cols), lambda q, i, c: (q, i, 0))),
        out_shape=_sds((NQ, hr, cols), BF16),
        compiler_params=_params(("arbitrary", "arbitrary"), None),
    )(core, g, rbuf)


def _chip_sum_pair(own, rbuf):
    _, hr, cols = own.shape
    tr = _row_block(hr)

    def body(a_ref, b_ref, o_ref):
        o_ref[...] = (a_ref[...] + b_ref[...]).astype(BF16)

    spec = pl.BlockSpec((None, tr, cols), lambda q, i: (q, i, 0))
    return _call(
        body, name="chip_sum_pair",
        grid=(NQ, hr // tr),
        in_specs=[spec, spec], out_specs=spec,
        out_shape=_sds((NQ, hr, cols), BF16),
        compiler_params=_params(("arbitrary", "arbitrary"), None),
    )(own, rbuf)


def _sum4(cs, rbuf, chip, after=()):
    _, hr, cols = rbuf.shape
    tr = _row_block(hr)

    def body(q_ref, c_ref, r1_ref, r2_ref, r3_ref, *rest):
        acc = c_ref[...].astype(F32)
        for r in (r1_ref, r2_ref, r3_ref):
            acc = acc + r[...].astype(F32)
        rest[-1][...] = acc

    slot = lambda k: pl.BlockSpec((None, tr, cols), lambda i, q: ((q[0] + k) % NQ, i, 0))
    return pl.pallas_call(
        body, name="sum4", interpret=False,
        grid_spec=pltpu.PrefetchScalarGridSpec(
            num_scalar_prefetch=1, grid=(hr // tr,),
            in_specs=[slot(0), slot(1), slot(2), slot(3)] + [ANY] * len(after),
            out_specs=pl.BlockSpec((tr, cols), lambda i, q: (i, 0))),
        out_shape=_sds((hr, cols), F32),
        compiler_params=_params(("arbitrary",), None),
    )(chip, cs, rbuf, rbuf, rbuf, *after)


def _adamw_halves(w, own, recv, m, v, side=None, after=()):
    rows, cols = w.shape
    hr = rows // 2
    tr = _row_block(hr, mult=8)
    nb = hr // tr

    def body(w_ref, own_ref, recv_ref, m_ref, v_ref, *rest):
        g_ref, d_ref, mo_ref, vo_ref = rest[len(after):]
        g = jnp.where(pl.program_id(0) == lax.axis_index("c"), own_ref[...], recv_ref[...])
        d, mn, vn = _adamw(w_ref[...], g, m_ref[...], v_ref[...])
        g_ref[...] = g
        d_ref[...] = d
        mo_ref[...] = mn
        vo_ref[...] = vn

    full = pl.BlockSpec((tr, cols), lambda h, i: (h * nb + i, 0))
    half = pl.BlockSpec((tr, cols), lambda h, i: (i, 0))
    step = lambda h, i: lambda: (pl.program_id(0) == h) & (pl.program_id(1) == i)
    return _side_call(
        body, side, (step(0, 0), None, step(1, nb - 1)), name="adamw_halves",
        grid=(2, nb), in_specs=[full, half, half, full, full] + [ANY] * len(after), out_specs=[full] * 4,
        out_shape=[_sds((rows, cols), F32)] * 4, scratch_shapes=[],
        compiler_params=_params(("arbitrary", "arbitrary"), None),
        args=(w, own, recv, m, v, *after))


def _cast_place(w, chip, wide=False):
    rows, cols = w.shape
    tr = _row_block(rows)

    def body(q_ref, w_ref, o_ref):
        o_ref[...] = w_ref[...].astype(BF16)

    if wide:
        out_spec = pl.BlockSpec((None, tr, cols), lambda i, q: (q[0] % 2, i, q[0] // 2))
    else:
        out_spec = pl.BlockSpec((None, tr, cols), lambda i, q: (q[0], i, 0))
    return pl.pallas_call(
        body, name="cast_place", interpret=False,
        grid_spec=pltpu.PrefetchScalarGridSpec(
            num_scalar_prefetch=1, grid=(rows // tr,),
            in_specs=[pl.BlockSpec((tr, cols), lambda i, q: (i, 0))],
            out_specs=out_spec),
        out_shape=_sds((2, rows, 2 * cols) if wide else (NQ, rows, cols), BF16),
        compiler_params=_params(("arbitrary",), None),
    )(chip, w)


def _ada_grad_adamw(cact_t, dmod_q, w, m, v, side=None):
    rows, cols = w.shape
    tc = 256
    assert cols % tc == 0

    def body(c_ref, d_ref, w_ref, m_ref, v_ref, g_ref, dl_ref, mo_ref, vo_ref):
        g = jnp.dot(c_ref[...], d_ref[...], precision=HIGHEST, preferred_element_type=F32)
        d, mn, vn = _adamw(w_ref[...], g, m_ref[...], v_ref[...])
        g_ref[...] = g
        dl_ref[...] = d
        mo_ref[...] = mn
        vo_ref[...] = vn

    spec = pl.BlockSpec((rows, tc), lambda i: (0, i))
    step = lambda s: lambda: pl.program_id(0) == s
    return _side_call(
        body, side, (step(0), None, step(cols // tc - 1)), name="ada_grad_adamw",
        grid=(cols // tc,),
        in_specs=[pl.BlockSpec((rows, 8), lambda i: (0, 0)), pl.BlockSpec((8, tc), lambda i: (0, i)),
                  spec, spec, spec],
        out_specs=[spec] * 4,
        out_shape=[_sds((rows, cols), F32)] * 4,
        scratch_shapes=[],
        compiler_params=_params(("arbitrary",), None),
        args=(cact_t, dmod_q, w, m, v))


def _me():
    x, y, c = lax.axis_index("x"), lax.axis_index("y"), lax.axis_index("c")
    return x, y, c


_OFFSETS7 = [(dx, dy, dc) for dx in (0, 1) for dy in (0, 1) for dc in (0, 1) if (dx, dy, dc) != (0, 0, 0)]
_CHIP_OFFSETS = [(1, 0), (0, 1), (1, 1)]


def _ada_fwd(c, w_ada_q, b_ada_q, side=None):
    ncol = w_ada_q.shape[1]

    def body(c_ref, w_ref, b_ref, cact_ref, modsel_ref, blk, gath, res, parts, send_sems, recv_sems, side_start=None):
        x, y, cc = _me()
        me = 4 * x + 2 * y + cc
        q = 2 * x + y
        cv = c_ref[...]
        ca = cv * jax.nn.sigmoid(cv)
        row = lax.broadcasted_iota(jnp.int32, (8, D), 0)
        blk[...] = jnp.where(row == me, jnp.broadcast_to(ca, (8, D)), 0.0)
        gath[me] = blk[...]
        sends = []
        for k, (dx, dy, dc) in enumerate(_OFFSETS7):
            cp = pltpu.make_async_remote_copy(blk, gath.at[me], send_sems.at[k], recv_sems.at[k],
                                              device_id=(x ^ dx, y ^ dy, cc ^ dc), device_id_type=MESH)
            cp.start()
            sends.append(cp)
        if side_start is not None:
            side_start()
        for cp in sends:
            cp.wait_recv()
        cact = gath[0]
        for d in range(1, N_DEV):
            cact = cact + gath[d]
        cact_ref[...] = cact
        res[...] = jnp.dot(cact, w_ref[...], precision=HIGHEST, preferred_element_type=F32) + b_ref[...]
        parts[q] = res[...]
        sends2 = []
        for k, (dx, dy) in enumerate(_CHIP_OFFSETS):
            cp = pltpu.make_async_remote_copy(res, parts.at[q], send_sems.at[7 + k], recv_sems.at[7 + k],
                                              device_id=(x ^ dx, y ^ dy, cc), device_id_type=MESH)
            cp.start()
            sends2.append(cp)
        for cp in sends2:
            cp.wait_recv()
        row2 = lax.broadcasted_iota(jnp.int32, (8, ncol), 0)
        out = jnp.zeros((8, ncol), F32)
        for s in range(NQ):
            mine = jnp.sum(jnp.where(row2 == me, parts[s], 0.0), axis=0, keepdims=True)
            out = out + jnp.where(row2 == s, jnp.broadcast_to(mine, (8, ncol)), 0.0)
        modsel_ref[...] = out
        for cp in sends + sends2:
            cp.wait_send()

    return _side_call(
        body, side, None, name="ada_fwd",
        in_specs=[VMEM, VMEM, VMEM], out_specs=[VMEM, VMEM],
        out_shape=[_sds((8, D), F32), _sds((8, ncol), F32)],
        scratch_shapes=[pltpu.VMEM((8, D), F32), pltpu.VMEM((N_DEV, 8, D), F32), pltpu.VMEM((8, ncol), F32),
                        pltpu.VMEM((NQ, 8, ncol), F32), pltpu.SemaphoreType.DMA((10,)), pltpu.SemaphoreType.DMA((10,))],
        compiler_params=_params(None, VMEM_LIMIT), start_in_body=side is not None,
        args=(c, w_ada_q, b_ada_q))


class _Side:
    def __init__(self, ins, out_shapes, aliases, nsem, start, mid=None, finish=None):
        self.ins, self.out_shapes, self.aliases, self.nsem = list(ins), list(out_shapes), dict(aliases), nsem
        self.start, self.mid, self.finish = start, mid, finish


def _join(*sides):
    ins, outs, aliases, offs, nsem = [], [], {}, [], 0
    for s in sides:
        offs.append((len(ins), len(outs), nsem))
        aliases.update({len(ins) + a: len(outs) + b for a, b in s.aliases.items()})
        ins += s.ins
        outs += s.out_shapes
        nsem += s.nsem

    def hook(name):
        def run(i, o, ss, rs, base):
            for s, (io, oo, so) in zip(sides, offs):
                fn = getattr(s, name)
                if fn is not None:
                    fn(i[io:io + len(s.ins)], o[oo:oo + len(s.out_shapes)], ss, rs, base + so)
        return run

    return _Side(ins, outs, aliases, nsem, hook("start"), hook("mid"), hook("finish"))


def _side_call(body, side, when, *, name, in_specs, out_specs, out_shape, scratch_shapes, args, aliases=None,
               start_in_body=False, **kw):
    n_in, n_out = len(in_specs), len(out_specs)
    aliases = dict(aliases or {})
    if side is None:
        return _call(body, name=name, in_specs=in_specs, out_specs=out_specs, out_shape=out_shape,
                     scratch_shapes=scratch_shapes, input_output_aliases=aliases, **kw)(*args), []
    ns_in, ns_out = len(side.ins), len(side.out_shapes)

    def hook(fn, k, operands):
        if fn is None:
            return
        if when is None:
            fn(*operands, 0)
        elif when[k] is not None:
            pl.when(when[k]())(functools.partial(fn, *operands, 0))

    def wrapped(*refs):
        ins, s_ins = refs[:n_in], refs[n_in:n_in + ns_in]
        o0 = n_in + ns_in
        outs, s_outs = refs[o0:o0 + n_out], refs[o0 + n_out:o0 + n_out + ns_out]
        rest = refs[o0 + n_out + ns_out:]
        scratch, operands = rest[:-2], (s_ins, s_outs, rest[-2], rest[-1])
        if start_in_body:
            body(*ins, *outs, *scratch, side_start=functools.partial(hook, side.start, 0, operands))
        else:
            hook(side.start, 0, operands)
            body(*ins, *outs, *scratch)
        hook(side.mid, 1, operands)
        hook(side.finish, 2, operands)

    res = _call(
        wrapped, name=name,
        in_specs=list(in_specs) + [ANY] * ns_in, out_specs=list(out_specs) + [ANY] * ns_out,
        out_shape=list(out_shape) + side.out_shapes,
        scratch_shapes=list(scratch_shapes) + [pltpu.SemaphoreType.DMA((side.nsem,)),
                                               pltpu.SemaphoreType.DMA((side.nsem,))],
        input_output_aliases={**aliases, **{n_in + a: n_out + b for a, b in side.aliases.items()}},
        **kw)(*args, *side.ins)
    return res[:n_out], res[n_out:]


def _run_side(side, name):
    return _side_call(lambda: None, side, None, name=name, in_specs=[], out_specs=[], out_shape=[],
                      scratch_shapes=[], args=[])[1]


def _remote(src, dst, ss, rs, k, dev):
    return pltpu.make_async_remote_copy(src, dst, ss.at[k], rs.at[k], device_id=dev, device_id_type=MESH)


def _gather_side(bufs):
    n = len(bufs)

    def plan(outs, w):
        x, y, cc = _me()
        hr = outs[w].shape[1] // 2
        mine, other = cc * hr, (1 - cc) * hr
        qx, qy, qd, q = 2 * (x ^ 1) + y, 2 * x + (y ^ 1), 2 * (x ^ 1) + (y ^ 1), 2 * x + y
        xn, yn, sib = (x ^ 1, y, cc), (x, y ^ 1, cc), (x, y, 1 - cc)
        if outs[w].shape[0] == NQ:
            at = lambda slot, r0, nr: outs[w].at[slot, pl.ds(r0, nr)]
        else:
            cols = outs[w].shape[2] // 2
            at = lambda slot, r0, nr: outs[w].at[slot % 2, pl.ds(r0, nr), pl.ds((slot // 2) * cols, cols)]
        send = [(at(q, mine, hr), xn), (at(q, mine, hr), yn),
                (at(qx, mine, hr // 2), yn), (at(qy, mine + hr // 2, hr // 2), xn),
                (at(qx, mine, hr), sib), (at(qy, mine, hr), sib), (at(qd, mine, hr), sib)]
        recv = [at(qx, mine, hr), at(qy, mine, hr), at(qd, mine, hr // 2), at(qd, mine + hr // 2, hr // 2),
                at(qx, other, hr), at(qy, other, hr), at(qd, other, hr)]
        return send, recv

    def op(outs, ss, rs, b, w, k, what):
        send, recv = plan(outs, w)
        if what == "wait_recv":
            _remote(recv[k], recv[k], ss, rs, b + 7 * w + k, send[k][1]).wait_recv()
        else:
            getattr(_remote(send[k][0], send[k][0], ss, rs, b + 7 * w + k, send[k][1]), what)()

    def start(ins, outs, ss, rs, b):
        for w in range(n):
            for k in (0, 1):
                op(outs, ss, rs, b, w, k, "start")

    def mid(ins, outs, ss, rs, b):
        for w in range(n):
            for k in (0, 1):
                op(outs, ss, rs, b, w, k, "wait_recv")
                op(outs, ss, rs, b, w, 2 + k, "start")
                op(outs, ss, rs, b, w, 4 + k, "start")

    def finish(ins, outs, ss, rs, b):
        for w in range(n):
            for k in (2, 3):
                op(outs, ss, rs, b, w, k, "wait_recv")
            op(outs, ss, rs, b, w, 6, "start")
        for w in range(n):
            for k in (4, 5, 6):
                op(outs, ss, rs, b, w, k, "wait_recv")
            for k in range(7):
                op(outs, ss, rs, b, w, k, "wait_send")

    return _Side(bufs, [_sds(tuple(w.shape), w.dtype) for w in bufs], {i: i for i in range(n)}, 7 * n,
                 start, mid, finish)


def _copies_side(ins, out_shapes, nsem, copies):
    def start(*a):
        for cp in copies(*a):
            cp.start()

    def finish(*a):
        for cp in copies(*a):
            cp.wait()

    return _Side(ins, out_shapes, {}, nsem, start, None, finish)


def _swap_side(gs):
    def copies(ins, outs, ss, rs, b):
        x, y, cc = _me()
        return [_remote(ins[w].at[:, 1 - cc], outs[w], ss, rs, b + w, (x, y, 1 - cc)) for w in range(len(gs))]

    return _copies_side(gs, [_sds((NQ,) + tuple(g.shape[2:]), F32) for g in gs], len(gs), copies)


def _exchange_side(cs):
    def copies(ins, outs, ss, rs, b):
        x, y, cc = _me()
        return [_remote(ins[w].at[2 * (x ^ dx) + (y ^ dy)], outs[w].at[2 * x + y], ss, rs, b + 3 * w + j,
                        (x ^ dx, y ^ dy, cc))
                for w in range(len(cs)) for j, (dx, dy) in enumerate(_CHIP_OFFSETS)]

    return _copies_side(cs, [_sds(tuple(c.shape), c.dtype) for c in cs], 3 * len(cs), copies)


def _exchange_copies(srcs, lands, send_sems, recv_sems):
    x, y, cc = _me()
    return [pltpu.make_async_remote_copy(srcs[w].at[2 * (x ^ dx) + (y ^ dy)], lands[w].at[2 * x + y],
                                         send_sems.at[3 * w + j], recv_sems.at[3 * w + j],
                                         device_id=(x ^ dx, y ^ dy, cc), device_id_type=MESH)
            for w in range(len(srcs)) for j, (dx, dy) in enumerate(_CHIP_OFFSETS)]


def _exchange_start(cs, after=()):
    n = len(cs)
    hbm, sem = pl.BlockSpec(memory_space=pltpu.HBM), pl.BlockSpec(memory_space=pltpu.SEMAPHORE)
    srcs = [pltpu.with_memory_space_constraint(c, pltpu.HBM) for c in cs]
    lands = [pltpu.with_memory_space_constraint(lax.empty(c.shape, c.dtype), pltpu.HBM) for c in cs]

    def body(*refs):
        sems = 2 * n + len(after)
        for cp in _exchange_copies(refs[:n], refs[n:2 * n], refs[sems], refs[sems + 1]):
            cp.start()
        refs[-1][...] = jnp.zeros_like(refs[-1])

    res = pl.pallas_call(
        body, name="exchange_start", interpret=False,
        out_shape=(pltpu.SemaphoreType.DMA((3 * n,)), pltpu.SemaphoreType.DMA((3 * n,)),
                   *[pltpu.HBM(c.shape, c.dtype) for c in cs], *[pltpu.HBM(c.shape, c.dtype) for c in cs],
                   _sds((8, LANE), F32)),
        in_specs=(hbm,) * (2 * n) + (ANY,) * len(after), out_specs=(sem, sem) + (hbm,) * (2 * n) + (VMEM,),
        input_output_aliases={i: 2 + i for i in range(2 * n)},
        compiler_params=pltpu.CompilerParams(has_side_effects=pltpu.SideEffectType.DATAFLOW_SIDE_EFFECTING),
    )(*srcs, *lands, *after)
    return res[0], res[1], list(res[2:2 + n]), list(res[2 + n:2 + 2 * n]), res[-1]


def _exchange_wait(send_sems, recv_sems, srcs, lands, after):
    n = len(srcs)
    hbm, sem = pl.BlockSpec(memory_space=pltpu.HBM), pl.BlockSpec(memory_space=pltpu.SEMAPHORE)

    def body(*refs):
        for cp in _exchange_copies(refs[:n], refs[n:2 * n], refs[2 * n], refs[2 * n + 1]):
            cp.wait_send()
            cp.wait_recv()

    res = pl.pallas_call(
        body, name="exchange_wait", interpret=False,
        out_shape=[pltpu.HBM(c.shape, c.dtype) for c in srcs + lands],
        in_specs=(hbm,) * (2 * n) + (sem, sem) + (ANY,) * len(after), out_specs=(hbm,) * (2 * n),
        input_output_aliases={i: i for i in range(2 * n)},
        compiler_params=pltpu.CompilerParams(has_side_effects=pltpu.SideEffectType.DATAFLOW_SIDE_EFFECTING),
    )(*srcs, *lands, send_sems, recv_sems, *after)
    return list(res[:n]), list(res[n:])


def _share_side(fs):
    def copies(ins, outs, ss, rs, b):
        x, y, cc = _me()
        return [_remote(ins[w], outs[w], ss, rs, b + w, (x, y, 1 - cc)) for w in range(len(fs))]

    return _copies_side(fs, [_sds(tuple(f.shape), F32) for f in fs], len(fs), copies)


def _small_allreduce_adamw(g, w, m, v, nd):
    rows = g.shape[0]
    nr = rows - nd
    hr = nr // 2
    assert nd % 8 == 0 and hr % 8 == 0

    def body(g_ref, w_ref, m_ref, v_ref, gs_ref, d_ref, mo_ref, vo_ref, gath, sib, csum, slots, tot, ss, rs):
        x, y, cc = _me()
        me = 4 * x + 2 * y + cc
        q = 2 * x + y
        sibling = (x, y, 1 - cc)
        dm = g_ref.at[pl.ds(0, nd)]
        gath[me] = g_ref[0:nd, :]
        to_all = [_remote(dm, gath.at[me], ss, rs, k, (x ^ dx, y ^ dy, cc ^ dc)) for k, (dx, dy, dc) in enumerate(_OFFSETS7)]
        to_sib = _remote(g_ref.at[pl.ds(nd, nr)], sib, ss, rs, 7, sibling)
        for cp in to_all + [to_sib]:
            cp.start()
        to_sib.wait_recv()
        csum[...] = g_ref[nd:, :] + sib[...]
        mine = pl.ds(pl.multiple_of(cc * hr, 8), hr)
        slots[q] = csum[mine, :]
        to_chips = [_remote(csum.at[mine], slots.at[q], ss, rs, 8 + j, (x ^ dx, y ^ dy, cc))
                    for j, (dx, dy) in enumerate(_CHIP_OFFSETS)]
        for cp in to_chips:
            cp.start()
        for cp in to_chips:
            cp.wait_recv()
        tot[mine, :] = (slots[0] + slots[1]) + (slots[2] + slots[3])
        halves = _remote(tot.at[mine], tot.at[mine], ss, rs, 11, sibling)
        halves.start()
        for cp in to_all:
            cp.wait_recv()
        dsum = gath[0]
        for dev in range(1, N_DEV):
            dsum = dsum + gath[dev]
        halves.wait_recv()
        for lo, n, total in ((0, nd, dsum), (nd, nr, tot[...])):
            gs_ref[lo:lo + n, :] = total
            d, mn, vn = _adamw(w_ref[lo:lo + n, :], total, m_ref[lo:lo + n, :], v_ref[lo:lo + n, :])
            d_ref[lo:lo + n, :] = d
            mo_ref[lo:lo + n, :] = mn
            vo_ref[lo:lo + n, :] = vn
        for cp in to_all + [to_sib, halves] + to_chips:
            cp.wait_send()

    return _call(
        body, name="small_allreduce_adamw",
        in_specs=[VMEM] * 4, out_specs=[VMEM] * 5,
        out_shape=[_sds((rows, LANE), F32)] * 4 + [_sds((N_DEV, nd, LANE), F32)],
        scratch_shapes=[pltpu.VMEM((nr, LANE), F32), pltpu.VMEM((nr, LANE), F32), pltpu.VMEM((NQ, hr, LANE), F32),
                        pltpu.VMEM((nr, LANE), F32), pltpu.SemaphoreType.DMA((12,)), pltpu.SemaphoreType.DMA((12,))],
        compiler_params=_params(None, VMEM_LIMIT),
    )(g, w, m, v)


_SMALL = ["b_ada", "norm_ffn1_g", "norm_mix_g", "pool_scale", "gmlp_ln_g", "gmlp_ln_b", "b_spatial",
          "norm_ffn2_g", "norm_final_g", "w_pool", "w_spatial"]


def _pack(parts):
    blocks, layout, r0 = [], {}, 0
    for name in _SMALL:
        a = parts[name]
        n = a.size
        rows = -(-n // LANE)
        rows8 = -(-rows // 8) * 8
        flat = a.reshape(-1).astype(F32)
        if rows8 * LANE != n:
            flat = jnp.concatenate([flat, jnp.zeros((rows8 * LANE - n,), F32)])
        blocks.append(flat.reshape(rows8, LANE))
        layout[name] = (r0, n, a.shape)
        r0 += rows8
    return jnp.concatenate(blocks, axis=0), layout


def _unpack(packed, layout):
    out = {}
    for name, (r0, n, shape) in layout.items():
        rows = -(-n // LANE)
        out[name] = packed[r0:r0 + rows].reshape(-1)[:n].reshape(shape)
    return out


def _modv(mod9, sub, gain):
    rows = jnp.concatenate([mod9[3 * sub:3 * sub + 3], gain.reshape(1, D), jnp.zeros((4, D), F32)], axis=0)
    return rows


_BIG = ["ffn1_w_in", "ffn1_w_out", "w_mix_in", "w_mix_out", "ffn2_w_in", "ffn2_w_out"]


def kernel(x, c, w_ada, b_ada, norm_ffn1_g, ffn1_w_in, ffn1_w_out, norm_mix_g, w_mix_in, w_pool, pool_scale, gmlp_ln_g, gmlp_ln_b, w_spatial, b_spatial, w_mix_out, norm_ffn2_g, ffn2_w_in, ffn2_w_out, norm_final_g, loss_target, m_w_ada, m_b_ada, m_norm_ffn1_g, m_ffn1_w_in, m_ffn1_w_out, m_norm_mix_g, m_w_mix_in, m_w_pool, m_pool_scale, m_gmlp_ln_g, m_gmlp_ln_b, m_w_spatial, m_b_spatial, m_w_mix_out, m_norm_ffn2_g, m_ffn2_w_in, m_ffn2_w_out, m_norm_final_g, v_w_ada, v_b_ada, v_norm_ffn1_g, v_ffn1_w_in, v_ffn1_w_out, v_norm_mix_g, v_w_mix_in, v_w_pool, v_pool_scale, v_gmlp_ln_g, v_gmlp_ln_b, v_w_spatial, v_b_spatial, v_w_mix_out, v_norm_ffn2_g, v_ffn2_w_in, v_ffn2_w_out, v_norm_final_g):
    names = ["w_ada", "b_ada", "norm_ffn1_g", "ffn1_w_in", "ffn1_w_out", "norm_mix_g", "w_mix_in", "w_pool",
             "pool_scale", "gmlp_ln_g", "gmlp_ln_b", "w_spatial", "b_spatial", "w_mix_out", "norm_ffn2_g",
             "ffn2_w_in", "ffn2_w_out", "norm_final_g"]
    W = dict(zip(names, [w_ada, b_ada, norm_ffn1_g, ffn1_w_in, ffn1_w_out, norm_mix_g, w_mix_in, w_pool, pool_scale,
                         gmlp_ln_g, gmlp_ln_b, w_spatial, b_spatial, w_mix_out, norm_ffn2_g, ffn2_w_in, ffn2_w_out,
                         norm_final_g]))
    M = dict(zip(names, [m_w_ada, m_b_ada, m_norm_ffn1_g, m_ffn1_w_in, m_ffn1_w_out, m_norm_mix_g, m_w_mix_in, m_w_pool,
                         m_pool_scale, m_gmlp_ln_g, m_gmlp_ln_b, m_w_spatial, m_b_spatial, m_w_mix_out, m_norm_ffn2_g,
                         m_ffn2_w_in, m_ffn2_w_out, m_norm_final_g]))
    V = dict(zip(names, [v_w_ada, v_b_ada, v_norm_ffn1_g, v_ffn1_w_in, v_ffn1_w_out, v_norm_mix_g, v_w_mix_in, v_w_pool,
                         v_pool_scale, v_gmlp_ln_g, v_gmlp_ln_b, v_w_spatial, v_b_spatial, v_w_mix_out, v_norm_ffn2_g,
                         v_ffn2_w_in, v_ffn2_w_out, v_norm_final_g]))

    xi, yi, ci = _me()
    q = 2 * xi + yi
    core = ci.astype(jnp.int32).reshape(1)

    chip = q.astype(jnp.int32).reshape(1)
    place = lambda n: _cast_place(W[n][0], chip, wide=n in ("ffn1_w_in", "ffn2_w_in"))

    ncol = w_ada.shape[2]
    b_q = lax.dynamic_slice_in_dim(b_ada, q * ncol, ncol, axis=1)
    (cact_all, modsel), (win1, wout1) = _ada_fwd(
        c, w_ada[0], b_q, side=_gather_side([place("ffn1_w_in"), place("ffn1_w_out")]))
    mod9 = modsel[:NQ].reshape(9, D)
    xs, target = x[0], loss_target[0]
    mv1 = _modv(mod9, 0, norm_ffn1_g[0])
    mv2 = _modv(mod9, 1, norm_mix_g[0])
    mv3 = _modv(mod9, 2, norm_ffn2_g[0])
    wcat, wtcat, bias = _prep_spatial(w_spatial[0], b_spatial[0].T)
    wpool = w_pool[0].astype(BF16)
    vecs = jnp.concatenate([pool_scale, gmlp_ln_g, gmlp_ln_b, jnp.zeros((5, DP), F32)], axis=0)
    gf = jnp.concatenate([norm_final_g.reshape(1, D), jnp.zeros((7, D), F32)], axis=0)

    later =["w_mix_in", "w_mix_out", "ffn2_w_in", "ffn2_w_out"]
    (x1, g1s, u1s), got = _ffn_fwd(xs, mv1, win1, wout1.reshape(2, CH, D), side=_gather_side([place(n) for n in later]))
    wmi, wmo, win2, wout2 = got
    wmi = jnp.transpose(wmi, (1, 0, 2)).reshape(D, DPROJ)
    wmo = wmo.reshape(DP + DG, D)
    x2, pooled, zb = _mix_fwd(x1, mv2, wmi, wpool, vecs, wcat, bias, wmo)
    (dx3, g3s, u3s, loss_blk, dgf), _ = _ffn_fwd(x2, mv3, win2, wout2.reshape(2, CH, D), head=(target, gf))

    wo1, wo2 = wout1.reshape(2, CH, D), wout2.reshape(2, CH, D)
    (dx2, oin2, oout2, rin2, rout2, vec3), _ = _ffn_bwd(x2, dx3, g3s, u3s, mv3, win2, wo2)
    cs2 = [_chip_sum_pair(oin2, rin2), _chip_sum_pair(oout2, rout2)]
    (dx1, dwmi, dwmo, dwpool, dwsp, dbsp, v512, vec2), ex2 = _mix_bwd(
        x1, dx2, pooled, zb, mv2, wmi, wpool, vecs, wcat, wtcat, bias, wmo, side=_exchange_side(cs2))
    half2 = [_sum4(cs, e, chip) for cs, e in zip(cs2, ex2)]
    qcols = w_mix_in.shape[2]
    vmix = [jnp.transpose(dwmi.reshape(D, NQ, qcols), (1, 0, 2)).reshape(NQ, 2, D // 2, qcols),
            dwmo.reshape(NQ, 2, (DP + DG) // 8, D)]
    first1, got = _ffn_bwd_pass(0, xs, dx1, g1s, u1s, mv1, win1, wo1,
                                side=_join(_swap_side(vmix), _share_side(half2)))
    sibmix, other2 = got[:2], got[2:]
    cs_mix = [_chip_sum(g, r, core) for g, r in zip(vmix, sibmix)]
    (grad_x, oin1, oout1, rin1, rout1, vec1), ex_mix = _ffn_bwd_pass(
        1, xs, dx1, g1s, u1s, mv1, win1, wo1, prev=first1[:5], side=_exchange_side(cs_mix))
    vec1 = first1[5] + vec1
    cs_ffn1 = [_chip_sum_pair(oin1, rin1), _chip_sum_pair(oout1, rout1)]

    dmod =jnp.concatenate([vec1[0:3], vec2[0:3], vec3[0:3]], axis=0)
    grads = dict(
        b_ada=dmod.reshape(1, 9 * D), norm_ffn1_g=vec1[3:4], norm_mix_g=vec2[3:4], norm_ffn2_g=vec3[3:4],
        pool_scale=v512[0:1], gmlp_ln_g=v512[1:2], gmlp_ln_b=v512[2:3], b_spatial=dbsp[None],
        norm_final_g=dgf[0], w_pool=dwpool[None], w_spatial=dwsp[None])

    gp, layout = _pack({n: grads[n] for n in _SMALL})
    gp = jnp.concatenate([gp, loss_blk, loss_blk], axis=0)
    pad = jnp.zeros((16, LANE), F32)
    wp, mp, vp = [jnp.concatenate([_pack({n: src[n] for n in _SMALL})[0], pad], axis=0) for src in (W, M, V)]
    r0, nb, _ = layout["b_ada"]
    assert r0 == 0
    out_g, out_d, out_m, out_v = {}, {}, {}, {}
    gs, dl, mo, vo, gath = _small_allreduce_adamw(gp, wp, mp, vp, nb // LANE)
    loss = gs[-16, 0]
    for packed, dst in ((gs, out_g), (dl, out_d), (mo, out_m), (vo, out_v)):
        for n, a in _unpack(packed, layout).items():
            dst[n] = a.reshape(W[n].shape)

    def update(n, own, recv, after=()):
        (g2, d, mn, vn), _ = _adamw_halves(W[n][0], own, recv, M[n][0], V[n][0], after=after)
        out_g[n], out_d[n], out_m[n], out_v[n] = g2[None], d[None], mn[None], vn[None]
        return g2

    ssem, rsem, cs_fly, land_fly, token = _exchange_start(cs_ffn1, after=(gath,))
    dmod_q = lax.dynamic_slice_in_dim(gath.reshape(N_DEV, nb), q * ncol, ncol, axis=1) + token[0:8, 0:1]
    (ga, da, ma, va), _ = _ada_grad_adamw(cact_all.T, dmod_q, w_ada[0], m_w_ada[0], v_w_ada[0])
    out_g["w_ada"], out_d["w_ada"], out_m["w_ada"], out_v["w_ada"] = ga[None], da[None], ma[None], va[None]
    done = [ga, update("ffn2_w_in", half2[0], other2[0], after=(token,)),
            update("ffn2_w_out", half2[1], other2[1], after=(token,))]
    half_mix = [_sum4(cs, e, chip, after=(token,)) for cs, e in zip(cs_mix, ex_mix)]
    other_mix = _run_side(_share_side(half_mix), "share_mix")
    done += [update(n, own, recv) for n, own, recv in zip(["w_mix_in", "w_mix_out"], half_mix, other_mix)]
    cs_ffn1, ex_ffn1 = _exchange_wait(ssem, rsem, cs_fly, land_fly, after=done)
    half1 = [_sum4(cs, e, chip) for cs, e in zip(cs_ffn1, ex_ffn1)]
    other1 = _run_side(_share_side(half1), "share_ffn1")
    for n, own, recv in zip(["ffn1_w_in", "ffn1_w_out"], half1, other1):
        update(n, own, recv)

    return (loss, grad_x[None], *[out_g[n] for n in names], *[out_d[n] for n in names],
            *[out_m[n] for n in names], *[out_v[n] for n in names])
```

```python
import functools
import math

import jax
import jax.numpy as jnp
from jax import lax
from jax.experimental import pallas as pl
from jax.experimental.pallas import tpu as pltpu

F32 = jnp.float32
BF16 = jnp.bfloat16
MESH = pl.DeviceIdType.MESH
HIGHEST = lax.Precision.HIGHEST

EPS = 1e-6
D = 1024
DFF = 2816
CH = DFF // 2
NQ = 4
DP = 512
DG = 512
DPROJ = DP + 2 * DG
POOL_WINDOWS = (2, 4, 8, 16)
HALO = 16
CHUNK = 128
LANE = 128
N_DEV = 8

ADAM_LR = 0.001
ADAM_B1 = 0.9
ADAM_B2 = 0.999
ADAM_EPS = 1e-08
ADAM_WD = 0.01
ADAM_STEP = 10

VMEM_LIMIT = 62 * 1024 * 1024

TM_FFN_FWD = 512
TM_FFN_BWD = 512
TM_MIX = 512


def _call(body, **kw):
    return pl.pallas_call(body, interpret=False, **kw)


def _params(sem=None, vmem=None):
    return pltpu.CompilerParams(dimension_semantics=sem, vmem_limit_bytes=vmem)


def _sds(shape, dtype):
    return jax.ShapeDtypeStruct(shape, dtype)


ANY = pl.BlockSpec(memory_space=pl.ANY)
VMEM = pl.BlockSpec(memory_space=pltpu.VMEM)
SMEM = pl.BlockSpec(memory_space=pltpu.SMEM)


def _norm_mod(x, gn, sc, sh):
    r = lax.rsqrt(jnp.mean(x * x, axis=-1, keepdims=True) + EPS)
    xn = x * r
    hp = xn * gn
    return r, xn, hp, hp * (1.0 + sc) + sh


def _norm_mod_bwd(dh, r, xn, hp, gn, sc):
    one_sc = 1.0 + sc
    dsh = jnp.sum(dh, axis=0, keepdims=True)
    dsc = jnp.sum(dh * hp, axis=0, keepdims=True)
    dgn = jnp.sum(dh * one_sc * xn, axis=0, keepdims=True)
    dxn = dh * (gn * one_sc)
    dx = r * (dxn - xn * jnp.mean(dxn * xn, axis=-1, keepdims=True))
    return dsh, dsc, dgn, dx


def _dot(a, b):
    return jnp.dot(a, b, preferred_element_type=F32)


def _dot_nt(a, b):
    return lax.dot_general(a, b, (((1,), (1,)), ((), ())), preferred_element_type=F32)


def _dot_tn(a, b):
    return lax.dot_general(a, b, (((0,), (0,)), ((), ())), preferred_element_type=F32)


_GELU_C = math.sqrt(2.0 / math.pi)
_GELU_A = 0.044715


def _gelu_fwd_bwd(x):
    x2 = x * x
    t = jnp.tanh(_GELU_C * (x + _GELU_A * x * x2))
    g = 0.5 * x * (1.0 + t)
    dg = 0.5 * (1.0 + t) + 0.5 * x * (1.0 - t * t) * (_GELU_C * (1.0 + 3.0 * _GELU_A * x2))
    return g, dg


def _adamw(w, g, m, v):
    m = ADAM_B1 * m + (1.0 - ADAM_B1) * g
    v = ADAM_B2 * v + (1.0 - ADAM_B2) * (g * g)
    m_hat = m / (1.0 - ADAM_B1 ** ADAM_STEP)
    v_hat = v / (1.0 - ADAM_B2 ** ADAM_STEP)
    delta = -ADAM_LR * (m_hat / (jnp.sqrt(v_hat) + ADAM_EPS) + ADAM_WD * w)
    return delta, m, v


def _row_block(rows, cap=256, mult=16):
    best = None
    for t in range(mult, min(rows, cap) + 1, mult):
        if rows % t == 0:
            best = t
    assert best is not None, rows
    return best


def _head_math(x, target, gf):
    r = lax.rsqrt(jnp.mean(x * x, axis=-1, keepdims=True) + EPS)
    xn = x * r
    err = xn * gf - target
    dy = err * (1.0 / D)
    dxn = dy * gf
    dx = r * (dxn - xn * jnp.mean(dxn * xn, axis=-1, keepdims=True))
    return (0.5 / D) * jnp.sum(err * err), jnp.sum(dy * xn, axis=0, keepdims=True), dx


def _ffn_fwd(x, modv, win, wout, side=None, head=None):
    S = x.shape[0]
    tm = TM_FFN_FWD
    nt = S // tm

    def body(*refs):
        if head is None:
            x_ref, mod_ref, wgu_ref, wo_ref, xo_ref, gs_ref, us_ref, acc_scr = refs
        else:
            (x_ref, mod_ref, wgu_ref, wo_ref, t_ref, gf_ref,
             xo_ref, gs_ref, us_ref, loss_ref, dgf_ref, acc_scr) = refs

        @pl.when((pl.program_id(0) == 0) & (pl.program_id(1) == 0))
        def _():
            acc_scr[...] = jnp.zeros_like(acc_scr)
            if head is not None:
                loss_ref[...] = jnp.zeros_like(loss_ref)
                dgf_ref[...] = jnp.zeros_like(dgf_ref)

        j = pl.program_id(1)
        h = _norm_mod(x_ref[...], mod_ref[3:4, :], mod_ref[1:2, :], mod_ref[0:1, :])[3].astype(BF16)
        gu = _dot(h, wgu_ref[...])
        g = gu[:, :CH].astype(BF16)
        u = gu[:, CH:].astype(BF16)
        gs_ref[...] = g
        us_ref[...] = u
        gf = g.astype(F32)
        a = (gf * jax.nn.sigmoid(gf) * u.astype(F32)).astype(BF16)
        acc = jnp.where(j == 0, 0.0, acc_scr[...]) + _dot(a, wo_ref[...])
        acc_scr[...] = acc
        xo = x_ref[...] + (0.5 * mod_ref[2:3, :]) * acc
        if head is None:
            xo_ref[...] = xo
        else:
            @pl.when(j == 1)
            def _():
                loss, dgf, dx = _head_math(xo, t_ref[...], gf_ref[0:1, :])
                loss_ref[...] += loss
                dgf_ref[0:1, :] += dgf
                xo_ref[...] = dx

    step = lambda i, j: lambda: (pl.program_id(0) == i) & (pl.program_id(1) == j)
    tile = pl.BlockSpec((tm, D), lambda i, j: (i, 0))
    const = lambda shape: pl.BlockSpec(shape, lambda i, j: (0, 0))
    chunk = pl.BlockSpec((tm, CH), lambda i, j: (i, j))
    in_specs = [tile, const((8, D)), pl.BlockSpec((None, D, 2 * CH), lambda i, j: (j, 0, 0)),
                pl.BlockSpec((None, CH, D), lambda i, j: (j, 0, 0))]
    out_specs = [tile, chunk, chunk]
    out_shape = [_sds((S, D), F32), _sds((S, DFF), BF16), _sds((S, DFF), BF16)]
    args = (x, modv, win, wout)
    if head is not None:
        in_specs += [tile, const((8, D))]
        out_specs += [const((8, LANE)), const((8, D))]
        out_shape += [_sds((8, LANE), F32), _sds((8, D), F32)]
        args += tuple(head)
    return _side_call(
        body, side, (step(0, 0), step((7 * nt) // 10, 0), step(nt - 1, 1)), name="ffn_fwd",
        grid=(nt, 2), in_specs=in_specs, out_specs=out_specs, out_shape=out_shape,
        scratch_shapes=[pltpu.VMEM((tm, D), F32)],
        compiler_params=_params(("arbitrary", "arbitrary"), VMEM_LIMIT),
        args=args)


def _ffn_bwd_pass(jj, x, dxo, gs, us, modv, win, wout, prev=None, side=None):
    S = x.shape[0]
    tm = TM_FFN_BWD
    nt = S // tm
    hi, ho = D // 2, CH // 4
    last = prev is not None
    assert last == (jj == 1)

    def body(*refs):
        x_ref, dxo_ref, gs_ref, us_ref, mod_ref, win_hbm, wo_ref = refs[:7]
        k = 12 if last else 7
        out_ref, dwin_ref, dwout_ref, rwin_ref, rwout_ref, vec_ref = refs[k:k + 6]
        accgu, accw, wgu, sems, fsend, frecv = refs[k + 6:]
        i = pl.program_id(0)

        @pl.when(i == 0)
        def _():
            load = pltpu.make_async_copy(win_hbm.at[jj], wgu, sems.at[0])
            load.start()
            accgu[...] = jnp.zeros_like(accgu)
            accw[...] = jnp.zeros_like(accw)
            vec_ref[...] = jnp.zeros_like(vec_ref)
            load.wait()

        gn, sc, sh, gate = mod_ref[3:4, :], mod_ref[1:2, :], mod_ref[0:1, :], mod_ref[2:3, :]

        r, xn, hp, h = _norm_mod(x_ref[...], gn, sc, sh)
        dxo = dxo_ref[...]
        dy = (dxo * (0.5 * gate)).astype(BF16)
        g = gs_ref[...].astype(F32)
        u = us_ref[...].astype(F32)
        sig = jax.nn.sigmoid(g)
        sl = g * sig
        a = (sl * u).astype(BF16)
        da = _dot_nt(dy, wo_ref[...])
        dg = (da * u * (sig * (1.0 + g * (1.0 - sig)))).astype(BF16)
        du = (da * sl).astype(BF16)
        dgu = jnp.concatenate([dg, du], axis=1)
        dhp = _dot_nt(dgu, wgu[...])
        if last:
            dsh, dsc, dgn, dxin = _norm_mod_bwd(refs[7][...] + dhp, r, xn, hp, gn, sc)
            vec_ref[0:1, :] += dsh
            vec_ref[1:2, :] += dsc
            vec_ref[3:4, :] += dgn
            out_ref[...] = dxo + dxin
        else:
            out_ref[...] = dhp
        accw[...] += _dot_tn(a, dxo.astype(BF16))
        accgu[...] += _dot_tn(h.astype(BF16), dgu)

        @pl.when(i == nt - 1)
        def _():
            gw = accw[...]
            vec_ref[2:3, :] += 0.5 * jnp.sum(wo_ref[...].astype(F32) * gw, axis=0, keepdims=True)
            accw[...] = gw * (0.5 * gate)
            mx, my, cc = _me()
            part = lambda acc, base, n, c, col: acc.at[pl.ds(base + c * n, n), pl.ds(col[0], col[1])]
            pieces = [(accgu, 0, hi, (0, CH), dwin_ref, rwin_ref, jj), (accgu, 0, hi, (CH, CH), dwin_ref, rwin_ref, 2 + jj),
                      (accw, 0, ho, (0, D), dwout_ref, rwout_ref, 2 * jj),
                      (accw, 2 * ho, ho, (0, D), dwout_ref, rwout_ref, 2 * jj + 1)]
            loc = [pltpu.make_async_copy(part(acc, base, n, cc, col), own.at[slot], sems.at[p])
                   for p, (acc, base, n, col, own, _, slot) in enumerate(pieces)]
            rem = [pltpu.make_async_remote_copy(part(acc, base, n, 1 - cc, col), sib.at[slot], fsend.at[p], frecv.at[p],
                                                device_id=(mx, my, 1 - cc), device_id_type=MESH)
                   for p, (acc, base, n, col, _, sib, slot) in enumerate(pieces)]
            for cp in loc + rem:
                cp.start()
            for cp in loc:
                cp.wait()
            for cp in rem:
                cp.wait()

    once = pl.Buffered(1)
    tile = pl.BlockSpec((tm, D), lambda i: (i, 0))
    chunk = pl.BlockSpec((tm, CH), lambda i: (i, jj))
    in_specs = [tile, tile, chunk, chunk, pl.BlockSpec((8, D), lambda i: (0, 0)), ANY,
                pl.BlockSpec((None, CH, D), lambda i: (jj, 0, 0), pipeline_mode=once)]
    args = (x, dxo, gs, us, modv, win, wout)
    if last:
        in_specs += [tile, ANY, ANY, ANY, ANY]
        args += tuple(prev)
    step = lambda s: lambda: pl.program_id(0) == s
    return _side_call(
        body, side, (step(0), None, step(nt - 1)), name="ffn_bwd",
        grid=(nt,), in_specs=in_specs,
        out_specs=[tile, ANY, ANY, ANY, ANY, pl.BlockSpec((8, D), lambda i: (0, 0))],
        out_shape=[_sds((S, D), F32), _sds((NQ, hi, CH), F32), _sds((NQ, ho, D), F32), _sds((NQ, hi, CH), F32),
                   _sds((NQ, ho, D), F32), _sds((8, D), F32)],
        scratch_shapes=[pltpu.VMEM((D, 2 * CH), F32), pltpu.VMEM((CH, D), F32), pltpu.VMEM((D, 2 * CH), BF16),
                        pltpu.SemaphoreType.DMA((4,)), pltpu.SemaphoreType.DMA((4,)), pltpu.SemaphoreType.DMA((4,))],
        aliases={8 + p: 1 + p for p in range(4)} if last else {},
        compiler_params=_params(("arbitrary",), VMEM_LIMIT),
        args=args)


def _ffn_bwd(x, dxo, gs, us, modv, win, wout, side=None):
    first, extra = _ffn_bwd_pass(0, x, dxo, gs, us, modv, win, wout, side=side)
    (dx, dwin, dwout, rwin, rwout, vec), _ = _ffn_bwd_pass(1, x, dxo, gs, us, modv, win, wout, prev=first[:5])
    return (dx, dwin, dwout, rwin, rwout, first[5] + vec), extra


def _prep_spatial(w_spatial, b_spatial_t):
    def body(w_ref, b_ref, wcat_ref, wtcat_ref, bias_ref):
        row = lax.broadcasted_iota(jnp.int32, (CHUNK, CHUNK), 0)
        col = lax.broadcasted_iota(jnp.int32, (CHUNK, CHUNK), 1)
        tril = col <= row
        for p in range(4):
            wa = jnp.where(tril, w_ref[2 * p], 0.0)
            wb = jnp.where(tril, w_ref[2 * p + 1], 0.0)
            wcat_ref[p] = jnp.concatenate([wa, wb], axis=1).astype(BF16)
            wtcat_ref[p] = jnp.concatenate([wa.T, wb.T], axis=1).astype(BF16)
        head = lax.broadcasted_iota(jnp.int32, (8, DG), 0)
        ch = lax.broadcasted_iota(jnp.int32, (8, DG), 1)
        spread = jnp.where(ch // 64 == head, 1.0, 0.0).astype(F32)
        bias_ref[...] = jnp.dot(b_ref[...], spread, precision=HIGHEST, preferred_element_type=F32)

    return _call(
        body, name="prep_spatial",
        in_specs=[VMEM, VMEM], out_specs=[VMEM, VMEM, VMEM],
        out_shape=[_sds((4, CHUNK, 2 * CHUNK), BF16), _sds((4, CHUNK, 2 * CHUNK), BF16), _sds((CHUNK, DG), F32)],
    )(w_spatial, b_spatial_t)


def _pair_rhs(blocks):
    lane = lax.broadcasted_iota(jnp.int32, (CHUNK, LANE), 1)
    lo = lane < 64
    top = jnp.concatenate([jnp.where(lo, b, 0.0) for b in blocks], axis=1)
    bot = jnp.concatenate([jnp.where(lo, 0.0, b) for b in blocks], axis=1)
    return top, bot


def _gmlp_branch(zb, vecs, wcat_ref, bias_ref, nchunks):
    z, dz = _gelu_fwd_bwd(zb)
    u = z[:, :DG]
    v = z[:, DG:]
    ln_g, ln_b = vecs[1:2, :], vecs[2:3, :]
    mu = jnp.mean(v, axis=-1, keepdims=True)
    vc = v - mu
    rstd = lax.rsqrt(jnp.mean(vc * vc, axis=-1, keepdims=True) + EPS)
    vhat = vc * rstd
    vl = vhat * ln_g + ln_b
    sv_cols = []
    for p in range(4):
        blocks = [vl[k * CHUNK:(k + 1) * CHUNK, p * LANE:(p + 1) * LANE] for k in range(nchunks)]
        top, bot = _pair_rhs(blocks)
        rhs = jnp.concatenate([top, bot], axis=0).astype(BF16)
        out = _dot(wcat_ref[p], rhs)
        bias = bias_ref[:, p * LANE:(p + 1) * LANE]
        sv_cols.append(jnp.concatenate([out[:, k * LANE:(k + 1) * LANE] + bias for k in range(nchunks)], axis=0))
    sv = jnp.concatenate(sv_cols, axis=1)
    return dict(u=u, dz=dz, rstd=rstd, vhat=vhat, vl=vl, sv=sv, yb=u * sv)


def _mix_fwd(x, modv, win, wpool, vecs, wcat, bias, wout):
    S = x.shape[0]
    tm = TM_MIX
    nt = S // tm
    nchunks = tm // CHUNK

    def body(x_ref, mod_ref, win_ref, wpool_ref, vec_ref, wcat_ref, bias_ref, wout_ref,
             xo_ref, pooled_ref, zb_ref, ext):
        i = pl.program_id(0)

        @pl.when(i == 0)
        def _():
            ext[0:HALO, :] = jnp.zeros((HALO, DP), F32)

        x = x_ref[...]
        _, _, _, h = _norm_mod(x, mod_ref[3:4, :], mod_ref[1:2, :], mod_ref[0:1, :])
        proj = _dot(h.astype(BF16), win_ref[...])
        xa = proj[:, :DP]
        zb = proj[:, DP:]
        zb_ref[...] = zb
        ext[HALO:HALO + tm, :] = xa
        pos = i * tm + lax.broadcasted_iota(jnp.int32, (tm, 1), 0)
        vecs = vec_ref[...]
        ya_cols = []
        pooled_cols = []
        for gi, w in enumerate(POOL_WINDOWS):
            cols = slice(gi * LANE, (gi + 1) * LANE)
            s = xa[:, cols]
            for k in range(1, w):
                s = s + ext[HALO - k:HALO - k + tm, cols]
            cnt = jnp.minimum(pos + 1, w).astype(F32)
            pooled = (s / cnt - xa[:, cols]).astype(BF16)
            pooled_cols.append(pooled)
            ya_cols.append(_dot(pooled, wpool_ref[gi]) * vecs[0:1, cols])
        pooled_ref[...] = jnp.concatenate(pooled_cols, axis=1)
        ext[0:HALO, :] = ext[tm:tm + HALO, :]

        gm = _gmlp_branch(zb, vecs, wcat_ref, bias_ref, nchunks)
        cat = jnp.concatenate(ya_cols + [gm["yb"]], axis=1).astype(BF16)
        xo_ref[...] = x + mod_ref[2:3, :] * _dot(cat, wout_ref[...])

    full = lambda shape: pl.BlockSpec(shape, lambda i: (0,) * len(shape))
    return _call(
        body, name="mix_fwd",
        grid=(nt,),
        in_specs=[pl.BlockSpec((tm, D), lambda i: (i, 0)), full((8, D)), full((D, DPROJ)),
                  full((4, LANE, LANE)), full((8, DP)), full((4, CHUNK, 2 * CHUNK)), full((CHUNK, DG)),
                  full((DP + DG, D))],
        out_specs=[pl.BlockSpec((tm, D), lambda i: (i, 0)), pl.BlockSpec((tm, DP), lambda i: (i, 0)),
                   pl.BlockSpec((tm, 2 * DG), lambda i: (i, 0))],
        out_shape=[_sds((S, D), F32), _sds((S, DP), BF16), _sds((S, 2 * DG), F32)],
        scratch_shapes=[pltpu.VMEM((tm + HALO, DP), F32)],
        compiler_params=_params(("arbitrary",), VMEM_LIMIT),
    )(x, modv, win, wpool, vecs, wcat, bias, wout)


def _mix_bwd(x, dxo, pooled, zb, modv, win, wpool, vecs, wcat, wtcat, bias, wout, side=None):
    S = x.shape[0]
    tm = TM_MIX
    nt = S // tm
    nchunks = tm // CHUNK

    def body(x_ref, dxo_ref, pooled_ref, zb_ref, mod_ref, win_ref, wpool_ref, vec_ref, wcat_ref, wtcat_ref,
             bias_ref, wout_ref,
             dx_ref, dwin_ref, dwout_ref, dwpool_ref, dwsp_ref, dbsp_ref, v512_ref, vd_ref, qext, dsv_acc, dwin_acc):
        step = pl.program_id(0)
        tile = nt - 1 - step

        @pl.when(step == 0)
        def _():
            dwin_acc[...] = jnp.zeros_like(dwin_acc)
            dwout_ref[...] = jnp.zeros_like(dwout_ref)
            dwpool_ref[...] = jnp.zeros_like(dwpool_ref)
            dwsp_ref[...] = jnp.zeros_like(dwsp_ref)
            v512_ref[...] = jnp.zeros_like(v512_ref)
            vd_ref[...] = jnp.zeros_like(vd_ref)
            dsv_acc[...] = jnp.zeros_like(dsv_acc)
            qext[tm:tm + HALO, :] = jnp.zeros((HALO, DP), F32)

        gn, sc, sh, gate = mod_ref[3:4, :], mod_ref[1:2, :], mod_ref[0:1, :], mod_ref[2:3, :]
        vecs = vec_ref[...]
        x = x_ref[...]
        r, xn, hp, h = _norm_mod(x, gn, sc, sh)
        hb = h.astype(BF16)
        dxo = dxo_ref[...]

        pooled = pooled_ref[...]
        mixed_cols = [_dot(pooled[:, gi * LANE:(gi + 1) * LANE], wpool_ref[gi]) for gi in range(4)]
        mixed = jnp.concatenate(mixed_cols, axis=1)
        scale = vecs[0:1, :]
        gm = _gmlp_branch(zb_ref[...], vecs, wcat_ref, bias_ref, nchunks)
        cat = jnp.concatenate([mixed * scale, gm["yb"]], axis=1).astype(BF16)

        dwout_ref[...] += _dot_tn(cat, dxo.astype(BF16))
        dcat = _dot_nt((dxo * gate).astype(BF16), wout_ref[...])
        dya = dcat[:, :DP]
        dyb = dcat[:, DP:]

        v512_ref[0:1, :] += jnp.sum(dya * mixed, axis=0, keepdims=True)
        dmixed = (dya * scale).astype(BF16)
        pos = tile * tm + lax.broadcasted_iota(jnp.int32, (tm, 1), 0)
        dpooled_cols = []
        for gi, w in enumerate(POOL_WINDOWS):
            cols = slice(gi * LANE, (gi + 1) * LANE)
            dp = _dot_nt(dmixed[:, cols], wpool_ref[gi])
            dwpool_ref[gi] += _dot_tn(pooled[:, cols], dmixed[:, cols])
            cnt = jnp.minimum(pos + 1, w).astype(F32)
            qext[0:tm, cols] = dp / cnt
            dpooled_cols.append(dp)
        dxa_cols = []
        for gi, w in enumerate(POOL_WINDOWS):
            cols = slice(gi * LANE, (gi + 1) * LANE)
            s = qext[0:tm, cols]
            for k in range(1, w):
                s = s + qext[k:k + tm, cols]
            dxa_cols.append(s - dpooled_cols[gi])
        qext[tm:tm + HALO, :] = qext[0:HALO, :]

        u, sv, vl = gm["u"], gm["sv"], gm["vl"]
        du = dyb * sv
        dsv = dyb * u
        dvl_cols = []
        for p in range(4):
            cols = slice(p * LANE, (p + 1) * LANE)
            dblocks = [dsv[k * CHUNK:(k + 1) * CHUNK, cols] for k in range(nchunks)]
            vblocks = [vl[k * CHUNK:(k + 1) * CHUNK, cols] for k in range(nchunks)]
            tot = dblocks[0]
            for b in dblocks[1:]:
                tot = tot + b
            dsv_acc[:, cols] += tot
            top, bot = _pair_rhs(dblocks)
            out = _dot(wtcat_ref[p], jnp.concatenate([top, bot], axis=0).astype(BF16))
            dvl_cols.append(jnp.concatenate([out[:, k * LANE:(k + 1) * LANE] for k in range(nchunks)], axis=0))
            vcat = jnp.concatenate(vblocks, axis=1).astype(BF16)
            dwsp_ref[2 * p] += _dot_nt(top.astype(BF16), vcat)
            dwsp_ref[2 * p + 1] += _dot_nt(bot.astype(BF16), vcat)
        dvl = jnp.concatenate(dvl_cols, axis=1)
        vhat, rstd = gm["vhat"], gm["rstd"]
        v512_ref[1:2, :] += jnp.sum(dvl * vhat, axis=0, keepdims=True)
        v512_ref[2:3, :] += jnp.sum(dvl, axis=0, keepdims=True)
        dvh = dvl * vecs[1:2, :]
        dv = rstd * (dvh - jnp.mean(dvh, axis=-1, keepdims=True)
                     - vhat * jnp.mean(dvh * vhat, axis=-1, keepdims=True))
        dzb = jnp.concatenate([du, dv], axis=1) * gm["dz"]

        dproj = jnp.concatenate(dxa_cols + [dzb], axis=1).astype(BF16)
        dwin_acc[...] += _dot_tn(hb, dproj)
        dh = _dot_nt(dproj, win_ref[...])
        dsh, dsc, dgn, dxin = _norm_mod_bwd(dh, r, xn, hp, gn, sc)
        vd_ref[0:1, :] += dsh
        vd_ref[1:2, :] += dsc
        vd_ref[3:4, :] += dgn
        dx_ref[...] = dxo + dxin

        @pl.when(step == nt - 1)
        def _():
            gw = dwout_ref[...]
            vd_ref[2:3, :] += jnp.sum(wout_ref[...].astype(F32) * gw, axis=0, keepdims=True)
            dwout_ref[...] = gw * gate
            for p in range(NQ):
                dwin_ref[p] = dwin_acc[:, p * (DPROJ // NQ):(p + 1) * (DPROJ // NQ)]
            row = lax.broadcasted_iota(jnp.int32, (CHUNK, CHUNK), 0)
            col = lax.broadcasted_iota(jnp.int32, (CHUNK, CHUNK), 1)
            for hh in range(8):
                dwsp_ref[hh] = jnp.where(col <= row, dwsp_ref[hh], 0.0)
            head = lax.broadcasted_iota(jnp.int32, (8, DG), 0)
            ch = lax.broadcasted_iota(jnp.int32, (8, DG), 1)
            spread = jnp.where(ch // 64 == head, 1.0, 0.0).astype(F32)
            dbsp_ref[...] = lax.dot_general(spread, dsv_acc[...], (((1,), (1,)), ((), ())),
                                            precision=HIGHEST, preferred_element_type=F32)

    full = lambda shape: pl.BlockSpec(shape, lambda s: (0,) * len(shape))
    rev = lambda cols: pl.BlockSpec((tm, cols), lambda s: (nt - 1 - s, 0))
    step = lambda s: lambda: pl.program_id(0) == s
    return _side_call(
        body, side, (step(0), None, step(nt - 1)), name="mix_bwd",
        grid=(nt,),
        in_specs=[rev(D), rev(D), rev(DP), rev(2 * DG), full((8, D)), full((D, DPROJ)), full((4, LANE, LANE)),
                  full((8, DP)), full((4, CHUNK, 2 * CHUNK)), full((4, CHUNK, 2 * CHUNK)), full((CHUNK, DG)),
                  full((DP + DG, D))],
        out_specs=[rev(D), full((NQ, D, DPROJ // NQ)), full((DP + DG, D)), full((4, LANE, LANE)),
                   full((8, CHUNK, CHUNK)), full((8, CHUNK)), full((8, DP)), full((8, D))],
        out_shape=[_sds((S, D), F32), _sds((NQ, D, DPROJ // NQ), F32), _sds((DP + DG, D), F32),
                   _sds((4, LANE, LANE), F32), _sds((8, CHUNK, CHUNK), F32), _sds((8, CHUNK), F32), _sds((8, DP), F32),
                   _sds((8, D), F32)],
        scratch_shapes=[pltpu.VMEM((tm + HALO, DP), F32), pltpu.VMEM((CHUNK, DG), F32), pltpu.VMEM((D, DPROJ), F32)],
        compiler_params=_params(("arbitrary",), VMEM_LIMIT),
        args=(x, dxo, pooled, zb, modv, win, wpool, vecs, wcat, wtcat, bias, wout))


def _chip_sum(g, rbuf, core):
    _, _, hr, cols = g.shape
    tr = _row_block(hr)

    def body(c_ref, g_ref, r_ref, o_ref):
        o_ref[...] = (g_ref[...] + r_ref[...]).astype(BF16)

    return pl.pallas_call(
        body, name="chip_sum", interpret=False,
        grid_spec=pltpu.PrefetchScalarGridSpec(
            num_scalar_prefetch=1, grid=(NQ, hr // tr),
            in_specs=[pl.BlockSpec((None, None, tr, cols), lambda q, i, c: (q, c[0], i, 0)),
                      pl.BlockSpec((None, tr, cols), lambda q, i, c: (q, i, 0))],
            out_specs=pl.BlockSpec((None, tr, cols), lambda q, i, c: (q, i, 0))),
        out_shape=_sds((NQ, hr, cols), BF16),
        compiler_params=_params(("arbitrary", "arbitrary"), None),
    )(core, g, rbuf)


def _chip_sum_pair(own, rbuf):
    _, hr, cols = own.shape
    tr = _row_block(hr)

    def body(a_ref, b_ref, o_ref):
        o_ref[...] = (a_ref[...] + b_ref[...]).astype(BF16)

    spec = pl.BlockSpec((None, tr, cols), lambda q, i: (q, i, 0))
    return _call(
        body, name="chip_sum_pair",
        grid=(NQ, hr // tr),
        in_specs=[spec, spec], out_specs=spec,
        out_shape=_sds((NQ, hr, cols), BF16),
        compiler_params=_params(("arbitrary", "arbitrary"), None),
    )(own, rbuf)


def _sum4(cs, rbuf, chip, after=()):
    _, hr, cols = rbuf.shape
    tr = _row_block(hr)

    def body(q_ref, c_ref, r1_ref, r2_ref, r3_ref, *rest):
        acc = c_ref[...].astype(F32)
        for r in (r1_ref, r2_ref, r3_ref):
            acc = acc + r[...].astype(F32)
        rest[-1][...] = acc

    slot = lambda k: pl.BlockSpec((None, tr, cols), lambda i, q: ((q[0] + k) % NQ, i, 0))
    return pl.pallas_call(
        body, name="sum4", interpret=False,
        grid_spec=pltpu.PrefetchScalarGridSpec(
            num_scalar_prefetch=1, grid=(hr // tr,),
            in_specs=[slot(0), slot(1), slot(2), slot(3)] + [ANY] * len(after),
            out_specs=pl.BlockSpec((tr, cols), lambda i, q: (i, 0))),
        out_shape=_sds((hr, cols), F32),
        compiler_params=_params(("arbitrary",), None),
    )(chip, cs, rbuf, rbuf, rbuf, *after)


def _adamw_halves(w, own, recv, m, v, side=None, after=()):
    rows, cols = w.shape
    hr = rows // 2
    tr = _row_block(hr, mult=8)
    nb = hr // tr

    def body(w_ref, own_ref, recv_ref, m_ref, v_ref, *rest):
        g_ref, d_ref, mo_ref, vo_ref = rest[len(after):]
        g = jnp.where(pl.program_id(0) == lax.axis_index("c"), own_ref[...], recv_ref[...])
        d, mn, vn = _adamw(w_ref[...], g, m_ref[...], v_ref[...])
        g_ref[...] = g
        d_ref[...] = d
        mo_ref[...] = mn
        vo_ref[...] = vn

    full = pl.BlockSpec((tr, cols), lambda h, i: (h * nb + i, 0))
    half = pl.BlockSpec((tr, cols), lambda h, i: (i, 0))
    step = lambda h, i: lambda: (pl.program_id(0) == h) & (pl.program_id(1) == i)
    return _side_call(
        body, side, (step(0, 0), None, step(1, nb - 1)), name="adamw_halves",
        grid=(2, nb), in_specs=[full, half, half, full, full] + [ANY] * len(after), out_specs=[full] * 4,
        out_shape=[_sds((rows, cols), F32)] * 4, scratch_shapes=[],
        compiler_params=_params(("arbitrary", "arbitrary"), None),
        args=(w, own, recv, m, v, *after))


def _cast_place(w, chip, wide=False):
    rows, cols = w.shape
    tr = _row_block(rows)

    def body(q_ref, w_ref, o_ref):
        o_ref[...] = w_ref[...].astype(BF16)

    if wide:
        out_spec = pl.BlockSpec((None, tr, cols), lambda i, q: (q[0] % 2, i, q[0] // 2))
    else:
        out_spec = pl.BlockSpec((None, tr, cols), lambda i, q: (q[0], i, 0))
    return pl.pallas_call(
        body, name="cast_place", interpret=False,
        grid_spec=pltpu.PrefetchScalarGridSpec(
            num_scalar_prefetch=1, grid=(rows // tr,),
            in_specs=[pl.BlockSpec((tr, cols), lambda i, q: (i, 0))],
            out_specs=out_spec),
        out_shape=_sds((2, rows, 2 * cols) if wide else (NQ, rows, cols), BF16),
        compiler_params=_params(("arbitrary",), None),
    )(chip, w)


def _ada_grad_adamw(cact_t, dmod_q, w, m, v, side=None):
    rows, cols = w.shape
    tc = 256
    assert cols % tc == 0

    def body(c_ref, d_ref, w_ref, m_ref, v_ref, g_ref, dl_ref, mo_ref, vo_ref):
        g = jnp.dot(c_ref[...], d_ref[...], precision=HIGHEST, preferred_element_type=F32)
        d, mn, vn = _adamw(w_ref[...], g, m_ref[...], v_ref[...])
        g_ref[...] = g
        dl_ref[...] = d
        mo_ref[...] = mn
        vo_ref[...] = vn

    spec = pl.BlockSpec((rows, tc), lambda i: (0, i))
    step = lambda s: lambda: pl.program_id(0) == s
    return _side_call(
        body, side, (step(0), None, step(cols // tc - 1)), name="ada_grad_adamw",
        grid=(cols // tc,),
        in_specs=[pl.BlockSpec((rows, 8), lambda i: (0, 0)), pl.BlockSpec((8, tc), lambda i: (0, i)),
                  spec, spec, spec],
        out_specs=[spec] * 4,
        out_shape=[_sds((rows, cols), F32)] * 4,
        scratch_shapes=[],
        compiler_params=_params(("arbitrary",), None),
        args=(cact_t, dmod_q, w, m, v))


def _me():
    x, y, c = lax.axis_index("x"), lax.axis_index("y"), lax.axis_index("c")
    return x, y, c


_OFFSETS7 = [(dx, dy, dc) for dx in (0, 1) for dy in (0, 1) for dc in (0, 1) if (dx, dy, dc) != (0, 0, 0)]
_CHIP_OFFSETS = [(1, 0), (0, 1), (1, 1)]


def _ada_fwd(c, w_ada_q, b_ada_q, side=None):
    ncol = w_ada_q.shape[1]

    def body(c_ref, w_ref, b_ref, cact_ref, modsel_ref, blk, gath, res, parts, send_sems, recv_sems, side_start=None):
        x, y, cc = _me()
        me = 4 * x + 2 * y + cc
        q = 2 * x + y
        cv = c_ref[...]
        ca = cv * jax.nn.sigmoid(cv)
        row = lax.broadcasted_iota(jnp.int32, (8, D), 0)
        blk[...] = jnp.where(row == me, jnp.broadcast_to(ca, (8, D)), 0.0)
        gath[me] = blk[...]
        sends = []
        for k, (dx, dy, dc) in enumerate(_OFFSETS7):
            cp = pltpu.make_async_remote_copy(blk, gath.at[me], send_sems.at[k], recv_sems.at[k],
                                              device_id=(x ^ dx, y ^ dy, cc ^ dc), device_id_type=MESH)
            cp.start()
            sends.append(cp)
        if side_start is not None:
            side_start()
        for cp in sends:
            cp.wait_recv()
        cact = gath[0]
        for d in range(1, N_DEV):
            cact = cact + gath[d]
        cact_ref[...] = cact
        res[...] = jnp.dot(cact, w_ref[...], precision=HIGHEST, preferred_element_type=F32) + b_ref[...]
        parts[q] = res[...]
        sends2 = []
        for k, (dx, dy) in enumerate(_CHIP_OFFSETS):
            cp = pltpu.make_async_remote_copy(res, parts.at[q], send_sems.at[7 + k], recv_sems.at[7 + k],
                                              device_id=(x ^ dx, y ^ dy, cc), device_id_type=MESH)
            cp.start()
            sends2.append(cp)
        for cp in sends2:
            cp.wait_recv()
        row2 = lax.broadcasted_iota(jnp.int32, (8, ncol), 0)
        out = jnp.zeros((8, ncol), F32)
        for s in range(NQ):
            mine = jnp.sum(jnp.where(row2 == me, parts[s], 0.0), axis=0, keepdims=True)
            out = out + jnp.where(row2 == s, jnp.broadcast_to(mine, (8, ncol)), 0.0)
        modsel_ref[...] = out
        for cp in sends + sends2:
            cp.wait_send()

    return _side_call(
        body, side, None, name="ada_fwd",
        in_specs=[VMEM, VMEM, VMEM], out_specs=[VMEM, VMEM],
        out_shape=[_sds((8, D), F32), _sds((8, ncol), F32)],
        scratch_shapes=[pltpu.VMEM((8, D), F32), pltpu.VMEM((N_DEV, 8, D), F32), pltpu.VMEM((8, ncol), F32),
                        pltpu.VMEM((NQ, 8, ncol), F32), pltpu.SemaphoreType.DMA((10,)), pltpu.SemaphoreType.DMA((10,))],
        compiler_params=_params(None, VMEM_LIMIT), start_in_body=side is not None,
        args=(c, w_ada_q, b_ada_q))


class _Side:
    def __init__(self, ins, out_shapes, aliases, nsem, start, mid=None, finish=None):
        self.ins, self.out_shapes, self.aliases, self.nsem = list(ins), list(out_shapes), dict(aliases), nsem
        self.start, self.mid, self.finish = start, mid, finish


def _join(*sides):
    ins, outs, aliases, offs, nsem = [], [], {}, [], 0
    for s in sides:
        offs.append((len(ins), len(outs), nsem))
        aliases.update({len(ins) + a: len(outs) + b for a, b in s.aliases.items()})
        ins += s.ins
        outs += s.out_shapes
        nsem += s.nsem

    def hook(name):
        def run(i, o, ss, rs, base):
            for s, (io, oo, so) in zip(sides, offs):
                fn = getattr(s, name)
                if fn is not None:
                    fn(i[io:io + len(s.ins)], o[oo:oo + len(s.out_shapes)], ss, rs, base + so)
        return run

    return _Side(ins, outs, aliases, nsem, hook("start"), hook("mid"), hook("finish"))


def _side_call(body, side, when, *, name, in_specs, out_specs, out_shape, scratch_shapes, args, aliases=None,
               start_in_body=False, **kw):
    n_in, n_out = len(in_specs), len(out_specs)
    aliases = dict(aliases or {})
    if side is None:
        return _call(body, name=name, in_specs=in_specs, out_specs=out_specs, out_shape=out_shape,
                     scratch_shapes=scratch_shapes, input_output_aliases=aliases, **kw)(*args), []
    ns_in, ns_out = len(side.ins), len(side.out_shapes)

    def hook(fn, k, operands):
        if fn is None:
            return
        if when is None:
            fn(*operands, 0)
        elif when[k] is not None:
            pl.when(when[k]())(functools.partial(fn, *operands, 0))

    def wrapped(*refs):
        ins, s_ins = refs[:n_in], refs[n_in:n_in + ns_in]
        o0 = n_in + ns_in
        outs, s_outs = refs[o0:o0 + n_out], refs[o0 + n_out:o0 + n_out + ns_out]
        rest = refs[o0 + n_out + ns_out:]
        scratch, operands = rest[:-2], (s_ins, s_outs, rest[-2], rest[-1])
        if start_in_body:
            body(*ins, *outs, *scratch, side_start=functools.partial(hook, side.start, 0, operands))
        else:
            hook(side.start, 0, operands)
            body(*ins, *outs, *scratch)
        hook(side.mid, 1, operands)
        hook(side.finish, 2, operands)

    res = _call(
        wrapped, name=name,
        in_specs=list(in_specs) + [ANY] * ns_in, out_specs=list(out_specs) + [ANY] * ns_out,
        out_shape=list(out_shape) + side.out_shapes,
        scratch_shapes=list(scratch_shapes) + [pltpu.SemaphoreType.DMA((side.nsem,)),
                                               pltpu.SemaphoreType.DMA((side.nsem,))],
        input_output_aliases={**aliases, **{n_in + a: n_out + b for a, b in side.aliases.items()}},
        **kw)(*args, *side.ins)
    return res[:n_out], res[n_out:]


def _run_side(side, name):
    return _side_call(lambda: None, side, None, name=name, in_specs=[], out_specs=[], out_shape=[],
                      scratch_shapes=[], args=[])[1]


def _remote(src, dst, ss, rs, k, dev):
    return pltpu.make_async_remote_copy(src, dst, ss.at[k], rs.at[k], device_id=dev, device_id_type=MESH)


def _gather_side(bufs):
    n = len(bufs)

    def plan(outs, w):
        x, y, cc = _me()
        hr = outs[w].shape[1] // 2
        mine, other = cc * hr, (1 - cc) * hr
        qx, qy, qd, q = 2 * (x ^ 1) + y, 2 * x + (y ^ 1), 2 * (x ^ 1) + (y ^ 1), 2 * x + y
        xn, yn, sib = (x ^ 1, y, cc), (x, y ^ 1, cc), (x, y, 1 - cc)
        if outs[w].shape[0] == NQ:
            at = lambda slot, r0, nr: outs[w].at[slot, pl.ds(r0, nr)]
        else:
            cols = outs[w].shape[2] // 2
            at = lambda slot, r0, nr: outs[w].at[slot % 2, pl.ds(r0, nr), pl.ds((slot // 2) * cols, cols)]
        send = [(at(q, mine, hr), xn), (at(q, mine, hr), yn),
                (at(qx, mine, hr // 2), yn), (at(qy, mine + hr // 2, hr // 2), xn),
                (at(qx, mine, hr), sib), (at(qy, mine, hr), sib), (at(qd, mine, hr), sib)]
        recv = [at(qx, mine, hr), at(qy, mine, hr), at(qd, mine, hr // 2), at(qd, mine + hr // 2, hr // 2),
                at(qx, other, hr), at(qy, other, hr), at(qd, other, hr)]
        return send, recv

    def op(outs, ss, rs, b, w, k, what):
        send, recv = plan(outs, w)
        if what == "wait_recv":
            _remote(recv[k], recv[k], ss, rs, b + 7 * w + k, send[k][1]).wait_recv()
        else:
            getattr(_remote(send[k][0], send[k][0], ss, rs, b + 7 * w + k, send[k][1]), what)()

    def start(ins, outs, ss, rs, b):
        for w in range(n):
            for k in (0, 1):
                op(outs, ss, rs, b, w, k, "start")

    def mid(ins, outs, ss, rs, b):
        for w in range(n):
            for k in (0, 1):
                op(outs, ss, rs, b, w, k, "wait_recv")
                op(outs, ss, rs, b, w, 2 + k, "start")
                op(outs, ss, rs, b, w, 4 + k, "start")

    def finish(ins, outs, ss, rs, b):
        for w in range(n):
            for k in (2, 3):
                op(outs, ss, rs, b, w, k, "wait_recv")
            op(outs, ss, rs, b, w, 6, "start")
        for w in range(n):
            for k in (4, 5, 6):
                op(outs, ss, rs, b, w, k, "wait_recv")
            for k in range(7):
                op(outs, ss, rs, b, w, k, "wait_send")

    return _Side(bufs, [_sds(tuple(w.shape), w.dtype) for w in bufs], {i: i for i in range(n)}, 7 * n,
                 start, mid, finish)


def _copies_side(ins, out_shapes, nsem, copies):
    def start(*a):
        for cp in copies(*a):
            cp.start()

    def finish(*a):
        for cp in copies(*a):
            cp.wait()

    return _Side(ins, out_shapes, {}, nsem, start, None, finish)


def _swap_side(gs):
    def copies(ins, outs, ss, rs, b):
        x, y, cc = _me()
        return [_remote(ins[w].at[:, 1 - cc], outs[w], ss, rs, b + w, (x, y, 1 - cc)) for w in range(len(gs))]

    return _copies_side(gs, [_sds((NQ,) + tuple(g.shape[2:]), F32) for g in gs], len(gs), copies)


def _exchange_side(cs):
    def copies(ins, outs, ss, rs, b):
        x, y, cc = _me()
        return [_remote(ins[w].at[2 * (x ^ dx) + (y ^ dy)], outs[w].at[2 * x + y], ss, rs, b + 3 * w + j,
                        (x ^ dx, y ^ dy, cc))
                for w in range(len(cs)) for j, (dx, dy) in enumerate(_CHIP_OFFSETS)]

    return _copies_side(cs, [_sds(tuple(c.shape), c.dtype) for c in cs], 3 * len(cs), copies)


def _exchange_copies(srcs, lands, send_sems, recv_sems):
    x, y, cc = _me()
    return [pltpu.make_async_remote_copy(srcs[w].at[2 * (x ^ dx) + (y ^ dy)], lands[w].at[2 * x + y],
                                         send_sems.at[3 * w + j], recv_sems.at[3 * w + j],
                                         device_id=(x ^ dx, y ^ dy, cc), device_id_type=MESH)
            for w in range(len(srcs)) for j, (dx, dy) in enumerate(_CHIP_OFFSETS)]


def _exchange_start(cs, after=()):
    n = len(cs)
    hbm, sem = pl.BlockSpec(memory_space=pltpu.HBM), pl.BlockSpec(memory_space=pltpu.SEMAPHORE)
    srcs = [pltpu.with_memory_space_constraint(c, pltpu.HBM) for c in cs]
    lands = [pltpu.with_memory_space_constraint(lax.empty(c.shape, c.dtype), pltpu.HBM) for c in cs]

    def body(*refs):
        sems = 2 * n + len(after)
        for cp in _exchange_copies(refs[:n], refs[n:2 * n], refs[sems], refs[sems + 1]):
            cp.start()
        refs[-1][...] = jnp.zeros_like(refs[-1])

    res = pl.pallas_call(
        body, name="exchange_start", interpret=False,
        out_shape=(pltpu.SemaphoreType.DMA((3 * n,)), pltpu.SemaphoreType.DMA((3 * n,)),
                   *[pltpu.HBM(c.shape, c.dtype) for c in cs], *[pltpu.HBM(c.shape, c.dtype) for c in cs],
                   _sds((8, LANE), F32)),
        in_specs=(hbm,) * (2 * n) + (ANY,) * len(after), out_specs=(sem, sem) + (hbm,) * (2 * n) + (VMEM,),
        input_output_aliases={i: 2 + i for i in range(2 * n)},
        compiler_params=pltpu.CompilerParams(has_side_effects=pltpu.SideEffectType.DATAFLOW_SIDE_EFFECTING),
    )(*srcs, *lands, *after)
    return res[0], res[1], list(res[2:2 + n]), list(res[2 + n:2 + 2 * n]), res[-1]


def _exchange_wait(send_sems, recv_sems, srcs, lands, after):
    n = len(srcs)
    hbm, sem = pl.BlockSpec(memory_space=pltpu.HBM), pl.BlockSpec(memory_space=pltpu.SEMAPHORE)

    def body(*refs):
        for cp in _exchange_copies(refs[:n], refs[n:2 * n], refs[2 * n], refs[2 * n + 1]):
            cp.wait_send()
            cp.wait_recv()

    res = pl.pallas_call(
        body, name="exchange_wait", interpret=False,
        out_shape=[pltpu.HBM(c.shape, c.dtype) for c in srcs + lands],
        in_specs=(hbm,) * (2 * n) + (sem, sem) + (ANY,) * len(after), out_specs=(hbm,) * (2 * n),
        input_output_aliases={i: i for i in range(2 * n)},
        compiler_params=pltpu.CompilerParams(has_side_effects=pltpu.SideEffectType.DATAFLOW_SIDE_EFFECTING),
    )(*srcs, *lands, send_sems, recv_sems, *after)
    return list(res[:n]), list(res[n:])


def _share_side(fs):
    def copies(ins, outs, ss, rs, b):
        x, y, cc = _me()
        return [_remote(ins[w], outs[w], ss, rs, b + w, (x, y, 1 - cc)) for w in range(len(fs))]

    return _copies_side(fs, [_sds(tuple(f.shape), F32) for f in fs], len(fs), copies)


def _small_allreduce_adamw(g, w, m, v, nd):
    rows = g.shape[0]
    nr = rows - nd
    hr = nr // 2
    assert nd % 8 == 0 and hr % 8 == 0

    def body(g_ref, w_ref, m_ref, v_ref, gs_ref, d_ref, mo_ref, vo_ref, gath, sib, csum, slots, tot, ss, rs):
        x, y, cc = _me()
        me = 4 * x + 2 * y + cc
        q = 2 * x + y
        sibling = (x, y, 1 - cc)
        dm = g_ref.at[pl.ds(0, nd)]
        gath[me] = g_ref[0:nd, :]
        to_all = [_remote(dm, gath.at[me], ss, rs, k, (x ^ dx, y ^ dy, cc ^ dc)) for k, (dx, dy, dc) in enumerate(_OFFSETS7)]
        to_sib = _remote(g_ref.at[pl.ds(nd, nr)], sib, ss, rs, 7, sibling)
        for cp in to_all + [to_sib]:
            cp.start()
        to_sib.wait_recv()
        csum[...] = g_ref[nd:, :] + sib[...]
        mine = pl.ds(pl.multiple_of(cc * hr, 8), hr)
        slots[q] = csum[mine, :]
        to_chips = [_remote(csum.at[mine], slots.at[q], ss, rs, 8 + j, (x ^ dx, y ^ dy, cc))
                    for j, (dx, dy) in enumerate(_CHIP_OFFSETS)]
        for cp in to_chips:
            cp.start()
        for cp in to_chips:
            cp.wait_recv()
        tot[mine, :] = (slots[0] + slots[1]) + (slots[2] + slots[3])
        halves = _remote(tot.at[mine], tot.at[mine], ss, rs, 11, sibling)
        halves.start()
        for cp in to_all:
            cp.wait_recv()
        dsum = gath[0]
        for dev in range(1, N_DEV):
            dsum = dsum + gath[dev]
        halves.wait_recv()
        for lo, n, total in ((0, nd, dsum), (nd, nr, tot[...])):
            gs_ref[lo:lo + n, :] = total
            d, mn, vn = _adamw(w_ref[lo:lo + n, :], total, m_ref[lo:lo + n, :], v_ref[lo:lo + n, :])
            d_ref[lo:lo + n, :] = d
            mo_ref[lo:lo + n, :] = mn
            vo_ref[lo:lo + n, :] = vn
        for cp in to_all + [to_sib, halves] + to_chips:
            cp.wait_send()

    return _call(
        body, name="small_allreduce_adamw",
        in_specs=[VMEM] * 4, out_specs=[VMEM] * 5,
        out_shape=[_sds((rows, LANE), F32)] * 4 + [_sds((N_DEV, nd, LANE), F32)],
        scratch_shapes=[pltpu.VMEM((nr, LANE), F32), pltpu.VMEM((nr, LANE), F32), pltpu.VMEM((NQ, hr, LANE), F32),
                        pltpu.VMEM((nr, LANE), F32), pltpu.SemaphoreType.DMA((12,)), pltpu.SemaphoreType.DMA((12,))],
        compiler_params=_params(None, VMEM_LIMIT),
    )(g, w, m, v)


_SMALL = ["b_ada", "norm_ffn1_g", "norm_mix_g", "pool_scale", "gmlp_ln_g", "gmlp_ln_b", "b_spatial",
          "norm_ffn2_g", "norm_final_g", "w_pool", "w_spatial"]


def _pack(parts):
    blocks, layout, r0 = [], {}, 0
    for name in _SMALL:
        a = parts[name]
        n = a.size
        rows = -(-n // LANE)
        rows8 = -(-rows // 8) * 8
        flat = a.reshape(-1).astype(F32)
        if rows8 * LANE != n:
            flat = jnp.concatenate([flat, jnp.zeros((rows8 * LANE - n,), F32)])
        blocks.append(flat.reshape(rows8, LANE))
        layout[name] = (r0, n, a.shape)
        r0 += rows8
    return jnp.concatenate(blocks, axis=0), layout


def _unpack(packed, layout):
    out = {}
    for name, (r0, n, shape) in layout.items():
        rows = -(-n // LANE)
        out[name] = packed[r0:r0 + rows].reshape(-1)[:n].reshape(shape)
    return out


def _modv(mod9, sub, gain):
    rows = jnp.concatenate([mod9[3 * sub:3 * sub + 3], gain.reshape(1, D), jnp.zeros((4, D), F32)], axis=0)
    return rows


_BIG = ["ffn1_w_in", "ffn1_w_out", "w_mix_in", "w_mix_out", "ffn2_w_in", "ffn2_w_out"]


def kernel(x, c, w_ada, b_ada, norm_ffn1_g, ffn1_w_in, ffn1_w_out, norm_mix_g, w_mix_in, w_pool, pool_scale, gmlp_ln_g, gmlp_ln_b, w_spatial, b_spatial, w_mix_out, norm_ffn2_g, ffn2_w_in, ffn2_w_out, norm_final_g, loss_target, m_w_ada, m_b_ada, m_norm_ffn1_g, m_ffn1_w_in, m_ffn1_w_out, m_norm_mix_g, m_w_mix_in, m_w_pool, m_pool_scale, m_gmlp_ln_g, m_gmlp_ln_b, m_w_spatial, m_b_spatial, m_w_mix_out, m_norm_ffn2_g, m_ffn2_w_in, m_ffn2_w_out, m_norm_final_g, v_w_ada, v_b_ada, v_norm_ffn1_g, v_ffn1_w_in, v_ffn1_w_out, v_norm_mix_g, v_w_mix_in, v_w_pool, v_pool_scale, v_gmlp_ln_g, v_gmlp_ln_b, v_w_spatial, v_b_spatial, v_w_mix_out, v_norm_ffn2_g, v_ffn2_w_in, v_ffn2_w_out, v_norm_final_g):
    names = ["w_ada", "b_ada", "norm_ffn1_g", "ffn1_w_in", "ffn1_w_out", "norm_mix_g", "w_mix_in", "w_pool",
             "pool_scale", "gmlp_ln_g", "gmlp_ln_b", "w_spatial", "b_spatial", "w_mix_out", "norm_ffn2_g",
             "ffn2_w_in", "ffn2_w_out", "norm_final_g"]
    W = dict(zip(names, [w_ada, b_ada, norm_ffn1_g, ffn1_w_in, ffn1_w_out, norm_mix_g, w_mix_in, w_pool, pool_scale,
                         gmlp_ln_g, gmlp_ln_b, w_spatial, b_spatial, w_mix_out, norm_ffn2_g, ffn2_w_in, ffn2_w_out,
                         norm_final_g]))
    M = dict(zip(names, [m_w_ada, m_b_ada, m_norm_ffn1_g, m_ffn1_w_in, m_ffn1_w_out, m_norm_mix_g, m_w_mix_in, m_w_pool,
                         m_pool_scale, m_gmlp_ln_g, m_gmlp_ln_b, m_w_spatial, m_b_spatial, m_w_mix_out, m_norm_ffn2_g,
                         m_ffn2_w_in, m_ffn2_w_out, m_norm_final_g]))
    V = dict(zip(names, [v_w_ada, v_b_ada, v_norm_ffn1_g, v_ffn1_w_in, v_ffn1_w_out, v_norm_mix_g, v_w_mix_in, v_w_pool,
                         v_pool_scale, v_gmlp_ln_g, v_gmlp_ln_b, v_w_spatial, v_b_spatial, v_w_mix_out, v_norm_ffn2_g,
                         v_ffn2_w_in, v_ffn2_w_out, v_norm_final_g]))

    xi, yi, ci = _me()
    q = 2 * xi + yi
    core = ci.astype(jnp.int32).reshape(1)

    chip = q.astype(jnp.int32).reshape(1)
    place = lambda n: _cast_place(W[n][0], chip, wide=n in ("ffn1_w_in", "ffn2_w_in"))

    ncol = w_ada.shape[2]
    b_q = lax.dynamic_slice_in_dim(b_ada, q * ncol, ncol, axis=1)
    (cact_all, modsel), (win1, wout1) = _ada_fwd(
        c, w_ada[0], b_q, side=_gather_side([place("ffn1_w_in"), place("ffn1_w_out")]))
    mod9 = modsel[:NQ].reshape(9, D)
    xs, target = x[0], loss_target[0]
    mv1 = _modv(mod9, 0, norm_ffn1_g[0])
    mv2 = _modv(mod9, 1, norm_mix_g[0])
    mv3 = _modv(mod9, 2, norm_ffn2_g[0])
    wcat, wtcat, bias = _prep_spatial(w_spatial[0], b_spatial[0].T)
    wpool = w_pool[0].astype(BF16)
    vecs = jnp.concatenate([pool_scale, gmlp_ln_g, gmlp_ln_b, jnp.zeros((5, DP), F32)], axis=0)
    gf = jnp.concatenate([norm_final_g.reshape(1, D), jnp.zeros((7, D), F32)], axis=0)

    later =["w_mix_in", "w_mix_out", "ffn2_w_in", "ffn2_w_out"]
    (x1, g1s, u1s), got = _ffn_fwd(xs, mv1, win1, wout1.reshape(2, CH, D), side=_gather_side([place(n) for n in later]))
    wmi, wmo, win2, wout2 = got
    wmi = jnp.transpose(wmi, (1, 0, 2)).reshape(D, DPROJ)
    wmo = wmo.reshape(DP + DG, D)
    x2, pooled, zb = _mix_fwd(x1, mv2, wmi, wpool, vecs, wcat, bias, wmo)
    (dx3, g3s, u3s, loss_blk, dgf), _ = _ffn_fwd(x2, mv3, win2, wout2.reshape(2, CH, D), head=(target, gf))

    wo1, wo2 = wout1.reshape(2, CH, D), wout2.reshape(2, CH, D)
    (dx2, oin2, oout2, rin2, rout2, vec3), _ = _ffn_bwd(x2, dx3, g3s, u3s, mv3, win2, wo2)
    cs2 = [_chip_sum_pair(oin2, rin2), _chip_sum_pair(oout2, rout2)]
    (dx1, dwmi, dwmo, dwpool, dwsp, dbsp, v512, vec2), ex2 = _mix_bwd(
        x1, dx2, pooled, zb, mv2, wmi, wpool, vecs, wcat, wtcat, bias, wmo, side=_exchange_side(cs2))
    half2 = [_sum4(cs, e, chip) for cs, e in zip(cs2, ex2)]
    qcols = w_mix_in.shape[2]
    vmix = [dwmi.reshape(NQ, 2, D // 2, qcols), dwmo.reshape(NQ, 2, (DP + DG) // 8, D)]
    first1, got = _ffn_bwd_pass(0, xs, dx1, g1s, u1s, mv1, win1, wo1,
                                side=_join(_swap_side(vmix), _share_side(half2)))
    sibmix, other2 = got[:2], got[2:]
    cs_mix = [_chip_sum(g, r, core) for g, r in zip(vmix, sibmix)]
    (grad_x, oin1, oout1, rin1, rout1, vec1), ex_mix = _ffn_bwd_pass(
        1, xs, dx1, g1s, u1s, mv1, win1, wo1, prev=first1[:5], side=_exchange_side(cs_mix))
    vec1 = first1[5] + vec1
    cs_ffn1 = [_chip_sum_pair(oin1, rin1), _chip_sum_pair(oout1, rout1)]

    dmod =jnp.concatenate([vec1[0:3], vec2[0:3], vec3[0:3]], axis=0)
    grads = dict(
        b_ada=dmod.reshape(1, 9 * D), norm_ffn1_g=vec1[3:4], norm_mix_g=vec2[3:4], norm_ffn2_g=vec3[3:4],
        pool_scale=v512[0:1], gmlp_ln_g=v512[1:2], gmlp_ln_b=v512[2:3], b_spatial=dbsp[None],
        norm_final_g=dgf[0], w_pool=dwpool[None], w_spatial=dwsp[None])

    gp, layout = _pack({n: grads[n] for n in _SMALL})
    gp = jnp.concatenate([gp, loss_blk, loss_blk], axis=0)
    pad = jnp.zeros((16, LANE), F32)
    wp, mp, vp = [jnp.concatenate([_pack({n: src[n] for n in _SMALL})[0], pad], axis=0) for src in (W, M, V)]
    r0, nb, _ = layout["b_ada"]
    assert r0 == 0
    out_g, out_d, out_m, out_v = {}, {}, {}, {}
    gs, dl, mo, vo, gath = _small_allreduce_adamw(gp, wp, mp, vp, nb // LANE)
    loss = gs[-16, 0]
    for packed, dst in ((gs, out_g), (dl, out_d), (mo, out_m), (vo, out_v)):
        for n, a in _unpack(packed, layout).items():
            dst[n] = a.reshape(W[n].shape)

    def update(n, own, recv, after=()):
        (g2, d, mn, vn), _ = _adamw_halves(W[n][0], own, recv, M[n][0], V[n][0], after=after)
        out_g[n], out_d[n], out_m[n], out_v[n] = g2[None], d[None], mn[None], vn[None]
        return g2

    ssem, rsem, cs_fly, land_fly, token = _exchange_start(cs_ffn1, after=(gath,))
    dmod_q = lax.dynamic_slice_in_dim(gath.reshape(N_DEV, nb), q * ncol, ncol, axis=1) + token[0:8, 0:1]
    (ga, da, ma, va), _ = _ada_grad_adamw(cact_all.T, dmod_q, w_ada[0], m_w_ada[0], v_w_ada[0])
    out_g["w_ada"], out_d["w_ada"], out_m["w_ada"], out_v["w_ada"] = ga[None], da[None], ma[None], va[None]
    done = [ga, update("ffn2_w_in", half2[0], other2[0], after=(token,)),
            update("ffn2_w_out", half2[1], other2[1], after=(token,))]
    half_mix = [_sum4(cs, e, chip, after=(token,)) for cs, e in zip(cs_mix, ex_mix)]
    other_mix = _run_side(_share_side(half_mix), "share_mix")
    done += [update(n, own, recv) for n, own, recv in zip(["w_mix_in", "w_mix_out"], half_mix, other_mix)]
    cs_ffn1, ex_ffn1 = _exchange_wait(ssem, rsem, cs_fly, land_fly, after=done)
    half1 = [_sum4(cs, e, chip) for cs, e in zip(cs_ffn1, ex_ffn1)]
    other1 = _run_side(_share_side(half1), "share_ffn1")
    for n, own, recv in zip(["ffn1_w_in", "ffn1_w_out"], half1, other1):
        update(n, own, recv)

    return (loss, grad_x[None], *[out_g[n] for n in names], *[out_d[n] for n in names],
            *[out_m[n] for n in names], *[out_v[n] for n in names])
```

```python
import functools
import math

import jax
import jax.numpy as jnp
from jax import lax
from jax.experimental import pallas as pl
from jax.experimental.pallas import tpu as pltpu

F32 = jnp.float32
BF16 = jnp.bfloat16
MESH = pl.DeviceIdType.MESH
HIGHEST = lax.Precision.HIGHEST

EPS = 1e-6
D = 1024
DFF = 2816
CH = DFF // 2
NQ = 4
DP = 512
DG = 512
DPROJ = DP + 2 * DG
POOL_WINDOWS = (2, 4, 8, 16)
HALO = 16
CHUNK = 128
LANE = 128
N_DEV = 8

ADAM_LR = 0.001
ADAM_B1 = 0.9
ADAM_B2 = 0.999
ADAM_EPS = 1e-08
ADAM_WD = 0.01
ADAM_STEP = 10

VMEM_LIMIT = 62 * 1024 * 1024

TM_FFN_FWD = 512
TM_FFN_BWD = 512
TM_MIX = 512


def _call(body, **kw):
    return pl.pallas_call(body, interpret=False, **kw)


def _params(sem=None, vmem=None):
    return pltpu.CompilerParams(dimension_semantics=sem, vmem_limit_bytes=vmem)


def _sds(shape, dtype):
    return jax.ShapeDtypeStruct(shape, dtype)


ANY = pl.BlockSpec(memory_space=pl.ANY)
VMEM = pl.BlockSpec(memory_space=pltpu.VMEM)
SMEM = pl.BlockSpec(memory_space=pltpu.SMEM)


def _norm_mod(x, gn, sc, sh):
    r = lax.rsqrt(jnp.mean(x * x, axis=-1, keepdims=True) + EPS)
    xn = x * r
    hp = xn * gn
    return r, xn, hp, hp * (1.0 + sc) + sh


def _norm_mod_bwd(dh, r, xn, hp, gn, sc):
    one_sc = 1.0 + sc
    dsh = jnp.sum(dh, axis=0, keepdims=True)
    dsc = jnp.sum(dh * hp, axis=0, keepdims=True)
    dgn = jnp.sum(dh * one_sc * xn, axis=0, keepdims=True)
    dxn = dh * (gn * one_sc)
    dx = r * (dxn - xn * jnp.mean(dxn * xn, axis=-1, keepdims=True))
    return dsh, dsc, dgn, dx


def _dot(a, b):
    return jnp.dot(a, b, preferred_element_type=F32)


def _dot_nt(a, b):
    return lax.dot_general(a, b, (((1,), (1,)), ((), ())), preferred_element_type=F32)


def _dot_tn(a, b):
    return lax.dot_general(a, b, (((0,), (0,)), ((), ())), preferred_element_type=F32)


_GELU_C = math.sqrt(2.0 / math.pi)
_GELU_A = 0.044715


def _gelu_fwd_bwd(x):
    x2 = x * x
    t = jnp.tanh(_GELU_C * (x + _GELU_A * x * x2))
    g = 0.5 * x * (1.0 + t)
    dg = 0.5 * (1.0 + t) + 0.5 * x * (1.0 - t * t) * (_GELU_C * (1.0 + 3.0 * _GELU_A * x2))
    return g, dg


def _adamw(w, g, m, v):
    m = ADAM_B1 * m + (1.0 - ADAM_B1) * g
    v = ADAM_B2 * v + (1.0 - ADAM_B2) * (g * g)
    m_hat = m / (1.0 - ADAM_B1 ** ADAM_STEP)
    v_hat = v / (1.0 - ADAM_B2 ** ADAM_STEP)
    delta = -ADAM_LR * (m_hat / (jnp.sqrt(v_hat) + ADAM_EPS) + ADAM_WD * w)
    return delta, m, v


def _row_block(rows, cap=256, mult=16):
    best = None
    for t in range(mult, min(rows, cap) + 1, mult):
        if rows % t == 0:
            best = t
    assert best is not None, rows
    return best


def _head_math(x, target, gf):
    r = lax.rsqrt(jnp.mean(x * x, axis=-1, keepdims=True) + EPS)
    xn = x * r
    err = xn * gf - target
    dy = err * (1.0 / D)
    dxn = dy * gf
    dx = r * (dxn - xn * jnp.mean(dxn * xn, axis=-1, keepdims=True))
    return (0.5 / D) * jnp.sum(err * err), jnp.sum(dy * xn, axis=0, keepdims=True), dx


def _ffn_fwd(x, modv, win, wout, side=None, head=None):
    S = x.shape[0]
    tm = TM_FFN_FWD
    nt = S // tm

    def body(*refs):
        if head is None:
            x_ref, mod_ref, win_hbm, wout_hbm, xo_ref, gs_ref, us_ref, acc_scr, wgu, wo, wsem = refs
        else:
            (x_ref, mod_ref, win_hbm, wout_hbm, t_ref, gf_ref,
             xo_ref, gs_ref, us_ref, loss_ref, dgf_ref, acc_scr, wgu, wo, wsem) = refs

        @pl.when((pl.program_id(0) == 0) & (pl.program_id(1) == 0))
        def _():
            loads = [pltpu.make_async_copy(win_hbm, wgu, wsem.at[0]), pltpu.make_async_copy(wout_hbm, wo, wsem.at[1])]
            for cp in loads:
                cp.start()
            for cp in loads:
                cp.wait()
            acc_scr[...] = jnp.zeros_like(acc_scr)
            if head is not None:
                loss_ref[...] = jnp.zeros_like(loss_ref)
                dgf_ref[...] = jnp.zeros_like(dgf_ref)

        j = pl.program_id(1)
        h = _norm_mod(x_ref[...], mod_ref[3:4, :], mod_ref[1:2, :], mod_ref[0:1, :])[3].astype(BF16)
        gu = _dot(h, wgu[j])
        g = gu[:, :CH].astype(BF16)
        u = gu[:, CH:].astype(BF16)
        gs_ref[...] = g
        us_ref[...] = u
        gf = g.astype(F32)
        a = (gf * jax.nn.sigmoid(gf) * u.astype(F32)).astype(BF16)
        acc = jnp.where(j == 0, 0.0, acc_scr[...]) + _dot(a, wo[j])
        acc_scr[...] = acc
        xo = x_ref[...] + (0.5 * mod_ref[2:3, :]) * acc
        if head is None:
            xo_ref[...] = xo
        else:
            @pl.when(j == 1)
            def _():
                loss, dgf, dx = _head_math(xo, t_ref[...], gf_ref[0:1, :])
                loss_ref[...] += loss
                dgf_ref[0:1, :] += dgf
                xo_ref[...] = dx

    step = lambda i, j: lambda: (pl.program_id(0) == i) & (pl.program_id(1) == j)
    tile = pl.BlockSpec((tm, D), lambda i, j: (i, 0))
    const = lambda shape: pl.BlockSpec(shape, lambda i, j: (0, 0))
    chunk = pl.BlockSpec((tm, CH), lambda i, j: (i, j))
    in_specs = [tile, const((8, D)), ANY, ANY]
    out_specs = [tile, chunk, chunk]
    out_shape = [_sds((S, D), F32), _sds((S, DFF), BF16), _sds((S, DFF), BF16)]
    args = (x, modv, win, wout)
    if head is not None:
        in_specs += [tile, const((8, D))]
        out_specs += [const((8, LANE)), const((8, D))]
        out_shape += [_sds((8, LANE), F32), _sds((8, D), F32)]
        args += tuple(head)
    return _side_call(
        body, side, (step(0, 0), step((7 * nt) // 10, 0), step(nt - 1, 1)), name="ffn_fwd",
        grid=(nt, 2), in_specs=in_specs, out_specs=out_specs, out_shape=out_shape,
        scratch_shapes=[pltpu.VMEM((tm, D), F32), pltpu.VMEM((2, D, 2 * CH), BF16), pltpu.VMEM((2, CH, D), BF16),
                        pltpu.SemaphoreType.DMA((2,))],
        compiler_params=_params(("arbitrary", "arbitrary"), VMEM_LIMIT),
        args=args)


def _ffn_bwd_pass(jj, x, dxo, gs, us, modv, win, wout, prev=None, side=None):
    S = x.shape[0]
    tm = TM_FFN_BWD
    nt = S // tm
    hi, ho = D // 2, CH // 4
    last = prev is not None
    assert last == (jj == 1)

    def body(*refs):
        x_ref, dxo_ref, gs_ref, us_ref, mod_ref, win_hbm, wo_ref = refs[:7]
        k = 12 if last else 7
        out_ref, dwin_ref, dwout_ref, rwin_ref, rwout_ref, vec_ref = refs[k:k + 6]
        accgu, accw, wgu, sems, fsend, frecv = refs[k + 6:]
        i = pl.program_id(0)

        @pl.when(i == 0)
        def _():
            load = pltpu.make_async_copy(win_hbm.at[jj], wgu, sems.at[0])
            load.start()
            accgu[...] = jnp.zeros_like(accgu)
            accw[...] = jnp.zeros_like(accw)
            vec_ref[...] = jnp.zeros_like(vec_ref)
            load.wait()

        gn, sc, sh, gate = mod_ref[3:4, :], mod_ref[1:2, :], mod_ref[0:1, :], mod_ref[2:3, :]

        r, xn, hp, h = _norm_mod(x_ref[...], gn, sc, sh)
        dxo = dxo_ref[...]
        dy = (dxo * (0.5 * gate)).astype(BF16)
        g = gs_ref[...].astype(F32)
        u = us_ref[...].astype(F32)
        sig = jax.nn.sigmoid(g)
        sl = g * sig
        a = (sl * u).astype(BF16)
        da = _dot_nt(dy, wo_ref[...])
        dg = (da * u * (sig * (1.0 + g * (1.0 - sig)))).astype(BF16)
        du = (da * sl).astype(BF16)
        dgu = jnp.concatenate([dg, du], axis=1)
        dhp = _dot_nt(dgu, wgu[...])
        if last:
            dsh, dsc, dgn, dxin = _norm_mod_bwd(refs[7][...] + dhp, r, xn, hp, gn, sc)
            vec_ref[0:1, :] += dsh
            vec_ref[1:2, :] += dsc
            vec_ref[3:4, :] += dgn
            out_ref[...] = dxo + dxin
        else:
            out_ref[...] = dhp
        accw[...] += _dot_tn(a, dxo.astype(BF16))
        accgu[...] += _dot_tn(h.astype(BF16), dgu)

        @pl.when(i == nt - 1)
        def _():
            gw = accw[...]
            vec_ref[2:3, :] += 0.5 * jnp.sum(wo_ref[...].astype(F32) * gw, axis=0, keepdims=True)
            accw[...] = gw * (0.5 * gate)
            mx, my, cc = _me()
            part = lambda acc, base, n, c, col: acc.at[pl.ds(base + c * n, n), pl.ds(col[0], col[1])]
            pieces = [(accgu, 0, hi, (0, CH), dwin_ref, rwin_ref, jj), (accgu, 0, hi, (CH, CH), dwin_ref, rwin_ref, 2 + jj),
                      (accw, 0, ho, (0, D), dwout_ref, rwout_ref, 2 * jj),
                      (accw, 2 * ho, ho, (0, D), dwout_ref, rwout_ref, 2 * jj + 1)]
            loc = [pltpu.make_async_copy(part(acc, base, n, cc, col), own.at[slot], sems.at[p])
                   for p, (acc, base, n, col, own, _, slot) in enumerate(pieces)]
            rem = [pltpu.make_async_remote_copy(part(acc, base, n, 1 - cc, col), sib.at[slot], fsend.at[p], frecv.at[p],
                                                device_id=(mx, my, 1 - cc), device_id_type=MESH)
                   for p, (acc, base, n, col, _, sib, slot) in enumerate(pieces)]
            for cp in loc + rem:
                cp.start()
            for cp in loc:
                cp.wait()
            for cp in rem:
                cp.wait()

    once = pl.Buffered(1)
    tile = pl.BlockSpec((tm, D), lambda i: (i, 0))
    chunk = pl.BlockSpec((tm, CH), lambda i: (i, jj))
    in_specs = [tile, tile, chunk, chunk, pl.BlockSpec((8, D), lambda i: (0, 0)), ANY,
                pl.BlockSpec((None, CH, D), lambda i: (jj, 0, 0), pipeline_mode=once)]
    args = (x, dxo, gs, us, modv, win, wout)
    if last:
        in_specs += [tile, ANY, ANY, ANY, ANY]
        args += tuple(prev)
    step = lambda s: lambda: pl.program_id(0) == s
    return _side_call(
        body, side, (step(0), None, step(nt - 1)), name="ffn_bwd",
        grid=(nt,), in_specs=in_specs,
        out_specs=[tile, ANY, ANY, ANY, ANY, pl.BlockSpec((8, D), lambda i: (0, 0))],
        out_shape=[_sds((S, D), F32), _sds((NQ, hi, CH), F32), _sds((NQ, ho, D), F32), _sds((NQ, hi, CH), F32),
                   _sds((NQ, ho, D), F32), _sds((8, D), F32)],
        scratch_shapes=[pltpu.VMEM((D, 2 * CH), F32), pltpu.VMEM((CH, D), F32), pltpu.VMEM((D, 2 * CH), BF16),
                        pltpu.SemaphoreType.DMA((4,)), pltpu.SemaphoreType.DMA((4,)), pltpu.SemaphoreType.DMA((4,))],
        aliases={8 + p: 1 + p for p in range(4)} if last else {},
        compiler_params=_params(("arbitrary",), VMEM_LIMIT),
        args=args)


def _ffn_bwd(x, dxo, gs, us, modv, win, wout, side=None):
    first, extra = _ffn_bwd_pass(0, x, dxo, gs, us, modv, win, wout, side=side)
    (dx, dwin, dwout, rwin, rwout, vec), _ = _ffn_bwd_pass(1, x, dxo, gs, us, modv, win, wout, prev=first[:5])
    return (dx, dwin, dwout, rwin, rwout, first[5] + vec), extra


def _prep_spatial(w_spatial, b_spatial_t):
    def body(w_ref, b_ref, wcat_ref, wtcat_ref, bias_ref):
        row = lax.broadcasted_iota(jnp.int32, (CHUNK, CHUNK), 0)
        col = lax.broadcasted_iota(jnp.int32, (CHUNK, CHUNK), 1)
        tril = col <= row
        for p in range(4):
            wa = jnp.where(tril, w_ref[2 * p], 0.0)
            wb = jnp.where(tril, w_ref[2 * p + 1], 0.0)
            wcat_ref[p] = jnp.concatenate([wa, wb], axis=1).astype(BF16)
            wtcat_ref[p] = jnp.concatenate([wa.T, wb.T], axis=1).astype(BF16)
        head = lax.broadcasted_iota(jnp.int32, (8, DG), 0)
        ch = lax.broadcasted_iota(jnp.int32, (8, DG), 1)
        spread = jnp.where(ch // 64 == head, 1.0, 0.0).astype(F32)
        bias_ref[...] = jnp.dot(b_ref[...], spread, precision=HIGHEST, preferred_element_type=F32)

    return _call(
        body, name="prep_spatial",
        in_specs=[VMEM, VMEM], out_specs=[VMEM, VMEM, VMEM],
        out_shape=[_sds((4, CHUNK, 2 * CHUNK), BF16), _sds((4, CHUNK, 2 * CHUNK), BF16), _sds((CHUNK, DG), F32)],
    )(w_spatial, b_spatial_t)


def _pair_rhs(blocks):
    lane = lax.broadcasted_iota(jnp.int32, (CHUNK, LANE), 1)
    lo = lane < 64
    top = jnp.concatenate([jnp.where(lo, b, 0.0) for b in blocks], axis=1)
    bot = jnp.concatenate([jnp.where(lo, 0.0, b) for b in blocks], axis=1)
    return top, bot


def _gmlp_branch(zb, vecs, wcat_ref, bias_ref, nchunks):
    z, dz = _gelu_fwd_bwd(zb)
    u = z[:, :DG]
    v = z[:, DG:]
    ln_g, ln_b = vecs[1:2, :], vecs[2:3, :]
    mu = jnp.mean(v, axis=-1, keepdims=True)
    vc = v - mu
    rstd = lax.rsqrt(jnp.mean(vc * vc, axis=-1, keepdims=True) + EPS)
    vhat = vc * rstd
    vl = vhat * ln_g + ln_b
    sv_cols = []
    for p in range(4):
        blocks = [vl[k * CHUNK:(k + 1) * CHUNK, p * LANE:(p + 1) * LANE] for k in range(nchunks)]
        top, bot = _pair_rhs(blocks)
        rhs = jnp.concatenate([top, bot], axis=0).astype(BF16)
        out = _dot(wcat_ref[p], rhs)
        bias = bias_ref[:, p * LANE:(p + 1) * LANE]
        sv_cols.append(jnp.concatenate([out[:, k * LANE:(k + 1) * LANE] + bias for k in range(nchunks)], axis=0))
    sv = jnp.concatenate(sv_cols, axis=1)
    return dict(u=u, dz=dz, rstd=rstd, vhat=vhat, vl=vl, sv=sv, yb=u * sv)


def _mix_fwd(x, modv, win, wpool, vecs, wcat, bias, wout):
    S = x.shape[0]
    tm = TM_MIX
    nt = S // tm
    nchunks = tm // CHUNK

    def body(x_ref, mod_ref, win_ref, wpool_ref, vec_ref, wcat_ref, bias_ref, wout_ref,
             xo_ref, pooled_ref, zb_ref, ext):
        i = pl.program_id(0)

        @pl.when(i == 0)
        def _():
            ext[0:HALO, :] = jnp.zeros((HALO, DP), F32)

        x = x_ref[...]
        _, _, _, h = _norm_mod(x, mod_ref[3:4, :], mod_ref[1:2, :], mod_ref[0:1, :])
        proj = _dot(h.astype(BF16), win_ref[...])
        xa = proj[:, :DP]
        zb = proj[:, DP:]
        zb_ref[...] = zb
        ext[HALO:HALO + tm, :] = xa
        pos = i * tm + lax.broadcasted_iota(jnp.int32, (tm, 1), 0)
        vecs = vec_ref[...]
        ya_cols = []
        pooled_cols = []
        for gi, w in enumerate(POOL_WINDOWS):
            cols = slice(gi * LANE, (gi + 1) * LANE)
            s = xa[:, cols]
            for k in range(1, w):
                s = s + ext[HALO - k:HALO - k + tm, cols]
            cnt = jnp.minimum(pos + 1, w).astype(F32)
            pooled = (s / cnt - xa[:, cols]).astype(BF16)
            pooled_cols.append(pooled)
            ya_cols.append(_dot(pooled, wpool_ref[gi]) * vecs[0:1, cols])
        pooled_ref[...] = jnp.concatenate(pooled_cols, axis=1)
        ext[0:HALO, :] = ext[tm:tm + HALO, :]

        gm = _gmlp_branch(zb, vecs, wcat_ref, bias_ref, nchunks)
        cat = jnp.concatenate(ya_cols + [gm["yb"]], axis=1).astype(BF16)
        xo_ref[...] = x + mod_ref[2:3, :] * _dot(cat, wout_ref[...])

    full = lambda shape: pl.BlockSpec(shape, lambda i: (0,) * len(shape))
    return _call(
        body, name="mix_fwd",
        grid=(nt,),
        in_specs=[pl.BlockSpec((tm, D), lambda i: (i, 0)), full((8, D)), full((D, DPROJ)),
                  full((4, LANE, LANE)), full((8, DP)), full((4, CHUNK, 2 * CHUNK)), full((CHUNK, DG)),
                  full((DP + DG, D))],
        out_specs=[pl.BlockSpec((tm, D), lambda i: (i, 0)), pl.BlockSpec((tm, DP), lambda i: (i, 0)),
                   pl.BlockSpec((tm, 2 * DG), lambda i: (i, 0))],
        out_shape=[_sds((S, D), F32), _sds((S, DP), BF16), _sds((S, 2 * DG), F32)],
        scratch_shapes=[pltpu.VMEM((tm + HALO, DP), F32)],
        compiler_params=_params(("arbitrary",), VMEM_LIMIT),
    )(x, modv, win, wpool, vecs, wcat, bias, wout)


def _mix_bwd(x, dxo, pooled, zb, modv, win, wpool, vecs, wcat, wtcat, bias, wout, side=None):
    S = x.shape[0]
    tm = TM_MIX
    nt = S // tm
    nchunks = tm // CHUNK

    def body(x_ref, dxo_ref, pooled_ref, zb_ref, mod_ref, win_ref, wpool_ref, vec_ref, wcat_ref, wtcat_ref,
             bias_ref, wout_ref,
             dx_ref, dwin_ref, dwout_ref, dwpool_ref, dwsp_ref, dbsp_ref, v512_ref, vd_ref, qext, dsv_acc):
        step = pl.program_id(0)
        tile = nt - 1 - step

        @pl.when(step == 0)
        def _():
            dwin_ref[...] = jnp.zeros_like(dwin_ref)
            dwout_ref[...] = jnp.zeros_like(dwout_ref)
            dwpool_ref[...] = jnp.zeros_like(dwpool_ref)
            dwsp_ref[...] = jnp.zeros_like(dwsp_ref)
            v512_ref[...] = jnp.zeros_like(v512_ref)
            vd_ref[...] = jnp.zeros_like(vd_ref)
            dsv_acc[...] = jnp.zeros_like(dsv_acc)
            qext[tm:tm + HALO, :] = jnp.zeros((HALO, DP), F32)

        gn, sc, sh, gate = mod_ref[3:4, :], mod_ref[1:2, :], mod_ref[0:1, :], mod_ref[2:3, :]
        vecs = vec_ref[...]
        x = x_ref[...]
        r, xn, hp, h = _norm_mod(x, gn, sc, sh)
        hb = h.astype(BF16)
        dxo = dxo_ref[...]

        pooled = pooled_ref[...]
        mixed_cols = [_dot(pooled[:, gi * LANE:(gi + 1) * LANE], wpool_ref[gi]) for gi in range(4)]
        mixed = jnp.concatenate(mixed_cols, axis=1)
        scale = vecs[0:1, :]
        gm = _gmlp_branch(zb_ref[...], vecs, wcat_ref, bias_ref, nchunks)
        cat = jnp.concatenate([mixed * scale, gm["yb"]], axis=1).astype(BF16)

        dwout_ref[...] += _dot_tn(cat, dxo.astype(BF16))
        dcat = _dot_nt((dxo * gate).astype(BF16), wout_ref[...])
        dya = dcat[:, :DP]
        dyb = dcat[:, DP:]

        v512_ref[0:1, :] += jnp.sum(dya * mixed, axis=0, keepdims=True)
        dmixed = (dya * scale).astype(BF16)
        pos = tile * tm + lax.broadcasted_iota(jnp.int32, (tm, 1), 0)
        dpooled_cols = []
        for gi, w in enumerate(POOL_WINDOWS):
            cols = slice(gi * LANE, (gi + 1) * LANE)
            dp = _dot_nt(dmixed[:, cols], wpool_ref[gi])
            dwpool_ref[gi] += _dot_tn(pooled[:, cols], dmixed[:, cols])
            cnt = jnp.minimum(pos + 1, w).astype(F32)
            qext[0:tm, cols] = dp / cnt
            dpooled_cols.append(dp)
        dxa_cols = []
        for gi, w in enumerate(POOL_WINDOWS):
            cols = slice(gi * LANE, (gi + 1) * LANE)
            s = qext[0:tm, cols]
            for k in range(1, w):
                s = s + qext[k:k + tm, cols]
            dxa_cols.append(s - dpooled_cols[gi])
        qext[tm:tm + HALO, :] = qext[0:HALO, :]

        u, sv, vl = gm["u"], gm["sv"], gm["vl"]
        du = dyb * sv
        dsv = dyb * u
        dvl_cols = []
        for p in range(4):
            cols = slice(p * LANE, (p + 1) * LANE)
            dblocks = [dsv[k * CHUNK:(k + 1) * CHUNK, cols] for k in range(nchunks)]
            vblocks = [vl[k * CHUNK:(k + 1) * CHUNK, cols] for k in range(nchunks)]
            tot = dblocks[0]
            for b in dblocks[1:]:
                tot = tot + b
            dsv_acc[:, cols] += tot
            top, bot = _pair_rhs(dblocks)
            out = _dot(wtcat_ref[p], jnp.concatenate([top, bot], axis=0).astype(BF16))
            dvl_cols.append(jnp.concatenate([out[:, k * LANE:(k + 1) * LANE] for k in range(nchunks)], axis=0))
            vcat = jnp.concatenate(vblocks, axis=1).astype(BF16)
            dwsp_ref[2 * p] += _dot_nt(top.astype(BF16), vcat)
            dwsp_ref[2 * p + 1] += _dot_nt(bot.astype(BF16), vcat)
        dvl = jnp.concatenate(dvl_cols, axis=1)
        vhat, rstd = gm["vhat"], gm["rstd"]
        v512_ref[1:2, :] += jnp.sum(dvl * vhat, axis=0, keepdims=True)
        v512_ref[2:3, :] += jnp.sum(dvl, axis=0, keepdims=True)
        dvh = dvl * vecs[1:2, :]
        dv = rstd * (dvh - jnp.mean(dvh, axis=-1, keepdims=True)
                     - vhat * jnp.mean(dvh * vhat, axis=-1, keepdims=True))
        dzb = jnp.concatenate([du, dv], axis=1) * gm["dz"]

        dproj = jnp.concatenate(dxa_cols + [dzb], axis=1).astype(BF16)
        dwin_ref[...] += _dot_tn(hb, dproj)
        dh = _dot_nt(dproj, win_ref[...])
        dsh, dsc, dgn, dxin = _norm_mod_bwd(dh, r, xn, hp, gn, sc)
        vd_ref[0:1, :] += dsh
        vd_ref[1:2, :] += dsc
        vd_ref[3:4, :] += dgn
        dx_ref[...] = dxo + dxin

        @pl.when(step == nt - 1)
        def _():
            gw = dwout_ref[...]
            vd_ref[2:3, :] += jnp.sum(wout_ref[...].astype(F32) * gw, axis=0, keepdims=True)
            dwout_ref[...] = gw * gate
            row = lax.broadcasted_iota(jnp.int32, (CHUNK, CHUNK), 0)
            col = lax.broadcasted_iota(jnp.int32, (CHUNK, CHUNK), 1)
            for hh in range(8):
                dwsp_ref[hh] = jnp.where(col <= row, dwsp_ref[hh], 0.0)
            head = lax.broadcasted_iota(jnp.int32, (8, DG), 0)
            ch = lax.broadcasted_iota(jnp.int32, (8, DG), 1)
            spread = jnp.where(ch // 64 == head, 1.0, 0.0).astype(F32)
            dbsp_ref[...] = lax.dot_general(spread, dsv_acc[...], (((1,), (1,)), ((), ())),
                                            precision=HIGHEST, preferred_element_type=F32)

    full = lambda shape: pl.BlockSpec(shape, lambda s: (0,) * len(shape))
    rev = lambda cols: pl.BlockSpec((tm, cols), lambda s: (nt - 1 - s, 0))
    step = lambda s: lambda: pl.program_id(0) == s
    return _side_call(
        body, side, (step(0), None, step(nt - 1)), name="mix_bwd",
        grid=(nt,),
        in_specs=[rev(D), rev(D), rev(DP), rev(2 * DG), full((8, D)), full((D, DPROJ)), full((4, LANE, LANE)),
                  full((8, DP)), full((4, CHUNK, 2 * CHUNK)), full((4, CHUNK, 2 * CHUNK)), full((CHUNK, DG)),
                  full((DP + DG, D))],
        out_specs=[rev(D), full((D, DPROJ)), full((DP + DG, D)), full((4, LANE, LANE)), full((8, CHUNK, CHUNK)),
                   full((8, CHUNK)), full((8, DP)), full((8, D))],
        out_shape=[_sds((S, D), F32), _sds((D, DPROJ), F32), _sds((DP + DG, D), F32), _sds((4, LANE, LANE), F32),
                   _sds((8, CHUNK, CHUNK), F32), _sds((8, CHUNK), F32), _sds((8, DP), F32), _sds((8, D), F32)],
        scratch_shapes=[pltpu.VMEM((tm + HALO, DP), F32), pltpu.VMEM((CHUNK, DG), F32)],
        compiler_params=_params(("arbitrary",), VMEM_LIMIT),
        args=(x, dxo, pooled, zb, modv, win, wpool, vecs, wcat, wtcat, bias, wout))


def _chip_sum(g, rbuf, core):
    _, _, hr, cols = g.shape
    tr = _row_block(hr)

    def body(c_ref, g_ref, r_ref, o_ref):
        o_ref[...] = (g_ref[...] + r_ref[...]).astype(BF16)

    return pl.pallas_call(
        body, name="chip_sum", interpret=False,
        grid_spec=pltpu.PrefetchScalarGridSpec(
            num_scalar_prefetch=1, grid=(NQ, hr // tr),
            in_specs=[pl.BlockSpec((None, None, tr, cols), lambda q, i, c: (q, c[0], i, 0)),
                      pl.BlockSpec((None, tr, cols), lambda q, i, c: (q, i, 0))],
            out_specs=pl.BlockSpec((None, tr, cols), lambda q, i, c: (q, i, 0))),
        out_shape=_sds((NQ, hr, cols), BF16),
        compiler_params=_params(("arbitrary", "arbitrary"), None),
    )(core, g, rbuf)


def _chip_sum_pair(own, rbuf):
    _, hr, cols = own.shape
    tr = _row_block(hr)

    def body(a_ref, b_ref, o_ref):
        o_ref[...] = (a_ref[...] + b_ref[...]).astype(BF16)

    spec = pl.BlockSpec((None, tr, cols), lambda q, i: (q, i, 0))
    return _call(
        body, name="chip_sum_pair",
        grid=(NQ, hr // tr),
        in_specs=[spec, spec], out_specs=spec,
        out_shape=_sds((NQ, hr, cols), BF16),
        compiler_params=_params(("arbitrary", "arbitrary"), None),
    )(own, rbuf)


def _sum4(cs, rbuf, chip, after=()):
    _, hr, cols = rbuf.shape
    tr = _row_block(hr)

    def body(q_ref, c_ref, r1_ref, r2_ref, r3_ref, *rest):
        acc = c_ref[...].astype(F32)
        for r in (r1_ref, r2_ref, r3_ref):
            acc = acc + r[...].astype(F32)
        rest[-1][...] = acc

    slot = lambda k: pl.BlockSpec((None, tr, cols), lambda i, q: ((q[0] + k) % NQ, i, 0))
    return pl.pallas_call(
        body, name="sum4", interpret=False,
        grid_spec=pltpu.PrefetchScalarGridSpec(
            num_scalar_prefetch=1, grid=(hr // tr,),
            in_specs=[slot(0), slot(1), slot(2), slot(3)] + [ANY] * len(after),
            out_specs=pl.BlockSpec((tr, cols), lambda i, q: (i, 0))),
        out_shape=_sds((hr, cols), F32),
        compiler_params=_params(("arbitrary",), None),
    )(chip, cs, rbuf, rbuf, rbuf, *after)


def _adamw_halves(w, own, recv, m, v, side=None, after=()):
    rows, cols = w.shape
    hr = rows // 2
    tr = _row_block(hr, mult=8)
    nb = hr // tr

    def body(w_ref, own_ref, recv_ref, m_ref, v_ref, *rest):
        g_ref, d_ref, mo_ref, vo_ref = rest[len(after):]
        g = jnp.where(pl.program_id(0) == lax.axis_index("c"), own_ref[...], recv_ref[...])
        d, mn, vn = _adamw(w_ref[...], g, m_ref[...], v_ref[...])
        g_ref[...] = g
        d_ref[...] = d
        mo_ref[...] = mn
        vo_ref[...] = vn

    full = pl.BlockSpec((tr, cols), lambda h, i: (h * nb + i, 0))
    half = pl.BlockSpec((tr, cols), lambda h, i: (i, 0))
    step = lambda h, i: lambda: (pl.program_id(0) == h) & (pl.program_id(1) == i)
    return _side_call(
        body, side, (step(0, 0), None, step(1, nb - 1)), name="adamw_halves",
        grid=(2, nb), in_specs=[full, half, half, full, full] + [ANY] * len(after), out_specs=[full] * 4,
        out_shape=[_sds((rows, cols), F32)] * 4, scratch_shapes=[],
        compiler_params=_params(("arbitrary", "arbitrary"), None),
        args=(w, own, recv, m, v, *after))


def _cast_place(w, chip, wide=False):
    rows, cols = w.shape
    tr = _row_block(rows)

    def body(q_ref, w_ref, o_ref):
        o_ref[...] = w_ref[...].astype(BF16)

    if wide:
        out_spec = pl.BlockSpec((None, tr, cols), lambda i, q: (q[0] % 2, i, q[0] // 2))
    else:
        out_spec = pl.BlockSpec((None, tr, cols), lambda i, q: (q[0], i, 0))
    return pl.pallas_call(
        body, name="cast_place", interpret=False,
        grid_spec=pltpu.PrefetchScalarGridSpec(
            num_scalar_prefetch=1, grid=(rows // tr,),
            in_specs=[pl.BlockSpec((tr, cols), lambda i, q: (i, 0))],
            out_specs=out_spec),
        out_shape=_sds((2, rows, 2 * cols) if wide else (NQ, rows, cols), BF16),
        compiler_params=_params(("arbitrary",), None),
    )(chip, w)


def _ada_grad_adamw(cact_t, dmod_q, w, m, v, side=None):
    rows, cols = w.shape
    tc = 256
    assert cols % tc == 0

    def body(c_ref, d_ref, w_ref, m_ref, v_ref, g_ref, dl_ref, mo_ref, vo_ref):
        g = jnp.dot(c_ref[...], d_ref[...], precision=HIGHEST, preferred_element_type=F32)
        d, mn, vn = _adamw(w_ref[...], g, m_ref[...], v_ref[...])
        g_ref[...] = g
        dl_ref[...] = d
        mo_ref[...] = mn
        vo_ref[...] = vn

    spec = pl.BlockSpec((rows, tc), lambda i: (0, i))
    step = lambda s: lambda: pl.program_id(0) == s
    return _side_call(
        body, side, (step(0), None, step(cols // tc - 1)), name="ada_grad_adamw",
        grid=(cols // tc,),
        in_specs=[pl.BlockSpec((rows, 8), lambda i: (0, 0)), pl.BlockSpec((8, tc), lambda i: (0, i)),
                  spec, spec, spec],
        out_specs=[spec] * 4,
        out_shape=[_sds((rows, cols), F32)] * 4,
        scratch_shapes=[],
        compiler_params=_params(("arbitrary",), None),
        args=(cact_t, dmod_q, w, m, v))


def _me():
    x, y, c = lax.axis_index("x"), lax.axis_index("y"), lax.axis_index("c")
    return x, y, c


_OFFSETS7 = [(dx, dy, dc) for dx in (0, 1) for dy in (0, 1) for dc in (0, 1) if (dx, dy, dc) != (0, 0, 0)]
_CHIP_OFFSETS = [(1, 0), (0, 1), (1, 1)]


def _ada_fwd(c, w_ada_q, b_ada_q, side=None):
    ncol = w_ada_q.shape[1]

    def body(c_ref, w_ref, b_ref, cact_ref, modsel_ref, blk, gath, res, parts, send_sems, recv_sems, side_start=None):
        x, y, cc = _me()
        me = 4 * x + 2 * y + cc
        q = 2 * x + y
        cv = c_ref[...]
        ca = cv * jax.nn.sigmoid(cv)
        row = lax.broadcasted_iota(jnp.int32, (8, D), 0)
        blk[...] = jnp.where(row == me, jnp.broadcast_to(ca, (8, D)), 0.0)
        gath[me] = blk[...]
        sends = []
        for k, (dx, dy, dc) in enumerate(_OFFSETS7):
            cp = pltpu.make_async_remote_copy(blk, gath.at[me], send_sems.at[k], recv_sems.at[k],
                                              device_id=(x ^ dx, y ^ dy, cc ^ dc), device_id_type=MESH)
            cp.start()
            sends.append(cp)
        if side_start is not None:
            side_start()
        for cp in sends:
            cp.wait_recv()
        cact = gath[0]
        for d in range(1, N_DEV):
            cact = cact + gath[d]
        cact_ref[...] = cact
        res[...] = jnp.dot(cact, w_ref[...], precision=HIGHEST, preferred_element_type=F32) + b_ref[...]
        parts[q] = res[...]
        sends2 = []
        for k, (dx, dy) in enumerate(_CHIP_OFFSETS):
            cp = pltpu.make_async_remote_copy(res, parts.at[q], send_sems.at[7 + k], recv_sems.at[7 + k],
                                              device_id=(x ^ dx, y ^ dy, cc), device_id_type=MESH)
            cp.start()
            sends2.append(cp)
        for cp in sends2:
            cp.wait_recv()
        row2 = lax.broadcasted_iota(jnp.int32, (8, ncol), 0)
        out = jnp.zeros((8, ncol), F32)
        for s in range(NQ):
            mine = jnp.sum(jnp.where(row2 == me, parts[s], 0.0), axis=0, keepdims=True)
            out = out + jnp.where(row2 == s, jnp.broadcast_to(mine, (8, ncol)), 0.0)
        modsel_ref[...] = out
        for cp in sends + sends2:
            cp.wait_send()

    return _side_call(
        body, side, None, name="ada_fwd",
        in_specs=[VMEM, VMEM, VMEM], out_specs=[VMEM, VMEM],
        out_shape=[_sds((8, D), F32), _sds((8, ncol), F32)],
        scratch_shapes=[pltpu.VMEM((8, D), F32), pltpu.VMEM((N_DEV, 8, D), F32), pltpu.VMEM((8, ncol), F32),
                        pltpu.VMEM((NQ, 8, ncol), F32), pltpu.SemaphoreType.DMA((10,)), pltpu.SemaphoreType.DMA((10,))],
        compiler_params=_params(None, VMEM_LIMIT), start_in_body=side is not None,
        args=(c, w_ada_q, b_ada_q))


class _Side:
    def __init__(self, ins, out_shapes, aliases, nsem, start, mid=None, finish=None):
        self.ins, self.out_shapes, self.aliases, self.nsem = list(ins), list(out_shapes), dict(aliases), nsem
        self.start, self.mid, self.finish = start, mid, finish


def _join(*sides):
    ins, outs, aliases, offs, nsem = [], [], {}, [], 0
    for s in sides:
        offs.append((len(ins), len(outs), nsem))
        aliases.update({len(ins) + a: len(outs) + b for a, b in s.aliases.items()})
        ins += s.ins
        outs += s.out_shapes
        nsem += s.nsem

    def hook(name):
        def run(i, o, ss, rs, base):
            for s, (io, oo, so) in zip(sides, offs):
                fn = getattr(s, name)
                if fn is not None:
                    fn(i[io:io + len(s.ins)], o[oo:oo + len(s.out_shapes)], ss, rs, base + so)
        return run

    return _Side(ins, outs, aliases, nsem, hook("start"), hook("mid"), hook("finish"))


def _side_call(body, side, when, *, name, in_specs, out_specs, out_shape, scratch_shapes, args, aliases=None,
               start_in_body=False, **kw):
    n_in, n_out = len(in_specs), len(out_specs)
    aliases = dict(aliases or {})
    if side is None:
        return _call(body, name=name, in_specs=in_specs, out_specs=out_specs, out_shape=out_shape,
                     scratch_shapes=scratch_shapes, input_output_aliases=aliases, **kw)(*args), []
    ns_in, ns_out = len(side.ins), len(side.out_shapes)

    def hook(fn, k, operands):
        if fn is None:
            return
        if when is None:
            fn(*operands, 0)
        elif when[k] is not None:
            pl.when(when[k]())(functools.partial(fn, *operands, 0))

    def wrapped(*refs):
        ins, s_ins = refs[:n_in], refs[n_in:n_in + ns_in]
        o0 = n_in + ns_in
        outs, s_outs = refs[o0:o0 + n_out], refs[o0 + n_out:o0 + n_out + ns_out]
        rest = refs[o0 + n_out + ns_out:]
        scratch, operands = rest[:-2], (s_ins, s_outs, rest[-2], rest[-1])
        if start_in_body:
            body(*ins, *outs, *scratch, side_start=functools.partial(hook, side.start, 0, operands))
        else:
            hook(side.start, 0, operands)
            body(*ins, *outs, *scratch)
        hook(side.mid, 1, operands)
        hook(side.finish, 2, operands)

    res = _call(
        wrapped, name=name,
        in_specs=list(in_specs) + [ANY] * ns_in, out_specs=list(out_specs) + [ANY] * ns_out,
        out_shape=list(out_shape) + side.out_shapes,
        scratch_shapes=list(scratch_shapes) + [pltpu.SemaphoreType.DMA((side.nsem,)),
                                               pltpu.SemaphoreType.DMA((side.nsem,))],
        input_output_aliases={**aliases, **{n_in + a: n_out + b for a, b in side.aliases.items()}},
        **kw)(*args, *side.ins)
    return res[:n_out], res[n_out:]


def _run_side(side, name):
    return _side_call(lambda: None, side, None, name=name, in_specs=[], out_specs=[], out_shape=[],
                      scratch_shapes=[], args=[])[1]


def _remote(src, dst, ss, rs, k, dev):
    return pltpu.make_async_remote_copy(src, dst, ss.at[k], rs.at[k], device_id=dev, device_id_type=MESH)


def _gather_side(bufs):
    n = len(bufs)

    def plan(outs, w):
        x, y, cc = _me()
        hr = outs[w].shape[1] // 2
        mine, other = cc * hr, (1 - cc) * hr
        qx, qy, qd, q = 2 * (x ^ 1) + y, 2 * x + (y ^ 1), 2 * (x ^ 1) + (y ^ 1), 2 * x + y
        xn, yn, sib = (x ^ 1, y, cc), (x, y ^ 1, cc), (x, y, 1 - cc)
        if outs[w].shape[0] == NQ:
            at = lambda slot, r0, nr: outs[w].at[slot, pl.ds(r0, nr)]
        else:
            cols = outs[w].shape[2] // 2
            at = lambda slot, r0, nr: outs[w].at[slot % 2, pl.ds(r0, nr), pl.ds((slot // 2) * cols, cols)]
        send = [(at(q, mine, hr), xn), (at(q, mine, hr), yn),
                (at(qx, mine, hr // 2), yn), (at(qy, mine + hr // 2, hr // 2), xn),
                (at(qx, mine, hr), sib), (at(qy, mine, hr), sib), (at(qd, mine, hr), sib)]
        recv = [at(qx, mine, hr), at(qy, mine, hr), at(qd, mine, hr // 2), at(qd, mine + hr // 2, hr // 2),
                at(qx, other, hr), at(qy, other, hr), at(qd, other, hr)]
        return send, recv

    def op(outs, ss, rs, b, w, k, what):
        send, recv = plan(outs, w)
        if what == "wait_recv":
            _remote(recv[k], recv[k], ss, rs, b + 7 * w + k, send[k][1]).wait_recv()
        else:
            getattr(_remote(send[k][0], send[k][0], ss, rs, b + 7 * w + k, send[k][1]), what)()

    def start(ins, outs, ss, rs, b):
        for w in range(n):
            for k in (0, 1):
                op(outs, ss, rs, b, w, k, "start")

    def mid(ins, outs, ss, rs, b):
        for w in range(n):
            for k in (0, 1):
                op(outs, ss, rs, b, w, k, "wait_recv")
                op(outs, ss, rs, b, w, 2 + k, "start")
                op(outs, ss, rs, b, w, 4 + k, "start")

    def finish(ins, outs, ss, rs, b):
        for w in range(n):
            for k in (2, 3):
                op(outs, ss, rs, b, w, k, "wait_recv")
            op(outs, ss, rs, b, w, 6, "start")
        for w in range(n):
            for k in (4, 5, 6):
                op(outs, ss, rs, b, w, k, "wait_recv")
            for k in range(7):
                op(outs, ss, rs, b, w, k, "wait_send")

    return _Side(bufs, [_sds(tuple(w.shape), w.dtype) for w in bufs], {i: i for i in range(n)}, 7 * n,
                 start, mid, finish)


def _copies_side(ins, out_shapes, nsem, copies):
    def start(*a):
        for cp in copies(*a):
            cp.start()

    def finish(*a):
        for cp in copies(*a):
            cp.wait()

    return _Side(ins, out_shapes, {}, nsem, start, None, finish)


def _swap_side(gs):
    def copies(ins, outs, ss, rs, b):
        x, y, cc = _me()
        return [_remote(ins[w].at[:, 1 - cc], outs[w], ss, rs, b + w, (x, y, 1 - cc)) for w in range(len(gs))]

    return _copies_side(gs, [_sds((NQ,) + tuple(g.shape[2:]), F32) for g in gs], len(gs), copies)


def _exchange_side(cs):
    def copies(ins, outs, ss, rs, b):
        x, y, cc = _me()
        return [_remote(ins[w].at[2 * (x ^ dx) + (y ^ dy)], outs[w].at[2 * x + y], ss, rs, b + 3 * w + j,
                        (x ^ dx, y ^ dy, cc))
                for w in range(len(cs)) for j, (dx, dy) in enumerate(_CHIP_OFFSETS)]

    return _copies_side(cs, [_sds(tuple(c.shape), c.dtype) for c in cs], 3 * len(cs), copies)


def _exchange_copies(srcs, lands, send_sems, recv_sems):
    x, y, cc = _me()
    return [pltpu.make_async_remote_copy(srcs[w].at[2 * (x ^ dx) + (y ^ dy)], lands[w].at[2 * x + y],
                                         send_sems.at[3 * w + j], recv_sems.at[3 * w + j],
                                         device_id=(x ^ dx, y ^ dy, cc), device_id_type=MESH)
            for w in range(len(srcs)) for j, (dx, dy) in enumerate(_CHIP_OFFSETS)]


def _exchange_start(cs, after=()):
    n = len(cs)
    hbm, sem = pl.BlockSpec(memory_space=pltpu.HBM), pl.BlockSpec(memory_space=pltpu.SEMAPHORE)
    srcs = [pltpu.with_memory_space_constraint(c, pltpu.HBM) for c in cs]
    lands = [pltpu.with_memory_space_constraint(lax.empty(c.shape, c.dtype), pltpu.HBM) for c in cs]

    def body(*refs):
        sems = 2 * n + len(after)
        for cp in _exchange_copies(refs[:n], refs[n:2 * n], refs[sems], refs[sems + 1]):
            cp.start()
        refs[-1][...] = jnp.zeros_like(refs[-1])

    res = pl.pallas_call(
        body, name="exchange_start", interpret=False,
        out_shape=(pltpu.SemaphoreType.DMA((3 * n,)), pltpu.SemaphoreType.DMA((3 * n,)),
                   *[pltpu.HBM(c.shape, c.dtype) for c in cs], *[pltpu.HBM(c.shape, c.dtype) for c in cs],
                   _sds((8, LANE), F32)),
        in_specs=(hbm,) * (2 * n) + (ANY,) * len(after), out_specs=(sem, sem) + (hbm,) * (2 * n) + (VMEM,),
        input_output_aliases={i: 2 + i for i in range(2 * n)},
        compiler_params=pltpu.CompilerParams(has_side_effects=pltpu.SideEffectType.DATAFLOW_SIDE_EFFECTING),
    )(*srcs, *lands, *after)
    return res[0], res[1], list(res[2:2 + n]), list(res[2 + n:2 + 2 * n]), res[-1]


def _exchange_wait(send_sems, recv_sems, srcs, lands, after):
    n = len(srcs)
    hbm, sem = pl.BlockSpec(memory_space=pltpu.HBM), pl.BlockSpec(memory_space=pltpu.SEMAPHORE)

    def body(*refs):
        for cp in _exchange_copies(refs[:n], refs[n:2 * n], refs[2 * n], refs[2 * n + 1]):
            cp.wait_send()
            cp.wait_recv()

    res = pl.pallas_call(
        body, name="exchange_wait", interpret=False,
        out_shape=[pltpu.HBM(c.shape, c.dtype) for c in srcs + lands],
        in_specs=(hbm,) * (2 * n) + (sem, sem) + (ANY,) * len(after), out_specs=(hbm,) * (2 * n),
        input_output_aliases={i: i for i in range(2 * n)},
        compiler_params=pltpu.CompilerParams(has_side_effects=pltpu.SideEffectType.DATAFLOW_SIDE_EFFECTING),
    )(*srcs, *lands, send_sems, recv_sems, *after)
    return list(res[:n]), list(res[n:])


def _share_side(fs):
    def copies(ins, outs, ss, rs, b):
        x, y, cc = _me()
        return [_remote(ins[w], outs[w], ss, rs, b + w, (x, y, 1 - cc)) for w in range(len(fs))]

    return _copies_side(fs, [_sds(tuple(f.shape), F32) for f in fs], len(fs), copies)


def _small_allreduce_adamw(g, w, m, v, nd):
    rows = g.shape[0]
    nr = rows - nd
    hr = nr // 2
    assert nd % 8 == 0 and hr % 8 == 0

    def body(g_ref, w_ref, m_ref, v_ref, gs_ref, d_ref, mo_ref, vo_ref, gath, sib, csum, slots, tot, ss, rs):
        x, y, cc = _me()
        me = 4 * x + 2 * y + cc
        q = 2 * x + y
        sibling = (x, y, 1 - cc)
        dm = g_ref.at[pl.ds(0, nd)]
        gath[me] = g_ref[0:nd, :]
        to_all = [_remote(dm, gath.at[me], ss, rs, k, (x ^ dx, y ^ dy, cc ^ dc)) for k, (dx, dy, dc) in enumerate(_OFFSETS7)]
        to_sib = _remote(g_ref.at[pl.ds(nd, nr)], sib, ss, rs, 7, sibling)
        for cp in to_all + [to_sib]:
            cp.start()
        to_sib.wait_recv()
        csum[...] = g_ref[nd:, :] + sib[...]
        mine = pl.ds(pl.multiple_of(cc * hr, 8), hr)
        slots[q] = csum[mine, :]
        to_chips = [_remote(csum.at[mine], slots.at[q], ss, rs, 8 + j, (x ^ dx, y ^ dy, cc))
                    for j, (dx, dy) in enumerate(_CHIP_OFFSETS)]
        for cp in to_chips:
            cp.start()
        for cp in to_chips:
            cp.wait_recv()
        tot[mine, :] = (slots[0] + slots[1]) + (slots[2] + slots[3])
        halves = _remote(tot.at[mine], tot.at[mine], ss, rs, 11, sibling)
        halves.start()
        for cp in to_all:
            cp.wait_recv()
        dsum = gath[0]
        for dev in range(1, N_DEV):
            dsum = dsum + gath[dev]
        halves.wait_recv()
        for lo, n, total in ((0, nd, dsum), (nd, nr, tot[...])):
            gs_ref[lo:lo + n, :] = total
            d, mn, vn = _adamw(w_ref[lo:lo + n, :], total, m_ref[lo:lo + n, :], v_ref[lo:lo + n, :])
            d_ref[lo:lo + n, :] = d
            mo_ref[lo:lo + n, :] = mn
            vo_ref[lo:lo + n, :] = vn
        for cp in to_all + [to_sib, halves] + to_chips:
            cp.wait_send()

    return _call(
        body, name="small_allreduce_adamw",
        in_specs=[VMEM] * 4, out_specs=[VMEM] * 5,
        out_shape=[_sds((rows, LANE), F32)] * 4 + [_sds((N_DEV, nd, LANE), F32)],
        scratch_shapes=[pltpu.VMEM((nr, LANE), F32), pltpu.VMEM((nr, LANE), F32), pltpu.VMEM((NQ, hr, LANE), F32),
                        pltpu.VMEM((nr, LANE), F32), pltpu.SemaphoreType.DMA((12,)), pltpu.SemaphoreType.DMA((12,))],
        compiler_params=_params(None, VMEM_LIMIT),
    )(g, w, m, v)


_SMALL = ["b_ada", "norm_ffn1_g", "norm_mix_g", "pool_scale", "gmlp_ln_g", "gmlp_ln_b", "b_spatial",
          "norm_ffn2_g", "norm_final_g", "w_pool", "w_spatial"]


def _pack(parts):
    blocks, layout, r0 = [], {}, 0
    for name in _SMALL:
        a = parts[name]
        n = a.size
        rows = -(-n // LANE)
        rows8 = -(-rows // 8) * 8
        flat = a.reshape(-1).astype(F32)
        if rows8 * LANE != n:
            flat = jnp.concatenate([flat, jnp.zeros((rows8 * LANE - n,), F32)])
        blocks.append(flat.reshape(rows8, LANE))
        layout[name] = (r0, n, a.shape)
        r0 += rows8
    return jnp.concatenate(blocks, axis=0), layout


def _unpack(packed, layout):
    out = {}
    for name, (r0, n, shape) in layout.items():
        rows = -(-n // LANE)
        out[name] = packed[r0:r0 + rows].reshape(-1)[:n].reshape(shape)
    return out


def _modv(mod9, sub, gain):
    rows = jnp.concatenate([mod9[3 * sub:3 * sub + 3], gain.reshape(1, D), jnp.zeros((4, D), F32)], axis=0)
    return rows


_BIG = ["ffn1_w_in", "ffn1_w_out", "w_mix_in", "w_mix_out", "ffn2_w_in", "ffn2_w_out"]


def kernel(x, c, w_ada, b_ada, norm_ffn1_g, ffn1_w_in, ffn1_w_out, norm_mix_g, w_mix_in, w_pool, pool_scale, gmlp_ln_g, gmlp_ln_b, w_spatial, b_spatial, w_mix_out, norm_ffn2_g, ffn2_w_in, ffn2_w_out, norm_final_g, loss_target, m_w_ada, m_b_ada, m_norm_ffn1_g, m_ffn1_w_in, m_ffn1_w_out, m_norm_mix_g, m_w_mix_in, m_w_pool, m_pool_scale, m_gmlp_ln_g, m_gmlp_ln_b, m_w_spatial, m_b_spatial, m_w_mix_out, m_norm_ffn2_g, m_ffn2_w_in, m_ffn2_w_out, m_norm_final_g, v_w_ada, v_b_ada, v_norm_ffn1_g, v_ffn1_w_in, v_ffn1_w_out, v_norm_mix_g, v_w_mix_in, v_w_pool, v_pool_scale, v_gmlp_ln_g, v_gmlp_ln_b, v_w_spatial, v_b_spatial, v_w_mix_out, v_norm_ffn2_g, v_ffn2_w_in, v_ffn2_w_out, v_norm_final_g):
    names = ["w_ada", "b_ada", "norm_ffn1_g", "ffn1_w_in", "ffn1_w_out", "norm_mix_g", "w_mix_in", "w_pool",
             "pool_scale", "gmlp_ln_g", "gmlp_ln_b", "w_spatial", "b_spatial", "w_mix_out", "norm_ffn2_g",
             "ffn2_w_in", "ffn2_w_out", "norm_final_g"]
    W = dict(zip(names, [w_ada, b_ada, norm_ffn1_g, ffn1_w_in, ffn1_w_out, norm_mix_g, w_mix_in, w_pool, pool_scale,
                         gmlp_ln_g, gmlp_ln_b, w_spatial, b_spatial, w_mix_out, norm_ffn2_g, ffn2_w_in, ffn2_w_out,
                         norm_final_g]))
    M = dict(zip(names, [m_w_ada, m_b_ada, m_norm_ffn1_g, m_ffn1_w_in, m_ffn1_w_out, m_norm_mix_g, m_w_mix_in, m_w_pool,
                         m_pool_scale, m_gmlp_ln_g, m_gmlp_ln_b, m_w_spatial, m_b_spatial, m_w_mix_out, m_norm_ffn2_g,
                         m_ffn2_w_in, m_ffn2_w_out, m_norm_final_g]))
    V = dict(zip(names, [v_w_ada, v_b_ada, v_norm_ffn1_g, v_ffn1_w_in, v_ffn1_w_out, v_norm_mix_g, v_w_mix_in, v_w_pool,
                         v_pool_scale, v_gmlp_ln_g, v_gmlp_ln_b, v_w_spatial, v_b_spatial, v_w_mix_out, v_norm_ffn2_g,
                         v_ffn2_w_in, v_ffn2_w_out, v_norm_final_g]))

    xi, yi, ci = _me()
    q = 2 * xi + yi
    core = ci.astype(jnp.int32).reshape(1)

    chip = q.astype(jnp.int32).reshape(1)
    place = lambda n: _cast_place(W[n][0], chip, wide=n in ("ffn1_w_in", "ffn2_w_in"))

    ncol = w_ada.shape[2]
    b_q = lax.dynamic_slice_in_dim(b_ada, q * ncol, ncol, axis=1)
    (cact_all, modsel), (win1, wout1) = _ada_fwd(
        c, w_ada[0], b_q, side=_gather_side([place("ffn1_w_in"), place("ffn1_w_out")]))
    mod9 = modsel[:NQ].reshape(9, D)
    xs, target = x[0], loss_target[0]
    mv1 = _modv(mod9, 0, norm_ffn1_g[0])
    mv2 = _modv(mod9, 1, norm_mix_g[0])
    mv3 = _modv(mod9, 2, norm_ffn2_g[0])
    wcat, wtcat, bias = _prep_spatial(w_spatial[0], b_spatial[0].T)
    wpool = w_pool[0].astype(BF16)
    vecs = jnp.concatenate([pool_scale, gmlp_ln_g, gmlp_ln_b, jnp.zeros((5, DP), F32)], axis=0)
    gf = jnp.concatenate([norm_final_g.reshape(1, D), jnp.zeros((7, D), F32)], axis=0)

    later =["w_mix_in", "w_mix_out", "ffn2_w_in", "ffn2_w_out"]
    (x1, g1s, u1s), got = _ffn_fwd(xs, mv1, win1, wout1.reshape(2, CH, D), side=_gather_side([place(n) for n in later]))
    wmi, wmo, win2, wout2 = got
    wmi = jnp.transpose(wmi, (1, 0, 2)).reshape(D, DPROJ)
    wmo = wmo.reshape(DP + DG, D)
    x2, pooled, zb = _mix_fwd(x1, mv2, wmi, wpool, vecs, wcat, bias, wmo)
    (dx3, g3s, u3s, loss_blk, dgf), _ = _ffn_fwd(x2, mv3, win2, wout2.reshape(2, CH, D), head=(target, gf))

    wo1, wo2 = wout1.reshape(2, CH, D), wout2.reshape(2, CH, D)
    (dx2, oin2, oout2, rin2, rout2, vec3), _ = _ffn_bwd(x2, dx3, g3s, u3s, mv3, win2, wo2)
    cs2 = [_chip_sum_pair(oin2, rin2), _chip_sum_pair(oout2, rout2)]
    (dx1, dwmi, dwmo, dwpool, dwsp, dbsp, v512, vec2), ex2 = _mix_bwd(
        x1, dx2, pooled, zb, mv2, wmi, wpool, vecs, wcat, wtcat, bias, wmo, side=_exchange_side(cs2))
    half2 = [_sum4(cs, e, chip) for cs, e in zip(cs2, ex2)]
    qcols = w_mix_in.shape[2]
    vmix = [jnp.transpose(dwmi.reshape(D, NQ, qcols), (1, 0, 2)).reshape(NQ, 2, D // 2, qcols),
            dwmo.reshape(NQ, 2, (DP + DG) // 8, D)]
    first1, got = _ffn_bwd_pass(0, xs, dx1, g1s, u1s, mv1, win1, wo1,
                                side=_join(_swap_side(vmix), _share_side(half2)))
    sibmix, other2 = got[:2], got[2:]
    cs_mix = [_chip_sum(g, r, core) for g, r in zip(vmix, sibmix)]
    (grad_x, oin1, oout1, rin1, rout1, vec1), ex_mix = _ffn_bwd_pass(
        1, xs, dx1, g1s, u1s, mv1, win1, wo1, prev=first1[:5], side=_exchange_side(cs_mix))
    vec1 = first1[5] + vec1
    cs_ffn1 = [_chip_sum_pair(oin1, rin1), _chip_sum_pair(oout1, rout1)]

    dmod =jnp.concatenate([vec1[0:3], vec2[0:3], vec3[0:3]], axis=0)
    grads = dict(
        b_ada=dmod.reshape(1, 9 * D), norm_ffn1_g=vec1[3:4], norm_mix_g=vec2[3:4], norm_ffn2_g=vec3[3:4],
        pool_scale=v512[0:1], gmlp_ln_g=v512[1:2], gmlp_ln_b=v512[2:3], b_spatial=dbsp[None],
        norm_final_g=dgf[0], w_pool=dwpool[None], w_spatial=dwsp[None])

    gp, layout = _pack({n: grads[n] for n in _SMALL})
    gp = jnp.concatenate([gp, loss_blk, loss_blk], axis=0)
    pad = jnp.zeros((16, LANE), F32)
    wp, mp, vp = [jnp.concatenate([_pack({n: src[n] for n in _SMALL})[0], pad], axis=0) for src in (W, M, V)]
    r0, nb, _ = layout["b_ada"]
    assert r0 == 0
    out_g, out_d, out_m, out_v = {}, {}, {}, {}
    gs, dl, mo, vo, gath = _small_allreduce_adamw(gp, wp, mp, vp, nb // LANE)
    loss = gs[-16, 0]
    for packed, dst in ((gs, out_g), (dl, out_d), (mo, out_m), (vo, out_v)):
        for n, a in _unpack(packed, layout).items():
            dst[n] = a.reshape(W[n].shape)

    def update(n, own, recv, after=()):
        (g2, d, mn, vn), _ = _adamw_halves(W[n][0], own, recv, M[n][0], V[n][0], after=after)
        out_g[n], out_d[n], out_m[n], out_v[n] = g2[None], d[None], mn[None], vn[None]
        return g2

    ssem, rsem, cs_fly, land_fly, token = _exchange_start(cs_ffn1, after=(gath,))
    dmod_q = lax.dynamic_slice_in_dim(gath.reshape(N_DEV, nb), q * ncol, ncol, axis=1) + token[0:8, 0:1]
    (ga, da, ma, va), _ = _ada_grad_adamw(cact_all.T, dmod_q, w_ada[0], m_w_ada[0], v_w_ada[0])
    out_g["w_ada"], out_d["w_ada"], out_m["w_ada"], out_v["w_ada"] = ga[None], da[None], ma[None], va[None]
    done = [ga, update("ffn2_w_in", half2[0], other2[0], after=(token,)),
            update("ffn2_w_out", half2[1], other2[1], after=(token,))]
    half_mix = [_sum4(cs, e, chip, after=(token,)) for cs, e in zip(cs_mix, ex_mix)]
    other_mix = _run_side(_share_side(half_mix), "share_mix")
    done += [update(n, own, recv) for n, own, recv in zip(["w_mix_in", "w_mix_out"], half_mix, other_mix)]
    cs_ffn1, ex_ffn1 = _exchange_wait(ssem, rsem, cs_fly, land_fly, after=done)
    half1 = [_sum4(cs, e, chip) for cs, e in zip(cs_ffn1, ex_ffn1)]
    other1 = _run_side(_share_side(half1), "share_ffn1")
    for n, own, recv in zip(["ffn1_w_in", "ffn1_w_out"], half1, other1):
        update(n, own, recv)

    return (loss, grad_x[None], *[out_g[n] for n in names], *[out_d[n] for n in names],
            *[out_m[n] for n in names], *[out_v[n] for n in names])
```
